```python
import jax, jax.numpy as jnp
from jax import lax
import numpy as np

D_MODEL = 2048
BATCH = 8
SEQ = 4096
DEPTH = 2

CHUNK = 64
GMLP_BLOCK = 128
MIX_WIDTH = D_MODEL
A_WIDTH = MIX_WIDTH // 2
B_WIDTH = MIX_WIDTH - A_WIDTH
A_HEADS = 8
A_HEAD_DIM = A_WIDTH // A_HEADS
B_GROUPS = 8
CONV_WIDTH = 3
IN_COLS = 2 * A_WIDTH + 3 * B_WIDTH
D_FF = -(-8 * D_MODEL // (3 * 256)) * 256
RMS_EPS = 1e-6
LN_EPS = 1e-5

kernel_name = "hybrid_gmlp_shortconv_swiglu_trunk"


def rmsnorm(x, g):
    xf = x.astype(jnp.float32)
    y = xf * lax.rsqrt(jnp.mean(xf * xf, axis=-1, keepdims=True) + RMS_EPS)
    return (y * g.astype(jnp.float32)).astype(x.dtype)


def layernorm(x, g, b):
    xf = x.astype(jnp.float32)
    mu = jnp.mean(xf, axis=-1, keepdims=True)
    xc = xf - mu
    y = xc * lax.rsqrt(jnp.mean(xc * xc, axis=-1, keepdims=True) + LN_EPS)
    return (y * g.astype(jnp.float32) + b.astype(jnp.float32)).astype(x.dtype)


def chunk_causal_block_mask():
    pos = jnp.arange(GMLP_BLOCK)
    return (pos[None, :] // CHUNK) <= (pos[:, None] // CHUNK)


def spatial_gating(u, v, w_s, b_s, ln_g, ln_b):
    bsz, seq, _ = v.shape
    v = layernorm(v, ln_g, ln_b)
    v = v.reshape(bsz, seq // GMLP_BLOCK, GMLP_BLOCK, A_HEADS, A_HEAD_DIM)
    w = jnp.where(chunk_causal_block_mask()[None], w_s, jnp.zeros((), w_s.dtype))
    mixed = jnp.einsum('hij,bcjhd->bcihd', w, v) + b_s.T[None, None, :, :, None]
    return u * mixed.reshape(bsz, seq, A_WIDTH)


def causal_depthwise_conv(h, w):
    c = h.shape[-1]
    return lax.conv_general_dilated(
        h, w[:, None, :].astype(h.dtype), window_strides=(1,),
        padding=[(CONV_WIDTH - 1, 0)],
        dimension_numbers=('NWC', 'WIO', 'NWC'),
        feature_group_count=c)


def hybrid_layer(x, norm1_g, w_in, ln_g, ln_b, w_s, b_s, conv_w, group_norm_g,
                 w_out, norm2_g, w_gate, w_up, w_down):
    h = rmsnorm(x, norm1_g)
    z = jnp.einsum('bsd,dn->bsn', h, w_in)
    z_a = jax.nn.gelu(z[..., :2 * A_WIDTH])
    u, v = z_a[..., :A_WIDTH], z_a[..., A_WIDTH:]
    off = 2 * A_WIDTH
    gate_b = z[..., off:off + B_WIDTH]
    gate_c = z[..., off + B_WIDTH:off + 2 * B_WIDTH]
    h_b = z[..., off + 2 * B_WIDTH:]
    y_a = spatial_gating(u, v, w_s, b_s, ln_g, ln_b)
    y_b = gate_b * causal_depthwise_conv(gate_c * h_b, conv_w)
    y = jnp.concatenate([rmsnorm(y_a, group_norm_g[:A_WIDTH]),
                         rmsnorm(y_b, group_norm_g[A_WIDTH:])], axis=-1)
    x = x + jnp.einsum('bsm,md->bsd', y, w_out)
    h2 = rmsnorm(x, norm2_g)
    act = jax.nn.silu(jnp.einsum('bsd,df->bsf', h2, w_gate)) * jnp.einsum('bsd,df->bsf', h2, w_up)
    return x + jnp.einsum('bsf,fd->bsd', act, w_down)


def _fwd_setup_inputs(seed: int = 0) -> dict:
    key = jax.random.key(seed)
    ks = jax.random.split(key, 16)
    f32 = jnp.float32
    nrm = lambda k, shape, scale: jax.random.normal(k, shape, f32) * scale
    return {
        "x": jax.random.normal(ks[0], (BATCH, SEQ, D_MODEL), f32),
        "norm1_g": 1.0 + nrm(ks[1], (DEPTH, D_MODEL), 0.02),
        "w_in": nrm(ks[2], (DEPTH, D_MODEL, IN_COLS), D_MODEL ** -0.5),
        "gmlp_ln_g": 1.0 + nrm(ks[3], (DEPTH, A_WIDTH), 0.02),
        "gmlp_ln_b": nrm(ks[4], (DEPTH, A_WIDTH), 0.02),
        "w_spatial": nrm(ks[5], (DEPTH, A_HEADS, GMLP_BLOCK, GMLP_BLOCK), GMLP_BLOCK ** -0.5),
        "b_spatial": 1.0 + nrm(ks[6], (DEPTH, A_HEADS, GMLP_BLOCK), 0.1),
        "conv_w": nrm(ks[7], (DEPTH, CONV_WIDTH, B_WIDTH), CONV_WIDTH ** -0.5),
        "group_norm_g": 1.0 + nrm(ks[8], (DEPTH, MIX_WIDTH), 0.02),
        "w_out": nrm(ks[9], (DEPTH, MIX_WIDTH, D_MODEL), MIX_WIDTH ** -0.5),
        "norm2_g": 1.0 + nrm(ks[10], (DEPTH, D_MODEL), 0.02),
        "w_gate": nrm(ks[11], (DEPTH, D_MODEL, D_FF), D_MODEL ** -0.5),
        "w_up": nrm(ks[12], (DEPTH, D_MODEL, D_FF), D_MODEL ** -0.5),
        "w_down": nrm(ks[13], (DEPTH, D_FF, D_MODEL), D_FF ** -0.5),
        "final_norm_g": 1.0 + nrm(ks[14], (D_MODEL,), 0.02),
    }


def _fwd_reference(x, norm1_g, w_in, gmlp_ln_g, gmlp_ln_b, w_spatial, b_spatial, conv_w,
              group_norm_g, w_out, norm2_g, w_gate, w_up, w_down, final_norm_g):
    for layer in range(DEPTH):
        x = hybrid_layer(x, norm1_g[layer], w_in[layer], gmlp_ln_g[layer], gmlp_ln_b[layer],
                         w_spatial[layer], b_spatial[layer], conv_w[layer], group_norm_g[layer],
                         w_out[layer], norm2_g[layer], w_gate[layer], w_up[layer], w_down[layer])
    return rmsnorm(x, final_norm_g)


import jax as _jax
import jax.numpy as _jnp

TWIN_FORMAT = 'train_step'
FWD_PARAMS = ['x', 'norm1_g', 'w_in', 'gmlp_ln_g', 'gmlp_ln_b', 'w_spatial', 'b_spatial', 'conv_w', 'group_norm_g', 'w_out', 'norm2_g', 'w_gate', 'w_up', 'w_down', 'final_norm_g']
TWIN_WEIGHTS = ['norm1_g', 'w_in', 'gmlp_ln_g', 'gmlp_ln_b', 'w_spatial', 'b_spatial', 'conv_w', 'group_norm_g', 'w_out', 'norm2_g', 'w_gate', 'w_up', 'w_down', 'final_norm_g']
TWIN_DIFF_INPUT = 'x'
TWIN_INPUTS = ['x', 'norm1_g', 'w_in', 'gmlp_ln_g', 'gmlp_ln_b', 'w_spatial', 'b_spatial', 'conv_w', 'group_norm_g', 'w_out', 'norm2_g', 'w_gate', 'w_up', 'w_down', 'final_norm_g', 'loss_target', 'm_norm1_g', 'm_w_in', 'm_gmlp_ln_g', 'm_gmlp_ln_b', 'm_w_spatial', 'm_b_spatial', 'm_conv_w', 'm_group_norm_g', 'm_w_out', 'm_norm2_g', 'm_w_gate', 'm_w_up', 'm_w_down', 'm_final_norm_g', 'v_norm1_g', 'v_w_in', 'v_gmlp_ln_g', 'v_gmlp_ln_b', 'v_w_spatial', 'v_b_spatial', 'v_conv_w', 'v_group_norm_g', 'v_w_out', 'v_norm2_g', 'v_w_gate', 'v_w_up', 'v_w_down', 'v_final_norm_g']
TWIN_OUTPUTS = ['loss', 'grad_x', 'grad_norm1_g', 'grad_w_in', 'grad_gmlp_ln_g', 'grad_gmlp_ln_b', 'grad_w_spatial', 'grad_b_spatial', 'grad_conv_w', 'grad_group_norm_g', 'grad_w_out', 'grad_norm2_g', 'grad_w_gate', 'grad_w_up', 'grad_w_down', 'grad_final_norm_g', 'delta_norm1_g', 'delta_w_in', 'delta_gmlp_ln_g', 'delta_gmlp_ln_b', 'delta_w_spatial', 'delta_b_spatial', 'delta_conv_w', 'delta_group_norm_g', 'delta_w_out', 'delta_norm2_g', 'delta_w_gate', 'delta_w_up', 'delta_w_down', 'delta_final_norm_g', 'new_m_norm1_g', 'new_m_w_in', 'new_m_gmlp_ln_g', 'new_m_gmlp_ln_b', 'new_m_w_spatial', 'new_m_b_spatial', 'new_m_conv_w', 'new_m_group_norm_g', 'new_m_w_out', 'new_m_norm2_g', 'new_m_w_gate', 'new_m_w_up', 'new_m_w_down', 'new_m_final_norm_g', 'new_v_norm1_g', 'new_v_w_in', 'new_v_gmlp_ln_g', 'new_v_gmlp_ln_b', 'new_v_w_spatial', 'new_v_b_spatial', 'new_v_conv_w', 'new_v_group_norm_g', 'new_v_w_out', 'new_v_norm2_g', 'new_v_w_gate', 'new_v_w_up', 'new_v_w_down', 'new_v_final_norm_g']
TWIN_LEAF_KINDS = {'loss': 'loss', 'grad_x': 'grad_x', 'grad_norm1_g': 'grad_w', 'grad_w_in': 'grad_w', 'grad_gmlp_ln_g': 'grad_w', 'grad_gmlp_ln_b': 'grad_w', 'grad_w_spatial': 'grad_w', 'grad_b_spatial': 'grad_w', 'grad_conv_w': 'grad_w', 'grad_group_norm_g': 'grad_w', 'grad_w_out': 'grad_w', 'grad_norm2_g': 'grad_w', 'grad_w_gate': 'grad_w', 'grad_w_up': 'grad_w', 'grad_w_down': 'grad_w', 'grad_final_norm_g': 'grad_w', 'delta_norm1_g': 'delta_w', 'delta_w_in': 'delta_w', 'delta_gmlp_ln_g': 'delta_w', 'delta_gmlp_ln_b': 'delta_w', 'delta_w_spatial': 'delta_w', 'delta_b_spatial': 'delta_w', 'delta_conv_w': 'delta_w', 'delta_group_norm_g': 'delta_w', 'delta_w_out': 'delta_w', 'delta_norm2_g': 'delta_w', 'delta_w_gate': 'delta_w', 'delta_w_up': 'delta_w', 'delta_w_down': 'delta_w', 'delta_final_norm_g': 'delta_w', 'new_m_norm1_g': 'new_m', 'new_m_w_in': 'new_m', 'new_m_gmlp_ln_g': 'new_m', 'new_m_gmlp_ln_b': 'new_m', 'new_m_w_spatial': 'new_m', 'new_m_b_spatial': 'new_m', 'new_m_conv_w': 'new_m', 'new_m_group_norm_g': 'new_m', 'new_m_w_out': 'new_m', 'new_m_norm2_g': 'new_m', 'new_m_w_gate': 'new_m', 'new_m_w_up': 'new_m', 'new_m_w_down': 'new_m', 'new_m_final_norm_g': 'new_m', 'new_v_norm1_g': 'new_v', 'new_v_w_in': 'new_v', 'new_v_gmlp_ln_g': 'new_v', 'new_v_gmlp_ln_b': 'new_v', 'new_v_w_spatial': 'new_v', 'new_v_b_spatial': 'new_v', 'new_v_conv_w': 'new_v', 'new_v_group_norm_g': 'new_v', 'new_v_w_out': 'new_v', 'new_v_norm2_g': 'new_v', 'new_v_w_gate': 'new_v', 'new_v_w_up': 'new_v', 'new_v_w_down': 'new_v', 'new_v_final_norm_g': 'new_v'}


def _forward(args):
    return _fwd_reference(*[args[k] for k in FWD_PARAMS])


def _output_shape():
    out = _jax.eval_shape(lambda: _forward(_fwd_setup_inputs(0)))
    return out.shape, out.dtype

N_MICROBATCH = 1
ADAM_LR = 0.001
ADAM_B1 = 0.9
ADAM_B2 = 0.999
ADAM_EPS = 1e-08
ADAM_WD = 0.01
ADAM_STEP = 10
PER_EXAMPLE_BATCH_AXIS = {'x': 0, 'loss_target': 0}
SHARED_INPUTS = []
_WEIGHT_DTYPES = {'norm1_g': _jnp.float32, 'w_in': _jnp.float32, 'gmlp_ln_g': _jnp.float32, 'gmlp_ln_b': _jnp.float32, 'w_spatial': _jnp.float32, 'b_spatial': _jnp.float32, 'conv_w': _jnp.float32, 'group_norm_g': _jnp.float32, 'w_out': _jnp.float32, 'norm2_g': _jnp.float32, 'w_gate': _jnp.float32, 'w_up': _jnp.float32, 'w_down': _jnp.float32, 'final_norm_g': _jnp.float32}
MOMENT_SCALE = {'norm1_g': 1.051981e-01, 'w_in': 6.663672e-02, 'gmlp_ln_g': 4.538012e-02, 'gmlp_ln_b': 4.280388e-02, 'w_spatial': 4.457261e-02, 'b_spatial': 5.095983e-02, 'conv_w': 7.035872e-02, 'group_norm_g': 6.898862e-02, 'w_out': 6.980697e-02, 'norm2_g': 5.238641e-02, 'w_gate': 2.235576e-02, 'w_up': 2.166671e-02, 'w_down': 3.597575e-02, 'final_norm_g': 1.599595e+01}


def _to_microbatches(a, axis):
    t = _jnp.moveaxis(a, axis, 0)
    t = t.reshape((N_MICROBATCH, t.shape[0] // N_MICROBATCH) + t.shape[1:])
    return _jnp.moveaxis(t, 1, axis + 1)


def setup_inputs(seed: int = 0) -> dict:
    inp = _fwd_setup_inputs(seed)
    key = _jax.random.fold_in(_jax.random.key(seed), 7919)
    shape, _ = _output_shape()
    out = dict(inp)
    out["loss_target"] = _jax.random.normal(_jax.random.fold_in(key, 0), shape, _jnp.float32)
    for i, name in enumerate(TWIN_WEIGHTS):
        w = inp[name].astype(_jnp.float32)
        if MOMENT_SCALE is None:
            s = _jnp.sqrt(_jnp.mean(_jnp.square(w)) + 1e-30)
        else:
            s = MOMENT_SCALE[name]
        km, kv = _jax.random.split(_jax.random.fold_in(key, i + 1))
        out[name] = w
        out["m_" + name] = s * _jax.random.normal(km, w.shape, _jnp.float32)
        out["v_" + name] = (s * s) * _jax.random.uniform(kv, w.shape, _jnp.float32, 0.5, 1.5)
    if N_MICROBATCH > 1:
        for name, axis in PER_EXAMPLE_BATCH_AXIS.items():
            out[name] = _to_microbatches(out[name], axis)
    return {'x': out['x'], 'norm1_g': out['norm1_g'], 'w_in': out['w_in'], 'gmlp_ln_g': out['gmlp_ln_g'], 'gmlp_ln_b': out['gmlp_ln_b'], 'w_spatial': out['w_spatial'], 'b_spatial': out['b_spatial'], 'conv_w': out['conv_w'], 'group_norm_g': out['group_norm_g'], 'w_out': out['w_out'], 'norm2_g': out['norm2_g'], 'w_gate': out['w_gate'], 'w_up': out['w_up'], 'w_down': out['w_down'], 'final_norm_g': out['final_norm_g'], 'loss_target': out['loss_target'], 'm_norm1_g': out['m_norm1_g'], 'm_w_in': out['m_w_in'], 'm_gmlp_ln_g': out['m_gmlp_ln_g'], 'm_gmlp_ln_b': out['m_gmlp_ln_b'], 'm_w_spatial': out['m_w_spatial'], 'm_b_spatial': out['m_b_spatial'], 'm_conv_w': out['m_conv_w'], 'm_group_norm_g': out['m_group_norm_g'], 'm_w_out': out['m_w_out'], 'm_norm2_g': out['m_norm2_g'], 'm_w_gate': out['m_w_gate'], 'm_w_up': out['m_w_up'], 'm_w_down': out['m_w_down'], 'm_final_norm_g': out['m_final_norm_g'], 'v_norm1_g': out['v_norm1_g'], 'v_w_in': out['v_w_in'], 'v_gmlp_ln_g': out['v_gmlp_ln_g'], 'v_gmlp_ln_b': out['v_gmlp_ln_b'], 'v_w_spatial': out['v_w_spatial'], 'v_b_spatial': out['v_b_spatial'], 'v_conv_w': out['v_conv_w'], 'v_group_norm_g': out['v_group_norm_g'], 'v_w_out': out['v_w_out'], 'v_norm2_g': out['v_norm2_g'], 'v_w_gate': out['v_w_gate'], 'v_w_up': out['v_w_up'], 'v_w_down': out['v_w_down'], 'v_final_norm_g': out['v_final_norm_g']}


def _loss(weights, diff, rest, loss_target):
    with _jax.named_scope("forward"):
        args = {**rest, TWIN_DIFF_INPUT: diff, **{k: w.astype(_WEIGHT_DTYPES[k]) for k, w in weights.items()}}
        y = _forward(args)
    with _jax.named_scope("loss_head"):
        err = _jnp.square(y.astype(_jnp.float32) - loss_target)
        return 0.5 * _jnp.sum(_jnp.mean(err, axis=-1)) if err.ndim else 0.5 * err


def _adamw(w, g, m, v):
    m = ADAM_B1 * m + (1.0 - ADAM_B1) * g
    v = ADAM_B2 * v + (1.0 - ADAM_B2) * _jnp.square(g)
    m_hat = m / (1.0 - ADAM_B1 ** ADAM_STEP)
    v_hat = v / (1.0 - ADAM_B2 ** ADAM_STEP)
    delta = -ADAM_LR * (m_hat / (_jnp.sqrt(v_hat) + ADAM_EPS) + ADAM_WD * w)
    return delta, m, v


def reference(x, norm1_g, w_in, gmlp_ln_g, gmlp_ln_b, w_spatial, b_spatial, conv_w, group_norm_g, w_out, norm2_g, w_gate, w_up, w_down, final_norm_g, loss_target, m_norm1_g, m_w_in, m_gmlp_ln_g, m_gmlp_ln_b, m_w_spatial, m_b_spatial, m_conv_w, m_group_norm_g, m_w_out, m_norm2_g, m_w_gate, m_w_up, m_w_down, m_final_norm_g, v_norm1_g, v_w_in, v_gmlp_ln_g, v_gmlp_ln_b, v_w_spatial, v_b_spatial, v_conv_w, v_group_norm_g, v_w_out, v_norm2_g, v_w_gate, v_w_up, v_w_down, v_final_norm_g):
    given = dict(x=x, norm1_g=norm1_g, w_in=w_in, gmlp_ln_g=gmlp_ln_g, gmlp_ln_b=gmlp_ln_b, w_spatial=w_spatial, b_spatial=b_spatial, conv_w=conv_w, group_norm_g=group_norm_g, w_out=w_out, norm2_g=norm2_g, w_gate=w_gate, w_up=w_up, w_down=w_down, final_norm_g=final_norm_g, loss_target=loss_target, m_norm1_g=m_norm1_g, m_w_in=m_w_in, m_gmlp_ln_g=m_gmlp_ln_g, m_gmlp_ln_b=m_gmlp_ln_b, m_w_spatial=m_w_spatial, m_b_spatial=m_b_spatial, m_conv_w=m_conv_w, m_group_norm_g=m_group_norm_g, m_w_out=m_w_out, m_norm2_g=m_norm2_g, m_w_gate=m_w_gate, m_w_up=m_w_up, m_w_down=m_w_down, m_final_norm_g=m_final_norm_g, v_norm1_g=v_norm1_g, v_w_in=v_w_in, v_gmlp_ln_g=v_gmlp_ln_g, v_gmlp_ln_b=v_gmlp_ln_b, v_w_spatial=v_w_spatial, v_b_spatial=v_b_spatial, v_conv_w=v_conv_w, v_group_norm_g=v_group_norm_g, v_w_out=v_w_out, v_norm2_g=v_norm2_g, v_w_gate=v_w_gate, v_w_up=v_w_up, v_w_down=v_w_down, v_final_norm_g=v_final_norm_g)
    weights = {n: given[n] for n in TWIN_WEIGHTS}
    shared = {n: given[n] for n in SHARED_INPUTS}
    per_example = {n: given[n] for n in ['x']}
    grad_fn = _jax.value_and_grad(_loss, argnums=(0, 1))

    def one_microbatch(ex, loss_target):
        ex = dict(ex)
        diff = ex.pop(TWIN_DIFF_INPUT)
        return grad_fn(weights, diff, {**shared, **ex}, loss_target)

    if N_MICROBATCH == 1:
        loss, (grad_w, grad_x) = one_microbatch(per_example, given["loss_target"])
    else:
        def body(carry, xs):
            loss_sum, grad_sum = carry
            l_k, (gw_k, gx_k) = one_microbatch(xs[0], xs[1])
            with _jax.named_scope("update"):
                return (loss_sum + l_k, _jax.tree.map(_jnp.add, grad_sum, gw_k)), gx_k

        init = (_jnp.zeros((), _jnp.float32), _jax.tree.map(_jnp.zeros_like, weights))
        (loss, grad_w), grad_x = _jax.lax.scan(body, init, (per_example, given["loss_target"]))
    with _jax.named_scope("update"):
        delta_w, new_m, new_v = {}, {}, {}
        for n in TWIN_WEIGHTS:
            delta_w[n], new_m[n], new_v[n] = _adamw(weights[n], grad_w[n], given["m_" + n], given["v_" + n])
    return (loss, grad_x, *[grad_w[n] for n in TWIN_WEIGHTS], *[delta_w[n] for n in TWIN_WEIGHTS],
            *[new_m[n] for n in TWIN_WEIGHTS], *[new_v[n] for n in TWIN_WEIGHTS])
```

```python
import functools
import math
import operator

import jax
import jax.numpy as jnp
from jax import lax
from jax.experimental import pallas as pl
from jax.experimental.pallas import tpu as pltpu

F32 = jnp.float32
BF16 = jnp.bfloat16
MESH = pl.DeviceIdType.MESH

N_DEV = 8
N_LAYERS = 2
HEADS = 8
BLK = 128
CHUNK = 64
HALO = 16
RMS_EPS = 1e-6
LN_EPS = 1e-5
ADAM_LR, ADAM_B1, ADAM_B2, ADAM_EPS, ADAM_WD, ADAM_STEP = 0.001, 0.9, 0.999, 1e-8, 0.01, 10
GELU_C = math.sqrt(2.0 / math.pi)
GELU_A = 0.044715

VMEM_LIMIT_V7X = 56 * 1024 * 1024
_TM = 512
_TT = 512
_TM_MIX = 256
_TM_NORM = 512

RELATIONS = [(0, 0, 1), (0, 1, 0), (0, 1, 1), (1, 0, 0), (1, 0, 1), (1, 1, 0), (1, 1, 1)]


def _cparams(n_axes):
    return pltpu.CompilerParams(dimension_semantics=("arbitrary",) * n_axes, vmem_limit_bytes=VMEM_LIMIT_V7X)


def _sds(shape, dtype):
    return jax.ShapeDtypeStruct(tuple(shape), dtype)


def _matmul(name, grid, nk, kaxis, pairs, dims, extras, outs, epilogue, sum_pairs, acc_shape):
    n_p, n_e, n_o = len(pairs), len(extras), len(outs)
    n_acc = 0 if nk == 1 else (1 if sum_pairs else n_p)

    def body(*refs):
        a_refs = refs[0:2 * n_p:2]
        b_refs = refs[1:2 * n_p:2]
        e_refs = refs[2 * n_p:2 * n_p + n_e]
        o_refs = refs[2 * n_p + n_e:2 * n_p + n_e + n_o]
        acc_refs = refs[2 * n_p + n_e + n_o:]
        prods = [lax.dot_general(a[...], b[...], (dims, ((), ())), preferred_element_type=F32)
                 for a, b in zip(a_refs, b_refs)]
        if sum_pairs and n_p > 1:
            prods = [functools.reduce(operator.add, prods)]
        if nk == 1:
            epilogue(prods, e_refs, o_refs)
            return
        k = pl.program_id(kaxis)

        @pl.when(k == 0)
        def _():
            for acc, p in zip(acc_refs, prods):
                acc[...] = p

        @pl.when(k > 0)
        def _():
            for acc, p in zip(acc_refs, prods):
                acc[...] += p

        @pl.when(k == nk - 1)
        def _():
            epilogue([acc[...] for acc in acc_refs], e_refs, o_refs)

    operands, in_specs = [], []
    for a, a_spec, b, b_spec in pairs:
        operands += [a, b]
        in_specs += [a_spec, b_spec]
    for e, e_spec in extras:
        operands.append(e)
        in_specs.append(e_spec)
    res = pl.pallas_call(
        body, name=name, grid=grid,
        out_shape=tuple(o for o, _ in outs), in_specs=in_specs, out_specs=tuple(s for _, s in outs),
        scratch_shapes=[pltpu.VMEM(acc_shape, F32) for _ in range(n_acc)],
        compiler_params=_cparams(len(grid)),
    )(*operands)
    return res


NN = ((1,), (0,))
NT = ((1,), (1,))
TN = ((0,), (0,))


def _silu_parts(g):
    s = 1.0 / (1.0 + jnp.exp(-g))
    return s, g * s


def _mm_in(h, w_in):
    t, d = h.shape
    c = w_in.shape[2]
    tm = min(_TM, t)

    def epi(accs, e, o):
        o[0][...] = accs[0].astype(BF16)

    return _matmul(
        "mm_in", (N_DEV, t // tm), 1, None,
        [(h, pl.BlockSpec((tm, d), lambda j, i: (i, 0)), w_in, pl.BlockSpec((None, d, c), lambda j, i: (j, 0, 0)))],
        NN, [], [(_sds((t, N_DEV * c), BF16), pl.BlockSpec((tm, c), lambda j, i: (i, j)))], epi, True, None)[0]


def _mm_out(y, w_out, x):
    t, m = y.shape
    d = w_out.shape[1]
    tm, tn = min(_TM, t), 1024

    def epi(accs, e, o):
        o[0][...] = e[0][...] + accs[0]

    return _matmul(
        "mm_out", (t // tm, d // tn), 1, None,
        [(y, pl.BlockSpec((tm, m), lambda i, j: (i, 0)), w_out, pl.BlockSpec((m, tn), lambda i, j: (0, j)))],
        NN, [(x, pl.BlockSpec((tm, tn), lambda i, j: (i, j)))],
        [(_sds((t, d), F32), pl.BlockSpec((tm, tn), lambda i, j: (i, j)))], epi, True, None)[0]


def _mm_swiglu(h2, wg, wu):
    t, d = h2.shape
    c = wg.shape[2]
    tm = min(_TM, t)

    def epi(accs, e, o):
        g, u = accs
        _, sg = _silu_parts(g)
        o[0][...] = g.astype(BF16)
        o[1][...] = u.astype(BF16)
        o[2][...] = (sg * u).astype(BF16)

    wspec = pl.BlockSpec((None, d, c), lambda j, i: (j, 0, 0))
    hspec = pl.BlockSpec((tm, d), lambda j, i: (i, 0))
    ospec = pl.BlockSpec((None, tm, c), lambda j, i: (j, i, 0))
    osh = _sds((N_DEV, t, c), BF16)
    return _matmul("mm_swiglu", (N_DEV, t // tm), 1, None, [(h2, hspec, wg, wspec), (h2, hspec, wu, wspec)],
                   NN, [], [(osh, ospec)] * 3, epi, False, None)


def _mm_down(act, wd, x1):
    _, t, c = act.shape
    d = wd.shape[2]
    tm = min(_TM, t)

    def epi(accs, e, o):
        o[0][...] = e[0][...] + accs[0]

    return _matmul(
        "mm_down", (t // tm, N_DEV), N_DEV, 1,
        [(act, pl.BlockSpec((None, tm, c), lambda i, k: (k, i, 0)), wd, pl.BlockSpec((None, c, d), lambda i, k: (k, 0, 0)))],
        NN, [(x1, pl.BlockSpec((tm, d), lambda i, k: (i, 0)))],
        [(_sds((t, d), F32), pl.BlockSpec((tm, d), lambda i, k: (i, 0)))], epi, True, (tm, d))[0]


def _mm_dact(dxb, wd, gate, up):
    t, d = dxb.shape
    c = wd.shape[1]
    tm = min(_TM, t)

    def epi(accs, e, o):
        da = accs[0]
        g = e[0][...].astype(F32)
        u = e[1][...].astype(F32)
        s, sg = _silu_parts(g)
        o[0][...] = (da * u * (s + sg * (1.0 - s))).astype(BF16)
        o[1][...] = (da * sg).astype(BF16)

    bspec = pl.BlockSpec((None, tm, c), lambda j, i: (j, i, 0))
    osh = _sds((N_DEV, t, c), BF16)
    return _matmul(
        "mm_dact", (N_DEV, t // tm), 1, None,
        [(dxb, pl.BlockSpec((tm, d), lambda j, i: (i, 0)), wd, pl.BlockSpec((None, c, d), lambda j, i: (j, 0, 0)))],
        NT, [(gate, bspec), (up, bspec)], [(osh, bspec)] * 2, epi, True, None)


def _mm_dh2(dgate, dup, wg, wu):
    _, t, c = dgate.shape
    d = wg.shape[1]
    tm = min(_TM, t)

    def epi(accs, e, o):
        o[0][...] = accs[0]

    aspec = pl.BlockSpec((None, tm, c), lambda i, k: (k, i, 0))
    wspec = pl.BlockSpec((None, d, c), lambda i, k: (k, 0, 0))
    return _matmul("mm_dh2", (t // tm, N_DEV), N_DEV, 1, [(dgate, aspec, wg, wspec), (dup, aspec, wu, wspec)], NT, [],
                   [(_sds((t, d), F32), pl.BlockSpec((tm, d), lambda i, k: (i, 0)))], epi, True, (tm, d))[0]


def _mm_dw_cols(name, a, bs):
    t, d = a.shape
    c = bs[0].shape[2]
    tt = min(_TT, t)
    nk = t // tt

    def epi(accs, e, o):
        for acc, out in zip(accs, o):
            out[...] = acc.astype(BF16)

    aspec = pl.BlockSpec((tt, d), lambda j, k: (k, 0))
    bspec = pl.BlockSpec((None, tt, c), lambda j, k: (j, k, 0))
    ospec = pl.BlockSpec((None, d, c), lambda j, k: (j, 0, 0))
    return _matmul(name, (N_DEV, nk), nk, 1, [(a, aspec, b, bspec) for b in bs], TN, [],
                   [(_sds((N_DEV, d, c), BF16), ospec)] * len(bs), epi, False, (d, c))


def _mm_dw_in(h, dz):
    t, d = h.shape
    c = dz.shape[1] // N_DEV
    tt = min(_TT, t)
    nk = t // tt

    def epi(accs, e, o):
        o[0][...] = accs[0].astype(BF16)

    return _matmul(
        "mm_dw_in", (N_DEV, nk), nk, 1,
        [(h, pl.BlockSpec((tt, d), lambda j, k: (k, 0)), dz, pl.BlockSpec((tt, c), lambda j, k: (k, j)))], TN, [],
        [(_sds((N_DEV, d, c), BF16), pl.BlockSpec((None, d, c), lambda j, k: (j, 0, 0)))], epi, True, (d, c))[0]


def _mm_dw_down(act, dxb):
    _, t, c = act.shape
    d = dxb.shape[1]
    tt = min(_TT, t)
    nk = t // tt

    def epi(accs, e, o):
        o[0][...] = accs[0].astype(BF16)

    return _matmul(
        "mm_dw_down", (N_DEV, nk), nk, 1,
        [(act, pl.BlockSpec((None, tt, c), lambda j, k: (j, k, 0)), dxb, pl.BlockSpec((tt, d), lambda j, k: (k, 0)))], TN, [],
        [(_sds((N_DEV, c, d), BF16), pl.BlockSpec((None, c, d), lambda j, k: (j, 0, 0)))], epi, True, (c, d))[0]


def _mm_dw_out(y, dxb):
    t, m = y.shape
    d = dxb.shape[1]
    tt = min(_TT, t)
    nk = t // tt
    tmm = 1024

    def epi(accs, e, o):
        o[0][...] = accs[0].astype(BF16)

    return _matmul(
        "mm_dw_out", (m // tmm, nk), nk, 1,
        [(y, pl.BlockSpec((tt, tmm), lambda j, k: (k, j)), dxb, pl.BlockSpec((tt, d), lambda j, k: (k, 0)))], TN, [],
        [(_sds((m, d), BF16), pl.BlockSpec((tmm, d), lambda j, k: (j, 0)))], epi, True, (tmm, d))[0]


def _mm_dy(dxb, w_out):
    t, d = dxb.shape
    m = w_out.shape[0]
    tm, tn = min(_TM, t), 1024

    def epi(accs, e, o):
        o[0][...] = accs[0].astype(BF16)

    return _matmul(
        "mm_dy", (t // tm, m // tn), 1, None,
        [(dxb, pl.BlockSpec((tm, d), lambda i, j: (i, 0)), w_out, pl.BlockSpec((tn, d), lambda i, j: (j, 0)))], NT, [],
        [(_sds((t, m), BF16), pl.BlockSpec((tm, tn), lambda i, j: (i, j)))], epi, True, None)[0]


def _mm_dh(dz, w_in):
    t = dz.shape[0]
    _, d, c = w_in.shape
    tm = min(_TM, t)

    def epi(accs, e, o):
        o[0][...] = accs[0]

    return _matmul(
        "mm_dh", (t // tm, N_DEV), N_DEV, 1,
        [(dz, pl.BlockSpec((tm, c), lambda i, k: (i, k)), w_in, pl.BlockSpec((None, d, c), lambda i, k: (k, 0, 0)))], NT, [],
        [(_sds((t, d), F32), pl.BlockSpec((tm, d), lambda i, k: (i, 0)))], epi, True, (tm, d))[0]


def _rmsnorm_fwd(x, g):
    t, d = x.shape
    tm = min(_TM_NORM, t)

    def body(x_ref, g_ref, o_ref):
        xv = x_ref[...]
        rs = lax.rsqrt(jnp.mean(xv * xv, axis=-1, keepdims=True) + RMS_EPS)
        o_ref[...] = (xv * rs * g_ref[...]).astype(BF16)

    return pl.pallas_call(
        body, name="rmsnorm_fwd", grid=(t // tm,), out_shape=_sds((t, d), BF16),
        in_specs=[pl.BlockSpec((tm, d), lambda i: (i, 0)), pl.BlockSpec((1, d), lambda i: (0, 0))],
        out_specs=pl.BlockSpec((tm, d), lambda i: (i, 0)), compiler_params=_cparams(1))(x, g)


def _rmsnorm_bwd_math(xv, g, dh):
    rs = lax.rsqrt(jnp.mean(xv * xv, axis=-1, keepdims=True) + RMS_EPS)
    xh = xv * rs
    gd = dh * g
    dx = rs * (gd - xh * jnp.mean(gd * xh, axis=-1, keepdims=True))
    return dx, jnp.sum(dh * xh, axis=0, keepdims=True)


def _rmsnorm_bwd(x, g, dh, dres):
    t, d = x.shape
    tm = min(_TM_NORM, t)

    def body(x_ref, g_ref, dh_ref, dres_ref, dx_ref, dxb_ref, dg_ref):
        dx, dg = _rmsnorm_bwd_math(x_ref[...], g_ref[...], dh_ref[...])
        dx = dx + dres_ref[...]
        dx_ref[...] = dx
        dxb_ref[...] = dx.astype(BF16)

        @pl.when(pl.program_id(0) == 0)
        def _():
            dg_ref[...] = dg

        @pl.when(pl.program_id(0) > 0)
        def _():
            dg_ref[...] += dg

    row = pl.BlockSpec((tm, d), lambda i: (i, 0))
    vec = pl.BlockSpec((1, d), lambda i: (0, 0))
    return pl.pallas_call(
        body, name="rmsnorm_bwd", grid=(t // tm,),
        out_shape=(_sds((t, d), F32), _sds((t, d), BF16), _sds((1, d), F32)),
        in_specs=[row, vec, row, row], out_specs=(row, row, vec), compiler_params=_cparams(1))(x, g, dh, dres)


def _loss_head(x, g, target):
    t, d = x.shape
    tm = min(_TM_NORM, t)

    def body(x_ref, g_ref, t_ref, dx_ref, dxb_ref, dg_ref, loss_ref):
        xv, gv = x_ref[...], g_ref[...]
        rs = lax.rsqrt(jnp.mean(xv * xv, axis=-1, keepdims=True) + RMS_EPS)
        diff = xv * rs * gv - t_ref[...]
        part = 0.5 * jnp.sum(jnp.mean(diff * diff, axis=-1, keepdims=True), axis=0, keepdims=True)
        part = jnp.broadcast_to(part, (1, 128))
        dx, dg = _rmsnorm_bwd_math(xv, gv, diff * (1.0 / d))
        dx_ref[...] = dx
        dxb_ref[...] = dx.astype(BF16)

        @pl.when(pl.program_id(0) == 0)
        def _():
            dg_ref[...] = dg
            loss_ref[...] = part

        @pl.when(pl.program_id(0) > 0)
        def _():
            dg_ref[...] += dg
            loss_ref[...] += part

    row = pl.BlockSpec((tm, d), lambda i: (i, 0))
    vec = pl.BlockSpec((1, d), lambda i: (0, 0))
    return pl.pallas_call(
        body, name="loss_head", grid=(t // tm,),
        out_shape=(_sds((t, d), F32), _sds((t, d), BF16), _sds((1, d), F32), _sds((1, 128), F32)),
        in_specs=[row, vec, row], out_specs=(row, row, vec, pl.BlockSpec((1, 128), lambda i: (0, 0))),
        compiler_params=_cparams(1))(x, g, target)


def _gelu(x):
    th = jnp.tanh(GELU_C * (x + GELU_A * x * x * x))
    return 0.5 * x * (1.0 + th), th


def _gelu_grad(x, th):
    return 0.5 * (1.0 + th) + 0.5 * x * (1.0 - th * th) * GELU_C * (1.0 + 3.0 * GELU_A * x * x)


def _masked_ws(ws_ref, h):
    i = lax.broadcasted_iota(jnp.int32, (BLK, BLK), 0) // CHUNK
    j = lax.broadcasted_iota(jnp.int32, (BLK, BLK), 1) // CHUNK
    return jnp.where(j <= i, ws_ref[h], 0.0)


def _shift_down(q, n, first_rows):
    rolled = pltpu.roll(q, n, 0)
    row = lax.broadcasted_iota(jnp.int32, q.shape, 0)
    for r, val in enumerate(first_rows):
        rolled = jnp.where(row == r, val, rolled)
    return rolled


def _shift_up(q, n, last_rows):
    tm = q.shape[0]
    rolled = pltpu.roll(q, tm - n, 0)
    row = lax.broadcasted_iota(jnp.int32, q.shape, 0)
    for r, val in enumerate(last_rows):
        rolled = jnp.where(row == tm - n + r, val, rolled)
    return rolled


def _mixer_specs(t, a, tm):
    hb = tm // HALO
    last = t // HALO - 1
    tile = pl.BlockSpec((tm, 5 * a), lambda i: (i, 0))
    prev = [pl.BlockSpec((HALO, a), functools.partial(lambda i, col: (jnp.maximum(i * hb - 1, 0), col), col=col))
            for col in (3, 4)]
    nxt = [pl.BlockSpec((HALO, a), functools.partial(lambda i, col: (jnp.minimum((i + 1) * hb, last), col), col=col))
           for col in (2, 3, 4)]
    return tile, prev, nxt


def _group_a_fwd(zu, zv, lng, lnb, ws_ref, bb_ref, mixed_ref, vln_ref):
    u, thu = _gelu(zu)
    v, thv = _gelu(zv)
    mu = jnp.mean(v, axis=-1, keepdims=True)
    vc = v - mu
    rs = lax.rsqrt(jnp.mean(vc * vc, axis=-1, keepdims=True) + LN_EPS)
    vhat = vc * rs
    vln_ref[...] = vhat * lng + lnb
    tm, a = zu.shape
    hd = a // HEADS
    for h in range(HEADS):
        w = _masked_ws(ws_ref, h).astype(BF16)
        for b in range(tm // BLK):
            rows, cols = pl.ds(b * BLK, BLK), pl.ds(h * hd, hd)
            mixed_ref[rows, cols] = jnp.dot(w, vln_ref[rows, cols].astype(BF16), preferred_element_type=F32) + bb_ref[h]
    return u, thu, thv, rs, vhat


def _mixer_fwd(z, ln_g, ln_b, w_spatial, bb, conv_w, gg):
    t = z.shape[0]
    a = z.shape[1] // 5
    tm = min(_TM_MIX, t)
    tile, prev, _ = _mixer_specs(t, a, tm)

    def body(z_ref, pc_ref, ph_ref, lng_ref, lnb_ref, ws_ref, bb_ref, cw_ref, gg_ref, y_ref, mixed_ref, vln_ref):
        i = pl.program_id(0)
        zu = z_ref[:, 0:a].astype(F32)
        zv = z_ref[:, a:2 * a].astype(F32)
        u, _, _, _, _ = _group_a_fwd(zu, zv, lng_ref[...], lnb_ref[...], ws_ref, bb_ref, mixed_ref, vln_ref)
        ya = u * mixed_ref[...]
        ra = lax.rsqrt(jnp.mean(ya * ya, axis=-1, keepdims=True) + RMS_EPS)
        y_ref[:, 0:a] = (ya * ra * gg_ref[:, 0:a]).astype(BF16)

        zb = z_ref[:, 2 * a:3 * a].astype(F32)
        q = z_ref[:, 3 * a:4 * a].astype(F32) * z_ref[:, 4 * a:5 * a].astype(F32)
        qp = jnp.where(i > 0, pc_ref[...].astype(F32) * ph_ref[...].astype(F32), 0.0)
        qm1 = _shift_down(q, 1, [qp[HALO - 1:HALO]])
        qm2 = _shift_down(q, 2, [qp[HALO - 2:HALO - 1], qp[HALO - 1:HALO]])
        cv = cw_ref[0:1, :] * qm2 + cw_ref[1:2, :] * qm1 + cw_ref[2:3, :] * q
        yb = zb * cv
        rb = lax.rsqrt(jnp.mean(yb * yb, axis=-1, keepdims=True) + RMS_EPS)
        y_ref[:, a:2 * a] = (yb * rb * gg_ref[:, a:2 * a]).astype(BF16)

    full = lambda shape: pl.BlockSpec(shape, lambda i: (0,) * len(shape))
    return pl.pallas_call(
        body, name="mixer_fwd", grid=(t // tm,), out_shape=_sds((t, 2 * a), BF16),
        in_specs=[tile, *prev, full((1, a)), full((1, a)), full(w_spatial.shape), full(bb.shape), full(conv_w.shape),
                  full((1, 2 * a))],
        out_specs=pl.BlockSpec((tm, 2 * a), lambda i: (i, 0)),
        scratch_shapes=[pltpu.VMEM((tm, a), F32), pltpu.VMEM((tm, a), F32)],
        compiler_params=_cparams(1))(z, z, z, ln_g, ln_b, w_spatial, bb, conv_w, gg)


def _mixer_bwd(z, dy, ln_g, ln_b, w_spatial, bb, conv_w, gg):
    t = z.shape[0]
    a = z.shape[1] // 5
    hd = a // HEADS
    tm = min(_TM_MIX, t)
    n_tiles = t // tm
    tile, prev, nxt = _mixer_specs(t, a, tm)
    hb = tm // HALO
    dy_tile = pl.BlockSpec((tm, 2 * a), lambda i: (i, 0))
    dy_next = pl.BlockSpec((HALO, a), lambda i: (jnp.minimum((i + 1) * hb, t // HALO - 1), 1))

    def body(z_ref, pc_ref, ph_ref, nb_ref, nc_ref, nh_ref, dy_ref, ndy_ref, lng_ref, lnb_ref, ws_ref, bb_ref, cw_ref,
             gg_ref, dz_ref, dlng_ref, dlnb_ref, dws_ref, dbb_ref, dcw_ref, dgg_ref, mixed_ref, vln_ref, dmix_ref,
             dvln_ref):
        i = pl.program_id(0)

        @pl.when(i == 0)
        def _():
            for ref in (dlng_ref, dlnb_ref, dws_ref, dbb_ref, dcw_ref, dgg_ref):
                ref[...] = jnp.zeros(ref.shape, F32)

        lng = lng_ref[...]
        zu = z_ref[:, 0:a].astype(F32)
        zv = z_ref[:, a:2 * a].astype(F32)
        u, thu, thv, rs, vhat = _group_a_fwd(zu, zv, lng, lnb_ref[...], ws_ref, bb_ref, mixed_ref, vln_ref)
        mixed = mixed_ref[...]
        ya = u * mixed
        ra = lax.rsqrt(jnp.mean(ya * ya, axis=-1, keepdims=True) + RMS_EPS)
        da = dy_ref[:, 0:a].astype(F32)
        yah = ya * ra
        dgg_ref[:, 0:a] += jnp.sum(da * yah, axis=0, keepdims=True)
        ga = da * gg_ref[:, 0:a]
        dya = ra * (ga - yah * jnp.mean(ga * yah, axis=-1, keepdims=True))
        dz_ref[:, 0:a] = (dya * mixed * _gelu_grad(zu, thu)).astype(BF16)
        dmix_ref[...] = dya * u
        for h in range(HEADS):
            w = _masked_ws(ws_ref, h).astype(BF16)
            dw = jnp.zeros((BLK, BLK), F32)
            db = jnp.zeros((BLK, hd), F32)
            for b in range(tm // BLK):
                rows, cols = pl.ds(b * BLK, BLK), pl.ds(h * hd, hd)
                dm = dmix_ref[rows, cols]
                dmb = dm.astype(BF16)
                db = db + dm
                dw = dw + lax.dot_general(dmb, vln_ref[rows, cols].astype(BF16), (NT, ((), ())),
                                          preferred_element_type=F32)
                dvln_ref[rows, cols] = lax.dot_general(w, dmb, (TN, ((), ())), preferred_element_type=F32)
            dws_ref[h] += dw
            dbb_ref[h] += db
        dvln = dvln_ref[...]
        dlng_ref[...] += jnp.sum(dvln * vhat, axis=0, keepdims=True)
        dlnb_ref[...] += jnp.sum(dvln, axis=0, keepdims=True)
        dvh = dvln * lng
        dv = rs * (dvh - jnp.mean(dvh, axis=-1, keepdims=True) - vhat * jnp.mean(dvh * vhat, axis=-1, keepdims=True))
        dz_ref[:, a:2 * a] = (dv * _gelu_grad(zv, thv)).astype(BF16)

        w0, w1, w2 = cw_ref[0:1, :], cw_ref[1:2, :], cw_ref[2:3, :]
        ggb = gg_ref[:, a:2 * a]
        zb = z_ref[:, 2 * a:3 * a].astype(F32)
        zc = z_ref[:, 3 * a:4 * a].astype(F32)
        zh = z_ref[:, 4 * a:5 * a].astype(F32)
        q = zc * zh
        qp = jnp.where(i > 0, pc_ref[...].astype(F32) * ph_ref[...].astype(F32), 0.0)
        qm1 = _shift_down(q, 1, [qp[HALO - 1:HALO]])
        qm2 = _shift_down(q, 2, [qp[HALO - 2:HALO - 1], qp[HALO - 1:HALO]])
        cv = w0 * qm2 + w1 * qm1 + w2 * q

        def conv_out_grad(zb_, cv_, dout_):
            yb = zb_ * cv_
            rb = lax.rsqrt(jnp.mean(yb * yb, axis=-1, keepdims=True) + RMS_EPS)
            ybh = yb * rb
            gb = dout_ * ggb
            dyb = rb * (gb - ybh * jnp.mean(gb * ybh, axis=-1, keepdims=True))
            return dyb * zb_, dyb * cv_, ybh

        db_out = dy_ref[:, a:2 * a].astype(F32)
        g, dzb, ybh = conv_out_grad(zb, cv, db_out)
        dgg_ref[:, a:2 * a] += jnp.sum(db_out * ybh, axis=0, keepdims=True)
        dz_ref[:, 2 * a:3 * a] = dzb.astype(BF16)
        qn = nc_ref[...].astype(F32) * nh_ref[...].astype(F32)
        zbn = nb_ref[...].astype(F32)
        cvn = w0 * _shift_down(qn, 2, [q[tm - 2:tm - 1], q[tm - 1:tm]]) + w1 * _shift_down(qn, 1, [q[tm - 1:tm]]) + w2 * qn
        gn, _, _ = conv_out_grad(zbn, cvn, ndy_ref[...].astype(F32))
        gn = jnp.where(i < n_tiles - 1, gn, 0.0)
        dq = w2 * g + w1 * _shift_up(g, 1, [gn[0:1]]) + w0 * _shift_up(g, 2, [gn[0:1], gn[1:2]])
        dz_ref[:, 3 * a:4 * a] = (dq * zh).astype(BF16)
        dz_ref[:, 4 * a:5 * a] = (dq * zc).astype(BF16)
        dcw_ref[0:1, :] += jnp.sum(g * qm2, axis=0, keepdims=True)
        dcw_ref[1:2, :] += jnp.sum(g * qm1, axis=0, keepdims=True)
        dcw_ref[2:3, :] += jnp.sum(g * q, axis=0, keepdims=True)

        @pl.when(i == n_tiles - 1)
        def _():
            for h in range(HEADS):
                dbb_ref[h] = jnp.broadcast_to(jnp.sum(dbb_ref[h], axis=1, keepdims=True), (BLK, hd))
                dws_ref[h] = _masked_ws(dws_ref, h)

    full = lambda shape: pl.BlockSpec(tuple(shape), lambda i: (0,) * len(shape))
    out_shapes = (_sds((t, 5 * a), BF16), _sds((1, a), F32), _sds((1, a), F32), _sds(w_spatial.shape, F32),
                  _sds(bb.shape, F32), _sds((8, a), F32), _sds((1, 2 * a), F32))
    return pl.pallas_call(
        body, name="mixer_bwd", grid=(n_tiles,), out_shape=out_shapes,
        in_specs=[tile, *prev, *nxt, dy_tile, dy_next, full((1, a)), full((1, a)), full(w_spatial.shape), full(bb.shape),
                  full(conv_w.shape), full((1, 2 * a))],
        out_specs=(tile, *[full(s.shape) for s in out_shapes[1:]]),
        scratch_shapes=[pltpu.VMEM((tm, a), F32)] * 4,
        compiler_params=_cparams(1))(z, z, z, z, z, z, dy, dy, ln_g, ln_b, w_spatial, bb, conv_w, gg)


def _place():
    return lax.axis_index("x"), lax.axis_index("y"), lax.axis_index("c")


def _all_gather(name, shards):
    n = len(shards)

    def body(*refs):
        in_refs, out_refs = refs[:n], refs[n:2 * n]
        send_sems, recv_sems, local_sems = refs[2 * n:]
        x, y, c = _place()
        me, sibling = (x, y, c), (x, y, 1 - c)
        chips = [(1 - x, y), (x, 1 - y), (1 - x, 1 - y)]

        def copy(a, k, block, to, src=None):
            slot = out_refs[a].at[4 * block[0] + 2 * block[1] + block[2]]
            return pltpu.make_async_remote_copy(
                src_ref=slot if src is None else src, dst_ref=slot, send_sem=send_sems.at[a, k],
                recv_sem=recv_sems.at[a, k], device_id=to, device_id_type=MESH)

        mine, first, passed = [], [], []
        for a in range(n):
            mine.append(pltpu.make_async_copy(in_refs[a], out_refs[a].at[4 * x + 2 * y + c], local_sems.at[a]))
            mine[-1].start()
            first.append(copy(a, 0, me, sibling, src=in_refs[a]))
            first += [copy(a, 1 + j, me, (*chip, c), src=in_refs[a]) for j, chip in enumerate(chips)]
        for cp in first:
            cp.start()
        for a in range(n):
            for j, chip in enumerate(chips):
                copy(a, 1 + j, (*chip, c), me).wait_recv()
                passed.append(copy(a, 4 + j, (*chip, c), sibling))
                passed[-1].start()
        for a in range(n):
            copy(a, 0, sibling, me).wait_recv()
            for j, chip in enumerate(chips):
                copy(a, 4 + j, (*chip, 1 - c), me).wait_recv()
        for cp in first + passed:
            cp.wait_send()
        for cp in mine:
            cp.wait()

    hbm = pl.BlockSpec(memory_space=pl.ANY)
    return pl.pallas_call(
        body, name=name, out_shape=tuple(_sds((N_DEV, *s.shape), s.dtype) for s in shards),
        in_specs=[hbm] * n, out_specs=tuple([hbm] * n),
        scratch_shapes=[pltpu.SemaphoreType.DMA((n, 7)), pltpu.SemaphoreType.DMA((n, 7)), pltpu.SemaphoreType.DMA((n,))],
    )(*shards)


def _reduce_scatter_send(name, grads, layer, lands):
    n = len(grads)
    aliased = lands is not None

    def body(*refs):
        g_refs = refs[:n]
        land_refs = refs[2 * n:3 * n] if aliased else refs[n:2 * n]
        send_sems, recv_sems, local_sems = refs[-3:]
        x, y, c = _place()
        me = 4 * x + 2 * y + c
        copies, local = [], []
        for a in range(n):
            local.append(pltpu.make_async_copy(g_refs[a].at[me], land_refs[a].at[layer, me], local_sems.at[a]))
            local[-1].start()
            for k, (dx, dy, dc) in enumerate(RELATIONS):
                px, py, pc = x ^ dx, y ^ dy, c ^ dc
                peer = 4 * px + 2 * py + pc
                copies.append((
                    pltpu.make_async_remote_copy(
                        src_ref=g_refs[a].at[peer], dst_ref=land_refs[a].at[layer, me], send_sem=send_sems.at[a, k],
                        recv_sem=recv_sems.at[a, k], device_id=(px, py, pc), device_id_type=MESH),
                    pltpu.make_async_remote_copy(
                        src_ref=g_refs[a].at[peer], dst_ref=land_refs[a].at[layer, peer], send_sem=send_sems.at[a, k],
                        recv_sem=recv_sems.at[a, k], device_id=(px, py, pc), device_id_type=MESH)))
                copies[-1][0].start()
        for send, recv in copies:
            recv.wait_recv()
        for send, recv in copies:
            send.wait_send()
        for cp in local:
            cp.wait()

    hbm = pl.BlockSpec(memory_space=pl.ANY)
    out_shape = tuple(_sds((N_LAYERS, *g.shape), g.dtype) for g in grads)
    return pl.pallas_call(
        body, name=name, out_shape=out_shape,
        in_specs=[hbm] * (2 * n if aliased else n), out_specs=tuple([hbm] * n),
        input_output_aliases={n + a: a for a in range(n)} if aliased else {},
        scratch_shapes=[pltpu.SemaphoreType.DMA((n, 7)), pltpu.SemaphoreType.DMA((n, 7)), pltpu.SemaphoreType.DMA((n,))],
    )(*grads, *(lands if aliased else ()))


def _all_reduce_small(pack):
    r = pack.shape[0]

    def body(in_ref, out_ref, acc_ref, recv_ref, send_sems, recv_sems):
        x, y, c = _place()
        partners = [(x, y, 1 - c), (1 - x, y, c), (x, 1 - y, c)]
        acc_ref[0] = in_ref[...]
        for s, partner in enumerate(partners):
            cp = pltpu.make_async_remote_copy(
                src_ref=acc_ref.at[s], dst_ref=recv_ref.at[s], send_sem=send_sems.at[s], recv_sem=recv_sems.at[s],
                device_id=partner, device_id_type=MESH)
            cp.start()
            cp.wait()
            if s < 2:
                acc_ref[s + 1] = acc_ref[s] + recv_ref[s]
            else:
                out_ref[...] = acc_ref[s] + recv_ref[s]

    vmem = pl.BlockSpec(memory_space=pltpu.VMEM)
    return pl.pallas_call(
        body, name="all_reduce_small", out_shape=_sds(pack.shape, F32), in_specs=[vmem], out_specs=vmem,
        scratch_shapes=[pltpu.VMEM((3, r, 128), F32), pltpu.VMEM((3, r, 128), F32), pltpu.SemaphoreType.DMA((3,)),
                        pltpu.SemaphoreType.DMA((3,))],
        compiler_params=pltpu.CompilerParams(vmem_limit_bytes=VMEM_LIMIT_V7X),
    )(pack)


def _adamw_math(w, g, m, v):
    m = ADAM_B1 * m + (1.0 - ADAM_B1) * g
    v = ADAM_B2 * v + (1.0 - ADAM_B2) * (g * g)
    m_hat = m / (1.0 - ADAM_B1 ** ADAM_STEP)
    v_hat = v / (1.0 - ADAM_B2 ** ADAM_STEP)
    delta = -ADAM_LR * (m_hat / (jnp.sqrt(v_hat) + ADAM_EPS) + ADAM_WD * w)
    return delta, m, v


def _adamw_big(name, land, w, m, v):
    nl, _, r, c = land.shape
    tr = max(8, min(r, (256 * 640) // c // 8 * 8))
    while r % tr:
        tr -= 8

    def body(land_ref, w_ref, m_ref, v_ref, g_out, d_out, m_out, v_out):
        g = land_ref[0].astype(F32)
        for s in range(1, N_DEV):
            g = g + land_ref[s].astype(F32)
        delta, mn, vn = _adamw_math(w_ref[...], g, m_ref[...], v_ref[...])
        g_out[...] = g
        d_out[...] = delta
        m_out[...] = mn
        v_out[...] = vn

    blk = pl.BlockSpec((None, tr, c), lambda l, i: (l, i, 0))
    return pl.pallas_call(
        body, name=name, grid=(nl, r // tr), out_shape=tuple([_sds((nl, r, c), F32)] * 4),
        in_specs=[pl.BlockSpec((None, N_DEV, tr, c), lambda l, i: (l, 0, i, 0)), blk, blk, blk],
        out_specs=tuple([blk] * 4), compiler_params=_cparams(2))(land, w, m, v)


def _adamw_small(g, w, m, v):
    def body(g_ref, w_ref, m_ref, v_ref, d_out, m_out, v_out):
        delta, mn, vn = _adamw_math(w_ref[...], g_ref[...], m_ref[...], v_ref[...])
        d_out[...] = delta
        m_out[...] = mn
        v_out[...] = vn

    return pl.pallas_call(body, name="adamw_small", out_shape=tuple([_sds(g.shape, F32)] * 3),
                          compiler_params=pltpu.CompilerParams(vmem_limit_bytes=VMEM_LIMIT_V7X))(g, w, m, v)


def _rows(a):
    return a.reshape(-1, 128)


def kernel(x, norm1_g, w_in, gmlp_ln_g, gmlp_ln_b, w_spatial, b_spatial, conv_w, group_norm_g, w_out, norm2_g, w_gate, w_up, w_down, final_norm_g, loss_target, m_norm1_g, m_w_in, m_gmlp_ln_g, m_gmlp_ln_b, m_w_spatial, m_b_spatial, m_conv_w, m_group_norm_g, m_w_out, m_norm2_g, m_w_gate, m_w_up, m_w_down, m_final_norm_g, v_norm1_g, v_w_in, v_gmlp_ln_g, v_gmlp_ln_b, v_w_spatial, v_b_spatial, v_conv_w, v_group_norm_g, v_w_out, v_norm2_g, v_w_gate, v_w_up, v_w_down, v_final_norm_g):
    nl = N_LAYERS
    t, d = x.shape[1], x.shape[2]
    a = d // 2
    hd = a // HEADS
    xin = x.reshape(t, d)
    target = loss_target.reshape(t, d)
    me = 4 * lax.axis_index("x") + 2 * lax.axis_index("y") + lax.axis_index("c")

    big = {"w_in": w_in, "w_out": w_out, "w_gate": w_gate, "w_up": w_up, "w_down": w_down}
    gathered = []
    for l in range(nl):
        shards = [big[k][l].astype(BF16) for k in big]
        if l == 0:
            shards.append(conv_w)
        gathered.append(_all_gather(f"all_gather_l{l}", shards))
    conv_full = jnp.transpose(gathered[0][5], (1, 2, 0, 3)).reshape(nl, 3, a)
    bb = jnp.broadcast_to(b_spatial[..., None], (nl, HEADS, BLK, hd))

    saved = []
    xl = xin
    for l in range(nl):
        wi, wo, wg, wu, wd = gathered[l][:5]
        wo = wo.reshape(-1, d)
        h = _rmsnorm_fwd(xl, norm1_g[l:l + 1])
        z = _mm_in(h, wi)
        y = _mixer_fwd(z, gmlp_ln_g[l:l + 1], gmlp_ln_b[l:l + 1], w_spatial[l], bb[l], conv_full[l],
                       group_norm_g[l:l + 1])
        x1 = _mm_out(y, wo, xl)
        h2 = _rmsnorm_fwd(x1, norm2_g[l:l + 1])
        gate, up, act = _mm_swiglu(h2, wg, wu)
        x2 = _mm_down(act, wd, x1)
        saved.append(dict(x=xl, h=h, z=z, y=y, x1=x1, h2=h2, gate=gate, up=up, act=act))
        xl = x2

    dx, dxb, d_final_g, loss_part = _loss_head(xl, final_norm_g.reshape(1, d), target)
    small = [None] * nl
    lands = None
    for l in reversed(range(nl)):
        wi, wo, wg, wu, wd = gathered[l][:5]
        wo = wo.reshape(-1, d)
        s = saved[l]
        dgate, dup = _mm_dact(dxb, wd, s["gate"], s["up"])
        dw_down = _mm_dw_down(s["act"], dxb)
        dh2 = _mm_dh2(dgate, dup, wg, wu)
        dw_gate, dw_up = _mm_dw_cols("mm_dw_gate_up", s["h2"], [dgate, dup])
        dx1, dx1b, d_n2 = _rmsnorm_bwd(s["x1"], norm2_g[l:l + 1], dh2, dx)
        dy = _mm_dy(dx1b, wo)
        dw_out = _mm_dw_out(s["y"], dx1b).reshape(N_DEV, -1, d)
        dz, d_lng, d_lnb, d_ws, d_bb, d_cw, d_gg = _mixer_bwd(
            s["z"], dy, gmlp_ln_g[l:l + 1], gmlp_ln_b[l:l + 1], w_spatial[l], bb[l], conv_full[l], group_norm_g[l:l + 1])
        dh = _mm_dh(dz, wi)
        dw_in = _mm_dw_in(s["h"], dz)
        dx, dxb, d_n1 = _rmsnorm_bwd(s["x"], norm1_g[l:l + 1], dh, dx1)
        lands = _reduce_scatter_send(f"reduce_scatter_l{l}", [dw_in, dw_out, dw_gate, dw_up, dw_down], l, lands)
        small[l] = dict(norm1_g=d_n1, gmlp_ln_g=d_lng, gmlp_ln_b=d_lnb, w_spatial=d_ws, b_spatial=d_bb[:, :, 0],
                        group_norm_g=d_gg, norm2_g=d_n2, conv_w=d_cw[0:3])
    grad_x = dx.reshape(x.shape)

    rep = ["norm1_g", "gmlp_ln_g", "gmlp_ln_b", "w_spatial", "b_spatial", "group_norm_g", "norm2_g"]
    rep_w = dict(norm1_g=norm1_g, gmlp_ln_g=gmlp_ln_g, gmlp_ln_b=gmlp_ln_b, w_spatial=w_spatial, b_spatial=b_spatial,
                 group_norm_g=group_norm_g, norm2_g=norm2_g)
    rep_m = dict(norm1_g=m_norm1_g, gmlp_ln_g=m_gmlp_ln_g, gmlp_ln_b=m_gmlp_ln_b, w_spatial=m_w_spatial,
                 b_spatial=m_b_spatial, group_norm_g=m_group_norm_g, norm2_g=m_norm2_g)
    rep_v = dict(norm1_g=v_norm1_g, gmlp_ln_g=v_gmlp_ln_g, gmlp_ln_b=v_gmlp_ln_b, w_spatial=v_w_spatial,
                 b_spatial=v_b_spatial, group_norm_g=v_group_norm_g, norm2_g=v_norm2_g)
    parts = [_rows(jnp.stack([small[l][k].reshape(rep_w[k].shape[1:]) for l in range(nl)])) for k in rep]
    parts.append(_rows(d_final_g))
    parts.append(_rows(jnp.stack([small[l]["conv_w"] for l in range(nl)])))
    parts.append(jnp.broadcast_to(loss_part, (8, 128)))
    sizes = [p.shape[0] for p in parts]
    total = _all_reduce_small(jnp.concatenate(parts, axis=0))
    offs = [0]
    for n in sizes:
        offs.append(offs[-1] + n)
    pieces = [total[offs[i]:offs[i + 1]] for i in range(len(parts))]
    loss = pieces[-1][0, 0]
    conv_g_full = pieces[-2].reshape(nl, 3, N_DEV, hd)
    conv_g = lax.dynamic_index_in_dim(conv_g_full, me, axis=2, keepdims=False)
    n_rep = offs[len(rep) + 1]
    pad = jnp.zeros((2, 128), F32)

    def small_pack(named, final, conv):
        return jnp.concatenate([_rows(named[k]) for k in rep] + [_rows(final), _rows(conv), pad], axis=0)

    g_small = jnp.concatenate([total[:n_rep], _rows(conv_g), pad], axis=0)
    d_small, m_small, v_small = _adamw_small(
        g_small, small_pack(rep_w, final_norm_g, conv_w), small_pack(rep_m, m_final_norm_g, m_conv_w),
        small_pack(rep_v, v_final_norm_g, v_conv_w))

    def unpack(packed):
        out = {k: packed[offs[i]:offs[i + 1]].reshape(rep_w[k].shape) for i, k in enumerate(rep)}
        out["final_norm_g"] = packed[offs[len(rep)]:n_rep].reshape(final_norm_g.shape)
        out["conv_w"] = packed[n_rep:n_rep + 6].reshape(conv_w.shape)
        return out

    res = {"grad": unpack(g_small), "delta": unpack(d_small), "m": unpack(m_small), "v": unpack(v_small)}

    big_m = {"w_in": m_w_in, "w_out": m_w_out, "w_gate": m_w_gate, "w_up": m_w_up, "w_down": m_w_down}
    big_v = {"w_in": v_w_in, "w_out": v_w_out, "w_gate": v_w_gate, "w_up": v_w_up, "w_down": v_w_down}
    for i, k in enumerate(big):
        g_k, d_k, m_k, v_k = _adamw_big(f"adamw_{k}", lands[i], big[k], big_m[k], big_v[k])
        res["grad"][k], res["delta"][k], res["m"][k], res["v"][k] = g_k, d_k, m_k, v_k

    order = ["norm1_g", "w_in", "gmlp_ln_g", "gmlp_ln_b", "w_spatial", "b_spatial", "conv_w", "group_norm_g", "w_out",
             "norm2_g", "w_gate", "w_up", "w_down", "final_norm_g"]
    return (loss, grad_x, *[res["grad"][k] for k in order], *[res["delta"][k] for k in order],
            *[res["m"][k] for k in order], *[res["v"][k] for k in order])
```

```python
import functools
import math
import operator

import jax
import jax.numpy as jnp
from jax import lax
from jax.experimental import pallas as pl
from jax.experimental.pallas import tpu as pltpu

F32 = jnp.float32
BF16 = jnp.bfloat16
MESH = pl.DeviceIdType.MESH

N_DEV = 8
N_LAYERS = 2
HEADS = 8
BLK = 128
CHUNK = 64
HALO = 16
RMS_EPS = 1e-6
LN_EPS = 1e-5
ADAM_LR, ADAM_B1, ADAM_B2, ADAM_EPS, ADAM_WD, ADAM_STEP = 0.001, 0.9, 0.999, 1e-8, 0.01, 10
GELU_C = math.sqrt(2.0 / math.pi)
GELU_A = 0.044715

VMEM_LIMIT_V7X = 56 * 1024 * 1024
_TM = 1024
_TN = 1024
_TT = 1024
_TM_MIX = 256
_TM_NORM = 512

RELATIONS = [(0, 0, 1), (0, 1, 0), (0, 1, 1), (1, 0, 0), (1, 0, 1), (1, 1, 0), (1, 1, 1)]


def _cparams(n_axes):
    return pltpu.CompilerParams(dimension_semantics=("arbitrary",) * n_axes, vmem_limit_bytes=VMEM_LIMIT_V7X)


def _sds(shape, dtype):
    return jax.ShapeDtypeStruct(tuple(shape), dtype)


def _place():
    return lax.axis_index("x"), lax.axis_index("y"), lax.axis_index("c")


def _index(place):
    return 4 * place[0] + 2 * place[1] + place[2]


class _Piece:
    def __init__(self, operands, out_shapes, aliases, n_sems, start, mid, finish):
        self.operands, self.out_shapes, self.aliases, self.n_sems = list(operands), list(out_shapes), dict(aliases), n_sems
        self.start, self.mid, self.finish = start, mid, finish


class _Ctx:
    def __init__(self, ins, outs, sems, offs):
        self.ins, self.outs, self.sems = ins, outs, sems
        self.o_in, self.o_out, self.o_send, self.o_recv, self.o_loc = offs

    def inp(self, i):
        return self.ins[self.o_in + i]

    def out(self, i):
        return self.outs[self.o_out + i]

    def send(self, k):
        return self.sems[0].at[self.o_send + k]

    def recv(self, k):
        return self.sems[1].at[self.o_recv + k]

    def local(self, k):
        return self.sems[2].at[self.o_loc + k]


class _Hosted:
    def __init__(self, pieces, n_in_before, n_out_before):
        self.pieces = [p for p in (pieces or []) if p is not None]
        self.operands, self.out_shapes, self.aliases, self.offs = [], [], {}, []
        counts = [0, 0, 0]
        for p in self.pieces:
            self.offs.append((len(self.operands), len(self.out_shapes), *counts))
            for i, j in p.aliases.items():
                self.aliases[n_in_before + len(self.operands) + i] = n_out_before + len(self.out_shapes) + j
            self.operands += p.operands
            self.out_shapes += p.out_shapes
            counts = [c + n for c, n in zip(counts, p.n_sems)]
        hbm = pl.BlockSpec(memory_space=pl.ANY)
        self.in_specs = [hbm] * len(self.operands)
        self.out_specs = [hbm] * len(self.out_shapes)
        self.scratch = [pltpu.SemaphoreType.DMA((max(c, 1),)) for c in counts] if self.pieces else []

    def run(self, stage, ins, outs, sems):
        for p, offs in zip(self.pieces, self.offs):
            getattr(p, stage)(_Ctx(ins, outs, sems, offs))

    def wrap(self, grid, compute, ins, outs, sems):
        if not self.pieces:
            compute()
            return
        n_steps = math.prod(grid)
        lin = 0
        for ax, g in enumerate(grid):
            lin = lin * g + pl.program_id(ax)
        pl.when(lin == 0)(lambda: self.run("start", ins, outs, sems))
        compute()
        pl.when(lin == max(n_steps - 3, 0))(lambda: self.run("mid", ins, outs, sems))
        pl.when(lin == n_steps - 1)(lambda: self.run("finish", ins, outs, sems))


def _cols_view(width):
    return lambda ref, p: ref.at[:, pl.ds(pl.multiple_of(p * width, 128), width)]


def _rows_view(height):
    return lambda ref, p: ref.at[pl.ds(pl.multiple_of(p * height, 16), height), :]


def _slot_view(ref, p):
    return ref.at[p]


def _ag_piece(specs):
    n = len(specs)

    def plan(ctx):
        x, y, c = _place()
        me, sib = (x, y, c), (x, y, 1 - c)
        chips = [(1 - x, y), (x, 1 - y), (1 - x, 1 - y)]

        def copy(a, k, block, to, src=None):
            dst = specs[a][2](ctx.out(a), _index(block))
            return pltpu.make_async_remote_copy(
                src_ref=dst if src is None else src, dst_ref=dst, send_sem=ctx.send(7 * a + k),
                recv_sem=ctx.recv(7 * a + k), device_id=to, device_id_type=MESH)

        return me, sib, chips, c, copy

    def start(ctx):
        me, sib, chips, c, copy = plan(ctx)
        for a in range(n):
            pltpu.make_async_copy(ctx.inp(a), specs[a][2](ctx.out(a), _index(me)), ctx.local(a)).start()
            copy(a, 0, me, sib, src=ctx.inp(a)).start()
            for j, chip in enumerate(chips):
                copy(a, 1 + j, me, (*chip, c), src=ctx.inp(a)).start()

    def mid(ctx):
        me, sib, chips, c, copy = plan(ctx)
        for a in range(n):
            for j, chip in enumerate(chips):
                copy(a, 1 + j, (*chip, c), me).wait_recv()
                copy(a, 4 + j, (*chip, c), sib).start()

    def finish(ctx):
        me, sib, chips, c, copy = plan(ctx)
        for a in range(n):
            copy(a, 0, sib, me).wait_recv()
            for j, chip in enumerate(chips):
                copy(a, 4 + j, (*chip, 1 - c), me).wait_recv()
        for a in range(n):
            copy(a, 0, me, sib, src=ctx.inp(a)).wait_send()
            for j, chip in enumerate(chips):
                copy(a, 1 + j, me, (*chip, c), src=ctx.inp(a)).wait_send()
                copy(a, 4 + j, (*chip, c), sib).wait_send()
            pltpu.make_async_copy(ctx.inp(a), specs[a][2](ctx.out(a), _index(me)), ctx.local(a)).wait()

    return _Piece([s[0] for s in specs], [s[1] for s in specs], {}, (7 * n, 7 * n, n), start, mid, finish)


def _rs_piece(specs, layer, lands):
    n = len(specs)

    def copies(ctx):
        x, y, c = _place()
        me = _index((x, y, c))
        out = []
        for a in range(n):
            g, land = ctx.inp(a), ctx.out(a)
            view = specs[a][2]
            local = pltpu.make_async_copy(view(g, me), land.at[layer, me], ctx.local(a))
            for k, (dx, dy, dc) in enumerate(RELATIONS):
                peer = (x ^ dx, y ^ dy, c ^ dc)
                send = pltpu.make_async_remote_copy(
                    src_ref=view(g, _index(peer)), dst_ref=land.at[layer, me], send_sem=ctx.send(7 * a + k),
                    recv_sem=ctx.recv(7 * a + k), device_id=peer, device_id_type=MESH)
                recv = pltpu.make_async_remote_copy(
                    src_ref=view(g, _index(peer)), dst_ref=land.at[layer, _index(peer)], send_sem=ctx.send(7 * a + k),
                    recv_sem=ctx.recv(7 * a + k), device_id=peer, device_id_type=MESH)
                out.append((send, recv))
            out.append((local, None))
        return out

    def start(ctx):
        for send, _ in copies(ctx):
            send.start()

    def finish(ctx):
        for send, recv in copies(ctx):
            if recv is None:
                send.wait()
            else:
                recv.wait_recv()
                send.wait_send()

    operands = [s[0] for s in specs] + (list(lands) if lands is not None else [])
    aliases = {n + a: a for a in range(n)} if lands is not None else {}
    return _Piece(operands, [s[1] for s in specs], aliases, (7 * n, 7 * n, n), start, lambda ctx: None, finish)


def _comm_only(name, pieces):
    hosted = _Hosted(pieces, 0, 0)
    n_in, n_out = len(hosted.operands), len(hosted.out_shapes)

    def body(*refs):
        ins, outs, sems = refs[:n_in], refs[n_in:n_in + n_out], refs[n_in + n_out:]
        for stage in ("start", "mid", "finish"):
            hosted.run(stage, ins, outs, sems)

    return pl.pallas_call(
        body, name=name, out_shape=tuple(hosted.out_shapes), in_specs=hosted.in_specs, out_specs=tuple(hosted.out_specs),
        input_output_aliases=hosted.aliases, scratch_shapes=hosted.scratch)(*hosted.operands)


def _matmul(name, grid, nk, kaxis, pairs, dims, extras, outs, epilogue, sum_pairs, acc_shape, comm=None):
    n_p, n_e, n_o = len(pairs), len(extras), len(outs)
    n_acc = 0 if nk == 1 else (1 if sum_pairs else n_p)
    n_in = 2 * n_p + n_e
    hosted = _Hosted(comm, n_in, n_o)
    n_ci, n_co = len(hosted.operands), len(hosted.out_shapes)

    def body(*refs):
        a_refs = refs[0:2 * n_p:2]
        b_refs = refs[1:2 * n_p:2]
        e_refs = refs[2 * n_p:n_in]
        c_ins = refs[n_in:n_in + n_ci]
        o_refs = refs[n_in + n_ci:n_in + n_ci + n_o]
        c_outs = refs[n_in + n_ci + n_o:n_in + n_ci + n_o + n_co]
        acc_refs = refs[n_in + n_ci + n_o + n_co:n_in + n_ci + n_o + n_co + n_acc]
        sems = refs[n_in + n_ci + n_o + n_co + n_acc:]

        def dots():
            prods = [lax.dot_general(a[...], b[...], (dims, ((), ())), preferred_element_type=F32)
                     for a, b in zip(a_refs, b_refs)]
            if sum_pairs and n_p > 1:
                prods = [functools.reduce(operator.add, prods)]
            return prods

        def compute():
            if nk == 1:
                epilogue(dots(), e_refs, o_refs)
                return
            k = pl.program_id(kaxis)

            @pl.when(k == 0)
            def _():
                for acc, p in zip(acc_refs, dots()):
                    acc[...] = p

            if nk > 2:
                @pl.when((k > 0) & (k < nk - 1))
                def _():
                    for acc, p in zip(acc_refs, dots()):
                        acc[...] += p

            @pl.when(k == nk - 1)
            def _():
                epilogue([acc[...] + p for acc, p in zip(acc_refs, dots())], e_refs, o_refs)

        hosted.wrap(grid, compute, c_ins, c_outs, sems)

    operands, in_specs = [], []
    for a, a_spec, b, b_spec in pairs:
        operands += [a, b]
        in_specs += [a_spec, b_spec]
    for e, e_spec in extras:
        operands.append(e)
        in_specs.append(e_spec)
    res = pl.pallas_call(
        body, name=name, grid=grid,
        out_shape=tuple([o for o, _ in outs] + hosted.out_shapes),
        in_specs=in_specs + hosted.in_specs, out_specs=tuple([s for _, s in outs] + hosted.out_specs),
        input_output_aliases=hosted.aliases,
        scratch_shapes=[pltpu.VMEM(acc_shape, F32) for _ in range(n_acc)] + hosted.scratch,
        compiler_params=_cparams(len(grid)),
    )(*operands, *hosted.operands)
    return list(res[:n_o]), list(res[n_o:])


NN = ((1,), (0,))
NT = ((1,), (1,))
TN = ((0,), (0,))


def _tile(n, want):
    if n <= want:
        return n
    t = want // 128 * 128
    while n % t:
        t -= 128
    return t


def _silu_parts(g):
    s = 1.0 / (1.0 + jnp.exp(-g))
    return s, g * s


def _mm_in(h, w_in, comm=None):
    t, d = h.shape
    n = w_in.shape[1]
    tm, tn = _tile(t, _TM), _tile(n, _TN)

    def epi(accs, e, o):
        o[0][...] = accs[0].astype(BF16)

    outs, couts = _matmul(
        "mm_in", (n // tn, t // tm), 1, None,
        [(h, pl.BlockSpec((tm, d), lambda j, i: (i, 0)), w_in, pl.BlockSpec((d, tn), lambda j, i: (0, j)))],
        NN, [], [(_sds((t, n), BF16), pl.BlockSpec((tm, tn), lambda j, i: (i, j)))], epi, True, None, comm)
    return outs[0], couts


def _mm_out(y, w_out, x, comm=None):
    t, m = y.shape
    d = w_out.shape[1]
    tm, tn = _tile(t, _TM), _tile(d, _TN)

    def epi(accs, e, o):
        o[0][...] = e[0][...] + accs[0]

    outs, couts = _matmul(
        "mm_out", (t // tm, d // tn), 1, None,
        [(y, pl.BlockSpec((tm, m), lambda i, j: (i, 0)), w_out, pl.BlockSpec((m, tn), lambda i, j: (0, j)))],
        NN, [(x, pl.BlockSpec((tm, tn), lambda i, j: (i, j)))],
        [(_sds((t, d), F32), pl.BlockSpec((tm, tn), lambda i, j: (i, j)))], epi, True, None, comm)
    return outs[0], couts


def _mm_swiglu(h2, wgt, wut, comm=None):
    t, d = h2.shape
    f = wgt.shape[0]
    tm, tn = _tile(t, _TM), _tile(f, 512)

    def epi(accs, e, o):
        g, u = accs
        _, sg = _silu_parts(g)
        o[0][...] = g.astype(BF16)
        o[1][...] = u.astype(BF16)
        o[2][...] = (sg * u).astype(BF16)

    wspec = pl.BlockSpec((tn, d), lambda j, i: (j, 0))
    hspec = pl.BlockSpec((tm, d), lambda j, i: (i, 0))
    ospec = pl.BlockSpec((tm, tn), lambda j, i: (i, j))
    osh = _sds((t, f), BF16)
    outs, couts = _matmul("mm_swiglu", (f // tn, t // tm), 1, None, [(h2, hspec, wgt, wspec), (h2, hspec, wut, wspec)],
                          NT, [], [(osh, ospec)] * 3, epi, False, None, comm)
    return outs, couts


def _mm_down(act, wd, x1, comm=None):
    t, f = act.shape
    d = wd.shape[1]
    tm, tn = _tile(t, _TM), _tile(d, _TN)
    nk = 2
    tk = f // nk

    def epi(accs, e, o):
        o[0][...] = e[0][...] + accs[0]

    outs, couts = _matmul(
        "mm_down", (t // tm, d // tn, nk), nk, 2,
        [(act, pl.BlockSpec((tm, tk), lambda i, j, k: (i, k)), wd, pl.BlockSpec((tk, tn), lambda i, j, k: (k, j)))],
        NN, [(x1, pl.BlockSpec((tm, tn), lambda i, j, k: (i, j)))],
        [(_sds((t, d), F32), pl.BlockSpec((tm, tn), lambda i, j, k: (i, j)))], epi, True, (tm, tn), comm)
    return outs[0], couts


def _mm_dact(dxb, wd, gate, up, comm=None):
    t, d = dxb.shape
    f = wd.shape[0]
    tm, tn = _tile(t, _TM), _tile(f, 512)

    def epi(accs, e, o):
        da = accs[0]
        g = e[0][...].astype(F32)
        u = e[1][...].astype(F32)
        s, sg = _silu_parts(g)
        o[0][...] = (da * u * (s + sg * (1.0 - s))).astype(BF16)
        o[1][...] = (da * sg).astype(BF16)

    bspec = pl.BlockSpec((tm, tn), lambda j, i: (i, j))
    osh = _sds((t, f), BF16)
    outs, couts = _matmul(
        "mm_dact", (f // tn, t // tm), 1, None,
        [(dxb, pl.BlockSpec((tm, d), lambda j, i: (i, 0)), wd, pl.BlockSpec((tn, d), lambda j, i: (j, 0)))],
        NT, [(gate, bspec), (up, bspec)], [(osh, bspec)] * 2, epi, True, None, comm)
    return outs, couts


def _mm_dh2(dgate, dup, wgt, wut, comm=None):
    t, f = dgate.shape
    d = wgt.shape[1]
    tm, tn = _tile(t, _TM), _tile(d, _TN)
    nk = 4
    tk = f // nk

    def epi(accs, e, o):
        o[0][...] = accs[0]

    aspec = pl.BlockSpec((tm, tk), lambda i, j, k: (i, k))
    wspec = pl.BlockSpec((tk, tn), lambda i, j, k: (k, j))
    outs, couts = _matmul("mm_dh2", (t // tm, d // tn, nk), nk, 2, [(dgate, aspec, wgt, wspec), (dup, aspec, wut, wspec)],
                          NN, [], [(_sds((t, d), F32), pl.BlockSpec((tm, tn), lambda i, j, k: (i, j)))], epi, True,
                          (tm, tn), comm)
    return outs[0], couts


def _mm_dw(name, a_list, b, tmo, tno, comm=None):
    t, m = a_list[0].shape
    n = b.shape[1]
    tt = _tile(t, _TT)
    nk = t // tt
    tmo, tno = _tile(m, tmo), _tile(n, tno)

    def epi(accs, e, o):
        for acc, out in zip(accs, o):
            out[...] = acc.astype(BF16)

    aspec = pl.BlockSpec((tt, tmo), lambda i, j, k: (k, i))
    bspec = pl.BlockSpec((tt, tno), lambda i, j, k: (k, j))
    ospec = pl.BlockSpec((tmo, tno), lambda i, j, k: (i, j))
    if nk == 1:
        return _matmul(name, (m // tmo, n // tno, 1), 1, None, [(a, aspec, b, bspec) for a in a_list], TN, [],
                       [(_sds((m, n), BF16), ospec)] * len(a_list), epi, False, None, comm)
    return _matmul(name, (m // tmo, n // tno, nk), nk, 2, [(a, aspec, b, bspec) for a in a_list], TN, [],
                   [(_sds((m, n), BF16), ospec)] * len(a_list), epi, False, (tmo, tno), comm)


def _mm_dy(dxb, w_out, comm=None):
    t, d = dxb.shape
    m = w_out.shape[0]
    tm, tn = _tile(t, _TM), _tile(m, _TN)

    def epi(accs, e, o):
        o[0][...] = accs[0].astype(BF16)

    outs, couts = _matmul(
        "mm_dy", (t // tm, m // tn), 1, None,
        [(dxb, pl.BlockSpec((tm, d), lambda i, j: (i, 0)), w_out, pl.BlockSpec((tn, d), lambda i, j: (j, 0)))], NT, [],
        [(_sds((t, m), BF16), pl.BlockSpec((tm, tn), lambda i, j: (i, j)))], epi, True, None, comm)
    return outs[0], couts


def _mm_dh(dz, w_in, comm=None):
    t, n = dz.shape
    d = w_in.shape[0]
    tm, tn = _tile(t, _TM), _tile(d, _TN)
    nk = 2
    tk = n // nk

    def epi(accs, e, o):
        o[0][...] = accs[0]

    outs, couts = _matmul(
        "mm_dh", (t // tm, d // tn, nk), nk, 2,
        [(dz, pl.BlockSpec((tm, tk), lambda i, j, k: (i, k)), w_in, pl.BlockSpec((tn, tk), lambda i, j, k: (j, k)))], NT,
        [], [(_sds((t, d), F32), pl.BlockSpec((tm, tn), lambda i, j, k: (i, j)))], epi, True, (tm, tn), comm)
    return outs[0], couts


def _rmsnorm_fwd(x, g):
    t, d = x.shape
    tm = min(_TM_NORM, t)

    def body(x_ref, g_ref, o_ref):
        xv = x_ref[...]
        rs = lax.rsqrt(jnp.mean(xv * xv, axis=-1, keepdims=True) + RMS_EPS)
        o_ref[...] = (xv * rs * g_ref[...]).astype(BF16)

    return pl.pallas_call(
        body, name="rmsnorm_fwd", grid=(t // tm,), out_shape=_sds((t, d), BF16),
        in_specs=[pl.BlockSpec((tm, d), lambda i: (i, 0)), pl.BlockSpec((1, d), lambda i: (0, 0))],
        out_specs=pl.BlockSpec((tm, d), lambda i: (i, 0)), compiler_params=_cparams(1))(x, g)


def _rmsnorm_bwd_math(xv, g, dh):
    rs = lax.rsqrt(jnp.mean(xv * xv, axis=-1, keepdims=True) + RMS_EPS)
    xh = xv * rs
    gd = dh * g
    dx = rs * (gd - xh * jnp.mean(gd * xh, axis=-1, keepdims=True))
    return dx, jnp.sum(dh * xh, axis=0, keepdims=True)


def _rmsnorm_bwd(x, g, dh, dres):
    t, d = x.shape
    tm = min(_TM_NORM, t)

    def body(x_ref, g_ref, dh_ref, dres_ref, dx_ref, dxb_ref, dg_ref):
        dx, dg = _rmsnorm_bwd_math(x_ref[...], g_ref[...], dh_ref[...])
        dx = dx + dres_ref[...]
        dx_ref[...] = dx
        dxb_ref[...] = dx.astype(BF16)

        @pl.when(pl.program_id(0) == 0)
        def _():
            dg_ref[...] = dg

        @pl.when(pl.program_id(0) > 0)
        def _():
            dg_ref[...] += dg

    row = pl.BlockSpec((tm, d), lambda i: (i, 0))
    vec = pl.BlockSpec((1, d), lambda i: (0, 0))
    return pl.pallas_call(
        body, name="rmsnorm_bwd", grid=(t // tm,),
        out_shape=(_sds((t, d), F32), _sds((t, d), BF16), _sds((1, d), F32)),
        in_specs=[row, vec, row, row], out_specs=(row, row, vec), compiler_params=_cparams(1))(x, g, dh, dres)


def _loss_head(x, g, target):
    t, d = x.shape
    tm = min(_TM_NORM, t)

    def body(x_ref, g_ref, t_ref, dx_ref, dxb_ref, dg_ref, loss_ref):
        xv, gv = x_ref[...], g_ref[...]
        rs = lax.rsqrt(jnp.mean(xv * xv, axis=-1, keepdims=True) + RMS_EPS)
        diff = xv * rs * gv - t_ref[...]
        part = 0.5 * jnp.sum(jnp.mean(diff * diff, axis=-1, keepdims=True), axis=0, keepdims=True)
        part = jnp.broadcast_to(part, (1, 128))
        dx, dg = _rmsnorm_bwd_math(xv, gv, diff * (1.0 / d))
        dx_ref[...] = dx
        dxb_ref[...] = dx.astype(BF16)

        @pl.when(pl.program_id(0) == 0)
        def _():
            dg_ref[...] = dg
            loss_ref[...] = part

        @pl.when(pl.program_id(0) > 0)
        def _():
            dg_ref[...] += dg
            loss_ref[...] += part

    row = pl.BlockSpec((tm, d), lambda i: (i, 0))
    vec = pl.BlockSpec((1, d), lambda i: (0, 0))
    return pl.pallas_call(
        body, name="loss_head", grid=(t // tm,),
        out_shape=(_sds((t, d), F32), _sds((t, d), BF16), _sds((1, d), F32), _sds((1, 128), F32)),
        in_specs=[row, vec, row], out_specs=(row, row, vec, pl.BlockSpec((1, 128), lambda i: (0, 0))),
        compiler_params=_cparams(1))(x, g, target)


def _gelu(x):
    th = jnp.tanh(GELU_C * (x + GELU_A * x * x * x))
    return 0.5 * x * (1.0 + th), th


def _gelu_grad(x, th):
    return 0.5 * (1.0 + th) + 0.5 * x * (1.0 - th * th) * GELU_C * (1.0 + 3.0 * GELU_A * x * x)


def _masked_ws(ws_ref, h):
    i = lax.broadcasted_iota(jnp.int32, (BLK, BLK), 0) // CHUNK
    j = lax.broadcasted_iota(jnp.int32, (BLK, BLK), 1) // CHUNK
    return jnp.where(j <= i, ws_ref[h], 0.0)


def _shift_down(q, n, first_rows):
    rolled = pltpu.roll(q, n, 0)
    row = lax.broadcasted_iota(jnp.int32, q.shape, 0)
    for r, val in enumerate(first_rows):
        rolled = jnp.where(row == r, val, rolled)
    return rolled


def _shift_up(q, n, last_rows):
    tm = q.shape[0]
    rolled = pltpu.roll(q, tm - n, 0)
    row = lax.broadcasted_iota(jnp.int32, q.shape, 0)
    for r, val in enumerate(last_rows):
        rolled = jnp.where(row == tm - n + r, val, rolled)
    return rolled


def _mixer_specs(t, a, tm):
    hb = tm // HALO
    last = t // HALO - 1
    tile = pl.BlockSpec((tm, 5 * a), lambda i: (i, 0))
    prev = [pl.BlockSpec((HALO, a), functools.partial(lambda i, col: (jnp.maximum(i * hb - 1, 0), col), col=col))
            for col in (3, 4)]
    nxt = [pl.BlockSpec((HALO, a), functools.partial(lambda i, col: (jnp.minimum((i + 1) * hb, last), col), col=col))
           for col in (2, 3, 4)]
    return tile, prev, nxt


def _group_a_fwd(zu, zv, lng, lnb, ws_ref, bb_ref, mixed_ref, vln_ref):
    u, thu = _gelu(zu)
    v, thv = _gelu(zv)
    mu = jnp.mean(v, axis=-1, keepdims=True)
    vc = v - mu
    rs = lax.rsqrt(jnp.mean(vc * vc, axis=-1, keepdims=True) + LN_EPS)
    vhat = vc * rs
    vln_ref[...] = vhat * lng + lnb
    tm, a = zu.shape
    hd = a // HEADS
    for h in range(HEADS):
        w = _masked_ws(ws_ref, h).astype(BF16)
        for b in range(tm // BLK):
            rows, cols = pl.ds(b * BLK, BLK), pl.ds(h * hd, hd)
            mixed_ref[rows, cols] = jnp.dot(w, vln_ref[rows, cols].astype(BF16), preferred_element_type=F32) + bb_ref[h]
    return u, thu, thv, rs, vhat


def _mixer_fwd(z, ln_g, ln_b, w_spatial, bb, conv_w, gg):
    t = z.shape[0]
    a = z.shape[1] // 5
    tm = min(_TM_MIX, t)
    tile, prev, _ = _mixer_specs(t, a, tm)

    def body(z_ref, pc_ref, ph_ref, lng_ref, lnb_ref, ws_ref, bb_ref, cw_ref, gg_ref, y_ref, mixed_ref, vln_ref):
        i = pl.program_id(0)
        zu = z_ref[:, 0:a].astype(F32)
        zv = z_ref[:, a:2 * a].astype(F32)
        u, _, _, _, _ = _group_a_fwd(zu, zv, lng_ref[...], lnb_ref[...], ws_ref, bb_ref, mixed_ref, vln_ref)
        ya = u * mixed_ref[...]
        ra = lax.rsqrt(jnp.mean(ya * ya, axis=-1, keepdims=True) + RMS_EPS)
        y_ref[:, 0:a] = (ya * ra * gg_ref[:, 0:a]).astype(BF16)

        zb = z_ref[:, 2 * a:3 * a].astype(F32)
        q = z_ref[:, 3 * a:4 * a].astype(F32) * z_ref[:, 4 * a:5 * a].astype(F32)
        qp = jnp.where(i > 0, pc_ref[...].astype(F32) * ph_ref[...].astype(F32), 0.0)
        qm1 = _shift_down(q, 1, [qp[HALO - 1:HALO]])
        qm2 = _shift_down(q, 2, [qp[HALO - 2:HALO - 1], qp[HALO - 1:HALO]])
        cv = cw_ref[0:1, :] * qm2 + cw_ref[1:2, :] * qm1 + cw_ref[2:3, :] * q
        yb = zb * cv
        rb = lax.rsqrt(jnp.mean(yb * yb, axis=-1, keepdims=True) + RMS_EPS)
        y_ref[:, a:2 * a] = (yb * rb * gg_ref[:, a:2 * a]).astype(BF16)

    full = lambda shape: pl.BlockSpec(shape, lambda i: (0,) * len(shape))
    return pl.pallas_call(
        body, name="mixer_fwd", grid=(t // tm,), out_shape=_sds((t, 2 * a), BF16),
        in_specs=[tile, *prev, full((1, a)), full((1, a)), full(w_spatial.shape), full(bb.shape), full(conv_w.shape),
                  full((1, 2 * a))],
        out_specs=pl.BlockSpec((tm, 2 * a), lambda i: (i, 0)),
        scratch_shapes=[pltpu.VMEM((tm, a), F32), pltpu.VMEM((tm, a), F32)],
        compiler_params=_cparams(1))(z, z, z, ln_g, ln_b, w_spatial, bb, conv_w, gg)


def _mixer_bwd(z, dy, ln_g, ln_b, w_spatial, bb, conv_w, gg):
    t = z.shape[0]
    a = z.shape[1] // 5
    hd = a // HEADS
    tm = min(_TM_MIX, t)
    n_tiles = t // tm
    tile, prev, nxt = _mixer_specs(t, a, tm)
    hb = tm // HALO
    dy_tile = pl.BlockSpec((tm, 2 * a), lambda i: (i, 0))
    dy_next = pl.BlockSpec((HALO, a), lambda i: (jnp.minimum((i + 1) * hb, t // HALO - 1), 1))

    def body(z_ref, pc_ref, ph_ref, nb_ref, nc_ref, nh_ref, dy_ref, ndy_ref, lng_ref, lnb_ref, ws_ref, bb_ref, cw_ref,
             gg_ref, dz_ref, dlng_ref, dlnb_ref, dws_ref, dbb_ref, dcw_ref, dgg_ref, mixed_ref, vln_ref, dmix_ref,
             dvln_ref):
        i = pl.program_id(0)

        @pl.when(i == 0)
        def _():
            for ref in (dlng_ref, dlnb_ref, dws_ref, dbb_ref, dcw_ref, dgg_ref):
                ref[...] = jnp.zeros(ref.shape, F32)

        lng = lng_ref[...]
        zu = z_ref[:, 0:a].astype(F32)
        zv = z_ref[:, a:2 * a].astype(F32)
        u, thu, thv, rs, vhat = _group_a_fwd(zu, zv, lng, lnb_ref[...], ws_ref, bb_ref, mixed_ref, vln_ref)
        mixed = mixed_ref[...]
        ya = u * mixed
        ra = lax.rsqrt(jnp.mean(ya * ya, axis=-1, keepdims=True) + RMS_EPS)
        da = dy_ref[:, 0:a].astype(F32)
        yah = ya * ra
        dgg_ref[:, 0:a] += jnp.sum(da * yah, axis=0, keepdims=True)
        ga = da * gg_ref[:, 0:a]
        dya = ra * (ga - yah * jnp.mean(ga * yah, axis=-1, keepdims=True))
        dz_ref[:, 0:a] = (dya * mixed * _gelu_grad(zu, thu)).astype(BF16)
        dmix_ref[...] = dya * u
        for h in range(HEADS):
            w = _masked_ws(ws_ref, h).astype(BF16)
            dw = jnp.zeros((BLK, BLK), F32)
            db = jnp.zeros((BLK, hd), F32)
            for b in range(tm // BLK):
                rows, cols = pl.ds(b * BLK, BLK), pl.ds(h * hd, hd)
                dm = dmix_ref[rows, cols]
                dmb = dm.astype(BF16)
                db = db + dm
                dw = dw + lax.dot_general(dmb, vln_ref[rows, cols].astype(BF16), (NT, ((), ())),
                                          preferred_element_type=F32)
                dvln_ref[rows, cols] = lax.dot_general(w, dmb, (TN, ((), ())), preferred_element_type=F32)
            dws_ref[h] += dw
            dbb_ref[h] += db
        dvln = dvln_ref[...]
        dlng_ref[...] += jnp.sum(dvln * vhat, axis=0, keepdims=True)
        dlnb_ref[...] += jnp.sum(dvln, axis=0, keepdims=True)
        dvh = dvln * lng
        dv = rs * (dvh - jnp.mean(dvh, axis=-1, keepdims=True) - vhat * jnp.mean(dvh * vhat, axis=-1, keepdims=True))
        dz_ref[:, a:2 * a] = (dv * _gelu_grad(zv, thv)).astype(BF16)

        w0, w1, w2 = cw_ref[0:1, :], cw_ref[1:2, :], cw_ref[2:3, :]
        ggb = gg_ref[:, a:2 * a]
        zb = z_ref[:, 2 * a:3 * a].astype(F32)
        zc = z_ref[:, 3 * a:4 * a].astype(F32)
        zh = z_ref[:, 4 * a:5 * a].astype(F32)
        q = zc * zh
        qp = jnp.where(i > 0, pc_ref[...].astype(F32) * ph_ref[...].astype(F32), 0.0)
        qm1 = _shift_down(q, 1, [qp[HALO - 1:HALO]])
        qm2 = _shift_down(q, 2, [qp[HALO - 2:HALO - 1], qp[HALO - 1:HALO]])
        cv = w0 * qm2 + w1 * qm1 + w2 * q

        def conv_out_grad(zb_, cv_, dout_):
            yb = zb_ * cv_
            rb = lax.rsqrt(jnp.mean(yb * yb, axis=-1, keepdims=True) + RMS_EPS)
            ybh = yb * rb
            gb = dout_ * ggb
            dyb = rb * (gb - ybh * jnp.mean(gb * ybh, axis=-1, keepdims=True))
            return dyb * zb_, dyb * cv_, ybh

        db_out = dy_ref[:, a:2 * a].astype(F32)
        g, dzb, ybh = conv_out_grad(zb, cv, db_out)
        dgg_ref[:, a:2 * a] += jnp.sum(db_out * ybh, axis=0, keepdims=True)
        dz_ref[:, 2 * a:3 * a] = dzb.astype(BF16)
        qn = nc_ref[...].astype(F32) * nh_ref[...].astype(F32)
        zbn = nb_ref[...].astype(F32)
        cvn = w0 * _shift_down(qn, 2, [q[tm - 2:tm - 1], q[tm - 1:tm]]) + w1 * _shift_down(qn, 1, [q[tm - 1:tm]]) + w2 * qn
        gn, _, _ = conv_out_grad(zbn, cvn, ndy_ref[...].astype(F32))
        gn = jnp.where(i < n_tiles - 1, gn, 0.0)
        dq = w2 * g + w1 * _shift_up(g, 1, [gn[0:1]]) + w0 * _shift_up(g, 2, [gn[0:1], gn[1:2]])
        dz_ref[:, 3 * a:4 * a] = (dq * zh).astype(BF16)
        dz_ref[:, 4 * a:5 * a] = (dq * zc).astype(BF16)
        dcw_ref[0:1, :] += jnp.sum(g * qm2, axis=0, keepdims=True)
        dcw_ref[1:2, :] += jnp.sum(g * qm1, axis=0, keepdims=True)
        dcw_ref[2:3, :] += jnp.sum(g * q, axis=0, keepdims=True)

        @pl.when(i == n_tiles - 1)
        def _():
            for h in range(HEADS):
                dbb_ref[h] = jnp.broadcast_to(jnp.sum(dbb_ref[h], axis=1, keepdims=True), (BLK, hd))
                dws_ref[h] = _masked_ws(dws_ref, h)

    full = lambda shape: pl.BlockSpec(tuple(shape), lambda i: (0,) * len(shape))
    out_shapes = (_sds((t, 5 * a), BF16), _sds((1, a), F32), _sds((1, a), F32), _sds(w_spatial.shape, F32),
                  _sds(bb.shape, F32), _sds((8, a), F32), _sds((1, 2 * a), F32))
    return pl.pallas_call(
        body, name="mixer_bwd", grid=(n_tiles,), out_shape=out_shapes,
        in_specs=[tile, *prev, *nxt, dy_tile, dy_next, full((1, a)), full((1, a)), full(w_spatial.shape), full(bb.shape),
                  full(conv_w.shape), full((1, 2 * a))],
        out_specs=(tile, *[full(s.shape) for s in out_shapes[1:]]),
        scratch_shapes=[pltpu.VMEM((tm, a), F32)] * 4,
        compiler_params=_cparams(1))(z, z, z, z, z, z, dy, dy, ln_g, ln_b, w_spatial, bb, conv_w, gg)


def _all_reduce_small(pack):
    r = pack.shape[0]

    def body(in_ref, out_ref, acc_ref, recv_ref, send_sems, recv_sems):
        x, y, c = _place()
        partners = [(x, y, 1 - c), (1 - x, y, c), (x, 1 - y, c)]
        acc_ref[0] = in_ref[...]
        for s, partner in enumerate(partners):
            cp = pltpu.make_async_remote_copy(
                src_ref=acc_ref.at[s], dst_ref=recv_ref.at[s], send_sem=send_sems.at[s], recv_sem=recv_sems.at[s],
                device_id=partner, device_id_type=MESH)
            cp.start()
            cp.wait()
            if s < 2:
                acc_ref[s + 1] = acc_ref[s] + recv_ref[s]
            else:
                out_ref[...] = acc_ref[s] + recv_ref[s]

    vmem = pl.BlockSpec(memory_space=pltpu.VMEM)
    return pl.pallas_call(
        body, name="all_reduce_small", out_shape=_sds(pack.shape, F32), in_specs=[vmem], out_specs=vmem,
        scratch_shapes=[pltpu.VMEM((3, r, 128), F32), pltpu.VMEM((3, r, 128), F32), pltpu.SemaphoreType.DMA((3,)),
                        pltpu.SemaphoreType.DMA((3,))],
        compiler_params=pltpu.CompilerParams(vmem_limit_bytes=VMEM_LIMIT_V7X),
    )(pack)


def _adamw_math(w, g, m, v):
    m = ADAM_B1 * m + (1.0 - ADAM_B1) * g
    v = ADAM_B2 * v + (1.0 - ADAM_B2) * (g * g)
    m_hat = m / (1.0 - ADAM_B1 ** ADAM_STEP)
    v_hat = v / (1.0 - ADAM_B2 ** ADAM_STEP)
    delta = -ADAM_LR * (m_hat / (jnp.sqrt(v_hat) + ADAM_EPS) + ADAM_WD * w)
    return delta, m, v


def _adamw_big(name, land, w, m, v):
    nl, n_slots, r, c = land.shape
    tr = max(8, min(r, (256 * 640) // c // 8 * 8))
    while r % tr:
        tr -= 8

    def body(land_ref, w_ref, m_ref, v_ref, g_out, d_out, m_out, v_out):
        g = land_ref[0].astype(F32)
        for s in range(1, n_slots):
            g = g + land_ref[s].astype(F32)
        delta, mn, vn = _adamw_math(w_ref[...], g, m_ref[...], v_ref[...])
        g_out[...] = g
        d_out[...] = delta
        m_out[...] = mn
        v_out[...] = vn

    blk = pl.BlockSpec((None, tr, c), lambda l, i: (l, i, 0))
    return pl.pallas_call(
        body, name=name, grid=(nl, r // tr), out_shape=tuple([_sds((nl, r, c), F32)] * 4),
        in_specs=[pl.BlockSpec((None, n_slots, tr, c), lambda l, i: (l, 0, i, 0)), blk, blk, blk],
        out_specs=tuple([blk] * 4), compiler_params=_cparams(2))(land, w, m, v)


def _adamw_small(g, w, m, v):
    def body(g_ref, w_ref, m_ref, v_ref, d_out, m_out, v_out):
        delta, mn, vn = _adamw_math(w_ref[...], g_ref[...], m_ref[...], v_ref[...])
        d_out[...] = delta
        m_out[...] = mn
        v_out[...] = vn

    return pl.pallas_call(body, name="adamw_small", out_shape=tuple([_sds(g.shape, F32)] * 3),
                          compiler_params=pltpu.CompilerParams(vmem_limit_bytes=VMEM_LIMIT_V7X))(g, w, m, v)


def _rows(a):
    return a.reshape(-1, 128)


BIG = ["w_in", "w_out", "w_gate", "w_up", "w_down"]
AG_HOSTS = {
    ("mm_in", 0): [("w_out", 0), ("w_gate", 0)], ("mm_out", 0): [("w_up", 0)],
    ("mm_swiglu", 0): [("w_down", 0), ("w_in", 1)], ("mm_down", 0): [("w_out", 1), ("w_gate", 1)],
    ("mm_in", 1): [("w_up", 1)], ("mm_swiglu", 1): [("w_down", 1)],
}


def kernel(x, norm1_g, w_in, gmlp_ln_g, gmlp_ln_b, w_spatial, b_spatial, conv_w, group_norm_g, w_out, norm2_g, w_gate, w_up, w_down, final_norm_g, loss_target, m_norm1_g, m_w_in, m_gmlp_ln_g, m_gmlp_ln_b, m_w_spatial, m_b_spatial, m_conv_w, m_group_norm_g, m_w_out, m_norm2_g, m_w_gate, m_w_up, m_w_down, m_final_norm_g, v_norm1_g, v_w_in, v_gmlp_ln_g, v_gmlp_ln_b, v_w_spatial, v_b_spatial, v_conv_w, v_group_norm_g, v_w_out, v_norm2_g, v_w_gate, v_w_up, v_w_down, v_final_norm_g):
    nl = N_LAYERS
    t, d = x.shape[1], x.shape[2]
    a = d // 2
    hd = a // HEADS
    xin = x.reshape(t, d)
    target = loss_target.reshape(t, d)
    me = _index(_place())

    tr = lambda w: jnp.transpose(w, (0, 2, 1))
    big = {"w_in": w_in, "w_out": w_out, "w_gate": tr(w_gate), "w_up": tr(w_up), "w_down": w_down}
    big_m = {"w_in": m_w_in, "w_out": m_w_out, "w_gate": tr(m_w_gate), "w_up": tr(m_w_up), "w_down": m_w_down}
    big_v = {"w_in": v_w_in, "w_out": v_w_out, "w_gate": tr(v_w_gate), "w_up": tr(v_w_up), "w_down": v_w_down}
    block = {k: big[k].shape[1:] for k in BIG}
    view = {k: _cols_view(block[k][1]) if k == "w_in" else _rows_view(block[k][0]) for k in BIG}
    full_shape = {k: (block[k][0], N_DEV * block[k][1]) if k == "w_in" else (N_DEV * block[k][0], block[k][1])
                  for k in BIG}

    def ag_spec(k, l):
        return (big[k][l].astype(BF16), _sds(full_shape[k], BF16), view[k])

    first = _comm_only("all_gather_first", [_ag_piece([ag_spec("w_in", 0), (conv_w, _sds((N_DEV, *conv_w.shape), F32),
                                                                             _slot_view)])])
    weights = {("w_in", 0): first[0]}
    conv_full = jnp.transpose(first[1], (1, 2, 0, 3)).reshape(nl, 3, a)
    bb = jnp.broadcast_to(b_spatial[..., None], (nl, HEADS, BLK, hd))

    def hosted(name, l):
        keys = AG_HOSTS.get((name, l), [])
        return keys, ([_ag_piece([ag_spec(k, kl) for k, kl in keys])] if keys else None)

    def landed(keys, couts):
        for key, arr in zip(keys, couts):
            weights[key] = arr

    saved = []
    xl = xin
    for l in range(nl):
        h = _rmsnorm_fwd(xl, norm1_g[l:l + 1])
        keys, comm = hosted("mm_in", l)
        z, couts = _mm_in(h, weights[("w_in", l)], comm)
        landed(keys, couts)
        y = _mixer_fwd(z, gmlp_ln_g[l:l + 1], gmlp_ln_b[l:l + 1], w_spatial[l], bb[l], conv_full[l],
                       group_norm_g[l:l + 1])
        keys, comm = hosted("mm_out", l)
        x1, couts = _mm_out(y, weights[("w_out", l)], xl, comm)
        landed(keys, couts)
        h2 = _rmsnorm_fwd(x1, norm2_g[l:l + 1])
        keys, comm = hosted("mm_swiglu", l)
        (gate, up, act), couts = _mm_swiglu(h2, weights[("w_gate", l)], weights[("w_up", l)], comm)
        landed(keys, couts)
        keys, comm = hosted("mm_down", l)
        x2, couts = _mm_down(act, weights[("w_down", l)], x1, comm)
        landed(keys, couts)
        saved.append(dict(x=xl, h=h, z=z, y=y, x1=x1, h2=h2, gate=gate, up=up, act=act))
        xl = x2

    dx, dxb, d_final_g, loss_part = _loss_head(xl, final_norm_g.reshape(1, d), target)
    small = [None] * nl
    lands = None
    land_shape = {k: _sds((nl, N_DEV, *block[k]), BF16) for k in BIG}
    for l in reversed(range(nl)):
        s = saved[l]
        wi, wo, wgt, wut, wd = [weights[(k, l)] for k in BIG]
        (dgate, dup), _ = _mm_dact(dxb, wd, s["gate"], s["up"])
        (dw_down,), _ = _mm_dw("mm_dw_down", [s["act"]], dxb, 2816, 1024)
        dh2, _ = _mm_dh2(dgate, dup, wgt, wut)
        (dw_gate, dw_up), _ = _mm_dw("mm_dw_gate_up", [dgate, dup], s["h2"], 1408, 1024)
        dx1, dx1b, d_n2 = _rmsnorm_bwd(s["x1"], norm2_g[l:l + 1], dh2, dx)
        dy, _ = _mm_dy(dx1b, wo)
        (dw_out,), _ = _mm_dw("mm_dw_out", [s["y"]], dx1b, 1024, 1024)
        dz, d_lng, d_lnb, d_ws, d_bb, d_cw, d_gg = _mixer_bwd(
            s["z"], dy, gmlp_ln_g[l:l + 1], gmlp_ln_b[l:l + 1], w_spatial[l], bb[l], conv_full[l], group_norm_g[l:l + 1])
        dh, _ = _mm_dh(dz, wi)
        (dw_in,), _ = _mm_dw("mm_dw_in", [s["h"]], dz, 2048, 1024)
        dx, dxb, d_n1 = _rmsnorm_bwd(s["x"], norm1_g[l:l + 1], dh, dx1)
        grads = dict(w_in=dw_in, w_out=dw_out, w_gate=dw_gate, w_up=dw_up, w_down=dw_down)
        lands = _comm_only(f"reduce_scatter_l{l}", [_rs_piece([(grads[k], land_shape[k], view[k]) for k in BIG], l, lands)])
        small[l] = dict(norm1_g=d_n1, gmlp_ln_g=d_lng, gmlp_ln_b=d_lnb, w_spatial=d_ws, b_spatial=d_bb[:, :, 0],
                        group_norm_g=d_gg, norm2_g=d_n2, conv_w=d_cw[0:3])
    grad_x = dx.reshape(x.shape)

    rep = ["norm1_g", "gmlp_ln_g", "gmlp_ln_b", "w_spatial", "b_spatial", "group_norm_g", "norm2_g"]
    rep_w = dict(norm1_g=norm1_g, gmlp_ln_g=gmlp_ln_g, gmlp_ln_b=gmlp_ln_b, w_spatial=w_spatial, b_spatial=b_spatial,
                 group_norm_g=group_norm_g, norm2_g=norm2_g)
    rep_m = dict(norm1_g=m_norm1_g, gmlp_ln_g=m_gmlp_ln_g, gmlp_ln_b=m_gmlp_ln_b, w_spatial=m_w_spatial,
                 b_spatial=m_b_spatial, group_norm_g=m_group_norm_g, norm2_g=m_norm2_g)
    rep_v = dict(norm1_g=v_norm1_g, gmlp_ln_g=v_gmlp_ln_g, gmlp_ln_b=v_gmlp_ln_b, w_spatial=v_w_spatial,
                 b_spatial=v_b_spatial, group_norm_g=v_group_norm_g, norm2_g=v_norm2_g)
    parts = [_rows(jnp.stack([small[l][k].reshape(rep_w[k].shape[1:]) for l in range(nl)])) for k in rep]
    parts.append(_rows(d_final_g))
    parts.append(_rows(jnp.stack([small[l]["conv_w"] for l in range(nl)])))
    parts.append(jnp.broadcast_to(loss_part, (8, 128)))
    sizes = [p.shape[0] for p in parts]
    total = _all_reduce_small(jnp.concatenate(parts, axis=0))
    offs = [0]
    for n in sizes:
        offs.append(offs[-1] + n)
    pieces = [total[offs[i]:offs[i + 1]] for i in range(len(parts))]
    loss = pieces[-1][0, 0]
    conv_g_full = pieces[-2].reshape(nl, 3, N_DEV, a // N_DEV)
    conv_g = lax.dynamic_index_in_dim(conv_g_full, me, axis=2, keepdims=False)
    n_rep = offs[len(rep) + 1]
    pad = jnp.zeros((2, 128), F32)

    def small_pack(named, final, conv):
        return jnp.concatenate([_rows(named[k]) for k in rep] + [_rows(final), _rows(conv), pad], axis=0)

    g_small = jnp.concatenate([total[:n_rep], _rows(conv_g), pad], axis=0)
    d_small, m_small, v_small = _adamw_small(
        g_small, small_pack(rep_w, final_norm_g, conv_w), small_pack(rep_m, m_final_norm_g, m_conv_w),
        small_pack(rep_v, v_final_norm_g, v_conv_w))

    def unpack(packed):
        out = {k: packed[offs[i]:offs[i + 1]].reshape(rep_w[k].shape) for i, k in enumerate(rep)}
        out["final_norm_g"] = packed[offs[len(rep)]:n_rep].reshape(final_norm_g.shape)
        out["conv_w"] = packed[n_rep:n_rep + 6].reshape(conv_w.shape)
        return out

    res = {"grad": unpack(g_small), "delta": unpack(d_small), "m": unpack(m_small), "v": unpack(v_small)}

    for i, k in enumerate(BIG):
        outs = _adamw_big(f"adamw_{k}", lands[i], big[k], big_m[k], big_v[k])
        if k in ("w_gate", "w_up"):
            outs = [tr(o) for o in outs]
        res["grad"][k], res["delta"][k], res["m"][k], res["v"][k] = outs

    order = ["norm1_g", "w_in", "gmlp_ln_g", "gmlp_ln_b", "w_spatial", "b_spatial", "conv_w", "group_norm_g", "w_out",
             "norm2_g", "w_gate", "w_up", "w_down", "final_norm_g"]
    return (loss, grad_x, *[res["grad"][k] for k in order], *[res["delta"][k] for k in order],
            *[res["m"][k] for k in order], *[res["v"][k] for k in order])
```

```python
import functools
import math
import operator

import jax
import jax.numpy as jnp
from jax import lax
from jax.experimental import pallas as pl
from jax.experimental.pallas import tpu as pltpu

F32 = jnp.float32
BF16 = jnp.bfloat16
MESH = pl.DeviceIdType.MESH

N_DEV = 8
N_LAYERS = 2
HEADS = 8
BLK = 128
CHUNK = 64
HALO = 16
RMS_EPS = 1e-6
LN_EPS = 1e-5
ADAM_LR, ADAM_B1, ADAM_B2, ADAM_EPS, ADAM_WD, ADAM_STEP = 0.001, 0.9, 0.999, 1e-8, 0.01, 10
GELU_C = math.sqrt(2.0 / math.pi)
GELU_A = 0.044715

VMEM_LIMIT_V7X = 56 * 1024 * 1024
_TM = 1024
_TN = 1024
_TT = 1024
_TM_MIX = 256
_TM_NORM = 512


def _cparams(n_axes):
    return pltpu.CompilerParams(dimension_semantics=("arbitrary",) * n_axes, vmem_limit_bytes=VMEM_LIMIT_V7X)


def _sds(shape, dtype):
    return jax.ShapeDtypeStruct(tuple(shape), dtype)


def _place():
    return lax.axis_index("x"), lax.axis_index("y"), lax.axis_index("c")


def _index(place):
    return 4 * place[0] + 2 * place[1] + place[2]


class _Piece:
    def __init__(self, operands, out_shapes, aliases, n_sems, start, mid, finish):
        self.operands, self.out_shapes, self.aliases, self.n_sems = list(operands), list(out_shapes), dict(aliases), n_sems
        self.start, self.mid, self.finish = start, mid, finish


class _Ctx:
    def __init__(self, ins, outs, sems, offs):
        self.ins, self.outs, self.sems = ins, outs, sems
        self.o_in, self.o_out, self.o_send, self.o_recv, self.o_loc = offs

    def inp(self, i):
        return self.ins[self.o_in + i]

    def out(self, i):
        return self.outs[self.o_out + i]

    def send(self, k):
        return self.sems[0].at[self.o_send + k]

    def recv(self, k):
        return self.sems[1].at[self.o_recv + k]

    def local(self, k):
        return self.sems[2].at[self.o_loc + k]


class _Hosted:
    def __init__(self, pieces, n_in_before, n_out_before):
        self.pieces = [p for p in (pieces or []) if p is not None]
        self.operands, self.out_shapes, self.aliases, self.offs = [], [], {}, []
        counts = [0, 0, 0]
        for p in self.pieces:
            self.offs.append((len(self.operands), len(self.out_shapes), *counts))
            for i, j in p.aliases.items():
                self.aliases[n_in_before + len(self.operands) + i] = n_out_before + len(self.out_shapes) + j
            self.operands += p.operands
            self.out_shapes += p.out_shapes
            counts = [c + n for c, n in zip(counts, p.n_sems)]
        hbm = pl.BlockSpec(memory_space=pl.ANY)
        self.in_specs = [hbm] * len(self.operands)
        self.out_specs = [hbm] * len(self.out_shapes)
        self.scratch = [pltpu.SemaphoreType.DMA((max(c, 1),)) for c in counts] if self.pieces else []

    def run(self, stage, ins, outs, sems):
        for p, offs in zip(self.pieces, self.offs):
            getattr(p, stage)(_Ctx(ins, outs, sems, offs))

    def wrap(self, grid, compute, ins, outs, sems):
        if not self.pieces:
            compute()
            return
        n_steps = math.prod(grid)
        lin = 0
        for ax, g in enumerate(grid):
            lin = lin * g + pl.program_id(ax)
        pl.when(lin == 0)(lambda: self.run("start", ins, outs, sems))
        compute()
        pl.when(lin == max(n_steps - 3, 0))(lambda: self.run("mid", ins, outs, sems))
        pl.when(lin == n_steps - 1)(lambda: self.run("finish", ins, outs, sems))


def _cols_view(width):
    return lambda ref, p: ref.at[:, pl.ds(pl.multiple_of(p * width, 128), width)]


def _rows_view(height):
    return lambda ref, p: ref.at[pl.ds(pl.multiple_of(p * height, 16), height), :]


def _slot_view(ref, p):
    return ref.at[p]


def _ag_piece(specs):
    n = len(specs)

    def plan(ctx):
        x, y, c = _place()
        me, sib = (x, y, c), (x, y, 1 - c)
        chips = [(1 - x, y), (x, 1 - y), (1 - x, 1 - y)]

        def copy(a, k, block, to, src=None):
            dst = specs[a][2](ctx.out(a), _index(block))
            return pltpu.make_async_remote_copy(
                src_ref=dst if src is None else src, dst_ref=dst, send_sem=ctx.send(7 * a + k),
                recv_sem=ctx.recv(7 * a + k), device_id=to, device_id_type=MESH)

        return me, sib, chips, c, copy

    def start(ctx):
        me, sib, chips, c, copy = plan(ctx)
        for a in range(n):
            pltpu.make_async_copy(ctx.inp(a), specs[a][2](ctx.out(a), _index(me)), ctx.local(a)).start()
            copy(a, 0, me, sib, src=ctx.inp(a)).start()
            for j, chip in enumerate(chips):
                copy(a, 1 + j, me, (*chip, c), src=ctx.inp(a)).start()

    def mid(ctx):
        me, sib, chips, c, copy = plan(ctx)
        for a in range(n):
            for j, chip in enumerate(chips):
                copy(a, 1 + j, (*chip, c), me).wait_recv()
                copy(a, 4 + j, (*chip, c), sib).start()

    def finish(ctx):
        me, sib, chips, c, copy = plan(ctx)
        for a in range(n):
            copy(a, 0, sib, me).wait_recv()
            for j, chip in enumerate(chips):
                copy(a, 4 + j, (*chip, 1 - c), me).wait_recv()
        for a in range(n):
            copy(a, 0, me, sib, src=ctx.inp(a)).wait_send()
            for j, chip in enumerate(chips):
                copy(a, 1 + j, me, (*chip, c), src=ctx.inp(a)).wait_send()
                copy(a, 4 + j, (*chip, c), sib).wait_send()
            pltpu.make_async_copy(ctx.inp(a), specs[a][2](ctx.out(a), _index(me)), ctx.local(a)).wait()

    return _Piece([s[0] for s in specs], [s[1] for s in specs], {}, (7 * n, 7 * n, n), start, mid, finish)


N_CHIPS = 4


def _rs_core_piece(specs):
    n = len(specs)

    def copies(ctx):
        x, y, c = _place()
        out = []
        for a in range(n):
            for q in range(N_CHIPS):
                out.append(pltpu.make_async_remote_copy(
                    src_ref=specs[a][2](ctx.inp(a), 2 * q + (1 - c)), dst_ref=ctx.out(a).at[q],
                    send_sem=ctx.send(N_CHIPS * a + q), recv_sem=ctx.recv(N_CHIPS * a + q), device_id=(x, y, 1 - c),
                    device_id_type=MESH))
        return out

    def start(ctx):
        for cp in copies(ctx):
            cp.start()

    def finish(ctx):
        for cp in copies(ctx):
            cp.wait_recv()
            cp.wait_send()

    return _Piece([s[0] for s in specs], [s[1] for s in specs], {}, (N_CHIPS * n, N_CHIPS * n, 0), start,
                  lambda ctx: None, finish)


def _rs_chip_piece(specs, layer, lands):
    n = len(specs)
    hops = [(1, 0), (0, 1), (1, 1)]

    def copies(ctx):
        x, y, c = _place()
        mine = 2 * x + y
        out = []
        for a in range(n):
            sums, land = ctx.inp(a), ctx.out(a)
            out.append((pltpu.make_async_copy(sums.at[mine], land.at[layer, mine], ctx.local(a)), None))
            for j, (dx, dy) in enumerate(hops):
                px, py = x ^ dx, y ^ dy
                peer = 2 * px + py
                send = pltpu.make_async_remote_copy(
                    src_ref=sums.at[peer], dst_ref=land.at[layer, mine], send_sem=ctx.send(3 * a + j),
                    recv_sem=ctx.recv(3 * a + j), device_id=(px, py, c), device_id_type=MESH)
                recv = pltpu.make_async_remote_copy(
                    src_ref=sums.at[peer], dst_ref=land.at[layer, peer], send_sem=ctx.send(3 * a + j),
                    recv_sem=ctx.recv(3 * a + j), device_id=(px, py, c), device_id_type=MESH)
                out.append((send, recv))
        return out

    def start(ctx):
        for send, _ in copies(ctx):
            send.start()

    def finish(ctx):
        for send, recv in copies(ctx):
            if recv is None:
                send.wait()
            else:
                recv.wait_recv()
                send.wait_send()

    operands = [s[0] for s in specs] + (list(lands) if lands is not None else [])
    aliases = {n + a: a for a in range(n)} if lands is not None else {}
    return _Piece(operands, [s[1] for s in specs], aliases, (3 * n, 3 * n, n), start, lambda ctx: None, finish)


def _chip_sums(name, grad, stage, by_cols, core):
    _, r, c = stage.shape
    tr = r
    while tr * c > 512 * 1024 or r % tr or tr % 16:
        tr -= 16
    n_t = r // tr

    def body(core_ref, g_ref, s_ref, o_ref):
        o_ref[...] = (g_ref[...].astype(F32) + s_ref[...].astype(F32)).astype(BF16)

    if by_cols:
        gspec = pl.BlockSpec((tr, c), lambda q, i, core_ref: (i, 2 * q + core_ref[0]))
    else:
        gspec = pl.BlockSpec((tr, c), lambda q, i, core_ref: ((2 * q + core_ref[0]) * n_t + i, 0))
    sspec = pl.BlockSpec((None, tr, c), lambda q, i, core_ref: (q, i, 0))
    return pl.pallas_call(
        body, name=name, out_shape=_sds(stage.shape, BF16),
        grid_spec=pltpu.PrefetchScalarGridSpec(num_scalar_prefetch=1, grid=(N_CHIPS, n_t), in_specs=[gspec, sspec],
                                               out_specs=sspec),
        compiler_params=_cparams(2))(core, grad, stage)


def _comm_only(name, pieces):
    hosted = _Hosted(pieces, 0, 0)
    n_in, n_out = len(hosted.operands), len(hosted.out_shapes)

    def body(*refs):
        ins, outs, sems = refs[:n_in], refs[n_in:n_in + n_out], refs[n_in + n_out:]
        for stage in ("start", "mid", "finish"):
            hosted.run(stage, ins, outs, sems)

    return pl.pallas_call(
        body, name=name, out_shape=tuple(hosted.out_shapes), in_specs=hosted.in_specs, out_specs=tuple(hosted.out_specs),
        input_output_aliases=hosted.aliases, scratch_shapes=hosted.scratch)(*hosted.operands)


def _matmul(name, grid, nk, kaxis, pairs, dims, extras, outs, epilogue, sum_pairs, acc_shape, comm=None):
    n_p, n_e, n_o = len(pairs), len(extras), len(outs)
    n_acc = 0 if nk == 1 else (1 if sum_pairs else n_p)
    n_in = 2 * n_p + n_e
    hosted = _Hosted(comm, n_in, n_o)
    n_ci, n_co = len(hosted.operands), len(hosted.out_shapes)

    def body(*refs):
        a_refs = refs[0:2 * n_p:2]
        b_refs = refs[1:2 * n_p:2]
        e_refs = refs[2 * n_p:n_in]
        c_ins = refs[n_in:n_in + n_ci]
        o_refs = refs[n_in + n_ci:n_in + n_ci + n_o]
        c_outs = refs[n_in + n_ci + n_o:n_in + n_ci + n_o + n_co]
        acc_refs = refs[n_in + n_ci + n_o + n_co:n_in + n_ci + n_o + n_co + n_acc]
        sems = refs[n_in + n_ci + n_o + n_co + n_acc:]

        def dots():
            prods = [lax.dot_general(a[...], b[...], (dims, ((), ())), preferred_element_type=F32)
                     for a, b in zip(a_refs, b_refs)]
            if sum_pairs and n_p > 1:
                prods = [functools.reduce(operator.add, prods)]
            return prods

        def compute():
            if nk == 1:
                epilogue(dots(), e_refs, o_refs)
                return
            k = pl.program_id(kaxis)

            @pl.when(k == 0)
            def _():
                for acc, p in zip(acc_refs, dots()):
                    acc[...] = p

            if nk > 2:
                @pl.when((k > 0) & (k < nk - 1))
                def _():
                    for acc, p in zip(acc_refs, dots()):
                        acc[...] += p

            @pl.when(k == nk - 1)
            def _():
                epilogue([acc[...] + p for acc, p in zip(acc_refs, dots())], e_refs, o_refs)

        hosted.wrap(grid, compute, c_ins, c_outs, sems)

    operands, in_specs = [], []
    for a, a_spec, b, b_spec in pairs:
        operands += [a, b]
        in_specs += [a_spec, b_spec]
    for e, e_spec in extras:
        operands.append(e)
        in_specs.append(e_spec)
    res = pl.pallas_call(
        body, name=name, grid=grid,
        out_shape=tuple([o for o, _ in outs] + hosted.out_shapes),
        in_specs=in_specs + hosted.in_specs, out_specs=tuple([s for _, s in outs] + hosted.out_specs),
        input_output_aliases=hosted.aliases,
        scratch_shapes=[pltpu.VMEM(acc_shape, F32) for _ in range(n_acc)] + hosted.scratch,
        compiler_params=_cparams(len(grid)),
    )(*operands, *hosted.operands)
    return list(res[:n_o]), list(res[n_o:])


NN = ((1,), (0,))
NT = ((1,), (1,))
TN = ((0,), (0,))


def _tile(n, want):
    if n <= want:
        return n
    t = want // 128 * 128
    while n % t:
        t -= 128
    return t


def _silu_parts(g):
    s = 1.0 / (1.0 + jnp.exp(-g))
    return s, g * s


def _mm_in(h, w_in, comm=None):
    t, d = h.shape
    n = w_in.shape[1]
    tm, tn = _tile(t, _TM), _tile(n, _TN)

    def epi(accs, e, o):
        o[0][...] = accs[0].astype(BF16)

    outs, couts = _matmul(
        "mm_in", (n // tn, t // tm), 1, None,
        [(h, pl.BlockSpec((tm, d), lambda j, i: (i, 0)), w_in, pl.BlockSpec((d, tn), lambda j, i: (0, j)))],
        NN, [], [(_sds((t, n), BF16), pl.BlockSpec((tm, tn), lambda j, i: (i, j)))], epi, True, None, comm)
    return outs[0], couts


def _mm_out(y, w_out, x, comm=None):
    t, m = y.shape
    d = w_out.shape[1]
    tm, tn = _tile(t, _TM), _tile(d, _TN)

    def epi(accs, e, o):
        o[0][...] = e[0][...] + accs[0]

    outs, couts = _matmul(
        "mm_out", (t // tm, d // tn), 1, None,
        [(y, pl.BlockSpec((tm, m), lambda i, j: (i, 0)), w_out, pl.BlockSpec((m, tn), lambda i, j: (0, j)))],
        NN, [(x, pl.BlockSpec((tm, tn), lambda i, j: (i, j)))],
        [(_sds((t, d), F32), pl.BlockSpec((tm, tn), lambda i, j: (i, j)))], epi, True, None, comm)
    return outs[0], couts


def _mm_swiglu(h2, wgt, wut, comm=None):
    t, d = h2.shape
    f = wgt.shape[0]
    tm, tn = _tile(t, _TM), _tile(f, 512)

    def epi(accs, e, o):
        g, u = accs
        _, sg = _silu_parts(g)
        o[0][...] = g.astype(BF16)
        o[1][...] = u.astype(BF16)
        o[2][...] = (sg * u).astype(BF16)

    wspec = pl.BlockSpec((tn, d), lambda j, i: (j, 0))
    hspec = pl.BlockSpec((tm, d), lambda j, i: (i, 0))
    ospec = pl.BlockSpec((tm, tn), lambda j, i: (i, j))
    osh = _sds((t, f), BF16)
    outs, couts = _matmul("mm_swiglu", (f // tn, t // tm), 1, None, [(h2, hspec, wgt, wspec), (h2, hspec, wut, wspec)],
                          NT, [], [(osh, ospec)] * 3, epi, False, None, comm)
    return outs, couts


def _mm_down(act, wd, x1, comm=None):
    t, f = act.shape
    d = wd.shape[1]
    tm, tn = _tile(t, _TM), _tile(d, _TN)
    nk = 2
    tk = f // nk

    def epi(accs, e, o):
        o[0][...] = e[0][...] + accs[0]

    outs, couts = _matmul(
        "mm_down", (t // tm, d // tn, nk), nk, 2,
        [(act, pl.BlockSpec((tm, tk), lambda i, j, k: (i, k)), wd, pl.BlockSpec((tk, tn), lambda i, j, k: (k, j)))],
        NN, [(x1, pl.BlockSpec((tm, tn), lambda i, j, k: (i, j)))],
        [(_sds((t, d), F32), pl.BlockSpec((tm, tn), lambda i, j, k: (i, j)))], epi, True, (tm, tn), comm)
    return outs[0], couts


def _mm_dact(dxb, wd, gate, up, comm=None):
    t, d = dxb.shape
    f = wd.shape[0]
    tm, tn = _tile(t, _TM), _tile(f, 512)

    def epi(accs, e, o):
        da = accs[0]
        g = e[0][...].astype(F32)
        u = e[1][...].astype(F32)
        s, sg = _silu_parts(g)
        o[0][...] = (da * u * (s + sg * (1.0 - s))).astype(BF16)
        o[1][...] = (da * sg).astype(BF16)

    bspec = pl.BlockSpec((tm, tn), lambda j, i: (i, j))
    osh = _sds((t, f), BF16)
    outs, couts = _matmul(
        "mm_dact", (f // tn, t // tm), 1, None,
        [(dxb, pl.BlockSpec((tm, d), lambda j, i: (i, 0)), wd, pl.BlockSpec((tn, d), lambda j, i: (j, 0)))],
        NT, [(gate, bspec), (up, bspec)], [(osh, bspec)] * 2, epi, True, None, comm)
    return outs, couts


def _mm_dh2(dgate, dup, wgt, wut, comm=None):
    t, f = dgate.shape
    d = wgt.shape[1]
    tm, tn = _tile(t, _TM), _tile(d, _TN)
    nk = 4
    tk = f // nk

    def epi(accs, e, o):
        o[0][...] = accs[0]

    aspec = pl.BlockSpec((tm, tk), lambda i, j, k: (i, k))
    wspec = pl.BlockSpec((tk, tn), lambda i, j, k: (k, j))
    outs, couts = _matmul("mm_dh2", (t // tm, d // tn, nk), nk, 2, [(dgate, aspec, wgt, wspec), (dup, aspec, wut, wspec)],
                          NN, [], [(_sds((t, d), F32), pl.BlockSpec((tm, tn), lambda i, j, k: (i, j)))], epi, True,
                          (tm, tn), comm)
    return outs[0], couts


def _mm_dw(name, a_list, b, tmo, tno, comm=None):
    t, m = a_list[0].shape
    n = b.shape[1]
    tt = _tile(t, _TT)
    nk = t // tt
    tmo, tno = _tile(m, tmo), _tile(n, tno)

    def epi(accs, e, o):
        for acc, out in zip(accs, o):
            out[...] = acc.astype(BF16)

    aspec = pl.BlockSpec((tt, tmo), lambda i, j, k: (k, i))
    bspec = pl.BlockSpec((tt, tno), lambda i, j, k: (k, j))
    ospec = pl.BlockSpec((tmo, tno), lambda i, j, k: (i, j))
    if nk == 1:
        return _matmul(name, (m // tmo, n // tno, 1), 1, None, [(a, aspec, b, bspec) for a in a_list], TN, [],
                       [(_sds((m, n), BF16), ospec)] * len(a_list), epi, False, None, comm)
    return _matmul(name, (m // tmo, n // tno, nk), nk, 2, [(a, aspec, b, bspec) for a in a_list], TN, [],
                   [(_sds((m, n), BF16), ospec)] * len(a_list), epi, False, (tmo, tno), comm)


def _mm_dy(dxb, w_out, comm=None):
    t, d = dxb.shape
    m = w_out.shape[0]
    tm, tn = _tile(t, _TM), _tile(m, _TN)

    def epi(accs, e, o):
        o[0][...] = accs[0].astype(BF16)

    outs, couts = _matmul(
        "mm_dy", (t // tm, m // tn), 1, None,
        [(dxb, pl.BlockSpec((tm, d), lambda i, j: (i, 0)), w_out, pl.BlockSpec((tn, d), lambda i, j: (j, 0)))], NT, [],
        [(_sds((t, m), BF16), pl.BlockSpec((tm, tn), lambda i, j: (i, j)))], epi, True, None, comm)
    return outs[0], couts


def _mm_dh(dz, w_in, comm=None):
    t, n = dz.shape
    d = w_in.shape[0]
    tm, tn = _tile(t, _TM), _tile(d, _TN)
    nk = 2
    tk = n // nk

    def epi(accs, e, o):
        o[0][...] = accs[0]

    outs, couts = _matmul(
        "mm_dh", (t // tm, d // tn, nk), nk, 2,
        [(dz, pl.BlockSpec((tm, tk), lambda i, j, k: (i, k)), w_in, pl.BlockSpec((tn, tk), lambda i, j, k: (j, k)))], NT,
        [], [(_sds((t, d), F32), pl.BlockSpec((tm, tn), lambda i, j, k: (i, j)))], epi, True, (tm, tn), comm)
    return outs[0], couts


def _rmsnorm_fwd(x, g):
    t, d = x.shape
    tm = min(_TM_NORM, t)

    def body(x_ref, g_ref, o_ref):
        xv = x_ref[...]
        rs = lax.rsqrt(jnp.mean(xv * xv, axis=-1, keepdims=True) + RMS_EPS)
        o_ref[...] = (xv * rs * g_ref[...]).astype(BF16)

    return pl.pallas_call(
        body, name="rmsnorm_fwd", grid=(t // tm,), out_shape=_sds((t, d), BF16),
        in_specs=[pl.BlockSpec((tm, d), lambda i: (i, 0)), pl.BlockSpec((1, d), lambda i: (0, 0))],
        out_specs=pl.BlockSpec((tm, d), lambda i: (i, 0)), compiler_params=_cparams(1))(x, g)


def _rmsnorm_bwd_math(xv, g, dh):
    rs = lax.rsqrt(jnp.mean(xv * xv, axis=-1, keepdims=True) + RMS_EPS)
    xh = xv * rs
    gd = dh * g
    dx = rs * (gd - xh * jnp.mean(gd * xh, axis=-1, keepdims=True))
    return dx, jnp.sum(dh * xh, axis=0, keepdims=True)


def _rmsnorm_bwd(x, g, dh, dres):
    t, d = x.shape
    tm = min(_TM_NORM, t)

    def body(x_ref, g_ref, dh_ref, dres_ref, dx_ref, dxb_ref, dg_ref):
        dx, dg = _rmsnorm_bwd_math(x_ref[...], g_ref[...], dh_ref[...])
        dx = dx + dres_ref[...]
        dx_ref[...] = dx
        dxb_ref[...] = dx.astype(BF16)

        @pl.when(pl.program_id(0) == 0)
        def _():
            dg_ref[...] = dg

        @pl.when(pl.program_id(0) > 0)
        def _():
            dg_ref[...] += dg

    row = pl.BlockSpec((tm, d), lambda i: (i, 0))
    vec = pl.BlockSpec((1, d), lambda i: (0, 0))
    return pl.pallas_call(
        body, name="rmsnorm_bwd", grid=(t // tm,),
        out_shape=(_sds((t, d), F32), _sds((t, d), BF16), _sds((1, d), F32)),
        in_specs=[row, vec, row, row], out_specs=(row, row, vec), compiler_params=_cparams(1))(x, g, dh, dres)


def _loss_head(x, g, target):
    t, d = x.shape
    tm = min(_TM_NORM, t)

    def body(x_ref, g_ref, t_ref, dx_ref, dxb_ref, dg_ref, loss_ref):
        xv, gv = x_ref[...], g_ref[...]
        rs = lax.rsqrt(jnp.mean(xv * xv, axis=-1, keepdims=True) + RMS_EPS)
        diff = xv * rs * gv - t_ref[...]
        part = 0.5 * jnp.sum(jnp.mean(diff * diff, axis=-1, keepdims=True), axis=0, keepdims=True)
        part = jnp.broadcast_to(part, (1, 128))
        dx, dg = _rmsnorm_bwd_math(xv, gv, diff * (1.0 / d))
        dx_ref[...] = dx
        dxb_ref[...] = dx.astype(BF16)

        @pl.when(pl.program_id(0) == 0)
        def _():
            dg_ref[...] = dg
            loss_ref[...] = part

        @pl.when(pl.program_id(0) > 0)
        def _():
            dg_ref[...] += dg
            loss_ref[...] += part

    row = pl.BlockSpec((tm, d), lambda i: (i, 0))
    vec = pl.BlockSpec((1, d), lambda i: (0, 0))
    return pl.pallas_call(
        body, name="loss_head", grid=(t // tm,),
        out_shape=(_sds((t, d), F32), _sds((t, d), BF16), _sds((1, d), F32), _sds((1, 128), F32)),
        in_specs=[row, vec, row], out_specs=(row, row, vec, pl.BlockSpec((1, 128), lambda i: (0, 0))),
        compiler_params=_cparams(1))(x, g, target)


def _gelu(x):
    th = jnp.tanh(GELU_C * (x + GELU_A * x * x * x))
    return 0.5 * x * (1.0 + th), th


def _gelu_grad(x, th):
    return 0.5 * (1.0 + th) + 0.5 * x * (1.0 - th * th) * GELU_C * (1.0 + 3.0 * GELU_A * x * x)


def _masked_ws(ws_ref, h):
    i = lax.broadcasted_iota(jnp.int32, (BLK, BLK), 0) // CHUNK
    j = lax.broadcasted_iota(jnp.int32, (BLK, BLK), 1) // CHUNK
    return jnp.where(j <= i, ws_ref[h], 0.0)


def _shift_down(q, n, first_rows):
    rolled = pltpu.roll(q, n, 0)
    row = lax.broadcasted_iota(jnp.int32, q.shape, 0)
    for r, val in enumerate(first_rows):
        rolled = jnp.where(row == r, val, rolled)
    return rolled


def _shift_up(q, n, last_rows):
    tm = q.shape[0]
    rolled = pltpu.roll(q, tm - n, 0)
    row = lax.broadcasted_iota(jnp.int32, q.shape, 0)
    for r, val in enumerate(last_rows):
        rolled = jnp.where(row == tm - n + r, val, rolled)
    return rolled


def _mixer_specs(t, a, tm):
    hb = tm // HALO
    last = t // HALO - 1
    tile = pl.BlockSpec((tm, 5 * a), lambda i: (i, 0))
    prev = [pl.BlockSpec((HALO, a), functools.partial(lambda i, col: (jnp.maximum(i * hb - 1, 0), col), col=col))
            for col in (3, 4)]
    nxt = [pl.BlockSpec((HALO, a), functools.partial(lambda i, col: (jnp.minimum((i + 1) * hb, last), col), col=col))
           for col in (2, 3, 4)]
    return tile, prev, nxt


def _group_a_fwd(zu, zv, lng, lnb, ws_ref, bb_ref, mixed_ref, vln_ref):
    u, thu = _gelu(zu)
    v, thv = _gelu(zv)
    mu = jnp.mean(v, axis=-1, keepdims=True)
    vc = v - mu
    rs = lax.rsqrt(jnp.mean(vc * vc, axis=-1, keepdims=True) + LN_EPS)
    vhat = vc * rs
    vln_ref[...] = vhat * lng + lnb
    tm, a = zu.shape
    hd = a // HEADS
    for h in range(HEADS):
        w = _masked_ws(ws_ref, h).astype(BF16)
        for b in range(tm // BLK):
            rows, cols = pl.ds(b * BLK, BLK), pl.ds(h * hd, hd)
            mixed_ref[rows, cols] = jnp.dot(w, vln_ref[rows, cols].astype(BF16), preferred_element_type=F32) + bb_ref[h]
    return u, thu, thv, rs, vhat


def _mixer_fwd(z, ln_g, ln_b, w_spatial, bb, conv_w, gg):
    t = z.shape[0]
    a = z.shape[1] // 5
    tm = min(_TM_MIX, t)
    tile, prev, _ = _mixer_specs(t, a, tm)

    def body(z_ref, pc_ref, ph_ref, lng_ref, lnb_ref, ws_ref, bb_ref, cw_ref, gg_ref, y_ref, mixed_ref, vln_ref):
        i = pl.program_id(0)
        zu = z_ref[:, 0:a].astype(F32)
        zv = z_ref[:, a:2 * a].astype(F32)
        u, _, _, _, _ = _group_a_fwd(zu, zv, lng_ref[...], lnb_ref[...], ws_ref, bb_ref, mixed_ref, vln_ref)
        ya = u * mixed_ref[...]
        ra = lax.rsqrt(jnp.mean(ya * ya, axis=-1, keepdims=True) + RMS_EPS)
        y_ref[:, 0:a] = (ya * ra * gg_ref[:, 0:a]).astype(BF16)

        zb = z_ref[:, 2 * a:3 * a].astype(F32)
        q = z_ref[:, 3 * a:4 * a].astype(F32) * z_ref[:, 4 * a:5 * a].astype(F32)
        qp = jnp.where(i > 0, pc_ref[...].astype(F32) * ph_ref[...].astype(F32), 0.0)
        qm1 = _shift_down(q, 1, [qp[HALO - 1:HALO]])
        qm2 = _shift_down(q, 2, [qp[HALO - 2:HALO - 1], qp[HALO - 1:HALO]])
        cv = cw_ref[0:1, :] * qm2 + cw_ref[1:2, :] * qm1 + cw_ref[2:3, :] * q
        yb = zb * cv
        rb = lax.rsqrt(jnp.mean(yb * yb, axis=-1, keepdims=True) + RMS_EPS)
        y_ref[:, a:2 * a] = (yb * rb * gg_ref[:, a:2 * a]).astype(BF16)

    full = lambda shape: pl.BlockSpec(shape, lambda i: (0,) * len(shape))
    return pl.pallas_call(
        body, name="mixer_fwd", grid=(t // tm,), out_shape=_sds((t, 2 * a), BF16),
        in_specs=[tile, *prev, full((1, a)), full((1, a)), full(w_spatial.shape), full(bb.shape), full(conv_w.shape),
                  full((1, 2 * a))],
        out_specs=pl.BlockSpec((tm, 2 * a), lambda i: (i, 0)),
        scratch_shapes=[pltpu.VMEM((tm, a), F32), pltpu.VMEM((tm, a), F32)],
        compiler_params=_cparams(1))(z, z, z, ln_g, ln_b, w_spatial, bb, conv_w, gg)


def _mixer_bwd(z, dy, ln_g, ln_b, w_spatial, bb, conv_w, gg):
    t = z.shape[0]
    a = z.shape[1] // 5
    hd = a // HEADS
    tm = min(_TM_MIX, t)
    n_tiles = t // tm
    tile, prev, nxt = _mixer_specs(t, a, tm)
    hb = tm // HALO
    dy_tile = pl.BlockSpec((tm, 2 * a), lambda i: (i, 0))
    dy_next = pl.BlockSpec((HALO, a), lambda i: (jnp.minimum((i + 1) * hb, t // HALO - 1), 1))

    def body(z_ref, pc_ref, ph_ref, nb_ref, nc_ref, nh_ref, dy_ref, ndy_ref, lng_ref, lnb_ref, ws_ref, bb_ref, cw_ref,
             gg_ref, dz_ref, dlng_ref, dlnb_ref, dws_ref, dbb_ref, dcw_ref, dgg_ref, mixed_ref, vln_ref, dmix_ref,
             dvln_ref):
        i = pl.program_id(0)

        @pl.when(i == 0)
        def _():
            for ref in (dlng_ref, dlnb_ref, dws_ref, dbb_ref, dcw_ref, dgg_ref):
                ref[...] = jnp.zeros(ref.shape, F32)

        lng = lng_ref[...]
        zu = z_ref[:, 0:a].astype(F32)
        zv = z_ref[:, a:2 * a].astype(F32)
        u, thu, thv, rs, vhat = _group_a_fwd(zu, zv, lng, lnb_ref[...], ws_ref, bb_ref, mixed_ref, vln_ref)
        mixed = mixed_ref[...]
        ya = u * mixed
        ra = lax.rsqrt(jnp.mean(ya * ya, axis=-1, keepdims=True) + RMS_EPS)
        da = dy_ref[:, 0:a].astype(F32)
        yah = ya * ra
        dgg_ref[:, 0:a] += jnp.sum(da * yah, axis=0, keepdims=True)
        ga = da * gg_ref[:, 0:a]
        dya = ra * (ga - yah * jnp.mean(ga * yah, axis=-1, keepdims=True))
        dz_ref[:, 0:a] = (dya * mixed * _gelu_grad(zu, thu)).astype(BF16)
        dmix_ref[...] = dya * u
        for h in range(HEADS):
            w = _masked_ws(ws_ref, h).astype(BF16)
            dw = jnp.zeros((BLK, BLK), F32)
            db = jnp.zeros((BLK, hd), F32)
            for b in range(tm // BLK):
                rows, cols = pl.ds(b * BLK, BLK), pl.ds(h * hd, hd)
                dm = dmix_ref[rows, cols]
                dmb = dm.astype(BF16)
                db = db + dm
                dw = dw + lax.dot_general(dmb, vln_ref[rows, cols].astype(BF16), (NT, ((), ())),
                                          preferred_element_type=F32)
                dvln_ref[rows, cols] = lax.dot_general(w, dmb, (TN, ((), ())), preferred_element_type=F32)
            dws_ref[h] += dw
            dbb_ref[h] += db
        dvln = dvln_ref[...]
        dlng_ref[...] += jnp.sum(dvln * vhat, axis=0, keepdims=True)
        dlnb_ref[...] += jnp.sum(dvln, axis=0, keepdims=True)
        dvh = dvln * lng
        dv = rs * (dvh - jnp.mean(dvh, axis=-1, keepdims=True) - vhat * jnp.mean(dvh * vhat, axis=-1, keepdims=True))
        dz_ref[:, a:2 * a] = (dv * _gelu_grad(zv, thv)).astype(BF16)

        w0, w1, w2 = cw_ref[0:1, :], cw_ref[1:2, :], cw_ref[2:3, :]
        ggb = gg_ref[:, a:2 * a]
        zb = z_ref[:, 2 * a:3 * a].astype(F32)
        zc = z_ref[:, 3 * a:4 * a].astype(F32)
        zh = z_ref[:, 4 * a:5 * a].astype(F32)
        q = zc * zh
        qp = jnp.where(i > 0, pc_ref[...].astype(F32) * ph_ref[...].astype(F32), 0.0)
        qm1 = _shift_down(q, 1, [qp[HALO - 1:HALO]])
        qm2 = _shift_down(q, 2, [qp[HALO - 2:HALO - 1], qp[HALO - 1:HALO]])
        cv = w0 * qm2 + w1 * qm1 + w2 * q

        def conv_out_grad(zb_, cv_, dout_):
            yb = zb_ * cv_
            rb = lax.rsqrt(jnp.mean(yb * yb, axis=-1, keepdims=True) + RMS_EPS)
            ybh = yb * rb
            gb = dout_ * ggb
            dyb = rb * (gb - ybh * jnp.mean(gb * ybh, axis=-1, keepdims=True))
            return dyb * zb_, dyb * cv_, ybh

        db_out = dy_ref[:, a:2 * a].astype(F32)
        g, dzb, ybh = conv_out_grad(zb, cv, db_out)
        dgg_ref[:, a:2 * a] += jnp.sum(db_out * ybh, axis=0, keepdims=True)
        dz_ref[:, 2 * a:3 * a] = dzb.astype(BF16)
        qn = nc_ref[...].astype(F32) * nh_ref[...].astype(F32)
        zbn = nb_ref[...].astype(F32)
        cvn = w0 * _shift_down(qn, 2, [q[tm - 2:tm - 1], q[tm - 1:tm]]) + w1 * _shift_down(qn, 1, [q[tm - 1:tm]]) + w2 * qn
        gn, _, _ = conv_out_grad(zbn, cvn, ndy_ref[...].astype(F32))
        gn = jnp.where(i < n_tiles - 1, gn, 0.0)
        dq = w2 * g + w1 * _shift_up(g, 1, [gn[0:1]]) + w0 * _shift_up(g, 2, [gn[0:1], gn[1:2]])
        dz_ref[:, 3 * a:4 * a] = (dq * zh).astype(BF16)
        dz_ref[:, 4 * a:5 * a] = (dq * zc).astype(BF16)
        dcw_ref[0:1, :] += jnp.sum(g * qm2, axis=0, keepdims=True)
        dcw_ref[1:2, :] += jnp.sum(g * qm1, axis=0, keepdims=True)
        dcw_ref[2:3, :] += jnp.sum(g * q, axis=0, keepdims=True)

        @pl.when(i == n_tiles - 1)
        def _():
            for h in range(HEADS):
                dbb_ref[h] = jnp.broadcast_to(jnp.sum(dbb_ref[h], axis=1, keepdims=True), (BLK, hd))
                dws_ref[h] = _masked_ws(dws_ref, h)

    full = lambda shape: pl.BlockSpec(tuple(shape), lambda i: (0,) * len(shape))
    out_shapes = (_sds((t, 5 * a), BF16), _sds((1, a), F32), _sds((1, a), F32), _sds(w_spatial.shape, F32),
                  _sds(bb.shape, F32), _sds((8, a), F32), _sds((1, 2 * a), F32))
    return pl.pallas_call(
        body, name="mixer_bwd", grid=(n_tiles,), out_shape=out_shapes,
        in_specs=[tile, *prev, *nxt, dy_tile, dy_next, full((1, a)), full((1, a)), full(w_spatial.shape), full(bb.shape),
                  full(conv_w.shape), full((1, 2 * a))],
        out_specs=(tile, *[full(s.shape) for s in out_shapes[1:]]),
        scratch_shapes=[pltpu.VMEM((tm, a), F32)] * 4,
        compiler_params=_cparams(1))(z, z, z, z, z, z, dy, dy, ln_g, ln_b, w_spatial, bb, conv_w, gg)


def _all_reduce_small(pack):
    r = pack.shape[0]

    def body(in_ref, out_ref, acc_ref, recv_ref, send_sems, recv_sems):
        x, y, c = _place()
        partners = [(x, y, 1 - c), (1 - x, y, c), (x, 1 - y, c)]
        acc_ref[0] = in_ref[...]
        for s, partner in enumerate(partners):
            cp = pltpu.make_async_remote_copy(
                src_ref=acc_ref.at[s], dst_ref=recv_ref.at[s], send_sem=send_sems.at[s], recv_sem=recv_sems.at[s],
                device_id=partner, device_id_type=MESH)
            cp.start()
            cp.wait()
            if s < 2:
                acc_ref[s + 1] = acc_ref[s] + recv_ref[s]
            else:
                out_ref[...] = acc_ref[s] + recv_ref[s]

    vmem = pl.BlockSpec(memory_space=pltpu.VMEM)
    return pl.pallas_call(
        body, name="all_reduce_small", out_shape=_sds(pack.shape, F32), in_specs=[vmem], out_specs=vmem,
        scratch_shapes=[pltpu.VMEM((3, r, 128), F32), pltpu.VMEM((3, r, 128), F32), pltpu.SemaphoreType.DMA((3,)),
                        pltpu.SemaphoreType.DMA((3,))],
        compiler_params=pltpu.CompilerParams(vmem_limit_bytes=VMEM_LIMIT_V7X),
    )(pack)


def _adamw_math(w, g, m, v):
    m = ADAM_B1 * m + (1.0 - ADAM_B1) * g
    v = ADAM_B2 * v + (1.0 - ADAM_B2) * (g * g)
    m_hat = m / (1.0 - ADAM_B1 ** ADAM_STEP)
    v_hat = v / (1.0 - ADAM_B2 ** ADAM_STEP)
    delta = -ADAM_LR * (m_hat / (jnp.sqrt(v_hat) + ADAM_EPS) + ADAM_WD * w)
    return delta, m, v


def _adamw_big(name, land, w, m, v, comm=None):
    nl, n_slots, r, c = land.shape
    tr = max(8, min(r, (256 * 640) // c // 8 * 8))
    while r % tr:
        tr -= 8
    grid = (nl, r // tr)
    hosted = _Hosted(comm, 4, 4)
    n_ci, n_co = len(hosted.operands), len(hosted.out_shapes)

    def body(*refs):
        land_ref, w_ref, m_ref, v_ref = refs[:4]
        c_ins = refs[4:4 + n_ci]
        g_out, d_out, m_out, v_out = refs[4 + n_ci:8 + n_ci]
        c_outs = refs[8 + n_ci:8 + n_ci + n_co]
        sems = refs[8 + n_ci + n_co:]

        def compute():
            g = land_ref[0].astype(F32)
            for s in range(1, n_slots):
                g = g + land_ref[s].astype(F32)
            delta, mn, vn = _adamw_math(w_ref[...], g, m_ref[...], v_ref[...])
            g_out[...] = g
            d_out[...] = delta
            m_out[...] = mn
            v_out[...] = vn

        hosted.wrap(grid, compute, c_ins, c_outs, sems)

    blk = pl.BlockSpec((None, tr, c), lambda l, i: (l, i, 0))
    res = pl.pallas_call(
        body, name=name, grid=grid, out_shape=tuple([_sds((nl, r, c), F32)] * 4 + hosted.out_shapes),
        in_specs=[pl.BlockSpec((None, n_slots, tr, c), lambda l, i: (l, 0, i, 0)), blk, blk, blk] + hosted.in_specs,
        out_specs=tuple([blk] * 4 + hosted.out_specs), input_output_aliases=hosted.aliases,
        scratch_shapes=hosted.scratch, compiler_params=_cparams(2))(land, w, m, v, *hosted.operands)
    return list(res[:4]), list(res[4:])


def _adamw_small(g, w, m, v):
    def body(g_ref, w_ref, m_ref, v_ref, d_out, m_out, v_out):
        delta, mn, vn = _adamw_math(w_ref[...], g_ref[...], m_ref[...], v_ref[...])
        d_out[...] = delta
        m_out[...] = mn
        v_out[...] = vn

    return pl.pallas_call(body, name="adamw_small", out_shape=tuple([_sds(g.shape, F32)] * 3),
                          compiler_params=pltpu.CompilerParams(vmem_limit_bytes=VMEM_LIMIT_V7X))(g, w, m, v)


def _rows(a):
    return a.reshape(-1, 128)


BIG = ["w_in", "w_out", "w_gate", "w_up", "w_down"]
AG_HOSTS = {
    ("mm_in", 0): [("w_out", 0), ("w_gate", 0)], ("mm_out", 0): [("w_up", 0)],
    ("mm_swiglu", 0): [("w_down", 0), ("w_in", 1)], ("mm_down", 0): [("w_out", 1), ("w_gate", 1)],
    ("mm_in", 1): [("w_up", 1)], ("mm_swiglu", 1): [("w_down", 1)],
}


def kernel(x, norm1_g, w_in, gmlp_ln_g, gmlp_ln_b, w_spatial, b_spatial, conv_w, group_norm_g, w_out, norm2_g, w_gate, w_up, w_down, final_norm_g, loss_target, m_norm1_g, m_w_in, m_gmlp_ln_g, m_gmlp_ln_b, m_w_spatial, m_b_spatial, m_conv_w, m_group_norm_g, m_w_out, m_norm2_g, m_w_gate, m_w_up, m_w_down, m_final_norm_g, v_norm1_g, v_w_in, v_gmlp_ln_g, v_gmlp_ln_b, v_w_spatial, v_b_spatial, v_conv_w, v_group_norm_g, v_w_out, v_norm2_g, v_w_gate, v_w_up, v_w_down, v_final_norm_g):
    nl = N_LAYERS
    t, d = x.shape[1], x.shape[2]
    a = d // 2
    hd = a // HEADS
    xin = x.reshape(t, d)
    target = loss_target.reshape(t, d)
    me = _index(_place())

    tr = lambda w: jnp.transpose(w, (0, 2, 1))
    big = {"w_in": w_in, "w_out": w_out, "w_gate": tr(w_gate), "w_up": tr(w_up), "w_down": w_down}
    big_m = {"w_in": m_w_in, "w_out": m_w_out, "w_gate": tr(m_w_gate), "w_up": tr(m_w_up), "w_down": m_w_down}
    big_v = {"w_in": v_w_in, "w_out": v_w_out, "w_gate": tr(v_w_gate), "w_up": tr(v_w_up), "w_down": v_w_down}
    block = {k: big[k].shape[1:] for k in BIG}
    view = {k: _cols_view(block[k][1]) if k == "w_in" else _rows_view(block[k][0]) for k in BIG}
    full_shape = {k: (block[k][0], N_DEV * block[k][1]) if k == "w_in" else (N_DEV * block[k][0], block[k][1])
                  for k in BIG}

    def ag_spec(k, l):
        return (big[k][l].astype(BF16), _sds(full_shape[k], BF16), view[k])

    first = _comm_only("all_gather_first", [_ag_piece([ag_spec("w_in", 0), (conv_w, _sds((N_DEV, *conv_w.shape), F32),
                                                                             _slot_view)])])
    weights = {("w_in", 0): first[0]}
    conv_full = jnp.transpose(first[1], (1, 2, 0, 3)).reshape(nl, 3, a)
    bb = jnp.broadcast_to(b_spatial[..., None], (nl, HEADS, BLK, hd))

    def hosted(name, l):
        keys = AG_HOSTS.get((name, l), [])
        return keys, ([_ag_piece([ag_spec(k, kl) for k, kl in keys])] if keys else None)

    def landed(keys, couts):
        for key, arr in zip(keys, couts):
            weights[key] = arr

    saved = []
    xl = xin
    for l in range(nl):
        h = _rmsnorm_fwd(xl, norm1_g[l:l + 1])
        keys, comm = hosted("mm_in", l)
        z, couts = _mm_in(h, weights[("w_in", l)], comm)
        landed(keys, couts)
        y = _mixer_fwd(z, gmlp_ln_g[l:l + 1], gmlp_ln_b[l:l + 1], w_spatial[l], bb[l], conv_full[l],
                       group_norm_g[l:l + 1])
        keys, comm = hosted("mm_out", l)
        x1, couts = _mm_out(y, weights[("w_out", l)], xl, comm)
        landed(keys, couts)
        h2 = _rmsnorm_fwd(x1, norm2_g[l:l + 1])
        keys, comm = hosted("mm_swiglu", l)
        (gate, up, act), couts = _mm_swiglu(h2, weights[("w_gate", l)], weights[("w_up", l)], comm)
        landed(keys, couts)
        keys, comm = hosted("mm_down", l)
        x2, couts = _mm_down(act, weights[("w_down", l)], x1, comm)
        landed(keys, couts)
        saved.append(dict(x=xl, h=h, z=z, y=y, x1=x1, h2=h2, gate=gate, up=up, act=act))
        xl = x2

    dx, dxb, d_final_g, loss_part = _loss_head(xl, final_norm_g.reshape(1, d), target)
    small = [None] * nl
    core = lax.axis_index("c").astype(jnp.int32).reshape(1)
    stage_shape = {k: _sds((N_CHIPS, *block[k]), BF16) for k in BIG}
    land_shape = {k: _sds((nl, N_CHIPS, *block[k]), BF16) for k in BIG}
    grads = [dict() for _ in range(nl)]
    stages = [dict() for _ in range(nl)]
    sums = [dict() for _ in range(nl)]
    lands = {k: None for k in BIG}

    def core_job(l, keys):
        def sink(outs):
            stages[l].update(zip(keys, outs))
        return _rs_core_piece([(grads[l][k], stage_shape[k], view[k]) for k in keys]), sink

    def chip_job(l, keys):
        def sink(outs):
            lands.update(zip(keys, outs))
        return _rs_chip_piece([(sums[l][k], land_shape[k]) for k in keys], l,
                              None if lands[keys[0]] is None else [lands[k] for k in keys]), sink

    def add_up(l, keys):
        for k in keys:
            sums[l][k] = _chip_sums(f"chip_sums_{k}", grads[l][k], stages[l][k], k == "w_in", core)

    def host(*jobs):
        def deliver(couts):
            i = 0
            for piece, sink in jobs:
                n_out = len(piece.out_shapes)
                sink(couts[i:i + n_out])
                i += n_out
        return [piece for piece, _ in jobs], deliver

    for l in reversed(range(nl)):
        s = saved[l]
        wi, wo, wgt, wut, wd = [weights[(k, l)] for k in BIG]
        later = l + 1 < nl
        comm, deliver = host(chip_job(l + 1, ["w_out"]), core_job(l + 1, ["w_in"])) if later else host()
        (dgate, dup), couts = _mm_dact(dxb, wd, s["gate"], s["up"], comm)
        deliver(couts)
        if later:
            add_up(l + 1, ["w_in"])
        comm, deliver = host(chip_job(l + 1, ["w_in"])) if later else host()
        (grads[l]["w_down"],), couts = _mm_dw("mm_dw_down", [s["act"]], dxb, 2816, 1024, comm)
        deliver(couts)
        comm, deliver = host(core_job(l, ["w_down"]))
        dh2, couts = _mm_dh2(dgate, dup, wgt, wut, comm)
        deliver(couts)
        add_up(l, ["w_down"])
        comm, deliver = host(chip_job(l, ["w_down"]))
        (grads[l]["w_gate"], grads[l]["w_up"]), couts = _mm_dw("mm_dw_gate_up", [dgate, dup], s["h2"], 1408, 1024, comm)
        deliver(couts)
        dx1, dx1b, d_n2 = _rmsnorm_bwd(s["x1"], norm2_g[l:l + 1], dh2, dx)
        comm, deliver = host(core_job(l, ["w_gate", "w_up"]))
        dy, couts = _mm_dy(dx1b, wo, comm)
        deliver(couts)
        add_up(l, ["w_gate", "w_up"])
        (grads[l]["w_out"],), _ = _mm_dw("mm_dw_out", [s["y"]], dx1b, 1024, 1024)
        dz, d_lng, d_lnb, d_ws, d_bb, d_cw, d_gg = _mixer_bwd(
            s["z"], dy, gmlp_ln_g[l:l + 1], gmlp_ln_b[l:l + 1], w_spatial[l], bb[l], conv_full[l], group_norm_g[l:l + 1])
        comm, deliver = host(chip_job(l, ["w_gate"]))
        dh, couts = _mm_dh(dz, wi, comm)
        deliver(couts)
        comm, deliver = host(chip_job(l, ["w_up"]), core_job(l, ["w_out"]))
        (grads[l]["w_in"],), couts = _mm_dw("mm_dw_in", [s["h"]], dz, 2048, 1024, comm)
        deliver(couts)
        add_up(l, ["w_out"])
        dx, dxb, d_n1 = _rmsnorm_bwd(s["x"], norm1_g[l:l + 1], dh, dx1)
        small[l] = dict(norm1_g=d_n1, gmlp_ln_g=d_lng, gmlp_ln_b=d_lnb, w_spatial=d_ws, b_spatial=d_bb[:, :, 0],
                        group_norm_g=d_gg, norm2_g=d_n2, conv_w=d_cw[0:3])
    grad_x = dx.reshape(x.shape)

    rep = ["norm1_g", "gmlp_ln_g", "gmlp_ln_b", "w_spatial", "b_spatial", "group_norm_g", "norm2_g"]
    rep_w = dict(norm1_g=norm1_g, gmlp_ln_g=gmlp_ln_g, gmlp_ln_b=gmlp_ln_b, w_spatial=w_spatial, b_spatial=b_spatial,
                 group_norm_g=group_norm_g, norm2_g=norm2_g)
    rep_m = dict(norm1_g=m_norm1_g, gmlp_ln_g=m_gmlp_ln_g, gmlp_ln_b=m_gmlp_ln_b, w_spatial=m_w_spatial,
                 b_spatial=m_b_spatial, group_norm_g=m_group_norm_g, norm2_g=m_norm2_g)
    rep_v = dict(norm1_g=v_norm1_g, gmlp_ln_g=v_gmlp_ln_g, gmlp_ln_b=v_gmlp_ln_b, w_spatial=v_w_spatial,
                 b_spatial=v_b_spatial, group_norm_g=v_group_norm_g, norm2_g=v_norm2_g)
    parts = [_rows(jnp.stack([small[l][k].reshape(rep_w[k].shape[1:]) for l in range(nl)])) for k in rep]
    parts.append(_rows(d_final_g))
    parts.append(_rows(jnp.stack([small[l]["conv_w"] for l in range(nl)])))
    parts.append(jnp.broadcast_to(loss_part, (8, 128)))
    sizes = [p.shape[0] for p in parts]
    total = _all_reduce_small(jnp.concatenate(parts, axis=0))
    offs = [0]
    for n in sizes:
        offs.append(offs[-1] + n)
    pieces = [total[offs[i]:offs[i + 1]] for i in range(len(parts))]
    loss = pieces[-1][0, 0]
    conv_g_full = pieces[-2].reshape(nl, 3, N_DEV, a // N_DEV)
    conv_g = lax.dynamic_index_in_dim(conv_g_full, me, axis=2, keepdims=False)
    n_rep = offs[len(rep) + 1]
    pad = jnp.zeros((2, 128), F32)

    def small_pack(named, final, conv):
        return jnp.concatenate([_rows(named[k]) for k in rep] + [_rows(final), _rows(conv), pad], axis=0)

    g_small = jnp.concatenate([total[:n_rep], _rows(conv_g), pad], axis=0)
    d_small, m_small, v_small = _adamw_small(
        g_small, small_pack(rep_w, final_norm_g, conv_w), small_pack(rep_m, m_final_norm_g, m_conv_w),
        small_pack(rep_v, v_final_norm_g, v_conv_w))

    def unpack(packed):
        out = {k: packed[offs[i]:offs[i + 1]].reshape(rep_w[k].shape) for i, k in enumerate(rep)}
        out["final_norm_g"] = packed[offs[len(rep)]:n_rep].reshape(final_norm_g.shape)
        out["conv_w"] = packed[n_rep:n_rep + 6].reshape(conv_w.shape)
        return out

    res = {"grad": unpack(g_small), "delta": unpack(d_small), "m": unpack(m_small), "v": unpack(v_small)}

    tail = {"w_down": lambda: host(chip_job(0, ["w_out"]), core_job(0, ["w_in"])),
            "w_gate": lambda: host(chip_job(0, ["w_in"]))}
    for k in ["w_down", "w_gate", "w_up", "w_out", "w_in"]:
        comm, deliver = tail[k]() if k in tail else host()
        outs, couts = _adamw_big(f"adamw_{k}", lands[k], big[k], big_m[k], big_v[k], comm)
        deliver(couts)
        if k == "w_down":
            add_up(0, ["w_in"])
        if k in ("w_gate", "w_up"):
            outs = [tr(o) for o in outs]
        res["grad"][k], res["delta"][k], res["m"][k], res["v"][k] = outs

    order = ["norm1_g", "w_in", "gmlp_ln_g", "gmlp_ln_b", "w_spatial", "b_spatial", "conv_w", "group_norm_g", "w_out",
             "norm2_g", "w_gate", "w_up", "w_down", "final_norm_g"]
    return (loss, grad_x, *[res["grad"][k] for k in order], *[res["delta"][k] for k in order],
            *[res["m"][k] for k in order], *[res["v"][k] for k in order])
```

```python
import functools
import math
import operator

import jax
import jax.numpy as jnp
from jax import lax
from jax.experimental import pallas as pl
from jax.experimental.pallas import tpu as pltpu

F32 = jnp.float32
BF16 = jnp.bfloat16
MESH = pl.DeviceIdType.MESH

N_DEV = 8
N_LAYERS = 2
HEADS = 8
BLK = 128
CHUNK = 64
HALO = 16
RMS_EPS = 1e-6
LN_EPS = 1e-5
ADAM_LR, ADAM_B1, ADAM_B2, ADAM_EPS, ADAM_WD, ADAM_STEP = 0.001, 0.9, 0.999, 1e-8, 0.01, 10
GELU_C = math.sqrt(2.0 / math.pi)
GELU_A = 0.044715

VMEM_LIMIT_V7X = 56 * 1024 * 1024
_TM = 1024
_TN = 1024
_TT = 1024
_TM_MIX = 256
_TM_NORM = 512


def _cparams(n_axes):
    return pltpu.CompilerParams(dimension_semantics=("arbitrary",) * n_axes, vmem_limit_bytes=VMEM_LIMIT_V7X)


def _sds(shape, dtype):
    return jax.ShapeDtypeStruct(tuple(shape), dtype)


def _place():
    return lax.axis_index("x"), lax.axis_index("y"), lax.axis_index("c")


def _index(place):
    return 4 * place[0] + 2 * place[1] + place[2]


class _Piece:
    def __init__(self, operands, out_shapes, aliases, n_sems, start, finish, mid1=None, mid2=None):
        self.operands, self.out_shapes, self.aliases, self.n_sems = list(operands), list(out_shapes), dict(aliases), n_sems
        nothing = lambda ctx: None
        self.start, self.mid1, self.mid2, self.finish = start, mid1 or nothing, mid2 or nothing, finish


class _Ctx:
    def __init__(self, ins, outs, sems, offs):
        self.ins, self.outs, self.sems = ins, outs, sems
        self.o_in, self.o_out, self.o_send, self.o_recv, self.o_loc = offs

    def inp(self, i):
        return self.ins[self.o_in + i]

    def out(self, i):
        return self.outs[self.o_out + i]

    def send(self, k):
        return self.sems[0].at[self.o_send + k]

    def recv(self, k):
        return self.sems[1].at[self.o_recv + k]

    def local(self, k):
        return self.sems[2].at[self.o_loc + k]


class _Hosted:
    def __init__(self, pieces, n_in_before, n_out_before):
        self.pieces = [p for p in (pieces or []) if p is not None]
        self.operands, self.out_shapes, self.aliases, self.offs = [], [], {}, []
        counts = [0, 0, 0]
        for p in self.pieces:
            self.offs.append((len(self.operands), len(self.out_shapes), *counts))
            for i, j in p.aliases.items():
                self.aliases[n_in_before + len(self.operands) + i] = n_out_before + len(self.out_shapes) + j
            self.operands += p.operands
            self.out_shapes += p.out_shapes
            counts = [c + n for c, n in zip(counts, p.n_sems)]
        hbm = pl.BlockSpec(memory_space=pl.ANY)
        self.in_specs = [hbm] * len(self.operands)
        self.out_specs = [hbm] * len(self.out_shapes)
        self.scratch = [pltpu.SemaphoreType.DMA((max(c, 1),)) for c in counts] if self.pieces else []

    def run(self, stage, ins, outs, sems):
        for p, offs in zip(self.pieces, self.offs):
            getattr(p, stage)(_Ctx(ins, outs, sems, offs))

    def wrap(self, grid, compute, ins, outs, sems):
        if not self.pieces:
            compute()
            return
        n_steps = math.prod(grid)
        lin = 0
        for ax, g in enumerate(grid):
            lin = lin * g + pl.program_id(ax)
        pl.when(lin == 0)(lambda: self.run("start", ins, outs, sems))
        compute()
        pl.when(lin == n_steps // 2)(lambda: self.run("mid1", ins, outs, sems))
        pl.when(lin == max(n_steps - 3, n_steps // 2))(lambda: self.run("mid2", ins, outs, sems))
        pl.when(lin == n_steps - 1)(lambda: self.run("finish", ins, outs, sems))


def _cols_view(width):
    return lambda ref, p: ref.at[:, pl.ds(pl.multiple_of(p * width, 128), width)]


def _rows_view(height):
    return lambda ref, p: ref.at[pl.ds(pl.multiple_of(p * height, 16), height), :]


def _cols_halves(rows, width):
    hr = rows // 2
    return (lambda ref, p, h: ref.at[pl.ds(h * hr, hr), pl.ds(pl.multiple_of(p * width, 128), width)],
            lambda ref, h: ref.at[pl.ds(h * hr, hr), :], 2)


def _rows_halves(height):
    hh = height // 2
    return (lambda ref, p, h: ref.at[pl.ds(pl.multiple_of(p * height + h * hh, 16), hh), :],
            lambda ref, h: ref.at[pl.ds(h * hh, hh), :], 2)


_SLOT_WHOLE = (lambda ref, p, h: ref.at[p], lambda ref, h: ref, 1)


def _ag_piece(specs):
    units = [(a, h) for a, s in enumerate(specs) for h in range(s[2][2])]

    def plan(ctx):
        x, y, c = _place()
        me, sib, xn, yn, dg = (x, y, c), (x, y, 1 - c), (1 - x, y, c), (x, 1 - y, c), (1 - x, 1 - y, c)

        def copy(u, k, block, to, from_shard=False):
            a, h = units[u]
            dst_of, src_of, _ = specs[a][2]
            dst = dst_of(ctx.out(a), _index(block), h)
            return pltpu.make_async_remote_copy(
                src_ref=src_of(ctx.inp(a), h) if from_shard else dst, dst_ref=dst, send_sem=ctx.send(7 * u + k),
                recv_sem=ctx.recv(7 * u + k), device_id=to, device_id_type=MESH)

        def local(u):
            a, h = units[u]
            dst_of, src_of, _ = specs[a][2]
            return pltpu.make_async_copy(src_of(ctx.inp(a), h), dst_of(ctx.out(a), _index(me), h), ctx.local(u))

        def relay(u):
            return copy(u, 3, xn, yn) if units[u][1] % 2 == 0 else copy(u, 3, yn, xn)

        return me, sib, xn, yn, dg, c, copy, local, relay

    def start(ctx):
        me, sib, xn, yn, dg, c, copy, local, relay = plan(ctx)
        for u in range(len(units)):
            local(u).start()
            for k, to in enumerate((sib, xn, yn)):
                copy(u, k, me, to, from_shard=True).start()

    def mid1(ctx):
        me, sib, xn, yn, dg, c, copy, local, relay = plan(ctx)
        for u in range(len(units)):
            copy(u, 1, xn, me).wait_recv()
            copy(u, 2, yn, me).wait_recv()
            relay(u).start()
            copy(u, 4, xn, sib).start()
            copy(u, 5, yn, sib).start()

    def mid2(ctx):
        me, sib, xn, yn, dg, c, copy, local, relay = plan(ctx)
        for u in range(len(units)):
            copy(u, 3, dg, me).wait_recv()
            copy(u, 6, dg, sib).start()

    def finish(ctx):
        me, sib, xn, yn, dg, c, copy, local, relay = plan(ctx)
        other = lambda place: (place[0], place[1], 1 - c)
        for u in range(len(units)):
            for k, block in ((0, sib), (4, other(xn)), (5, other(yn)), (6, other(dg))):
                copy(u, k, block, me).wait_recv()
        for u in range(len(units)):
            for k, to in enumerate((sib, xn, yn)):
                copy(u, k, me, to, from_shard=True).wait_send()
            relay(u).wait_send()
            for k, block in ((4, xn), (5, yn), (6, dg)):
                copy(u, k, block, sib).wait_send()
            local(u).wait()

    n_u = len(units)
    return _Piece([s[0] for s in specs], [s[1] for s in specs], {}, (7 * n_u, 7 * n_u, n_u), start, finish, mid1, mid2)


N_CHIPS = 4


def _rs_core_piece(specs):
    n = len(specs)

    def copies(ctx):
        x, y, c = _place()
        out = []
        for a in range(n):
            for q in range(N_CHIPS):
                out.append(pltpu.make_async_remote_copy(
                    src_ref=specs[a][2](ctx.inp(a), 2 * q + (1 - c)), dst_ref=ctx.out(a).at[q],
                    send_sem=ctx.send(N_CHIPS * a + q), recv_sem=ctx.recv(N_CHIPS * a + q), device_id=(x, y, 1 - c),
                    device_id_type=MESH))
        return out

    def start(ctx):
        for cp in copies(ctx):
            cp.start()

    def finish(ctx):
        for cp in copies(ctx):
            cp.wait_recv()
            cp.wait_send()

    return _Piece([s[0] for s in specs], [s[1] for s in specs], {}, (N_CHIPS * n, N_CHIPS * n, 0), start, finish)


def _rs_chip_piece(specs, layer, lands):
    n = len(specs)
    hops = [(1, 0), (0, 1), (1, 1)]

    def copies(ctx):
        x, y, c = _place()
        mine = 2 * x + y
        out = []
        for a in range(n):
            sums, land = ctx.inp(a), ctx.out(a)
            out.append((pltpu.make_async_copy(sums.at[mine], land.at[layer, mine], ctx.local(a)), None))
            for j, (dx, dy) in enumerate(hops):
                px, py = x ^ dx, y ^ dy
                peer = 2 * px + py
                send = pltpu.make_async_remote_copy(
                    src_ref=sums.at[peer], dst_ref=land.at[layer, mine], send_sem=ctx.send(3 * a + j),
                    recv_sem=ctx.recv(3 * a + j), device_id=(px, py, c), device_id_type=MESH)
                recv = pltpu.make_async_remote_copy(
                    src_ref=sums.at[peer], dst_ref=land.at[layer, peer], send_sem=ctx.send(3 * a + j),
                    recv_sem=ctx.recv(3 * a + j), device_id=(px, py, c), device_id_type=MESH)
                out.append((send, recv))
        return out

    def start(ctx):
        for send, _ in copies(ctx):
            send.start()

    def finish(ctx):
        for send, recv in copies(ctx):
            if recv is None:
                send.wait()
            else:
                recv.wait_recv()
                send.wait_send()

    operands = [s[0] for s in specs] + (list(lands) if lands is not None else [])
    aliases = {n + a: a for a in range(n)} if lands is not None else {}
    return _Piece(operands, [s[1] for s in specs], aliases, (3 * n, 3 * n, n), start, finish)


def _chip_sums(name, grad, stage, by_cols, core):
    _, r, c = stage.shape
    tr = r
    while tr * c > 512 * 1024 or r % tr or tr % 16:
        tr -= 16
    n_t = r // tr

    def body(core_ref, g_ref, s_ref, o_ref):
        o_ref[...] = (g_ref[...].astype(F32) + s_ref[...].astype(F32)).astype(BF16)

    if by_cols:
        gspec = pl.BlockSpec((tr, c), lambda q, i, core_ref: (i, 2 * q + core_ref[0]))
    else:
        gspec = pl.BlockSpec((tr, c), lambda q, i, core_ref: ((2 * q + core_ref[0]) * n_t + i, 0))
    sspec = pl.BlockSpec((None, tr, c), lambda q, i, core_ref: (q, i, 0))
    return pl.pallas_call(
        body, name=name, out_shape=_sds(stage.shape, BF16),
        grid_spec=pltpu.PrefetchScalarGridSpec(num_scalar_prefetch=1, grid=(N_CHIPS, n_t), in_specs=[gspec, sspec],
                                               out_specs=sspec),
        compiler_params=_cparams(2))(core, grad, stage)


def _comm_only(name, pieces):
    hosted = _Hosted(pieces, 0, 0)
    n_in, n_out = len(hosted.operands), len(hosted.out_shapes)

    def body(*refs):
        ins, outs, sems = refs[:n_in], refs[n_in:n_in + n_out], refs[n_in + n_out:]
        for stage in ("start", "mid1", "mid2", "finish"):
            hosted.run(stage, ins, outs, sems)

    return pl.pallas_call(
        body, name=name, out_shape=tuple(hosted.out_shapes), in_specs=hosted.in_specs, out_specs=tuple(hosted.out_specs),
        input_output_aliases=hosted.aliases, scratch_shapes=hosted.scratch)(*hosted.operands)


def _matmul(name, grid, nk, kaxis, pairs, dims, extras, outs, epilogue, sum_pairs, acc_shape, comm=None, split=None):
    n_p, n_e, n_o = len(pairs), len(extras), len(outs)
    n_acc = 0 if nk == 1 else (1 if sum_pairs else n_p)
    n_in = 2 * n_p + n_e
    hosted = _Hosted(comm, n_in, n_o)
    n_ci, n_co = len(hosted.operands), len(hosted.out_shapes)

    def body(*refs):
        a_refs = refs[0:2 * n_p:2]
        b_refs = refs[1:2 * n_p:2]
        e_refs = refs[2 * n_p:n_in]
        c_ins = refs[n_in:n_in + n_ci]
        o_refs = refs[n_in + n_ci:n_in + n_ci + n_o]
        c_outs = refs[n_in + n_ci + n_o:n_in + n_ci + n_o + n_co]
        acc_refs = refs[n_in + n_ci + n_o + n_co:n_in + n_ci + n_o + n_co + n_acc]
        sems = refs[n_in + n_ci + n_o + n_co + n_acc:]

        def dots():
            prods = [lax.dot_general(a[...], b[...], (dims, ((), ())), preferred_element_type=F32)
                     for a, b in zip(a_refs, b_refs)]
            if sum_pairs and n_p > 1:
                prods = [functools.reduce(operator.add, prods)]
            return prods

        def compute():
            if nk == 1 and split is not None:
                n_split, b_axis = split
                width = b_refs[0].shape[b_axis] // n_split
                for s in range(n_split):
                    cols = pl.ds(s * width, width)
                    epilogue([lax.dot_general(a[...], b[cols, :] if b_axis == 0 else b[:, cols], (dims, ((), ())),
                                              preferred_element_type=F32) for a, b in zip(a_refs, b_refs)],
                             e_refs, o_refs, cols)
                return
            if nk == 1:
                epilogue(dots(), e_refs, o_refs)
                return
            k = pl.program_id(kaxis)

            @pl.when(k == 0)
            def _():
                for acc, p in zip(acc_refs, dots()):
                    acc[...] = p

            if nk > 2:
                @pl.when((k > 0) & (k < nk - 1))
                def _():
                    for acc, p in zip(acc_refs, dots()):
                        acc[...] += p

            @pl.when(k == nk - 1)
            def _():
                epilogue([acc[...] + p for acc, p in zip(acc_refs, dots())], e_refs, o_refs)

        hosted.wrap(grid, compute, c_ins, c_outs, sems)

    operands, in_specs = [], []
    for a, a_spec, b, b_spec in pairs:
        operands += [a, b]
        in_specs += [a_spec, b_spec]
    for e, e_spec in extras:
        operands.append(e)
        in_specs.append(e_spec)
    res = pl.pallas_call(
        body, name=name, grid=grid,
        out_shape=tuple([o for o, _ in outs] + hosted.out_shapes),
        in_specs=in_specs + hosted.in_specs, out_specs=tuple([s for _, s in outs] + hosted.out_specs),
        input_output_aliases=hosted.aliases,
        scratch_shapes=[pltpu.VMEM(acc_shape, F32) for _ in range(n_acc)] + hosted.scratch,
        compiler_params=_cparams(len(grid)),
    )(*operands, *hosted.operands)
    return list(res[:n_o]), list(res[n_o:])


NN = ((1,), (0,))
NT = ((1,), (1,))
TN = ((0,), (0,))


def _tile(n, want):
    if n <= want:
        return n
    t = want // 128 * 128
    while n % t:
        t -= 128
    return t


def _silu_parts(g):
    s = 0.5 + 0.5 * jnp.tanh(0.5 * g)
    return s, g * s


def _mm_in(h, w_in, comm=None):
    t, d = h.shape
    n = w_in.shape[1]
    tm, tn = _tile(t, _TM), _tile(n, _TN)

    def epi(accs, e, o):
        o[0][...] = accs[0].astype(BF16)

    outs, couts = _matmul(
        "mm_in", (n // tn, t // tm), 1, None,
        [(h, pl.BlockSpec((tm, d), lambda j, i: (i, 0)), w_in, pl.BlockSpec((d, tn), lambda j, i: (0, j)))],
        NN, [], [(_sds((t, n), BF16), pl.BlockSpec((tm, tn), lambda j, i: (i, j)))], epi, True, None, comm)
    return outs[0], couts


def _mm_out(y, w_out, x, comm=None):
    t, m = y.shape
    d = w_out.shape[1]
    tm, tn = _tile(t, _TM), _tile(d, _TN)

    def epi(accs, e, o):
        o[0][...] = e[0][...] + accs[0]

    outs, couts = _matmul(
        "mm_out", (t // tm, d // tn), 1, None,
        [(y, pl.BlockSpec((tm, m), lambda i, j: (i, 0)), w_out, pl.BlockSpec((m, tn), lambda i, j: (0, j)))],
        NN, [(x, pl.BlockSpec((tm, tn), lambda i, j: (i, j)))],
        [(_sds((t, d), F32), pl.BlockSpec((tm, tn), lambda i, j: (i, j)))], epi, True, None, comm)
    return outs[0], couts


def _mm_swiglu(h2, wgt, wut, comm=None):
    t, d = h2.shape
    f = wgt.shape[0]
    tm, tn = _tile(t, _TM), _tile(f, 512)

    def epi(accs, e, o, cols):
        g, u = accs
        _, sg = _silu_parts(g)
        o[0][:, cols] = g.astype(BF16)
        o[1][:, cols] = u.astype(BF16)
        o[2][:, cols] = (sg * u).astype(BF16)

    wspec = pl.BlockSpec((tn, d), lambda j, i: (j, 0))
    hspec = pl.BlockSpec((tm, d), lambda j, i: (i, 0))
    ospec = pl.BlockSpec((tm, tn), lambda j, i: (i, j))
    osh = _sds((t, f), BF16)
    outs, couts = _matmul("mm_swiglu", (f // tn, t // tm), 1, None, [(h2, hspec, wgt, wspec), (h2, hspec, wut, wspec)],
                          NT, [], [(osh, ospec)] * 3, epi, False, None, comm, split=(tn // 256, 0))
    return outs, couts


def _mm_down(act, wd, x1, comm=None):
    t, f = act.shape
    d = wd.shape[1]
    tm, tn = _tile(t, _TM), _tile(d, _TN)
    nk = 2
    tk = f // nk

    def epi(accs, e, o):
        o[0][...] = e[0][...] + accs[0]

    outs, couts = _matmul(
        "mm_down", (t // tm, d // tn, nk), nk, 2,
        [(act, pl.BlockSpec((tm, tk), lambda i, j, k: (i, k)), wd, pl.BlockSpec((tk, tn), lambda i, j, k: (k, j)))],
        NN, [(x1, pl.BlockSpec((tm, tn), lambda i, j, k: (i, j)))],
        [(_sds((t, d), F32), pl.BlockSpec((tm, tn), lambda i, j, k: (i, j)))], epi, True, (tm, tn), comm)
    return outs[0], couts


def _mm_dact(dxb, wd, gate, up, comm=None):
    t, d = dxb.shape
    f = wd.shape[0]
    tm, tn = _tile(t, _TM), _tile(f, 512)

    def epi(accs, e, o, cols):
        da = accs[0]
        g = e[0][:, cols].astype(F32)
        u = e[1][:, cols].astype(F32)
        s, sg = _silu_parts(g)
        o[0][:, cols] = (da * u * (s + sg * (1.0 - s))).astype(BF16)
        o[1][:, cols] = (da * sg).astype(BF16)

    bspec = pl.BlockSpec((tm, tn), lambda j, i: (i, j))
    osh = _sds((t, f), BF16)
    outs, couts = _matmul(
        "mm_dact", (f // tn, t // tm), 1, None,
        [(dxb, pl.BlockSpec((tm, d), lambda j, i: (i, 0)), wd, pl.BlockSpec((tn, d), lambda j, i: (j, 0)))],
        NT, [(gate, bspec), (up, bspec)], [(osh, bspec)] * 2, epi, True, None, comm, split=(tn // 256, 0))
    return outs, couts


def _mm_dh2(dgate, dup, wgt, wut, comm=None):
    t, f = dgate.shape
    d = wgt.shape[1]
    tm, tn = _tile(t, _TM), _tile(d, _TN)
    nk = 4
    tk = f // nk

    def epi(accs, e, o):
        o[0][...] = accs[0]

    aspec = pl.BlockSpec((tm, tk), lambda i, j, k: (i, k))
    wspec = pl.BlockSpec((tk, tn), lambda i, j, k: (k, j))
    outs, couts = _matmul("mm_dh2", (t // tm, d // tn, nk), nk, 2, [(dgate, aspec, wgt, wspec), (dup, aspec, wut, wspec)],
                          NN, [], [(_sds((t, d), F32), pl.BlockSpec((tm, tn), lambda i, j, k: (i, j)))], epi, True,
                          (tm, tn), comm)
    return outs[0], couts


def _mm_dw(name, a_list, b, tmo, tno, comm=None):
    t, m = a_list[0].shape
    n = b.shape[1]
    tt = _tile(t, _TT)
    nk = t // tt
    tmo, tno = _tile(m, tmo), _tile(n, tno)

    def epi(accs, e, o):
        for acc, out in zip(accs, o):
            out[...] = acc.astype(BF16)

    aspec = pl.BlockSpec((tt, tmo), lambda i, j, k: (k, i))
    bspec = pl.BlockSpec((tt, tno), lambda i, j, k: (k, j))
    ospec = pl.BlockSpec((tmo, tno), lambda i, j, k: (i, j))
    if nk == 1:
        return _matmul(name, (m // tmo, n // tno, 1), 1, None, [(a, aspec, b, bspec) for a in a_list], TN, [],
                       [(_sds((m, n), BF16), ospec)] * len(a_list), epi, False, None, comm)
    return _matmul(name, (m // tmo, n // tno, nk), nk, 2, [(a, aspec, b, bspec) for a in a_list], TN, [],
                   [(_sds((m, n), BF16), ospec)] * len(a_list), epi, False, (tmo, tno), comm)


def _mm_dy(dxb, w_out, comm=None):
    t, d = dxb.shape
    m = w_out.shape[0]
    tm, tn = _tile(t, _TM), _tile(m, _TN)

    def epi(accs, e, o):
        o[0][...] = accs[0].astype(BF16)

    outs, couts = _matmul(
        "mm_dy", (t // tm, m // tn), 1, None,
        [(dxb, pl.BlockSpec((tm, d), lambda i, j: (i, 0)), w_out, pl.BlockSpec((tn, d), lambda i, j: (j, 0)))], NT, [],
        [(_sds((t, m), BF16), pl.BlockSpec((tm, tn), lambda i, j: (i, j)))], epi, True, None, comm)
    return outs[0], couts


def _mm_dh(dz, w_in, comm=None):
    t, n = dz.shape
    d = w_in.shape[0]
    tm, tn = _tile(t, _TM), _tile(d, _TN)
    nk = 2
    tk = n // nk

    def epi(accs, e, o):
        o[0][...] = accs[0]

    outs, couts = _matmul(
        "mm_dh", (t // tm, d // tn, nk), nk, 2,
        [(dz, pl.BlockSpec((tm, tk), lambda i, j, k: (i, k)), w_in, pl.BlockSpec((tn, tk), lambda i, j, k: (j, k)))], NT,
        [], [(_sds((t, d), F32), pl.BlockSpec((tm, tn), lambda i, j, k: (i, j)))], epi, True, (tm, tn), comm)
    return outs[0], couts


def _rmsnorm_fwd(x, g):
    t, d = x.shape
    tm = min(_TM_NORM, t)

    def body(x_ref, g_ref, o_ref):
        xv = x_ref[...]
        rs = lax.rsqrt(jnp.mean(xv * xv, axis=-1, keepdims=True) + RMS_EPS)
        o_ref[...] = (xv * rs * g_ref[...]).astype(BF16)

    return pl.pallas_call(
        body, name="rmsnorm_fwd", grid=(t // tm,), out_shape=_sds((t, d), BF16),
        in_specs=[pl.BlockSpec((tm, d), lambda i: (i, 0)), pl.BlockSpec((1, d), lambda i: (0, 0))],
        out_specs=pl.BlockSpec((tm, d), lambda i: (i, 0)), compiler_params=_cparams(1))(x, g)


def _rmsnorm_bwd_math(xv, g, dh):
    rs = lax.rsqrt(jnp.mean(xv * xv, axis=-1, keepdims=True) + RMS_EPS)
    xh = xv * rs
    gd = dh * g
    dx = rs * (gd - xh * jnp.mean(gd * xh, axis=-1, keepdims=True))
    return dx, jnp.sum(dh * xh, axis=0, keepdims=True)


def _rmsnorm_bwd(x, g, dh, dres):
    t, d = x.shape
    tm = min(_TM_NORM, t)

    def body(x_ref, g_ref, dh_ref, dres_ref, dx_ref, dxb_ref, dg_ref):
        dx, dg = _rmsnorm_bwd_math(x_ref[...], g_ref[...], dh_ref[...])
        dx = dx + dres_ref[...]
        dx_ref[...] = dx
        dxb_ref[...] = dx.astype(BF16)

        @pl.when(pl.program_id(0) == 0)
        def _():
            dg_ref[...] = dg

        @pl.when(pl.program_id(0) > 0)
        def _():
            dg_ref[...] += dg

    row = pl.BlockSpec((tm, d), lambda i: (i, 0))
    vec = pl.BlockSpec((1, d), lambda i: (0, 0))
    return pl.pallas_call(
        body, name="rmsnorm_bwd", grid=(t // tm,),
        out_shape=(_sds((t, d), F32), _sds((t, d), BF16), _sds((1, d), F32)),
        in_specs=[row, vec, row, row], out_specs=(row, row, vec), compiler_params=_cparams(1))(x, g, dh, dres)


def _loss_head(x, g, target):
    t, d = x.shape
    tm = min(_TM_NORM, t)

    def body(x_ref, g_ref, t_ref, dx_ref, dxb_ref, dg_ref, loss_ref):
        xv, gv = x_ref[...], g_ref[...]
        rs = lax.rsqrt(jnp.mean(xv * xv, axis=-1, keepdims=True) + RMS_EPS)
        diff = xv * rs * gv - t_ref[...]
        part = 0.5 * jnp.sum(jnp.mean(diff * diff, axis=-1, keepdims=True), axis=0, keepdims=True)
        part = jnp.broadcast_to(part, (1, 128))
        dx, dg = _rmsnorm_bwd_math(xv, gv, diff * (1.0 / d))
        dx_ref[...] = dx
        dxb_ref[...] = dx.astype(BF16)

        @pl.when(pl.program_id(0) == 0)
        def _():
            dg_ref[...] = dg
            loss_ref[...] = part

        @pl.when(pl.program_id(0) > 0)
        def _():
            dg_ref[...] += dg
            loss_ref[...] += part

    row = pl.BlockSpec((tm, d), lambda i: (i, 0))
    vec = pl.BlockSpec((1, d), lambda i: (0, 0))
    return pl.pallas_call(
        body, name="loss_head", grid=(t // tm,),
        out_shape=(_sds((t, d), F32), _sds((t, d), BF16), _sds((1, d), F32), _sds((1, 128), F32)),
        in_specs=[row, vec, row], out_specs=(row, row, vec, pl.BlockSpec((1, 128), lambda i: (0, 0))),
        compiler_params=_cparams(1))(x, g, target)


def _gelu(x):
    th = jnp.tanh(GELU_C * (x + GELU_A * x * x * x))
    return 0.5 * x * (1.0 + th), th


def _gelu_grad(x, th):
    return 0.5 * (1.0 + th) + 0.5 * x * (1.0 - th * th) * GELU_C * (1.0 + 3.0 * GELU_A * x * x)


def _masked_ws(ws_ref, h):
    i = lax.broadcasted_iota(jnp.int32, (BLK, BLK), 0) // CHUNK
    j = lax.broadcasted_iota(jnp.int32, (BLK, BLK), 1) // CHUNK
    return jnp.where(j <= i, ws_ref[h], 0.0)


def _shift_down(q, n, first_rows):
    rolled = pltpu.roll(q, n, 0)
    row = lax.broadcasted_iota(jnp.int32, q.shape, 0)
    for r, val in enumerate(first_rows):
        rolled = jnp.where(row == r, val, rolled)
    return rolled


def _shift_up(q, n, last_rows):
    tm = q.shape[0]
    rolled = pltpu.roll(q, tm - n, 0)
    row = lax.broadcasted_iota(jnp.int32, q.shape, 0)
    for r, val in enumerate(last_rows):
        rolled = jnp.where(row == tm - n + r, val, rolled)
    return rolled


def _mixer_specs(t, a, tm):
    hb = tm // HALO
    last = t // HALO - 1
    tile = pl.BlockSpec((tm, 5 * a), lambda i: (i, 0))
    prev = [pl.BlockSpec((HALO, a), functools.partial(lambda i, col: (jnp.maximum(i * hb - 1, 0), col), col=col))
            for col in (3, 4)]
    nxt = [pl.BlockSpec((HALO, a), functools.partial(lambda i, col: (jnp.minimum((i + 1) * hb, last), col), col=col))
           for col in (2, 3, 4)]
    return tile, prev, nxt


def _group_a_fwd(zu, zv, lng, lnb, ws_ref, bb_ref, mixed_ref, vln_ref):
    u, thu = _gelu(zu)
    v, thv = _gelu(zv)
    mu = jnp.mean(v, axis=-1, keepdims=True)
    vc = v - mu
    rs = lax.rsqrt(jnp.mean(vc * vc, axis=-1, keepdims=True) + LN_EPS)
    vhat = vc * rs
    vln_ref[...] = vhat * lng + lnb
    tm, a = zu.shape
    hd = a // HEADS
    for h in range(HEADS):
        w = _masked_ws(ws_ref, h).astype(BF16)
        for b in range(tm // BLK):
            rows, cols = pl.ds(b * BLK, BLK), pl.ds(h * hd, hd)
            mixed_ref[rows, cols] = jnp.dot(w, vln_ref[rows, cols].astype(BF16), preferred_element_type=F32) + bb_ref[h]
    return u, thu, thv, rs, vhat


def _mixer_fwd(z, ln_g, ln_b, w_spatial, bb, conv_w, gg):
    t = z.shape[0]
    a = z.shape[1] // 5
    tm = min(_TM_MIX, t)
    tile, prev, _ = _mixer_specs(t, a, tm)

    def body(z_ref, pc_ref, ph_ref, lng_ref, lnb_ref, ws_ref, bb_ref, cw_ref, gg_ref, y_ref, mixed_ref, vln_ref):
        i = pl.program_id(0)
        zu = z_ref[:, 0:a].astype(F32)
        zv = z_ref[:, a:2 * a].astype(F32)
        u, _, _, _, _ = _group_a_fwd(zu, zv, lng_ref[...], lnb_ref[...], ws_ref, bb_ref, mixed_ref, vln_ref)
        ya = u * mixed_ref[...]
        ra = lax.rsqrt(jnp.mean(ya * ya, axis=-1, keepdims=True) + RMS_EPS)
        y_ref[:, 0:a] = (ya * ra * gg_ref[:, 0:a]).astype(BF16)

        zb = z_ref[:, 2 * a:3 * a].astype(F32)
        q = z_ref[:, 3 * a:4 * a].astype(F32) * z_ref[:, 4 * a:5 * a].astype(F32)
        qp = jnp.where(i > 0, pc_ref[...].astype(F32) * ph_ref[...].astype(F32), 0.0)
        qm1 = _shift_down(q, 1, [qp[HALO - 1:HALO]])
        qm2 = _shift_down(q, 2, [qp[HALO - 2:HALO - 1], qp[HALO - 1:HALO]])
        cv = cw_ref[0:1, :] * qm2 + cw_ref[1:2, :] * qm1 + cw_ref[2:3, :] * q
        yb = zb * cv
        rb = lax.rsqrt(jnp.mean(yb * yb, axis=-1, keepdims=True) + RMS_EPS)
        y_ref[:, a:2 * a] = (yb * rb * gg_ref[:, a:2 * a]).astype(BF16)

    full = lambda shape: pl.BlockSpec(shape, lambda i: (0,) * len(shape))
    return pl.pallas_call(
        body, name="mixer_fwd", grid=(t // tm,), out_shape=_sds((t, 2 * a), BF16),
        in_specs=[tile, *prev, full((1, a)), full((1, a)), full(w_spatial.shape), full(bb.shape), full(conv_w.shape),
                  full((1, 2 * a))],
        out_specs=pl.BlockSpec((tm, 2 * a), lambda i: (i, 0)),
        scratch_shapes=[pltpu.VMEM((tm, a), F32), pltpu.VMEM((tm, a), F32)],
        compiler_params=_cparams(1))(z, z, z, ln_g, ln_b, w_spatial, bb, conv_w, gg)


def _mixer_bwd(z, dy, ln_g, ln_b, w_spatial, bb, conv_w, gg):
    t = z.shape[0]
    a = z.shape[1] // 5
    hd = a // HEADS
    tm = min(_TM_MIX, t)
    n_tiles = t // tm
    tile, prev, nxt = _mixer_specs(t, a, tm)
    hb = tm // HALO
    dy_tile = pl.BlockSpec((tm, 2 * a), lambda i: (i, 0))
    dy_next = pl.BlockSpec((HALO, a), lambda i: (jnp.minimum((i + 1) * hb, t // HALO - 1), 1))

    def body(z_ref, pc_ref, ph_ref, nb_ref, nc_ref, nh_ref, dy_ref, ndy_ref, lng_ref, lnb_ref, ws_ref, bb_ref, cw_ref,
             gg_ref, dz_ref, dlng_ref, dlnb_ref, dws_ref, dbb_ref, dcw_ref, dgg_ref, mixed_ref, vln_ref, dmix_ref,
             dvln_ref):
        i = pl.program_id(0)

        @pl.when(i == 0)
        def _():
            for ref in (dlng_ref, dlnb_ref, dws_ref, dbb_ref, dcw_ref, dgg_ref):
                ref[...] = jnp.zeros(ref.shape, F32)

        lng = lng_ref[...]
        zu = z_ref[:, 0:a].astype(F32)
        zv = z_ref[:, a:2 * a].astype(F32)
        u, thu, thv, rs, vhat = _group_a_fwd(zu, zv, lng, lnb_ref[...], ws_ref, bb_ref, mixed_ref, vln_ref)
        mixed = mixed_ref[...]
        ya = u * mixed
        ra = lax.rsqrt(jnp.mean(ya * ya, axis=-1, keepdims=True) + RMS_EPS)
        da = dy_ref[:, 0:a].astype(F32)
        yah = ya * ra
        dgg_ref[:, 0:a] += jnp.sum(da * yah, axis=0, keepdims=True)
        ga = da * gg_ref[:, 0:a]
        dya = ra * (ga - yah * jnp.mean(ga * yah, axis=-1, keepdims=True))
        dz_ref[:, 0:a] = (dya * mixed * _gelu_grad(zu, thu)).astype(BF16)
        dmix_ref[...] = dya * u
        for h in range(HEADS):
            w = _masked_ws(ws_ref, h).astype(BF16)
            dw = jnp.zeros((BLK, BLK), F32)
            db = jnp.zeros((BLK, hd), F32)
            for b in range(tm // BLK):
                rows, cols = pl.ds(b * BLK, BLK), pl.ds(h * hd, hd)
                dm = dmix_ref[rows, cols]
                dmb = dm.astype(BF16)
                db = db + dm
                dw = dw + lax.dot_general(dmb, vln_ref[rows, cols].astype(BF16), (NT, ((), ())),
                                          preferred_element_type=F32)
                dvln_ref[rows, cols] = lax.dot_general(w, dmb, (TN, ((), ())), preferred_element_type=F32)
            dws_ref[h] += dw
            dbb_ref[h] += db
        dvln = dvln_ref[...]
        dlng_ref[...] += jnp.sum(dvln * vhat, axis=0, keepdims=True)
        dlnb_ref[...] += jnp.sum(dvln, axis=0, keepdims=True)
        dvh = dvln * lng
        dv = rs * (dvh - jnp.mean(dvh, axis=-1, keepdims=True) - vhat * jnp.mean(dvh * vhat, axis=-1, keepdims=True))
        dz_ref[:, a:2 * a] = (dv * _gelu_grad(zv, thv)).astype(BF16)

        w0, w1, w2 = cw_ref[0:1, :], cw_ref[1:2, :], cw_ref[2:3, :]
        ggb = gg_ref[:, a:2 * a]
        zb = z_ref[:, 2 * a:3 * a].astype(F32)
        zc = z_ref[:, 3 * a:4 * a].astype(F32)
        zh = z_ref[:, 4 * a:5 * a].astype(F32)
        q = zc * zh
        qp = jnp.where(i > 0, pc_ref[...].astype(F32) * ph_ref[...].astype(F32), 0.0)
        qm1 = _shift_down(q, 1, [qp[HALO - 1:HALO]])
        qm2 = _shift_down(q, 2, [qp[HALO - 2:HALO - 1], qp[HALO - 1:HALO]])
        cv = w0 * qm2 + w1 * qm1 + w2 * q

        def conv_out_grad(zb_, cv_, dout_):
            yb = zb_ * cv_
            rb = lax.rsqrt(jnp.mean(yb * yb, axis=-1, keepdims=True) + RMS_EPS)
            ybh = yb * rb
            gb = dout_ * ggb
            dyb = rb * (gb - ybh * jnp.mean(gb * ybh, axis=-1, keepdims=True))
            return dyb * zb_, dyb * cv_, ybh

        db_out = dy_ref[:, a:2 * a].astype(F32)
        g, dzb, ybh = conv_out_grad(zb, cv, db_out)
        dgg_ref[:, a:2 * a] += jnp.sum(db_out * ybh, axis=0, keepdims=True)
        dz_ref[:, 2 * a:3 * a] = dzb.astype(BF16)
        qn = nc_ref[...].astype(F32) * nh_ref[...].astype(F32)
        zbn = nb_ref[...].astype(F32)
        cvn = w0 * _shift_down(qn, 2, [q[tm - 2:tm - 1], q[tm - 1:tm]]) + w1 * _shift_down(qn, 1, [q[tm - 1:tm]]) + w2 * qn
        gn, _, _ = conv_out_grad(zbn, cvn, ndy_ref[...].astype(F32))
        gn = jnp.where(i < n_tiles - 1, gn, 0.0)
        dq = w2 * g + w1 * _shift_up(g, 1, [gn[0:1]]) + w0 * _shift_up(g, 2, [gn[0:1], gn[1:2]])
        dz_ref[:, 3 * a:4 * a] = (dq * zh).astype(BF16)
        dz_ref[:, 4 * a:5 * a] = (dq * zc).astype(BF16)
        dcw_ref[0:1, :] += jnp.sum(g * qm2, axis=0, keepdims=True)
        dcw_ref[1:2, :] += jnp.sum(g * qm1, axis=0, keepdims=True)
        dcw_ref[2:3, :] += jnp.sum(g * q, axis=0, keepdims=True)

        @pl.when(i == n_tiles - 1)
        def _():
            for h in range(HEADS):
                dbb_ref[h] = jnp.broadcast_to(jnp.sum(dbb_ref[h], axis=1, keepdims=True), (BLK, hd))
                dws_ref[h] = _masked_ws(dws_ref, h)

    full = lambda shape: pl.BlockSpec(tuple(shape), lambda i: (0,) * len(shape))
    out_shapes = (_sds((t, 5 * a), BF16), _sds((1, a), F32), _sds((1, a), F32), _sds(w_spatial.shape, F32),
                  _sds(bb.shape, F32), _sds((8, a), F32), _sds((1, 2 * a), F32))
    return pl.pallas_call(
        body, name="mixer_bwd", grid=(n_tiles,), out_shape=out_shapes,
        in_specs=[tile, *prev, *nxt, dy_tile, dy_next, full((1, a)), full((1, a)), full(w_spatial.shape), full(bb.shape),
                  full(conv_w.shape), full((1, 2 * a))],
        out_specs=(tile, *[full(s.shape) for s in out_shapes[1:]]),
        scratch_shapes=[pltpu.VMEM((tm, a), F32)] * 4,
        compiler_params=_cparams(1))(z, z, z, z, z, z, dy, dy, ln_g, ln_b, w_spatial, bb, conv_w, gg)


def _all_reduce_small(pack, comm=None):
    r = pack.shape[0]
    hosted = _Hosted(comm, 1, 1)
    n_ci, n_co = len(hosted.operands), len(hosted.out_shapes)

    def body(*refs):
        in_ref, c_ins, out_ref, c_outs = refs[0], refs[1:1 + n_ci], refs[1 + n_ci], refs[2 + n_ci:2 + n_ci + n_co]
        acc_ref, recv_ref, send_sems, recv_sems = refs[2 + n_ci + n_co:6 + n_ci + n_co]
        sems = refs[6 + n_ci + n_co:]
        hosted.run("start", c_ins, c_outs, sems)
        x, y, c = _place()
        partners = [(x, y, 1 - c), (1 - x, y, c), (x, 1 - y, c)]
        acc_ref[0] = in_ref[...]
        for s, partner in enumerate(partners):
            cp = pltpu.make_async_remote_copy(
                src_ref=acc_ref.at[s], dst_ref=recv_ref.at[s], send_sem=send_sems.at[s], recv_sem=recv_sems.at[s],
                device_id=partner, device_id_type=MESH)
            cp.start()
            cp.wait()
            if s < 2:
                acc_ref[s + 1] = acc_ref[s] + recv_ref[s]
            else:
                out_ref[...] = acc_ref[s] + recv_ref[s]
        for stage in ("mid1", "mid2", "finish"):
            hosted.run(stage, c_ins, c_outs, sems)

    vmem = pl.BlockSpec(memory_space=pltpu.VMEM)
    res = pl.pallas_call(
        body, name="all_reduce_small", out_shape=tuple([_sds(pack.shape, F32)] + hosted.out_shapes),
        in_specs=[vmem] + hosted.in_specs, out_specs=tuple([vmem] + hosted.out_specs),
        input_output_aliases=hosted.aliases,
        scratch_shapes=[pltpu.VMEM((3, r, 128), F32), pltpu.VMEM((3, r, 128), F32), pltpu.SemaphoreType.DMA((3,)),
                        pltpu.SemaphoreType.DMA((3,))] + hosted.scratch,
        compiler_params=pltpu.CompilerParams(vmem_limit_bytes=VMEM_LIMIT_V7X),
    )(pack, *hosted.operands)
    return res[0], list(res[1:])


def _adamw_math(w, g, m, v):
    m = ADAM_B1 * m + (1.0 - ADAM_B1) * g
    v = ADAM_B2 * v + (1.0 - ADAM_B2) * (g * g)
    m_hat = m / (1.0 - ADAM_B1 ** ADAM_STEP)
    v_hat = v / (1.0 - ADAM_B2 ** ADAM_STEP)
    delta = -ADAM_LR * (m_hat / (jnp.sqrt(v_hat) + ADAM_EPS) + ADAM_WD * w)
    return delta, m, v


def _adamw_big(name, land, w, m, v, comm=None):
    nl, n_slots, r, c = land.shape
    tr = max(8, min(r, (256 * 640) // c // 8 * 8))
    while r % tr:
        tr -= 8
    grid = (nl, r // tr)
    hosted = _Hosted(comm, 4, 4)
    n_ci, n_co = len(hosted.operands), len(hosted.out_shapes)

    def body(*refs):
        land_ref, w_ref, m_ref, v_ref = refs[:4]
        c_ins = refs[4:4 + n_ci]
        g_out, d_out, m_out, v_out = refs[4 + n_ci:8 + n_ci]
        c_outs = refs[8 + n_ci:8 + n_ci + n_co]
        sems = refs[8 + n_ci + n_co:]

        def compute():
            g = land_ref[0].astype(F32)
            for s in range(1, n_slots):
                g = g + land_ref[s].astype(F32)
            delta, mn, vn = _adamw_math(w_ref[...], g, m_ref[...], v_ref[...])
            g_out[...] = g
            d_out[...] = delta
            m_out[...] = mn
            v_out[...] = vn

        hosted.wrap(grid, compute, c_ins, c_outs, sems)

    blk = pl.BlockSpec((None, tr, c), lambda l, i: (l, i, 0))
    res = pl.pallas_call(
        body, name=name, grid=grid, out_shape=tuple([_sds((nl, r, c), F32)] * 4 + hosted.out_shapes),
        in_specs=[pl.BlockSpec((None, n_slots, tr, c), lambda l, i: (l, 0, i, 0)), blk, blk, blk] + hosted.in_specs,
        out_specs=tuple([blk] * 4 + hosted.out_specs), input_output_aliases=hosted.aliases,
        scratch_shapes=hosted.scratch, compiler_params=_cparams(2))(land, w, m, v, *hosted.operands)
    return list(res[:4]), list(res[4:])


def _adamw_small(g, w, m, v):
    def body(g_ref, w_ref, m_ref, v_ref, d_out, m_out, v_out):
        delta, mn, vn = _adamw_math(w_ref[...], g_ref[...], m_ref[...], v_ref[...])
        d_out[...] = delta
        m_out[...] = mn
        v_out[...] = vn

    return pl.pallas_call(body, name="adamw_small", out_shape=tuple([_sds(g.shape, F32)] * 3),
                          compiler_params=pltpu.CompilerParams(vmem_limit_bytes=VMEM_LIMIT_V7X))(g, w, m, v)


def _rows(a):
    return a.reshape(-1, 128)


BIG = ["w_in", "w_out", "w_gate", "w_up", "w_down"]
AG_HOSTS = {
    ("mm_in", 0): [("w_out", 0), ("w_gate", 0)], ("mm_out", 0): [("w_up", 0)],
    ("mm_swiglu", 0): [("w_down", 0), ("w_in", 1)], ("mm_down", 0): [("w_out", 1), ("w_gate", 1)],
    ("mm_in", 1): [("w_up", 1)], ("mm_swiglu", 1): [("w_down", 1)],
}


def kernel(x, norm1_g, w_in, gmlp_ln_g, gmlp_ln_b, w_spatial, b_spatial, conv_w, group_norm_g, w_out, norm2_g, w_gate, w_up, w_down, final_norm_g, loss_target, m_norm1_g, m_w_in, m_gmlp_ln_g, m_gmlp_ln_b, m_w_spatial, m_b_spatial, m_conv_w, m_group_norm_g, m_w_out, m_norm2_g, m_w_gate, m_w_up, m_w_down, m_final_norm_g, v_norm1_g, v_w_in, v_gmlp_ln_g, v_gmlp_ln_b, v_w_spatial, v_b_spatial, v_conv_w, v_group_norm_g, v_w_out, v_norm2_g, v_w_gate, v_w_up, v_w_down, v_final_norm_g):
    nl = N_LAYERS
    t, d = x.shape[1], x.shape[2]
    a = d // 2
    hd = a // HEADS
    xin = x.reshape(t, d)
    target = loss_target.reshape(t, d)
    me = _index(_place())

    tr = lambda w: jnp.transpose(w, (0, 2, 1))
    big = {"w_in": w_in, "w_out": w_out, "w_gate": tr(w_gate), "w_up": tr(w_up), "w_down": w_down}
    big_m = {"w_in": m_w_in, "w_out": m_w_out, "w_gate": tr(m_w_gate), "w_up": tr(m_w_up), "w_down": m_w_down}
    big_v = {"w_in": v_w_in, "w_out": v_w_out, "w_gate": tr(v_w_gate), "w_up": tr(v_w_up), "w_down": v_w_down}
    block = {k: big[k].shape[1:] for k in BIG}
    view = {k: _cols_view(block[k][1]) if k == "w_in" else _rows_view(block[k][0]) for k in BIG}
    full_shape = {k: (block[k][0], N_DEV * block[k][1]) if k == "w_in" else (N_DEV * block[k][0], block[k][1])
                  for k in BIG}

    halves = {k: _cols_halves(*block[k]) if k == "w_in" else _rows_halves(block[k][0]) for k in BIG}

    def ag_spec(k, l):
        return (big[k][l].astype(BF16), _sds(full_shape[k], BF16), halves[k])

    first = _comm_only("all_gather_first", [_ag_piece([ag_spec("w_in", 0), (conv_w, _sds((N_DEV, *conv_w.shape), F32),
                                                                             _SLOT_WHOLE)])])
    weights = {("w_in", 0): first[0]}
    conv_full = jnp.transpose(first[1], (1, 2, 0, 3)).reshape(nl, 3, a)
    bb = jnp.broadcast_to(b_spatial[..., None], (nl, HEADS, BLK, hd))

    def hosted(name, l):
        keys = AG_HOSTS.get((name, l), [])
        return keys, ([_ag_piece([ag_spec(k, kl) for k, kl in keys])] if keys else None)

    def landed(keys, couts):
        for key, arr in zip(keys, couts):
            weights[key] = arr

    saved = []
    xl = xin
    for l in range(nl):
        h = _rmsnorm_fwd(xl, norm1_g[l:l + 1])
        keys, comm = hosted("mm_in", l)
        z, couts = _mm_in(h, weights[("w_in", l)], comm)
        landed(keys, couts)
        y = _mixer_fwd(z, gmlp_ln_g[l:l + 1], gmlp_ln_b[l:l + 1], w_spatial[l], bb[l], conv_full[l],
                       group_norm_g[l:l + 1])
        keys, comm = hosted("mm_out", l)
        x1, couts = _mm_out(y, weights[("w_out", l)], xl, comm)
        landed(keys, couts)
        h2 = _rmsnorm_fwd(x1, norm2_g[l:l + 1])
        keys, comm = hosted("mm_swiglu", l)
        (gate, up, act), couts = _mm_swiglu(h2, weights[("w_gate", l)], weights[("w_up", l)], comm)
        landed(keys, couts)
        keys, comm = hosted("mm_down", l)
        x2, couts = _mm_down(act, weights[("w_down", l)], x1, comm)
        landed(keys, couts)
        saved.append(dict(x=xl, h=h, z=z, y=y, x1=x1, h2=h2, gate=gate, up=up, act=act))
        xl = x2

    dx, dxb, d_final_g, loss_part = _loss_head(xl, final_norm_g.reshape(1, d), target)
    small = [None] * nl
    core = lax.axis_index("c").astype(jnp.int32).reshape(1)
    stage_shape = {k: _sds((N_CHIPS, *block[k]), BF16) for k in BIG}
    land_shape = {k: _sds((nl, N_CHIPS, *block[k]), BF16) for k in BIG}
    grads = [dict() for _ in range(nl)]
    stages = [dict() for _ in range(nl)]
    sums = [dict() for _ in range(nl)]
    lands = {k: None for k in BIG}

    def core_job(l, keys):
        def sink(outs):
            stages[l].update(zip(keys, outs))
        return _rs_core_piece([(grads[l][k], stage_shape[k], view[k]) for k in keys]), sink

    def chip_job(l, keys):
        def sink(outs):
            lands.update(zip(keys, outs))
        return _rs_chip_piece([(sums[l][k], land_shape[k]) for k in keys], l,
                              None if lands[keys[0]] is None else [lands[k] for k in keys]), sink

    def add_up(l, keys):
        for k in keys:
            sums[l][k] = _chip_sums(f"chip_sums_{k}", grads[l][k], stages[l][k], k == "w_in", core)

    def host(*jobs):
        def deliver(couts):
            i = 0
            for piece, sink in jobs:
                n_out = len(piece.out_shapes)
                sink(couts[i:i + n_out])
                i += n_out
        return [piece for piece, _ in jobs], deliver

    for l in reversed(range(nl)):
        s = saved[l]
        wi, wo, wgt, wut, wd = [weights[(k, l)] for k in BIG]
        later = l + 1 < nl
        comm, deliver = host(chip_job(l + 1, ["w_out", "w_in"])) if later else host()
        (dgate, dup), couts = _mm_dact(dxb, wd, s["gate"], s["up"], comm)
        deliver(couts)
        (grads[l]["w_down"],), _ = _mm_dw("mm_dw_down", [s["act"]], dxb, 2816, 1024)
        comm, deliver = host(core_job(l, ["w_down"]))
        dh2, couts = _mm_dh2(dgate, dup, wgt, wut, comm)
        deliver(couts)
        add_up(l, ["w_down"])
        comm, deliver = host(chip_job(l, ["w_down"]))
        (grads[l]["w_gate"], grads[l]["w_up"]), couts = _mm_dw("mm_dw_gate_up", [dgate, dup], s["h2"], 1408, 1024, comm)
        deliver(couts)
        dx1, dx1b, d_n2 = _rmsnorm_bwd(s["x1"], norm2_g[l:l + 1], dh2, dx)
        comm, deliver = host(core_job(l, ["w_gate", "w_up"]))
        dy, couts = _mm_dy(dx1b, wo, comm)
        deliver(couts)
        add_up(l, ["w_gate", "w_up"])
        (grads[l]["w_out"],), _ = _mm_dw("mm_dw_out", [s["y"]], dx1b, 1024, 1024)
        dz, d_lng, d_lnb, d_ws, d_bb, d_cw, d_gg = _mixer_bwd(
            s["z"], dy, gmlp_ln_g[l:l + 1], gmlp_ln_b[l:l + 1], w_spatial[l], bb[l], conv_full[l], group_norm_g[l:l + 1])
        comm, deliver = host(chip_job(l, ["w_gate"]), core_job(l, ["w_out"]))
        (grads[l]["w_in"],), couts = _mm_dw("mm_dw_in", [s["h"]], dz, 2048, 1024, comm)
        deliver(couts)
        comm, deliver = host(chip_job(l, ["w_up"]), core_job(l, ["w_in"]))
        dh, couts = _mm_dh(dz, wi, comm)
        deliver(couts)
        add_up(l, ["w_out", "w_in"])
        dx, dxb, d_n1 = _rmsnorm_bwd(s["x"], norm1_g[l:l + 1], dh, dx1)
        small[l] = dict(norm1_g=d_n1, gmlp_ln_g=d_lng, gmlp_ln_b=d_lnb, w_spatial=d_ws, b_spatial=d_bb[:, :, 0],
                        group_norm_g=d_gg, norm2_g=d_n2, conv_w=d_cw[0:3])
    grad_x = dx.reshape(x.shape)

    rep = ["norm1_g", "gmlp_ln_g", "gmlp_ln_b", "w_spatial", "b_spatial", "group_norm_g", "norm2_g"]
    rep_w = dict(norm1_g=norm1_g, gmlp_ln_g=gmlp_ln_g, gmlp_ln_b=gmlp_ln_b, w_spatial=w_spatial, b_spatial=b_spatial,
                 group_norm_g=group_norm_g, norm2_g=norm2_g)
    rep_m = dict(norm1_g=m_norm1_g, gmlp_ln_g=m_gmlp_ln_g, gmlp_ln_b=m_gmlp_ln_b, w_spatial=m_w_spatial,
                 b_spatial=m_b_spatial, group_norm_g=m_group_norm_g, norm2_g=m_norm2_g)
    rep_v = dict(norm1_g=v_norm1_g, gmlp_ln_g=v_gmlp_ln_g, gmlp_ln_b=v_gmlp_ln_b, w_spatial=v_w_spatial,
                 b_spatial=v_b_spatial, group_norm_g=v_group_norm_g, norm2_g=v_norm2_g)
    parts = [_rows(jnp.stack([small[l][k].reshape(rep_w[k].shape[1:]) for l in range(nl)])) for k in rep]
    parts.append(_rows(d_final_g))
    parts.append(_rows(jnp.stack([small[l]["conv_w"] for l in range(nl)])))
    parts.append(jnp.broadcast_to(loss_part, (8, 128)))
    sizes = [p.shape[0] for p in parts]
    comm, deliver = host(chip_job(0, ["w_out", "w_in"]))
    total, couts = _all_reduce_small(jnp.concatenate(parts, axis=0), comm)
    deliver(couts)
    offs = [0]
    for n in sizes:
        offs.append(offs[-1] + n)
    pieces = [total[offs[i]:offs[i + 1]] for i in range(len(parts))]
    loss = pieces[-1][0, 0]
    conv_g_full = pieces[-2].reshape(nl, 3, N_DEV, a // N_DEV)
    conv_g = lax.dynamic_index_in_dim(conv_g_full, me, axis=2, keepdims=False)
    n_rep = offs[len(rep) + 1]
    pad = jnp.zeros((2, 128), F32)

    def small_pack(named, final, conv):
        return jnp.concatenate([_rows(named[k]) for k in rep] + [_rows(final), _rows(conv), pad], axis=0)

    g_small = jnp.concatenate([total[:n_rep], _rows(conv_g), pad], axis=0)
    d_small, m_small, v_small = _adamw_small(
        g_small, small_pack(rep_w, final_norm_g, conv_w), small_pack(rep_m, m_final_norm_g, m_conv_w),
        small_pack(rep_v, v_final_norm_g, v_conv_w))

    def unpack(packed):
        out = {k: packed[offs[i]:offs[i + 1]].reshape(rep_w[k].shape) for i, k in enumerate(rep)}
        out["final_norm_g"] = packed[offs[len(rep)]:n_rep].reshape(final_norm_g.shape)
        out["conv_w"] = packed[n_rep:n_rep + 6].reshape(conv_w.shape)
        return out

    res = {"grad": unpack(g_small), "delta": unpack(d_small), "m": unpack(m_small), "v": unpack(v_small)}

    for k in BIG:
        outs, _ = _adamw_big(f"adamw_{k}", lands[k], big[k], big_m[k], big_v[k])
        if k in ("w_gate", "w_up"):
            outs = [tr(o) for o in outs]
        res["grad"][k], res["delta"][k], res["m"][k], res["v"][k] = outs

    order = ["norm1_g", "w_in", "gmlp_ln_g", "gmlp_ln_b", "w_spatial", "b_spatial", "conv_w", "group_norm_g", "w_out",
             "norm2_g", "w_gate", "w_up", "w_down", "final_norm_g"]
    return (loss, grad_x, *[res["grad"][k] for k in order], *[res["delta"][k] for k in order],
            *[res["m"][k] for k in order], *[res["v"][k] for k in order])
```

```python
import functools
import math
import operator

import jax
import jax.numpy as jnp
from jax import lax
from jax.experimental import pallas as pl
from jax.experimental.pallas import tpu as pltpu

F32 = jnp.float32
BF16 = jnp.bfloat16
MESH = pl.DeviceIdType.MESH

N_DEV = 8
N_LAYERS = 2
HEADS = 8
BLK = 128
CHUNK = 64
HALO = 16
RMS_EPS = 1e-6
LN_EPS = 1e-5
ADAM_LR, ADAM_B1, ADAM_B2, ADAM_EPS, ADAM_WD, ADAM_STEP = 0.001, 0.9, 0.999, 1e-8, 0.01, 10
GELU_C = math.sqrt(2.0 / math.pi)
GELU_A = 0.044715

VMEM_LIMIT_V7X = 56 * 1024 * 1024
_TM = 1024
_TN = 1024
_TT = 1024
_TM_MIX = 256
_TM_NORM = 512


def _cparams(n_axes):
    return pltpu.CompilerParams(dimension_semantics=("arbitrary",) * n_axes, vmem_limit_bytes=VMEM_LIMIT_V7X)


def _sds(shape, dtype):
    return jax.ShapeDtypeStruct(tuple(shape), dtype)


def _place():
    return lax.axis_index("x"), lax.axis_index("y"), lax.axis_index("c")


def _index(place):
    return 4 * place[0] + 2 * place[1] + place[2]


class _Piece:
    def __init__(self, operands, out_shapes, aliases, n_sems, start, finish, mid1=None, mid2=None):
        self.operands, self.out_shapes, self.aliases, self.n_sems = list(operands), list(out_shapes), dict(aliases), n_sems
        nothing = lambda ctx: None
        self.start, self.mid1, self.mid2, self.finish = start, mid1 or nothing, mid2 or nothing, finish


class _Ctx:
    def __init__(self, ins, outs, sems, offs):
        self.ins, self.outs, self.sems = ins, outs, sems
        self.o_in, self.o_out, self.o_send, self.o_recv, self.o_loc = offs

    def inp(self, i):
        return self.ins[self.o_in + i]

    def out(self, i):
        return self.outs[self.o_out + i]

    def send(self, k):
        return self.sems[0].at[self.o_send + k]

    def recv(self, k):
        return self.sems[1].at[self.o_recv + k]

    def local(self, k):
        return self.sems[2].at[self.o_loc + k]


class _Hosted:
    def __init__(self, pieces, n_in_before, n_out_before):
        self.pieces = [p for p in (pieces or []) if p is not None]
        self.operands, self.out_shapes, self.aliases, self.offs = [], [], {}, []
        counts = [0, 0, 0]
        for p in self.pieces:
            self.offs.append((len(self.operands), len(self.out_shapes), *counts))
            for i, j in p.aliases.items():
                self.aliases[n_in_before + len(self.operands) + i] = n_out_before + len(self.out_shapes) + j
            self.operands += p.operands
            self.out_shapes += p.out_shapes
            counts = [c + n for c, n in zip(counts, p.n_sems)]
        hbm = pl.BlockSpec(memory_space=pl.ANY)
        self.in_specs = [hbm] * len(self.operands)
        self.out_specs = [hbm] * len(self.out_shapes)
        self.scratch = [pltpu.SemaphoreType.DMA((max(c, 1),)) for c in counts] if self.pieces else []

    def run(self, stage, ins, outs, sems):
        for p, offs in zip(self.pieces, self.offs):
            getattr(p, stage)(_Ctx(ins, outs, sems, offs))

    def wrap(self, grid, compute, ins, outs, sems):
        if not self.pieces:
            compute()
            return
        n_steps = math.prod(grid)
        lin = 0
        for ax, g in enumerate(grid):
            lin = lin * g + pl.program_id(ax)
        pl.when(lin == 0)(lambda: self.run("start", ins, outs, sems))
        compute()
        pl.when(lin == n_steps // 2)(lambda: self.run("mid1", ins, outs, sems))
        pl.when(lin == max(n_steps - 3, n_steps // 2))(lambda: self.run("mid2", ins, outs, sems))
        pl.when(lin == n_steps - 1)(lambda: self.run("finish", ins, outs, sems))


def _cols_view(width):
    return lambda ref, p: ref.at[:, pl.ds(pl.multiple_of(p * width, 128), width)]


def _rows_view(height):
    return lambda ref, p: ref.at[pl.ds(pl.multiple_of(p * height, 16), height), :]


def _cols_halves(rows, width):
    hr = rows // 2
    return (lambda ref, p, h: ref.at[pl.ds(h * hr, hr), pl.ds(pl.multiple_of(p * width, 128), width)],
            lambda ref, h: ref.at[pl.ds(h * hr, hr), :], 2)


def _rows_halves(height):
    hh = height // 2
    return (lambda ref, p, h: ref.at[pl.ds(pl.multiple_of(p * height + h * hh, 16), hh), :],
            lambda ref, h: ref.at[pl.ds(h * hh, hh), :], 2)


_SLOT_WHOLE = (lambda ref, p, h: ref.at[p], lambda ref, h: ref, 1)


def _ag_piece(specs):
    units = [(a, h) for a, s in enumerate(specs) for h in s[3]]

    def plan(ctx):
        x, y, c = _place()
        me, sib, xn, yn, dg = (x, y, c), (x, y, 1 - c), (1 - x, y, c), (x, 1 - y, c), (1 - x, 1 - y, c)

        def copy(u, k, block, to, from_shard=False):
            a, h = units[u]
            dst_of, src_of, _ = specs[a][2]
            dst = dst_of(ctx.out(a), _index(block), h)
            return pltpu.make_async_remote_copy(
                src_ref=src_of(ctx.inp(a), h) if from_shard else dst, dst_ref=dst, send_sem=ctx.send(7 * u + k),
                recv_sem=ctx.recv(7 * u + k), device_id=to, device_id_type=MESH)

        def local(u):
            a, h = units[u]
            dst_of, src_of, _ = specs[a][2]
            return pltpu.make_async_copy(src_of(ctx.inp(a), h), dst_of(ctx.out(a), _index(me), h), ctx.local(u))

        def relay(u):
            return copy(u, 3, xn, yn) if units[u][1] % 2 == 0 else copy(u, 3, yn, xn)

        return me, sib, xn, yn, dg, c, copy, local, relay

    def start(ctx):
        me, sib, xn, yn, dg, c, copy, local, relay = plan(ctx)
        for u in range(len(units)):
            local(u).start()
            for k, to in enumerate((sib, xn, yn)):
                copy(u, k, me, to, from_shard=True).start()

    def mid1(ctx):
        me, sib, xn, yn, dg, c, copy, local, relay = plan(ctx)
        for u in range(len(units)):
            copy(u, 1, xn, me).wait_recv()
            copy(u, 2, yn, me).wait_recv()
            relay(u).start()
            copy(u, 4, xn, sib).start()
            copy(u, 5, yn, sib).start()

    def mid2(ctx):
        me, sib, xn, yn, dg, c, copy, local, relay = plan(ctx)
        for u in range(len(units)):
            copy(u, 3, dg, me).wait_recv()
            copy(u, 6, dg, sib).start()

    def finish(ctx):
        me, sib, xn, yn, dg, c, copy, local, relay = plan(ctx)
        other = lambda place: (place[0], place[1], 1 - c)
        for u in range(len(units)):
            for k, block in ((0, sib), (4, other(xn)), (5, other(yn)), (6, other(dg))):
                copy(u, k, block, me).wait_recv()
        for u in range(len(units)):
            for k, to in enumerate((sib, xn, yn)):
                copy(u, k, me, to, from_shard=True).wait_send()
            relay(u).wait_send()
            for k, block in ((4, xn), (5, yn), (6, dg)):
                copy(u, k, block, sib).wait_send()
            local(u).wait()

    n_u = len(units)
    operands, aliases = [s[0] for s in specs], {}
    for a, spec in enumerate(specs):
        if spec[4] is not None:
            aliases[len(operands)] = a
            operands.append(spec[4])
    return _Piece(operands, [s[1] for s in specs], aliases, (7 * n_u, 7 * n_u, n_u), start, finish, mid1, mid2)


N_CHIPS = 4


def _rs_core_piece(specs):
    n = len(specs)

    def copies(ctx):
        x, y, c = _place()
        out = []
        for a in range(n):
            for q in range(N_CHIPS):
                out.append(pltpu.make_async_remote_copy(
                    src_ref=specs[a][2](ctx.inp(a), 2 * q + (1 - c)), dst_ref=ctx.out(a).at[q],
                    send_sem=ctx.send(N_CHIPS * a + q), recv_sem=ctx.recv(N_CHIPS * a + q), device_id=(x, y, 1 - c),
                    device_id_type=MESH))
        return out

    def start(ctx):
        for cp in copies(ctx):
            cp.start()

    def finish(ctx):
        for cp in copies(ctx):
            cp.wait_recv()
            cp.wait_send()

    return _Piece([s[0] for s in specs], [s[1] for s in specs], {}, (N_CHIPS * n, N_CHIPS * n, 0), start, finish)


def _rs_chip_piece(specs, layer):
    n = len(specs)
    hops = [(1, 0), (0, 1), (1, 1)]

    def copies(ctx):
        x, y, c = _place()
        mine = 2 * x + y
        out = []
        for a in range(n):
            rows = pl.ds(*specs[a][2])
            sums, land = ctx.inp(a), ctx.out(a)
            out.append((pltpu.make_async_copy(sums.at[mine, rows], land.at[layer, mine, rows], ctx.local(a)), None))
            for j, (dx, dy) in enumerate(hops):
                px, py = x ^ dx, y ^ dy
                peer = 2 * px + py
                send = pltpu.make_async_remote_copy(
                    src_ref=sums.at[peer, rows], dst_ref=land.at[layer, mine, rows], send_sem=ctx.send(3 * a + j),
                    recv_sem=ctx.recv(3 * a + j), device_id=(px, py, c), device_id_type=MESH)
                recv = pltpu.make_async_remote_copy(
                    src_ref=sums.at[peer, rows], dst_ref=land.at[layer, peer, rows], send_sem=ctx.send(3 * a + j),
                    recv_sem=ctx.recv(3 * a + j), device_id=(px, py, c), device_id_type=MESH)
                out.append((send, recv))
        return out

    def start(ctx):
        for send, _ in copies(ctx):
            send.start()

    def finish(ctx):
        for send, recv in copies(ctx):
            if recv is None:
                send.wait()
            else:
                recv.wait_recv()
                send.wait_send()

    operands, aliases = [s[0] for s in specs], {}
    for a, spec in enumerate(specs):
        if spec[3] is not None:
            aliases[len(operands)] = a
            operands.append(spec[3])
    return _Piece(operands, [s[1] for s in specs], aliases, (3 * n, 3 * n, n), start, finish)


def _chip_sums(name, grad, stage, by_cols, core):
    _, r, c = stage.shape
    tr = r
    while tr * c > 1024 * 1024 or r % tr or tr % 16:
        tr -= 16
    n_t = r // tr

    def body(core_ref, g_ref, s_ref, o_ref):
        o_ref[...] = (g_ref[...].astype(F32) + s_ref[...].astype(F32)).astype(BF16)

    if by_cols:
        gspec = pl.BlockSpec((tr, c), lambda q, i, core_ref: (i, 2 * q + core_ref[0]))
    else:
        gspec = pl.BlockSpec((tr, c), lambda q, i, core_ref: ((2 * q + core_ref[0]) * n_t + i, 0))
    sspec = pl.BlockSpec((None, tr, c), lambda q, i, core_ref: (q, i, 0))
    return pl.pallas_call(
        body, name=name, out_shape=_sds(stage.shape, BF16),
        grid_spec=pltpu.PrefetchScalarGridSpec(num_scalar_prefetch=1, grid=(N_CHIPS, n_t), in_specs=[gspec, sspec],
                                               out_specs=sspec),
        compiler_params=_cparams(2))(core, grad, stage)


def _comm_only(name, pieces):
    hosted = _Hosted(pieces, 0, 0)
    n_in, n_out = len(hosted.operands), len(hosted.out_shapes)

    def body(*refs):
        ins, outs, sems = refs[:n_in], refs[n_in:n_in + n_out], refs[n_in + n_out:]
        for stage in ("start", "mid1", "mid2", "finish"):
            hosted.run(stage, ins, outs, sems)

    return pl.pallas_call(
        body, name=name, out_shape=tuple(hosted.out_shapes), in_specs=hosted.in_specs, out_specs=tuple(hosted.out_specs),
        input_output_aliases=hosted.aliases, scratch_shapes=hosted.scratch)(*hosted.operands)


def _matmul(name, grid, nk, kaxis, pairs, dims, extras, outs, epilogue, sum_pairs, acc_shape, comm=None, split=None):
    n_p, n_e, n_o = len(pairs), len(extras), len(outs)
    n_acc = 0 if nk == 1 else (1 if sum_pairs else n_p)
    n_in = 2 * n_p + n_e
    hosted = _Hosted(comm, n_in, n_o)
    n_ci, n_co = len(hosted.operands), len(hosted.out_shapes)

    def body(*refs):
        a_refs = refs[0:2 * n_p:2]
        b_refs = refs[1:2 * n_p:2]
        e_refs = refs[2 * n_p:n_in]
        c_ins = refs[n_in:n_in + n_ci]
        o_refs = refs[n_in + n_ci:n_in + n_ci + n_o]
        c_outs = refs[n_in + n_ci + n_o:n_in + n_ci + n_o + n_co]
        acc_refs = refs[n_in + n_ci + n_o + n_co:n_in + n_ci + n_o + n_co + n_acc]
        sems = refs[n_in + n_ci + n_o + n_co + n_acc:]

        def dots():
            prods = [lax.dot_general(a[...], b[...], (dims, ((), ())), preferred_element_type=F32)
                     for a, b in zip(a_refs, b_refs)]
            if sum_pairs and n_p > 1:
                prods = [functools.reduce(operator.add, prods)]
            return prods

        def compute():
            if nk == 1 and split is not None:
                n_split, b_axis = split
                width = b_refs[0].shape[b_axis] // n_split
                for s in range(n_split):
                    cols = pl.ds(s * width, width)
                    epilogue([lax.dot_general(a[...], b[cols, :] if b_axis == 0 else b[:, cols], (dims, ((), ())),
                                              preferred_element_type=F32) for a, b in zip(a_refs, b_refs)],
                             e_refs, o_refs, cols)
                return
            if nk == 1:
                epilogue(dots(), e_refs, o_refs)
                return
            k = pl.program_id(kaxis)

            @pl.when(k == 0)
            def _():
                for acc, p in zip(acc_refs, dots()):
                    acc[...] = p

            if nk > 2:
                @pl.when((k > 0) & (k < nk - 1))
                def _():
                    for acc, p in zip(acc_refs, dots()):
                        acc[...] += p

            @pl.when(k == nk - 1)
            def _():
                epilogue([acc[...] + p for acc, p in zip(acc_refs, dots())], e_refs, o_refs)

        hosted.wrap(grid, compute, c_ins, c_outs, sems)

    operands, in_specs = [], []
    for a, a_spec, b, b_spec in pairs:
        operands += [a, b]
        in_specs += [a_spec, b_spec]
    for e, e_spec in extras:
        operands.append(e)
        in_specs.append(e_spec)
    res = pl.pallas_call(
        body, name=name, grid=grid,
        out_shape=tuple([o for o, _ in outs] + hosted.out_shapes),
        in_specs=in_specs + hosted.in_specs, out_specs=tuple([s for _, s in outs] + hosted.out_specs),
        input_output_aliases=hosted.aliases,
        scratch_shapes=[pltpu.VMEM(acc_shape, F32) for _ in range(n_acc)] + hosted.scratch,
        compiler_params=_cparams(len(grid)),
    )(*operands, *hosted.operands)
    return list(res[:n_o]), list(res[n_o:])


NN = ((1,), (0,))
NT = ((1,), (1,))
TN = ((0,), (0,))


def _tile(n, want):
    if n <= want:
        return n
    t = want // 128 * 128
    while n % t:
        t -= 128
    return t


def _silu_parts(g):
    s = 0.5 + 0.5 * jnp.tanh(0.5 * g)
    return s, g * s


def _mm_in(h, w_in, comm=None):
    t, d = h.shape
    n = w_in.shape[1]
    tm, tn = _tile(t, _TM), _tile(n, _TN)

    def epi(accs, e, o):
        o[0][...] = accs[0].astype(BF16)

    outs, couts = _matmul(
        "mm_in", (n // tn, t // tm), 1, None,
        [(h, pl.BlockSpec((tm, d), lambda j, i: (i, 0)), w_in, pl.BlockSpec((d, tn), lambda j, i: (0, j)))],
        NN, [], [(_sds((t, n), BF16), pl.BlockSpec((tm, tn), lambda j, i: (i, j)))], epi, True, None, comm)
    return outs[0], couts


def _mm_out(y, w_out, x, comm=None):
    t, m = y.shape
    d = w_out.shape[1]
    tm, tn = _tile(t, _TM), _tile(d, _TN)

    def epi(accs, e, o):
        o[0][...] = e[0][...] + accs[0]

    outs, couts = _matmul(
        "mm_out", (t // tm, d // tn), 1, None,
        [(y, pl.BlockSpec((tm, m), lambda i, j: (i, 0)), w_out, pl.BlockSpec((m, tn), lambda i, j: (0, j)))],
        NN, [(x, pl.BlockSpec((tm, tn), lambda i, j: (i, j)))],
        [(_sds((t, d), F32), pl.BlockSpec((tm, tn), lambda i, j: (i, j)))], epi, True, None, comm)
    return outs[0], couts


def _mm_swiglu(h2, wgt, wut, comm=None):
    t, d = h2.shape
    f = wgt.shape[0]
    tm, tn = _tile(t, _TM), _tile(f, 512)

    def epi(accs, e, o, cols):
        g, u = accs
        _, sg = _silu_parts(g)
        o[0][:, cols] = g.astype(BF16)
        o[1][:, cols] = u.astype(BF16)
        o[2][:, cols] = (sg * u).astype(BF16)

    wspec = pl.BlockSpec((tn, d), lambda j, i: (j, 0))
    hspec = pl.BlockSpec((tm, d), lambda j, i: (i, 0))
    ospec = pl.BlockSpec((tm, tn), lambda j, i: (i, j))
    osh = _sds((t, f), BF16)
    outs, couts = _matmul("mm_swiglu", (f // tn, t // tm), 1, None, [(h2, hspec, wgt, wspec), (h2, hspec, wut, wspec)],
                          NT, [], [(osh, ospec)] * 3, epi, False, None, comm, split=(tn // 256, 0))
    return outs, couts


def _mm_down(act, wd, x1, comm=None):
    t, f = act.shape
    d = wd.shape[1]
    tm, tn = _tile(t, _TM), _tile(d, _TN)
    nk = 2
    tk = f // nk

    def epi(accs, e, o):
        o[0][...] = e[0][...] + accs[0]

    outs, couts = _matmul(
        "mm_down", (t // tm, d // tn, nk), nk, 2,
        [(act, pl.BlockSpec((tm, tk), lambda i, j, k: (i, k)), wd, pl.BlockSpec((tk, tn), lambda i, j, k: (k, j)))],
        NN, [(x1, pl.BlockSpec((tm, tn), lambda i, j, k: (i, j)))],
        [(_sds((t, d), F32), pl.BlockSpec((tm, tn), lambda i, j, k: (i, j)))], epi, True, (tm, tn), comm)
    return outs[0], couts


def _mm_dact(dxb, wd, gate, up, comm=None):
    t, d = dxb.shape
    f = wd.shape[0]
    tm, tn = _tile(t, _TM), _tile(f, 512)

    def epi(accs, e, o, cols):
        da = accs[0]
        g = e[0][:, cols].astype(F32)
        u = e[1][:, cols].astype(F32)
        s, sg = _silu_parts(g)
        o[0][:, cols] = (da * u * (s + sg * (1.0 - s))).astype(BF16)
        o[1][:, cols] = (da * sg).astype(BF16)

    bspec = pl.BlockSpec((tm, tn), lambda j, i: (i, j))
    osh = _sds((t, f), BF16)
    outs, couts = _matmul(
        "mm_dact", (f // tn, t // tm), 1, None,
        [(dxb, pl.BlockSpec((tm, d), lambda j, i: (i, 0)), wd, pl.BlockSpec((tn, d), lambda j, i: (j, 0)))],
        NT, [(gate, bspec), (up, bspec)], [(osh, bspec)] * 2, epi, True, None, comm, split=(tn // 256, 0))
    return outs, couts


def _mm_dh2(dgate, dup, wgt, wut, comm=None):
    t, f = dgate.shape
    d = wgt.shape[1]
    tm, tn = _tile(t, _TM), _tile(d, _TN)
    nk = 4
    tk = f // nk

    def epi(accs, e, o):
        o[0][...] = accs[0]

    aspec = pl.BlockSpec((tm, tk), lambda i, j, k: (i, k))
    wspec = pl.BlockSpec((tk, tn), lambda i, j, k: (k, j))
    outs, couts = _matmul("mm_dh2", (t // tm, d // tn, nk), nk, 2, [(dgate, aspec, wgt, wspec), (dup, aspec, wut, wspec)],
                          NN, [], [(_sds((t, d), F32), pl.BlockSpec((tm, tn), lambda i, j, k: (i, j)))], epi, True,
                          (tm, tn), comm)
    return outs[0], couts


def _mm_dw(name, a_list, b, tmo, tno, comm=None):
    t, m = a_list[0].shape
    n = b.shape[1]
    tt = _tile(t, _TT)
    nk = t // tt
    tmo, tno = _tile(m, tmo), _tile(n, tno)

    def epi(accs, e, o):
        for acc, out in zip(accs, o):
            out[...] = acc.astype(BF16)

    aspec = pl.BlockSpec((tt, tmo), lambda i, j, k: (k, i))
    bspec = pl.BlockSpec((tt, tno), lambda i, j, k: (k, j))
    ospec = pl.BlockSpec((tmo, tno), lambda i, j, k: (i, j))
    if nk == 1:
        return _matmul(name, (m // tmo, n // tno, 1), 1, None, [(a, aspec, b, bspec) for a in a_list], TN, [],
                       [(_sds((m, n), BF16), ospec)] * len(a_list), epi, False, None, comm)
    return _matmul(name, (m // tmo, n // tno, nk), nk, 2, [(a, aspec, b, bspec) for a in a_list], TN, [],
                   [(_sds((m, n), BF16), ospec)] * len(a_list), epi, False, (tmo, tno), comm)


def _mm_dy(dxb, w_out, comm=None):
    t, d = dxb.shape
    m = w_out.shape[0]
    tm, tn = _tile(t, _TM), _tile(m, _TN)

    def epi(accs, e, o):
        o[0][...] = accs[0].astype(BF16)

    outs, couts = _matmul(
        "mm_dy", (t // tm, m // tn), 1, None,
        [(dxb, pl.BlockSpec((tm, d), lambda i, j: (i, 0)), w_out, pl.BlockSpec((tn, d), lambda i, j: (j, 0)))], NT, [],
        [(_sds((t, m), BF16), pl.BlockSpec((tm, tn), lambda i, j: (i, j)))], epi, True, None, comm)
    return outs[0], couts


def _mm_dh(dz, w_in, comm=None):
    t, n = dz.shape
    d = w_in.shape[0]
    tm, tn = _tile(t, _TM), _tile(d, _TN)
    nk = 2
    tk = n // nk

    def epi(accs, e, o):
        o[0][...] = accs[0]

    outs, couts = _matmul(
        "mm_dh", (t // tm, d // tn, nk), nk, 2,
        [(dz, pl.BlockSpec((tm, tk), lambda i, j, k: (i, k)), w_in, pl.BlockSpec((tn, tk), lambda i, j, k: (j, k)))], NT,
        [], [(_sds((t, d), F32), pl.BlockSpec((tm, tn), lambda i, j, k: (i, j)))], epi, True, (tm, tn), comm)
    return outs[0], couts


def _rmsnorm_fwd(x, g):
    t, d = x.shape
    tm = min(_TM_NORM, t)

    def body(x_ref, g_ref, o_ref):
        xv = x_ref[...]
        rs = lax.rsqrt(jnp.mean(xv * xv, axis=-1, keepdims=True) + RMS_EPS)
        o_ref[...] = (xv * rs * g_ref[...]).astype(BF16)

    return pl.pallas_call(
        body, name="rmsnorm_fwd", grid=(t // tm,), out_shape=_sds((t, d), BF16),
        in_specs=[pl.BlockSpec((tm, d), lambda i: (i, 0)), pl.BlockSpec((1, d), lambda i: (0, 0))],
        out_specs=pl.BlockSpec((tm, d), lambda i: (i, 0)), compiler_params=_cparams(1))(x, g)


def _rmsnorm_bwd_math(xv, g, dh):
    rs = lax.rsqrt(jnp.mean(xv * xv, axis=-1, keepdims=True) + RMS_EPS)
    xh = xv * rs
    gd = dh * g
    dx = rs * (gd - xh * jnp.mean(gd * xh, axis=-1, keepdims=True))
    return dx, jnp.sum(dh * xh, axis=0, keepdims=True)


def _rmsnorm_bwd(x, g, dh, dres):
    t, d = x.shape
    tm = min(_TM_NORM, t)

    def body(x_ref, g_ref, dh_ref, dres_ref, dx_ref, dxb_ref, dg_ref):
        dx, dg = _rmsnorm_bwd_math(x_ref[...], g_ref[...], dh_ref[...])
        dx = dx + dres_ref[...]
        dx_ref[...] = dx
        dxb_ref[...] = dx.astype(BF16)

        @pl.when(pl.program_id(0) == 0)
        def _():
            dg_ref[...] = dg

        @pl.when(pl.program_id(0) > 0)
        def _():
            dg_ref[...] += dg

    row = pl.BlockSpec((tm, d), lambda i: (i, 0))
    vec = pl.BlockSpec((1, d), lambda i: (0, 0))
    return pl.pallas_call(
        body, name="rmsnorm_bwd", grid=(t // tm,),
        out_shape=(_sds((t, d), F32), _sds((t, d), BF16), _sds((1, d), F32)),
        in_specs=[row, vec, row, row], out_specs=(row, row, vec), compiler_params=_cparams(1))(x, g, dh, dres)


def _loss_head(x, g, target):
    t, d = x.shape
    tm = min(_TM_NORM, t)

    def body(x_ref, g_ref, t_ref, dx_ref, dxb_ref, dg_ref, loss_ref):
        xv, gv = x_ref[...], g_ref[...]
        rs = lax.rsqrt(jnp.mean(xv * xv, axis=-1, keepdims=True) + RMS_EPS)
        diff = xv * rs * gv - t_ref[...]
        part = 0.5 * jnp.sum(jnp.mean(diff * diff, axis=-1, keepdims=True), axis=0, keepdims=True)
        part = jnp.broadcast_to(part, (1, 128))
        dx, dg = _rmsnorm_bwd_math(xv, gv, diff * (1.0 / d))
        dx_ref[...] = dx
        dxb_ref[...] = dx.astype(BF16)

        @pl.when(pl.program_id(0) == 0)
        def _():
            dg_ref[...] = dg
            loss_ref[...] = part

        @pl.when(pl.program_id(0) > 0)
        def _():
            dg_ref[...] += dg
            loss_ref[...] += part

    row = pl.BlockSpec((tm, d), lambda i: (i, 0))
    vec = pl.BlockSpec((1, d), lambda i: (0, 0))
    return pl.pallas_call(
        body, name="loss_head", grid=(t // tm,),
        out_shape=(_sds((t, d), F32), _sds((t, d), BF16), _sds((1, d), F32), _sds((1, 128), F32)),
        in_specs=[row, vec, row], out_specs=(row, row, vec, pl.BlockSpec((1, 128), lambda i: (0, 0))),
        compiler_params=_cparams(1))(x, g, target)


def _gelu(x):
    th = jnp.tanh(GELU_C * (x + GELU_A * x * x * x))
    return 0.5 * x * (1.0 + th), th


def _gelu_grad(x, th):
    return 0.5 * (1.0 + th) + 0.5 * x * (1.0 - th * th) * GELU_C * (1.0 + 3.0 * GELU_A * x * x)


def _masked_ws(ws_ref, h):
    i = lax.broadcasted_iota(jnp.int32, (BLK, BLK), 0) // CHUNK
    j = lax.broadcasted_iota(jnp.int32, (BLK, BLK), 1) // CHUNK
    return jnp.where(j <= i, ws_ref[h], 0.0)


def _shift_down(q, n, first_rows):
    rolled = pltpu.roll(q, n, 0)
    row = lax.broadcasted_iota(jnp.int32, q.shape, 0)
    for r, val in enumerate(first_rows):
        rolled = jnp.where(row == r, val, rolled)
    return rolled


def _shift_up(q, n, last_rows):
    tm = q.shape[0]
    rolled = pltpu.roll(q, tm - n, 0)
    row = lax.broadcasted_iota(jnp.int32, q.shape, 0)
    for r, val in enumerate(last_rows):
        rolled = jnp.where(row == tm - n + r, val, rolled)
    return rolled


def _mixer_specs(t, a, tm):
    hb = tm // HALO
    last = t // HALO - 1
    tile = pl.BlockSpec((tm, 5 * a), lambda i: (i, 0))
    prev = [pl.BlockSpec((HALO, a), functools.partial(lambda i, col: (jnp.maximum(i * hb - 1, 0), col), col=col))
            for col in (3, 4)]
    nxt = [pl.BlockSpec((HALO, a), functools.partial(lambda i, col: (jnp.minimum((i + 1) * hb, last), col), col=col))
           for col in (2, 3, 4)]
    return tile, prev, nxt


def _group_a_fwd(zu, zv, lng, lnb, ws_ref, bb_ref, mixed_ref, vln_ref):
    u, thu = _gelu(zu)
    v, thv = _gelu(zv)
    mu = jnp.mean(v, axis=-1, keepdims=True)
    vc = v - mu
    rs = lax.rsqrt(jnp.mean(vc * vc, axis=-1, keepdims=True) + LN_EPS)
    vhat = vc * rs
    vln_ref[...] = vhat * lng + lnb
    tm, a = zu.shape
    hd = a // HEADS
    for h in range(HEADS):
        w = _masked_ws(ws_ref, h).astype(BF16)
        for b in range(tm // BLK):
            rows, cols = pl.ds(b * BLK, BLK), pl.ds(h * hd, hd)
            mixed_ref[rows, cols] = jnp.dot(w, vln_ref[rows, cols].astype(BF16), preferred_element_type=F32) + bb_ref[h]
    return u, thu, thv, rs, vhat


def _mixer_fwd(z, ln_g, ln_b, w_spatial, bb, conv_w, gg):
    t = z.shape[0]
    a = z.shape[1] // 5
    tm = min(_TM_MIX, t)
    tile, prev, _ = _mixer_specs(t, a, tm)

    def body(z_ref, pc_ref, ph_ref, lng_ref, lnb_ref, ws_ref, bb_ref, cw_ref, gg_ref, y_ref, mixed_ref, vln_ref):
        i = pl.program_id(0)
        zu = z_ref[:, 0:a].astype(F32)
        zv = z_ref[:, a:2 * a].astype(F32)
        u, _, _, _, _ = _group_a_fwd(zu, zv, lng_ref[...], lnb_ref[...], ws_ref, bb_ref, mixed_ref, vln_ref)
        ya = u * mixed_ref[...]
        ra = lax.rsqrt(jnp.mean(ya * ya, axis=-1, keepdims=True) + RMS_EPS)
        y_ref[:, 0:a] = (ya * ra * gg_ref[:, 0:a]).astype(BF16)

        zb = z_ref[:, 2 * a:3 * a].astype(F32)
        q = z_ref[:, 3 * a:4 * a].astype(F32) * z_ref[:, 4 * a:5 * a].astype(F32)
        qp = jnp.where(i > 0, pc_ref[...].astype(F32) * ph_ref[...].astype(F32), 0.0)
        qm1 = _shift_down(q, 1, [qp[HALO - 1:HALO]])
        qm2 = _shift_down(q, 2, [qp[HALO - 2:HALO - 1], qp[HALO - 1:HALO]])
        cv = cw_ref[0:1, :] * qm2 + cw_ref[1:2, :] * qm1 + cw_ref[2:3, :] * q
        yb = zb * cv
        rb = lax.rsqrt(jnp.mean(yb * yb, axis=-1, keepdims=True) + RMS_EPS)
        y_ref[:, a:2 * a] = (yb * rb * gg_ref[:, a:2 * a]).astype(BF16)

    full = lambda shape: pl.BlockSpec(shape, lambda i: (0,) * len(shape))
    return pl.pallas_call(
        body, name="mixer_fwd", grid=(t // tm,), out_shape=_sds((t, 2 * a), BF16),
        in_specs=[tile, *prev, full((1, a)), full((1, a)), full(w_spatial.shape), full(bb.shape), full(conv_w.shape),
                  full((1, 2 * a))],
        out_specs=pl.BlockSpec((tm, 2 * a), lambda i: (i, 0)),
        scratch_shapes=[pltpu.VMEM((tm, a), F32), pltpu.VMEM((tm, a), F32)],
        compiler_params=_cparams(1))(z, z, z, ln_g, ln_b, w_spatial, bb, conv_w, gg)


def _mixer_bwd(z, dy, ln_g, ln_b, w_spatial, bb, conv_w, gg, comm=None):
    t = z.shape[0]
    a = z.shape[1] // 5
    hd = a // HEADS
    tm = min(_TM_MIX, t)
    n_tiles = t // tm
    tile, prev, nxt = _mixer_specs(t, a, tm)
    hb = tm // HALO
    dy_tile = pl.BlockSpec((tm, 2 * a), lambda i: (i, 0))
    dy_next = pl.BlockSpec((HALO, a), lambda i: (jnp.minimum((i + 1) * hb, t // HALO - 1), 1))

    def body(z_ref, pc_ref, ph_ref, nb_ref, nc_ref, nh_ref, dy_ref, ndy_ref, lng_ref, lnb_ref, ws_ref, bb_ref, cw_ref,
             gg_ref, dz_ref, dlng_ref, dlnb_ref, dws_ref, dbb_ref, dcw_ref, dgg_ref, mixed_ref, vln_ref, dmix_ref,
             dvln_ref):
        i = pl.program_id(0)

        @pl.when(i == 0)
        def _():
            for ref in (dlng_ref, dlnb_ref, dws_ref, dbb_ref, dcw_ref, dgg_ref):
                ref[...] = jnp.zeros(ref.shape, F32)

        lng = lng_ref[...]
        zu = z_ref[:, 0:a].astype(F32)
        zv = z_ref[:, a:2 * a].astype(F32)
        u, thu, thv, rs, vhat = _group_a_fwd(zu, zv, lng, lnb_ref[...], ws_ref, bb_ref, mixed_ref, vln_ref)
        mixed = mixed_ref[...]
        ya = u * mixed
        ra = lax.rsqrt(jnp.mean(ya * ya, axis=-1, keepdims=True) + RMS_EPS)
        da = dy_ref[:, 0:a].astype(F32)
        yah = ya * ra
        dgg_ref[:, 0:a] += jnp.sum(da * yah, axis=0, keepdims=True)
        ga = da * gg_ref[:, 0:a]
        dya = ra * (ga - yah * jnp.mean(ga * yah, axis=-1, keepdims=True))
        dz_ref[:, 0:a] = (dya * mixed * _gelu_grad(zu, thu)).astype(BF16)
        dmix_ref[...] = dya * u
        for h in range(HEADS):
            w = _masked_ws(ws_ref, h).astype(BF16)
            dw = jnp.zeros((BLK, BLK), F32)
            db = jnp.zeros((BLK, hd), F32)
            for b in range(tm // BLK):
                rows, cols = pl.ds(b * BLK, BLK), pl.ds(h * hd, hd)
                dm = dmix_ref[rows, cols]
                dmb = dm.astype(BF16)
                db = db + dm
                dw = dw + lax.dot_general(dmb, vln_ref[rows, cols].astype(BF16), (NT, ((), ())),
                                          preferred_element_type=F32)
                dvln_ref[rows, cols] = lax.dot_general(w, dmb, (TN, ((), ())), preferred_element_type=F32)
            dws_ref[h] += dw
            dbb_ref[h] += db
        dvln = dvln_ref[...]
        dlng_ref[...] += jnp.sum(dvln * vhat, axis=0, keepdims=True)
        dlnb_ref[...] += jnp.sum(dvln, axis=0, keepdims=True)
        dvh = dvln * lng
        dv = rs * (dvh - jnp.mean(dvh, axis=-1, keepdims=True) - vhat * jnp.mean(dvh * vhat, axis=-1, keepdims=True))
        dz_ref[:, a:2 * a] = (dv * _gelu_grad(zv, thv)).astype(BF16)

        w0, w1, w2 = cw_ref[0:1, :], cw_ref[1:2, :], cw_ref[2:3, :]
        ggb = gg_ref[:, a:2 * a]
        zb = z_ref[:, 2 * a:3 * a].astype(F32)
        zc = z_ref[:, 3 * a:4 * a].astype(F32)
        zh = z_ref[:, 4 * a:5 * a].astype(F32)
        q = zc * zh
        qp = jnp.where(i > 0, pc_ref[...].astype(F32) * ph_ref[...].astype(F32), 0.0)
        qm1 = _shift_down(q, 1, [qp[HALO - 1:HALO]])
        qm2 = _shift_down(q, 2, [qp[HALO - 2:HALO - 1], qp[HALO - 1:HALO]])
        cv = w0 * qm2 + w1 * qm1 + w2 * q

        def conv_out_grad(zb_, cv_, dout_):
            yb = zb_ * cv_
            rb = lax.rsqrt(jnp.mean(yb * yb, axis=-1, keepdims=True) + RMS_EPS)
            ybh = yb * rb
            gb = dout_ * ggb
            dyb = rb * (gb - ybh * jnp.mean(gb * ybh, axis=-1, keepdims=True))
            return dyb * zb_, dyb * cv_, ybh

        db_out = dy_ref[:, a:2 * a].astype(F32)
        g, dzb, ybh = conv_out_grad(zb, cv, db_out)
        dgg_ref[:, a:2 * a] += jnp.sum(db_out * ybh, axis=0, keepdims=True)
        dz_ref[:, 2 * a:3 * a] = dzb.astype(BF16)
        qn = nc_ref[...].astype(F32) * nh_ref[...].astype(F32)
        zbn = nb_ref[...].astype(F32)
        cvn = w0 * _shift_down(qn, 2, [q[tm - 2:tm - 1], q[tm - 1:tm]]) + w1 * _shift_down(qn, 1, [q[tm - 1:tm]]) + w2 * qn
        gn, _, _ = conv_out_grad(zbn, cvn, ndy_ref[...].astype(F32))
        gn = jnp.where(i < n_tiles - 1, gn, 0.0)
        dq = w2 * g + w1 * _shift_up(g, 1, [gn[0:1]]) + w0 * _shift_up(g, 2, [gn[0:1], gn[1:2]])
        dz_ref[:, 3 * a:4 * a] = (dq * zh).astype(BF16)
        dz_ref[:, 4 * a:5 * a] = (dq * zc).astype(BF16)
        dcw_ref[0:1, :] += jnp.sum(g * qm2, axis=0, keepdims=True)
        dcw_ref[1:2, :] += jnp.sum(g * qm1, axis=0, keepdims=True)
        dcw_ref[2:3, :] += jnp.sum(g * q, axis=0, keepdims=True)

        @pl.when(i == n_tiles - 1)
        def _():
            for h in range(HEADS):
                dbb_ref[h] = jnp.broadcast_to(jnp.sum(dbb_ref[h], axis=1, keepdims=True), (BLK, hd))
                dws_ref[h] = _masked_ws(dws_ref, h)

    full = lambda shape: pl.BlockSpec(tuple(shape), lambda i: (0,) * len(shape))
    out_shapes = (_sds((t, 5 * a), BF16), _sds((1, a), F32), _sds((1, a), F32), _sds(w_spatial.shape, F32),
                  _sds(bb.shape, F32), _sds((8, a), F32), _sds((1, 2 * a), F32))
    n_in, n_out, n_scr = 14, len(out_shapes), 4
    hosted = _Hosted(comm, n_in, n_out)
    n_ci, n_co = len(hosted.operands), len(hosted.out_shapes)

    def hosting_body(*refs):
        ins, rest = refs[:n_in], refs[n_in:]
        c_ins, rest = rest[:n_ci], rest[n_ci:]
        outs, rest = rest[:n_out], rest[n_out:]
        c_outs, rest = rest[:n_co], rest[n_co:]
        hosted.wrap((n_tiles,), lambda: body(*ins, *outs, *rest[:n_scr]), c_ins, c_outs, rest[n_scr:])

    res = pl.pallas_call(
        hosting_body, name="mixer_bwd", grid=(n_tiles,), out_shape=tuple(list(out_shapes) + hosted.out_shapes),
        in_specs=[tile, *prev, *nxt, dy_tile, dy_next, full((1, a)), full((1, a)), full(w_spatial.shape), full(bb.shape),
                  full(conv_w.shape), full((1, 2 * a))] + hosted.in_specs,
        out_specs=(tile, *[full(s.shape) for s in out_shapes[1:]], *hosted.out_specs),
        input_output_aliases=hosted.aliases,
        scratch_shapes=[pltpu.VMEM((tm, a), F32)] * n_scr + hosted.scratch,
        compiler_params=_cparams(1))(z, z, z, z, z, z, dy, dy, ln_g, ln_b, w_spatial, bb, conv_w, gg, *hosted.operands)
    return list(res[:n_out]), list(res[n_out:])


def _all_reduce_small(pack, comm=None):
    r = pack.shape[0]
    hosted = _Hosted(comm, 1, 1)
    n_ci, n_co = len(hosted.operands), len(hosted.out_shapes)

    def body(*refs):
        in_ref, c_ins, out_ref, c_outs = refs[0], refs[1:1 + n_ci], refs[1 + n_ci], refs[2 + n_ci:2 + n_ci + n_co]
        acc_ref, recv_ref, send_sems, recv_sems = refs[2 + n_ci + n_co:6 + n_ci + n_co]
        sems = refs[6 + n_ci + n_co:]
        hosted.run("start", c_ins, c_outs, sems)
        x, y, c = _place()
        partners = [(x, y, 1 - c), (1 - x, y, c), (x, 1 - y, c)]
        acc_ref[0] = in_ref[...]
        for s, partner in enumerate(partners):
            cp = pltpu.make_async_remote_copy(
                src_ref=acc_ref.at[s], dst_ref=recv_ref.at[s], send_sem=send_sems.at[s], recv_sem=recv_sems.at[s],
                device_id=partner, device_id_type=MESH)
            cp.start()
            cp.wait()
            if s < 2:
                acc_ref[s + 1] = acc_ref[s] + recv_ref[s]
            else:
                out_ref[...] = acc_ref[s] + recv_ref[s]
        for stage in ("mid1", "mid2", "finish"):
            hosted.run(stage, c_ins, c_outs, sems)

    vmem = pl.BlockSpec(memory_space=pltpu.VMEM)
    res = pl.pallas_call(
        body, name="all_reduce_small", out_shape=tuple([_sds(pack.shape, F32)] + hosted.out_shapes),
        in_specs=[vmem] + hosted.in_specs, out_specs=tuple([vmem] + hosted.out_specs),
        input_output_aliases=hosted.aliases,
        scratch_shapes=[pltpu.VMEM((3, r, 128), F32), pltpu.VMEM((3, r, 128), F32), pltpu.SemaphoreType.DMA((3,)),
                        pltpu.SemaphoreType.DMA((3,))] + hosted.scratch,
        compiler_params=pltpu.CompilerParams(vmem_limit_bytes=VMEM_LIMIT_V7X),
    )(pack, *hosted.operands)
    return res[0], list(res[1:])


def _adamw_math(w, g, m, v):
    m = ADAM_B1 * m + (1.0 - ADAM_B1) * g
    v = ADAM_B2 * v + (1.0 - ADAM_B2) * (g * g)
    m_hat = m / (1.0 - ADAM_B1 ** ADAM_STEP)
    v_hat = v / (1.0 - ADAM_B2 ** ADAM_STEP)
    delta = -ADAM_LR * (m_hat / (jnp.sqrt(v_hat) + ADAM_EPS) + ADAM_WD * w)
    return delta, m, v


def _adamw_big(name, land, w, m, v, comm=None):
    nl, n_slots, r, c = land.shape
    tr = max(8, min(r, (256 * 640) // c // 8 * 8))
    while r % tr:
        tr -= 8
    grid = (nl, r // tr)
    hosted = _Hosted(comm, 4, 4)
    n_ci, n_co = len(hosted.operands), len(hosted.out_shapes)

    def body(*refs):
        land_ref, w_ref, m_ref, v_ref = refs[:4]
        c_ins = refs[4:4 + n_ci]
        g_out, d_out, m_out, v_out = refs[4 + n_ci:8 + n_ci]
        c_outs = refs[8 + n_ci:8 + n_ci + n_co]
        sems = refs[8 + n_ci + n_co:]

        def compute():
            g = land_ref[0].astype(F32)
            for s in range(1, n_slots):
                g = g + land_ref[s].astype(F32)
            delta, mn, vn = _adamw_math(w_ref[...], g, m_ref[...], v_ref[...])
            g_out[...] = g
            d_out[...] = delta
            m_out[...] = mn
            v_out[...] = vn

        hosted.wrap(grid, compute, c_ins, c_outs, sems)

    blk = pl.BlockSpec((None, tr, c), lambda l, i: (l, i, 0))
    res = pl.pallas_call(
        body, name=name, grid=grid, out_shape=tuple([_sds((nl, r, c), F32)] * 4 + hosted.out_shapes),
        in_specs=[pl.BlockSpec((None, n_slots, tr, c), lambda l, i: (l, 0, i, 0)), blk, blk, blk] + hosted.in_specs,
        out_specs=tuple([blk] * 4 + hosted.out_specs), input_output_aliases=hosted.aliases,
        scratch_shapes=hosted.scratch, compiler_params=_cparams(2))(land, w, m, v, *hosted.operands)
    return list(res[:4]), list(res[4:])


def _adamw_small(g, w, m, v):
    def body(g_ref, w_ref, m_ref, v_ref, d_out, m_out, v_out):
        delta, mn, vn = _adamw_math(w_ref[...], g_ref[...], m_ref[...], v_ref[...])
        d_out[...] = delta
        m_out[...] = mn
        v_out[...] = vn

    return pl.pallas_call(body, name="adamw_small", out_shape=tuple([_sds(g.shape, F32)] * 3),
                          compiler_params=pltpu.CompilerParams(vmem_limit_bytes=VMEM_LIMIT_V7X))(g, w, m, v)


def _rows(a):
    return a.reshape(-1, 128)


BIG = ["w_in", "w_out", "w_gate", "w_up", "w_down"]
BOTH = (0, 1)
AG_HOSTS = {
    ("mm_in", 0): [("w_out", 0, BOTH), ("w_gate", 0, (0,))], ("mm_out", 0): [("w_gate", 0, (1,)), ("w_up", 0, BOTH)],
    ("mm_swiglu", 0): [("w_down", 0, BOTH), ("w_in", 1, BOTH), ("w_out", 1, BOTH)], ("mm_down", 0): [("w_gate", 1, BOTH)],
    ("mm_in", 1): [("w_up", 1, BOTH)], ("mm_swiglu", 1): [("w_down", 1, BOTH)],
}


def kernel(x, norm1_g, w_in, gmlp_ln_g, gmlp_ln_b, w_spatial, b_spatial, conv_w, group_norm_g, w_out, norm2_g, w_gate, w_up, w_down, final_norm_g, loss_target, m_norm1_g, m_w_in, m_gmlp_ln_g, m_gmlp_ln_b, m_w_spatial, m_b_spatial, m_conv_w, m_group_norm_g, m_w_out, m_norm2_g, m_w_gate, m_w_up, m_w_down, m_final_norm_g, v_norm1_g, v_w_in, v_gmlp_ln_g, v_gmlp_ln_b, v_w_spatial, v_b_spatial, v_conv_w, v_group_norm_g, v_w_out, v_norm2_g, v_w_gate, v_w_up, v_w_down, v_final_norm_g):
    nl = N_LAYERS
    t, d = x.shape[1], x.shape[2]
    a = d // 2
    hd = a // HEADS
    xin = x.reshape(t, d)
    target = loss_target.reshape(t, d)
    me = _index(_place())

    tr = lambda w: jnp.transpose(w, (0, 2, 1))
    big = {"w_in": w_in, "w_out": w_out, "w_gate": tr(w_gate), "w_up": tr(w_up), "w_down": w_down}
    big_m = {"w_in": m_w_in, "w_out": m_w_out, "w_gate": tr(m_w_gate), "w_up": tr(m_w_up), "w_down": m_w_down}
    big_v = {"w_in": v_w_in, "w_out": v_w_out, "w_gate": tr(v_w_gate), "w_up": tr(v_w_up), "w_down": v_w_down}
    block = {k: big[k].shape[1:] for k in BIG}
    view = {k: _cols_view(block[k][1]) if k == "w_in" else _rows_view(block[k][0]) for k in BIG}
    full_shape = {k: (block[k][0], N_DEV * block[k][1]) if k == "w_in" else (N_DEV * block[k][0], block[k][1])
                  for k in BIG}

    halves = {k: _cols_halves(*block[k]) if k == "w_in" else _rows_halves(block[k][0]) for k in BIG}

    weights = {}
    shards = {(k, l): big[k][l].astype(BF16) for k in BIG for l in range(nl)}

    def ag_spec(k, l, which=BOTH):
        return (shards[(k, l)], _sds(full_shape[k], BF16), halves[k], which, weights.get((k, l)))

    first = _comm_only("all_gather_first", [_ag_piece([ag_spec("w_in", 0), (conv_w, _sds((N_DEV, *conv_w.shape), F32),
                                                                             _SLOT_WHOLE, (0,), None)])])
    weights[("w_in", 0)] = first[0]
    conv_full = jnp.transpose(first[1], (1, 2, 0, 3)).reshape(nl, 3, a)
    bb = jnp.broadcast_to(b_spatial[..., None], (nl, HEADS, BLK, hd))

    def hosted(name, l):
        keys = AG_HOSTS.get((name, l), [])
        return keys, ([_ag_piece([ag_spec(*key) for key in keys])] if keys else None)

    def landed(keys, couts):
        for (k, kl, _), arr in zip(keys, couts):
            weights[(k, kl)] = arr

    saved = []
    xl = xin
    for l in range(nl):
        h = _rmsnorm_fwd(xl, norm1_g[l:l + 1])
        keys, comm = hosted("mm_in", l)
        z, couts = _mm_in(h, weights[("w_in", l)], comm)
        landed(keys, couts)
        y = _mixer_fwd(z, gmlp_ln_g[l:l + 1], gmlp_ln_b[l:l + 1], w_spatial[l], bb[l], conv_full[l],
                       group_norm_g[l:l + 1])
        keys, comm = hosted("mm_out", l)
        x1, couts = _mm_out(y, weights[("w_out", l)], xl, comm)
        landed(keys, couts)
        h2 = _rmsnorm_fwd(x1, norm2_g[l:l + 1])
        keys, comm = hosted("mm_swiglu", l)
        (gate, up, act), couts = _mm_swiglu(h2, weights[("w_gate", l)], weights[("w_up", l)], comm)
        landed(keys, couts)
        keys, comm = hosted("mm_down", l)
        x2, couts = _mm_down(act, weights[("w_down", l)], x1, comm)
        landed(keys, couts)
        saved.append(dict(x=xl, h=h, z=z, y=y, x1=x1, h2=h2, gate=gate, up=up, act=act))
        xl = x2

    dx, dxb, d_final_g, loss_part = _loss_head(xl, final_norm_g.reshape(1, d), target)
    small = [None] * nl
    core = lax.axis_index("c").astype(jnp.int32).reshape(1)
    stage_shape = {k: _sds((N_CHIPS, *block[k]), BF16) for k in BIG}
    land_shape = {k: _sds((nl, N_CHIPS, *block[k]), BF16) for k in BIG}
    grads = [dict() for _ in range(nl)]
    stages = [dict() for _ in range(nl)]
    sums = [dict() for _ in range(nl)]
    lands = {k: None for k in BIG}

    def core_job(l, keys):
        def sink(outs):
            stages[l].update(zip(keys, outs))
        return _rs_core_piece([(grads[l][k], stage_shape[k], view[k]) for k in keys]), sink

    def chip_job(l, items):
        keys = [item[0] for item in items]

        def rows(k, p0, p1, n_parts):
            per = block[k][0] // n_parts
            return (p0 * per, (p1 - p0) * per)

        def sink(outs):
            lands.update(zip(keys, outs))
        return _rs_chip_piece([(sums[l][k], land_shape[k], rows(k, p0, p1, n_parts), lands[k])
                               for k, p0, p1, n_parts in items], l), sink

    def add_up(l, keys):
        for k in keys:
            sums[l][k] = _chip_sums(f"chip_sums_{k}", grads[l][k], stages[l][k], k == "w_in", core)

    def host(*jobs):
        def deliver(couts):
            i = 0
            for piece, sink in jobs:
                n_out = len(piece.out_shapes)
                sink(couts[i:i + n_out])
                i += n_out
        return [piece for piece, _ in jobs], deliver

    whole = lambda k: (k, 0, 1, 1)
    for l in reversed(range(nl)):
        s = saved[l]
        wi, wo, wgt, wut, wd = [weights[(k, l)] for k in BIG]
        later = l + 1 < nl
        comm, deliver = host(chip_job(l + 1, [("w_in", 0, 1, 2)])) if later else host()
        (grads[l]["w_down"],), couts = _mm_dw("mm_dw_down", [s["act"]], dxb, 2816, 1024, comm)
        deliver(couts)
        comm, deliver = (host(core_job(l, ["w_down"]), chip_job(l + 1, [("w_in", 1, 2, 2)])) if later
                         else host(core_job(l, ["w_down"])))
        (dgate, dup), couts = _mm_dact(dxb, wd, s["gate"], s["up"], comm)
        deliver(couts)
        add_up(l, ["w_down"])
        comm, deliver = host(chip_job(l, [whole("w_down")]))
        (grads[l]["w_gate"], grads[l]["w_up"]), couts = _mm_dw("mm_dw_gate_up", [dgate, dup], s["h2"], 1408, 1024, comm)
        deliver(couts)
        comm, deliver = host(core_job(l, ["w_gate", "w_up"]))
        dh2, couts = _mm_dh2(dgate, dup, wgt, wut, comm)
        deliver(couts)
        add_up(l, ["w_gate", "w_up"])
        dx1, dx1b, d_n2 = _rmsnorm_bwd(s["x1"], norm2_g[l:l + 1], dh2, dx)
        comm, deliver = host(chip_job(l, [("w_gate", 0, 1, 4)]))
        dy, couts = _mm_dy(dx1b, wo, comm)
        deliver(couts)
        comm, deliver = host(chip_job(l, [("w_gate", 1, 2, 4)]))
        (grads[l]["w_out"],), couts = _mm_dw("mm_dw_out", [s["y"]], dx1b, 1024, 1024, comm)
        deliver(couts)
        comm, deliver = host(chip_job(l, [("w_gate", 2, 4, 4)]))
        (dz, d_lng, d_lnb, d_ws, d_bb, d_cw, d_gg), couts = _mixer_bwd(
            s["z"], dy, gmlp_ln_g[l:l + 1], gmlp_ln_b[l:l + 1], w_spatial[l], bb[l], conv_full[l], group_norm_g[l:l + 1],
            comm)
        deliver(couts)
        comm, deliver = host(chip_job(l, [("w_up", 0, 3, 4)]), core_job(l, ["w_out"]))
        (grads[l]["w_in"],), couts = _mm_dw("mm_dw_in", [s["h"]], dz, 2048, 1024, comm)
        deliver(couts)
        add_up(l, ["w_out"])
        comm, deliver = host(chip_job(l, [("w_up", 3, 4, 4), whole("w_out")]), core_job(l, ["w_in"]))
        dh, couts = _mm_dh(dz, wi, comm)
        deliver(couts)
        add_up(l, ["w_in"])
        dx, dxb, d_n1 = _rmsnorm_bwd(s["x"], norm1_g[l:l + 1], dh, dx1)
        small[l] = dict(norm1_g=d_n1, gmlp_ln_g=d_lng, gmlp_ln_b=d_lnb, w_spatial=d_ws, b_spatial=d_bb[:, :, 0],
                        group_norm_g=d_gg, norm2_g=d_n2, conv_w=d_cw[0:3])
    grad_x = dx.reshape(x.shape)

    rep = ["norm1_g", "gmlp_ln_g", "gmlp_ln_b", "w_spatial", "b_spatial", "group_norm_g", "norm2_g"]
    rep_w = dict(norm1_g=norm1_g, gmlp_ln_g=gmlp_ln_g, gmlp_ln_b=gmlp_ln_b, w_spatial=w_spatial, b_spatial=b_spatial,
                 group_norm_g=group_norm_g, norm2_g=norm2_g)
    rep_m = dict(norm1_g=m_norm1_g, gmlp_ln_g=m_gmlp_ln_g, gmlp_ln_b=m_gmlp_ln_b, w_spatial=m_w_spatial,
                 b_spatial=m_b_spatial, group_norm_g=m_group_norm_g, norm2_g=m_norm2_g)
    rep_v = dict(norm1_g=v_norm1_g, gmlp_ln_g=v_gmlp_ln_g, gmlp_ln_b=v_gmlp_ln_b, w_spatial=v_w_spatial,
                 b_spatial=v_b_spatial, group_norm_g=v_group_norm_g, norm2_g=v_norm2_g)
    parts = [_rows(jnp.stack([small[l][k].reshape(rep_w[k].shape[1:]) for l in range(nl)])) for k in rep]
    parts.append(_rows(d_final_g))
    parts.append(_rows(jnp.stack([small[l]["conv_w"] for l in range(nl)])))
    parts.append(jnp.broadcast_to(loss_part, (8, 128)))
    sizes = [p.shape[0] for p in parts]
    comm, deliver = host(chip_job(0, [whole("w_in")]))
    total, couts = _all_reduce_small(jnp.concatenate(parts, axis=0), comm)
    deliver(couts)
    offs = [0]
    for n in sizes:
        offs.append(offs[-1] + n)
    pieces = [total[offs[i]:offs[i + 1]] for i in range(len(parts))]
    loss = pieces[-1][0, 0]
    conv_g_full = pieces[-2].reshape(nl, 3, N_DEV, a // N_DEV)
    conv_g = lax.dynamic_index_in_dim(conv_g_full, me, axis=2, keepdims=False)
    n_rep = offs[len(rep) + 1]
    pad = jnp.zeros((2, 128), F32)

    def small_pack(named, final, conv):
        return jnp.concatenate([_rows(named[k]) for k in rep] + [_rows(final), _rows(conv), pad], axis=0)

    g_small = jnp.concatenate([total[:n_rep], _rows(conv_g), pad], axis=0)
    d_small, m_small, v_small = _adamw_small(
        g_small, small_pack(rep_w, final_norm_g, conv_w), small_pack(rep_m, m_final_norm_g, m_conv_w),
        small_pack(rep_v, v_final_norm_g, v_conv_w))

    def unpack(packed):
        out = {k: packed[offs[i]:offs[i + 1]].reshape(rep_w[k].shape) for i, k in enumerate(rep)}
        out["final_norm_g"] = packed[offs[len(rep)]:n_rep].reshape(final_norm_g.shape)
        out["conv_w"] = packed[n_rep:n_rep + 6].reshape(conv_w.shape)
        return out

    res = {"grad": unpack(g_small), "delta": unpack(d_small), "m": unpack(m_small), "v": unpack(v_small)}

    for k in BIG:
        outs, _ = _adamw_big(f"adamw_{k}", lands[k], big[k], big_m[k], big_v[k])
        if k in ("w_gate", "w_up"):
            outs = [tr(o) for o in outs]
        res["grad"][k], res["delta"][k], res["m"][k], res["v"][k] = outs

    order = ["norm1_g", "w_in", "gmlp_ln_g", "gmlp_ln_b", "w_spatial", "b_spatial", "conv_w", "group_norm_g", "w_out",
             "norm2_g", "w_gate", "w_up", "w_down", "final_norm_g"]
    return (loss, grad_x, *[res["grad"][k] for k in order], *[res["delta"][k] for k in order],
            *[res["m"][k] for k in order], *[res["v"][k] for k in order])
```

```python
import functools
import math
import operator

import jax
import jax.numpy as jnp
from jax import lax
from jax.experimental import pallas as pl
from jax.experimental.pallas import tpu as pltpu

F32 = jnp.float32
BF16 = jnp.bfloat16
MESH = pl.DeviceIdType.MESH

N_DEV = 8
N_LAYERS = 2
HEADS = 8
BLK = 128
CHUNK = 64
HALO = 16
RMS_EPS = 1e-6
LN_EPS = 1e-5
ADAM_LR, ADAM_B1, ADAM_B2, ADAM_EPS, ADAM_WD, ADAM_STEP = 0.001, 0.9, 0.999, 1e-8, 0.01, 10
GELU_C = math.sqrt(2.0 / math.pi)
GELU_A = 0.044715

VMEM_LIMIT_V7X = 56 * 1024 * 1024
_TM = 1024
_TN = 1024
_TT = 1024
_TM_MIX = 256
_TM_NORM = 512


def _cparams(n_axes):
    return pltpu.CompilerParams(dimension_semantics=("arbitrary",) * n_axes, vmem_limit_bytes=VMEM_LIMIT_V7X)


def _sds(shape, dtype):
    return jax.ShapeDtypeStruct(tuple(shape), dtype)


def _place():
    return lax.axis_index("x"), lax.axis_index("y"), lax.axis_index("c")


def _index(place):
    return 4 * place[0] + 2 * place[1] + place[2]


class _Piece:
    def __init__(self, operands, out_shapes, aliases, n_sems, start, finish, mid1=None, mid2=None):
        self.operands, self.out_shapes, self.aliases, self.n_sems = list(operands), list(out_shapes), dict(aliases), n_sems
        nothing = lambda ctx: None
        self.start, self.mid1, self.mid2, self.finish = start, mid1 or nothing, mid2 or nothing, finish


class _Ctx:
    def __init__(self, ins, outs, sems, offs):
        self.ins, self.outs, self.sems = ins, outs, sems
        self.o_in, self.o_out, self.o_send, self.o_recv, self.o_loc = offs

    def inp(self, i):
        return self.ins[self.o_in + i]

    def out(self, i):
        return self.outs[self.o_out + i]

    def send(self, k):
        return self.sems[0].at[self.o_send + k]

    def recv(self, k):
        return self.sems[1].at[self.o_recv + k]

    def local(self, k):
        return self.sems[2].at[self.o_loc + k]


class _Hosted:
    def __init__(self, pieces, n_in_before, n_out_before):
        self.pieces = [p for p in (pieces or []) if p is not None]
        self.operands, self.out_shapes, self.aliases, self.offs = [], [], {}, []
        counts = [0, 0, 0]
        for p in self.pieces:
            self.offs.append((len(self.operands), len(self.out_shapes), *counts))
            for i, j in p.aliases.items():
                self.aliases[n_in_before + len(self.operands) + i] = n_out_before + len(self.out_shapes) + j
            self.operands += p.operands
            self.out_shapes += p.out_shapes
            counts = [c + n for c, n in zip(counts, p.n_sems)]
        hbm = pl.BlockSpec(memory_space=pl.ANY)
        self.in_specs = [hbm] * len(self.operands)
        self.out_specs = [hbm] * len(self.out_shapes)
        self.scratch = [pltpu.SemaphoreType.DMA((max(c, 1),)) for c in counts] if self.pieces else []

    def run(self, stage, ins, outs, sems):
        for p, offs in zip(self.pieces, self.offs):
            getattr(p, stage)(_Ctx(ins, outs, sems, offs))

    def wrap(self, grid, compute, ins, outs, sems):
        if not self.pieces:
            compute()
            return
        n_steps = math.prod(grid)
        lin = 0
        for ax, g in enumerate(grid):
            lin = lin * g + pl.program_id(ax)
        pl.when(lin == 0)(lambda: self.run("start", ins, outs, sems))
        compute()
        pl.when(lin == n_steps // 2)(lambda: self.run("mid1", ins, outs, sems))
        pl.when(lin == max(n_steps - 3, n_steps // 2))(lambda: self.run("mid2", ins, outs, sems))
        pl.when(lin == n_steps - 1)(lambda: self.run("finish", ins, outs, sems))


def _cols_view(width):
    return lambda ref, p: ref.at[:, pl.ds(pl.multiple_of(p * width, 128), width)]


def _rows_view(height):
    return lambda ref, p: ref.at[pl.ds(pl.multiple_of(p * height, 16), height), :]


def _cols_halves(rows, width):
    hr = rows // 2
    return (lambda ref, p, h: ref.at[pl.ds(h * hr, hr), pl.ds(pl.multiple_of(p * width, 128), width)],
            lambda ref, h: ref.at[pl.ds(h * hr, hr), :], 2)


def _rows_halves(height):
    hh = height // 2
    return (lambda ref, p, h: ref.at[pl.ds(pl.multiple_of(p * height + h * hh, 16), hh), :],
            lambda ref, h: ref.at[pl.ds(h * hh, hh), :], 2)


_SLOT_WHOLE = (lambda ref, p, h: ref.at[p], lambda ref, h: ref, 1)


def _ag_piece(specs):
    units = [(a, h) for a, s in enumerate(specs) for h in s[3]]

    def plan(ctx):
        x, y, c = _place()
        me, sib, xn, yn, dg = (x, y, c), (x, y, 1 - c), (1 - x, y, c), (x, 1 - y, c), (1 - x, 1 - y, c)

        def copy(u, k, block, to, from_shard=False):
            a, h = units[u]
            dst_of, src_of, _ = specs[a][2]
            dst = dst_of(ctx.out(a), _index(block), h)
            return pltpu.make_async_remote_copy(
                src_ref=src_of(ctx.inp(a), h) if from_shard else dst, dst_ref=dst, send_sem=ctx.send(7 * u + k),
                recv_sem=ctx.recv(7 * u + k), device_id=to, device_id_type=MESH)

        def local(u):
            a, h = units[u]
            dst_of, src_of, _ = specs[a][2]
            return pltpu.make_async_copy(src_of(ctx.inp(a), h), dst_of(ctx.out(a), _index(me), h), ctx.local(u))

        def relay(u):
            return copy(u, 3, xn, yn) if units[u][1] % 2 == 0 else copy(u, 3, yn, xn)

        return me, sib, xn, yn, dg, c, copy, local, relay

    def start(ctx):
        me, sib, xn, yn, dg, c, copy, local, relay = plan(ctx)
        for u in range(len(units)):
            local(u).start()
            for k, to in enumerate((sib, xn, yn)):
                copy(u, k, me, to, from_shard=True).start()

    def mid1(ctx):
        me, sib, xn, yn, dg, c, copy, local, relay = plan(ctx)
        for u in range(len(units)):
            copy(u, 1, xn, me).wait_recv()
            copy(u, 2, yn, me).wait_recv()
            relay(u).start()
            copy(u, 4, xn, sib).start()
            copy(u, 5, yn, sib).start()

    def mid2(ctx):
        me, sib, xn, yn, dg, c, copy, local, relay = plan(ctx)
        for u in range(len(units)):
            copy(u, 3, dg, me).wait_recv()
            copy(u, 6, dg, sib).start()

    def finish(ctx):
        me, sib, xn, yn, dg, c, copy, local, relay = plan(ctx)
        other = lambda place: (place[0], place[1], 1 - c)
        for u in range(len(units)):
            for k, block in ((0, sib), (4, other(xn)), (5, other(yn)), (6, other(dg))):
                copy(u, k, block, me).wait_recv()
        for u in range(len(units)):
            for k, to in enumerate((sib, xn, yn)):
                copy(u, k, me, to, from_shard=True).wait_send()
            relay(u).wait_send()
            for k, block in ((4, xn), (5, yn), (6, dg)):
                copy(u, k, block, sib).wait_send()
            local(u).wait()

    n_u = len(units)
    operands, aliases = [s[0] for s in specs], {}
    for a, spec in enumerate(specs):
        if spec[4] is not None:
            aliases[len(operands)] = a
            operands.append(spec[4])
    return _Piece(operands, [s[1] for s in specs], aliases, (7 * n_u, 7 * n_u, n_u), start, finish, mid1, mid2)


N_CHIPS = 4


def _rs_core_piece(specs):
    n = len(specs)

    def copies(ctx):
        x, y, c = _place()
        out = []
        for a in range(n):
            for q in range(N_CHIPS):
                out.append(pltpu.make_async_remote_copy(
                    src_ref=specs[a][2](ctx.inp(a), 2 * q + (1 - c)), dst_ref=ctx.out(a).at[q],
                    send_sem=ctx.send(N_CHIPS * a + q), recv_sem=ctx.recv(N_CHIPS * a + q), device_id=(x, y, 1 - c),
                    device_id_type=MESH))
        return out

    def start(ctx):
        for cp in copies(ctx):
            cp.start()

    def finish(ctx):
        for cp in copies(ctx):
            cp.wait_recv()
            cp.wait_send()

    return _Piece([s[0] for s in specs], [s[1] for s in specs], {}, (N_CHIPS * n, N_CHIPS * n, 0), start, finish)


def _rs_chip_piece(specs, layer):
    n = len(specs)
    hops = [(1, 0), (0, 1), (1, 1)]

    def copies(ctx):
        x, y, c = _place()
        mine = 2 * x + y
        out = []
        for a in range(n):
            rows = pl.ds(*specs[a][2])
            sums, land = ctx.inp(a), ctx.out(a)
            out.append((pltpu.make_async_copy(sums.at[mine, rows], land.at[layer, mine, rows], ctx.local(a)), None))
            for j, (dx, dy) in enumerate(hops):
                px, py = x ^ dx, y ^ dy
                peer = 2 * px + py
                send = pltpu.make_async_remote_copy(
                    src_ref=sums.at[peer, rows], dst_ref=land.at[layer, mine, rows], send_sem=ctx.send(3 * a + j),
                    recv_sem=ctx.recv(3 * a + j), device_id=(px, py, c), device_id_type=MESH)
                recv = pltpu.make_async_remote_copy(
                    src_ref=sums.at[peer, rows], dst_ref=land.at[layer, peer, rows], send_sem=ctx.send(3 * a + j),
                    recv_sem=ctx.recv(3 * a + j), device_id=(px, py, c), device_id_type=MESH)
                out.append((send, recv))
        return out

    def start(ctx):
        for send, _ in copies(ctx):
            send.start()

    def finish(ctx):
        for send, recv in copies(ctx):
            if recv is None:
                send.wait()
            else:
                recv.wait_recv()
                send.wait_send()

    operands, aliases = [s[0] for s in specs], {}
    for a, spec in enumerate(specs):
        if spec[3] is not None:
            aliases[len(operands)] = a
            operands.append(spec[3])
    return _Piece(operands, [s[1] for s in specs], aliases, (3 * n, 3 * n, n), start, finish)


def _chip_sums(name, grad, stage, by_cols, core):
    _, r, c = stage.shape
    tr = r
    while tr * c > 1024 * 1024 or r % tr or tr % 16:
        tr -= 16
    n_t = r // tr

    def body(core_ref, g_ref, s_ref, o_ref):
        o_ref[...] = (g_ref[...].astype(F32) + s_ref[...].astype(F32)).astype(BF16)

    if by_cols:
        gspec = pl.BlockSpec((tr, c), lambda q, i, core_ref: (i, 2 * q + core_ref[0]))
    else:
        gspec = pl.BlockSpec((tr, c), lambda q, i, core_ref: ((2 * q + core_ref[0]) * n_t + i, 0))
    sspec = pl.BlockSpec((None, tr, c), lambda q, i, core_ref: (q, i, 0))
    return pl.pallas_call(
        body, name=name, out_shape=_sds(stage.shape, BF16),
        grid_spec=pltpu.PrefetchScalarGridSpec(num_scalar_prefetch=1, grid=(N_CHIPS, n_t), in_specs=[gspec, sspec],
                                               out_specs=sspec),
        compiler_params=_cparams(2))(core, grad, stage)


def _call_hosting(body, name, grid, out_shapes, in_specs, out_specs, operands, scratch, comm):
    n_in, n_out, n_scr = len(operands), len(out_shapes), len(scratch)
    hosted = _Hosted(comm, n_in, n_out)
    n_ci, n_co = len(hosted.operands), len(hosted.out_shapes)

    def hosting_body(*refs):
        ins, rest = refs[:n_in], refs[n_in:]
        c_ins, rest = rest[:n_ci], rest[n_ci:]
        outs, rest = rest[:n_out], rest[n_out:]
        c_outs, rest = rest[:n_co], rest[n_co:]
        hosted.wrap(grid, lambda: body(*ins, *outs, *rest[:n_scr]), c_ins, c_outs, rest[n_scr:])

    res = pl.pallas_call(
        hosting_body, name=name, grid=grid, out_shape=tuple(list(out_shapes) + hosted.out_shapes),
        in_specs=list(in_specs) + hosted.in_specs, out_specs=tuple(list(out_specs) + hosted.out_specs),
        input_output_aliases=hosted.aliases, scratch_shapes=list(scratch) + hosted.scratch,
        compiler_params=_cparams(len(grid)))(*operands, *hosted.operands)
    return list(res[:n_out]), list(res[n_out:])


def _matmul(name, grid, nk, kaxis, pairs, dims, extras, outs, epilogue, sum_pairs, acc_shape, comm=None, split=None):
    n_p, n_e, n_o = len(pairs), len(extras), len(outs)
    n_acc = 0 if nk == 1 else (1 if sum_pairs else n_p)
    n_in = 2 * n_p + n_e
    hosted = _Hosted(comm, n_in, n_o)
    n_ci, n_co = len(hosted.operands), len(hosted.out_shapes)

    def body(*refs):
        a_refs = refs[0:2 * n_p:2]
        b_refs = refs[1:2 * n_p:2]
        e_refs = refs[2 * n_p:n_in]
        c_ins = refs[n_in:n_in + n_ci]
        o_refs = refs[n_in + n_ci:n_in + n_ci + n_o]
        c_outs = refs[n_in + n_ci + n_o:n_in + n_ci + n_o + n_co]
        acc_refs = refs[n_in + n_ci + n_o + n_co:n_in + n_ci + n_o + n_co + n_acc]
        sems = refs[n_in + n_ci + n_o + n_co + n_acc:]

        def dots():
            prods = [lax.dot_general(a[...], b[...], (dims, ((), ())), preferred_element_type=F32)
                     for a, b in zip(a_refs, b_refs)]
            if sum_pairs and n_p > 1:
                prods = [functools.reduce(operator.add, prods)]
            return prods

        def compute():
            if nk == 1 and split is not None:
                n_split, b_axis, n_row = split
                width = b_refs[0].shape[b_axis] // n_split
                height = a_refs[0].shape[0] // n_row
                for s in range(n_split):
                    cols = pl.ds(s * width, width)
                    for r in range(n_row):
                        rows = pl.ds(r * height, height)
                        epilogue([lax.dot_general(a[rows, :], b[cols, :] if b_axis == 0 else b[:, cols], (dims, ((), ())),
                                                  preferred_element_type=F32) for a, b in zip(a_refs, b_refs)],
                                 e_refs, o_refs, rows, cols)
                return
            if nk == 1:
                epilogue(dots(), e_refs, o_refs)
                return
            k = pl.program_id(kaxis)

            @pl.when(k == 0)
            def _():
                for acc, p in zip(acc_refs, dots()):
                    acc[...] = p

            if nk > 2:
                @pl.when((k > 0) & (k < nk - 1))
                def _():
                    for acc, p in zip(acc_refs, dots()):
                        acc[...] += p

            @pl.when(k == nk - 1)
            def _():
                epilogue([acc[...] + p for acc, p in zip(acc_refs, dots())], e_refs, o_refs)

        hosted.wrap(grid, compute, c_ins, c_outs, sems)

    operands, in_specs = [], []
    for a, a_spec, b, b_spec in pairs:
        operands += [a, b]
        in_specs += [a_spec, b_spec]
    for e, e_spec in extras:
        operands.append(e)
        in_specs.append(e_spec)
    res = pl.pallas_call(
        body, name=name, grid=grid,
        out_shape=tuple([o for o, _ in outs] + hosted.out_shapes),
        in_specs=in_specs + hosted.in_specs, out_specs=tuple([s for _, s in outs] + hosted.out_specs),
        input_output_aliases=hosted.aliases,
        scratch_shapes=[pltpu.VMEM(acc_shape, F32) for _ in range(n_acc)] + hosted.scratch,
        compiler_params=_cparams(len(grid)),
    )(*operands, *hosted.operands)
    return list(res[:n_o]), list(res[n_o:])


NN = ((1,), (0,))
NT = ((1,), (1,))
TN = ((0,), (0,))


def _tile(n, want):
    if n <= want:
        return n
    t = want // 128 * 128
    while n % t:
        t -= 128
    return t


def _silu_parts(g):
    s = 0.5 + 0.5 * jnp.tanh(0.5 * g)
    return s, g * s


def _mm_in(h, w_in, comm=None):
    t, d = h.shape
    n = w_in.shape[1]
    tm, tn = _tile(t, _TM), _tile(n, _TN)

    def epi(accs, e, o):
        o[0][...] = accs[0].astype(BF16)

    outs, couts = _matmul(
        "mm_in", (n // tn, t // tm), 1, None,
        [(h, pl.BlockSpec((tm, d), lambda j, i: (i, 0)), w_in, pl.BlockSpec((d, tn), lambda j, i: (0, j)))],
        NN, [], [(_sds((t, n), BF16), pl.BlockSpec((tm, tn), lambda j, i: (i, j)))], epi, True, None, comm)
    return outs[0], couts


def _mm_out(y, w_out, x, comm=None):
    t, m = y.shape
    d = w_out.shape[1]
    tm, tn = _tile(t, _TM), _tile(d, _TN)

    def epi(accs, e, o):
        o[0][...] = e[0][...] + accs[0]

    outs, couts = _matmul(
        "mm_out", (t // tm, d // tn), 1, None,
        [(y, pl.BlockSpec((tm, m), lambda i, j: (i, 0)), w_out, pl.BlockSpec((m, tn), lambda i, j: (0, j)))],
        NN, [(x, pl.BlockSpec((tm, tn), lambda i, j: (i, j)))],
        [(_sds((t, d), F32), pl.BlockSpec((tm, tn), lambda i, j: (i, j)))], epi, True, None, comm)
    return outs[0], couts


def _mm_swiglu(h2, wgt, wut, comm=None):
    t, d = h2.shape
    f = wgt.shape[0]
    tm, tn = _tile(t, _TM), _tile(f, 512)

    def epi(accs, e, o, rows, cols):
        g, u = accs
        _, sg = _silu_parts(g)
        o[0][rows, cols] = g.astype(BF16)
        o[1][rows, cols] = u.astype(BF16)
        o[2][rows, cols] = (sg * u).astype(BF16)

    wspec = pl.BlockSpec((tn, d), lambda j, i: (j, 0))
    hspec = pl.BlockSpec((tm, d), lambda j, i: (i, 0))
    ospec = pl.BlockSpec((tm, tn), lambda j, i: (i, j))
    osh = _sds((t, f), BF16)
    outs, couts = _matmul("mm_swiglu", (f // tn, t // tm), 1, None, [(h2, hspec, wgt, wspec), (h2, hspec, wut, wspec)],
                          NT, [], [(osh, ospec)] * 3, epi, False, None, comm, split=(tn // 256, 0, 2))
    return outs, couts


def _mm_down(act, wd, x1, comm=None):
    t, f = act.shape
    d = wd.shape[1]
    tm, tn = _tile(t, _TM), _tile(d, _TN)
    nk = 2
    tk = f // nk

    def epi(accs, e, o):
        o[0][...] = e[0][...] + accs[0]

    outs, couts = _matmul(
        "mm_down", (t // tm, d // tn, nk), nk, 2,
        [(act, pl.BlockSpec((tm, tk), lambda i, j, k: (i, k)), wd, pl.BlockSpec((tk, tn), lambda i, j, k: (k, j)))],
        NN, [(x1, pl.BlockSpec((tm, tn), lambda i, j, k: (i, j)))],
        [(_sds((t, d), F32), pl.BlockSpec((tm, tn), lambda i, j, k: (i, j)))], epi, True, (tm, tn), comm)
    return outs[0], couts


def _mm_dact(dxb, wd, gate, up, comm=None):
    t, d = dxb.shape
    f = wd.shape[0]
    tm, tn = _tile(t, _TM), _tile(f, 512)

    def epi(accs, e, o, rows, cols):
        da = accs[0]
        g = e[0][rows, cols].astype(F32)
        u = e[1][rows, cols].astype(F32)
        s, sg = _silu_parts(g)
        o[0][rows, cols] = (da * u * (s + sg * (1.0 - s))).astype(BF16)
        o[1][rows, cols] = (da * sg).astype(BF16)

    bspec = pl.BlockSpec((tm, tn), lambda j, i: (i, j))
    osh = _sds((t, f), BF16)
    outs, couts = _matmul(
        "mm_dact", (f // tn, t // tm), 1, None,
        [(dxb, pl.BlockSpec((tm, d), lambda j, i: (i, 0)), wd, pl.BlockSpec((tn, d), lambda j, i: (j, 0)))],
        NT, [(gate, bspec), (up, bspec)], [(osh, bspec)] * 2, epi, True, None, comm, split=(tn // 256, 0, 2))
    return outs, couts


def _mm_dh2(dgate, dup, wgt, wut, comm=None):
    t, f = dgate.shape
    d = wgt.shape[1]
    tm, tn = _tile(t, _TM), _tile(d, _TN)
    nk = 4
    tk = f // nk

    def epi(accs, e, o):
        o[0][...] = accs[0]

    aspec = pl.BlockSpec((tm, tk), lambda i, j, k: (i, k))
    wspec = pl.BlockSpec((tk, tn), lambda i, j, k: (k, j))
    outs, couts = _matmul("mm_dh2", (t // tm, d // tn, nk), nk, 2, [(dgate, aspec, wgt, wspec), (dup, aspec, wut, wspec)],
                          NN, [], [(_sds((t, d), F32), pl.BlockSpec((tm, tn), lambda i, j, k: (i, j)))], epi, True,
                          (tm, tn), comm)
    return outs[0], couts


def _mm_dw(name, a_list, b, tmo, tno, comm=None):
    t, m = a_list[0].shape
    n = b.shape[1]
    tt = _tile(t, _TT)
    nk = t // tt
    tmo, tno = _tile(m, tmo), _tile(n, tno)

    def epi(accs, e, o):
        for acc, out in zip(accs, o):
            out[...] = acc.astype(BF16)

    aspec = pl.BlockSpec((tt, tmo), lambda i, j, k: (k, i))
    bspec = pl.BlockSpec((tt, tno), lambda i, j, k: (k, j))
    ospec = pl.BlockSpec((tmo, tno), lambda i, j, k: (i, j))
    if nk == 1:
        return _matmul(name, (m // tmo, n // tno, 1), 1, None, [(a, aspec, b, bspec) for a in a_list], TN, [],
                       [(_sds((m, n), BF16), ospec)] * len(a_list), epi, False, None, comm)
    return _matmul(name, (m // tmo, n // tno, nk), nk, 2, [(a, aspec, b, bspec) for a in a_list], TN, [],
                   [(_sds((m, n), BF16), ospec)] * len(a_list), epi, False, (tmo, tno), comm)


def _mm_dy(dxb, w_out, comm=None):
    t, d = dxb.shape
    m = w_out.shape[0]
    tm, tn = _tile(t, _TM), _tile(m, _TN)

    def epi(accs, e, o):
        o[0][...] = accs[0].astype(BF16)

    outs, couts = _matmul(
        "mm_dy", (t // tm, m // tn), 1, None,
        [(dxb, pl.BlockSpec((tm, d), lambda i, j: (i, 0)), w_out, pl.BlockSpec((tn, d), lambda i, j: (j, 0)))], NT, [],
        [(_sds((t, m), BF16), pl.BlockSpec((tm, tn), lambda i, j: (i, j)))], epi, True, None, comm)
    return outs[0], couts


def _mm_dh(dz, w_in, comm=None):
    t, n = dz.shape
    d = w_in.shape[0]
    tm, tn = _tile(t, _TM), _tile(d, _TN)
    nk = 2
    tk = n // nk

    def epi(accs, e, o):
        o[0][...] = accs[0]

    outs, couts = _matmul(
        "mm_dh", (t // tm, d // tn, nk), nk, 2,
        [(dz, pl.BlockSpec((tm, tk), lambda i, j, k: (i, k)), w_in, pl.BlockSpec((tn, tk), lambda i, j, k: (j, k)))], NT,
        [], [(_sds((t, d), F32), pl.BlockSpec((tm, tn), lambda i, j, k: (i, j)))], epi, True, (tm, tn), comm)
    return outs[0], couts


def _rmsnorm_fwd(x, g, comm=None):
    t, d = x.shape
    tm = min(_TM_NORM, t)

    def body(x_ref, g_ref, o_ref):
        xv = x_ref[...]
        rs = lax.rsqrt(jnp.mean(xv * xv, axis=-1, keepdims=True) + RMS_EPS)
        o_ref[...] = (xv * rs * g_ref[...]).astype(BF16)

    outs, couts = _call_hosting(
        body, "rmsnorm_fwd", (t // tm,), [_sds((t, d), BF16)],
        [pl.BlockSpec((tm, d), lambda i: (i, 0)), pl.BlockSpec((1, d), lambda i: (0, 0))],
        [pl.BlockSpec((tm, d), lambda i: (i, 0))], [x, g], [], comm)
    return outs[0], couts


def _rmsnorm_bwd_math(xv, g, dh):
    rs = lax.rsqrt(jnp.mean(xv * xv, axis=-1, keepdims=True) + RMS_EPS)
    xh = xv * rs
    gd = dh * g
    dx = rs * (gd - xh * jnp.mean(gd * xh, axis=-1, keepdims=True))
    return dx, jnp.sum(dh * xh, axis=0, keepdims=True)


def _rmsnorm_bwd(x, g, dh, dres):
    t, d = x.shape
    tm = min(_TM_NORM, t)

    def body(x_ref, g_ref, dh_ref, dres_ref, dx_ref, dxb_ref, dg_ref):
        dx, dg = _rmsnorm_bwd_math(x_ref[...], g_ref[...], dh_ref[...])
        dx = dx + dres_ref[...]
        dx_ref[...] = dx
        dxb_ref[...] = dx.astype(BF16)

        @pl.when(pl.program_id(0) == 0)
        def _():
            dg_ref[...] = dg

        @pl.when(pl.program_id(0) > 0)
        def _():
            dg_ref[...] += dg

    row = pl.BlockSpec((tm, d), lambda i: (i, 0))
    vec = pl.BlockSpec((1, d), lambda i: (0, 0))
    return pl.pallas_call(
        body, name="rmsnorm_bwd", grid=(t // tm,),
        out_shape=(_sds((t, d), F32), _sds((t, d), BF16), _sds((1, d), F32)),
        in_specs=[row, vec, row, row], out_specs=(row, row, vec), compiler_params=_cparams(1))(x, g, dh, dres)


def _loss_head(x, g, target):
    t, d = x.shape
    tm = min(_TM_NORM, t)

    def body(x_ref, g_ref, t_ref, dx_ref, dxb_ref, dg_ref, loss_ref):
        xv, gv = x_ref[...], g_ref[...]
        rs = lax.rsqrt(jnp.mean(xv * xv, axis=-1, keepdims=True) + RMS_EPS)
        diff = xv * rs * gv - t_ref[...]
        part = 0.5 * jnp.sum(jnp.mean(diff * diff, axis=-1, keepdims=True), axis=0, keepdims=True)
        part = jnp.broadcast_to(part, (1, 128))
        dx, dg = _rmsnorm_bwd_math(xv, gv, diff * (1.0 / d))
        dx_ref[...] = dx
        dxb_ref[...] = dx.astype(BF16)

        @pl.when(pl.program_id(0) == 0)
        def _():
            dg_ref[...] = dg
            loss_ref[...] = part

        @pl.when(pl.program_id(0) > 0)
        def _():
            dg_ref[...] += dg
            loss_ref[...] += part

    row = pl.BlockSpec((tm, d), lambda i: (i, 0))
    vec = pl.BlockSpec((1, d), lambda i: (0, 0))
    return pl.pallas_call(
        body, name="loss_head", grid=(t // tm,),
        out_shape=(_sds((t, d), F32), _sds((t, d), BF16), _sds((1, d), F32), _sds((1, 128), F32)),
        in_specs=[row, vec, row], out_specs=(row, row, vec, pl.BlockSpec((1, 128), lambda i: (0, 0))),
        compiler_params=_cparams(1))(x, g, target)


def _gelu(x):
    th = jnp.tanh(GELU_C * (x + GELU_A * x * x * x))
    return 0.5 * x * (1.0 + th), th


def _gelu_grad(x, th):
    return 0.5 * (1.0 + th) + 0.5 * x * (1.0 - th * th) * GELU_C * (1.0 + 3.0 * GELU_A * x * x)


def _masked_ws(ws_ref, h):
    i = lax.broadcasted_iota(jnp.int32, (BLK, BLK), 0) // CHUNK
    j = lax.broadcasted_iota(jnp.int32, (BLK, BLK), 1) // CHUNK
    return jnp.where(j <= i, ws_ref[h], 0.0)


def _shift_down(q, n, first_rows):
    rolled = pltpu.roll(q, n, 0)
    row = lax.broadcasted_iota(jnp.int32, q.shape, 0)
    for r, val in enumerate(first_rows):
        rolled = jnp.where(row == r, val, rolled)
    return rolled


def _shift_up(q, n, last_rows):
    tm = q.shape[0]
    rolled = pltpu.roll(q, tm - n, 0)
    row = lax.broadcasted_iota(jnp.int32, q.shape, 0)
    for r, val in enumerate(last_rows):
        rolled = jnp.where(row == tm - n + r, val, rolled)
    return rolled


def _mixer_specs(t, a, tm):
    hb = tm // HALO
    last = t // HALO - 1
    tile = pl.BlockSpec((tm, 5 * a), lambda i: (i, 0))
    prev = [pl.BlockSpec((HALO, a), functools.partial(lambda i, col: (jnp.maximum(i * hb - 1, 0), col), col=col))
            for col in (3, 4)]
    nxt = [pl.BlockSpec((HALO, a), functools.partial(lambda i, col: (jnp.minimum((i + 1) * hb, last), col), col=col))
           for col in (2, 3, 4)]
    return tile, prev, nxt


def _group_a_fwd(zu, zv, lng, lnb, ws_ref, bb_ref, mixed_ref, vln_ref):
    u, thu = _gelu(zu)
    v, thv = _gelu(zv)
    mu = jnp.mean(v, axis=-1, keepdims=True)
    vc = v - mu
    rs = lax.rsqrt(jnp.mean(vc * vc, axis=-1, keepdims=True) + LN_EPS)
    vhat = vc * rs
    vln_ref[...] = vhat * lng + lnb
    tm, a = zu.shape
    hd = a // HEADS
    for h in range(HEADS):
        w = _masked_ws(ws_ref, h).astype(BF16)
        for b in range(tm // BLK):
            rows, cols = pl.ds(b * BLK, BLK), pl.ds(h * hd, hd)
            mixed_ref[rows, cols] = jnp.dot(w, vln_ref[rows, cols].astype(BF16), preferred_element_type=F32) + bb_ref[h]
    return u, thu, thv, rs, vhat


def _mixer_fwd(z, ln_g, ln_b, w_spatial, bb, conv_w, gg, comm=None):
    t = z.shape[0]
    a = z.shape[1] // 5
    tm = min(_TM_MIX, t)
    tile, prev, _ = _mixer_specs(t, a, tm)

    def body(z_ref, pc_ref, ph_ref, lng_ref, lnb_ref, ws_ref, bb_ref, cw_ref, gg_ref, y_ref, mixed_ref, vln_ref):
        i = pl.program_id(0)
        zu = z_ref[:, 0:a].astype(F32)
        zv = z_ref[:, a:2 * a].astype(F32)
        u, _, _, _, _ = _group_a_fwd(zu, zv, lng_ref[...], lnb_ref[...], ws_ref, bb_ref, mixed_ref, vln_ref)
        ya = u * mixed_ref[...]
        ra = lax.rsqrt(jnp.mean(ya * ya, axis=-1, keepdims=True) + RMS_EPS)
        y_ref[:, 0:a] = (ya * ra * gg_ref[:, 0:a]).astype(BF16)

        zb = z_ref[:, 2 * a:3 * a].astype(F32)
        q = z_ref[:, 3 * a:4 * a].astype(F32) * z_ref[:, 4 * a:5 * a].astype(F32)
        qp = jnp.where(i > 0, pc_ref[...].astype(F32) * ph_ref[...].astype(F32), 0.0)
        qm1 = _shift_down(q, 1, [qp[HALO - 1:HALO]])
        qm2 = _shift_down(q, 2, [qp[HALO - 2:HALO - 1], qp[HALO - 1:HALO]])
        cv = cw_ref[0:1, :] * qm2 + cw_ref[1:2, :] * qm1 + cw_ref[2:3, :] * q
        yb = zb * cv
        rb = lax.rsqrt(jnp.mean(yb * yb, axis=-1, keepdims=True) + RMS_EPS)
        y_ref[:, a:2 * a] = (yb * rb * gg_ref[:, a:2 * a]).astype(BF16)

    full = lambda shape: pl.BlockSpec(shape, lambda i: (0,) * len(shape))
    outs, couts = _call_hosting(
        body, "mixer_fwd", (t // tm,), [_sds((t, 2 * a), BF16)],
        [tile, *prev, full((1, a)), full((1, a)), full(w_spatial.shape), full(bb.shape), full(conv_w.shape),
         full((1, 2 * a))],
        [pl.BlockSpec((tm, 2 * a), lambda i: (i, 0))], [z, z, z, ln_g, ln_b, w_spatial, bb, conv_w, gg],
        [pltpu.VMEM((tm, a), F32), pltpu.VMEM((tm, a), F32)], comm)
    return outs[0], couts


def _mixer_bwd(z, dy, ln_g, ln_b, w_spatial, bb, conv_w, gg, comm=None):
    t = z.shape[0]
    a = z.shape[1] // 5
    hd = a // HEADS
    tm = min(_TM_MIX, t)
    n_tiles = t // tm
    tile, prev, nxt = _mixer_specs(t, a, tm)
    hb = tm // HALO
    dy_tile = pl.BlockSpec((tm, 2 * a), lambda i: (i, 0))
    dy_next = pl.BlockSpec((HALO, a), lambda i: (jnp.minimum((i + 1) * hb, t // HALO - 1), 1))

    def body(z_ref, pc_ref, ph_ref, nb_ref, nc_ref, nh_ref, dy_ref, ndy_ref, lng_ref, lnb_ref, ws_ref, bb_ref, cw_ref,
             gg_ref, dz_ref, dlng_ref, dlnb_ref, dws_ref, dbb_ref, dcw_ref, dgg_ref, mixed_ref, vln_ref, dmix_ref,
             dvln_ref):
        i = pl.program_id(0)

        @pl.when(i == 0)
        def _():
            for ref in (dlng_ref, dlnb_ref, dws_ref, dbb_ref, dcw_ref, dgg_ref):
                ref[...] = jnp.zeros(ref.shape, F32)

        lng = lng_ref[...]
        zu = z_ref[:, 0:a].astype(F32)
        zv = z_ref[:, a:2 * a].astype(F32)
        u, thu, thv, rs, vhat = _group_a_fwd(zu, zv, lng, lnb_ref[...], ws_ref, bb_ref, mixed_ref, vln_ref)
        mixed = mixed_ref[...]
        ya = u * mixed
        ra = lax.rsqrt(jnp.mean(ya * ya, axis=-1, keepdims=True) + RMS_EPS)
        da = dy_ref[:, 0:a].astype(F32)
        yah = ya * ra
        dgg_ref[:, 0:a] += jnp.sum(da * yah, axis=0, keepdims=True)
        ga = da * gg_ref[:, 0:a]
        dya = ra * (ga - yah * jnp.mean(ga * yah, axis=-1, keepdims=True))
        dz_ref[:, 0:a] = (dya * mixed * _gelu_grad(zu, thu)).astype(BF16)
        dmix_ref[...] = dya * u
        for h in range(HEADS):
            w = _masked_ws(ws_ref, h).astype(BF16)
            dw = jnp.zeros((BLK, BLK), F32)
            db = jnp.zeros((BLK, hd), F32)
            for b in range(tm // BLK):
                rows, cols = pl.ds(b * BLK, BLK), pl.ds(h * hd, hd)
                dm = dmix_ref[rows, cols]
                dmb = dm.astype(BF16)
                db = db + dm
                dw = dw + lax.dot_general(dmb, vln_ref[rows, cols].astype(BF16), (NT, ((), ())),
                                          preferred_element_type=F32)
                dvln_ref[rows, cols] = lax.dot_general(w, dmb, (TN, ((), ())), preferred_element_type=F32)
            dws_ref[h] += dw
            dbb_ref[h] += db
        dvln = dvln_ref[...]
        dlng_ref[...] += jnp.sum(dvln * vhat, axis=0, keepdims=True)
        dlnb_ref[...] += jnp.sum(dvln, axis=0, keepdims=True)
        dvh = dvln * lng
        dv = rs * (dvh - jnp.mean(dvh, axis=-1, keepdims=True) - vhat * jnp.mean(dvh * vhat, axis=-1, keepdims=True))
        dz_ref[:, a:2 * a] = (dv * _gelu_grad(zv, thv)).astype(BF16)

        w0, w1, w2 = cw_ref[0:1, :], cw_ref[1:2, :], cw_ref[2:3, :]
        ggb = gg_ref[:, a:2 * a]
        zb = z_ref[:, 2 * a:3 * a].astype(F32)
        zc = z_ref[:, 3 * a:4 * a].astype(F32)
        zh = z_ref[:, 4 * a:5 * a].astype(F32)
        q = zc * zh
        qp = jnp.where(i > 0, pc_ref[...].astype(F32) * ph_ref[...].astype(F32), 0.0)
        qm1 = _shift_down(q, 1, [qp[HALO - 1:HALO]])
        qm2 = _shift_down(q, 2, [qp[HALO - 2:HALO - 1], qp[HALO - 1:HALO]])
        cv = w0 * qm2 + w1 * qm1 + w2 * q

        def conv_out_grad(zb_, cv_, dout_):
            yb = zb_ * cv_
            rb = lax.rsqrt(jnp.mean(yb * yb, axis=-1, keepdims=True) + RMS_EPS)
            ybh = yb * rb
            gb = dout_ * ggb
            dyb = rb * (gb - ybh * jnp.mean(gb * ybh, axis=-1, keepdims=True))
            return dyb * zb_, dyb * cv_, ybh

        db_out = dy_ref[:, a:2 * a].astype(F32)
        g, dzb, ybh = conv_out_grad(zb, cv, db_out)
        dgg_ref[:, a:2 * a] += jnp.sum(db_out * ybh, axis=0, keepdims=True)
        dz_ref[:, 2 * a:3 * a] = dzb.astype(BF16)
        qn = nc_ref[...].astype(F32) * nh_ref[...].astype(F32)
        zbn = nb_ref[...].astype(F32)
        cvn = w0 * _shift_down(qn, 2, [q[tm - 2:tm - 1], q[tm - 1:tm]]) + w1 * _shift_down(qn, 1, [q[tm - 1:tm]]) + w2 * qn
        gn, _, _ = conv_out_grad(zbn, cvn, ndy_ref[...].astype(F32))
        gn = jnp.where(i < n_tiles - 1, gn, 0.0)
        dq = w2 * g + w1 * _shift_up(g, 1, [gn[0:1]]) + w0 * _shift_up(g, 2, [gn[0:1], gn[1:2]])
        dz_ref[:, 3 * a:4 * a] = (dq * zh).astype(BF16)
        dz_ref[:, 4 * a:5 * a] = (dq * zc).astype(BF16)
        dcw_ref[0:1, :] += jnp.sum(g * qm2, axis=0, keepdims=True)
        dcw_ref[1:2, :] += jnp.sum(g * qm1, axis=0, keepdims=True)
        dcw_ref[2:3, :] += jnp.sum(g * q, axis=0, keepdims=True)

        @pl.when(i == n_tiles - 1)
        def _():
            for h in range(HEADS):
                dbb_ref[h] = jnp.broadcast_to(jnp.sum(dbb_ref[h], axis=1, keepdims=True), (BLK, hd))
                dws_ref[h] = _masked_ws(dws_ref, h)

    full = lambda shape: pl.BlockSpec(tuple(shape), lambda i: (0,) * len(shape))
    out_shapes = (_sds((t, 5 * a), BF16), _sds((1, a), F32), _sds((1, a), F32), _sds(w_spatial.shape, F32),
                  _sds(bb.shape, F32), _sds((8, a), F32), _sds((1, 2 * a), F32))
    return _call_hosting(
        body, "mixer_bwd", (n_tiles,), out_shapes,
        [tile, *prev, *nxt, dy_tile, dy_next, full((1, a)), full((1, a)), full(w_spatial.shape), full(bb.shape),
         full(conv_w.shape), full((1, 2 * a))],
        [tile, *[full(s.shape) for s in out_shapes[1:]]], [z, z, z, z, z, z, dy, dy, ln_g, ln_b, w_spatial, bb, conv_w, gg],
        [pltpu.VMEM((tm, a), F32)] * 4, comm)


def _all_reduce_small(pack, comm=None):
    r = pack.shape[0]
    hosted = _Hosted(comm, 1, 1)
    n_ci, n_co = len(hosted.operands), len(hosted.out_shapes)

    def body(*refs):
        in_ref, c_ins, out_ref, c_outs = refs[0], refs[1:1 + n_ci], refs[1 + n_ci], refs[2 + n_ci:2 + n_ci + n_co]
        acc_ref, recv_ref, send_sems, recv_sems = refs[2 + n_ci + n_co:6 + n_ci + n_co]
        sems = refs[6 + n_ci + n_co:]
        hosted.run("start", c_ins, c_outs, sems)
        x, y, c = _place()
        partners = [(x, y, 1 - c), (1 - x, y, c), (x, 1 - y, c)]
        acc_ref[0] = in_ref[...]
        for s, partner in enumerate(partners):
            cp = pltpu.make_async_remote_copy(
                src_ref=acc_ref.at[s], dst_ref=recv_ref.at[s], send_sem=send_sems.at[s], recv_sem=recv_sems.at[s],
                device_id=partner, device_id_type=MESH)
            cp.start()
            cp.wait()
            if s < 2:
                acc_ref[s + 1] = acc_ref[s] + recv_ref[s]
            else:
                out_ref[...] = acc_ref[s] + recv_ref[s]
        for stage in ("mid1", "mid2", "finish"):
            hosted.run(stage, c_ins, c_outs, sems)

    vmem = pl.BlockSpec(memory_space=pltpu.VMEM)
    res = pl.pallas_call(
        body, name="all_reduce_small", out_shape=tuple([_sds(pack.shape, F32)] + hosted.out_shapes),
        in_specs=[vmem] + hosted.in_specs, out_specs=tuple([vmem] + hosted.out_specs),
        input_output_aliases=hosted.aliases,
        scratch_shapes=[pltpu.VMEM((3, r, 128), F32), pltpu.VMEM((3, r, 128), F32), pltpu.SemaphoreType.DMA((3,)),
                        pltpu.SemaphoreType.DMA((3,))] + hosted.scratch,
        compiler_params=pltpu.CompilerParams(vmem_limit_bytes=VMEM_LIMIT_V7X),
    )(pack, *hosted.operands)
    return res[0], list(res[1:])


def _adamw_math(w, g, m, v):
    m = ADAM_B1 * m + (1.0 - ADAM_B1) * g
    v = ADAM_B2 * v + (1.0 - ADAM_B2) * (g * g)
    m_hat = m / (1.0 - ADAM_B1 ** ADAM_STEP)
    v_hat = v / (1.0 - ADAM_B2 ** ADAM_STEP)
    delta = -ADAM_LR * (m_hat / (jnp.sqrt(v_hat) + ADAM_EPS) + ADAM_WD * w)
    return delta, m, v


def _adamw_big(name, land, w, m, v, comm=None):
    nl, n_slots, r, c = land.shape
    tr = max(8, min(r, (256 * 640) // c // 8 * 8))
    while r % tr:
        tr -= 8
    grid = (nl, r // tr)
    hosted = _Hosted(comm, 4, 4)
    n_ci, n_co = len(hosted.operands), len(hosted.out_shapes)

    def body(*refs):
        land_ref, w_ref, m_ref, v_ref = refs[:4]
        c_ins = refs[4:4 + n_ci]
        g_out, d_out, m_out, v_out = refs[4 + n_ci:8 + n_ci]
        c_outs = refs[8 + n_ci:8 + n_ci + n_co]
        sems = refs[8 + n_ci + n_co:]

        def compute():
            g = land_ref[0].astype(F32)
            for s in range(1, n_slots):
                g = g + land_ref[s].astype(F32)
            delta, mn, vn = _adamw_math(w_ref[...], g, m_ref[...], v_ref[...])
            g_out[...] = g
            d_out[...] = delta
            m_out[...] = mn
            v_out[...] = vn

        hosted.wrap(grid, compute, c_ins, c_outs, sems)

    blk = pl.BlockSpec((None, tr, c), lambda l, i: (l, i, 0))
    res = pl.pallas_call(
        body, name=name, grid=grid, out_shape=tuple([_sds((nl, r, c), F32)] * 4 + hosted.out_shapes),
        in_specs=[pl.BlockSpec((None, n_slots, tr, c), lambda l, i: (l, 0, i, 0)), blk, blk, blk] + hosted.in_specs,
        out_specs=tuple([blk] * 4 + hosted.out_specs), input_output_aliases=hosted.aliases,
        scratch_shapes=hosted.scratch, compiler_params=_cparams(2))(land, w, m, v, *hosted.operands)
    return list(res[:4]), list(res[4:])


def _adamw_small(g, w, m, v):
    def body(g_ref, w_ref, m_ref, v_ref, d_out, m_out, v_out):
        delta, mn, vn = _adamw_math(w_ref[...], g_ref[...], m_ref[...], v_ref[...])
        d_out[...] = delta
        m_out[...] = mn
        v_out[...] = vn

    return pl.pallas_call(body, name="adamw_small", out_shape=tuple([_sds(g.shape, F32)] * 3),
                          compiler_params=pltpu.CompilerParams(vmem_limit_bytes=VMEM_LIMIT_V7X))(g, w, m, v)


def _rows(a):
    return a.reshape(-1, 128)


BIG = ["w_in", "w_out", "w_gate", "w_up", "w_down"]
BOTH = (0, 1)
AG_HOSTS = {
    ("norm1", 0): [("w_in", 0, BOTH), ("conv_w", 0, (0,))],
    ("mm_in", 0): [("w_out", 0, BOTH), ("w_gate", 0, (0,))], ("mixer", 0): [("w_gate", 0, (1,))],
    ("mm_out", 0): [("w_up", 0, (0,))], ("norm2", 0): [("w_up", 0, (1,))],
    ("mm_swiglu", 0): [("w_down", 0, BOTH), ("w_in", 1, BOTH), ("w_out", 1, BOTH)], ("mm_down", 0): [("w_gate", 1, BOTH)],
    ("mm_in", 1): [("w_up", 1, BOTH)], ("mm_swiglu", 1): [("w_down", 1, BOTH)],
}


def kernel(x, norm1_g, w_in, gmlp_ln_g, gmlp_ln_b, w_spatial, b_spatial, conv_w, group_norm_g, w_out, norm2_g, w_gate, w_up, w_down, final_norm_g, loss_target, m_norm1_g, m_w_in, m_gmlp_ln_g, m_gmlp_ln_b, m_w_spatial, m_b_spatial, m_conv_w, m_group_norm_g, m_w_out, m_norm2_g, m_w_gate, m_w_up, m_w_down, m_final_norm_g, v_norm1_g, v_w_in, v_gmlp_ln_g, v_gmlp_ln_b, v_w_spatial, v_b_spatial, v_conv_w, v_group_norm_g, v_w_out, v_norm2_g, v_w_gate, v_w_up, v_w_down, v_final_norm_g):
    nl = N_LAYERS
    t, d = x.shape[1], x.shape[2]
    a = d // 2
    hd = a // HEADS
    xin = x.reshape(t, d)
    target = loss_target.reshape(t, d)
    me = _index(_place())

    tr = lambda w: jnp.transpose(w, (0, 2, 1))
    big = {"w_in": w_in, "w_out": w_out, "w_gate": tr(w_gate), "w_up": tr(w_up), "w_down": w_down}
    big_m = {"w_in": m_w_in, "w_out": m_w_out, "w_gate": tr(m_w_gate), "w_up": tr(m_w_up), "w_down": m_w_down}
    big_v = {"w_in": v_w_in, "w_out": v_w_out, "w_gate": tr(v_w_gate), "w_up": tr(v_w_up), "w_down": v_w_down}
    block = {k: big[k].shape[1:] for k in BIG}
    view = {k: _cols_view(block[k][1]) if k == "w_in" else _rows_view(block[k][0]) for k in BIG}
    full_shape = {k: (block[k][0], N_DEV * block[k][1]) if k == "w_in" else (N_DEV * block[k][0], block[k][1])
                  for k in BIG}

    halves = {k: _cols_halves(*block[k]) if k == "w_in" else _rows_halves(block[k][0]) for k in BIG}

    weights = {}
    shards = {(k, l): big[k][l].astype(BF16) for k in BIG for l in range(nl)}

    def ag_spec(k, l, which=BOTH):
        if k == "conv_w":
            return (conv_w, _sds((N_DEV, *conv_w.shape), F32), _SLOT_WHOLE, (0,), None)
        return (shards[(k, l)], _sds(full_shape[k], BF16), halves[k], which, weights.get((k, l)))

    bb = jnp.broadcast_to(b_spatial[..., None], (nl, HEADS, BLK, hd))

    def hosted(name, l):
        keys = AG_HOSTS.get((name, l), [])
        return keys, ([_ag_piece([ag_spec(*key) for key in keys])] if keys else None)

    def landed(keys, couts):
        for (k, kl, _), arr in zip(keys, couts):
            weights[(k, kl)] = arr

    saved = []
    xl = xin
    for l in range(nl):
        keys, comm = hosted("norm1", l)
        h, couts = _rmsnorm_fwd(xl, norm1_g[l:l + 1], comm)
        landed(keys, couts)
        if l == 0:
            conv_full = jnp.transpose(weights[("conv_w", 0)], (1, 2, 0, 3)).reshape(nl, 3, a)
        keys, comm = hosted("mm_in", l)
        z, couts = _mm_in(h, weights[("w_in", l)], comm)
        landed(keys, couts)
        keys, comm = hosted("mixer", l)
        y, couts = _mixer_fwd(z, gmlp_ln_g[l:l + 1], gmlp_ln_b[l:l + 1], w_spatial[l], bb[l], conv_full[l],
                              group_norm_g[l:l + 1], comm)
        landed(keys, couts)
        keys, comm = hosted("mm_out", l)
        x1, couts = _mm_out(y, weights[("w_out", l)], xl, comm)
        landed(keys, couts)
        keys, comm = hosted("norm2", l)
        h2, couts = _rmsnorm_fwd(x1, norm2_g[l:l + 1], comm)
        landed(keys, couts)
        keys, comm = hosted("mm_swiglu", l)
        (gate, up, act), couts = _mm_swiglu(h2, weights[("w_gate", l)], weights[("w_up", l)], comm)
        landed(keys, couts)
        keys, comm = hosted("mm_down", l)
        x2, couts = _mm_down(act, weights[("w_down", l)], x1, comm)
        landed(keys, couts)
        saved.append(dict(x=xl, h=h, z=z, y=y, x1=x1, h2=h2, gate=gate, up=up, act=act))
        xl = x2

    dx, dxb, d_final_g, loss_part = _loss_head(xl, final_norm_g.reshape(1, d), target)
    small = [None] * nl
    core = lax.axis_index("c").astype(jnp.int32).reshape(1)
    stage_shape = {k: _sds((N_CHIPS, *block[k]), BF16) for k in BIG}
    land_shape = {k: _sds((nl, N_CHIPS, *block[k]), BF16) for k in BIG}
    grads = [dict() for _ in range(nl)]
    stages = [dict() for _ in range(nl)]
    sums = [dict() for _ in range(nl)]
    lands = {k: None for k in BIG}

    def core_job(l, keys):
        def sink(outs):
            stages[l].update(zip(keys, outs))
        return _rs_core_piece([(grads[l][k], stage_shape[k], view[k]) for k in keys]), sink

    def chip_job(l, items):
        keys = [item[0] for item in items]

        def rows(k, p0, p1, n_parts):
            per = block[k][0] // n_parts
            return (p0 * per, (p1 - p0) * per)

        def sink(outs):
            lands.update(zip(keys, outs))
        return _rs_chip_piece([(sums[l][k], land_shape[k], rows(k, p0, p1, n_parts), lands[k])
                               for k, p0, p1, n_parts in items], l), sink

    def add_up(l, keys):
        for k in keys:
            sums[l][k] = _chip_sums(f"chip_sums_{k}", grads[l][k], stages[l][k], k == "w_in", core)

    def host(*jobs):
        def deliver(couts):
            i = 0
            for piece, sink in jobs:
                n_out = len(piece.out_shapes)
                sink(couts[i:i + n_out])
                i += n_out
        return [piece for piece, _ in jobs], deliver

    whole = lambda k: (k, 0, 1, 1)
    for l in reversed(range(nl)):
        s = saved[l]
        wi, wo, wgt, wut, wd = [weights[(k, l)] for k in BIG]
        later = l + 1 < nl
        comm, deliver = host(chip_job(l + 1, [("w_in", 0, 1, 2)])) if later else host()
        (grads[l]["w_down"],), couts = _mm_dw("mm_dw_down", [s["act"]], dxb, 2816, 1024, comm)
        deliver(couts)
        comm, deliver = (host(core_job(l, ["w_down"]), chip_job(l + 1, [("w_in", 1, 2, 2)])) if later
                         else host(core_job(l, ["w_down"])))
        (dgate, dup), couts = _mm_dact(dxb, wd, s["gate"], s["up"], comm)
        deliver(couts)
        add_up(l, ["w_down"])
        comm, deliver = host(chip_job(l, [whole("w_down")]))
        (grads[l]["w_gate"], grads[l]["w_up"]), couts = _mm_dw("mm_dw_gate_up", [dgate, dup], s["h2"], 1408, 1024, comm)
        deliver(couts)
        comm, deliver = host(core_job(l, ["w_gate", "w_up"]))
        dh2, couts = _mm_dh2(dgate, dup, wgt, wut, comm)
        deliver(couts)
        add_up(l, ["w_gate", "w_up"])
        dx1, dx1b, d_n2 = _rmsnorm_bwd(s["x1"], norm2_g[l:l + 1], dh2, dx)
        comm, deliver = host(chip_job(l, [("w_gate", 0, 1, 4)]))
        dy, couts = _mm_dy(dx1b, wo, comm)
        deliver(couts)
        comm, deliver = host(chip_job(l, [("w_gate", 1, 2, 4)]))
        (grads[l]["w_out"],), couts = _mm_dw("mm_dw_out", [s["y"]], dx1b, 1024, 1024, comm)
        deliver(couts)
        comm, deliver = host(chip_job(l, [("w_gate", 2, 4, 4)]))
        (dz, d_lng, d_lnb, d_ws, d_bb, d_cw, d_gg), couts = _mixer_bwd(
            s["z"], dy, gmlp_ln_g[l:l + 1], gmlp_ln_b[l:l + 1], w_spatial[l], bb[l], conv_full[l], group_norm_g[l:l + 1],
            comm)
        deliver(couts)
        comm, deliver = host(chip_job(l, [("w_up", 0, 3, 4)]), core_job(l, ["w_out"]))
        (grads[l]["w_in"],), couts = _mm_dw("mm_dw_in", [s["h"]], dz, 2048, 1024, comm)
        deliver(couts)
        add_up(l, ["w_out"])
        comm, deliver = host(chip_job(l, [("w_up", 3, 4, 4), whole("w_out")]), core_job(l, ["w_in"]))
        dh, couts = _mm_dh(dz, wi, comm)
        deliver(couts)
        add_up(l, ["w_in"])
        dx, dxb, d_n1 = _rmsnorm_bwd(s["x"], norm1_g[l:l + 1], dh, dx1)
        small[l] = dict(norm1_g=d_n1, gmlp_ln_g=d_lng, gmlp_ln_b=d_lnb, w_spatial=d_ws, b_spatial=d_bb[:, :, 0],
                        group_norm_g=d_gg, norm2_g=d_n2, conv_w=d_cw[0:3])
    grad_x = dx.reshape(x.shape)

    rep = ["norm1_g", "gmlp_ln_g", "gmlp_ln_b", "w_spatial", "b_spatial", "group_norm_g", "norm2_g"]
    rep_w = dict(norm1_g=norm1_g, gmlp_ln_g=gmlp_ln_g, gmlp_ln_b=gmlp_ln_b, w_spatial=w_spatial, b_spatial=b_spatial,
                 group_norm_g=group_norm_g, norm2_g=norm2_g)
    rep_m = dict(norm1_g=m_norm1_g, gmlp_ln_g=m_gmlp_ln_g, gmlp_ln_b=m_gmlp_ln_b, w_spatial=m_w_spatial,
                 b_spatial=m_b_spatial, group_norm_g=m_group_norm_g, norm2_g=m_norm2_g)
    rep_v = dict(norm1_g=v_norm1_g, gmlp_ln_g=v_gmlp_ln_g, gmlp_ln_b=v_gmlp_ln_b, w_spatial=v_w_spatial,
                 b_spatial=v_b_spatial, group_norm_g=v_group_norm_g, norm2_g=v_norm2_g)
    parts = [_rows(jnp.stack([small[l][k].reshape(rep_w[k].shape[1:]) for l in range(nl)])) for k in rep]
    parts.append(_rows(d_final_g))
    parts.append(_rows(jnp.stack([small[l]["conv_w"] for l in range(nl)])))
    parts.append(jnp.broadcast_to(loss_part, (8, 128)))
    sizes = [p.shape[0] for p in parts]
    comm, deliver = host(chip_job(0, [whole("w_in")]))
    total, couts = _all_reduce_small(jnp.concatenate(parts, axis=0), comm)
    deliver(couts)
    offs = [0]
    for n in sizes:
        offs.append(offs[-1] + n)
    pieces = [total[offs[i]:offs[i + 1]] for i in range(len(parts))]
    loss = pieces[-1][0, 0]
    conv_g_full = pieces[-2].reshape(nl, 3, N_DEV, a // N_DEV)
    conv_g = lax.dynamic_index_in_dim(conv_g_full, me, axis=2, keepdims=False)
    n_rep = offs[len(rep) + 1]
    pad = jnp.zeros((2, 128), F32)

    def small_pack(named, final, conv):
        return jnp.concatenate([_rows(named[k]) for k in rep] + [_rows(final), _rows(conv), pad], axis=0)

    g_small = jnp.concatenate([total[:n_rep], _rows(conv_g), pad], axis=0)
    d_small, m_small, v_small = _adamw_small(
        g_small, small_pack(rep_w, final_norm_g, conv_w), small_pack(rep_m, m_final_norm_g, m_conv_w),
        small_pack(rep_v, v_final_norm_g, v_conv_w))

    def unpack(packed):
        out = {k: packed[offs[i]:offs[i + 1]].reshape(rep_w[k].shape) for i, k in enumerate(rep)}
        out["final_norm_g"] = packed[offs[len(rep)]:n_rep].reshape(final_norm_g.shape)
        out["conv_w"] = packed[n_rep:n_rep + 6].reshape(conv_w.shape)
        return out

    res = {"grad": unpack(g_small), "delta": unpack(d_small), "m": unpack(m_small), "v": unpack(v_small)}

    for k in BIG:
        outs, _ = _adamw_big(f"adamw_{k}", lands[k], big[k], big_m[k], big_v[k])
        if k in ("w_gate", "w_up"):
            outs = [tr(o) for o in outs]
        res["grad"][k], res["delta"][k], res["m"][k], res["v"][k] = outs

    order = ["norm1_g", "w_in", "gmlp_ln_g", "gmlp_ln_b", "w_spatial", "b_spatial", "conv_w", "group_norm_g", "w_out",
             "norm2_g", "w_gate", "w_up", "w_down", "final_norm_g"]
    return (loss, grad_x, *[res["grad"][k] for k in order], *[res["delta"][k] for k in order],
            *[res["m"][k] for k in order], *[res["v"][k] for k in order])
```

```python
import functools
import math
import operator

import jax
import jax.numpy as jnp
from jax import lax
from jax.experimental import pallas as pl
from jax.experimental.pallas import tpu as pltpu

F32 = jnp.float32
BF16 = jnp.bfloat16
MESH = pl.DeviceIdType.MESH

N_DEV = 8
N_LAYERS = 2
HEADS = 8
BLK = 128
CHUNK = 64
HALO = 16
RMS_EPS = 1e-6
LN_EPS = 1e-5
ADAM_LR, ADAM_B1, ADAM_B2, ADAM_EPS, ADAM_WD, ADAM_STEP = 0.001, 0.9, 0.999, 1e-8, 0.01, 10
GELU_C = math.sqrt(2.0 / math.pi)
GELU_A = 0.044715

VMEM_LIMIT_V7X = 56 * 1024 * 1024
_TM = 1024
_TN = 1024
_TT = 1024
_TM_MIX = 256
_TM_NORM = 512


def _cparams(n_axes):
    return pltpu.CompilerParams(dimension_semantics=("arbitrary",) * n_axes, vmem_limit_bytes=VMEM_LIMIT_V7X)


def _sds(shape, dtype):
    return jax.ShapeDtypeStruct(tuple(shape), dtype)


def _place():
    return lax.axis_index("x"), lax.axis_index("y"), lax.axis_index("c")


def _index(place):
    return 4 * place[0] + 2 * place[1] + place[2]


class _Piece:
    def __init__(self, operands, out_shapes, aliases, n_sems, start, finish, mid1=None, mid2=None):
        self.operands, self.out_shapes, self.aliases, self.n_sems = list(operands), list(out_shapes), dict(aliases), n_sems
        nothing = lambda ctx: None
        self.start, self.mid1, self.mid2, self.finish = start, mid1 or nothing, mid2 or nothing, finish


class _Ctx:
    def __init__(self, ins, outs, sems, offs):
        self.ins, self.outs, self.sems = ins, outs, sems
        self.o_in, self.o_out, self.o_send, self.o_recv, self.o_loc = offs

    def inp(self, i):
        return self.ins[self.o_in + i]

    def out(self, i):
        return self.outs[self.o_out + i]

    def send(self, k):
        return self.sems[0].at[self.o_send + k]

    def recv(self, k):
        return self.sems[1].at[self.o_recv + k]

    def local(self, k):
        return self.sems[2].at[self.o_loc + k]


class _Hosted:
    def __init__(self, pieces, n_in_before, n_out_before):
        self.pieces = [p for p in (pieces or []) if p is not None]
        self.operands, self.out_shapes, self.aliases, self.offs = [], [], {}, []
        counts = [0, 0, 0]
        for p in self.pieces:
            self.offs.append((len(self.operands), len(self.out_shapes), *counts))
            for i, j in p.aliases.items():
                self.aliases[n_in_before + len(self.operands) + i] = n_out_before + len(self.out_shapes) + j
            self.operands += p.operands
            self.out_shapes += p.out_shapes
            counts = [c + n for c, n in zip(counts, p.n_sems)]
        hbm = pl.BlockSpec(memory_space=pl.ANY)
        self.in_specs = [hbm] * len(self.operands)
        self.out_specs = [hbm] * len(self.out_shapes)
        self.scratch = [pltpu.SemaphoreType.DMA((max(c, 1),)) for c in counts] if self.pieces else []

    def run(self, stage, ins, outs, sems):
        for p, offs in zip(self.pieces, self.offs):
            getattr(p, stage)(_Ctx(ins, outs, sems, offs))

    def wrap(self, grid, compute, ins, outs, sems):
        if not self.pieces:
            compute()
            return
        n_steps = math.prod(grid)
        lin = 0
        for ax, g in enumerate(grid):
            lin = lin * g + pl.program_id(ax)
        pl.when(lin == 0)(lambda: self.run("start", ins, outs, sems))
        compute()
        pl.when(lin == n_steps // 2)(lambda: self.run("mid1", ins, outs, sems))
        pl.when(lin == max(n_steps - 3, n_steps // 2))(lambda: self.run("mid2", ins, outs, sems))
        pl.when(lin == n_steps - 1)(lambda: self.run("finish", ins, outs, sems))


def _cols_view(width):
    return lambda ref, p: ref.at[:, pl.ds(pl.multiple_of(p * width, 128), width)]


def _rows_view(height):
    return lambda ref, p: ref.at[pl.ds(pl.multiple_of(p * height, 16), height), :]


def _cols_halves(rows, width, part, n_parts):
    hr = rows // n_parts // 2
    at = lambda h: pl.ds(part * 2 * hr + h * hr, hr)
    return (lambda ref, p, h: ref.at[at(h), pl.ds(pl.multiple_of(p * width, 128), width)],
            lambda ref, h: ref.at[at(h), :], 2)


def _rows_halves(height, part, n_parts):
    hh = height // n_parts // 2
    return (lambda ref, p, h: ref.at[pl.ds(pl.multiple_of(p * height + part * 2 * hh + h * hh, 16), hh), :],
            lambda ref, h: ref.at[pl.ds(part * 2 * hh + h * hh, hh), :], 2)


_SLOT_WHOLE = (lambda ref, p, h: ref.at[p], lambda ref, h: ref, 1)


def _ag_piece(specs):
    units = [(a, h) for a, s in enumerate(specs) for h in s[3]]

    def plan(ctx):
        x, y, c = _place()
        me, sib, xn, yn, dg = (x, y, c), (x, y, 1 - c), (1 - x, y, c), (x, 1 - y, c), (1 - x, 1 - y, c)

        def copy(u, k, block, to, from_shard=False):
            a, h = units[u]
            dst_of, src_of, _ = specs[a][2]
            dst = dst_of(ctx.out(a), _index(block), h)
            return pltpu.make_async_remote_copy(
                src_ref=src_of(ctx.inp(a), h) if from_shard else dst, dst_ref=dst, send_sem=ctx.send(7 * u + k),
                recv_sem=ctx.recv(7 * u + k), device_id=to, device_id_type=MESH)

        def local(u):
            a, h = units[u]
            dst_of, src_of, _ = specs[a][2]
            return pltpu.make_async_copy(src_of(ctx.inp(a), h), dst_of(ctx.out(a), _index(me), h), ctx.local(u))

        def relay(u):
            return copy(u, 3, xn, yn) if units[u][1] % 2 == 0 else copy(u, 3, yn, xn)

        return me, sib, xn, yn, dg, c, copy, local, relay

    def start(ctx):
        me, sib, xn, yn, dg, c, copy, local, relay = plan(ctx)
        for u in range(len(units)):
            local(u).start()
            for k, to in enumerate((sib, xn, yn)):
                copy(u, k, me, to, from_shard=True).start()

    def mid1(ctx):
        me, sib, xn, yn, dg, c, copy, local, relay = plan(ctx)
        for u in range(len(units)):
            copy(u, 1, xn, me).wait_recv()
            copy(u, 2, yn, me).wait_recv()
            relay(u).start()
            copy(u, 4, xn, sib).start()
            copy(u, 5, yn, sib).start()

    def mid2(ctx):
        me, sib, xn, yn, dg, c, copy, local, relay = plan(ctx)
        for u in range(len(units)):
            copy(u, 3, dg, me).wait_recv()
            copy(u, 6, dg, sib).start()

    def finish(ctx):
        me, sib, xn, yn, dg, c, copy, local, relay = plan(ctx)
        other = lambda place: (place[0], place[1], 1 - c)
        for u in range(len(units)):
            for k, block in ((0, sib), (4, other(xn)), (5, other(yn)), (6, other(dg))):
                copy(u, k, block, me).wait_recv()
        for u in range(len(units)):
            for k, to in enumerate((sib, xn, yn)):
                copy(u, k, me, to, from_shard=True).wait_send()
            relay(u).wait_send()
            for k, block in ((4, xn), (5, yn), (6, dg)):
                copy(u, k, block, sib).wait_send()
            local(u).wait()

    n_u = len(units)
    operands, aliases = [s[0] for s in specs], {}
    for a, spec in enumerate(specs):
        if spec[4] is not None:
            aliases[len(operands)] = a
            operands.append(spec[4])
    return _Piece(operands, [s[1] for s in specs], aliases, (7 * n_u, 7 * n_u, n_u), start, finish, mid1, mid2)


N_CHIPS = 4


def _rs_core_piece(specs):
    n = len(specs)

    def copies(ctx):
        x, y, c = _place()
        out = []
        for a in range(n):
            for q in range(N_CHIPS):
                out.append(pltpu.make_async_remote_copy(
                    src_ref=specs[a][2](ctx.inp(a), 2 * q + (1 - c)), dst_ref=ctx.out(a).at[q],
                    send_sem=ctx.send(N_CHIPS * a + q), recv_sem=ctx.recv(N_CHIPS * a + q), device_id=(x, y, 1 - c),
                    device_id_type=MESH))
        return out

    def start(ctx):
        for cp in copies(ctx):
            cp.start()

    def finish(ctx):
        for cp in copies(ctx):
            cp.wait_recv()
            cp.wait_send()

    return _Piece([s[0] for s in specs], [s[1] for s in specs], {}, (N_CHIPS * n, N_CHIPS * n, 0), start, finish)


def _rs_chip_piece(specs, layer):
    n = len(specs)
    hops = [(1, 0), (0, 1), (1, 1)]

    def copies(ctx):
        x, y, c = _place()
        mine = 2 * x + y
        out = []
        for a in range(n):
            rows = pl.ds(*specs[a][2])
            sums, land = ctx.inp(a), ctx.out(a)
            out.append((pltpu.make_async_copy(sums.at[mine, rows], land.at[layer, mine, rows], ctx.local(a)), None))
            for j, (dx, dy) in enumerate(hops):
                px, py = x ^ dx, y ^ dy
                peer = 2 * px + py
                send = pltpu.make_async_remote_copy(
                    src_ref=sums.at[peer, rows], dst_ref=land.at[layer, mine, rows], send_sem=ctx.send(3 * a + j),
                    recv_sem=ctx.recv(3 * a + j), device_id=(px, py, c), device_id_type=MESH)
                recv = pltpu.make_async_remote_copy(
                    src_ref=sums.at[peer, rows], dst_ref=land.at[layer, peer, rows], send_sem=ctx.send(3 * a + j),
                    recv_sem=ctx.recv(3 * a + j), device_id=(px, py, c), device_id_type=MESH)
                out.append((send, recv))
        return out

    def start(ctx):
        for send, _ in copies(ctx):
            send.start()

    def finish(ctx):
        for send, recv in copies(ctx):
            if recv is None:
                send.wait()
            else:
                recv.wait_recv()
                send.wait_send()

    operands, aliases = [s[0] for s in specs], {}
    for a, spec in enumerate(specs):
        if spec[3] is not None:
            aliases[len(operands)] = a
            operands.append(spec[3])
    return _Piece(operands, [s[1] for s in specs], aliases, (3 * n, 3 * n, n), start, finish)


def _chip_sums(name, grad, stage, by_cols, core):
    _, r, c = stage.shape
    tr = r
    while tr * c > 1024 * 1024 or r % tr or tr % 16:
        tr -= 16
    n_t = r // tr

    def body(core_ref, g_ref, s_ref, o_ref):
        o_ref[...] = (g_ref[...].astype(F32) + s_ref[...].astype(F32)).astype(BF16)

    if by_cols:
        gspec = pl.BlockSpec((tr, c), lambda q, i, core_ref: (i, 2 * q + core_ref[0]))
    else:
        gspec = pl.BlockSpec((tr, c), lambda q, i, core_ref: ((2 * q + core_ref[0]) * n_t + i, 0))
    sspec = pl.BlockSpec((None, tr, c), lambda q, i, core_ref: (q, i, 0))
    return pl.pallas_call(
        body, name=name, out_shape=_sds(stage.shape, BF16),
        grid_spec=pltpu.PrefetchScalarGridSpec(num_scalar_prefetch=1, grid=(N_CHIPS, n_t), in_specs=[gspec, sspec],
                                               out_specs=sspec),
        compiler_params=_cparams(2))(core, grad, stage)


def _call_hosting(body, name, grid, out_shapes, in_specs, out_specs, operands, scratch, comm):
    n_in, n_out, n_scr = len(operands), len(out_shapes), len(scratch)
    hosted = _Hosted(comm, n_in, n_out)
    n_ci, n_co = len(hosted.operands), len(hosted.out_shapes)

    def hosting_body(*refs):
        ins, rest = refs[:n_in], refs[n_in:]
        c_ins, rest = rest[:n_ci], rest[n_ci:]
        outs, rest = rest[:n_out], rest[n_out:]
        c_outs, rest = rest[:n_co], rest[n_co:]
        hosted.wrap(grid, lambda: body(*ins, *outs, *rest[:n_scr]), c_ins, c_outs, rest[n_scr:])

    res = pl.pallas_call(
        hosting_body, name=name, grid=grid, out_shape=tuple(list(out_shapes) + hosted.out_shapes),
        in_specs=list(in_specs) + hosted.in_specs, out_specs=tuple(list(out_specs) + hosted.out_specs),
        input_output_aliases=hosted.aliases, scratch_shapes=list(scratch) + hosted.scratch,
        compiler_params=_cparams(len(grid)))(*operands, *hosted.operands)
    return list(res[:n_out]), list(res[n_out:])


def _matmul(name, grid, nk, kaxis, pairs, dims, extras, outs, epilogue, sum_pairs, acc_shape, comm=None, split=None):
    n_p, n_e, n_o = len(pairs), len(extras), len(outs)
    n_acc = 0 if nk == 1 else (1 if sum_pairs else n_p)
    n_in = 2 * n_p + n_e
    hosted = _Hosted(comm, n_in, n_o)
    n_ci, n_co = len(hosted.operands), len(hosted.out_shapes)

    def body(*refs):
        a_refs = refs[0:2 * n_p:2]
        b_refs = refs[1:2 * n_p:2]
        e_refs = refs[2 * n_p:n_in]
        c_ins = refs[n_in:n_in + n_ci]
        o_refs = refs[n_in + n_ci:n_in + n_ci + n_o]
        c_outs = refs[n_in + n_ci + n_o:n_in + n_ci + n_o + n_co]
        acc_refs = refs[n_in + n_ci + n_o + n_co:n_in + n_ci + n_o + n_co + n_acc]
        sems = refs[n_in + n_ci + n_o + n_co + n_acc:]

        def dots():
            prods = [lax.dot_general(a[...], b[...], (dims, ((), ())), preferred_element_type=F32)
                     for a, b in zip(a_refs, b_refs)]
            if sum_pairs and n_p > 1:
                prods = [functools.reduce(operator.add, prods)]
            return prods

        def compute():
            if nk == 1 and split is not None:
                n_split, b_axis, n_row = split
                width = b_refs[0].shape[b_axis] // n_split
                height = a_refs[0].shape[0] // n_row
                for s in range(n_split):
                    cols = pl.ds(s * width, width)
                    for r in range(n_row):
                        rows = pl.ds(r * height, height)
                        epilogue([lax.dot_general(a[rows, :], b[cols, :] if b_axis == 0 else b[:, cols], (dims, ((), ())),
                                                  preferred_element_type=F32) for a, b in zip(a_refs, b_refs)],
                                 e_refs, o_refs, rows, cols)
                return
            if nk == 1:
                epilogue(dots(), e_refs, o_refs)
                return
            k = pl.program_id(kaxis)

            @pl.when(k == 0)
            def _():
                for acc, p in zip(acc_refs, dots()):
                    acc[...] = p

            if nk > 2:
                @pl.when((k > 0) & (k < nk - 1))
                def _():
                    for acc, p in zip(acc_refs, dots()):
                        acc[...] += p

            @pl.when(k == nk - 1)
            def _():
                epilogue([acc[...] + p for acc, p in zip(acc_refs, dots())], e_refs, o_refs)

        hosted.wrap(grid, compute, c_ins, c_outs, sems)

    operands, in_specs = [], []
    for a, a_spec, b, b_spec in pairs:
        operands += [a, b]
        in_specs += [a_spec, b_spec]
    for e, e_spec in extras:
        operands.append(e)
        in_specs.append(e_spec)
    res = pl.pallas_call(
        body, name=name, grid=grid,
        out_shape=tuple([o for o, _ in outs] + hosted.out_shapes),
        in_specs=in_specs + hosted.in_specs, out_specs=tuple([s for _, s in outs] + hosted.out_specs),
        input_output_aliases=hosted.aliases,
        scratch_shapes=[pltpu.VMEM(acc_shape, F32) for _ in range(n_acc)] + hosted.scratch,
        compiler_params=_cparams(len(grid)),
    )(*operands, *hosted.operands)
    return list(res[:n_o]), list(res[n_o:])


NN = ((1,), (0,))
NT = ((1,), (1,))
TN = ((0,), (0,))


def _tile(n, want):
    if n <= want:
        return n
    t = want // 128 * 128
    while n % t:
        t -= 128
    return t


def _silu_parts(g):
    s = 0.5 + 0.5 * jnp.tanh(0.5 * g)
    return s, g * s


def _mm_in(h, w_in, comm=None):
    t, d = h.shape
    n = w_in.shape[1]
    tm, tn = _tile(t, _TM), _tile(n, _TN)

    def epi(accs, e, o):
        o[0][...] = accs[0].astype(BF16)

    outs, couts = _matmul(
        "mm_in", (n // tn, t // tm), 1, None,
        [(h, pl.BlockSpec((tm, d), lambda j, i: (i, 0)), w_in, pl.BlockSpec((d, tn), lambda j, i: (0, j)))],
        NN, [], [(_sds((t, n), BF16), pl.BlockSpec((tm, tn), lambda j, i: (i, j)))], epi, True, None, comm)
    return outs[0], couts


def _mm_out(y, w_out, x, comm=None):
    t, m = y.shape
    d = w_out.shape[1]
    tm, tn = _tile(t, _TM), _tile(d, _TN)

    def epi(accs, e, o):
        o[0][...] = e[0][...] + accs[0]

    outs, couts = _matmul(
        "mm_out", (t // tm, d // tn), 1, None,
        [(y, pl.BlockSpec((tm, m), lambda i, j: (i, 0)), w_out, pl.BlockSpec((m, tn), lambda i, j: (0, j)))],
        NN, [(x, pl.BlockSpec((tm, tn), lambda i, j: (i, j)))],
        [(_sds((t, d), F32), pl.BlockSpec((tm, tn), lambda i, j: (i, j)))], epi, True, None, comm)
    return outs[0], couts


def _mm_swiglu(h2, wgt, wut, comm=None):
    t, d = h2.shape
    f = wgt.shape[0]
    tm, tn = _tile(t, _TM), _tile(f, 512)

    def epi(accs, e, o, rows, cols):
        g, u = accs
        _, sg = _silu_parts(g)
        o[0][rows, cols] = g.astype(BF16)
        o[1][rows, cols] = u.astype(BF16)
        o[2][rows, cols] = (sg * u).astype(BF16)

    wspec = pl.BlockSpec((tn, d), lambda j, i: (j, 0))
    hspec = pl.BlockSpec((tm, d), lambda j, i: (i, 0))
    ospec = pl.BlockSpec((tm, tn), lambda j, i: (i, j))
    osh = _sds((t, f), BF16)
    outs, couts = _matmul("mm_swiglu", (f // tn, t // tm), 1, None, [(h2, hspec, wgt, wspec), (h2, hspec, wut, wspec)],
                          NT, [], [(osh, ospec)] * 3, epi, False, None, comm, split=(tn // 256, 0, 2))
    return outs, couts


def _mm_down(act, wd, x1, comm=None):
    t, f = act.shape
    d = wd.shape[1]
    tm, tn = _tile(t, _TM), _tile(d, _TN)
    nk = 2
    tk = f // nk

    def epi(accs, e, o):
        o[0][...] = e[0][...] + accs[0]

    outs, couts = _matmul(
        "mm_down", (t // tm, d // tn, nk), nk, 2,
        [(act, pl.BlockSpec((tm, tk), lambda i, j, k: (i, k)), wd, pl.BlockSpec((tk, tn), lambda i, j, k: (k, j)))],
        NN, [(x1, pl.BlockSpec((tm, tn), lambda i, j, k: (i, j)))],
        [(_sds((t, d), F32), pl.BlockSpec((tm, tn), lambda i, j, k: (i, j)))], epi, True, (tm, tn), comm)
    return outs[0], couts


def _mm_dact(dxb, wd, gate, up, comm=None):
    t, d = dxb.shape
    f = wd.shape[0]
    tm, tn = _tile(t, _TM), _tile(f, 512)

    def epi(accs, e, o, rows, cols):
        da = accs[0]
        g = e[0][rows, cols].astype(F32)
        u = e[1][rows, cols].astype(F32)
        s, sg = _silu_parts(g)
        o[0][rows, cols] = (da * u * (s + sg * (1.0 - s))).astype(BF16)
        o[1][rows, cols] = (da * sg).astype(BF16)

    bspec = pl.BlockSpec((tm, tn), lambda j, i: (i, j))
    osh = _sds((t, f), BF16)
    outs, couts = _matmul(
        "mm_dact", (f // tn, t // tm), 1, None,
        [(dxb, pl.BlockSpec((tm, d), lambda j, i: (i, 0)), wd, pl.BlockSpec((tn, d), lambda j, i: (j, 0)))],
        NT, [(gate, bspec), (up, bspec)], [(osh, bspec)] * 2, epi, True, None, comm, split=(tn // 256, 0, 2))
    return outs, couts


def _mm_dh2(dgate, dup, wgt, wut, comm=None):
    t, f = dgate.shape
    d = wgt.shape[1]
    tm, tn = _tile(t, _TM), _tile(d, _TN)
    nk = 4
    tk = f // nk

    def epi(accs, e, o):
        o[0][...] = accs[0]

    aspec = pl.BlockSpec((tm, tk), lambda i, j, k: (i, k))
    wspec = pl.BlockSpec((tk, tn), lambda i, j, k: (k, j))
    outs, couts = _matmul("mm_dh2", (t // tm, d // tn, nk), nk, 2, [(dgate, aspec, wgt, wspec), (dup, aspec, wut, wspec)],
                          NN, [], [(_sds((t, d), F32), pl.BlockSpec((tm, tn), lambda i, j, k: (i, j)))], epi, True,
                          (tm, tn), comm)
    return outs[0], couts


def _mm_dw(name, a_list, b, tmo, tno, comm=None):
    t, m = a_list[0].shape
    n = b.shape[1]
    tt = _tile(t, _TT)
    nk = t // tt
    tmo, tno = _tile(m, tmo), _tile(n, tno)

    def epi(accs, e, o):
        for acc, out in zip(accs, o):
            out[...] = acc.astype(BF16)

    aspec = pl.BlockSpec((tt, tmo), lambda i, j, k: (k, i))
    bspec = pl.BlockSpec((tt, tno), lambda i, j, k: (k, j))
    ospec = pl.BlockSpec((tmo, tno), lambda i, j, k: (i, j))
    if nk == 1:
        return _matmul(name, (m // tmo, n // tno, 1), 1, None, [(a, aspec, b, bspec) for a in a_list], TN, [],
                       [(_sds((m, n), BF16), ospec)] * len(a_list), epi, False, None, comm)
    return _matmul(name, (m // tmo, n // tno, nk), nk, 2, [(a, aspec, b, bspec) for a in a_list], TN, [],
                   [(_sds((m, n), BF16), ospec)] * len(a_list), epi, False, (tmo, tno), comm)


def _mm_dy(dxb, w_out, comm=None):
    t, d = dxb.shape
    m = w_out.shape[0]
    tm, tn = _tile(t, _TM), _tile(m, _TN)

    def epi(accs, e, o):
        o[0][...] = accs[0].astype(BF16)

    outs, couts = _matmul(
        "mm_dy", (t // tm, m // tn), 1, None,
        [(dxb, pl.BlockSpec((tm, d), lambda i, j: (i, 0)), w_out, pl.BlockSpec((tn, d), lambda i, j: (j, 0)))], NT, [],
        [(_sds((t, m), BF16), pl.BlockSpec((tm, tn), lambda i, j: (i, j)))], epi, True, None, comm)
    return outs[0], couts


def _mm_dh(dz, w_in, comm=None):
    t, n = dz.shape
    d = w_in.shape[0]
    tm, tn = _tile(t, _TM), _tile(d, _TN)
    nk = 2
    tk = n // nk

    def epi(accs, e, o):
        o[0][...] = accs[0]

    outs, couts = _matmul(
        "mm_dh", (t // tm, d // tn, nk), nk, 2,
        [(dz, pl.BlockSpec((tm, tk), lambda i, j, k: (i, k)), w_in, pl.BlockSpec((tn, tk), lambda i, j, k: (j, k)))], NT,
        [], [(_sds((t, d), F32), pl.BlockSpec((tm, tn), lambda i, j, k: (i, j)))], epi, True, (tm, tn), comm)
    return outs[0], couts


def _rmsnorm_fwd(x, g, comm=None):
    t, d = x.shape
    tm = min(_TM_NORM, t)

    def body(x_ref, g_ref, o_ref):
        xv = x_ref[...]
        rs = lax.rsqrt(jnp.mean(xv * xv, axis=-1, keepdims=True) + RMS_EPS)
        o_ref[...] = (xv * rs * g_ref[...]).astype(BF16)

    outs, couts = _call_hosting(
        body, "rmsnorm_fwd", (t // tm,), [_sds((t, d), BF16)],
        [pl.BlockSpec((tm, d), lambda i: (i, 0)), pl.BlockSpec((1, d), lambda i: (0, 0))],
        [pl.BlockSpec((tm, d), lambda i: (i, 0))], [x, g], [], comm)
    return outs[0], couts


def _rmsnorm_bwd_math(xv, g, dh):
    rs = lax.rsqrt(jnp.mean(xv * xv, axis=-1, keepdims=True) + RMS_EPS)
    xh = xv * rs
    gd = dh * g
    dx = rs * (gd - xh * jnp.mean(gd * xh, axis=-1, keepdims=True))
    return dx, jnp.sum(dh * xh, axis=0, keepdims=True)


def _rmsnorm_bwd(x, g, dh, dres):
    t, d = x.shape
    tm = min(_TM_NORM, t)

    def body(x_ref, g_ref, dh_ref, dres_ref, dx_ref, dxb_ref, dg_ref):
        dx, dg = _rmsnorm_bwd_math(x_ref[...], g_ref[...], dh_ref[...])
        dx = dx + dres_ref[...]
        dx_ref[...] = dx
        dxb_ref[...] = dx.astype(BF16)

        @pl.when(pl.program_id(0) == 0)
        def _():
            dg_ref[...] = dg

        @pl.when(pl.program_id(0) > 0)
        def _():
            dg_ref[...] += dg

    row = pl.BlockSpec((tm, d), lambda i: (i, 0))
    vec = pl.BlockSpec((1, d), lambda i: (0, 0))
    return pl.pallas_call(
        body, name="rmsnorm_bwd", grid=(t // tm,),
        out_shape=(_sds((t, d), F32), _sds((t, d), BF16), _sds((1, d), F32)),
        in_specs=[row, vec, row, row], out_specs=(row, row, vec), compiler_params=_cparams(1))(x, g, dh, dres)


def _loss_head(x, g, target):
    t, d = x.shape
    tm = min(_TM_NORM, t)

    def body(x_ref, g_ref, t_ref, dx_ref, dxb_ref, dg_ref, loss_ref):
        xv, gv = x_ref[...], g_ref[...]
        rs = lax.rsqrt(jnp.mean(xv * xv, axis=-1, keepdims=True) + RMS_EPS)
        diff = xv * rs * gv - t_ref[...]
        part = 0.5 * jnp.sum(jnp.mean(diff * diff, axis=-1, keepdims=True), axis=0, keepdims=True)
        part = jnp.broadcast_to(part, (1, 128))
        dx, dg = _rmsnorm_bwd_math(xv, gv, diff * (1.0 / d))
        dx_ref[...] = dx
        dxb_ref[...] = dx.astype(BF16)

        @pl.when(pl.program_id(0) == 0)
        def _():
            dg_ref[...] = dg
            loss_ref[...] = part

        @pl.when(pl.program_id(0) > 0)
        def _():
            dg_ref[...] += dg
            loss_ref[...] += part

    row = pl.BlockSpec((tm, d), lambda i: (i, 0))
    vec = pl.BlockSpec((1, d), lambda i: (0, 0))
    return pl.pallas_call(
        body, name="loss_head", grid=(t // tm,),
        out_shape=(_sds((t, d), F32), _sds((t, d), BF16), _sds((1, d), F32), _sds((1, 128), F32)),
        in_specs=[row, vec, row], out_specs=(row, row, vec, pl.BlockSpec((1, 128), lambda i: (0, 0))),
        compiler_params=_cparams(1))(x, g, target)


def _gelu(x):
    th = jnp.tanh(GELU_C * (x + GELU_A * x * x * x))
    return 0.5 * x * (1.0 + th), th


def _gelu_grad(x, th):
    return 0.5 * (1.0 + th) + 0.5 * x * (1.0 - th * th) * GELU_C * (1.0 + 3.0 * GELU_A * x * x)


def _masked_ws(ws_ref, h):
    i = lax.broadcasted_iota(jnp.int32, (BLK, BLK), 0) // CHUNK
    j = lax.broadcasted_iota(jnp.int32, (BLK, BLK), 1) // CHUNK
    return jnp.where(j <= i, ws_ref[h], 0.0)


def _shift_down(q, n, first_rows):
    rolled = pltpu.roll(q, n, 0)
    row = lax.broadcasted_iota(jnp.int32, q.shape, 0)
    for r, val in enumerate(first_rows):
        rolled = jnp.where(row == r, val, rolled)
    return rolled


def _shift_up(q, n, last_rows):
    tm = q.shape[0]
    rolled = pltpu.roll(q, tm - n, 0)
    row = lax.broadcasted_iota(jnp.int32, q.shape, 0)
    for r, val in enumerate(last_rows):
        rolled = jnp.where(row == tm - n + r, val, rolled)
    return rolled


def _mixer_specs(t, a, tm):
    hb = tm // HALO
    last = t // HALO - 1
    tile = pl.BlockSpec((tm, 5 * a), lambda i: (i, 0))
    prev = [pl.BlockSpec((HALO, a), functools.partial(lambda i, col: (jnp.maximum(i * hb - 1, 0), col), col=col))
            for col in (3, 4)]
    nxt = [pl.BlockSpec((HALO, a), functools.partial(lambda i, col: (jnp.minimum((i + 1) * hb, last), col), col=col))
           for col in (2, 3, 4)]
    return tile, prev, nxt


def _group_a_fwd(zu, zv, lng, lnb, ws_ref, bb_ref, mixed_ref, vln_ref):
    u, thu = _gelu(zu)
    v, thv = _gelu(zv)
    mu = jnp.mean(v, axis=-1, keepdims=True)
    vc = v - mu
    rs = lax.rsqrt(jnp.mean(vc * vc, axis=-1, keepdims=True) + LN_EPS)
    vhat = vc * rs
    vln_ref[...] = vhat * lng + lnb
    tm, a = zu.shape
    hd = a // HEADS
    for h in range(HEADS):
        w = _masked_ws(ws_ref, h).astype(BF16)
        for b in range(tm // BLK):
            rows, cols = pl.ds(b * BLK, BLK), pl.ds(h * hd, hd)
            mixed_ref[rows, cols] = jnp.dot(w, vln_ref[rows, cols].astype(BF16), preferred_element_type=F32) + bb_ref[h]
    return u, thu, thv, rs, vhat


def _mixer_fwd(z, ln_g, ln_b, w_spatial, bb, conv_w, gg, comm=None):
    t = z.shape[0]
    a = z.shape[1] // 5
    tm = min(_TM_MIX, t)
    tile, prev, _ = _mixer_specs(t, a, tm)

    def body(z_ref, pc_ref, ph_ref, lng_ref, lnb_ref, ws_ref, bb_ref, cw_ref, gg_ref, y_ref, mixed_ref, vln_ref):
        i = pl.program_id(0)
        zu = z_ref[:, 0:a].astype(F32)
        zv = z_ref[:, a:2 * a].astype(F32)
        u, _, _, _, _ = _group_a_fwd(zu, zv, lng_ref[...], lnb_ref[...], ws_ref, bb_ref, mixed_ref, vln_ref)
        ya = u * mixed_ref[...]
        ra = lax.rsqrt(jnp.mean(ya * ya, axis=-1, keepdims=True) + RMS_EPS)
        y_ref[:, 0:a] = (ya * ra * gg_ref[:, 0:a]).astype(BF16)

        zb = z_ref[:, 2 * a:3 * a].astype(F32)
        q = z_ref[:, 3 * a:4 * a].astype(F32) * z_ref[:, 4 * a:5 * a].astype(F32)
        qp = jnp.where(i > 0, pc_ref[...].astype(F32) * ph_ref[...].astype(F32), 0.0)
        qm1 = _shift_down(q, 1, [qp[HALO - 1:HALO]])
        qm2 = _shift_down(q, 2, [qp[HALO - 2:HALO - 1], qp[HALO - 1:HALO]])
        cv = cw_ref[0:1, :] * qm2 + cw_ref[1:2, :] * qm1 + cw_ref[2:3, :] * q
        yb = zb * cv
        rb = lax.rsqrt(jnp.mean(yb * yb, axis=-1, keepdims=True) + RMS_EPS)
        y_ref[:, a:2 * a] = (yb * rb * gg_ref[:, a:2 * a]).astype(BF16)

    full = lambda shape: pl.BlockSpec(shape, lambda i: (0,) * len(shape))
    outs, couts = _call_hosting(
        body, "mixer_fwd", (t // tm,), [_sds((t, 2 * a), BF16)],
        [tile, *prev, full((1, a)), full((1, a)), full(w_spatial.shape), full(bb.shape), full(conv_w.shape),
         full((1, 2 * a))],
        [pl.BlockSpec((tm, 2 * a), lambda i: (i, 0))], [z, z, z, ln_g, ln_b, w_spatial, bb, conv_w, gg],
        [pltpu.VMEM((tm, a), F32), pltpu.VMEM((tm, a), F32)], comm)
    return outs[0], couts


def _mixer_bwd(z, dy, ln_g, ln_b, w_spatial, bb, conv_w, gg, comm=None):
    t = z.shape[0]
    a = z.shape[1] // 5
    hd = a // HEADS
    tm = min(_TM_MIX, t)
    n_tiles = t // tm
    tile, prev, nxt = _mixer_specs(t, a, tm)
    hb = tm // HALO
    dy_tile = pl.BlockSpec((tm, 2 * a), lambda i: (i, 0))
    dy_next = pl.BlockSpec((HALO, a), lambda i: (jnp.minimum((i + 1) * hb, t // HALO - 1), 1))

    def body(z_ref, pc_ref, ph_ref, nb_ref, nc_ref, nh_ref, dy_ref, ndy_ref, lng_ref, lnb_ref, ws_ref, bb_ref, cw_ref,
             gg_ref, dz_ref, dlng_ref, dlnb_ref, dws_ref, dbb_ref, dcw_ref, dgg_ref, mixed_ref, vln_ref, dmix_ref,
             dvln_ref):
        i = pl.program_id(0)

        @pl.when(i == 0)
        def _():
            for ref in (dlng_ref, dlnb_ref, dws_ref, dbb_ref, dcw_ref, dgg_ref):
                ref[...] = jnp.zeros(ref.shape, F32)

        lng = lng_ref[...]
        zu = z_ref[:, 0:a].astype(F32)
        zv = z_ref[:, a:2 * a].astype(F32)
        u, thu, thv, rs, vhat = _group_a_fwd(zu, zv, lng, lnb_ref[...], ws_ref, bb_ref, mixed_ref, vln_ref)
        mixed = mixed_ref[...]
        ya = u * mixed
        ra = lax.rsqrt(jnp.mean(ya * ya, axis=-1, keepdims=True) + RMS_EPS)
        da = dy_ref[:, 0:a].astype(F32)
        yah = ya * ra
        dgg_ref[:, 0:a] += jnp.sum(da * yah, axis=0, keepdims=True)
        ga = da * gg_ref[:, 0:a]
        dya = ra * (ga - yah * jnp.mean(ga * yah, axis=-1, keepdims=True))
        dz_ref[:, 0:a] = (dya * mixed * _gelu_grad(zu, thu)).astype(BF16)
        dmix_ref[...] = dya * u
        for h in range(HEADS):
            w = _masked_ws(ws_ref, h).astype(BF16)
            dw = jnp.zeros((BLK, BLK), F32)
            db = jnp.zeros((BLK, hd), F32)
            for b in range(tm // BLK):
                rows, cols = pl.ds(b * BLK, BLK), pl.ds(h * hd, hd)
                dm = dmix_ref[rows, cols]
                dmb = dm.astype(BF16)
                db = db + dm
                dw = dw + lax.dot_general(dmb, vln_ref[rows, cols].astype(BF16), (NT, ((), ())),
                                          preferred_element_type=F32)
                dvln_ref[rows, cols] = lax.dot_general(w, dmb, (TN, ((), ())), preferred_element_type=F32)
            dws_ref[h] += dw
            dbb_ref[h] += db
        dvln = dvln_ref[...]
        dlng_ref[...] += jnp.sum(dvln * vhat, axis=0, keepdims=True)
        dlnb_ref[...] += jnp.sum(dvln, axis=0, keepdims=True)
        dvh = dvln * lng
        dv = rs * (dvh - jnp.mean(dvh, axis=-1, keepdims=True) - vhat * jnp.mean(dvh * vhat, axis=-1, keepdims=True))
        dz_ref[:, a:2 * a] = (dv * _gelu_grad(zv, thv)).astype(BF16)

        w0, w1, w2 = cw_ref[0:1, :], cw_ref[1:2, :], cw_ref[2:3, :]
        ggb = gg_ref[:, a:2 * a]
        zb = z_ref[:, 2 * a:3 * a].astype(F32)
        zc = z_ref[:, 3 * a:4 * a].astype(F32)
        zh = z_ref[:, 4 * a:5 * a].astype(F32)
        q = zc * zh
        qp = jnp.where(i > 0, pc_ref[...].astype(F32) * ph_ref[...].astype(F32), 0.0)
        qm1 = _shift_down(q, 1, [qp[HALO - 1:HALO]])
        qm2 = _shift_down(q, 2, [qp[HALO - 2:HALO - 1], qp[HALO - 1:HALO]])
        cv = w0 * qm2 + w1 * qm1 + w2 * q

        def conv_out_grad(zb_, cv_, dout_):
            yb = zb_ * cv_
            rb = lax.rsqrt(jnp.mean(yb * yb, axis=-1, keepdims=True) + RMS_EPS)
            ybh = yb * rb
            gb = dout_ * ggb
            dyb = rb * (gb - ybh * jnp.mean(gb * ybh, axis=-1, keepdims=True))
            return dyb * zb_, dyb * cv_, ybh

        db_out = dy_ref[:, a:2 * a].astype(F32)
        g, dzb, ybh = conv_out_grad(zb, cv, db_out)
        dgg_ref[:, a:2 * a] += jnp.sum(db_out * ybh, axis=0, keepdims=True)
        dz_ref[:, 2 * a:3 * a] = dzb.astype(BF16)
        qn = nc_ref[...].astype(F32) * nh_ref[...].astype(F32)
        zbn = nb_ref[...].astype(F32)
        cvn = w0 * _shift_down(qn, 2, [q[tm - 2:tm - 1], q[tm - 1:tm]]) + w1 * _shift_down(qn, 1, [q[tm - 1:tm]]) + w2 * qn
        gn, _, _ = conv_out_grad(zbn, cvn, ndy_ref[...].astype(F32))
        gn = jnp.where(i < n_tiles - 1, gn, 0.0)
        dq = w2 * g + w1 * _shift_up(g, 1, [gn[0:1]]) + w0 * _shift_up(g, 2, [gn[0:1], gn[1:2]])
        dz_ref[:, 3 * a:4 * a] = (dq * zh).astype(BF16)
        dz_ref[:, 4 * a:5 * a] = (dq * zc).astype(BF16)
        dcw_ref[0:1, :] += jnp.sum(g * qm2, axis=0, keepdims=True)
        dcw_ref[1:2, :] += jnp.sum(g * qm1, axis=0, keepdims=True)
        dcw_ref[2:3, :] += jnp.sum(g * q, axis=0, keepdims=True)

        @pl.when(i == n_tiles - 1)
        def _():
            for h in range(HEADS):
                dbb_ref[h] = jnp.broadcast_to(jnp.sum(dbb_ref[h], axis=1, keepdims=True), (BLK, hd))
                dws_ref[h] = _masked_ws(dws_ref, h)

    full = lambda shape: pl.BlockSpec(tuple(shape), lambda i: (0,) * len(shape))
    out_shapes = (_sds((t, 5 * a), BF16), _sds((1, a), F32), _sds((1, a), F32), _sds(w_spatial.shape, F32),
                  _sds(bb.shape, F32), _sds((8, a), F32), _sds((1, 2 * a), F32))
    return _call_hosting(
        body, "mixer_bwd", (n_tiles,), out_shapes,
        [tile, *prev, *nxt, dy_tile, dy_next, full((1, a)), full((1, a)), full(w_spatial.shape), full(bb.shape),
         full(conv_w.shape), full((1, 2 * a))],
        [tile, *[full(s.shape) for s in out_shapes[1:]]], [z, z, z, z, z, z, dy, dy, ln_g, ln_b, w_spatial, bb, conv_w, gg],
        [pltpu.VMEM((tm, a), F32)] * 4, comm)


def _all_reduce_small(pack, comm=None):
    r = pack.shape[0]
    hosted = _Hosted(comm, 1, 1)
    n_ci, n_co = len(hosted.operands), len(hosted.out_shapes)

    def body(*refs):
        in_ref, c_ins, out_ref, c_outs = refs[0], refs[1:1 + n_ci], refs[1 + n_ci], refs[2 + n_ci:2 + n_ci + n_co]
        acc_ref, recv_ref, send_sems, recv_sems = refs[2 + n_ci + n_co:6 + n_ci + n_co]
        sems = refs[6 + n_ci + n_co:]
        hosted.run("start", c_ins, c_outs, sems)
        x, y, c = _place()
        partners = [(x, y, 1 - c), (1 - x, y, c), (x, 1 - y, c)]
        acc_ref[0] = in_ref[...]
        for s, partner in enumerate(partners):
            cp = pltpu.make_async_remote_copy(
                src_ref=acc_ref.at[s], dst_ref=recv_ref.at[s], send_sem=send_sems.at[s], recv_sem=recv_sems.at[s],
                device_id=partner, device_id_type=MESH)
            cp.start()
            cp.wait()
            if s < 2:
                acc_ref[s + 1] = acc_ref[s] + recv_ref[s]
            else:
                out_ref[...] = acc_ref[s] + recv_ref[s]
        for stage in ("mid1", "mid2", "finish"):
            hosted.run(stage, c_ins, c_outs, sems)

    vmem = pl.BlockSpec(memory_space=pltpu.VMEM)
    res = pl.pallas_call(
        body, name="all_reduce_small", out_shape=tuple([_sds(pack.shape, F32)] + hosted.out_shapes),
        in_specs=[vmem] + hosted.in_specs, out_specs=tuple([vmem] + hosted.out_specs),
        input_output_aliases=hosted.aliases,
        scratch_shapes=[pltpu.VMEM((3, r, 128), F32), pltpu.VMEM((3, r, 128), F32), pltpu.SemaphoreType.DMA((3,)),
                        pltpu.SemaphoreType.DMA((3,))] + hosted.scratch,
        compiler_params=pltpu.CompilerParams(vmem_limit_bytes=VMEM_LIMIT_V7X),
    )(pack, *hosted.operands)
    return res[0], list(res[1:])


def _adamw_math(w, g, m, v):
    m = ADAM_B1 * m + (1.0 - ADAM_B1) * g
    v = ADAM_B2 * v + (1.0 - ADAM_B2) * (g * g)
    m_hat = m / (1.0 - ADAM_B1 ** ADAM_STEP)
    v_hat = v / (1.0 - ADAM_B2 ** ADAM_STEP)
    delta = -ADAM_LR * (m_hat / (jnp.sqrt(v_hat) + ADAM_EPS) + ADAM_WD * w)
    return delta, m, v


def _adamw_big(name, land, w, m, v, comm=None):
    nl, n_slots, r, c = land.shape
    tr = max(8, min(r, (256 * 640) // c // 8 * 8))
    while r % tr:
        tr -= 8
    grid = (nl, r // tr)
    hosted = _Hosted(comm, 4, 4)
    n_ci, n_co = len(hosted.operands), len(hosted.out_shapes)

    def body(*refs):
        land_ref, w_ref, m_ref, v_ref = refs[:4]
        c_ins = refs[4:4 + n_ci]
        g_out, d_out, m_out, v_out = refs[4 + n_ci:8 + n_ci]
        c_outs = refs[8 + n_ci:8 + n_ci + n_co]
        sems = refs[8 + n_ci + n_co:]

        def compute():
            g = land_ref[0].astype(F32)
            for s in range(1, n_slots):
                g = g + land_ref[s].astype(F32)
            delta, mn, vn = _adamw_math(w_ref[...], g, m_ref[...], v_ref[...])
            g_out[...] = g
            d_out[...] = delta
            m_out[...] = mn
            v_out[...] = vn

        hosted.wrap(grid, compute, c_ins, c_outs, sems)

    blk = pl.BlockSpec((None, tr, c), lambda l, i: (l, i, 0))
    res = pl.pallas_call(
        body, name=name, grid=grid, out_shape=tuple([_sds((nl, r, c), F32)] * 4 + hosted.out_shapes),
        in_specs=[pl.BlockSpec((None, n_slots, tr, c), lambda l, i: (l, 0, i, 0)), blk, blk, blk] + hosted.in_specs,
        out_specs=tuple([blk] * 4 + hosted.out_specs), input_output_aliases=hosted.aliases,
        scratch_shapes=hosted.scratch, compiler_params=_cparams(2))(land, w, m, v, *hosted.operands)
    return list(res[:4]), list(res[4:])


def _adamw_small(g, w, m, v):
    def body(g_ref, w_ref, m_ref, v_ref, d_out, m_out, v_out):
        delta, mn, vn = _adamw_math(w_ref[...], g_ref[...], m_ref[...], v_ref[...])
        d_out[...] = delta
        m_out[...] = mn
        v_out[...] = vn

    return pl.pallas_call(body, name="adamw_small", out_shape=tuple([_sds(g.shape, F32)] * 3),
                          compiler_params=pltpu.CompilerParams(vmem_limit_bytes=VMEM_LIMIT_V7X))(g, w, m, v)


def _rows(a):
    return a.reshape(-1, 128)


BIG = ["w_in", "w_out", "w_gate", "w_up", "w_down"]
AG_HOSTS = {
    ("norm1", 0): [("w_in", 0), ("conv_w", 0)],
    ("mm_in", 0): [("w_out", 0), ("w_gate", 0, 0, 2)], ("mixer", 0): [("w_gate", 0, 1, 2)],
    ("mm_out", 0): [("w_up", 0, 0, 2)], ("norm2", 0): [("w_up", 0, 1, 2)],
    ("mm_swiglu", 0): [("w_down", 0), ("w_in", 1), ("w_out", 1)], ("mm_down", 0): [("w_gate", 1)],
    ("mm_in", 1): [("w_up", 1)], ("mm_swiglu", 1): [("w_down", 1)],
}


def kernel(x, norm1_g, w_in, gmlp_ln_g, gmlp_ln_b, w_spatial, b_spatial, conv_w, group_norm_g, w_out, norm2_g, w_gate, w_up, w_down, final_norm_g, loss_target, m_norm1_g, m_w_in, m_gmlp_ln_g, m_gmlp_ln_b, m_w_spatial, m_b_spatial, m_conv_w, m_group_norm_g, m_w_out, m_norm2_g, m_w_gate, m_w_up, m_w_down, m_final_norm_g, v_norm1_g, v_w_in, v_gmlp_ln_g, v_gmlp_ln_b, v_w_spatial, v_b_spatial, v_conv_w, v_group_norm_g, v_w_out, v_norm2_g, v_w_gate, v_w_up, v_w_down, v_final_norm_g):
    nl = N_LAYERS
    t, d = x.shape[1], x.shape[2]
    a = d // 2
    hd = a // HEADS
    xin = x.reshape(t, d)
    target = loss_target.reshape(t, d)
    me = _index(_place())

    tr = lambda w: jnp.transpose(w, (0, 2, 1))
    big = {"w_in": w_in, "w_out": w_out, "w_gate": tr(w_gate), "w_up": tr(w_up), "w_down": w_down}
    big_m = {"w_in": m_w_in, "w_out": m_w_out, "w_gate": tr(m_w_gate), "w_up": tr(m_w_up), "w_down": m_w_down}
    big_v = {"w_in": v_w_in, "w_out": v_w_out, "w_gate": tr(v_w_gate), "w_up": tr(v_w_up), "w_down": v_w_down}
    block = {k: big[k].shape[1:] for k in BIG}
    view = {k: _cols_view(block[k][1]) if k == "w_in" else _rows_view(block[k][0]) for k in BIG}
    full_shape = {k: (block[k][0], N_DEV * block[k][1]) if k == "w_in" else (N_DEV * block[k][0], block[k][1])
                  for k in BIG}

    weights = {}
    shards = {(k, l): big[k][l].astype(BF16) for k in BIG for l in range(nl)}

    def ag_spec(k, l, part=0, n_parts=1):
        if k == "conv_w":
            return (conv_w, _sds((N_DEV, *conv_w.shape), F32), _SLOT_WHOLE, (0,), None)
        halves = (_cols_halves(*block[k], part, n_parts) if k == "w_in" else _rows_halves(block[k][0], part, n_parts))
        return (shards[(k, l)], _sds(full_shape[k], BF16), halves, (0, 1), weights.get((k, l)))

    bb = jnp.broadcast_to(b_spatial[..., None], (nl, HEADS, BLK, hd))

    def hosted(name, l):
        keys = AG_HOSTS.get((name, l), [])
        return keys, ([_ag_piece([ag_spec(*key) for key in keys])] if keys else None)

    def landed(keys, couts):
        for key, arr in zip(keys, couts):
            weights[key[:2]] = arr

    saved = []
    xl = xin
    for l in range(nl):
        keys, comm = hosted("norm1", l)
        h, couts = _rmsnorm_fwd(xl, norm1_g[l:l + 1], comm)
        landed(keys, couts)
        if l == 0:
            conv_full = jnp.transpose(weights[("conv_w", 0)], (1, 2, 0, 3)).reshape(nl, 3, a)
        keys, comm = hosted("mm_in", l)
        z, couts = _mm_in(h, weights[("w_in", l)], comm)
        landed(keys, couts)
        keys, comm = hosted("mixer", l)
        y, couts = _mixer_fwd(z, gmlp_ln_g[l:l + 1], gmlp_ln_b[l:l + 1], w_spatial[l], bb[l], conv_full[l],
                              group_norm_g[l:l + 1], comm)
        landed(keys, couts)
        keys, comm = hosted("mm_out", l)
        x1, couts = _mm_out(y, weights[("w_out", l)], xl, comm)
        landed(keys, couts)
        keys, comm = hosted("norm2", l)
        h2, couts = _rmsnorm_fwd(x1, norm2_g[l:l + 1], comm)
        landed(keys, couts)
        keys, comm = hosted("mm_swiglu", l)
        (gate, up, act), couts = _mm_swiglu(h2, weights[("w_gate", l)], weights[("w_up", l)], comm)
        landed(keys, couts)
        keys, comm = hosted("mm_down", l)
        x2, couts = _mm_down(act, weights[("w_down", l)], x1, comm)
        landed(keys, couts)
        saved.append(dict(x=xl, h=h, z=z, y=y, x1=x1, h2=h2, gate=gate, up=up, act=act))
        xl = x2

    dx, dxb, d_final_g, loss_part = _loss_head(xl, final_norm_g.reshape(1, d), target)
    small = [None] * nl
    core = lax.axis_index("c").astype(jnp.int32).reshape(1)
    stage_shape = {k: _sds((N_CHIPS, *block[k]), BF16) for k in BIG}
    land_shape = {k: _sds((nl, N_CHIPS, *block[k]), BF16) for k in BIG}
    grads = [dict() for _ in range(nl)]
    stages = [dict() for _ in range(nl)]
    sums = [dict() for _ in range(nl)]
    lands = {k: None for k in BIG}

    def core_job(l, keys):
        def sink(outs):
            stages[l].update(zip(keys, outs))
        return _rs_core_piece([(grads[l][k], stage_shape[k], view[k]) for k in keys]), sink

    def chip_job(l, items):
        keys = [item[0] for item in items]

        def rows(k, p0, p1, n_parts):
            per = block[k][0] // n_parts
            return (p0 * per, (p1 - p0) * per)

        def sink(outs):
            lands.update(zip(keys, outs))
        return _rs_chip_piece([(sums[l][k], land_shape[k], rows(k, p0, p1, n_parts), lands[k])
                               for k, p0, p1, n_parts in items], l), sink

    def add_up(l, keys):
        for k in keys:
            sums[l][k] = _chip_sums(f"chip_sums_{k}", grads[l][k], stages[l][k], k == "w_in", core)

    def host(*jobs):
        def deliver(couts):
            i = 0
            for piece, sink in jobs:
                n_out = len(piece.out_shapes)
                sink(couts[i:i + n_out])
                i += n_out
        return [piece for piece, _ in jobs], deliver

    whole = lambda k: (k, 0, 1, 1)
    for l in reversed(range(nl)):
        s = saved[l]
        wi, wo, wgt, wut, wd = [weights[(k, l)] for k in BIG]
        later = l + 1 < nl
        comm, deliver = host(chip_job(l + 1, [("w_in", 0, 1, 2)])) if later else host()
        (grads[l]["w_down"],), couts = _mm_dw("mm_dw_down", [s["act"]], dxb, 2816, 1024, comm)
        deliver(couts)
        comm, deliver = (host(core_job(l, ["w_down"]), chip_job(l + 1, [("w_in", 1, 2, 2)])) if later
                         else host(core_job(l, ["w_down"])))
        (dgate, dup), couts = _mm_dact(dxb, wd, s["gate"], s["up"], comm)
        deliver(couts)
        add_up(l, ["w_down"])
        comm, deliver = host(chip_job(l, [whole("w_down")]))
        (grads[l]["w_gate"], grads[l]["w_up"]), couts = _mm_dw("mm_dw_gate_up", [dgate, dup], s["h2"], 1408, 1024, comm)
        deliver(couts)
        comm, deliver = host(core_job(l, ["w_gate", "w_up"]))
        dh2, couts = _mm_dh2(dgate, dup, wgt, wut, comm)
        deliver(couts)
        add_up(l, ["w_gate", "w_up"])
        dx1, dx1b, d_n2 = _rmsnorm_bwd(s["x1"], norm2_g[l:l + 1], dh2, dx)
        comm, deliver = host(chip_job(l, [("w_gate", 0, 1, 4)]))
        dy, couts = _mm_dy(dx1b, wo, comm)
        deliver(couts)
        comm, deliver = host(chip_job(l, [("w_gate", 1, 2, 4)]))
        (grads[l]["w_out"],), couts = _mm_dw("mm_dw_out", [s["y"]], dx1b, 1024, 1024, comm)
        deliver(couts)
        comm, deliver = host(chip_job(l, [("w_gate", 2, 4, 4)]))
        (dz, d_lng, d_lnb, d_ws, d_bb, d_cw, d_gg), couts = _mixer_bwd(
            s["z"], dy, gmlp_ln_g[l:l + 1], gmlp_ln_b[l:l + 1], w_spatial[l], bb[l], conv_full[l], group_norm_g[l:l + 1],
            comm)
        deliver(couts)
        comm, deliver = host(chip_job(l, [("w_up", 0, 3, 4)]), core_job(l, ["w_out"]))
        (grads[l]["w_in"],), couts = _mm_dw("mm_dw_in", [s["h"]], dz, 2048, 1024, comm)
        deliver(couts)
        add_up(l, ["w_out"])
        comm, deliver = host(chip_job(l, [("w_up", 3, 4, 4), whole("w_out")]), core_job(l, ["w_in"]))
        dh, couts = _mm_dh(dz, wi, comm)
        deliver(couts)
        add_up(l, ["w_in"])
        dx, dxb, d_n1 = _rmsnorm_bwd(s["x"], norm1_g[l:l + 1], dh, dx1)
        small[l] = dict(norm1_g=d_n1, gmlp_ln_g=d_lng, gmlp_ln_b=d_lnb, w_spatial=d_ws, b_spatial=d_bb[:, :, 0],
                        group_norm_g=d_gg, norm2_g=d_n2, conv_w=d_cw[0:3])
    grad_x = dx.reshape(x.shape)

    rep = ["norm1_g", "gmlp_ln_g", "gmlp_ln_b", "w_spatial", "b_spatial", "group_norm_g", "norm2_g"]
    rep_w = dict(norm1_g=norm1_g, gmlp_ln_g=gmlp_ln_g, gmlp_ln_b=gmlp_ln_b, w_spatial=w_spatial, b_spatial=b_spatial,
                 group_norm_g=group_norm_g, norm2_g=norm2_g)
    rep_m = dict(norm1_g=m_norm1_g, gmlp_ln_g=m_gmlp_ln_g, gmlp_ln_b=m_gmlp_ln_b, w_spatial=m_w_spatial,
                 b_spatial=m_b_spatial, group_norm_g=m_group_norm_g, norm2_g=m_norm2_g)
    rep_v = dict(norm1_g=v_norm1_g, gmlp_ln_g=v_gmlp_ln_g, gmlp_ln_b=v_gmlp_ln_b, w_spatial=v_w_spatial,
                 b_spatial=v_b_spatial, group_norm_g=v_group_norm_g, norm2_g=v_norm2_g)
    parts = [_rows(jnp.stack([small[l][k].reshape(rep_w[k].shape[1:]) for l in range(nl)])) for k in rep]
    parts.append(_rows(d_final_g))
    parts.append(_rows(jnp.stack([small[l]["conv_w"] for l in range(nl)])))
    parts.append(jnp.broadcast_to(loss_part, (8, 128)))
    sizes = [p.shape[0] for p in parts]
    comm, deliver = host(chip_job(0, [whole("w_in")]))
    total, couts = _all_reduce_small(jnp.concatenate(parts, axis=0), comm)
    deliver(couts)
    offs = [0]
    for n in sizes:
        offs.append(offs[-1] + n)
    pieces = [total[offs[i]:offs[i + 1]] for i in range(len(parts))]
    loss = pieces[-1][0, 0]
    conv_g_full = pieces[-2].reshape(nl, 3, N_DEV, a // N_DEV)
    conv_g = lax.dynamic_index_in_dim(conv_g_full, me, axis=2, keepdims=False)
    n_rep = offs[len(rep) + 1]
    pad = jnp.zeros((2, 128), F32)

    def small_pack(named, final, conv):
        return jnp.concatenate([_rows(named[k]) for k in rep] + [_rows(final), _rows(conv), pad], axis=0)

    g_small = jnp.concatenate([total[:n_rep], _rows(conv_g), pad], axis=0)
    d_small, m_small, v_small = _adamw_small(
        g_small, small_pack(rep_w, final_norm_g, conv_w), small_pack(rep_m, m_final_norm_g, m_conv_w),
        small_pack(rep_v, v_final_norm_g, v_conv_w))

    def unpack(packed):
        out = {k: packed[offs[i]:offs[i + 1]].reshape(rep_w[k].shape) for i, k in enumerate(rep)}
        out["final_norm_g"] = packed[offs[len(rep)]:n_rep].reshape(final_norm_g.shape)
        out["conv_w"] = packed[n_rep:n_rep + 6].reshape(conv_w.shape)
        return out

    res = {"grad": unpack(g_small), "delta": unpack(d_small), "m": unpack(m_small), "v": unpack(v_small)}

    for k in BIG:
        outs, _ = _adamw_big(f"adamw_{k}", lands[k], big[k], big_m[k], big_v[k])
        if k in ("w_gate", "w_up"):
            outs = [tr(o) for o in outs]
        res["grad"][k], res["delta"][k], res["m"][k], res["v"][k] = outs

    order = ["norm1_g", "w_in", "gmlp_ln_g", "gmlp_ln_b", "w_spatial", "b_spatial", "conv_w", "group_norm_g", "w_out",
             "norm2_g", "w_gate", "w_up", "w_down", "final_norm_g"]
    return (loss, grad_x, *[res["grad"][k] for k in order], *[res["delta"][k] for k in order],
            *[res["m"][k] for k in order], *[res["v"][k] for k in order])
```

```python
import functools
import math
import operator

import jax
import jax.numpy as jnp
from jax import lax
from jax.experimental import pallas as pl
from jax.experimental.pallas import tpu as pltpu

F32 = jnp.float32
BF16 = jnp.bfloat16
MESH = pl.DeviceIdType.MESH

N_DEV = 8
N_LAYERS = 2
HEADS = 8
BLK = 128
CHUNK = 64
HALO = 16
RMS_EPS = 1e-6
LN_EPS = 1e-5
ADAM_LR, ADAM_B1, ADAM_B2, ADAM_EPS, ADAM_WD, ADAM_STEP = 0.001, 0.9, 0.999, 1e-8, 0.01, 10
GELU_C = math.sqrt(2.0 / math.pi)
GELU_A = 0.044715

VMEM_LIMIT_V7X = 56 * 1024 * 1024
_TM = 1024
_TN = 1024
_TT = 1024
_TM_MIX = 256
_TM_NORM = 512


def _cparams(n_axes):
    return pltpu.CompilerParams(dimension_semantics=("arbitrary",) * n_axes, vmem_limit_bytes=VMEM_LIMIT_V7X)


def _sds(shape, dtype):
    return jax.ShapeDtypeStruct(tuple(shape), dtype)


def _place():
    return lax.axis_index("x"), lax.axis_index("y"), lax.axis_index("c")


def _index(place):
    return 4 * place[0] + 2 * place[1] + place[2]


class _Piece:
    def __init__(self, operands, out_shapes, aliases, n_sems, start, finish, mid1=None, mid2=None):
        self.operands, self.out_shapes, self.aliases, self.n_sems = list(operands), list(out_shapes), dict(aliases), n_sems
        nothing = lambda ctx: None
        self.start, self.mid1, self.mid2, self.finish = start, mid1 or nothing, mid2 or nothing, finish


class _Ctx:
    def __init__(self, ins, outs, sems, offs):
        self.ins, self.outs, self.sems = ins, outs, sems
        self.o_in, self.o_out, self.o_send, self.o_recv, self.o_loc = offs

    def inp(self, i):
        return self.ins[self.o_in + i]

    def out(self, i):
        return self.outs[self.o_out + i]

    def send(self, k):
        return self.sems[0].at[self.o_send + k]

    def recv(self, k):
        return self.sems[1].at[self.o_recv + k]

    def local(self, k):
        return self.sems[2].at[self.o_loc + k]


class _Hosted:
    def __init__(self, pieces, n_in_before, n_out_before):
        self.pieces = [p for p in (pieces or []) if p is not None]
        self.operands, self.out_shapes, self.aliases, self.offs = [], [], {}, []
        counts = [0, 0, 0]
        for p in self.pieces:
            self.offs.append((len(self.operands), len(self.out_shapes), *counts))
            for i, j in p.aliases.items():
                self.aliases[n_in_before + len(self.operands) + i] = n_out_before + len(self.out_shapes) + j
            self.operands += p.operands
            self.out_shapes += p.out_shapes
            counts = [c + n for c, n in zip(counts, p.n_sems)]
        hbm = pl.BlockSpec(memory_space=pl.ANY)
        self.in_specs = [hbm] * len(self.operands)
        self.out_specs = [hbm] * len(self.out_shapes)
        self.scratch = [pltpu.SemaphoreType.DMA((max(c, 1),)) for c in counts] if self.pieces else []

    def run(self, stage, ins, outs, sems):
        for p, offs in zip(self.pieces, self.offs):
            getattr(p, stage)(_Ctx(ins, outs, sems, offs))

    def wrap(self, grid, compute, ins, outs, sems):
        if not self.pieces:
            compute()
            return
        n_steps = math.prod(grid)
        lin = 0
        for ax, g in enumerate(grid):
            lin = lin * g + pl.program_id(ax)
        pl.when(lin == 0)(lambda: self.run("start", ins, outs, sems))
        compute()
        pl.when(lin == n_steps // 2)(lambda: self.run("mid1", ins, outs, sems))
        pl.when(lin == max(n_steps - 3, n_steps // 2))(lambda: self.run("mid2", ins, outs, sems))
        pl.when(lin == n_steps - 1)(lambda: self.run("finish", ins, outs, sems))


def _cols_view(width):
    return lambda ref, p: ref.at[:, pl.ds(pl.multiple_of(p * width, 128), width)]


def _rows_view(height):
    return lambda ref, p: ref.at[pl.ds(pl.multiple_of(p * height, 16), height), :]


def _cols_halves(rows, width, part, n_parts):
    hr = rows // n_parts // 2
    at = lambda h: pl.ds(part * 2 * hr + h * hr, hr)
    return (lambda ref, p, h: ref.at[at(h), pl.ds(pl.multiple_of(p * width, 128), width)],
            lambda ref, h: ref.at[at(h), :], 2)


def _rows_halves(height, part, n_parts):
    hh = height // n_parts // 2
    return (lambda ref, p, h: ref.at[pl.ds(pl.multiple_of(p * height + part * 2 * hh + h * hh, 16), hh), :],
            lambda ref, h: ref.at[pl.ds(part * 2 * hh + h * hh, hh), :], 2)


_SLOT_WHOLE = (lambda ref, p, h: ref.at[p], lambda ref, h: ref, 1)


def _ag_piece(specs):
    units = [(a, h) for a, s in enumerate(specs) for h in s[3]]

    def plan(ctx):
        x, y, c = _place()
        me, sib, xn, yn, dg = (x, y, c), (x, y, 1 - c), (1 - x, y, c), (x, 1 - y, c), (1 - x, 1 - y, c)

        def copy(u, k, block, to, from_shard=False):
            a, h = units[u]
            dst_of, src_of, _ = specs[a][2]
            dst = dst_of(ctx.out(a), _index(block), h)
            return pltpu.make_async_remote_copy(
                src_ref=src_of(ctx.inp(a), h) if from_shard else dst, dst_ref=dst, send_sem=ctx.send(7 * u + k),
                recv_sem=ctx.recv(7 * u + k), device_id=to, device_id_type=MESH)

        def local(u):
            a, h = units[u]
            dst_of, src_of, _ = specs[a][2]
            return pltpu.make_async_copy(src_of(ctx.inp(a), h), dst_of(ctx.out(a), _index(me), h), ctx.local(u))

        def relay(u):
            return copy(u, 3, xn, yn) if units[u][1] % 2 == 0 else copy(u, 3, yn, xn)

        return me, sib, xn, yn, dg, c, copy, local, relay

    def start(ctx):
        me, sib, xn, yn, dg, c, copy, local, relay = plan(ctx)
        for u in range(len(units)):
            local(u).start()
            for k, to in enumerate((sib, xn, yn)):
                copy(u, k, me, to, from_shard=True).start()

    def mid1(ctx):
        me, sib, xn, yn, dg, c, copy, local, relay = plan(ctx)
        for u in range(len(units)):
            copy(u, 1, xn, me).wait_recv()
            copy(u, 2, yn, me).wait_recv()
            relay(u).start()
            copy(u, 4, xn, sib).start()
            copy(u, 5, yn, sib).start()

    def mid2(ctx):
        me, sib, xn, yn, dg, c, copy, local, relay = plan(ctx)
        for u in range(len(units)):
            copy(u, 3, dg, me).wait_recv()
            copy(u, 6, dg, sib).start()

    def finish(ctx):
        me, sib, xn, yn, dg, c, copy, local, relay = plan(ctx)
        other = lambda place: (place[0], place[1], 1 - c)
        for u in range(len(units)):
            for k, block in ((0, sib), (4, other(xn)), (5, other(yn)), (6, other(dg))):
                copy(u, k, block, me).wait_recv()
        for u in range(len(units)):
            for k, to in enumerate((sib, xn, yn)):
                copy(u, k, me, to, from_shard=True).wait_send()
            relay(u).wait_send()
            for k, block in ((4, xn), (5, yn), (6, dg)):
                copy(u, k, block, sib).wait_send()
            local(u).wait()

    n_u = len(units)
    operands, aliases = [s[0] for s in specs], {}
    for a, spec in enumerate(specs):
        if spec[4] is not None:
            aliases[len(operands)] = a
            operands.append(spec[4])
    return _Piece(operands, [s[1] for s in specs], aliases, (7 * n_u, 7 * n_u, n_u), start, finish, mid1, mid2)


N_CHIPS = 4


def _rs_core_piece(specs):
    n = len(specs)

    def copies(ctx):
        x, y, c = _place()
        out = []
        for a in range(n):
            for q in range(N_CHIPS):
                out.append(pltpu.make_async_remote_copy(
                    src_ref=specs[a][2](ctx.inp(a), 2 * q + (1 - c)), dst_ref=ctx.out(a).at[q],
                    send_sem=ctx.send(N_CHIPS * a + q), recv_sem=ctx.recv(N_CHIPS * a + q), device_id=(x, y, 1 - c),
                    device_id_type=MESH))
        return out

    def start(ctx):
        for cp in copies(ctx):
            cp.start()

    def finish(ctx):
        for cp in copies(ctx):
            cp.wait_recv()
            cp.wait_send()

    return _Piece([s[0] for s in specs], [s[1] for s in specs], {}, (N_CHIPS * n, N_CHIPS * n, 0), start, finish)


def _rs_chip_piece(specs, layer):
    n = len(specs)
    hops = [(1, 0), (0, 1), (1, 1)]

    def copies(ctx):
        x, y, c = _place()
        mine = 2 * x + y
        out = []
        for a in range(n):
            rows = pl.ds(*specs[a][2])
            sums, land = ctx.inp(a), ctx.out(a)
            out.append((pltpu.make_async_copy(sums.at[mine, rows], land.at[layer, mine, rows], ctx.local(a)), None))
            for j, (dx, dy) in enumerate(hops):
                px, py = x ^ dx, y ^ dy
                peer = 2 * px + py
                send = pltpu.make_async_remote_copy(
                    src_ref=sums.at[peer, rows], dst_ref=land.at[layer, mine, rows], send_sem=ctx.send(3 * a + j),
                    recv_sem=ctx.recv(3 * a + j), device_id=(px, py, c), device_id_type=MESH)
                recv = pltpu.make_async_remote_copy(
                    src_ref=sums.at[peer, rows], dst_ref=land.at[layer, peer, rows], send_sem=ctx.send(3 * a + j),
                    recv_sem=ctx.recv(3 * a + j), device_id=(px, py, c), device_id_type=MESH)
                out.append((send, recv))
        return out

    def start(ctx):
        for send, _ in copies(ctx):
            send.start()

    def finish(ctx):
        for send, recv in copies(ctx):
            if recv is None:
                send.wait()
            else:
                recv.wait_recv()
                send.wait_send()

    operands, aliases = [s[0] for s in specs], {}
    for a, spec in enumerate(specs):
        if spec[3] is not None:
            aliases[len(operands)] = a
            operands.append(spec[3])
    return _Piece(operands, [s[1] for s in specs], aliases, (3 * n, 3 * n, n), start, finish)


def _chip_sums(name, grad, stage, by_cols, core):
    _, r, c = stage.shape
    tr = r
    while tr * c > 1024 * 1024 or r % tr or tr % 16:
        tr -= 16
    n_t = r // tr

    def body(core_ref, g_ref, s_ref, o_ref):
        o_ref[...] = (g_ref[...].astype(F32) + s_ref[...].astype(F32)).astype(BF16)

    if by_cols:
        gspec = pl.BlockSpec((tr, c), lambda q, i, core_ref: (i, 2 * q + core_ref[0]))
    else:
        gspec = pl.BlockSpec((tr, c), lambda q, i, core_ref: ((2 * q + core_ref[0]) * n_t + i, 0))
    sspec = pl.BlockSpec((None, tr, c), lambda q, i, core_ref: (q, i, 0))
    return pl.pallas_call(
        body, name=name, out_shape=_sds(stage.shape, BF16),
        grid_spec=pltpu.PrefetchScalarGridSpec(num_scalar_prefetch=1, grid=(N_CHIPS, n_t), in_specs=[gspec, sspec],
                                               out_specs=sspec),
        compiler_params=_cparams(2))(core, grad, stage)


def _call_hosting(body, name, grid, out_shapes, in_specs, out_specs, operands, scratch, comm):
    n_in, n_out, n_scr = len(operands), len(out_shapes), len(scratch)
    hosted = _Hosted(comm, n_in, n_out)
    n_ci, n_co = len(hosted.operands), len(hosted.out_shapes)

    def hosting_body(*refs):
        ins, rest = refs[:n_in], refs[n_in:]
        c_ins, rest = rest[:n_ci], rest[n_ci:]
        outs, rest = rest[:n_out], rest[n_out:]
        c_outs, rest = rest[:n_co], rest[n_co:]
        hosted.wrap(grid, lambda: body(*ins, *outs, *rest[:n_scr]), c_ins, c_outs, rest[n_scr:])

    res = pl.pallas_call(
        hosting_body, name=name, grid=grid, out_shape=tuple(list(out_shapes) + hosted.out_shapes),
        in_specs=list(in_specs) + hosted.in_specs, out_specs=tuple(list(out_specs) + hosted.out_specs),
        input_output_aliases=hosted.aliases, scratch_shapes=list(scratch) + hosted.scratch,
        compiler_params=_cparams(len(grid)))(*operands, *hosted.operands)
    return list(res[:n_out]), list(res[n_out:])


def _matmul(name, grid, nk, kaxis, pairs, dims, extras, outs, epilogue, sum_pairs, acc_shape, comm=None, split=None):
    n_p, n_e, n_o = len(pairs), len(extras), len(outs)
    n_acc = 0 if nk == 1 else (1 if sum_pairs else n_p)
    n_in = 2 * n_p + n_e
    hosted = _Hosted(comm, n_in, n_o)
    n_ci, n_co = len(hosted.operands), len(hosted.out_shapes)

    def body(*refs):
        a_refs = refs[0:2 * n_p:2]
        b_refs = refs[1:2 * n_p:2]
        e_refs = refs[2 * n_p:n_in]
        c_ins = refs[n_in:n_in + n_ci]
        o_refs = refs[n_in + n_ci:n_in + n_ci + n_o]
        c_outs = refs[n_in + n_ci + n_o:n_in + n_ci + n_o + n_co]
        acc_refs = refs[n_in + n_ci + n_o + n_co:n_in + n_ci + n_o + n_co + n_acc]
        sems = refs[n_in + n_ci + n_o + n_co + n_acc:]

        def dots():
            prods = [lax.dot_general(a[...], b[...], (dims, ((), ())), preferred_element_type=F32)
                     for a, b in zip(a_refs, b_refs)]
            if sum_pairs and n_p > 1:
                prods = [functools.reduce(operator.add, prods)]
            return prods

        def compute():
            if nk == 1 and split is not None:
                n_split, b_axis, n_row = split
                width = b_refs[0].shape[b_axis] // n_split
                height = a_refs[0].shape[0] // n_row
                for s in range(n_split):
                    cols = pl.ds(s * width, width)
                    for r in range(n_row):
                        rows = pl.ds(r * height, height)
                        epilogue([lax.dot_general(a[rows, :], b[cols, :] if b_axis == 0 else b[:, cols], (dims, ((), ())),
                                                  preferred_element_type=F32) for a, b in zip(a_refs, b_refs)],
                                 e_refs, o_refs, rows, cols)
                return
            if nk == 1:
                epilogue(dots(), e_refs, o_refs)
                return
            k = pl.program_id(kaxis)

            @pl.when(k == 0)
            def _():
                for acc, p in zip(acc_refs, dots()):
                    acc[...] = p

            if nk > 2:
                @pl.when((k > 0) & (k < nk - 1))
                def _():
                    for acc, p in zip(acc_refs, dots()):
                        acc[...] += p

            @pl.when(k == nk - 1)
            def _():
                epilogue([acc[...] + p for acc, p in zip(acc_refs, dots())], e_refs, o_refs)

        hosted.wrap(grid, compute, c_ins, c_outs, sems)

    operands, in_specs = [], []
    for a, a_spec, b, b_spec in pairs:
        operands += [a, b]
        in_specs += [a_spec, b_spec]
    for e, e_spec in extras:
        operands.append(e)
        in_specs.append(e_spec)
    res = pl.pallas_call(
        body, name=name, grid=grid,
        out_shape=tuple([o for o, _ in outs] + hosted.out_shapes),
        in_specs=in_specs + hosted.in_specs, out_specs=tuple([s for _, s in outs] + hosted.out_specs),
        input_output_aliases=hosted.aliases,
        scratch_shapes=[pltpu.VMEM(acc_shape, F32) for _ in range(n_acc)] + hosted.scratch,
        compiler_params=_cparams(len(grid)),
    )(*operands, *hosted.operands)
    return list(res[:n_o]), list(res[n_o:])


NN = ((1,), (0,))
NT = ((1,), (1,))
TN = ((0,), (0,))


def _tile(n, want):
    if n <= want:
        return n
    t = want // 128 * 128
    while n % t:
        t -= 128
    return t


def _silu_parts(g):
    s = 0.5 + 0.5 * jnp.tanh(0.5 * g)
    return s, g * s


def _mm_in(h, w_in, comm=None):
    t, d = h.shape
    n = w_in.shape[1]
    tm, tn = _tile(t, _TM), _tile(n, _TN)

    def epi(accs, e, o):
        o[0][...] = accs[0].astype(BF16)

    outs, couts = _matmul(
        "mm_in", (n // tn, t // tm), 1, None,
        [(h, pl.BlockSpec((tm, d), lambda j, i: (i, 0)), w_in, pl.BlockSpec((d, tn), lambda j, i: (0, j)))],
        NN, [], [(_sds((t, n), BF16), pl.BlockSpec((tm, tn), lambda j, i: (i, j)))], epi, True, None, comm)
    return outs[0], couts


def _mm_out(y, w_out, x, comm=None):
    t, m = y.shape
    d = w_out.shape[1]
    tm, tn = _tile(t, _TM), _tile(d, _TN)

    def epi(accs, e, o):
        o[0][...] = e[0][...] + accs[0]

    outs, couts = _matmul(
        "mm_out", (t // tm, d // tn), 1, None,
        [(y, pl.BlockSpec((tm, m), lambda i, j: (i, 0)), w_out, pl.BlockSpec((m, tn), lambda i, j: (0, j)))],
        NN, [(x, pl.BlockSpec((tm, tn), lambda i, j: (i, j)))],
        [(_sds((t, d), F32), pl.BlockSpec((tm, tn), lambda i, j: (i, j)))], epi, True, None, comm)
    return outs[0], couts


def _mm_swiglu(h2, wgt, wut, comm=None):
    t, d = h2.shape
    f = wgt.shape[0]
    tm, tn = _tile(t, 2 * _TM), _tile(f, 512)

    def epi(accs, e, o, rows, cols):
        g, u = accs
        s, sg = _silu_parts(g)
        o[0][rows, cols] = (sg * u).astype(BF16)
        o[1][rows, cols] = (u * (s + sg * (1.0 - s))).astype(BF16)
        o[2][rows, cols] = sg.astype(BF16)

    wspec = pl.BlockSpec((tn, d), lambda j, i: (j, 0))
    hspec = pl.BlockSpec((tm, d), lambda j, i: (i, 0))
    ospec = pl.BlockSpec((tm, tn), lambda j, i: (i, j))
    osh = _sds((t, f), BF16)
    outs, couts = _matmul("mm_swiglu", (f // tn, t // tm), 1, None, [(h2, hspec, wgt, wspec), (h2, hspec, wut, wspec)],
                          NT, [], [(osh, ospec)] * 3, epi, False, None, comm, split=(tn // 256, 0, 2))
    return outs, couts


def _mm_down(act, wd, x1, comm=None):
    t, f = act.shape
    d = wd.shape[1]
    tm, tn = _tile(t, _TM), _tile(d, _TN)
    nk = 2
    tk = f // nk

    def epi(accs, e, o):
        o[0][...] = e[0][...] + accs[0]

    outs, couts = _matmul(
        "mm_down", (t // tm, d // tn, nk), nk, 2,
        [(act, pl.BlockSpec((tm, tk), lambda i, j, k: (i, k)), wd, pl.BlockSpec((tk, tn), lambda i, j, k: (k, j)))],
        NN, [(x1, pl.BlockSpec((tm, tn), lambda i, j, k: (i, j)))],
        [(_sds((t, d), F32), pl.BlockSpec((tm, tn), lambda i, j, k: (i, j)))], epi, True, (tm, tn), comm)
    return outs[0], couts


def _mm_dact(dxb, wd, dact_dgate, dact_dup, comm=None):
    t, d = dxb.shape
    f = wd.shape[0]
    tm, tn = _tile(t, 2 * _TM), _tile(f, 512)

    def epi(accs, e, o, rows, cols):
        da = accs[0]
        o[0][rows, cols] = (da * e[0][rows, cols].astype(F32)).astype(BF16)
        o[1][rows, cols] = (da * e[1][rows, cols].astype(F32)).astype(BF16)

    bspec = pl.BlockSpec((tm, tn), lambda j, i: (i, j))
    osh = _sds((t, f), BF16)
    outs, couts = _matmul(
        "mm_dact", (f // tn, t // tm), 1, None,
        [(dxb, pl.BlockSpec((tm, d), lambda j, i: (i, 0)), wd, pl.BlockSpec((tn, d), lambda j, i: (j, 0)))],
        NT, [(dact_dgate, bspec), (dact_dup, bspec)], [(osh, bspec)] * 2, epi, True, None, comm, split=(tn // 256, 0, 2))
    return outs, couts


def _mm_dh2(dgate, dup, wgt, wut, comm=None):
    t, f = dgate.shape
    d = wgt.shape[1]
    tm, tn = _tile(t, _TM), _tile(d, _TN)
    nk = 4
    tk = f // nk

    def epi(accs, e, o):
        o[0][...] = accs[0]

    aspec = pl.BlockSpec((tm, tk), lambda i, j, k: (i, k))
    wspec = pl.BlockSpec((tk, tn), lambda i, j, k: (k, j))
    outs, couts = _matmul("mm_dh2", (t // tm, d // tn, nk), nk, 2, [(dgate, aspec, wgt, wspec), (dup, aspec, wut, wspec)],
                          NN, [], [(_sds((t, d), F32), pl.BlockSpec((tm, tn), lambda i, j, k: (i, j)))], epi, True,
                          (tm, tn), comm)
    return outs[0], couts


def _mm_dw(name, a_list, b, tmo, tno, comm=None):
    t, m = a_list[0].shape
    n = b.shape[1]
    tt = _tile(t, _TT)
    nk = t // tt
    tmo, tno = _tile(m, tmo), _tile(n, tno)

    def epi(accs, e, o):
        for acc, out in zip(accs, o):
            out[...] = acc.astype(BF16)

    aspec = pl.BlockSpec((tt, tmo), lambda i, j, k: (k, i))
    bspec = pl.BlockSpec((tt, tno), lambda i, j, k: (k, j))
    ospec = pl.BlockSpec((tmo, tno), lambda i, j, k: (i, j))
    if nk == 1:
        return _matmul(name, (m // tmo, n // tno, 1), 1, None, [(a, aspec, b, bspec) for a in a_list], TN, [],
                       [(_sds((m, n), BF16), ospec)] * len(a_list), epi, False, None, comm)
    return _matmul(name, (m // tmo, n // tno, nk), nk, 2, [(a, aspec, b, bspec) for a in a_list], TN, [],
                   [(_sds((m, n), BF16), ospec)] * len(a_list), epi, False, (tmo, tno), comm)


def _mm_dy(dxb, w_out, comm=None):
    t, d = dxb.shape
    m = w_out.shape[0]
    tm, tn = _tile(t, _TM), _tile(m, _TN)

    def epi(accs, e, o):
        o[0][...] = accs[0].astype(BF16)

    outs, couts = _matmul(
        "mm_dy", (t // tm, m // tn), 1, None,
        [(dxb, pl.BlockSpec((tm, d), lambda i, j: (i, 0)), w_out, pl.BlockSpec((tn, d), lambda i, j: (j, 0)))], NT, [],
        [(_sds((t, m), BF16), pl.BlockSpec((tm, tn), lambda i, j: (i, j)))], epi, True, None, comm)
    return outs[0], couts


def _mm_dh(dz, w_in, comm=None):
    t, n = dz.shape
    d = w_in.shape[0]
    tm, tn = _tile(t, _TM), _tile(d, _TN)
    nk = 2
    tk = n // nk

    def epi(accs, e, o):
        o[0][...] = accs[0]

    outs, couts = _matmul(
        "mm_dh", (t // tm, d // tn, nk), nk, 2,
        [(dz, pl.BlockSpec((tm, tk), lambda i, j, k: (i, k)), w_in, pl.BlockSpec((tn, tk), lambda i, j, k: (j, k)))], NT,
        [], [(_sds((t, d), F32), pl.BlockSpec((tm, tn), lambda i, j, k: (i, j)))], epi, True, (tm, tn), comm)
    return outs[0], couts


def _rmsnorm_fwd(x, g, comm=None):
    t, d = x.shape
    tm = min(_TM_NORM, t)

    def body(x_ref, g_ref, o_ref):
        xv = x_ref[...]
        rs = lax.rsqrt(jnp.mean(xv * xv, axis=-1, keepdims=True) + RMS_EPS)
        o_ref[...] = (xv * rs * g_ref[...]).astype(BF16)

    outs, couts = _call_hosting(
        body, "rmsnorm_fwd", (t // tm,), [_sds((t, d), BF16)],
        [pl.BlockSpec((tm, d), lambda i: (i, 0)), pl.BlockSpec((1, d), lambda i: (0, 0))],
        [pl.BlockSpec((tm, d), lambda i: (i, 0))], [x, g], [], comm)
    return outs[0], couts


def _rmsnorm_bwd_math(xv, g, dh):
    rs = lax.rsqrt(jnp.mean(xv * xv, axis=-1, keepdims=True) + RMS_EPS)
    xh = xv * rs
    gd = dh * g
    dx = rs * (gd - xh * jnp.mean(gd * xh, axis=-1, keepdims=True))
    return dx, jnp.sum(dh * xh, axis=0, keepdims=True)


def _rmsnorm_bwd(x, g, dh, dres):
    t, d = x.shape
    tm = min(_TM_NORM, t)

    def body(x_ref, g_ref, dh_ref, dres_ref, dx_ref, dxb_ref, dg_ref):
        dx, dg = _rmsnorm_bwd_math(x_ref[...], g_ref[...], dh_ref[...])
        dx = dx + dres_ref[...]
        dx_ref[...] = dx
        dxb_ref[...] = dx.astype(BF16)

        @pl.when(pl.program_id(0) == 0)
        def _():
            dg_ref[...] = dg

        @pl.when(pl.program_id(0) > 0)
        def _():
            dg_ref[...] += dg

    row = pl.BlockSpec((tm, d), lambda i: (i, 0))
    vec = pl.BlockSpec((1, d), lambda i: (0, 0))
    return pl.pallas_call(
        body, name="rmsnorm_bwd", grid=(t // tm,),
        out_shape=(_sds((t, d), F32), _sds((t, d), BF16), _sds((1, d), F32)),
        in_specs=[row, vec, row, row], out_specs=(row, row, vec), compiler_params=_cparams(1))(x, g, dh, dres)


def _loss_head(x, g, target):
    t, d = x.shape
    tm = min(_TM_NORM, t)

    def body(x_ref, g_ref, t_ref, dx_ref, dxb_ref, dg_ref, loss_ref):
        xv, gv = x_ref[...], g_ref[...]
        rs = lax.rsqrt(jnp.mean(xv * xv, axis=-1, keepdims=True) + RMS_EPS)
        diff = xv * rs * gv - t_ref[...]
        part = 0.5 * jnp.sum(jnp.mean(diff * diff, axis=-1, keepdims=True), axis=0, keepdims=True)
        part = jnp.broadcast_to(part, (1, 128))
        dx, dg = _rmsnorm_bwd_math(xv, gv, diff * (1.0 / d))
        dx_ref[...] = dx
        dxb_ref[...] = dx.astype(BF16)

        @pl.when(pl.program_id(0) == 0)
        def _():
            dg_ref[...] = dg
            loss_ref[...] = part

        @pl.when(pl.program_id(0) > 0)
        def _():
            dg_ref[...] += dg
            loss_ref[...] += part

    row = pl.BlockSpec((tm, d), lambda i: (i, 0))
    vec = pl.BlockSpec((1, d), lambda i: (0, 0))
    return pl.pallas_call(
        body, name="loss_head", grid=(t // tm,),
        out_shape=(_sds((t, d), F32), _sds((t, d), BF16), _sds((1, d), F32), _sds((1, 128), F32)),
        in_specs=[row, vec, row], out_specs=(row, row, vec, pl.BlockSpec((1, 128), lambda i: (0, 0))),
        compiler_params=_cparams(1))(x, g, target)


def _gelu(x):
    th = jnp.tanh(GELU_C * (x + GELU_A * x * x * x))
    return 0.5 * x * (1.0 + th), th


def _gelu_grad(x, th):
    return 0.5 * (1.0 + th) + 0.5 * x * (1.0 - th * th) * GELU_C * (1.0 + 3.0 * GELU_A * x * x)


def _masked_ws(ws_ref, h):
    i = lax.broadcasted_iota(jnp.int32, (BLK, BLK), 0) // CHUNK
    j = lax.broadcasted_iota(jnp.int32, (BLK, BLK), 1) // CHUNK
    return jnp.where(j <= i, ws_ref[h], 0.0)


def _shift_down(q, n, first_rows):
    rolled = pltpu.roll(q, n, 0)
    row = lax.broadcasted_iota(jnp.int32, q.shape, 0)
    for r, val in enumerate(first_rows):
        rolled = jnp.where(row == r, val, rolled)
    return rolled


def _shift_up(q, n, last_rows):
    tm = q.shape[0]
    rolled = pltpu.roll(q, tm - n, 0)
    row = lax.broadcasted_iota(jnp.int32, q.shape, 0)
    for r, val in enumerate(last_rows):
        rolled = jnp.where(row == tm - n + r, val, rolled)
    return rolled


def _mixer_specs(t, a, tm):
    hb = tm // HALO
    last = t // HALO - 1
    tile = pl.BlockSpec((tm, 5 * a), lambda i: (i, 0))
    prev = [pl.BlockSpec((HALO, a), functools.partial(lambda i, col: (jnp.maximum(i * hb - 1, 0), col), col=col))
            for col in (3, 4)]
    nxt = [pl.BlockSpec((HALO, a), functools.partial(lambda i, col: (jnp.minimum((i + 1) * hb, last), col), col=col))
           for col in (2, 3, 4)]
    return tile, prev, nxt


def _group_a_fwd(zu, zv, lng, lnb, ws_ref, bb_ref, mixed_ref, vln_ref):
    u, thu = _gelu(zu)
    v, thv = _gelu(zv)
    mu = jnp.mean(v, axis=-1, keepdims=True)
    vc = v - mu
    rs = lax.rsqrt(jnp.mean(vc * vc, axis=-1, keepdims=True) + LN_EPS)
    vhat = vc * rs
    vln_ref[...] = vhat * lng + lnb
    tm, a = zu.shape
    hd = a // HEADS
    for h in range(HEADS):
        w = _masked_ws(ws_ref, h).astype(BF16)
        for b in range(tm // BLK):
            rows, cols = pl.ds(b * BLK, BLK), pl.ds(h * hd, hd)
            mixed_ref[rows, cols] = jnp.dot(w, vln_ref[rows, cols].astype(BF16), preferred_element_type=F32) + bb_ref[h]
    return u, thu, thv, rs, vhat


def _mixer_fwd(z, ln_g, ln_b, w_spatial, bb, conv_w, gg, comm=None):
    t = z.shape[0]
    a = z.shape[1] // 5
    tm = min(_TM_MIX, t)
    tile, prev, _ = _mixer_specs(t, a, tm)

    def body(z_ref, pc_ref, ph_ref, lng_ref, lnb_ref, ws_ref, bb_ref, cw_ref, gg_ref, y_ref, mixed_ref, vln_ref):
        i = pl.program_id(0)
        zu = z_ref[:, 0:a].astype(F32)
        zv = z_ref[:, a:2 * a].astype(F32)
        u, _, _, _, _ = _group_a_fwd(zu, zv, lng_ref[...], lnb_ref[...], ws_ref, bb_ref, mixed_ref, vln_ref)
        ya = u * mixed_ref[...]
        ra = lax.rsqrt(jnp.mean(ya * ya, axis=-1, keepdims=True) + RMS_EPS)
        y_ref[:, 0:a] = (ya * ra * gg_ref[:, 0:a]).astype(BF16)

        zb = z_ref[:, 2 * a:3 * a].astype(F32)
        q = z_ref[:, 3 * a:4 * a].astype(F32) * z_ref[:, 4 * a:5 * a].astype(F32)
        qp = jnp.where(i > 0, pc_ref[...].astype(F32) * ph_ref[...].astype(F32), 0.0)
        qm1 = _shift_down(q, 1, [qp[HALO - 1:HALO]])
        qm2 = _shift_down(q, 2, [qp[HALO - 2:HALO - 1], qp[HALO - 1:HALO]])
        cv = cw_ref[0:1, :] * qm2 + cw_ref[1:2, :] * qm1 + cw_ref[2:3, :] * q
        yb = zb * cv
        rb = lax.rsqrt(jnp.mean(yb * yb, axis=-1, keepdims=True) + RMS_EPS)
        y_ref[:, a:2 * a] = (yb * rb * gg_ref[:, a:2 * a]).astype(BF16)

    full = lambda shape: pl.BlockSpec(shape, lambda i: (0,) * len(shape))
    outs, couts = _call_hosting(
        body, "mixer_fwd", (t // tm,), [_sds((t, 2 * a), BF16)],
        [tile, *prev, full((1, a)), full((1, a)), full(w_spatial.shape), full(bb.shape), full(conv_w.shape),
         full((1, 2 * a))],
        [pl.BlockSpec((tm, 2 * a), lambda i: (i, 0))], [z, z, z, ln_g, ln_b, w_spatial, bb, conv_w, gg],
        [pltpu.VMEM((tm, a), F32), pltpu.VMEM((tm, a), F32)], comm)
    return outs[0], couts


def _mixer_bwd(z, dy, ln_g, ln_b, w_spatial, bb, conv_w, gg, comm=None):
    t = z.shape[0]
    a = z.shape[1] // 5
    hd = a // HEADS
    tm = min(_TM_MIX, t)
    n_tiles = t // tm
    tile, prev, nxt = _mixer_specs(t, a, tm)
    hb = tm // HALO
    dy_tile = pl.BlockSpec((tm, 2 * a), lambda i: (i, 0))
    dy_next = pl.BlockSpec((HALO, a), lambda i: (jnp.minimum((i + 1) * hb, t // HALO - 1), 1))

    def body(z_ref, pc_ref, ph_ref, nb_ref, nc_ref, nh_ref, dy_ref, ndy_ref, lng_ref, lnb_ref, ws_ref, bb_ref, cw_ref,
             gg_ref, dz_ref, dlng_ref, dlnb_ref, dws_ref, dbb_ref, dcw_ref, dgg_ref, mixed_ref, vln_ref, dmix_ref,
             dvln_ref):
        i = pl.program_id(0)

        @pl.when(i == 0)
        def _():
            for ref in (dlng_ref, dlnb_ref, dws_ref, dbb_ref, dcw_ref, dgg_ref):
                ref[...] = jnp.zeros(ref.shape, F32)

        lng = lng_ref[...]
        zu = z_ref[:, 0:a].astype(F32)
        zv = z_ref[:, a:2 * a].astype(F32)
        u, thu, thv, rs, vhat = _group_a_fwd(zu, zv, lng, lnb_ref[...], ws_ref, bb_ref, mixed_ref, vln_ref)
        mixed = mixed_ref[...]
        ya = u * mixed
        ra = lax.rsqrt(jnp.mean(ya * ya, axis=-1, keepdims=True) + RMS_EPS)
        da = dy_ref[:, 0:a].astype(F32)
        yah = ya * ra
        dgg_ref[:, 0:a] += jnp.sum(da * yah, axis=0, keepdims=True)
        ga = da * gg_ref[:, 0:a]
        dya = ra * (ga - yah * jnp.mean(ga * yah, axis=-1, keepdims=True))
        dz_ref[:, 0:a] = (dya * mixed * _gelu_grad(zu, thu)).astype(BF16)
        dmix_ref[...] = dya * u
        for h in range(HEADS):
            w = _masked_ws(ws_ref, h).astype(BF16)
            dw = jnp.zeros((BLK, BLK), F32)
            db = jnp.zeros((BLK, hd), F32)
            for b in range(tm // BLK):
                rows, cols = pl.ds(b * BLK, BLK), pl.ds(h * hd, hd)
                dm = dmix_ref[rows, cols]
                dmb = dm.astype(BF16)
                db = db + dm
                dw = dw + lax.dot_general(dmb, vln_ref[rows, cols].astype(BF16), (NT, ((), ())),
                                          preferred_element_type=F32)
                dvln_ref[rows, cols] = lax.dot_general(w, dmb, (TN, ((), ())), preferred_element_type=F32)
            dws_ref[h] += dw
            dbb_ref[h] += db
        dvln = dvln_ref[...]
        dlng_ref[...] += jnp.sum(dvln * vhat, axis=0, keepdims=True)
        dlnb_ref[...] += jnp.sum(dvln, axis=0, keepdims=True)
        dvh = dvln * lng
        dv = rs * (dvh - jnp.mean(dvh, axis=-1, keepdims=True) - vhat * jnp.mean(dvh * vhat, axis=-1, keepdims=True))
        dz_ref[:, a:2 * a] = (dv * _gelu_grad(zv, thv)).astype(BF16)

        w0, w1, w2 = cw_ref[0:1, :], cw_ref[1:2, :], cw_ref[2:3, :]
        ggb = gg_ref[:, a:2 * a]
        zb = z_ref[:, 2 * a:3 * a].astype(F32)
        zc = z_ref[:, 3 * a:4 * a].astype(F32)
        zh = z_ref[:, 4 * a:5 * a].astype(F32)
        q = zc * zh
        qp = jnp.where(i > 0, pc_ref[...].astype(F32) * ph_ref[...].astype(F32), 0.0)
        qm1 = _shift_down(q, 1, [qp[HALO - 1:HALO]])
        qm2 = _shift_down(q, 2, [qp[HALO - 2:HALO - 1], qp[HALO - 1:HALO]])
        cv = w0 * qm2 + w1 * qm1 + w2 * q

        def conv_out_grad(zb_, cv_, dout_):
            yb = zb_ * cv_
            rb = lax.rsqrt(jnp.mean(yb * yb, axis=-1, keepdims=True) + RMS_EPS)
            ybh = yb * rb
            gb = dout_ * ggb
            dyb = rb * (gb - ybh * jnp.mean(gb * ybh, axis=-1, keepdims=True))
            return dyb * zb_, dyb * cv_, ybh

        db_out = dy_ref[:, a:2 * a].astype(F32)
        g, dzb, ybh = conv_out_grad(zb, cv, db_out)
        dgg_ref[:, a:2 * a] += jnp.sum(db_out * ybh, axis=0, keepdims=True)
        dz_ref[:, 2 * a:3 * a] = dzb.astype(BF16)
        qn = nc_ref[...].astype(F32) * nh_ref[...].astype(F32)
        zbn = nb_ref[...].astype(F32)
        cvn = w0 * _shift_down(qn, 2, [q[tm - 2:tm - 1], q[tm - 1:tm]]) + w1 * _shift_down(qn, 1, [q[tm - 1:tm]]) + w2 * qn
        gn, _, _ = conv_out_grad(zbn, cvn, ndy_ref[...].astype(F32))
        gn = jnp.where(i < n_tiles - 1, gn, 0.0)
        dq = w2 * g + w1 * _shift_up(g, 1, [gn[0:1]]) + w0 * _shift_up(g, 2, [gn[0:1], gn[1:2]])
        dz_ref[:, 3 * a:4 * a] = (dq * zh).astype(BF16)
        dz_ref[:, 4 * a:5 * a] = (dq * zc).astype(BF16)
        dcw_ref[0:1, :] += jnp.sum(g * qm2, axis=0, keepdims=True)
        dcw_ref[1:2, :] += jnp.sum(g * qm1, axis=0, keepdims=True)
        dcw_ref[2:3, :] += jnp.sum(g * q, axis=0, keepdims=True)

        @pl.when(i == n_tiles - 1)
        def _():
            for h in range(HEADS):
                dbb_ref[h] = jnp.broadcast_to(jnp.sum(dbb_ref[h], axis=1, keepdims=True), (BLK, hd))
                dws_ref[h] = _masked_ws(dws_ref, h)

    full = lambda shape: pl.BlockSpec(tuple(shape), lambda i: (0,) * len(shape))
    out_shapes = (_sds((t, 5 * a), BF16), _sds((1, a), F32), _sds((1, a), F32), _sds(w_spatial.shape, F32),
                  _sds(bb.shape, F32), _sds((8, a), F32), _sds((1, 2 * a), F32))
    return _call_hosting(
        body, "mixer_bwd", (n_tiles,), out_shapes,
        [tile, *prev, *nxt, dy_tile, dy_next, full((1, a)), full((1, a)), full(w_spatial.shape), full(bb.shape),
         full(conv_w.shape), full((1, 2 * a))],
        [tile, *[full(s.shape) for s in out_shapes[1:]]], [z, z, z, z, z, z, dy, dy, ln_g, ln_b, w_spatial, bb, conv_w, gg],
        [pltpu.VMEM((tm, a), F32)] * 4, comm)


def _all_reduce_small(pack, comm=None):
    r = pack.shape[0]
    hosted = _Hosted(comm, 1, 1)
    n_ci, n_co = len(hosted.operands), len(hosted.out_shapes)

    def body(*refs):
        in_ref, c_ins, out_ref, c_outs = refs[0], refs[1:1 + n_ci], refs[1 + n_ci], refs[2 + n_ci:2 + n_ci + n_co]
        acc_ref, recv_ref, send_sems, recv_sems = refs[2 + n_ci + n_co:6 + n_ci + n_co]
        sems = refs[6 + n_ci + n_co:]
        hosted.run("start", c_ins, c_outs, sems)
        x, y, c = _place()
        partners = [(x, y, 1 - c), (1 - x, y, c), (x, 1 - y, c)]
        acc_ref[0] = in_ref[...]
        for s, partner in enumerate(partners):
            cp = pltpu.make_async_remote_copy(
                src_ref=acc_ref.at[s], dst_ref=recv_ref.at[s], send_sem=send_sems.at[s], recv_sem=recv_sems.at[s],
                device_id=partner, device_id_type=MESH)
            cp.start()
            cp.wait()
            if s < 2:
                acc_ref[s + 1] = acc_ref[s] + recv_ref[s]
            else:
                out_ref[...] = acc_ref[s] + recv_ref[s]
        for stage in ("mid1", "mid2", "finish"):
            hosted.run(stage, c_ins, c_outs, sems)

    vmem = pl.BlockSpec(memory_space=pltpu.VMEM)
    res = pl.pallas_call(
        body, name="all_reduce_small", out_shape=tuple([_sds(pack.shape, F32)] + hosted.out_shapes),
        in_specs=[vmem] + hosted.in_specs, out_specs=tuple([vmem] + hosted.out_specs),
        input_output_aliases=hosted.aliases,
        scratch_shapes=[pltpu.VMEM((3, r, 128), F32), pltpu.VMEM((3, r, 128), F32), pltpu.SemaphoreType.DMA((3,)),
                        pltpu.SemaphoreType.DMA((3,))] + hosted.scratch,
        compiler_params=pltpu.CompilerParams(vmem_limit_bytes=VMEM_LIMIT_V7X),
    )(pack, *hosted.operands)
    return res[0], list(res[1:])


def _adamw_math(w, g, m, v):
    m = ADAM_B1 * m + (1.0 - ADAM_B1) * g
    v = ADAM_B2 * v + (1.0 - ADAM_B2) * (g * g)
    m_hat = m / (1.0 - ADAM_B1 ** ADAM_STEP)
    v_hat = v / (1.0 - ADAM_B2 ** ADAM_STEP)
    delta = -ADAM_LR * (m_hat / (jnp.sqrt(v_hat) + ADAM_EPS) + ADAM_WD * w)
    return delta, m, v


def _adamw_big(name, land, w, m, v, comm=None):
    nl, n_slots, r, c = land.shape
    tr = max(8, min(r, (256 * 640) // c // 8 * 8))
    while r % tr:
        tr -= 8
    grid = (nl, r // tr)
    hosted = _Hosted(comm, 4, 4)
    n_ci, n_co = len(hosted.operands), len(hosted.out_shapes)

    def body(*refs):
        land_ref, w_ref, m_ref, v_ref = refs[:4]
        c_ins = refs[4:4 + n_ci]
        g_out, d_out, m_out, v_out = refs[4 + n_ci:8 + n_ci]
        c_outs = refs[8 + n_ci:8 + n_ci + n_co]
        sems = refs[8 + n_ci + n_co:]

        def compute():
            g = land_ref[0].astype(F32)
            for s in range(1, n_slots):
                g = g + land_ref[s].astype(F32)
            delta, mn, vn = _adamw_math(w_ref[...], g, m_ref[...], v_ref[...])
            g_out[...] = g
            d_out[...] = delta
            m_out[...] = mn
            v_out[...] = vn

        hosted.wrap(grid, compute, c_ins, c_outs, sems)

    blk = pl.BlockSpec((None, tr, c), lambda l, i: (l, i, 0))
    res = pl.pallas_call(
        body, name=name, grid=grid, out_shape=tuple([_sds((nl, r, c), F32)] * 4 + hosted.out_shapes),
        in_specs=[pl.BlockSpec((None, n_slots, tr, c), lambda l, i: (l, 0, i, 0)), blk, blk, blk] + hosted.in_specs,
        out_specs=tuple([blk] * 4 + hosted.out_specs), input_output_aliases=hosted.aliases,
        scratch_shapes=hosted.scratch, compiler_params=_cparams(2))(land, w, m, v, *hosted.operands)
    return list(res[:4]), list(res[4:])


def _adamw_small(g, w, m, v):
    def body(g_ref, w_ref, m_ref, v_ref, d_out, m_out, v_out):
        delta, mn, vn = _adamw_math(w_ref[...], g_ref[...], m_ref[...], v_ref[...])
        d_out[...] = delta
        m_out[...] = mn
        v_out[...] = vn

    return pl.pallas_call(body, name="adamw_small", out_shape=tuple([_sds(g.shape, F32)] * 3),
                          compiler_params=pltpu.CompilerParams(vmem_limit_bytes=VMEM_LIMIT_V7X))(g, w, m, v)


def _rows(a):
    return a.reshape(-1, 128)


BIG = ["w_in", "w_out", "w_gate", "w_up", "w_down"]
AG_HOSTS = {
    ("norm1", 0): [("w_in", 0), ("conv_w", 0)],
    ("mm_in", 0): [("w_out", 0), ("w_gate", 0, 0, 2)], ("mixer", 0): [("w_gate", 0, 1, 2)],
    ("mm_out", 0): [("w_up", 0, 0, 2)], ("norm2", 0): [("w_up", 0, 1, 2)],
    ("mm_swiglu", 0): [("w_down", 0), ("w_in", 1), ("w_out", 1)], ("mm_down", 0): [("w_gate", 1)],
    ("mm_in", 1): [("w_up", 1)], ("mm_swiglu", 1): [("w_down", 1)],
}


def kernel(x, norm1_g, w_in, gmlp_ln_g, gmlp_ln_b, w_spatial, b_spatial, conv_w, group_norm_g, w_out, norm2_g, w_gate, w_up, w_down, final_norm_g, loss_target, m_norm1_g, m_w_in, m_gmlp_ln_g, m_gmlp_ln_b, m_w_spatial, m_b_spatial, m_conv_w, m_group_norm_g, m_w_out, m_norm2_g, m_w_gate, m_w_up, m_w_down, m_final_norm_g, v_norm1_g, v_w_in, v_gmlp_ln_g, v_gmlp_ln_b, v_w_spatial, v_b_spatial, v_conv_w, v_group_norm_g, v_w_out, v_norm2_g, v_w_gate, v_w_up, v_w_down, v_final_norm_g):
    nl = N_LAYERS
    t, d = x.shape[1], x.shape[2]
    a = d // 2
    hd = a // HEADS
    xin = x.reshape(t, d)
    target = loss_target.reshape(t, d)
    me = _index(_place())

    tr = lambda w: jnp.transpose(w, (0, 2, 1))
    big = {"w_in": w_in, "w_out": w_out, "w_gate": tr(w_gate), "w_up": tr(w_up), "w_down": w_down}
    big_m = {"w_in": m_w_in, "w_out": m_w_out, "w_gate": tr(m_w_gate), "w_up": tr(m_w_up), "w_down": m_w_down}
    big_v = {"w_in": v_w_in, "w_out": v_w_out, "w_gate": tr(v_w_gate), "w_up": tr(v_w_up), "w_down": v_w_down}
    block = {k: big[k].shape[1:] for k in BIG}
    view = {k: _cols_view(block[k][1]) if k == "w_in" else _rows_view(block[k][0]) for k in BIG}
    full_shape = {k: (block[k][0], N_DEV * block[k][1]) if k == "w_in" else (N_DEV * block[k][0], block[k][1])
                  for k in BIG}

    weights = {}
    shards = {(k, l): big[k][l].astype(BF16) for k in BIG for l in range(nl)}

    def ag_spec(k, l, part=0, n_parts=1):
        if k == "conv_w":
            return (conv_w, _sds((N_DEV, *conv_w.shape), F32), _SLOT_WHOLE, (0,), None)
        halves = (_cols_halves(*block[k], part, n_parts) if k == "w_in" else _rows_halves(block[k][0], part, n_parts))
        return (shards[(k, l)], _sds(full_shape[k], BF16), halves, (0, 1), weights.get((k, l)))

    bb = jnp.broadcast_to(b_spatial[..., None], (nl, HEADS, BLK, hd))

    def hosted(name, l):
        keys = AG_HOSTS.get((name, l), [])
        return keys, ([_ag_piece([ag_spec(*key) for key in keys])] if keys else None)

    def landed(keys, couts):
        for key, arr in zip(keys, couts):
            weights[key[:2]] = arr

    saved = []
    xl = xin
    for l in range(nl):
        keys, comm = hosted("norm1", l)
        h, couts = _rmsnorm_fwd(xl, norm1_g[l:l + 1], comm)
        landed(keys, couts)
        if l == 0:
            conv_full = jnp.transpose(weights[("conv_w", 0)], (1, 2, 0, 3)).reshape(nl, 3, a)
        keys, comm = hosted("mm_in", l)
        z, couts = _mm_in(h, weights[("w_in", l)], comm)
        landed(keys, couts)
        keys, comm = hosted("mixer", l)
        y, couts = _mixer_fwd(z, gmlp_ln_g[l:l + 1], gmlp_ln_b[l:l + 1], w_spatial[l], bb[l], conv_full[l],
                              group_norm_g[l:l + 1], comm)
        landed(keys, couts)
        keys, comm = hosted("mm_out", l)
        x1, couts = _mm_out(y, weights[("w_out", l)], xl, comm)
        landed(keys, couts)
        keys, comm = hosted("norm2", l)
        h2, couts = _rmsnorm_fwd(x1, norm2_g[l:l + 1], comm)
        landed(keys, couts)
        keys, comm = hosted("mm_swiglu", l)
        (act, dact_dgate, dact_dup), couts = _mm_swiglu(h2, weights[("w_gate", l)], weights[("w_up", l)], comm)
        landed(keys, couts)
        keys, comm = hosted("mm_down", l)
        x2, couts = _mm_down(act, weights[("w_down", l)], x1, comm)
        landed(keys, couts)
        saved.append(dict(x=xl, h=h, z=z, y=y, x1=x1, h2=h2, dact_dgate=dact_dgate, dact_dup=dact_dup, act=act))
        xl = x2

    dx, dxb, d_final_g, loss_part = _loss_head(xl, final_norm_g.reshape(1, d), target)
    small = [None] * nl
    core = lax.axis_index("c").astype(jnp.int32).reshape(1)
    stage_shape = {k: _sds((N_CHIPS, *block[k]), BF16) for k in BIG}
    land_shape = {k: _sds((nl, N_CHIPS, *block[k]), BF16) for k in BIG}
    grads = [dict() for _ in range(nl)]
    stages = [dict() for _ in range(nl)]
    sums = [dict() for _ in range(nl)]
    lands = {k: None for k in BIG}

    def core_job(l, keys):
        def sink(outs):
            stages[l].update(zip(keys, outs))
        return _rs_core_piece([(grads[l][k], stage_shape[k], view[k]) for k in keys]), sink

    def chip_job(l, items):
        keys = [item[0] for item in items]

        def rows(k, p0, p1, n_parts):
            per = block[k][0] // n_parts
            return (p0 * per, (p1 - p0) * per)

        def sink(outs):
            lands.update(zip(keys, outs))
        return _rs_chip_piece([(sums[l][k], land_shape[k], rows(k, p0, p1, n_parts), lands[k])
                               for k, p0, p1, n_parts in items], l), sink

    def add_up(l, keys):
        for k in keys:
            sums[l][k] = _chip_sums(f"chip_sums_{k}", grads[l][k], stages[l][k], k == "w_in", core)

    def host(*jobs):
        def deliver(couts):
            i = 0
            for piece, sink in jobs:
                n_out = len(piece.out_shapes)
                sink(couts[i:i + n_out])
                i += n_out
        return [piece for piece, _ in jobs], deliver

    whole = lambda k: (k, 0, 1, 1)
    for l in reversed(range(nl)):
        s = saved[l]
        wi, wo, wgt, wut, wd = [weights[(k, l)] for k in BIG]
        later = l + 1 < nl
        comm, deliver = host(chip_job(l + 1, [("w_in", 0, 1, 2)])) if later else host()
        (grads[l]["w_down"],), couts = _mm_dw("mm_dw_down", [s["act"]], dxb, 2816, 1024, comm)
        deliver(couts)
        comm, deliver = (host(core_job(l, ["w_down"]), chip_job(l + 1, [("w_in", 1, 2, 2)])) if later
                         else host(core_job(l, ["w_down"])))
        (dgate, dup), couts = _mm_dact(dxb, wd, s["dact_dgate"], s["dact_dup"], comm)
        deliver(couts)
        add_up(l, ["w_down"])
        comm, deliver = host(chip_job(l, [whole("w_down")]))
        (grads[l]["w_gate"], grads[l]["w_up"]), couts = _mm_dw("mm_dw_gate_up", [dgate, dup], s["h2"], 1408, 1024, comm)
        deliver(couts)
        comm, deliver = host(core_job(l, ["w_gate", "w_up"]))
        dh2, couts = _mm_dh2(dgate, dup, wgt, wut, comm)
        deliver(couts)
        add_up(l, ["w_gate", "w_up"])
        dx1, dx1b, d_n2 = _rmsnorm_bwd(s["x1"], norm2_g[l:l + 1], dh2, dx)
        comm, deliver = host(chip_job(l, [("w_gate", 0, 1, 4)]))
        dy, couts = _mm_dy(dx1b, wo, comm)
        deliver(couts)
        comm, deliver = host(chip_job(l, [("w_gate", 1, 2, 4)]))
        (grads[l]["w_out"],), couts = _mm_dw("mm_dw_out", [s["y"]], dx1b, 1024, 1024, comm)
        deliver(couts)
        comm, deliver = host(chip_job(l, [("w_gate", 2, 4, 4)]))
        (dz, d_lng, d_lnb, d_ws, d_bb, d_cw, d_gg), couts = _mixer_bwd(
            s["z"], dy, gmlp_ln_g[l:l + 1], gmlp_ln_b[l:l + 1], w_spatial[l], bb[l], conv_full[l], group_norm_g[l:l + 1],
            comm)
        deliver(couts)
        comm, deliver = host(chip_job(l, [("w_up", 0, 3, 4)]), core_job(l, ["w_out"]))
        (grads[l]["w_in"],), couts = _mm_dw("mm_dw_in", [s["h"]], dz, 2048, 1024, comm)
        deliver(couts)
        add_up(l, ["w_out"])
        comm, deliver = host(chip_job(l, [("w_up", 3, 4, 4), whole("w_out")]), core_job(l, ["w_in"]))
        dh, couts = _mm_dh(dz, wi, comm)
        deliver(couts)
        add_up(l, ["w_in"])
        dx, dxb, d_n1 = _rmsnorm_bwd(s["x"], norm1_g[l:l + 1], dh, dx1)
        small[l] = dict(norm1_g=d_n1, gmlp_ln_g=d_lng, gmlp_ln_b=d_lnb, w_spatial=d_ws, b_spatial=d_bb[:, :, 0],
                        group_norm_g=d_gg, norm2_g=d_n2, conv_w=d_cw[0:3])
    grad_x = dx.reshape(x.shape)

    rep = ["norm1_g", "gmlp_ln_g", "gmlp_ln_b", "w_spatial", "b_spatial", "group_norm_g", "norm2_g"]
    rep_w = dict(norm1_g=norm1_g, gmlp_ln_g=gmlp_ln_g, gmlp_ln_b=gmlp_ln_b, w_spatial=w_spatial, b_spatial=b_spatial,
                 group_norm_g=group_norm_g, norm2_g=norm2_g)
    rep_m = dict(norm1_g=m_norm1_g, gmlp_ln_g=m_gmlp_ln_g, gmlp_ln_b=m_gmlp_ln_b, w_spatial=m_w_spatial,
                 b_spatial=m_b_spatial, group_norm_g=m_group_norm_g, norm2_g=m_norm2_g)
    rep_v = dict(norm1_g=v_norm1_g, gmlp_ln_g=v_gmlp_ln_g, gmlp_ln_b=v_gmlp_ln_b, w_spatial=v_w_spatial,
                 b_spatial=v_b_spatial, group_norm_g=v_group_norm_g, norm2_g=v_norm2_g)
    parts = [_rows(jnp.stack([small[l][k].reshape(rep_w[k].shape[1:]) for l in range(nl)])) for k in rep]
    parts.append(_rows(d_final_g))
    parts.append(_rows(jnp.stack([small[l]["conv_w"] for l in range(nl)])))
    parts.append(jnp.broadcast_to(loss_part, (8, 128)))
    sizes = [p.shape[0] for p in parts]
    comm, deliver = host(chip_job(0, [whole("w_in")]))
    total, couts = _all_reduce_small(jnp.concatenate(parts, axis=0), comm)
    deliver(couts)
    offs = [0]
    for n in sizes:
        offs.append(offs[-1] + n)
    pieces = [total[offs[i]:offs[i + 1]] for i in range(len(parts))]
    loss = pieces[-1][0, 0]
    conv_g_full = pieces[-2].reshape(nl, 3, N_DEV, a // N_DEV)
    conv_g = lax.dynamic_index_in_dim(conv_g_full, me, axis=2, keepdims=False)
    n_rep = offs[len(rep) + 1]
    pad = jnp.zeros((2, 128), F32)

    def small_pack(named, final, conv):
        return jnp.concatenate([_rows(named[k]) for k in rep] + [_rows(final), _rows(conv), pad], axis=0)

    g_small = jnp.concatenate([total[:n_rep], _rows(conv_g), pad], axis=0)
    d_small, m_small, v_small = _adamw_small(
        g_small, small_pack(rep_w, final_norm_g, conv_w), small_pack(rep_m, m_final_norm_g, m_conv_w),
        small_pack(rep_v, v_final_norm_g, v_conv_w))

    def unpack(packed):
        out = {k: packed[offs[i]:offs[i + 1]].reshape(rep_w[k].shape) for i, k in enumerate(rep)}
        out["final_norm_g"] = packed[offs[len(rep)]:n_rep].reshape(final_norm_g.shape)
        out["conv_w"] = packed[n_rep:n_rep + 6].reshape(conv_w.shape)
        return out

    res = {"grad": unpack(g_small), "delta": unpack(d_small), "m": unpack(m_small), "v": unpack(v_small)}

    for k in BIG:
        outs, _ = _adamw_big(f"adamw_{k}", lands[k], big[k], big_m[k], big_v[k])
        if k in ("w_gate", "w_up"):
            outs = [tr(o) for o in outs]
        res["grad"][k], res["delta"][k], res["m"][k], res["v"][k] = outs

    order = ["norm1_g", "w_in", "gmlp_ln_g", "gmlp_ln_b", "w_spatial", "b_spatial", "conv_w", "group_norm_g", "w_out",
             "norm2_g", "w_gate", "w_up", "w_down", "final_norm_g"]
    return (loss, grad_x, *[res["grad"][k] for k in order], *[res["delta"][k] for k in order],
            *[res["m"][k] for k in order], *[res["v"][k] for k in order])
```

```python
import functools
import math
import operator

import jax
import jax.numpy as jnp
from jax import lax
from jax.experimental import pallas as pl
from jax.experimental.pallas import tpu as pltpu

F32 = jnp.float32
BF16 = jnp.bfloat16
MESH = pl.DeviceIdType.MESH

N_DEV = 8
N_LAYERS = 2
HEADS = 8
BLK = 128
CHUNK = 64
HALO = 16
RMS_EPS = 1e-6
LN_EPS = 1e-5
ADAM_LR, ADAM_B1, ADAM_B2, ADAM_EPS, ADAM_WD, ADAM_STEP = 0.001, 0.9, 0.999, 1e-8, 0.01, 10
GELU_C = math.sqrt(2.0 / math.pi)
GELU_A = 0.044715

VMEM_LIMIT_V7X = 56 * 1024 * 1024
_TM = 1024
_TN = 1024
_TT = 1024
_TM_MIX = 256
_TM_NORM = 512


def _cparams(n_axes):
    return pltpu.CompilerParams(dimension_semantics=("arbitrary",) * n_axes, vmem_limit_bytes=VMEM_LIMIT_V7X)


def _sds(shape, dtype):
    return jax.ShapeDtypeStruct(tuple(shape), dtype)


def _place():
    return lax.axis_index("x"), lax.axis_index("y"), lax.axis_index("c")


def _index(place):
    return 4 * place[0] + 2 * place[1] + place[2]


class _Piece:
    def __init__(self, operands, out_shapes, aliases, n_sems, start, finish, mid1=None, mid2=None):
        self.operands, self.out_shapes, self.aliases, self.n_sems = list(operands), list(out_shapes), dict(aliases), n_sems
        nothing = lambda ctx: None
        self.start, self.mid1, self.mid2, self.finish = start, mid1 or nothing, mid2 or nothing, finish


class _Ctx:
    def __init__(self, ins, outs, sems, offs):
        self.ins, self.outs, self.sems = ins, outs, sems
        self.o_in, self.o_out, self.o_send, self.o_recv, self.o_loc = offs

    def inp(self, i):
        return self.ins[self.o_in + i]

    def out(self, i):
        return self.outs[self.o_out + i]

    def send(self, k):
        return self.sems[0].at[self.o_send + k]

    def recv(self, k):
        return self.sems[1].at[self.o_recv + k]

    def local(self, k):
        return self.sems[2].at[self.o_loc + k]


class _Hosted:
    def __init__(self, pieces, n_in_before, n_out_before):
        self.pieces = [p for p in (pieces or []) if p is not None]
        self.operands, self.out_shapes, self.aliases, self.offs = [], [], {}, []
        counts = [0, 0, 0]
        for p in self.pieces:
            self.offs.append((len(self.operands), len(self.out_shapes), *counts))
            for i, j in p.aliases.items():
                self.aliases[n_in_before + len(self.operands) + i] = n_out_before + len(self.out_shapes) + j
            self.operands += p.operands
            self.out_shapes += p.out_shapes
            counts = [c + n for c, n in zip(counts, p.n_sems)]
        hbm = pl.BlockSpec(memory_space=pl.ANY)
        self.in_specs = [hbm] * len(self.operands)
        self.out_specs = [hbm] * len(self.out_shapes)
        self.scratch = [pltpu.SemaphoreType.DMA((max(c, 1),)) for c in counts] if self.pieces else []

    def run(self, stage, ins, outs, sems):
        for p, offs in zip(self.pieces, self.offs):
            getattr(p, stage)(_Ctx(ins, outs, sems, offs))

    def wrap(self, grid, compute, ins, outs, sems):
        if not self.pieces:
            compute()
            return
        n_steps = math.prod(grid)
        lin = 0
        for ax, g in enumerate(grid):
            lin = lin * g + pl.program_id(ax)
        pl.when(lin == 0)(lambda: self.run("start", ins, outs, sems))
        compute()
        pl.when(lin == n_steps // 2)(lambda: self.run("mid1", ins, outs, sems))
        pl.when(lin == max(n_steps - 3, n_steps // 2))(lambda: self.run("mid2", ins, outs, sems))
        pl.when(lin == n_steps - 1)(lambda: self.run("finish", ins, outs, sems))


def _cols_view(width):
    return lambda ref, p: ref.at[:, pl.ds(pl.multiple_of(p * width, 128), width)]


def _rows_view(height):
    return lambda ref, p: ref.at[pl.ds(pl.multiple_of(p * height, 16), height), :]


def _cols_halves(rows, width, part, n_parts):
    hr = rows // n_parts // 2
    at = lambda h: pl.ds(part * 2 * hr + h * hr, hr)
    return (lambda ref, p, h: ref.at[at(h), pl.ds(pl.multiple_of(p * width, 128), width)],
            lambda ref, h: ref.at[at(h), :], 2)


def _rows_halves(height, part, n_parts):
    hh = height // n_parts // 2
    return (lambda ref, p, h: ref.at[pl.ds(pl.multiple_of(p * height + part * 2 * hh + h * hh, 16), hh), :],
            lambda ref, h: ref.at[pl.ds(part * 2 * hh + h * hh, hh), :], 2)


_SLOT_WHOLE = (lambda ref, p, h: ref.at[p], lambda ref, h: ref, 1)


def _ag_piece(specs):
    units = [(a, h) for a, s in enumerate(specs) for h in s[3]]

    def plan(ctx):
        x, y, c = _place()
        me, sib, xn, yn, dg = (x, y, c), (x, y, 1 - c), (1 - x, y, c), (x, 1 - y, c), (1 - x, 1 - y, c)

        def copy(u, k, block, to, from_shard=False):
            a, h = units[u]
            dst_of, src_of, _ = specs[a][2]
            dst = dst_of(ctx.out(a), _index(block), h)
            return pltpu.make_async_remote_copy(
                src_ref=src_of(ctx.inp(a), h) if from_shard else dst, dst_ref=dst, send_sem=ctx.send(7 * u + k),
                recv_sem=ctx.recv(7 * u + k), device_id=to, device_id_type=MESH)

        def local(u):
            a, h = units[u]
            dst_of, src_of, _ = specs[a][2]
            return pltpu.make_async_copy(src_of(ctx.inp(a), h), dst_of(ctx.out(a), _index(me), h), ctx.local(u))

        def relay(u):
            return copy(u, 3, xn, yn) if units[u][1] % 2 == 0 else copy(u, 3, yn, xn)

        return me, sib, xn, yn, dg, c, copy, local, relay

    def start(ctx):
        me, sib, xn, yn, dg, c, copy, local, relay = plan(ctx)
        for u in range(len(units)):
            local(u).start()
            for k, to in enumerate((sib, xn, yn)):
                copy(u, k, me, to, from_shard=True).start()

    def mid1(ctx):
        me, sib, xn, yn, dg, c, copy, local, relay = plan(ctx)
        for u in range(len(units)):
            copy(u, 1, xn, me).wait_recv()
            copy(u, 2, yn, me).wait_recv()
            relay(u).start()
            copy(u, 4, xn, sib).start()
            copy(u, 5, yn, sib).start()

    def mid2(ctx):
        me, sib, xn, yn, dg, c, copy, local, relay = plan(ctx)
        for u in range(len(units)):
            copy(u, 3, dg, me).wait_recv()
            copy(u, 6, dg, sib).start()

    def finish(ctx):
        me, sib, xn, yn, dg, c, copy, local, relay = plan(ctx)
        other = lambda place: (place[0], place[1], 1 - c)
        for u in range(len(units)):
            for k, block in ((0, sib), (4, other(xn)), (5, other(yn)), (6, other(dg))):
                copy(u, k, block, me).wait_recv()
        for u in range(len(units)):
            for k, to in enumerate((sib, xn, yn)):
                copy(u, k, me, to, from_shard=True).wait_send()
            relay(u).wait_send()
            for k, block in ((4, xn), (5, yn), (6, dg)):
                copy(u, k, block, sib).wait_send()
            local(u).wait()

    n_u = len(units)
    operands, aliases = [s[0] for s in specs], {}
    for a, spec in enumerate(specs):
        if spec[4] is not None:
            aliases[len(operands)] = a
            operands.append(spec[4])
    return _Piece(operands, [s[1] for s in specs], aliases, (7 * n_u, 7 * n_u, n_u), start, finish, mid1, mid2)


N_CHIPS = 4


def _rs_core_piece(specs):
    n = len(specs)

    def copies(ctx):
        x, y, c = _place()
        out = []
        for a in range(n):
            for q in range(N_CHIPS):
                out.append(pltpu.make_async_remote_copy(
                    src_ref=specs[a][2](ctx.inp(a), 2 * q + (1 - c)), dst_ref=ctx.out(a).at[q],
                    send_sem=ctx.send(N_CHIPS * a + q), recv_sem=ctx.recv(N_CHIPS * a + q), device_id=(x, y, 1 - c),
                    device_id_type=MESH))
        return out

    def start(ctx):
        for cp in copies(ctx):
            cp.start()

    def finish(ctx):
        for cp in copies(ctx):
            cp.wait_recv()
            cp.wait_send()

    return _Piece([s[0] for s in specs], [s[1] for s in specs], {}, (N_CHIPS * n, N_CHIPS * n, 0), start, finish)


def _rs_chip_piece(specs, layer):
    n = len(specs)
    hops = [(1, 0), (0, 1), (1, 1)]

    def copies(ctx):
        x, y, c = _place()
        mine = 2 * x + y
        out = []
        for a in range(n):
            rows = pl.ds(*specs[a][2])
            sums, land = ctx.inp(a), ctx.out(a)
            out.append((pltpu.make_async_copy(sums.at[mine, rows], land.at[layer, mine, rows], ctx.local(a)), None))
            for j, (dx, dy) in enumerate(hops):
                px, py = x ^ dx, y ^ dy
                peer = 2 * px + py
                send = pltpu.make_async_remote_copy(
                    src_ref=sums.at[peer, rows], dst_ref=land.at[layer, mine, rows], send_sem=ctx.send(3 * a + j),
                    recv_sem=ctx.recv(3 * a + j), device_id=(px, py, c), device_id_type=MESH)
                recv = pltpu.make_async_remote_copy(
                    src_ref=sums.at[peer, rows], dst_ref=land.at[layer, peer, rows], send_sem=ctx.send(3 * a + j),
                    recv_sem=ctx.recv(3 * a + j), device_id=(px, py, c), device_id_type=MESH)
                out.append((send, recv))
        return out

    def start(ctx):
        for send, _ in copies(ctx):
            send.start()

    def finish(ctx):
        for send, recv in copies(ctx):
            if recv is None:
                send.wait()
            else:
                recv.wait_recv()
                send.wait_send()

    operands, aliases = [s[0] for s in specs], {}
    for a, spec in enumerate(specs):
        if spec[3] is not None:
            aliases[len(operands)] = a
            operands.append(spec[3])
    return _Piece(operands, [s[1] for s in specs], aliases, (3 * n, 3 * n, n), start, finish)


def _chip_sums(name, grad, stage, by_cols, core):
    _, r, c = stage.shape
    tr = r
    while tr * c > 1024 * 1024 or r % tr or tr % 16:
        tr -= 16
    n_t = r // tr

    def body(core_ref, g_ref, s_ref, o_ref):
        o_ref[...] = (g_ref[...].astype(F32) + s_ref[...].astype(F32)).astype(BF16)

    if by_cols:
        gspec = pl.BlockSpec((tr, c), lambda q, i, core_ref: (i, 2 * q + core_ref[0]))
    else:
        gspec = pl.BlockSpec((tr, c), lambda q, i, core_ref: ((2 * q + core_ref[0]) * n_t + i, 0))
    sspec = pl.BlockSpec((None, tr, c), lambda q, i, core_ref: (q, i, 0))
    return pl.pallas_call(
        body, name=name, out_shape=_sds(stage.shape, BF16),
        grid_spec=pltpu.PrefetchScalarGridSpec(num_scalar_prefetch=1, grid=(N_CHIPS, n_t), in_specs=[gspec, sspec],
                                               out_specs=sspec),
        compiler_params=_cparams(2))(core, grad, stage)


def _call_hosting(body, name, grid, out_shapes, in_specs, out_specs, operands, scratch, comm):
    n_in, n_out, n_scr = len(operands), len(out_shapes), len(scratch)
    hosted = _Hosted(comm, n_in, n_out)
    n_ci, n_co = len(hosted.operands), len(hosted.out_shapes)

    def hosting_body(*refs):
        ins, rest = refs[:n_in], refs[n_in:]
        c_ins, rest = rest[:n_ci], rest[n_ci:]
        outs, rest = rest[:n_out], rest[n_out:]
        c_outs, rest = rest[:n_co], rest[n_co:]
        hosted.wrap(grid, lambda: body(*ins, *outs, *rest[:n_scr]), c_ins, c_outs, rest[n_scr:])

    res = pl.pallas_call(
        hosting_body, name=name, grid=grid, out_shape=tuple(list(out_shapes) + hosted.out_shapes),
        in_specs=list(in_specs) + hosted.in_specs, out_specs=tuple(list(out_specs) + hosted.out_specs),
        input_output_aliases=hosted.aliases, scratch_shapes=list(scratch) + hosted.scratch,
        compiler_params=_cparams(len(grid)))(*operands, *hosted.operands)
    return list(res[:n_out]), list(res[n_out:])


def _matmul(name, grid, nk, kaxis, pairs, dims, extras, outs, epilogue, sum_pairs, acc_shape, comm=None, split=None):
    n_p, n_e, n_o = len(pairs), len(extras), len(outs)
    n_acc = 0 if nk == 1 else (1 if sum_pairs else n_p)
    n_in = 2 * n_p + n_e
    hosted = _Hosted(comm, n_in, n_o)
    n_ci, n_co = len(hosted.operands), len(hosted.out_shapes)

    def body(*refs):
        a_refs = refs[0:2 * n_p:2]
        b_refs = refs[1:2 * n_p:2]
        e_refs = refs[2 * n_p:n_in]
        c_ins = refs[n_in:n_in + n_ci]
        o_refs = refs[n_in + n_ci:n_in + n_ci + n_o]
        c_outs = refs[n_in + n_ci + n_o:n_in + n_ci + n_o + n_co]
        acc_refs = refs[n_in + n_ci + n_o + n_co:n_in + n_ci + n_o + n_co + n_acc]
        sems = refs[n_in + n_ci + n_o + n_co + n_acc:]

        def dots():
            if sum_pairs and n_p > 1 and dims == NN:
                a_all = jnp.concatenate([a[...] for a in a_refs], axis=1)
                b_all = jnp.concatenate([b[...] for b in b_refs], axis=0)
                return [lax.dot_general(a_all, b_all, (dims, ((), ())), preferred_element_type=F32)]
            prods = [lax.dot_general(a[...], b[...], (dims, ((), ())), preferred_element_type=F32)
                     for a, b in zip(a_refs, b_refs)]
            if sum_pairs and n_p > 1:
                prods = [functools.reduce(operator.add, prods)]
            return prods

        def compute():
            if nk == 1 and split is not None:
                n_split, b_axis, n_row = split
                width = b_refs[0].shape[b_axis] // n_split
                height = a_refs[0].shape[0] // n_row
                for s in range(n_split):
                    cols = pl.ds(s * width, width)
                    for r in range(n_row):
                        rows = pl.ds(r * height, height)
                        epilogue([lax.dot_general(a[rows, :], b[cols, :] if b_axis == 0 else b[:, cols], (dims, ((), ())),
                                                  preferred_element_type=F32) for a, b in zip(a_refs, b_refs)],
                                 e_refs, o_refs, rows, cols)
                return
            if nk == 1:
                epilogue(dots(), e_refs, o_refs)
                return
            k = pl.program_id(kaxis)

            @pl.when(k == 0)
            def _():
                for acc, p in zip(acc_refs, dots()):
                    acc[...] = p

            if nk > 2:
                @pl.when((k > 0) & (k < nk - 1))
                def _():
                    for acc, p in zip(acc_refs, dots()):
                        acc[...] += p

            @pl.when(k == nk - 1)
            def _():
                epilogue([acc[...] + p for acc, p in zip(acc_refs, dots())], e_refs, o_refs)

        hosted.wrap(grid, compute, c_ins, c_outs, sems)

    operands, in_specs = [], []
    for a, a_spec, b, b_spec in pairs:
        operands += [a, b]
        in_specs += [a_spec, b_spec]
    for e, e_spec in extras:
        operands.append(e)
        in_specs.append(e_spec)
    res = pl.pallas_call(
        body, name=name, grid=grid,
        out_shape=tuple([o for o, _ in outs] + hosted.out_shapes),
        in_specs=in_specs + hosted.in_specs, out_specs=tuple([s for _, s in outs] + hosted.out_specs),
        input_output_aliases=hosted.aliases,
        scratch_shapes=[pltpu.VMEM(acc_shape, F32) for _ in range(n_acc)] + hosted.scratch,
        compiler_params=_cparams(len(grid)),
    )(*operands, *hosted.operands)
    return list(res[:n_o]), list(res[n_o:])


NN = ((1,), (0,))
NT = ((1,), (1,))
TN = ((0,), (0,))


def _tile(n, want):
    if n <= want:
        return n
    t = want // 128 * 128
    while n % t:
        t -= 128
    return t


def _silu_parts(g):
    s = 0.5 + 0.5 * jnp.tanh(0.5 * g)
    return s, g * s


def _mm_in(h, w_in, comm=None):
    t, d = h.shape
    n = w_in.shape[1]
    tm, tn = _tile(t, _TM), _tile(n, _TN)

    def epi(accs, e, o):
        o[0][...] = accs[0].astype(BF16)

    outs, couts = _matmul(
        "mm_in", (n // tn, t // tm), 1, None,
        [(h, pl.BlockSpec((tm, d), lambda j, i: (i, 0)), w_in, pl.BlockSpec((d, tn), lambda j, i: (0, j)))],
        NN, [], [(_sds((t, n), BF16), pl.BlockSpec((tm, tn), lambda j, i: (i, j)))], epi, True, None, comm)
    return outs[0], couts


def _mm_out(y, w_out, x, comm=None):
    t, m = y.shape
    d = w_out.shape[1]
    tm, tn = _tile(t, _TM), _tile(d, _TN)

    def epi(accs, e, o):
        o[0][...] = e[0][...] + accs[0]

    outs, couts = _matmul(
        "mm_out", (t // tm, d // tn), 1, None,
        [(y, pl.BlockSpec((tm, m), lambda i, j: (i, 0)), w_out, pl.BlockSpec((m, tn), lambda i, j: (0, j)))],
        NN, [(x, pl.BlockSpec((tm, tn), lambda i, j: (i, j)))],
        [(_sds((t, d), F32), pl.BlockSpec((tm, tn), lambda i, j: (i, j)))], epi, True, None, comm)
    return outs[0], couts


def _mm_swiglu(h2, wgt, wut, comm=None):
    t, d = h2.shape
    f = wgt.shape[0]
    tm, tn = _tile(t, 2 * _TM), _tile(f, 512)

    def epi(accs, e, o, rows, cols):
        g, u = accs
        s, sg = _silu_parts(g)
        o[0][rows, cols] = (sg * u).astype(BF16)
        o[1][rows, cols] = (u * (s + sg * (1.0 - s))).astype(BF16)
        o[2][rows, cols] = sg.astype(BF16)

    wspec = pl.BlockSpec((tn, d), lambda i, j: (j, 0))
    hspec = pl.BlockSpec((tm, d), lambda i, j: (i, 0))
    ospec = pl.BlockSpec((tm, tn), lambda i, j: (i, j))
    osh = _sds((t, f), BF16)
    outs, couts = _matmul("mm_swiglu", (t // tm, f // tn), 1, None, [(h2, hspec, wgt, wspec), (h2, hspec, wut, wspec)],
                          NT, [], [(osh, ospec)] * 3, epi, False, None, comm, split=(tn // 256, 0, 2))
    return outs, couts


def _mm_down(act, wd, x1, comm=None):
    t, f = act.shape
    d = wd.shape[1]
    tm, tn = _tile(t, _TM), _tile(d, _TN)
    nk = 2
    tk = f // nk

    def epi(accs, e, o):
        o[0][...] = e[0][...] + accs[0]

    outs, couts = _matmul(
        "mm_down", (t // tm, d // tn, nk), nk, 2,
        [(act, pl.BlockSpec((tm, tk), lambda i, j, k: (i, k)), wd, pl.BlockSpec((tk, tn), lambda i, j, k: (k, j)))],
        NN, [(x1, pl.BlockSpec((tm, tn), lambda i, j, k: (i, j)))],
        [(_sds((t, d), F32), pl.BlockSpec((tm, tn), lambda i, j, k: (i, j)))], epi, True, (tm, tn), comm)
    return outs[0], couts


def _mm_dact(dxb, wd, dact_dgate, dact_dup, comm=None):
    t, d = dxb.shape
    f = wd.shape[0]
    tm, tn = _tile(t, 2 * _TM), _tile(f, 512)

    def epi(accs, e, o, rows, cols):
        da = accs[0]
        o[0][rows, cols] = (da * e[0][rows, cols].astype(F32)).astype(BF16)
        o[1][rows, cols] = (da * e[1][rows, cols].astype(F32)).astype(BF16)

    bspec = pl.BlockSpec((tm, tn), lambda i, j: (i, j))
    osh = _sds((t, f), BF16)
    outs, couts = _matmul(
        "mm_dact", (t // tm, f // tn), 1, None,
        [(dxb, pl.BlockSpec((tm, d), lambda i, j: (i, 0)), wd, pl.BlockSpec((tn, d), lambda i, j: (j, 0)))],
        NT, [(dact_dgate, bspec), (dact_dup, bspec)], [(osh, bspec)] * 2, epi, True, None, comm, split=(tn // 256, 0, 2))
    return outs, couts


def _mm_dh2(dgate, dup, wgt, wut, comm=None):
    t, f = dgate.shape
    d = wgt.shape[1]
    tm, tn = _tile(t, _TM), _tile(d, _TN)
    nk = 4
    tk = f // nk

    def epi(accs, e, o):
        o[0][...] = accs[0]

    aspec = pl.BlockSpec((tm, tk), lambda i, j, k: (i, k))
    wspec = pl.BlockSpec((tk, tn), lambda i, j, k: (k, j))
    outs, couts = _matmul("mm_dh2", (t // tm, d // tn, nk), nk, 2, [(dgate, aspec, wgt, wspec), (dup, aspec, wut, wspec)],
                          NN, [], [(_sds((t, d), F32), pl.BlockSpec((tm, tn), lambda i, j, k: (i, j)))], epi, True,
                          (tm, tn), comm)
    return outs[0], couts


def _mm_dw(name, a_list, b, tmo, tno, comm=None):
    t, m = a_list[0].shape
    n = b.shape[1]
    tt = _tile(t, _TT)
    nk = t // tt
    tmo, tno = _tile(m, tmo), _tile(n, tno)

    def epi(accs, e, o):
        for acc, out in zip(accs, o):
            out[...] = acc.astype(BF16)

    aspec = pl.BlockSpec((tt, tmo), lambda i, j, k: (k, i))
    bspec = pl.BlockSpec((tt, tno), lambda i, j, k: (k, j))
    ospec = pl.BlockSpec((tmo, tno), lambda i, j, k: (i, j))
    if nk == 1:
        return _matmul(name, (m // tmo, n // tno, 1), 1, None, [(a, aspec, b, bspec) for a in a_list], TN, [],
                       [(_sds((m, n), BF16), ospec)] * len(a_list), epi, False, None, comm)
    return _matmul(name, (m // tmo, n // tno, nk), nk, 2, [(a, aspec, b, bspec) for a in a_list], TN, [],
                   [(_sds((m, n), BF16), ospec)] * len(a_list), epi, False, (tmo, tno), comm)


def _mm_dy(dxb, w_out, comm=None):
    t, d = dxb.shape
    m = w_out.shape[0]
    tm, tn = _tile(t, _TM), _tile(m, _TN)

    def epi(accs, e, o):
        o[0][...] = accs[0].astype(BF16)

    outs, couts = _matmul(
        "mm_dy", (t // tm, m // tn), 1, None,
        [(dxb, pl.BlockSpec((tm, d), lambda i, j: (i, 0)), w_out, pl.BlockSpec((tn, d), lambda i, j: (j, 0)))], NT, [],
        [(_sds((t, m), BF16), pl.BlockSpec((tm, tn), lambda i, j: (i, j)))], epi, True, None, comm)
    return outs[0], couts


def _mm_dh(dz, w_in, comm=None):
    t, n = dz.shape
    d = w_in.shape[0]
    tm, tn = _tile(t, _TM), _tile(d, _TN)
    nk = 2
    tk = n // nk

    def epi(accs, e, o):
        o[0][...] = accs[0]

    outs, couts = _matmul(
        "mm_dh", (t // tm, d // tn, nk), nk, 2,
        [(dz, pl.BlockSpec((tm, tk), lambda i, j, k: (i, k)), w_in, pl.BlockSpec((tn, tk), lambda i, j, k: (j, k)))], NT,
        [], [(_sds((t, d), F32), pl.BlockSpec((tm, tn), lambda i, j, k: (i, j)))], epi, True, (tm, tn), comm)
    return outs[0], couts


def _rmsnorm_fwd(x, g, comm=None):
    t, d = x.shape
    tm = min(_TM_NORM, t)

    def body(x_ref, g_ref, o_ref):
        xv = x_ref[...]
        rs = lax.rsqrt(jnp.mean(xv * xv, axis=-1, keepdims=True) + RMS_EPS)
        o_ref[...] = (xv * rs * g_ref[...]).astype(BF16)

    outs, couts = _call_hosting(
        body, "rmsnorm_fwd", (t // tm,), [_sds((t, d), BF16)],
        [pl.BlockSpec((tm, d), lambda i: (i, 0)), pl.BlockSpec((1, d), lambda i: (0, 0))],
        [pl.BlockSpec((tm, d), lambda i: (i, 0))], [x, g], [], comm)
    return outs[0], couts


def _rmsnorm_bwd_math(xv, g, dh):
    rs = lax.rsqrt(jnp.mean(xv * xv, axis=-1, keepdims=True) + RMS_EPS)
    xh = xv * rs
    gd = dh * g
    dx = rs * (gd - xh * jnp.mean(gd * xh, axis=-1, keepdims=True))
    return dx, jnp.sum(dh * xh, axis=0, keepdims=True)


def _rmsnorm_bwd(x, g, dh, dres):
    t, d = x.shape
    tm = min(_TM_NORM, t)

    def body(x_ref, g_ref, dh_ref, dres_ref, dx_ref, dxb_ref, dg_ref):
        dx, dg = _rmsnorm_bwd_math(x_ref[...], g_ref[...], dh_ref[...])
        dx = dx + dres_ref[...]
        dx_ref[...] = dx
        dxb_ref[...] = dx.astype(BF16)

        @pl.when(pl.program_id(0) == 0)
        def _():
            dg_ref[...] = dg

        @pl.when(pl.program_id(0) > 0)
        def _():
            dg_ref[...] += dg

    row = pl.BlockSpec((tm, d), lambda i: (i, 0))
    vec = pl.BlockSpec((1, d), lambda i: (0, 0))
    return pl.pallas_call(
        body, name="rmsnorm_bwd", grid=(t // tm,),
        out_shape=(_sds((t, d), F32), _sds((t, d), BF16), _sds((1, d), F32)),
        in_specs=[row, vec, row, row], out_specs=(row, row, vec), compiler_params=_cparams(1))(x, g, dh, dres)


def _loss_head(x, g, target):
    t, d = x.shape
    tm = min(_TM_NORM, t)

    def body(x_ref, g_ref, t_ref, dx_ref, dxb_ref, dg_ref, loss_ref):
        xv, gv = x_ref[...], g_ref[...]
        rs = lax.rsqrt(jnp.mean(xv * xv, axis=-1, keepdims=True) + RMS_EPS)
        diff = xv * rs * gv - t_ref[...]
        part = 0.5 * jnp.sum(jnp.mean(diff * diff, axis=-1, keepdims=True), axis=0, keepdims=True)
        part = jnp.broadcast_to(part, (1, 128))
        dx, dg = _rmsnorm_bwd_math(xv, gv, diff * (1.0 / d))
        dx_ref[...] = dx
        dxb_ref[...] = dx.astype(BF16)

        @pl.when(pl.program_id(0) == 0)
        def _():
            dg_ref[...] = dg
            loss_ref[...] = part

        @pl.when(pl.program_id(0) > 0)
        def _():
            dg_ref[...] += dg
            loss_ref[...] += part

    row = pl.BlockSpec((tm, d), lambda i: (i, 0))
    vec = pl.BlockSpec((1, d), lambda i: (0, 0))
    return pl.pallas_call(
        body, name="loss_head", grid=(t // tm,),
        out_shape=(_sds((t, d), F32), _sds((t, d), BF16), _sds((1, d), F32), _sds((1, 128), F32)),
        in_specs=[row, vec, row], out_specs=(row, row, vec, pl.BlockSpec((1, 128), lambda i: (0, 0))),
        compiler_params=_cparams(1))(x, g, target)


def _gelu(x):
    th = jnp.tanh(GELU_C * (x + GELU_A * x * x * x))
    return 0.5 * x * (1.0 + th), th


def _gelu_grad(x, th):
    return 0.5 * (1.0 + th) + 0.5 * x * (1.0 - th * th) * GELU_C * (1.0 + 3.0 * GELU_A * x * x)


def _masked_ws(ws_ref, h):
    i = lax.broadcasted_iota(jnp.int32, (BLK, BLK), 0) // CHUNK
    j = lax.broadcasted_iota(jnp.int32, (BLK, BLK), 1) // CHUNK
    return jnp.where(j <= i, ws_ref[h], 0.0)


def _shift_down(q, n, first_rows):
    rolled = pltpu.roll(q, n, 0)
    row = lax.broadcasted_iota(jnp.int32, q.shape, 0)
    for r, val in enumerate(first_rows):
        rolled = jnp.where(row == r, val, rolled)
    return rolled


def _shift_up(q, n, last_rows):
    tm = q.shape[0]
    rolled = pltpu.roll(q, tm - n, 0)
    row = lax.broadcasted_iota(jnp.int32, q.shape, 0)
    for r, val in enumerate(last_rows):
        rolled = jnp.where(row == tm - n + r, val, rolled)
    return rolled


def _mixer_specs(t, a, tm):
    hb = tm // HALO
    last = t // HALO - 1
    tile = pl.BlockSpec((tm, 5 * a), lambda i: (i, 0))
    prev = [pl.BlockSpec((HALO, a), functools.partial(lambda i, col: (jnp.maximum(i * hb - 1, 0), col), col=col))
            for col in (3, 4)]
    nxt = [pl.BlockSpec((HALO, a), functools.partial(lambda i, col: (jnp.minimum((i + 1) * hb, last), col), col=col))
           for col in (2, 3, 4)]
    return tile, prev, nxt


def _group_a_fwd(zu, zv, lng, lnb, ws_ref, bb_ref, mixed_ref, vln_ref):
    u, thu = _gelu(zu)
    v, thv = _gelu(zv)
    mu = jnp.mean(v, axis=-1, keepdims=True)
    vc = v - mu
    rs = lax.rsqrt(jnp.mean(vc * vc, axis=-1, keepdims=True) + LN_EPS)
    vhat = vc * rs
    vln_ref[...] = vhat * lng + lnb
    tm, a = zu.shape
    hd = a // HEADS
    for h in range(HEADS):
        w = _masked_ws(ws_ref, h).astype(BF16)
        for b in range(tm // BLK):
            rows, cols = pl.ds(b * BLK, BLK), pl.ds(h * hd, hd)
            mixed_ref[rows, cols] = jnp.dot(w, vln_ref[rows, cols].astype(BF16), preferred_element_type=F32) + bb_ref[h]
    return u, thu, thv, rs, vhat


def _mixer_fwd(z, ln_g, ln_b, w_spatial, bb, conv_w, gg, comm=None):
    t = z.shape[0]
    a = z.shape[1] // 5
    tm = min(_TM_MIX, t)
    tile, prev, _ = _mixer_specs(t, a, tm)

    def body(z_ref, pc_ref, ph_ref, lng_ref, lnb_ref, ws_ref, bb_ref, cw_ref, gg_ref, y_ref, mixed_ref, vln_ref):
        i = pl.program_id(0)
        zu = z_ref[:, 0:a].astype(F32)
        zv = z_ref[:, a:2 * a].astype(F32)
        u, _, _, _, _ = _group_a_fwd(zu, zv, lng_ref[...], lnb_ref[...], ws_ref, bb_ref, mixed_ref, vln_ref)
        ya = u * mixed_ref[...]
        ra = lax.rsqrt(jnp.mean(ya * ya, axis=-1, keepdims=True) + RMS_EPS)
        y_ref[:, 0:a] = (ya * ra * gg_ref[:, 0:a]).astype(BF16)

        zb = z_ref[:, 2 * a:3 * a].astype(F32)
        q = z_ref[:, 3 * a:4 * a].astype(F32) * z_ref[:, 4 * a:5 * a].astype(F32)
        qp = jnp.where(i > 0, pc_ref[...].astype(F32) * ph_ref[...].astype(F32), 0.0)
        qm1 = _shift_down(q, 1, [qp[HALO - 1:HALO]])
        qm2 = _shift_down(q, 2, [qp[HALO - 2:HALO - 1], qp[HALO - 1:HALO]])
        cv = cw_ref[0:1, :] * qm2 + cw_ref[1:2, :] * qm1 + cw_ref[2:3, :] * q
        yb = zb * cv
        rb = lax.rsqrt(jnp.mean(yb * yb, axis=-1, keepdims=True) + RMS_EPS)
        y_ref[:, a:2 * a] = (yb * rb * gg_ref[:, a:2 * a]).astype(BF16)

    full = lambda shape: pl.BlockSpec(shape, lambda i: (0,) * len(shape))
    outs, couts = _call_hosting(
        body, "mixer_fwd", (t // tm,), [_sds((t, 2 * a), BF16)],
        [tile, *prev, full((1, a)), full((1, a)), full(w_spatial.shape), full(bb.shape), full(conv_w.shape),
         full((1, 2 * a))],
        [pl.BlockSpec((tm, 2 * a), lambda i: (i, 0))], [z, z, z, ln_g, ln_b, w_spatial, bb, conv_w, gg],
        [pltpu.VMEM((tm, a), F32), pltpu.VMEM((tm, a), F32)], comm)
    return outs[0], couts


def _mixer_bwd(z, dy, ln_g, ln_b, w_spatial, bb, conv_w, gg, comm=None):
    t = z.shape[0]
    a = z.shape[1] // 5
    hd = a // HEADS
    tm = min(_TM_MIX, t)
    n_tiles = t // tm
    tile, prev, nxt = _mixer_specs(t, a, tm)
    hb = tm // HALO
    dy_tile = pl.BlockSpec((tm, 2 * a), lambda i: (i, 0))
    dy_next = pl.BlockSpec((HALO, a), lambda i: (jnp.minimum((i + 1) * hb, t // HALO - 1), 1))

    def body(z_ref, pc_ref, ph_ref, nb_ref, nc_ref, nh_ref, dy_ref, ndy_ref, lng_ref, lnb_ref, ws_ref, bb_ref, cw_ref,
             gg_ref, dz_ref, dlng_ref, dlnb_ref, dws_ref, dbb_ref, dcw_ref, dgg_ref, mixed_ref, vln_ref, dmix_ref,
             dvln_ref):
        i = pl.program_id(0)

        @pl.when(i == 0)
        def _():
            for ref in (dlng_ref, dlnb_ref, dws_ref, dbb_ref, dcw_ref, dgg_ref):
                ref[...] = jnp.zeros(ref.shape, F32)

        lng = lng_ref[...]
        zu = z_ref[:, 0:a].astype(F32)
        zv = z_ref[:, a:2 * a].astype(F32)
        u, thu, thv, rs, vhat = _group_a_fwd(zu, zv, lng, lnb_ref[...], ws_ref, bb_ref, mixed_ref, vln_ref)
        mixed = mixed_ref[...]
        ya = u * mixed
        ra = lax.rsqrt(jnp.mean(ya * ya, axis=-1, keepdims=True) + RMS_EPS)
        da = dy_ref[:, 0:a].astype(F32)
        yah = ya * ra
        dgg_ref[:, 0:a] += jnp.sum(da * yah, axis=0, keepdims=True)
        ga = da * gg_ref[:, 0:a]
        dya = ra * (ga - yah * jnp.mean(ga * yah, axis=-1, keepdims=True))
        dz_ref[:, 0:a] = (dya * mixed * _gelu_grad(zu, thu)).astype(BF16)
        dmix_ref[...] = dya * u
        for h in range(HEADS):
            w = _masked_ws(ws_ref, h).astype(BF16)
            dw = jnp.zeros((BLK, BLK), F32)
            db = jnp.zeros((BLK, hd), F32)
            for b in range(tm // BLK):
                rows, cols = pl.ds(b * BLK, BLK), pl.ds(h * hd, hd)
                dm = dmix_ref[rows, cols]
                dmb = dm.astype(BF16)
                db = db + dm
                dw = dw + lax.dot_general(dmb, vln_ref[rows, cols].astype(BF16), (NT, ((), ())),
                                          preferred_element_type=F32)
                dvln_ref[rows, cols] = lax.dot_general(w, dmb, (TN, ((), ())), preferred_element_type=F32)
            dws_ref[h] += dw
            dbb_ref[h] += db
        dvln = dvln_ref[...]
        dlng_ref[...] += jnp.sum(dvln * vhat, axis=0, keepdims=True)
        dlnb_ref[...] += jnp.sum(dvln, axis=0, keepdims=True)
        dvh = dvln * lng
        dv = rs * (dvh - jnp.mean(dvh, axis=-1, keepdims=True) - vhat * jnp.mean(dvh * vhat, axis=-1, keepdims=True))
        dz_ref[:, a:2 * a] = (dv * _gelu_grad(zv, thv)).astype(BF16)

        w0, w1, w2 = cw_ref[0:1, :], cw_ref[1:2, :], cw_ref[2:3, :]
        ggb = gg_ref[:, a:2 * a]
        zb = z_ref[:, 2 * a:3 * a].astype(F32)
        zc = z_ref[:, 3 * a:4 * a].astype(F32)
        zh = z_ref[:, 4 * a:5 * a].astype(F32)
        q = zc * zh
        qp = jnp.where(i > 0, pc_ref[...].astype(F32) * ph_ref[...].astype(F32), 0.0)
        qm1 = _shift_down(q, 1, [qp[HALO - 1:HALO]])
        qm2 = _shift_down(q, 2, [qp[HALO - 2:HALO - 1], qp[HALO - 1:HALO]])
        cv = w0 * qm2 + w1 * qm1 + w2 * q

        def conv_out_grad(zb_, cv_, dout_):
            yb = zb_ * cv_
            rb = lax.rsqrt(jnp.mean(yb * yb, axis=-1, keepdims=True) + RMS_EPS)
            ybh = yb * rb
            gb = dout_ * ggb
            dyb = rb * (gb - ybh * jnp.mean(gb * ybh, axis=-1, keepdims=True))
            return dyb * zb_, dyb * cv_, ybh

        db_out = dy_ref[:, a:2 * a].astype(F32)
        g, dzb, ybh = conv_out_grad(zb, cv, db_out)
        dgg_ref[:, a:2 * a] += jnp.sum(db_out * ybh, axis=0, keepdims=True)
        dz_ref[:, 2 * a:3 * a] = dzb.astype(BF16)
        qn = nc_ref[...].astype(F32) * nh_ref[...].astype(F32)
        zbn = nb_ref[...].astype(F32)
        cvn = w0 * _shift_down(qn, 2, [q[tm - 2:tm - 1], q[tm - 1:tm]]) + w1 * _shift_down(qn, 1, [q[tm - 1:tm]]) + w2 * qn
        gn, _, _ = conv_out_grad(zbn, cvn, ndy_ref[...].astype(F32))
        gn = jnp.where(i < n_tiles - 1, gn, 0.0)
        dq = w2 * g + w1 * _shift_up(g, 1, [gn[0:1]]) + w0 * _shift_up(g, 2, [gn[0:1], gn[1:2]])
        dz_ref[:, 3 * a:4 * a] = (dq * zh).astype(BF16)
        dz_ref[:, 4 * a:5 * a] = (dq * zc).astype(BF16)
        dcw_ref[0:1, :] += jnp.sum(g * qm2, axis=0, keepdims=True)
        dcw_ref[1:2, :] += jnp.sum(g * qm1, axis=0, keepdims=True)
        dcw_ref[2:3, :] += jnp.sum(g * q, axis=0, keepdims=True)

        @pl.when(i == n_tiles - 1)
        def _():
            for h in range(HEADS):
                dbb_ref[h] = jnp.broadcast_to(jnp.sum(dbb_ref[h], axis=1, keepdims=True), (BLK, hd))
                dws_ref[h] = _masked_ws(dws_ref, h)

    full = lambda shape: pl.BlockSpec(tuple(shape), lambda i: (0,) * len(shape))
    out_shapes = (_sds((t, 5 * a), BF16), _sds((1, a), F32), _sds((1, a), F32), _sds(w_spatial.shape, F32),
                  _sds(bb.shape, F32), _sds((8, a), F32), _sds((1, 2 * a), F32))
    return _call_hosting(
        body, "mixer_bwd", (n_tiles,), out_shapes,
        [tile, *prev, *nxt, dy_tile, dy_next, full((1, a)), full((1, a)), full(w_spatial.shape), full(bb.shape),
         full(conv_w.shape), full((1, 2 * a))],
        [tile, *[full(s.shape) for s in out_shapes[1:]]], [z, z, z, z, z, z, dy, dy, ln_g, ln_b, w_spatial, bb, conv_w, gg],
        [pltpu.VMEM((tm, a), F32)] * 4, comm)


def _all_reduce_small(pack, comm=None):
    r = pack.shape[0]
    hosted = _Hosted(comm, 1, 1)
    n_ci, n_co = len(hosted.operands), len(hosted.out_shapes)

    def body(*refs):
        in_ref, c_ins, out_ref, c_outs = refs[0], refs[1:1 + n_ci], refs[1 + n_ci], refs[2 + n_ci:2 + n_ci + n_co]
        acc_ref, recv_ref, send_sems, recv_sems = refs[2 + n_ci + n_co:6 + n_ci + n_co]
        sems = refs[6 + n_ci + n_co:]
        hosted.run("start", c_ins, c_outs, sems)
        x, y, c = _place()
        partners = [(x, y, 1 - c), (1 - x, y, c), (x, 1 - y, c)]
        acc_ref[0] = in_ref[...]
        for s, partner in enumerate(partners):
            cp = pltpu.make_async_remote_copy(
                src_ref=acc_ref.at[s], dst_ref=recv_ref.at[s], send_sem=send_sems.at[s], recv_sem=recv_sems.at[s],
                device_id=partner, device_id_type=MESH)
            cp.start()
            cp.wait()
            if s < 2:
                acc_ref[s + 1] = acc_ref[s] + recv_ref[s]
            else:
                out_ref[...] = acc_ref[s] + recv_ref[s]
        for stage in ("mid1", "mid2", "finish"):
            hosted.run(stage, c_ins, c_outs, sems)

    vmem = pl.BlockSpec(memory_space=pltpu.VMEM)
    res = pl.pallas_call(
        body, name="all_reduce_small", out_shape=tuple([_sds(pack.shape, F32)] + hosted.out_shapes),
        in_specs=[vmem] + hosted.in_specs, out_specs=tuple([vmem] + hosted.out_specs),
        input_output_aliases=hosted.aliases,
        scratch_shapes=[pltpu.VMEM((3, r, 128), F32), pltpu.VMEM((3, r, 128), F32), pltpu.SemaphoreType.DMA((3,)),
                        pltpu.SemaphoreType.DMA((3,))] + hosted.scratch,
        compiler_params=pltpu.CompilerParams(vmem_limit_bytes=VMEM_LIMIT_V7X),
    )(pack, *hosted.operands)
    return res[0], list(res[1:])


def _adamw_math(w, g, m, v):
    m = ADAM_B1 * m + (1.0 - ADAM_B1) * g
    v = ADAM_B2 * v + (1.0 - ADAM_B2) * (g * g)
    m_hat = m / (1.0 - ADAM_B1 ** ADAM_STEP)
    v_hat = v / (1.0 - ADAM_B2 ** ADAM_STEP)
    delta = -ADAM_LR * (m_hat / (jnp.sqrt(v_hat) + ADAM_EPS) + ADAM_WD * w)
    return delta, m, v


def _adamw_big(name, land, w, m, v, comm=None):
    nl, n_slots, r, c = land.shape
    tr = max(8, min(r, (256 * 640) // c // 8 * 8))
    while r % tr:
        tr -= 8
    grid = (nl, r // tr)
    hosted = _Hosted(comm, 4, 4)
    n_ci, n_co = len(hosted.operands), len(hosted.out_shapes)

    def body(*refs):
        land_ref, w_ref, m_ref, v_ref = refs[:4]
        c_ins = refs[4:4 + n_ci]
        g_out, d_out, m_out, v_out = refs[4 + n_ci:8 + n_ci]
        c_outs = refs[8 + n_ci:8 + n_ci + n_co]
        sems = refs[8 + n_ci + n_co:]

        def compute():
            g = land_ref[0].astype(F32)
            for s in range(1, n_slots):
                g = g + land_ref[s].astype(F32)
            delta, mn, vn = _adamw_math(w_ref[...], g, m_ref[...], v_ref[...])
            g_out[...] = g
            d_out[...] = delta
            m_out[...] = mn
            v_out[...] = vn

        hosted.wrap(grid, compute, c_ins, c_outs, sems)

    blk = pl.BlockSpec((None, tr, c), lambda l, i: (l, i, 0))
    res = pl.pallas_call(
        body, name=name, grid=grid, out_shape=tuple([_sds((nl, r, c), F32)] * 4 + hosted.out_shapes),
        in_specs=[pl.BlockSpec((None, n_slots, tr, c), lambda l, i: (l, 0, i, 0)), blk, blk, blk] + hosted.in_specs,
        out_specs=tuple([blk] * 4 + hosted.out_specs), input_output_aliases=hosted.aliases,
        scratch_shapes=hosted.scratch, compiler_params=_cparams(2))(land, w, m, v, *hosted.operands)
    return list(res[:4]), list(res[4:])


def _adamw_small(g, w, m, v):
    def body(g_ref, w_ref, m_ref, v_ref, d_out, m_out, v_out):
        delta, mn, vn = _adamw_math(w_ref[...], g_ref[...], m_ref[...], v_ref[...])
        d_out[...] = delta
        m_out[...] = mn
        v_out[...] = vn

    return pl.pallas_call(body, name="adamw_small", out_shape=tuple([_sds(g.shape, F32)] * 3),
                          compiler_params=pltpu.CompilerParams(vmem_limit_bytes=VMEM_LIMIT_V7X))(g, w, m, v)


def _rows(a):
    return a.reshape(-1, 128)


BIG = ["w_in", "w_out", "w_gate", "w_up", "w_down"]
AG_HOSTS = {
    ("norm1", 0): [("w_in", 0), ("conv_w", 0)],
    ("mm_in", 0): [("w_out", 0), ("w_gate", 0, 0, 2)], ("mixer", 0): [("w_gate", 0, 1, 2)],
    ("mm_out", 0): [("w_up", 0, 0, 2)], ("norm2", 0): [("w_up", 0, 1, 2)],
    ("mm_swiglu", 0): [("w_down", 0), ("w_in", 1), ("w_out", 1)], ("mm_down", 0): [("w_gate", 1)],
    ("mm_in", 1): [("w_up", 1)], ("mm_swiglu", 1): [("w_down", 1)],
}


def kernel(x, norm1_g, w_in, gmlp_ln_g, gmlp_ln_b, w_spatial, b_spatial, conv_w, group_norm_g, w_out, norm2_g, w_gate, w_up, w_down, final_norm_g, loss_target, m_norm1_g, m_w_in, m_gmlp_ln_g, m_gmlp_ln_b, m_w_spatial, m_b_spatial, m_conv_w, m_group_norm_g, m_w_out, m_norm2_g, m_w_gate, m_w_up, m_w_down, m_final_norm_g, v_norm1_g, v_w_in, v_gmlp_ln_g, v_gmlp_ln_b, v_w_spatial, v_b_spatial, v_conv_w, v_group_norm_g, v_w_out, v_norm2_g, v_w_gate, v_w_up, v_w_down, v_final_norm_g):
    nl = N_LAYERS
    t, d = x.shape[1], x.shape[2]
    a = d // 2
    hd = a // HEADS
    xin = x.reshape(t, d)
    target = loss_target.reshape(t, d)
    me = _index(_place())

    tr = lambda w: jnp.transpose(w, (0, 2, 1))
    big = {"w_in": w_in, "w_out": w_out, "w_gate": tr(w_gate), "w_up": tr(w_up), "w_down": w_down}
    big_m = {"w_in": m_w_in, "w_out": m_w_out, "w_gate": tr(m_w_gate), "w_up": tr(m_w_up), "w_down": m_w_down}
    big_v = {"w_in": v_w_in, "w_out": v_w_out, "w_gate": tr(v_w_gate), "w_up": tr(v_w_up), "w_down": v_w_down}
    block = {k: big[k].shape[1:] for k in BIG}
    view = {k: _cols_view(block[k][1]) if k == "w_in" else _rows_view(block[k][0]) for k in BIG}
    full_shape = {k: (block[k][0], N_DEV * block[k][1]) if k == "w_in" else (N_DEV * block[k][0], block[k][1])
                  for k in BIG}

    weights = {}
    shards = {(k, l): big[k][l].astype(BF16) for k in BIG for l in range(nl)}

    def ag_spec(k, l, part=0, n_parts=1):
        if k == "conv_w":
            return (conv_w, _sds((N_DEV, *conv_w.shape), F32), _SLOT_WHOLE, (0,), None)
        halves = (_cols_halves(*block[k], part, n_parts) if k == "w_in" else _rows_halves(block[k][0], part, n_parts))
        return (shards[(k, l)], _sds(full_shape[k], BF16), halves, (0, 1), weights.get((k, l)))

    bb = jnp.broadcast_to(b_spatial[..., None], (nl, HEADS, BLK, hd))

    def hosted(name, l):
        keys = AG_HOSTS.get((name, l), [])
        return keys, ([_ag_piece([ag_spec(*key) for key in keys])] if keys else None)

    def landed(keys, couts):
        for key, arr in zip(keys, couts):
            weights[key[:2]] = arr

    saved = []
    xl = xin
    for l in range(nl):
        keys, comm = hosted("norm1", l)
        h, couts = _rmsnorm_fwd(xl, norm1_g[l:l + 1], comm)
        landed(keys, couts)
        if l == 0:
            conv_full = jnp.transpose(weights[("conv_w", 0)], (1, 2, 0, 3)).reshape(nl, 3, a)
        keys, comm = hosted("mm_in", l)
        z, couts = _mm_in(h, weights[("w_in", l)], comm)
        landed(keys, couts)
        keys, comm = hosted("mixer", l)
        y, couts = _mixer_fwd(z, gmlp_ln_g[l:l + 1], gmlp_ln_b[l:l + 1], w_spatial[l], bb[l], conv_full[l],
                              group_norm_g[l:l + 1], comm)
        landed(keys, couts)
        keys, comm = hosted("mm_out", l)
        x1, couts = _mm_out(y, weights[("w_out", l)], xl, comm)
        landed(keys, couts)
        keys, comm = hosted("norm2", l)
        h2, couts = _rmsnorm_fwd(x1, norm2_g[l:l + 1], comm)
        landed(keys, couts)
        keys, comm = hosted("mm_swiglu", l)
        (act, dact_dgate, dact_dup), couts = _mm_swiglu(h2, weights[("w_gate", l)], weights[("w_up", l)], comm)
        landed(keys, couts)
        keys, comm = hosted("mm_down", l)
        x2, couts = _mm_down(act, weights[("w_down", l)], x1, comm)
        landed(keys, couts)
        saved.append(dict(x=xl, h=h, z=z, y=y, x1=x1, h2=h2, dact_dgate=dact_dgate, dact_dup=dact_dup, act=act))
        xl = x2

    dx, dxb, d_final_g, loss_part = _loss_head(xl, final_norm_g.reshape(1, d), target)
    small = [None] * nl
    core = lax.axis_index("c").astype(jnp.int32).reshape(1)
    stage_shape = {k: _sds((N_CHIPS, *block[k]), BF16) for k in BIG}
    land_shape = {k: _sds((nl, N_CHIPS, *block[k]), BF16) for k in BIG}
    grads = [dict() for _ in range(nl)]
    stages = [dict() for _ in range(nl)]
    sums = [dict() for _ in range(nl)]
    lands = {k: None for k in BIG}

    def core_job(l, keys):
        def sink(outs):
            stages[l].update(zip(keys, outs))
        return _rs_core_piece([(grads[l][k], stage_shape[k], view[k]) for k in keys]), sink

    def chip_job(l, items):
        keys = [item[0] for item in items]

        def rows(k, p0, p1, n_parts):
            per = block[k][0] // n_parts
            return (p0 * per, (p1 - p0) * per)

        def sink(outs):
            lands.update(zip(keys, outs))
        return _rs_chip_piece([(sums[l][k], land_shape[k], rows(k, p0, p1, n_parts), lands[k])
                               for k, p0, p1, n_parts in items], l), sink

    def add_up(l, keys):
        for k in keys:
            sums[l][k] = _chip_sums(f"chip_sums_{k}", grads[l][k], stages[l][k], k == "w_in", core)

    def host(*jobs):
        def deliver(couts):
            i = 0
            for piece, sink in jobs:
                n_out = len(piece.out_shapes)
                sink(couts[i:i + n_out])
                i += n_out
        return [piece for piece, _ in jobs], deliver

    whole = lambda k: (k, 0, 1, 1)
    for l in reversed(range(nl)):
        s = saved[l]
        wi, wo, wgt, wut, wd = [weights[(k, l)] for k in BIG]
        later = l + 1 < nl
        comm, deliver = host(chip_job(l + 1, [("w_in", 0, 1, 2)])) if later else host()
        (grads[l]["w_down"],), couts = _mm_dw("mm_dw_down", [s["act"]], dxb, 2816, 1024, comm)
        deliver(couts)
        comm, deliver = (host(core_job(l, ["w_down"]), chip_job(l + 1, [("w_in", 1, 2, 2)])) if later
                         else host(core_job(l, ["w_down"])))
        (dgate, dup), couts = _mm_dact(dxb, wd, s["dact_dgate"], s["dact_dup"], comm)
        deliver(couts)
        add_up(l, ["w_down"])
        comm, deliver = host(chip_job(l, [whole("w_down")]))
        (grads[l]["w_gate"], grads[l]["w_up"]), couts = _mm_dw("mm_dw_gate_up", [dgate, dup], s["h2"], 1408, 1024, comm)
        deliver(couts)
        comm, deliver = host(core_job(l, ["w_gate", "w_up"]))
        dh2, couts = _mm_dh2(dgate, dup, wgt, wut, comm)
        deliver(couts)
        add_up(l, ["w_gate", "w_up"])
        dx1, dx1b, d_n2 = _rmsnorm_bwd(s["x1"], norm2_g[l:l + 1], dh2, dx)
        comm, deliver = host(chip_job(l, [("w_gate", 0, 1, 4)]))
        dy, couts = _mm_dy(dx1b, wo, comm)
        deliver(couts)
        comm, deliver = host(chip_job(l, [("w_gate", 1, 2, 4)]))
        (grads[l]["w_out"],), couts = _mm_dw("mm_dw_out", [s["y"]], dx1b, 1024, 1024, comm)
        deliver(couts)
        comm, deliver = host(chip_job(l, [("w_gate", 2, 4, 4)]))
        (dz, d_lng, d_lnb, d_ws, d_bb, d_cw, d_gg), couts = _mixer_bwd(
            s["z"], dy, gmlp_ln_g[l:l + 1], gmlp_ln_b[l:l + 1], w_spatial[l], bb[l], conv_full[l], group_norm_g[l:l + 1],
            comm)
        deliver(couts)
        comm, deliver = host(chip_job(l, [("w_up", 0, 3, 4)]), core_job(l, ["w_out"]))
        (grads[l]["w_in"],), couts = _mm_dw("mm_dw_in", [s["h"]], dz, 2048, 1024, comm)
        deliver(couts)
        add_up(l, ["w_out"])
        comm, deliver = host(chip_job(l, [("w_up", 3, 4, 4), whole("w_out")]), core_job(l, ["w_in"]))
        dh, couts = _mm_dh(dz, wi, comm)
        deliver(couts)
        add_up(l, ["w_in"])
        dx, dxb, d_n1 = _rmsnorm_bwd(s["x"], norm1_g[l:l + 1], dh, dx1)
        small[l] = dict(norm1_g=d_n1, gmlp_ln_g=d_lng, gmlp_ln_b=d_lnb, w_spatial=d_ws, b_spatial=d_bb[:, :, 0],
                        group_norm_g=d_gg, norm2_g=d_n2, conv_w=d_cw[0:3])
    grad_x = dx.reshape(x.shape)

    rep = ["norm1_g", "gmlp_ln_g", "gmlp_ln_b", "w_spatial", "b_spatial", "group_norm_g", "norm2_g"]
    rep_w = dict(norm1_g=norm1_g, gmlp_ln_g=gmlp_ln_g, gmlp_ln_b=gmlp_ln_b, w_spatial=w_spatial, b_spatial=b_spatial,
                 group_norm_g=group_norm_g, norm2_g=norm2_g)
    rep_m = dict(norm1_g=m_norm1_g, gmlp_ln_g=m_gmlp_ln_g, gmlp_ln_b=m_gmlp_ln_b, w_spatial=m_w_spatial,
                 b_spatial=m_b_spatial, group_norm_g=m_group_norm_g, norm2_g=m_norm2_g)
    rep_v = dict(norm1_g=v_norm1_g, gmlp_ln_g=v_gmlp_ln_g, gmlp_ln_b=v_gmlp_ln_b, w_spatial=v_w_spatial,
                 b_spatial=v_b_spatial, group_norm_g=v_group_norm_g, norm2_g=v_norm2_g)
    parts = [_rows(jnp.stack([small[l][k].reshape(rep_w[k].shape[1:]) for l in range(nl)])) for k in rep]
    parts.append(_rows(d_final_g))
    parts.append(_rows(jnp.stack([small[l]["conv_w"] for l in range(nl)])))
    parts.append(jnp.broadcast_to(loss_part, (8, 128)))
    sizes = [p.shape[0] for p in parts]
    comm, deliver = host(chip_job(0, [whole("w_in")]))
    total, couts = _all_reduce_small(jnp.concatenate(parts, axis=0), comm)
    deliver(couts)
    offs = [0]
    for n in sizes:
        offs.append(offs[-1] + n)
    pieces = [total[offs[i]:offs[i + 1]] for i in range(len(parts))]
    loss = pieces[-1][0, 0]
    conv_g_full = pieces[-2].reshape(nl, 3, N_DEV, a // N_DEV)
    conv_g = lax.dynamic_index_in_dim(conv_g_full, me, axis=2, keepdims=False)
    n_rep = offs[len(rep) + 1]
    pad = jnp.zeros((2, 128), F32)

    def small_pack(named, final, conv):
        return jnp.concatenate([_rows(named[k]) for k in rep] + [_rows(final), _rows(conv), pad], axis=0)

    g_small = jnp.concatenate([total[:n_rep], _rows(conv_g), pad], axis=0)
    d_small, m_small, v_small = _adamw_small(
        g_small, small_pack(rep_w, final_norm_g, conv_w), small_pack(rep_m, m_final_norm_g, m_conv_w),
        small_pack(rep_v, v_final_norm_g, v_conv_w))

    def unpack(packed):
        out = {k: packed[offs[i]:offs[i + 1]].reshape(rep_w[k].shape) for i, k in enumerate(rep)}
        out["final_norm_g"] = packed[offs[len(rep)]:n_rep].reshape(final_norm_g.shape)
        out["conv_w"] = packed[n_rep:n_rep + 6].reshape(conv_w.shape)
        return out

    res = {"grad": unpack(g_small), "delta": unpack(d_small), "m": unpack(m_small), "v": unpack(v_small)}

    for k in BIG:
        outs, _ = _adamw_big(f"adamw_{k}", lands[k], big[k], big_m[k], big_v[k])
        if k in ("w_gate", "w_up"):
            outs = [tr(o) for o in outs]
        res["grad"][k], res["delta"][k], res["m"][k], res["v"][k] = outs

    order = ["norm1_g", "w_in", "gmlp_ln_g", "gmlp_ln_b", "w_spatial", "b_spatial", "conv_w", "group_norm_g", "w_out",
             "norm2_g", "w_gate", "w_up", "w_down", "final_norm_g"]
    return (loss, grad_x, *[res["grad"][k] for k in order], *[res["delta"][k] for k in order],
            *[res["m"][k] for k in order], *[res["v"][k] for k in order])
```

```python
import functools
import math
import operator

import jax
import jax.numpy as jnp
from jax import lax
from jax.experimental import pallas as pl
from jax.experimental.pallas import tpu as pltpu

F32 = jnp.float32
BF16 = jnp.bfloat16
MESH = pl.DeviceIdType.MESH

N_DEV = 8
N_LAYERS = 2
HEADS = 8
BLK = 128
CHUNK = 64
HALO = 16
RMS_EPS = 1e-6
LN_EPS = 1e-5
ADAM_LR, ADAM_B1, ADAM_B2, ADAM_EPS, ADAM_WD, ADAM_STEP = 0.001, 0.9, 0.999, 1e-8, 0.01, 10
GELU_C = math.sqrt(2.0 / math.pi)
GELU_A = 0.044715

VMEM_LIMIT_V7X = 56 * 1024 * 1024
_TM = 1024
_TN = 1024
_TT = 1024
_TM_MIX = 256
_TM_NORM = 512


def _cparams(n_axes):
    return pltpu.CompilerParams(dimension_semantics=("arbitrary",) * n_axes, vmem_limit_bytes=VMEM_LIMIT_V7X)


def _sds(shape, dtype):
    return jax.ShapeDtypeStruct(tuple(shape), dtype)


def _place():
    return lax.axis_index("x"), lax.axis_index("y"), lax.axis_index("c")


def _index(place):
    return 4 * place[0] + 2 * place[1] + place[2]


class _Piece:
    def __init__(self, operands, out_shapes, aliases, n_sems, start, finish, mid1=None, mid2=None, vmem=()):
        self.operands, self.out_shapes, self.aliases, self.n_sems = list(operands), list(out_shapes), dict(aliases), n_sems
        self.vmem = list(vmem)
        nothing = lambda ctx: None
        self.start, self.mid1, self.mid2, self.finish = start, mid1 or nothing, mid2 or nothing, finish


class _Ctx:
    def __init__(self, ins, outs, sems, offs):
        self.ins, self.outs, self.sems = ins, outs, sems
        self.o_in, self.o_out, self.o_send, self.o_recv, self.o_loc, self.o_vmem = offs

    def vmem(self, i):
        return self.sems[3 + self.o_vmem + i]

    def inp(self, i):
        return self.ins[self.o_in + i]

    def out(self, i):
        return self.outs[self.o_out + i]

    def send(self, k):
        return self.sems[0].at[self.o_send + k]

    def recv(self, k):
        return self.sems[1].at[self.o_recv + k]

    def local(self, k):
        return self.sems[2].at[self.o_loc + k]


class _Hosted:
    def __init__(self, pieces, n_in_before, n_out_before):
        self.pieces = [p for p in (pieces or []) if p is not None]
        self.operands, self.out_shapes, self.aliases, self.offs = [], [], {}, []
        counts, vmem = [0, 0, 0], []
        for p in self.pieces:
            self.offs.append((len(self.operands), len(self.out_shapes), *counts, len(vmem)))
            for i, j in p.aliases.items():
                self.aliases[n_in_before + len(self.operands) + i] = n_out_before + len(self.out_shapes) + j
            self.operands += p.operands
            self.out_shapes += p.out_shapes
            counts = [c + n for c, n in zip(counts, p.n_sems)]
            vmem += p.vmem
        hbm = pl.BlockSpec(memory_space=pl.ANY)
        self.in_specs = [hbm] * len(self.operands)
        self.out_specs = [hbm] * len(self.out_shapes)
        self.scratch = ([pltpu.SemaphoreType.DMA((max(c, 1),)) for c in counts] + vmem) if self.pieces else []

    def run(self, stage, ins, outs, sems):
        for p, offs in zip(self.pieces, self.offs):
            getattr(p, stage)(_Ctx(ins, outs, sems, offs))

    def wrap(self, grid, compute, ins, outs, sems):
        if not self.pieces:
            compute()
            return
        n_steps = math.prod(grid)
        lin = 0
        for ax, g in enumerate(grid):
            lin = lin * g + pl.program_id(ax)
        pl.when(lin == 0)(lambda: self.run("start", ins, outs, sems))
        compute()
        pl.when(lin == n_steps // 2)(lambda: self.run("mid1", ins, outs, sems))
        pl.when(lin == max(n_steps - 3, n_steps // 2))(lambda: self.run("mid2", ins, outs, sems))
        pl.when(lin == n_steps - 1)(lambda: self.run("finish", ins, outs, sems))


def _cols_view(width):
    return lambda ref, p: ref.at[:, pl.ds(pl.multiple_of(p * width, 128), width)]


def _rows_view(height):
    return lambda ref, p: ref.at[pl.ds(pl.multiple_of(p * height, 16), height), :]


def _cols_halves(rows, width, part, n_parts):
    hr = rows // n_parts // 2
    at = lambda h: pl.ds(part * 2 * hr + h * hr, hr)
    return (lambda ref, p, h: ref.at[at(h), pl.ds(pl.multiple_of(p * width, 128), width)],
            lambda ref, h: ref.at[at(h), :], 2)


def _rows_halves(height, part, n_parts):
    hh = height // n_parts // 2
    return (lambda ref, p, h: ref.at[pl.ds(pl.multiple_of(p * height + part * 2 * hh + h * hh, 16), hh), :],
            lambda ref, h: ref.at[pl.ds(part * 2 * hh + h * hh, hh), :], 2)


_SLOT_WHOLE = (lambda ref, p, h: ref.at[p], lambda ref, h: ref, 1)


def _ag_piece(specs):
    units = [(a, h) for a, s in enumerate(specs) for h in s[3]]

    def plan(ctx):
        x, y, c = _place()
        me, sib, xn, yn, dg = (x, y, c), (x, y, 1 - c), (1 - x, y, c), (x, 1 - y, c), (1 - x, 1 - y, c)

        def copy(u, k, block, to, from_shard=False):
            a, h = units[u]
            dst_of, src_of, _ = specs[a][2]
            dst = dst_of(ctx.out(a), _index(block), h)
            return pltpu.make_async_remote_copy(
                src_ref=src_of(ctx.inp(a), h) if from_shard else dst, dst_ref=dst, send_sem=ctx.send(7 * u + k),
                recv_sem=ctx.recv(7 * u + k), device_id=to, device_id_type=MESH)

        def local(u):
            a, h = units[u]
            dst_of, src_of, _ = specs[a][2]
            return pltpu.make_async_copy(src_of(ctx.inp(a), h), dst_of(ctx.out(a), _index(me), h), ctx.local(u))

        def relay(u):
            return copy(u, 3, xn, yn) if units[u][1] % 2 == 0 else copy(u, 3, yn, xn)

        return me, sib, xn, yn, dg, c, copy, local, relay

    def start(ctx):
        me, sib, xn, yn, dg, c, copy, local, relay = plan(ctx)
        for u in range(len(units)):
            local(u).start()
            for k, to in enumerate((sib, xn, yn)):
                copy(u, k, me, to, from_shard=True).start()

    def mid1(ctx):
        me, sib, xn, yn, dg, c, copy, local, relay = plan(ctx)
        for u in range(len(units)):
            copy(u, 1, xn, me).wait_recv()
            copy(u, 2, yn, me).wait_recv()
            relay(u).start()
            copy(u, 4, xn, sib).start()
            copy(u, 5, yn, sib).start()

    def mid2(ctx):
        me, sib, xn, yn, dg, c, copy, local, relay = plan(ctx)
        for u in range(len(units)):
            copy(u, 3, dg, me).wait_recv()
            copy(u, 6, dg, sib).start()

    def finish(ctx):
        me, sib, xn, yn, dg, c, copy, local, relay = plan(ctx)
        other = lambda place: (place[0], place[1], 1 - c)
        for u in range(len(units)):
            for k, block in ((0, sib), (4, other(xn)), (5, other(yn)), (6, other(dg))):
                copy(u, k, block, me).wait_recv()
        for u in range(len(units)):
            for k, to in enumerate((sib, xn, yn)):
                copy(u, k, me, to, from_shard=True).wait_send()
            relay(u).wait_send()
            for k, block in ((4, xn), (5, yn), (6, dg)):
                copy(u, k, block, sib).wait_send()
            local(u).wait()

    n_u = len(units)
    operands, aliases = [s[0] for s in specs], {}
    for a, spec in enumerate(specs):
        if spec[4] is not None:
            aliases[len(operands)] = a
            operands.append(spec[4])
    return _Piece(operands, [s[1] for s in specs], aliases, (7 * n_u, 7 * n_u, n_u), start, finish, mid1, mid2)


N_CHIPS = 4


def _rs_core_piece(specs):
    n = len(specs)

    def copies(ctx):
        x, y, c = _place()
        out = []
        for a in range(n):
            for q in range(N_CHIPS):
                out.append(pltpu.make_async_remote_copy(
                    src_ref=specs[a][2](ctx.inp(a), 2 * q + (1 - c)), dst_ref=ctx.out(a).at[q],
                    send_sem=ctx.send(N_CHIPS * a + q), recv_sem=ctx.recv(N_CHIPS * a + q), device_id=(x, y, 1 - c),
                    device_id_type=MESH))
        return out

    def start(ctx):
        for cp in copies(ctx):
            cp.start()

    def finish(ctx):
        for cp in copies(ctx):
            cp.wait_recv()
            cp.wait_send()

    return _Piece([s[0] for s in specs], [s[1] for s in specs], {}, (N_CHIPS * n, N_CHIPS * n, 0), start, finish)


def _rs_chip_piece(specs, layer):
    n = len(specs)
    hops = [(1, 0), (0, 1), (1, 1)]

    def copies(ctx):
        x, y, c = _place()
        mine = 2 * x + y
        out = []
        for a in range(n):
            rows = pl.ds(*specs[a][2])
            sums, land = ctx.inp(a), ctx.out(a)
            out.append((pltpu.make_async_copy(sums.at[mine, rows], land.at[layer, mine, rows], ctx.local(a)), None))
            for j, (dx, dy) in enumerate(hops):
                px, py = x ^ dx, y ^ dy
                peer = 2 * px + py
                send = pltpu.make_async_remote_copy(
                    src_ref=sums.at[peer, rows], dst_ref=land.at[layer, mine, rows], send_sem=ctx.send(3 * a + j),
                    recv_sem=ctx.recv(3 * a + j), device_id=(px, py, c), device_id_type=MESH)
                recv = pltpu.make_async_remote_copy(
                    src_ref=sums.at[peer, rows], dst_ref=land.at[layer, peer, rows], send_sem=ctx.send(3 * a + j),
                    recv_sem=ctx.recv(3 * a + j), device_id=(px, py, c), device_id_type=MESH)
                out.append((send, recv))
        return out

    def start(ctx):
        for send, _ in copies(ctx):
            send.start()

    def finish(ctx):
        for send, recv in copies(ctx):
            if recv is None:
                send.wait()
            else:
                recv.wait_recv()
                send.wait_send()

    operands, aliases = [s[0] for s in specs], {}
    for a, spec in enumerate(specs):
        if spec[3] is not None:
            aliases[len(operands)] = a
            operands.append(spec[3])
    return _Piece(operands, [s[1] for s in specs], aliases, (3 * n, 3 * n, n), start, finish)


def _all_reduce_piece(pack):
    r = pack.shape[0]

    def exchange(ctx, s):
        x, y, c = _place()
        partner = [(x, y, 1 - c), (1 - x, y, c), (x, 1 - y, c)][s]
        return pltpu.make_async_remote_copy(
            src_ref=ctx.vmem(0).at[s], dst_ref=ctx.vmem(1).at[s], send_sem=ctx.send(s), recv_sem=ctx.recv(s),
            device_id=partner, device_id_type=MESH)

    def start(ctx):
        load = pltpu.make_async_copy(ctx.inp(0), ctx.vmem(0).at[0], ctx.local(0))
        load.start()
        load.wait()
        exchange(ctx, 0).start()

    def step(s):
        def run(ctx):
            exchange(ctx, s - 1).wait()
            acc, got = ctx.vmem(0), ctx.vmem(1)
            acc[s] = acc[s - 1] + got[s - 1]
            if s < 3:
                exchange(ctx, s).start()
            else:
                store = pltpu.make_async_copy(acc.at[3], ctx.out(0), ctx.local(0))
                store.start()
                store.wait()
        return run

    return _Piece([pack], [_sds(pack.shape, F32)], {}, (3, 3, 1), start, step(3), step(1), step(2),
                  vmem=[pltpu.VMEM((4, r, 128), F32), pltpu.VMEM((3, r, 128), F32)])


def _chip_sums(name, grad, stage, by_cols, core):
    _, r, c = stage.shape
    tr = r
    while tr * c > 1024 * 1024 or r % tr or tr % 16:
        tr -= 16
    n_t = r // tr

    def body(core_ref, g_ref, s_ref, o_ref):
        o_ref[...] = (g_ref[...].astype(F32) + s_ref[...].astype(F32)).astype(BF16)

    if by_cols:
        gspec = pl.BlockSpec((tr, c), lambda q, i, core_ref: (i, 2 * q + core_ref[0]))
    else:
        gspec = pl.BlockSpec((tr, c), lambda q, i, core_ref: ((2 * q + core_ref[0]) * n_t + i, 0))
    sspec = pl.BlockSpec((None, tr, c), lambda q, i, core_ref: (q, i, 0))
    return pl.pallas_call(
        body, name=name, out_shape=_sds(stage.shape, BF16),
        grid_spec=pltpu.PrefetchScalarGridSpec(num_scalar_prefetch=1, grid=(N_CHIPS, n_t), in_specs=[gspec, sspec],
                                               out_specs=sspec),
        compiler_params=_cparams(2))(core, grad, stage)


def _call_hosting(body, name, grid, out_shapes, in_specs, out_specs, operands, scratch, comm):
    n_in, n_out, n_scr = len(operands), len(out_shapes), len(scratch)
    hosted = _Hosted(comm, n_in, n_out)
    n_ci, n_co = len(hosted.operands), len(hosted.out_shapes)

    def hosting_body(*refs):
        ins, rest = refs[:n_in], refs[n_in:]
        c_ins, rest = rest[:n_ci], rest[n_ci:]
        outs, rest = rest[:n_out], rest[n_out:]
        c_outs, rest = rest[:n_co], rest[n_co:]
        hosted.wrap(grid, lambda: body(*ins, *outs, *rest[:n_scr]), c_ins, c_outs, rest[n_scr:])

    res = pl.pallas_call(
        hosting_body, name=name, grid=grid, out_shape=tuple(list(out_shapes) + hosted.out_shapes),
        in_specs=list(in_specs) + hosted.in_specs, out_specs=tuple(list(out_specs) + hosted.out_specs),
        input_output_aliases=hosted.aliases, scratch_shapes=list(scratch) + hosted.scratch,
        compiler_params=_cparams(len(grid)))(*operands, *hosted.operands)
    return list(res[:n_out]), list(res[n_out:])


def _matmul(name, grid, nk, kaxis, pairs, dims, extras, outs, epilogue, sum_pairs, acc_shape, comm=None, split=None):
    n_p, n_e, n_o = len(pairs), len(extras), len(outs)
    n_acc = 0 if nk == 1 else (1 if sum_pairs else n_p)
    n_in = 2 * n_p + n_e
    hosted = _Hosted(comm, n_in, n_o)
    n_ci, n_co = len(hosted.operands), len(hosted.out_shapes)

    def body(*refs):
        a_refs = refs[0:2 * n_p:2]
        b_refs = refs[1:2 * n_p:2]
        e_refs = refs[2 * n_p:n_in]
        c_ins = refs[n_in:n_in + n_ci]
        o_refs = refs[n_in + n_ci:n_in + n_ci + n_o]
        c_outs = refs[n_in + n_ci + n_o:n_in + n_ci + n_o + n_co]
        acc_refs = refs[n_in + n_ci + n_o + n_co:n_in + n_ci + n_o + n_co + n_acc]
        sems = refs[n_in + n_ci + n_o + n_co + n_acc:]

        def dots():
            if sum_pairs and n_p > 1 and dims == NN:
                a_all = jnp.concatenate([a[...] for a in a_refs], axis=1)
                b_all = jnp.concatenate([b[...] for b in b_refs], axis=0)
                return [lax.dot_general(a_all, b_all, (dims, ((), ())), preferred_element_type=F32)]
            prods = [lax.dot_general(a[...], b[...], (dims, ((), ())), preferred_element_type=F32)
                     for a, b in zip(a_refs, b_refs)]
            if sum_pairs and n_p > 1:
                prods = [functools.reduce(operator.add, prods)]
            return prods

        def compute():
            if nk == 1 and split is not None:
                n_split, b_axis, n_row = split
                width = b_refs[0].shape[b_axis] // n_split
                height = a_refs[0].shape[0] // n_row
                for s in range(n_split):
                    cols = pl.ds(s * width, width)
                    for r in range(n_row):
                        rows = pl.ds(r * height, height)
                        epilogue([lax.dot_general(a[rows, :], b[cols, :] if b_axis == 0 else b[:, cols], (dims, ((), ())),
                                                  preferred_element_type=F32) for a, b in zip(a_refs, b_refs)],
                                 e_refs, o_refs, rows, cols)
                return
            if nk == 1:
                epilogue(dots(), e_refs, o_refs)
                return
            k = pl.program_id(kaxis)

            @pl.when(k == 0)
            def _():
                for acc, p in zip(acc_refs, dots()):
                    acc[...] = p

            if nk > 2:
                @pl.when((k > 0) & (k < nk - 1))
                def _():
                    for acc, p in zip(acc_refs, dots()):
                        acc[...] += p

            @pl.when(k == nk - 1)
            def _():
                epilogue([acc[...] + p for acc, p in zip(acc_refs, dots())], e_refs, o_refs)

        hosted.wrap(grid, compute, c_ins, c_outs, sems)

    operands, in_specs = [], []
    for a, a_spec, b, b_spec in pairs:
        operands += [a, b]
        in_specs += [a_spec, b_spec]
    for e, e_spec in extras:
        operands.append(e)
        in_specs.append(e_spec)
    res = pl.pallas_call(
        body, name=name, grid=grid,
        out_shape=tuple([o for o, _ in outs] + hosted.out_shapes),
        in_specs=in_specs + hosted.in_specs, out_specs=tuple([s for _, s in outs] + hosted.out_specs),
        input_output_aliases=hosted.aliases,
        scratch_shapes=[pltpu.VMEM(acc_shape, F32) for _ in range(n_acc)] + hosted.scratch,
        compiler_params=_cparams(len(grid)),
    )(*operands, *hosted.operands)
    return list(res[:n_o]), list(res[n_o:])


NN = ((1,), (0,))
NT = ((1,), (1,))
TN = ((0,), (0,))


def _tile(n, want):
    if n <= want:
        return n
    t = want // 128 * 128
    while n % t:
        t -= 128
    return t


def _silu_parts(g):
    s = 0.5 + 0.5 * jnp.tanh(0.5 * g)
    return s, g * s


def _mm_in(h, w_in, comm=None):
    t, d = h.shape
    n = w_in.shape[1]
    tm, tn = _tile(t, _TM), _tile(n, _TN)

    def epi(accs, e, o):
        o[0][...] = accs[0].astype(BF16)

    outs, couts = _matmul(
        "mm_in", (n // tn, t // tm), 1, None,
        [(h, pl.BlockSpec((tm, d), lambda j, i: (i, 0)), w_in, pl.BlockSpec((d, tn), lambda j, i: (0, j)))],
        NN, [], [(_sds((t, n), BF16), pl.BlockSpec((tm, tn), lambda j, i: (i, j)))], epi, True, None, comm)
    return outs[0], couts


def _mm_out(y, w_out, x, comm=None):
    t, m = y.shape
    d = w_out.shape[1]
    tm, tn = _tile(t, _TM), _tile(d, _TN)

    def epi(accs, e, o):
        o[0][...] = e[0][...] + accs[0]

    outs, couts = _matmul(
        "mm_out", (t // tm, d // tn), 1, None,
        [(y, pl.BlockSpec((tm, m), lambda i, j: (i, 0)), w_out, pl.BlockSpec((m, tn), lambda i, j: (0, j)))],
        NN, [(x, pl.BlockSpec((tm, tn), lambda i, j: (i, j)))],
        [(_sds((t, d), F32), pl.BlockSpec((tm, tn), lambda i, j: (i, j)))], epi, True, None, comm)
    return outs[0], couts


def _mm_swiglu(h2, wgt, wut, comm=None):
    t, d = h2.shape
    f = wgt.shape[0]
    tm, tn = _tile(t, 2 * _TM), _tile(f, 512)

    def epi(accs, e, o, rows, cols):
        g, u = accs
        s, sg = _silu_parts(g)
        o[0][rows, cols] = (sg * u).astype(BF16)
        o[1][rows, cols] = (u * (s + sg * (1.0 - s))).astype(BF16)
        o[2][rows, cols] = sg.astype(BF16)

    wspec = pl.BlockSpec((tn, d), lambda i, j: (j, 0))
    hspec = pl.BlockSpec((tm, d), lambda i, j: (i, 0))
    ospec = pl.BlockSpec((tm, tn), lambda i, j: (i, j))
    osh = _sds((t, f), BF16)
    outs, couts = _matmul("mm_swiglu", (t // tm, f // tn), 1, None, [(h2, hspec, wgt, wspec), (h2, hspec, wut, wspec)],
                          NT, [], [(osh, ospec)] * 3, epi, False, None, comm, split=(tn // 256, 0, 2))
    return outs, couts


def _mm_down(act, wd, x1, comm=None):
    t, f = act.shape
    d = wd.shape[1]
    tm, tn = _tile(t, _TM), _tile(d, _TN)
    nk = 2
    tk = f // nk

    def epi(accs, e, o):
        o[0][...] = e[0][...] + accs[0]

    outs, couts = _matmul(
        "mm_down", (t // tm, d // tn, nk), nk, 2,
        [(act, pl.BlockSpec((tm, tk), lambda i, j, k: (i, k)), wd, pl.BlockSpec((tk, tn), lambda i, j, k: (k, j)))],
        NN, [(x1, pl.BlockSpec((tm, tn), lambda i, j, k: (i, j)))],
        [(_sds((t, d), F32), pl.BlockSpec((tm, tn), lambda i, j, k: (i, j)))], epi, True, (tm, tn), comm)
    return outs[0], couts


def _mm_dact(dxb, wd, dact_dgate, dact_dup, comm=None):
    t, d = dxb.shape
    f = wd.shape[0]
    tm, tn = _tile(t, 2 * _TM), _tile(f, 512)

    def epi(accs, e, o, rows, cols):
        da = accs[0]
        o[0][rows, cols] = (da * e[0][rows, cols].astype(F32)).astype(BF16)
        o[1][rows, cols] = (da * e[1][rows, cols].astype(F32)).astype(BF16)

    bspec = pl.BlockSpec((tm, tn), lambda i, j: (i, j))
    osh = _sds((t, f), BF16)
    outs, couts = _matmul(
        "mm_dact", (t // tm, f // tn), 1, None,
        [(dxb, pl.BlockSpec((tm, d), lambda i, j: (i, 0)), wd, pl.BlockSpec((tn, d), lambda i, j: (j, 0)))],
        NT, [(dact_dgate, bspec), (dact_dup, bspec)], [(osh, bspec)] * 2, epi, True, None, comm, split=(tn // 256, 0, 2))
    return outs, couts


def _mm_dh2(dgate, dup, wgt, wut, comm=None):
    t, f = dgate.shape
    d = wgt.shape[1]
    tm, tn = _tile(t, _TM), _tile(d, _TN)
    nk = 4
    tk = f // nk

    def epi(accs, e, o):
        o[0][...] = accs[0]

    aspec = pl.BlockSpec((tm, tk), lambda i, j, k: (i, k))
    wspec = pl.BlockSpec((tk, tn), lambda i, j, k: (k, j))
    outs, couts = _matmul("mm_dh2", (t // tm, d // tn, nk), nk, 2, [(dgate, aspec, wgt, wspec), (dup, aspec, wut, wspec)],
                          NN, [], [(_sds((t, d), F32), pl.BlockSpec((tm, tn), lambda i, j, k: (i, j)))], epi, True,
                          (tm, tn), comm)
    return outs[0], couts


def _mm_dw(name, a_list, b, tmo, tno, comm=None):
    t, m = a_list[0].shape
    n = b.shape[1]
    tt = _tile(t, _TT)
    nk = t // tt
    tmo, tno = _tile(m, tmo), _tile(n, tno)

    def epi(accs, e, o):
        for acc, out in zip(accs, o):
            out[...] = acc.astype(BF16)

    aspec = pl.BlockSpec((tt, tmo), lambda i, j, k: (k, i))
    bspec = pl.BlockSpec((tt, tno), lambda i, j, k: (k, j))
    ospec = pl.BlockSpec((tmo, tno), lambda i, j, k: (i, j))
    if nk == 1:
        return _matmul(name, (m // tmo, n // tno, 1), 1, None, [(a, aspec, b, bspec) for a in a_list], TN, [],
                       [(_sds((m, n), BF16), ospec)] * len(a_list), epi, False, None, comm)
    return _matmul(name, (m // tmo, n // tno, nk), nk, 2, [(a, aspec, b, bspec) for a in a_list], TN, [],
                   [(_sds((m, n), BF16), ospec)] * len(a_list), epi, False, (tmo, tno), comm)


def _mm_dy(dxb, w_out, comm=None):
    t, d = dxb.shape
    m = w_out.shape[0]
    tm, tn = _tile(t, _TM), _tile(m, _TN)

    def epi(accs, e, o):
        o[0][...] = accs[0].astype(BF16)

    outs, couts = _matmul(
        "mm_dy", (t // tm, m // tn), 1, None,
        [(dxb, pl.BlockSpec((tm, d), lambda i, j: (i, 0)), w_out, pl.BlockSpec((tn, d), lambda i, j: (j, 0)))], NT, [],
        [(_sds((t, m), BF16), pl.BlockSpec((tm, tn), lambda i, j: (i, j)))], epi, True, None, comm)
    return outs[0], couts


def _mm_dh(dz, w_in, comm=None):
    t, n = dz.shape
    d = w_in.shape[0]
    tm, tn = _tile(t, _TM), _tile(d, _TN)
    nk = 2
    tk = n // nk

    def epi(accs, e, o):
        o[0][...] = accs[0]

    outs, couts = _matmul(
        "mm_dh", (t // tm, d // tn, nk), nk, 2,
        [(dz, pl.BlockSpec((tm, tk), lambda i, j, k: (i, k)), w_in, pl.BlockSpec((tn, tk), lambda i, j, k: (j, k)))], NT,
        [], [(_sds((t, d), F32), pl.BlockSpec((tm, tn), lambda i, j, k: (i, j)))], epi, True, (tm, tn), comm)
    return outs[0], couts


def _rmsnorm_fwd(x, g, comm=None):
    t, d = x.shape
    tm = min(_TM_NORM, t)

    def body(x_ref, g_ref, o_ref):
        xv = x_ref[...]
        rs = lax.rsqrt(jnp.mean(xv * xv, axis=-1, keepdims=True) + RMS_EPS)
        o_ref[...] = (xv * rs * g_ref[...]).astype(BF16)

    outs, couts = _call_hosting(
        body, "rmsnorm_fwd", (t // tm,), [_sds((t, d), BF16)],
        [pl.BlockSpec((tm, d), lambda i: (i, 0)), pl.BlockSpec((1, d), lambda i: (0, 0))],
        [pl.BlockSpec((tm, d), lambda i: (i, 0))], [x, g], [], comm)
    return outs[0], couts


def _rmsnorm_bwd_math(xv, g, dh):
    rs = lax.rsqrt(jnp.mean(xv * xv, axis=-1, keepdims=True) + RMS_EPS)
    xh = xv * rs
    gd = dh * g
    dx = rs * (gd - xh * jnp.mean(gd * xh, axis=-1, keepdims=True))
    return dx, jnp.sum(dh * xh, axis=0, keepdims=True)


def _rmsnorm_bwd(x, g, dh, dres):
    t, d = x.shape
    tm = min(_TM_NORM, t)

    def body(x_ref, g_ref, dh_ref, dres_ref, dx_ref, dxb_ref, dg_ref):
        dx, dg = _rmsnorm_bwd_math(x_ref[...], g_ref[...], dh_ref[...])
        dx = dx + dres_ref[...]
        dx_ref[...] = dx
        dxb_ref[...] = dx.astype(BF16)

        @pl.when(pl.program_id(0) == 0)
        def _():
            dg_ref[...] = dg

        @pl.when(pl.program_id(0) > 0)
        def _():
            dg_ref[...] += dg

    row = pl.BlockSpec((tm, d), lambda i: (i, 0))
    vec = pl.BlockSpec((1, d), lambda i: (0, 0))
    return pl.pallas_call(
        body, name="rmsnorm_bwd", grid=(t // tm,),
        out_shape=(_sds((t, d), F32), _sds((t, d), BF16), _sds((1, d), F32)),
        in_specs=[row, vec, row, row], out_specs=(row, row, vec), compiler_params=_cparams(1))(x, g, dh, dres)


def _loss_head(x, g, target):
    t, d = x.shape
    tm = min(_TM_NORM, t)

    def body(x_ref, g_ref, t_ref, dx_ref, dxb_ref, dg_ref, loss_ref):
        xv, gv = x_ref[...], g_ref[...]
        rs = lax.rsqrt(jnp.mean(xv * xv, axis=-1, keepdims=True) + RMS_EPS)
        diff = xv * rs * gv - t_ref[...]
        part = 0.5 * jnp.sum(jnp.mean(diff * diff, axis=-1, keepdims=True), axis=0, keepdims=True)
        part = jnp.broadcast_to(part, (1, 128))
        dx, dg = _rmsnorm_bwd_math(xv, gv, diff * (1.0 / d))
        dx_ref[...] = dx
        dxb_ref[...] = dx.astype(BF16)

        @pl.when(pl.program_id(0) == 0)
        def _():
            dg_ref[...] = dg
            loss_ref[...] = part

        @pl.when(pl.program_id(0) > 0)
        def _():
            dg_ref[...] += dg
            loss_ref[...] += part

    row = pl.BlockSpec((tm, d), lambda i: (i, 0))
    vec = pl.BlockSpec((1, d), lambda i: (0, 0))
    return pl.pallas_call(
        body, name="loss_head", grid=(t // tm,),
        out_shape=(_sds((t, d), F32), _sds((t, d), BF16), _sds((1, d), F32), _sds((1, 128), F32)),
        in_specs=[row, vec, row], out_specs=(row, row, vec, pl.BlockSpec((1, 128), lambda i: (0, 0))),
        compiler_params=_cparams(1))(x, g, target)


def _gelu(x):
    th = jnp.tanh(GELU_C * (x + GELU_A * x * x * x))
    return 0.5 * x * (1.0 + th), th


def _gelu_grad(x, th):
    return 0.5 * (1.0 + th) + 0.5 * x * (1.0 - th * th) * GELU_C * (1.0 + 3.0 * GELU_A * x * x)


def _masked_ws(ws_ref, h):
    i = lax.broadcasted_iota(jnp.int32, (BLK, BLK), 0) // CHUNK
    j = lax.broadcasted_iota(jnp.int32, (BLK, BLK), 1) // CHUNK
    return jnp.where(j <= i, ws_ref[h], 0.0)


def _shift_down(q, n, first_rows):
    rolled = pltpu.roll(q, n, 0)
    row = lax.broadcasted_iota(jnp.int32, q.shape, 0)
    for r, val in enumerate(first_rows):
        rolled = jnp.where(row == r, val, rolled)
    return rolled


def _shift_up(q, n, last_rows):
    tm = q.shape[0]
    rolled = pltpu.roll(q, tm - n, 0)
    row = lax.broadcasted_iota(jnp.int32, q.shape, 0)
    for r, val in enumerate(last_rows):
        rolled = jnp.where(row == tm - n + r, val, rolled)
    return rolled


def _mixer_specs(t, a, tm):
    hb = tm // HALO
    last = t // HALO - 1
    tile = pl.BlockSpec((tm, 5 * a), lambda i: (i, 0))
    prev = [pl.BlockSpec((HALO, a), functools.partial(lambda i, col: (jnp.maximum(i * hb - 1, 0), col), col=col))
            for col in (3, 4)]
    nxt = [pl.BlockSpec((HALO, a), functools.partial(lambda i, col: (jnp.minimum((i + 1) * hb, last), col), col=col))
           for col in (2, 3, 4)]
    return tile, prev, nxt


def _group_a_fwd(zu, zv, lng, lnb, ws_ref, bb_ref, mixed_ref, vln_ref):
    u, thu = _gelu(zu)
    v, thv = _gelu(zv)
    mu = jnp.mean(v, axis=-1, keepdims=True)
    vc = v - mu
    rs = lax.rsqrt(jnp.mean(vc * vc, axis=-1, keepdims=True) + LN_EPS)
    vhat = vc * rs
    vln_ref[...] = vhat * lng + lnb
    tm, a = zu.shape
    hd = a // HEADS
    for h in range(HEADS):
        w = _masked_ws(ws_ref, h).astype(BF16)
        for b in range(tm // BLK):
            rows, cols = pl.ds(b * BLK, BLK), pl.ds(h * hd, hd)
            mixed_ref[rows, cols] = jnp.dot(w, vln_ref[rows, cols].astype(BF16), preferred_element_type=F32) + bb_ref[h]
    return u, thu, thv, rs, vhat


def _mixer_fwd(z, ln_g, ln_b, w_spatial, bb, conv_w, gg, comm=None):
    t = z.shape[0]
    a = z.shape[1] // 5
    tm = min(_TM_MIX, t)
    tile, prev, _ = _mixer_specs(t, a, tm)

    def body(z_ref, pc_ref, ph_ref, lng_ref, lnb_ref, ws_ref, bb_ref, cw_ref, gg_ref, y_ref, mixed_ref, vln_ref):
        i = pl.program_id(0)
        zu = z_ref[:, 0:a].astype(F32)
        zv = z_ref[:, a:2 * a].astype(F32)
        u, _, _, _, _ = _group_a_fwd(zu, zv, lng_ref[...], lnb_ref[...], ws_ref, bb_ref, mixed_ref, vln_ref)
        ya = u * mixed_ref[...]
        ra = lax.rsqrt(jnp.mean(ya * ya, axis=-1, keepdims=True) + RMS_EPS)
        y_ref[:, 0:a] = (ya * ra * gg_ref[:, 0:a]).astype(BF16)

        zb = z_ref[:, 2 * a:3 * a].astype(F32)
        q = z_ref[:, 3 * a:4 * a].astype(F32) * z_ref[:, 4 * a:5 * a].astype(F32)
        qp = jnp.where(i > 0, pc_ref[...].astype(F32) * ph_ref[...].astype(F32), 0.0)
        qm1 = _shift_down(q, 1, [qp[HALO - 1:HALO]])
        qm2 = _shift_down(q, 2, [qp[HALO - 2:HALO - 1], qp[HALO - 1:HALO]])
        cv = cw_ref[0:1, :] * qm2 + cw_ref[1:2, :] * qm1 + cw_ref[2:3, :] * q
        yb = zb * cv
        rb = lax.rsqrt(jnp.mean(yb * yb, axis=-1, keepdims=True) + RMS_EPS)
        y_ref[:, a:2 * a] = (yb * rb * gg_ref[:, a:2 * a]).astype(BF16)

    full = lambda shape: pl.BlockSpec(shape, lambda i: (0,) * len(shape))
    outs, couts = _call_hosting(
        body, "mixer_fwd", (t // tm,), [_sds((t, 2 * a), BF16)],
        [tile, *prev, full((1, a)), full((1, a)), full(w_spatial.shape), full(bb.shape), full(conv_w.shape),
         full((1, 2 * a))],
        [pl.BlockSpec((tm, 2 * a), lambda i: (i, 0))], [z, z, z, ln_g, ln_b, w_spatial, bb, conv_w, gg],
        [pltpu.VMEM((tm, a), F32), pltpu.VMEM((tm, a), F32)], comm)
    return outs[0], couts


def _mixer_bwd(z, dy, ln_g, ln_b, w_spatial, bb, conv_w, gg, comm=None):
    t = z.shape[0]
    a = z.shape[1] // 5
    hd = a // HEADS
    tm = min(_TM_MIX, t)
    n_tiles = t // tm
    tile, prev, nxt = _mixer_specs(t, a, tm)
    hb = tm // HALO
    dy_tile = pl.BlockSpec((tm, 2 * a), lambda i: (i, 0))
    dy_next = pl.BlockSpec((HALO, a), lambda i: (jnp.minimum((i + 1) * hb, t // HALO - 1), 1))

    def body(z_ref, pc_ref, ph_ref, nb_ref, nc_ref, nh_ref, dy_ref, ndy_ref, lng_ref, lnb_ref, ws_ref, bb_ref, cw_ref,
             gg_ref, dz_ref, dlng_ref, dlnb_ref, dws_ref, dbb_ref, dcw_ref, dgg_ref, mixed_ref, vln_ref, dmix_ref,
             dvln_ref):
        i = pl.program_id(0)

        @pl.when(i == 0)
        def _():
            for ref in (dlng_ref, dlnb_ref, dws_ref, dbb_ref, dcw_ref, dgg_ref):
                ref[...] = jnp.zeros(ref.shape, F32)

        lng = lng_ref[...]
        zu = z_ref[:, 0:a].astype(F32)
        zv = z_ref[:, a:2 * a].astype(F32)
        u, thu, thv, rs, vhat = _group_a_fwd(zu, zv, lng, lnb_ref[...], ws_ref, bb_ref, mixed_ref, vln_ref)
        mixed = mixed_ref[...]
        ya = u * mixed
        ra = lax.rsqrt(jnp.mean(ya * ya, axis=-1, keepdims=True) + RMS_EPS)
        da = dy_ref[:, 0:a].astype(F32)
        yah = ya * ra
        dgg_ref[:, 0:a] += jnp.sum(da * yah, axis=0, keepdims=True)
        ga = da * gg_ref[:, 0:a]
        dya = ra * (ga - yah * jnp.mean(ga * yah, axis=-1, keepdims=True))
        dz_ref[:, 0:a] = (dya * mixed * _gelu_grad(zu, thu)).astype(BF16)
        dmix_ref[...] = dya * u
        for h in range(HEADS):
            w = _masked_ws(ws_ref, h).astype(BF16)
            dw = jnp.zeros((BLK, BLK), F32)
            db = jnp.zeros((BLK, hd), F32)
            for b in range(tm // BLK):
                rows, cols = pl.ds(b * BLK, BLK), pl.ds(h * hd, hd)
                dm = dmix_ref[rows, cols]
                dmb = dm.astype(BF16)
                db = db + dm
                dw = dw + lax.dot_general(dmb, vln_ref[rows, cols].astype(BF16), (NT, ((), ())),
                                          preferred_element_type=F32)
                dvln_ref[rows, cols] = lax.dot_general(w, dmb, (TN, ((), ())), preferred_element_type=F32)
            dws_ref[h] += dw
            dbb_ref[h] += db
        dvln = dvln_ref[...]
        dlng_ref[...] += jnp.sum(dvln * vhat, axis=0, keepdims=True)
        dlnb_ref[...] += jnp.sum(dvln, axis=0, keepdims=True)
        dvh = dvln * lng
        dv = rs * (dvh - jnp.mean(dvh, axis=-1, keepdims=True) - vhat * jnp.mean(dvh * vhat, axis=-1, keepdims=True))
        dz_ref[:, a:2 * a] = (dv * _gelu_grad(zv, thv)).astype(BF16)

        w0, w1, w2 = cw_ref[0:1, :], cw_ref[1:2, :], cw_ref[2:3, :]
        ggb = gg_ref[:, a:2 * a]
        zb = z_ref[:, 2 * a:3 * a].astype(F32)
        zc = z_ref[:, 3 * a:4 * a].astype(F32)
        zh = z_ref[:, 4 * a:5 * a].astype(F32)
        q = zc * zh
        qp = jnp.where(i > 0, pc_ref[...].astype(F32) * ph_ref[...].astype(F32), 0.0)
        qm1 = _shift_down(q, 1, [qp[HALO - 1:HALO]])
        qm2 = _shift_down(q, 2, [qp[HALO - 2:HALO - 1], qp[HALO - 1:HALO]])
        cv = w0 * qm2 + w1 * qm1 + w2 * q

        def conv_out_grad(zb_, cv_, dout_):
            yb = zb_ * cv_
            rb = lax.rsqrt(jnp.mean(yb * yb, axis=-1, keepdims=True) + RMS_EPS)
            ybh = yb * rb
            gb = dout_ * ggb
            dyb = rb * (gb - ybh * jnp.mean(gb * ybh, axis=-1, keepdims=True))
            return dyb * zb_, dyb * cv_, ybh

        db_out = dy_ref[:, a:2 * a].astype(F32)
        g, dzb, ybh = conv_out_grad(zb, cv, db_out)
        dgg_ref[:, a:2 * a] += jnp.sum(db_out * ybh, axis=0, keepdims=True)
        dz_ref[:, 2 * a:3 * a] = dzb.astype(BF16)
        qn = nc_ref[...].astype(F32) * nh_ref[...].astype(F32)
        zbn = nb_ref[...].astype(F32)
        cvn = w0 * _shift_down(qn, 2, [q[tm - 2:tm - 1], q[tm - 1:tm]]) + w1 * _shift_down(qn, 1, [q[tm - 1:tm]]) + w2 * qn
        gn, _, _ = conv_out_grad(zbn, cvn, ndy_ref[...].astype(F32))
        gn = jnp.where(i < n_tiles - 1, gn, 0.0)
        dq = w2 * g + w1 * _shift_up(g, 1, [gn[0:1]]) + w0 * _shift_up(g, 2, [gn[0:1], gn[1:2]])
        dz_ref[:, 3 * a:4 * a] = (dq * zh).astype(BF16)
        dz_ref[:, 4 * a:5 * a] = (dq * zc).astype(BF16)
        dcw_ref[0:1, :] += jnp.sum(g * qm2, axis=0, keepdims=True)
        dcw_ref[1:2, :] += jnp.sum(g * qm1, axis=0, keepdims=True)
        dcw_ref[2:3, :] += jnp.sum(g * q, axis=0, keepdims=True)

        @pl.when(i == n_tiles - 1)
        def _():
            for h in range(HEADS):
                dbb_ref[h] = jnp.broadcast_to(jnp.sum(dbb_ref[h], axis=1, keepdims=True), (BLK, hd))
                dws_ref[h] = _masked_ws(dws_ref, h)

    full = lambda shape: pl.BlockSpec(tuple(shape), lambda i: (0,) * len(shape))
    out_shapes = (_sds((t, 5 * a), BF16), _sds((1, a), F32), _sds((1, a), F32), _sds(w_spatial.shape, F32),
                  _sds(bb.shape, F32), _sds((8, a), F32), _sds((1, 2 * a), F32))
    return _call_hosting(
        body, "mixer_bwd", (n_tiles,), out_shapes,
        [tile, *prev, *nxt, dy_tile, dy_next, full((1, a)), full((1, a)), full(w_spatial.shape), full(bb.shape),
         full(conv_w.shape), full((1, 2 * a))],
        [tile, *[full(s.shape) for s in out_shapes[1:]]], [z, z, z, z, z, z, dy, dy, ln_g, ln_b, w_spatial, bb, conv_w, gg],
        [pltpu.VMEM((tm, a), F32)] * 4, comm)


def _all_reduce_small(pack, comm=None):
    r = pack.shape[0]
    hosted = _Hosted(comm, 1, 1)
    n_ci, n_co = len(hosted.operands), len(hosted.out_shapes)

    def body(*refs):
        in_ref, c_ins, out_ref, c_outs = refs[0], refs[1:1 + n_ci], refs[1 + n_ci], refs[2 + n_ci:2 + n_ci + n_co]
        acc_ref, recv_ref, send_sems, recv_sems = refs[2 + n_ci + n_co:6 + n_ci + n_co]
        sems = refs[6 + n_ci + n_co:]
        hosted.run("start", c_ins, c_outs, sems)
        x, y, c = _place()
        partners = [(x, y, 1 - c), (1 - x, y, c), (x, 1 - y, c)]
        acc_ref[0] = in_ref[...]
        for s, partner in enumerate(partners):
            cp = pltpu.make_async_remote_copy(
                src_ref=acc_ref.at[s], dst_ref=recv_ref.at[s], send_sem=send_sems.at[s], recv_sem=recv_sems.at[s],
                device_id=partner, device_id_type=MESH)
            cp.start()
            cp.wait()
            if s < 2:
                acc_ref[s + 1] = acc_ref[s] + recv_ref[s]
            else:
                out_ref[...] = acc_ref[s] + recv_ref[s]
        for stage in ("mid1", "mid2", "finish"):
            hosted.run(stage, c_ins, c_outs, sems)

    vmem = pl.BlockSpec(memory_space=pltpu.VMEM)
    res = pl.pallas_call(
        body, name="all_reduce_small", out_shape=tuple([_sds(pack.shape, F32)] + hosted.out_shapes),
        in_specs=[vmem] + hosted.in_specs, out_specs=tuple([vmem] + hosted.out_specs),
        input_output_aliases=hosted.aliases,
        scratch_shapes=[pltpu.VMEM((3, r, 128), F32), pltpu.VMEM((3, r, 128), F32), pltpu.SemaphoreType.DMA((3,)),
                        pltpu.SemaphoreType.DMA((3,))] + hosted.scratch,
        compiler_params=pltpu.CompilerParams(vmem_limit_bytes=VMEM_LIMIT_V7X),
    )(pack, *hosted.operands)
    return res[0], list(res[1:])


def _adamw_math(w, g, m, v):
    m = ADAM_B1 * m + (1.0 - ADAM_B1) * g
    v = ADAM_B2 * v + (1.0 - ADAM_B2) * (g * g)
    m_hat = m / (1.0 - ADAM_B1 ** ADAM_STEP)
    v_hat = v / (1.0 - ADAM_B2 ** ADAM_STEP)
    delta = -ADAM_LR * (m_hat / (jnp.sqrt(v_hat) + ADAM_EPS) + ADAM_WD * w)
    return delta, m, v


def _adamw_big(name, land, w, m, v, comm=None):
    nl, n_slots, r, c = land.shape
    tr = max(8, min(r, (256 * 640) // c // 8 * 8))
    while r % tr:
        tr -= 8
    grid = (nl, r // tr)
    hosted = _Hosted(comm, 4, 4)
    n_ci, n_co = len(hosted.operands), len(hosted.out_shapes)

    def body(*refs):
        land_ref, w_ref, m_ref, v_ref = refs[:4]
        c_ins = refs[4:4 + n_ci]
        g_out, d_out, m_out, v_out = refs[4 + n_ci:8 + n_ci]
        c_outs = refs[8 + n_ci:8 + n_ci + n_co]
        sems = refs[8 + n_ci + n_co:]

        def compute():
            g = land_ref[0].astype(F32)
            for s in range(1, n_slots):
                g = g + land_ref[s].astype(F32)
            delta, mn, vn = _adamw_math(w_ref[...], g, m_ref[...], v_ref[...])
            g_out[...] = g
            d_out[...] = delta
            m_out[...] = mn
            v_out[...] = vn

        hosted.wrap(grid, compute, c_ins, c_outs, sems)

    blk = pl.BlockSpec((None, tr, c), lambda l, i: (l, i, 0))
    res = pl.pallas_call(
        body, name=name, grid=grid, out_shape=tuple([_sds((nl, r, c), F32)] * 4 + hosted.out_shapes),
        in_specs=[pl.BlockSpec((None, n_slots, tr, c), lambda l, i: (l, 0, i, 0)), blk, blk, blk] + hosted.in_specs,
        out_specs=tuple([blk] * 4 + hosted.out_specs), input_output_aliases=hosted.aliases,
        scratch_shapes=hosted.scratch, compiler_params=_cparams(2))(land, w, m, v, *hosted.operands)
    return list(res[:4]), list(res[4:])


def _adamw_small(g, w, m, v):
    def body(g_ref, w_ref, m_ref, v_ref, d_out, m_out, v_out):
        delta, mn, vn = _adamw_math(w_ref[...], g_ref[...], m_ref[...], v_ref[...])
        d_out[...] = delta
        m_out[...] = mn
        v_out[...] = vn

    return pl.pallas_call(body, name="adamw_small", out_shape=tuple([_sds(g.shape, F32)] * 3),
                          compiler_params=pltpu.CompilerParams(vmem_limit_bytes=VMEM_LIMIT_V7X))(g, w, m, v)


def _rows(a):
    return a.reshape(-1, 128)


BIG = ["w_in", "w_out", "w_gate", "w_up", "w_down"]
AG_HOSTS = {
    ("norm1", 0): [("w_in", 0), ("conv_w", 0)],
    ("mm_in", 0): [("w_out", 0), ("w_gate", 0, 0, 2)], ("mixer", 0): [("w_gate", 0, 1, 2)],
    ("mm_out", 0): [("w_up", 0, 0, 2)], ("norm2", 0): [("w_up", 0, 1, 2)],
    ("mm_swiglu", 0): [("w_down", 0), ("w_in", 1), ("w_out", 1)], ("mm_down", 0): [("w_gate", 1)],
    ("mm_in", 1): [("w_up", 1)], ("mm_swiglu", 1): [("w_down", 1)],
}


def kernel(x, norm1_g, w_in, gmlp_ln_g, gmlp_ln_b, w_spatial, b_spatial, conv_w, group_norm_g, w_out, norm2_g, w_gate, w_up, w_down, final_norm_g, loss_target, m_norm1_g, m_w_in, m_gmlp_ln_g, m_gmlp_ln_b, m_w_spatial, m_b_spatial, m_conv_w, m_group_norm_g, m_w_out, m_norm2_g, m_w_gate, m_w_up, m_w_down, m_final_norm_g, v_norm1_g, v_w_in, v_gmlp_ln_g, v_gmlp_ln_b, v_w_spatial, v_b_spatial, v_conv_w, v_group_norm_g, v_w_out, v_norm2_g, v_w_gate, v_w_up, v_w_down, v_final_norm_g):
    nl = N_LAYERS
    t, d = x.shape[1], x.shape[2]
    a = d // 2
    hd = a // HEADS
    xin = x.reshape(t, d)
    target = loss_target.reshape(t, d)
    me = _index(_place())

    tr = lambda w: jnp.transpose(w, (0, 2, 1))
    big = {"w_in": w_in, "w_out": w_out, "w_gate": tr(w_gate), "w_up": tr(w_up), "w_down": w_down}
    big_m = {"w_in": m_w_in, "w_out": m_w_out, "w_gate": tr(m_w_gate), "w_up": tr(m_w_up), "w_down": m_w_down}
    big_v = {"w_in": v_w_in, "w_out": v_w_out, "w_gate": tr(v_w_gate), "w_up": tr(v_w_up), "w_down": v_w_down}
    block = {k: big[k].shape[1:] for k in BIG}
    view = {k: _cols_view(block[k][1]) if k == "w_in" else _rows_view(block[k][0]) for k in BIG}
    full_shape = {k: (block[k][0], N_DEV * block[k][1]) if k == "w_in" else (N_DEV * block[k][0], block[k][1])
                  for k in BIG}

    weights = {}
    shards = {(k, l): big[k][l].astype(BF16) for k in BIG for l in range(nl)}

    def ag_spec(k, l, part=0, n_parts=1):
        if k == "conv_w":
            return (conv_w, _sds((N_DEV, *conv_w.shape), F32), _SLOT_WHOLE, (0,), None)
        halves = (_cols_halves(*block[k], part, n_parts) if k == "w_in" else _rows_halves(block[k][0], part, n_parts))
        return (shards[(k, l)], _sds(full_shape[k], BF16), halves, (0, 1), weights.get((k, l)))

    bb = jnp.broadcast_to(b_spatial[..., None], (nl, HEADS, BLK, hd))

    def hosted(name, l):
        keys = AG_HOSTS.get((name, l), [])
        return keys, ([_ag_piece([ag_spec(*key) for key in keys])] if keys else None)

    def landed(keys, couts):
        for key, arr in zip(keys, couts):
            weights[key[:2]] = arr

    saved = []
    xl = xin
    for l in range(nl):
        keys, comm = hosted("norm1", l)
        h, couts = _rmsnorm_fwd(xl, norm1_g[l:l + 1], comm)
        landed(keys, couts)
        if l == 0:
            conv_full = jnp.transpose(weights[("conv_w", 0)], (1, 2, 0, 3)).reshape(nl, 3, a)
        keys, comm = hosted("mm_in", l)
        z, couts = _mm_in(h, weights[("w_in", l)], comm)
        landed(keys, couts)
        keys, comm = hosted("mixer", l)
        y, couts = _mixer_fwd(z, gmlp_ln_g[l:l + 1], gmlp_ln_b[l:l + 1], w_spatial[l], bb[l], conv_full[l],
                              group_norm_g[l:l + 1], comm)
        landed(keys, couts)
        keys, comm = hosted("mm_out", l)
        x1, couts = _mm_out(y, weights[("w_out", l)], xl, comm)
        landed(keys, couts)
        keys, comm = hosted("norm2", l)
        h2, couts = _rmsnorm_fwd(x1, norm2_g[l:l + 1], comm)
        landed(keys, couts)
        keys, comm = hosted("mm_swiglu", l)
        (act, dact_dgate, dact_dup), couts = _mm_swiglu(h2, weights[("w_gate", l)], weights[("w_up", l)], comm)
        landed(keys, couts)
        keys, comm = hosted("mm_down", l)
        x2, couts = _mm_down(act, weights[("w_down", l)], x1, comm)
        landed(keys, couts)
        saved.append(dict(x=xl, h=h, z=z, y=y, x1=x1, h2=h2, dact_dgate=dact_dgate, dact_dup=dact_dup, act=act))
        xl = x2

    dx, dxb, d_final_g, loss_part = _loss_head(xl, final_norm_g.reshape(1, d), target)
    small = [None] * nl
    core = lax.axis_index("c").astype(jnp.int32).reshape(1)
    stage_shape = {k: _sds((N_CHIPS, *block[k]), BF16) for k in BIG}
    land_shape = {k: _sds((nl, N_CHIPS, *block[k]), BF16) for k in BIG}
    grads = [dict() for _ in range(nl)]
    stages = [dict() for _ in range(nl)]
    sums = [dict() for _ in range(nl)]
    lands = {k: None for k in BIG}

    def core_job(l, keys):
        def sink(outs):
            stages[l].update(zip(keys, outs))
        return _rs_core_piece([(grads[l][k], stage_shape[k], view[k]) for k in keys]), sink

    def chip_job(l, items):
        keys = [item[0] for item in items]

        def rows(k, p0, p1, n_parts):
            per = block[k][0] // n_parts
            return (p0 * per, (p1 - p0) * per)

        def sink(outs):
            lands.update(zip(keys, outs))
        return _rs_chip_piece([(sums[l][k], land_shape[k], rows(k, p0, p1, n_parts), lands[k])
                               for k, p0, p1, n_parts in items], l), sink

    def add_up(l, keys):
        for k in keys:
            sums[l][k] = _chip_sums(f"chip_sums_{k}", grads[l][k], stages[l][k], k == "w_in", core)

    def host(*jobs):
        def deliver(couts):
            i = 0
            for piece, sink in jobs:
                n_out = len(piece.out_shapes)
                sink(couts[i:i + n_out])
                i += n_out
        return [piece for piece, _ in jobs], deliver

    whole = lambda k: (k, 0, 1, 1)
    rep = ["norm1_g", "gmlp_ln_g", "gmlp_ln_b", "w_spatial", "b_spatial", "group_norm_g", "norm2_g"]
    rep_w = dict(norm1_g=norm1_g, gmlp_ln_g=gmlp_ln_g, gmlp_ln_b=gmlp_ln_b, w_spatial=w_spatial, b_spatial=b_spatial,
                 group_norm_g=group_norm_g, norm2_g=norm2_g)
    rep_m = dict(norm1_g=m_norm1_g, gmlp_ln_g=m_gmlp_ln_g, gmlp_ln_b=m_gmlp_ln_b, w_spatial=m_w_spatial,
                 b_spatial=m_b_spatial, group_norm_g=m_group_norm_g, norm2_g=m_norm2_g)
    rep_v = dict(norm1_g=v_norm1_g, gmlp_ln_g=v_gmlp_ln_g, gmlp_ln_b=v_gmlp_ln_b, w_spatial=v_w_spatial,
                 b_spatial=v_b_spatial, group_norm_g=v_group_norm_g, norm2_g=v_norm2_g)

    def small_grad_parts():
        parts = [_rows(jnp.stack([small[l][k].reshape(rep_w[k].shape[1:]) for l in range(nl)])) for k in rep]
        parts.append(_rows(d_final_g))
        parts.append(_rows(jnp.stack([small[l]["conv_w"] for l in range(nl)])))
        parts.append(jnp.broadcast_to(loss_part, (8, 128)))
        return parts

    for l in reversed(range(nl)):
        s = saved[l]
        wi, wo, wgt, wut, wd = [weights[(k, l)] for k in BIG]
        later = l + 1 < nl
        comm, deliver = host(chip_job(l + 1, [("w_in", 0, 1, 2)])) if later else host()
        (grads[l]["w_down"],), couts = _mm_dw("mm_dw_down", [s["act"]], dxb, 2816, 1024, comm)
        deliver(couts)
        comm, deliver = (host(core_job(l, ["w_down"]), chip_job(l + 1, [("w_in", 1, 2, 2)])) if later
                         else host(core_job(l, ["w_down"])))
        (dgate, dup), couts = _mm_dact(dxb, wd, s["dact_dgate"], s["dact_dup"], comm)
        deliver(couts)
        add_up(l, ["w_down"])
        comm, deliver = host(chip_job(l, [("w_down", 0, 3, 4)]))
        (grads[l]["w_gate"],), couts = _mm_dw("mm_dw_gate", [dgate], s["h2"], 2816, 1024, comm)
        deliver(couts)
        comm, deliver = host(chip_job(l, [("w_down", 3, 4, 4)]), core_job(l, ["w_gate"]))
        (grads[l]["w_up"],), couts = _mm_dw("mm_dw_up", [dup], s["h2"], 2816, 1024, comm)
        deliver(couts)
        add_up(l, ["w_gate"])
        comm, deliver = host(chip_job(l, [whole("w_gate")]), core_job(l, ["w_up"]))
        dh2, couts = _mm_dh2(dgate, dup, wgt, wut, comm)
        deliver(couts)
        add_up(l, ["w_up"])
        dx1, dx1b, d_n2 = _rmsnorm_bwd(s["x1"], norm2_g[l:l + 1], dh2, dx)
        comm, deliver = host(chip_job(l, [("w_up", 0, 1, 4)]))
        dy, couts = _mm_dy(dx1b, wo, comm)
        deliver(couts)
        comm, deliver = host(chip_job(l, [("w_up", 1, 2, 4)]))
        (grads[l]["w_out"],), couts = _mm_dw("mm_dw_out", [s["y"]], dx1b, 1024, 1024, comm)
        deliver(couts)
        comm, deliver = host(chip_job(l, [("w_up", 2, 4, 4)]), core_job(l, ["w_out"]))
        (dz, d_lng, d_lnb, d_ws, d_bb, d_cw, d_gg), couts = _mixer_bwd(
            s["z"], dy, gmlp_ln_g[l:l + 1], gmlp_ln_b[l:l + 1], w_spatial[l], bb[l], conv_full[l], group_norm_g[l:l + 1],
            comm)
        deliver(couts)
        add_up(l, ["w_out"])
        small[l] = dict(norm1_g=jnp.zeros((1, d), F32), gmlp_ln_g=d_lng, gmlp_ln_b=d_lnb, w_spatial=d_ws,
                        b_spatial=d_bb[:, :, 0], group_norm_g=d_gg, norm2_g=d_n2, conv_w=d_cw[0:3])
        jobs = [chip_job(l, [whole("w_out")])]
        if l == 0:
            parts = small_grad_parts()
            all_reduce = _all_reduce_piece(jnp.concatenate(parts, axis=0))
            reduced = []
            jobs.append((all_reduce, reduced.extend))
        comm, deliver = host(*jobs)
        (grads[l]["w_in"],), couts = _mm_dw("mm_dw_in", [s["h"]], dz, 2048, 1024, comm)
        deliver(couts)
        comm, deliver = host(core_job(l, ["w_in"]))
        dh, couts = _mm_dh(dz, wi, comm)
        deliver(couts)
        add_up(l, ["w_in"])
        dx, dxb, small[l]["norm1_g"] = _rmsnorm_bwd(s["x"], norm1_g[l:l + 1], dh, dx1)
    grad_x = dx.reshape(x.shape)

    sizes = [p.shape[0] for p in parts]
    comm, deliver = host(chip_job(0, [whole("w_in")]))
    last, couts = _all_reduce_small(_rows(small[0]["norm1_g"]), comm)
    deliver(couts)
    total = lax.dynamic_update_slice(reduced[0], last, (0, 0))
    offs = [0]
    for n in sizes:
        offs.append(offs[-1] + n)
    pieces = [total[offs[i]:offs[i + 1]] for i in range(len(parts))]
    loss = pieces[-1][0, 0]
    conv_g_full = pieces[-2].reshape(nl, 3, N_DEV, a // N_DEV)
    conv_g = lax.dynamic_index_in_dim(conv_g_full, me, axis=2, keepdims=False)
    n_rep = offs[len(rep) + 1]
    pad = jnp.zeros((2, 128), F32)

    def small_pack(named, final, conv):
        return jnp.concatenate([_rows(named[k]) for k in rep] + [_rows(final), _rows(conv), pad], axis=0)

    g_small = jnp.concatenate([total[:n_rep], _rows(conv_g), pad], axis=0)
    d_small, m_small, v_small = _adamw_small(
        g_small, small_pack(rep_w, final_norm_g, conv_w), small_pack(rep_m, m_final_norm_g, m_conv_w),
        small_pack(rep_v, v_final_norm_g, v_conv_w))

    def unpack(packed):
        out = {k: packed[offs[i]:offs[i + 1]].reshape(rep_w[k].shape) for i, k in enumerate(rep)}
        out["final_norm_g"] = packed[offs[len(rep)]:n_rep].reshape(final_norm_g.shape)
        out["conv_w"] = packed[n_rep:n_rep + 6].reshape(conv_w.shape)
        return out

    res = {"grad": unpack(g_small), "delta": unpack(d_small), "m": unpack(m_small), "v": unpack(v_small)}

    for k in BIG:
        outs, _ = _adamw_big(f"adamw_{k}", lands[k], big[k], big_m[k], big_v[k])
        if k in ("w_gate", "w_up"):
            outs = [tr(o) for o in outs]
        res["grad"][k], res["delta"][k], res["m"][k], res["v"][k] = outs

    order = ["norm1_g", "w_in", "gmlp_ln_g", "gmlp_ln_b", "w_spatial", "b_spatial", "conv_w", "group_norm_g", "w_out",
             "norm2_g", "w_gate", "w_up", "w_down", "final_norm_g"]
    return (loss, grad_x, *[res["grad"][k] for k in order], *[res["delta"][k] for k in order],
            *[res["m"][k] for k in order], *[res["v"][k] for k in order])
```

```python
import functools
import math
import operator

import jax
import jax.numpy as jnp
from jax import lax
from jax.experimental import pallas as pl
from jax.experimental.pallas import tpu as pltpu

F32 = jnp.float32
BF16 = jnp.bfloat16
MESH = pl.DeviceIdType.MESH

N_DEV = 8
N_LAYERS = 2
HEADS = 8
BLK = 128
CHUNK = 64
HALO = 16
RMS_EPS = 1e-6
LN_EPS = 1e-5
ADAM_LR, ADAM_B1, ADAM_B2, ADAM_EPS, ADAM_WD, ADAM_STEP = 0.001, 0.9, 0.999, 1e-8, 0.01, 10
GELU_C = math.sqrt(2.0 / math.pi)
GELU_A = 0.044715

VMEM_LIMIT_V7X = 56 * 1024 * 1024
_TM = 1024
_TN = 1024
_TT = 1024
_TM_MIX = 256
_TM_NORM = 512


def _cparams(n_axes):
    return pltpu.CompilerParams(dimension_semantics=("arbitrary",) * n_axes, vmem_limit_bytes=VMEM_LIMIT_V7X)


def _sds(shape, dtype):
    return jax.ShapeDtypeStruct(tuple(shape), dtype)


def _place():
    return lax.axis_index("x"), lax.axis_index("y"), lax.axis_index("c")


def _index(place):
    return 4 * place[0] + 2 * place[1] + place[2]


class _Piece:
    def __init__(self, operands, out_shapes, aliases, n_sems, start, finish, mid1=None, mid2=None, vmem=(),
                 hooks=(0.6, 0.87)):
        self.operands, self.out_shapes, self.aliases, self.n_sems = list(operands), list(out_shapes), dict(aliases), n_sems
        self.vmem = list(vmem)
        self.hooks = hooks
        nothing = lambda ctx: None
        self.start, self.mid1, self.mid2, self.finish = start, mid1 or nothing, mid2 or nothing, finish


class _Ctx:
    def __init__(self, ins, outs, sems, offs):
        self.ins, self.outs, self.sems = ins, outs, sems
        self.o_in, self.o_out, self.o_send, self.o_recv, self.o_loc, self.o_vmem = offs

    def vmem(self, i):
        return self.sems[3 + self.o_vmem + i]

    def inp(self, i):
        return self.ins[self.o_in + i]

    def out(self, i):
        return self.outs[self.o_out + i]

    def send(self, k):
        return self.sems[0].at[self.o_send + k]

    def recv(self, k):
        return self.sems[1].at[self.o_recv + k]

    def local(self, k):
        return self.sems[2].at[self.o_loc + k]


class _Hosted:
    def __init__(self, pieces, n_in_before, n_out_before):
        self.pieces = [p for p in (pieces or []) if p is not None]
        self.operands, self.out_shapes, self.aliases, self.offs = [], [], {}, []
        counts, vmem = [0, 0, 0], []
        for p in self.pieces:
            self.offs.append((len(self.operands), len(self.out_shapes), *counts, len(vmem)))
            for i, j in p.aliases.items():
                self.aliases[n_in_before + len(self.operands) + i] = n_out_before + len(self.out_shapes) + j
            self.operands += p.operands
            self.out_shapes += p.out_shapes
            counts = [c + n for c, n in zip(counts, p.n_sems)]
            vmem += p.vmem
        hbm = pl.BlockSpec(memory_space=pl.ANY)
        self.in_specs = [hbm] * len(self.operands)
        self.out_specs = [hbm] * len(self.out_shapes)
        self.scratch = ([pltpu.SemaphoreType.DMA((max(c, 1),)) for c in counts] + vmem) if self.pieces else []

    def run(self, stage, ins, outs, sems):
        for p, offs in zip(self.pieces, self.offs):
            getattr(p, stage)(_Ctx(ins, outs, sems, offs))

    def wrap(self, grid, compute, ins, outs, sems):
        if not self.pieces:
            compute()
            return
        n_steps = math.prod(grid)
        lin = 0
        for ax, g in enumerate(grid):
            lin = lin * g + pl.program_id(ax)
        pl.when(lin == 0)(lambda: self.run("start", ins, outs, sems))
        compute()
        for stage, which in (("mid1", 0), ("mid2", 1)):
            for p, offs in zip(self.pieces, self.offs):
                at = min(n_steps - 1, int(p.hooks[which] * n_steps))
                pl.when(lin == at)(functools.partial(getattr(p, stage), _Ctx(ins, outs, sems, offs)))
        pl.when(lin == n_steps - 1)(lambda: self.run("finish", ins, outs, sems))


def _cols_view(width):
    return lambda ref, p: ref.at[:, pl.ds(pl.multiple_of(p * width, 128), width)]


def _rows_view(height):
    return lambda ref, p: ref.at[pl.ds(pl.multiple_of(p * height, 16), height), :]


def _cols_halves(rows, width, part, n_parts):
    hr = rows // n_parts // 2
    at = lambda h: pl.ds(part * 2 * hr + h * hr, hr)
    return (lambda ref, p, h: ref.at[at(h), pl.ds(pl.multiple_of(p * width, 128), width)],
            lambda ref, h: ref.at[at(h), :], 2)


def _rows_halves(height, part, n_parts):
    hh = height // n_parts // 2
    return (lambda ref, p, h: ref.at[pl.ds(pl.multiple_of(p * height + part * 2 * hh + h * hh, 16), hh), :],
            lambda ref, h: ref.at[pl.ds(part * 2 * hh + h * hh, hh), :], 2)


_SLOT_WHOLE = (lambda ref, p, h: ref.at[p], lambda ref, h: ref, 1)


def _ag_piece(specs):
    units = [(a, h) for a, s in enumerate(specs) for h in s[3]]

    def plan(ctx):
        x, y, c = _place()
        me, sib, xn, yn, dg = (x, y, c), (x, y, 1 - c), (1 - x, y, c), (x, 1 - y, c), (1 - x, 1 - y, c)

        def copy(u, k, block, to, from_shard=False):
            a, h = units[u]
            dst_of, src_of, _ = specs[a][2]
            dst = dst_of(ctx.out(a), _index(block), h)
            return pltpu.make_async_remote_copy(
                src_ref=src_of(ctx.inp(a), h) if from_shard else dst, dst_ref=dst, send_sem=ctx.send(7 * u + k),
                recv_sem=ctx.recv(7 * u + k), device_id=to, device_id_type=MESH)

        def local(u):
            a, h = units[u]
            dst_of, src_of, _ = specs[a][2]
            return pltpu.make_async_copy(src_of(ctx.inp(a), h), dst_of(ctx.out(a), _index(me), h), ctx.local(u))

        def relay(u):
            return copy(u, 3, xn, yn) if units[u][1] % 2 == 0 else copy(u, 3, yn, xn)

        return me, sib, xn, yn, dg, c, copy, local, relay

    def start(ctx):
        me, sib, xn, yn, dg, c, copy, local, relay = plan(ctx)
        for u in range(len(units)):
            local(u).start()
            for k, to in enumerate((sib, xn, yn)):
                copy(u, k, me, to, from_shard=True).start()

    def mid1(ctx):
        me, sib, xn, yn, dg, c, copy, local, relay = plan(ctx)
        for u in range(len(units)):
            copy(u, 1, xn, me).wait_recv()
            copy(u, 2, yn, me).wait_recv()
            relay(u).start()
            copy(u, 4, xn, sib).start()
            copy(u, 5, yn, sib).start()

    def mid2(ctx):
        me, sib, xn, yn, dg, c, copy, local, relay = plan(ctx)
        for u in range(len(units)):
            copy(u, 3, dg, me).wait_recv()
            copy(u, 6, dg, sib).start()

    def finish(ctx):
        me, sib, xn, yn, dg, c, copy, local, relay = plan(ctx)
        other = lambda place: (place[0], place[1], 1 - c)
        for u in range(len(units)):
            for k, block in ((0, sib), (4, other(xn)), (5, other(yn)), (6, other(dg))):
                copy(u, k, block, me).wait_recv()
        for u in range(len(units)):
            for k, to in enumerate((sib, xn, yn)):
                copy(u, k, me, to, from_shard=True).wait_send()
            relay(u).wait_send()
            for k, block in ((4, xn), (5, yn), (6, dg)):
                copy(u, k, block, sib).wait_send()
            local(u).wait()

    n_u = len(units)
    operands, aliases = [s[0] for s in specs], {}
    for a, spec in enumerate(specs):
        if spec[4] is not None:
            aliases[len(operands)] = a
            operands.append(spec[4])
    return _Piece(operands, [s[1] for s in specs], aliases, (7 * n_u, 7 * n_u, n_u), start, finish, mid1, mid2)


N_CHIPS = 4


def _rs_core_piece(specs):
    n = len(specs)

    def copies(ctx):
        x, y, c = _place()
        out = []
        for a in range(n):
            for q in range(N_CHIPS):
                out.append(pltpu.make_async_remote_copy(
                    src_ref=specs[a][2](ctx.inp(a), 2 * q + (1 - c)), dst_ref=ctx.out(a).at[q],
                    send_sem=ctx.send(N_CHIPS * a + q), recv_sem=ctx.recv(N_CHIPS * a + q), device_id=(x, y, 1 - c),
                    device_id_type=MESH))
        return out

    def start(ctx):
        for cp in copies(ctx):
            cp.start()

    def finish(ctx):
        for cp in copies(ctx):
            cp.wait_recv()
            cp.wait_send()

    return _Piece([s[0] for s in specs], [s[1] for s in specs], {}, (N_CHIPS * n, N_CHIPS * n, 0), start, finish)


def _rs_chip_piece(specs, layer):
    n = len(specs)
    hops = [(1, 0), (0, 1), (1, 1)]

    def copies(ctx):
        x, y, c = _place()
        mine = 2 * x + y
        out = []
        for a in range(n):
            first, landing, size = specs[a][2]
            rows, to = pl.ds(first, size), pl.ds(landing, size)
            sums, land = ctx.inp(a), ctx.out(a)
            out.append((pltpu.make_async_copy(sums.at[mine, rows], land.at[layer, mine, to], ctx.local(a)), None))
            for j, (dx, dy) in enumerate(hops):
                px, py = x ^ dx, y ^ dy
                peer = 2 * px + py
                send = pltpu.make_async_remote_copy(
                    src_ref=sums.at[peer, rows], dst_ref=land.at[layer, mine, to], send_sem=ctx.send(3 * a + j),
                    recv_sem=ctx.recv(3 * a + j), device_id=(px, py, c), device_id_type=MESH)
                recv = pltpu.make_async_remote_copy(
                    src_ref=sums.at[peer, rows], dst_ref=land.at[layer, peer, to], send_sem=ctx.send(3 * a + j),
                    recv_sem=ctx.recv(3 * a + j), device_id=(px, py, c), device_id_type=MESH)
                out.append((send, recv))
        return out

    def start(ctx):
        for send, _ in copies(ctx):
            send.start()

    def finish(ctx):
        for send, recv in copies(ctx):
            if recv is None:
                send.wait()
            else:
                recv.wait_recv()
                send.wait_send()

    operands, aliases = [s[0] for s in specs], {}
    for a, spec in enumerate(specs):
        if spec[3] is not None:
            aliases[len(operands)] = a
            operands.append(spec[3])
    return _Piece(operands, [s[1] for s in specs], aliases, (3 * n, 3 * n, n), start, finish)


def _all_reduce_piece(pack):
    r = pack.shape[0]

    def exchange(ctx, s):
        x, y, c = _place()
        partner = [(x, y, 1 - c), (1 - x, y, c), (x, 1 - y, c)][s]
        return pltpu.make_async_remote_copy(
            src_ref=ctx.vmem(0).at[s], dst_ref=ctx.vmem(1).at[s], send_sem=ctx.send(s), recv_sem=ctx.recv(s),
            device_id=partner, device_id_type=MESH)

    def start(ctx):
        load = pltpu.make_async_copy(ctx.inp(0), ctx.vmem(0).at[0], ctx.local(0))
        load.start()
        load.wait()
        exchange(ctx, 0).start()

    def step(s):
        def run(ctx):
            exchange(ctx, s - 1).wait()
            acc, got = ctx.vmem(0), ctx.vmem(1)
            acc[s] = acc[s - 1] + got[s - 1]
            if s < 3:
                exchange(ctx, s).start()
            else:
                store = pltpu.make_async_copy(acc.at[3], ctx.out(0), ctx.local(0))
                store.start()
                store.wait()
        return run

    return _Piece([pack], [_sds(pack.shape, F32)], {}, (3, 3, 1), start, step(3), step(1), step(2),
                  vmem=[pltpu.VMEM((4, r, 128), F32), pltpu.VMEM((3, r, 128), F32)], hooks=(0.25, 0.6))


def _chip_sums(name, grad, stage, by_cols, core):
    _, r, c = stage.shape
    tr = r
    while tr * c > 1024 * 1024 or r % tr or tr % 16:
        tr -= 16
    n_t = r // tr

    def body(core_ref, g_ref, s_ref, o_ref):
        o_ref[...] = (g_ref[...].astype(F32) + s_ref[...].astype(F32)).astype(BF16)

    if by_cols:
        gspec = pl.BlockSpec((tr, c), lambda q, i, core_ref: (i, 2 * q + core_ref[0]))
    else:
        gspec = pl.BlockSpec((tr, c), lambda q, i, core_ref: ((2 * q + core_ref[0]) * n_t + i, 0))
    sspec = pl.BlockSpec((None, tr, c), lambda q, i, core_ref: (q, i, 0))
    return pl.pallas_call(
        body, name=name, out_shape=_sds(stage.shape, BF16),
        grid_spec=pltpu.PrefetchScalarGridSpec(num_scalar_prefetch=1, grid=(N_CHIPS, n_t), in_specs=[gspec, sspec],
                                               out_specs=sspec),
        compiler_params=_cparams(2))(core, grad, stage)


def _call_hosting(body, name, grid, out_shapes, in_specs, out_specs, operands, scratch, comm):
    n_in, n_out, n_scr = len(operands), len(out_shapes), len(scratch)
    hosted = _Hosted(comm, n_in, n_out)
    n_ci, n_co = len(hosted.operands), len(hosted.out_shapes)

    def hosting_body(*refs):
        ins, rest = refs[:n_in], refs[n_in:]
        c_ins, rest = rest[:n_ci], rest[n_ci:]
        outs, rest = rest[:n_out], rest[n_out:]
        c_outs, rest = rest[:n_co], rest[n_co:]
        hosted.wrap(grid, lambda: body(*ins, *outs, *rest[:n_scr]), c_ins, c_outs, rest[n_scr:])

    res = pl.pallas_call(
        hosting_body, name=name, grid=grid, out_shape=tuple(list(out_shapes) + hosted.out_shapes),
        in_specs=list(in_specs) + hosted.in_specs, out_specs=tuple(list(out_specs) + hosted.out_specs),
        input_output_aliases=hosted.aliases, scratch_shapes=list(scratch) + hosted.scratch,
        compiler_params=_cparams(len(grid)))(*operands, *hosted.operands)
    return list(res[:n_out]), list(res[n_out:])


def _matmul(name, grid, nk, kaxis, pairs, dims, extras, outs, epilogue, sum_pairs, acc_shape, comm=None, split=None):
    n_p, n_e, n_o = len(pairs), len(extras), len(outs)
    n_acc = 0 if nk == 1 else (1 if sum_pairs else n_p)
    n_in = 2 * n_p + n_e
    hosted = _Hosted(comm, n_in, n_o)
    n_ci, n_co = len(hosted.operands), len(hosted.out_shapes)

    def body(*refs):
        a_refs = refs[0:2 * n_p:2]
        b_refs = refs[1:2 * n_p:2]
        e_refs = refs[2 * n_p:n_in]
        c_ins = refs[n_in:n_in + n_ci]
        o_refs = refs[n_in + n_ci:n_in + n_ci + n_o]
        c_outs = refs[n_in + n_ci + n_o:n_in + n_ci + n_o + n_co]
        acc_refs = refs[n_in + n_ci + n_o + n_co:n_in + n_ci + n_o + n_co + n_acc]
        sems = refs[n_in + n_ci + n_o + n_co + n_acc:]

        def dots():
            if sum_pairs and n_p > 1 and dims == NN:
                a_all = jnp.concatenate([a[...] for a in a_refs], axis=1)
                b_all = jnp.concatenate([b[...] for b in b_refs], axis=0)
                return [lax.dot_general(a_all, b_all, (dims, ((), ())), preferred_element_type=F32)]
            prods = [lax.dot_general(a[...], b[...], (dims, ((), ())), preferred_element_type=F32)
                     for a, b in zip(a_refs, b_refs)]
            if sum_pairs and n_p > 1:
                prods = [functools.reduce(operator.add, prods)]
            return prods

        def compute():
            if nk == 1 and split is not None:
                n_split, b_axis, n_row = split
                width = b_refs[0].shape[b_axis] // n_split
                height = a_refs[0].shape[0] // n_row
                for s in range(n_split):
                    cols = pl.ds(s * width, width)
                    for r in range(n_row):
                        rows = pl.ds(r * height, height)
                        epilogue([lax.dot_general(a[rows, :], b[cols, :] if b_axis == 0 else b[:, cols], (dims, ((), ())),
                                                  preferred_element_type=F32) for a, b in zip(a_refs, b_refs)],
                                 e_refs, o_refs, rows, cols)
                return
            if nk == 1:
                epilogue(dots(), e_refs, o_refs)
                return
            k = pl.program_id(kaxis)

            @pl.when(k == 0)
            def _():
                for acc, p in zip(acc_refs, dots()):
                    acc[...] = p

            if nk > 2:
                @pl.when((k > 0) & (k < nk - 1))
                def _():
                    for acc, p in zip(acc_refs, dots()):
                        acc[...] += p

            @pl.when(k == nk - 1)
            def _():
                epilogue([acc[...] + p for acc, p in zip(acc_refs, dots())], e_refs, o_refs)

        hosted.wrap(grid, compute, c_ins, c_outs, sems)

    operands, in_specs = [], []
    for a, a_spec, b, b_spec in pairs:
        operands += [a, b]
        in_specs += [a_spec, b_spec]
    for e, e_spec in extras:
        operands.append(e)
        in_specs.append(e_spec)
    res = pl.pallas_call(
        body, name=name, grid=grid,
        out_shape=tuple([o for o, _ in outs] + hosted.out_shapes),
        in_specs=in_specs + hosted.in_specs, out_specs=tuple([s for _, s in outs] + hosted.out_specs),
        input_output_aliases=hosted.aliases,
        scratch_shapes=[pltpu.VMEM(acc_shape, F32) for _ in range(n_acc)] + hosted.scratch,
        compiler_params=_cparams(len(grid)),
    )(*operands, *hosted.operands)
    return list(res[:n_o]), list(res[n_o:])


NN = ((1,), (0,))
NT = ((1,), (1,))
TN = ((0,), (0,))


def _tile(n, want):
    if n <= want:
        return n
    t = want // 128 * 128
    while n % t:
        t -= 128
    return t


def _silu_parts(g):
    s = 0.5 + 0.5 * jnp.tanh(0.5 * g)
    return s, g * s


def _mm_in(h, w_in, comm=None):
    t, d = h.shape
    n = w_in.shape[1]
    tm, tn = _tile(t, _TM), _tile(n, _TN)

    def epi(accs, e, o):
        o[0][...] = accs[0].astype(BF16)

    outs, couts = _matmul(
        "mm_in", (n // tn, t // tm), 1, None,
        [(h, pl.BlockSpec((tm, d), lambda j, i: (i, 0)), w_in, pl.BlockSpec((d, tn), lambda j, i: (0, j)))],
        NN, [], [(_sds((t, n), BF16), pl.BlockSpec((tm, tn), lambda j, i: (i, j)))], epi, True, None, comm)
    return outs[0], couts


def _mm_out(y, w_out, x, comm=None):
    t, m = y.shape
    d = w_out.shape[1]
    tm, tn = _tile(t, _TM), _tile(d, _TN)

    def epi(accs, e, o):
        o[0][...] = e[0][...] + accs[0]

    outs, couts = _matmul(
        "mm_out", (t // tm, d // tn), 1, None,
        [(y, pl.BlockSpec((tm, m), lambda i, j: (i, 0)), w_out, pl.BlockSpec((m, tn), lambda i, j: (0, j)))],
        NN, [(x, pl.BlockSpec((tm, tn), lambda i, j: (i, j)))],
        [(_sds((t, d), F32), pl.BlockSpec((tm, tn), lambda i, j: (i, j)))], epi, True, None, comm)
    return outs[0], couts


def _mm_swiglu(h2, wgt, wut, comm=None):
    t, d = h2.shape
    f = wgt.shape[0]
    tm, tn = _tile(t, 2 * _TM), _tile(f, 512)

    def epi(accs, e, o, rows, cols):
        g, u = accs
        s, sg = _silu_parts(g)
        o[0][rows, cols] = (sg * u).astype(BF16)
        o[1][rows, cols] = (u * (s + sg * (1.0 - s))).astype(BF16)
        o[2][rows, cols] = sg.astype(BF16)

    wspec = pl.BlockSpec((tn, d), lambda i, j: (j, 0))
    hspec = pl.BlockSpec((tm, d), lambda i, j: (i, 0))
    ospec = pl.BlockSpec((tm, tn), lambda i, j: (i, j))
    osh = _sds((t, f), BF16)
    outs, couts = _matmul("mm_swiglu", (t // tm, f // tn), 1, None, [(h2, hspec, wgt, wspec), (h2, hspec, wut, wspec)],
                          NT, [], [(osh, ospec)] * 3, epi, False, None, comm, split=(tn // 256, 0, 2))
    return outs, couts


def _mm_down(act, wd, x1, comm=None):
    t, f = act.shape
    d = wd.shape[1]
    tm, tn = _tile(t, _TM), _tile(d, _TN)
    nk = 2
    tk = f // nk

    def epi(accs, e, o):
        o[0][...] = e[0][...] + accs[0]

    outs, couts = _matmul(
        "mm_down", (t // tm, d // tn, nk), nk, 2,
        [(act, pl.BlockSpec((tm, tk), lambda i, j, k: (i, k)), wd, pl.BlockSpec((tk, tn), lambda i, j, k: (k, j)))],
        NN, [(x1, pl.BlockSpec((tm, tn), lambda i, j, k: (i, j)))],
        [(_sds((t, d), F32), pl.BlockSpec((tm, tn), lambda i, j, k: (i, j)))], epi, True, (tm, tn), comm)
    return outs[0], couts


def _mm_dact(dxb, wd, dact_dgate, dact_dup, comm=None):
    t, d = dxb.shape
    f = wd.shape[0]
    tm, tn = _tile(t, 2 * _TM), _tile(f, 512)

    def epi(accs, e, o, rows, cols):
        da = accs[0]
        o[0][rows, cols] = (da * e[0][rows, cols].astype(F32)).astype(BF16)
        o[1][rows, cols] = (da * e[1][rows, cols].astype(F32)).astype(BF16)

    bspec = pl.BlockSpec((tm, tn), lambda i, j: (i, j))
    osh = _sds((t, f), BF16)
    outs, couts = _matmul(
        "mm_dact", (t // tm, f // tn), 1, None,
        [(dxb, pl.BlockSpec((tm, d), lambda i, j: (i, 0)), wd, pl.BlockSpec((tn, d), lambda i, j: (j, 0)))],
        NT, [(dact_dgate, bspec), (dact_dup, bspec)], [(osh, bspec)] * 2, epi, True, None, comm, split=(tn // 256, 0, 2))
    return outs, couts


def _mm_dh2(dgate, dup, wgt, wut, comm=None):
    t, f = dgate.shape
    d = wgt.shape[1]
    tm, tn = _tile(t, _TM), _tile(d, _TN)
    nk = 4
    tk = f // nk

    def epi(accs, e, o):
        o[0][...] = accs[0]

    aspec = pl.BlockSpec((tm, tk), lambda i, j, k: (i, k))
    wspec = pl.BlockSpec((tk, tn), lambda i, j, k: (k, j))
    outs, couts = _matmul("mm_dh2", (t // tm, d // tn, nk), nk, 2, [(dgate, aspec, wgt, wspec), (dup, aspec, wut, wspec)],
                          NN, [], [(_sds((t, d), F32), pl.BlockSpec((tm, tn), lambda i, j, k: (i, j)))], epi, True,
                          (tm, tn), comm)
    return outs[0], couts


def _mm_dw(name, a_list, b, tmo, tno, comm=None, m_part=(0, 1)):
    t, m = a_list[0].shape
    m = m // m_part[1]
    n = b.shape[1]
    tt = _tile(t, _TT)
    nk = t // tt
    tmo, tno = _tile(m, tmo), _tile(n, tno)
    first = m_part[0] * (m // tmo)

    def epi(accs, e, o):
        for acc, out in zip(accs, o):
            out[...] = acc.astype(BF16)

    aspec = pl.BlockSpec((tt, tmo), lambda i, j, k: (k, first + i))
    bspec = pl.BlockSpec((tt, tno), lambda i, j, k: (k, j))
    ospec = pl.BlockSpec((tmo, tno), lambda i, j, k: (i, j))
    if nk == 1:
        return _matmul(name, (m // tmo, n // tno, 1), 1, None, [(a, aspec, b, bspec) for a in a_list], TN, [],
                       [(_sds((m, n), BF16), ospec)] * len(a_list), epi, False, None, comm)
    return _matmul(name, (m // tmo, n // tno, nk), nk, 2, [(a, aspec, b, bspec) for a in a_list], TN, [],
                   [(_sds((m, n), BF16), ospec)] * len(a_list), epi, False, (tmo, tno), comm)


def _mm_dy(dxb, w_out, comm=None):
    t, d = dxb.shape
    m = w_out.shape[0]
    tm, tn = _tile(t, _TM), _tile(m, _TN)

    def epi(accs, e, o):
        o[0][...] = accs[0].astype(BF16)

    outs, couts = _matmul(
        "mm_dy", (t // tm, m // tn), 1, None,
        [(dxb, pl.BlockSpec((tm, d), lambda i, j: (i, 0)), w_out, pl.BlockSpec((tn, d), lambda i, j: (j, 0)))], NT, [],
        [(_sds((t, m), BF16), pl.BlockSpec((tm, tn), lambda i, j: (i, j)))], epi, True, None, comm)
    return outs[0], couts


def _mm_dh(dz, w_in, comm=None):
    t, n = dz.shape
    d = w_in.shape[0]
    tm, tn = _tile(t, _TM), _tile(d, _TN)
    nk = 2
    tk = n // nk

    def epi(accs, e, o):
        o[0][...] = accs[0]

    outs, couts = _matmul(
        "mm_dh", (t // tm, d // tn, nk), nk, 2,
        [(dz, pl.BlockSpec((tm, tk), lambda i, j, k: (i, k)), w_in, pl.BlockSpec((tn, tk), lambda i, j, k: (j, k)))], NT,
        [], [(_sds((t, d), F32), pl.BlockSpec((tm, tn), lambda i, j, k: (i, j)))], epi, True, (tm, tn), comm)
    return outs[0], couts


def _rmsnorm_fwd(x, g, comm=None):
    t, d = x.shape
    tm = min(_TM_NORM, t)

    def body(x_ref, g_ref, o_ref):
        xv = x_ref[...]
        rs = lax.rsqrt(jnp.mean(xv * xv, axis=-1, keepdims=True) + RMS_EPS)
        o_ref[...] = (xv * rs * g_ref[...]).astype(BF16)

    outs, couts = _call_hosting(
        body, "rmsnorm_fwd", (t // tm,), [_sds((t, d), BF16)],
        [pl.BlockSpec((tm, d), lambda i: (i, 0)), pl.BlockSpec((1, d), lambda i: (0, 0))],
        [pl.BlockSpec((tm, d), lambda i: (i, 0))], [x, g], [], comm)
    return outs[0], couts


def _rmsnorm_bwd_math(xv, g, dh):
    rs = lax.rsqrt(jnp.mean(xv * xv, axis=-1, keepdims=True) + RMS_EPS)
    xh = xv * rs
    gd = dh * g
    dx = rs * (gd - xh * jnp.mean(gd * xh, axis=-1, keepdims=True))
    return dx, jnp.sum(dh * xh, axis=0, keepdims=True)


def _rmsnorm_bwd(x, g, dh, dres):
    t, d = x.shape
    tm = min(_TM_NORM, t)

    def body(x_ref, g_ref, dh_ref, dres_ref, dx_ref, dxb_ref, dg_ref):
        dx, dg = _rmsnorm_bwd_math(x_ref[...], g_ref[...], dh_ref[...])
        dx = dx + dres_ref[...]
        dx_ref[...] = dx
        dxb_ref[...] = dx.astype(BF16)

        @pl.when(pl.program_id(0) == 0)
        def _():
            dg_ref[...] = dg

        @pl.when(pl.program_id(0) > 0)
        def _():
            dg_ref[...] += dg

    row = pl.BlockSpec((tm, d), lambda i: (i, 0))
    vec = pl.BlockSpec((1, d), lambda i: (0, 0))
    return pl.pallas_call(
        body, name="rmsnorm_bwd", grid=(t // tm,),
        out_shape=(_sds((t, d), F32), _sds((t, d), BF16), _sds((1, d), F32)),
        in_specs=[row, vec, row, row], out_specs=(row, row, vec), compiler_params=_cparams(1))(x, g, dh, dres)


def _loss_head(x, g, target):
    t, d = x.shape
    tm = min(_TM_NORM, t)

    def body(x_ref, g_ref, t_ref, dx_ref, dxb_ref, dg_ref, loss_ref):
        xv, gv = x_ref[...], g_ref[...]
        rs = lax.rsqrt(jnp.mean(xv * xv, axis=-1, keepdims=True) + RMS_EPS)
        diff = xv * rs * gv - t_ref[...]
        part = 0.5 * jnp.sum(jnp.mean(diff * diff, axis=-1, keepdims=True), axis=0, keepdims=True)
        part = jnp.broadcast_to(part, (1, 128))
        dx, dg = _rmsnorm_bwd_math(xv, gv, diff * (1.0 / d))
        dx_ref[...] = dx
        dxb_ref[...] = dx.astype(BF16)

        @pl.when(pl.program_id(0) == 0)
        def _():
            dg_ref[...] = dg
            loss_ref[...] = part

        @pl.when(pl.program_id(0) > 0)
        def _():
            dg_ref[...] += dg
            loss_ref[...] += part

    row = pl.BlockSpec((tm, d), lambda i: (i, 0))
    vec = pl.BlockSpec((1, d), lambda i: (0, 0))
    return pl.pallas_call(
        body, name="loss_head", grid=(t // tm,),
        out_shape=(_sds((t, d), F32), _sds((t, d), BF16), _sds((1, d), F32), _sds((1, 128), F32)),
        in_specs=[row, vec, row], out_specs=(row, row, vec, pl.BlockSpec((1, 128), lambda i: (0, 0))),
        compiler_params=_cparams(1))(x, g, target)


def _gelu(x):
    th = jnp.tanh(GELU_C * (x + GELU_A * x * x * x))
    return 0.5 * x * (1.0 + th), th


def _gelu_grad(x, th):
    return 0.5 * (1.0 + th) + 0.5 * x * (1.0 - th * th) * GELU_C * (1.0 + 3.0 * GELU_A * x * x)


def _masked_ws(ws_ref, h):
    i = lax.broadcasted_iota(jnp.int32, (BLK, BLK), 0) // CHUNK
    j = lax.broadcasted_iota(jnp.int32, (BLK, BLK), 1) // CHUNK
    return jnp.where(j <= i, ws_ref[h], 0.0)


def _shift_down(q, n, first_rows):
    rolled = pltpu.roll(q, n, 0)
    row = lax.broadcasted_iota(jnp.int32, q.shape, 0)
    for r, val in enumerate(first_rows):
        rolled = jnp.where(row == r, val, rolled)
    return rolled


def _shift_up(q, n, last_rows):
    tm = q.shape[0]
    rolled = pltpu.roll(q, tm - n, 0)
    row = lax.broadcasted_iota(jnp.int32, q.shape, 0)
    for r, val in enumerate(last_rows):
        rolled = jnp.where(row == tm - n + r, val, rolled)
    return rolled


def _mixer_specs(t, a, tm):
    hb = tm // HALO
    last = t // HALO - 1
    tile = pl.BlockSpec((tm, 5 * a), lambda i: (i, 0))
    prev = [pl.BlockSpec((HALO, a), functools.partial(lambda i, col: (jnp.maximum(i * hb - 1, 0), col), col=col))
            for col in (3, 4)]
    nxt = [pl.BlockSpec((HALO, a), functools.partial(lambda i, col: (jnp.minimum((i + 1) * hb, last), col), col=col))
           for col in (2, 3, 4)]
    return tile, prev, nxt


def _group_a_fwd(zu, zv, lng, lnb, ws_ref, bb_ref, mixed_ref, vln_ref):
    u, thu = _gelu(zu)
    v, thv = _gelu(zv)
    mu = jnp.mean(v, axis=-1, keepdims=True)
    vc = v - mu
    rs = lax.rsqrt(jnp.mean(vc * vc, axis=-1, keepdims=True) + LN_EPS)
    vhat = vc * rs
    vln_ref[...] = vhat * lng + lnb
    tm, a = zu.shape
    hd = a // HEADS
    for h in range(HEADS):
        w = _masked_ws(ws_ref, h).astype(BF16)
        for b in range(tm // BLK):
            rows, cols = pl.ds(b * BLK, BLK), pl.ds(h * hd, hd)
            mixed_ref[rows, cols] = jnp.dot(w, vln_ref[rows, cols].astype(BF16), preferred_element_type=F32) + bb_ref[h]
    return u, thu, thv, rs, vhat


def _mixer_fwd(z, ln_g, ln_b, w_spatial, bb, conv_w, gg, comm=None):
    t = z.shape[0]
    a = z.shape[1] // 5
    tm = min(_TM_MIX, t)
    tile, prev, _ = _mixer_specs(t, a, tm)

    def body(z_ref, pc_ref, ph_ref, lng_ref, lnb_ref, ws_ref, bb_ref, cw_ref, gg_ref, y_ref, mixed_ref, vln_ref):
        i = pl.program_id(0)
        zu = z_ref[:, 0:a].astype(F32)
        zv = z_ref[:, a:2 * a].astype(F32)
        u, _, _, _, _ = _group_a_fwd(zu, zv, lng_ref[...], lnb_ref[...], ws_ref, bb_ref, mixed_ref, vln_ref)
        ya = u * mixed_ref[...]
        ra = lax.rsqrt(jnp.mean(ya * ya, axis=-1, keepdims=True) + RMS_EPS)
        y_ref[:, 0:a] = (ya * ra * gg_ref[:, 0:a]).astype(BF16)

        zb = z_ref[:, 2 * a:3 * a].astype(F32)
        q = z_ref[:, 3 * a:4 * a].astype(F32) * z_ref[:, 4 * a:5 * a].astype(F32)
        qp = jnp.where(i > 0, pc_ref[...].astype(F32) * ph_ref[...].astype(F32), 0.0)
        qm1 = _shift_down(q, 1, [qp[HALO - 1:HALO]])
        qm2 = _shift_down(q, 2, [qp[HALO - 2:HALO - 1], qp[HALO - 1:HALO]])
        cv = cw_ref[0:1, :] * qm2 + cw_ref[1:2, :] * qm1 + cw_ref[2:3, :] * q
        yb = zb * cv
        rb = lax.rsqrt(jnp.mean(yb * yb, axis=-1, keepdims=True) + RMS_EPS)
        y_ref[:, a:2 * a] = (yb * rb * gg_ref[:, a:2 * a]).astype(BF16)

    full = lambda shape: pl.BlockSpec(shape, lambda i: (0,) * len(shape))
    outs, couts = _call_hosting(
        body, "mixer_fwd", (t // tm,), [_sds((t, 2 * a), BF16)],
        [tile, *prev, full((1, a)), full((1, a)), full(w_spatial.shape), full(bb.shape), full(conv_w.shape),
         full((1, 2 * a))],
        [pl.BlockSpec((tm, 2 * a), lambda i: (i, 0))], [z, z, z, ln_g, ln_b, w_spatial, bb, conv_w, gg],
        [pltpu.VMEM((tm, a), F32), pltpu.VMEM((tm, a), F32)], comm)
    return outs[0], couts


def _mixer_bwd(z, dy, ln_g, ln_b, w_spatial, bb, conv_w, gg, comm=None):
    t = z.shape[0]
    a = z.shape[1] // 5
    hd = a // HEADS
    tm = min(_TM_MIX, t)
    n_tiles = t // tm
    tile, prev, nxt = _mixer_specs(t, a, tm)
    hb = tm // HALO
    dy_tile = pl.BlockSpec((tm, 2 * a), lambda i: (i, 0))
    dy_next = pl.BlockSpec((HALO, a), lambda i: (jnp.minimum((i + 1) * hb, t // HALO - 1), 1))

    def body(z_ref, pc_ref, ph_ref, nb_ref, nc_ref, nh_ref, dy_ref, ndy_ref, lng_ref, lnb_ref, ws_ref, bb_ref, cw_ref,
             gg_ref, dz_ref, dlng_ref, dlnb_ref, dws_ref, dbb_ref, dcw_ref, dgg_ref, mixed_ref, vln_ref, dmix_ref,
             dvln_ref):
        i = pl.program_id(0)

        @pl.when(i == 0)
        def _():
            for ref in (dlng_ref, dlnb_ref, dws_ref, dbb_ref, dcw_ref, dgg_ref):
                ref[...] = jnp.zeros(ref.shape, F32)

        lng = lng_ref[...]
        zu = z_ref[:, 0:a].astype(F32)
        zv = z_ref[:, a:2 * a].astype(F32)
        u, thu, thv, rs, vhat = _group_a_fwd(zu, zv, lng, lnb_ref[...], ws_ref, bb_ref, mixed_ref, vln_ref)
        mixed = mixed_ref[...]
        ya = u * mixed
        ra = lax.rsqrt(jnp.mean(ya * ya, axis=-1, keepdims=True) + RMS_EPS)
        da = dy_ref[:, 0:a].astype(F32)
        yah = ya * ra
        dgg_ref[:, 0:a] += jnp.sum(da * yah, axis=0, keepdims=True)
        ga = da * gg_ref[:, 0:a]
        dya = ra * (ga - yah * jnp.mean(ga * yah, axis=-1, keepdims=True))
        dz_ref[:, 0:a] = (dya * mixed * _gelu_grad(zu, thu)).astype(BF16)
        dmix_ref[...] = dya * u
        for h in range(HEADS):
            w = _masked_ws(ws_ref, h).astype(BF16)
            dw = jnp.zeros((BLK, BLK), F32)
            db = jnp.zeros((BLK, hd), F32)
            for b in range(tm // BLK):
                rows, cols = pl.ds(b * BLK, BLK), pl.ds(h * hd, hd)
                dm = dmix_ref[rows, cols]
                dmb = dm.astype(BF16)
                db = db + dm
                dw = dw + lax.dot_general(dmb, vln_ref[rows, cols].astype(BF16), (NT, ((), ())),
                                          preferred_element_type=F32)
                dvln_ref[rows, cols] = lax.dot_general(w, dmb, (TN, ((), ())), preferred_element_type=F32)
            dws_ref[h] += dw
            dbb_ref[h] += db
        dvln = dvln_ref[...]
        dlng_ref[...] += jnp.sum(dvln * vhat, axis=0, keepdims=True)
        dlnb_ref[...] += jnp.sum(dvln, axis=0, keepdims=True)
        dvh = dvln * lng
        dv = rs * (dvh - jnp.mean(dvh, axis=-1, keepdims=True) - vhat * jnp.mean(dvh * vhat, axis=-1, keepdims=True))
        dz_ref[:, a:2 * a] = (dv * _gelu_grad(zv, thv)).astype(BF16)

        w0, w1, w2 = cw_ref[0:1, :], cw_ref[1:2, :], cw_ref[2:3, :]
        ggb = gg_ref[:, a:2 * a]
        zb = z_ref[:, 2 * a:3 * a].astype(F32)
        zc = z_ref[:, 3 * a:4 * a].astype(F32)
        zh = z_ref[:, 4 * a:5 * a].astype(F32)
        q = zc * zh
        qp = jnp.where(i > 0, pc_ref[...].astype(F32) * ph_ref[...].astype(F32), 0.0)
        qm1 = _shift_down(q, 1, [qp[HALO - 1:HALO]])
        qm2 = _shift_down(q, 2, [qp[HALO - 2:HALO - 1], qp[HALO - 1:HALO]])
        cv = w0 * qm2 + w1 * qm1 + w2 * q

        def conv_out_grad(zb_, cv_, dout_):
            yb = zb_ * cv_
            rb = lax.rsqrt(jnp.mean(yb * yb, axis=-1, keepdims=True) + RMS_EPS)
            ybh = yb * rb
            gb = dout_ * ggb
            dyb = rb * (gb - ybh * jnp.mean(gb * ybh, axis=-1, keepdims=True))
            return dyb * zb_, dyb * cv_, ybh

        db_out = dy_ref[:, a:2 * a].astype(F32)
        g, dzb, ybh = conv_out_grad(zb, cv, db_out)
        dgg_ref[:, a:2 * a] += jnp.sum(db_out * ybh, axis=0, keepdims=True)
        dz_ref[:, 2 * a:3 * a] = dzb.astype(BF16)
        qn = nc_ref[...].astype(F32) * nh_ref[...].astype(F32)
        zbn = nb_ref[...].astype(F32)
        cvn = w0 * _shift_down(qn, 2, [q[tm - 2:tm - 1], q[tm - 1:tm]]) + w1 * _shift_down(qn, 1, [q[tm - 1:tm]]) + w2 * qn
        gn, _, _ = conv_out_grad(zbn, cvn, ndy_ref[...].astype(F32))
        gn = jnp.where(i < n_tiles - 1, gn, 0.0)
        dq = w2 * g + w1 * _shift_up(g, 1, [gn[0:1]]) + w0 * _shift_up(g, 2, [gn[0:1], gn[1:2]])
        dz_ref[:, 3 * a:4 * a] = (dq * zh).astype(BF16)
        dz_ref[:, 4 * a:5 * a] = (dq * zc).astype(BF16)
        dcw_ref[0:1, :] += jnp.sum(g * qm2, axis=0, keepdims=True)
        dcw_ref[1:2, :] += jnp.sum(g * qm1, axis=0, keepdims=True)
        dcw_ref[2:3, :] += jnp.sum(g * q, axis=0, keepdims=True)

        @pl.when(i == n_tiles - 1)
        def _():
            for h in range(HEADS):
                dbb_ref[h] = jnp.broadcast_to(jnp.sum(dbb_ref[h], axis=1, keepdims=True), (BLK, hd))
                dws_ref[h] = _masked_ws(dws_ref, h)

    full = lambda shape: pl.BlockSpec(tuple(shape), lambda i: (0,) * len(shape))
    out_shapes = (_sds((t, 5 * a), BF16), _sds((1, a), F32), _sds((1, a), F32), _sds(w_spatial.shape, F32),
                  _sds(bb.shape, F32), _sds((8, a), F32), _sds((1, 2 * a), F32))
    return _call_hosting(
        body, "mixer_bwd", (n_tiles,), out_shapes,
        [tile, *prev, *nxt, dy_tile, dy_next, full((1, a)), full((1, a)), full(w_spatial.shape), full(bb.shape),
         full(conv_w.shape), full((1, 2 * a))],
        [tile, *[full(s.shape) for s in out_shapes[1:]]], [z, z, z, z, z, z, dy, dy, ln_g, ln_b, w_spatial, bb, conv_w, gg],
        [pltpu.VMEM((tm, a), F32)] * 4, comm)


def _all_reduce_small(pack, comm=None):
    r = pack.shape[0]
    hosted = _Hosted(comm, 1, 1)
    n_ci, n_co = len(hosted.operands), len(hosted.out_shapes)

    def body(*refs):
        in_ref, c_ins, out_ref, c_outs = refs[0], refs[1:1 + n_ci], refs[1 + n_ci], refs[2 + n_ci:2 + n_ci + n_co]
        acc_ref, recv_ref, send_sems, recv_sems = refs[2 + n_ci + n_co:6 + n_ci + n_co]
        sems = refs[6 + n_ci + n_co:]
        hosted.run("start", c_ins, c_outs, sems)
        x, y, c = _place()
        partners = [(x, y, 1 - c), (1 - x, y, c), (x, 1 - y, c)]
        acc_ref[0] = in_ref[...]
        for s, partner in enumerate(partners):
            cp = pltpu.make_async_remote_copy(
                src_ref=acc_ref.at[s], dst_ref=recv_ref.at[s], send_sem=send_sems.at[s], recv_sem=recv_sems.at[s],
                device_id=partner, device_id_type=MESH)
            cp.start()
            cp.wait()
            if s < 2:
                acc_ref[s + 1] = acc_ref[s] + recv_ref[s]
            else:
                out_ref[...] = acc_ref[s] + recv_ref[s]
        for stage in ("mid1", "mid2", "finish"):
            hosted.run(stage, c_ins, c_outs, sems)

    vmem = pl.BlockSpec(memory_space=pltpu.VMEM)
    res = pl.pallas_call(
        body, name="all_reduce_small", out_shape=tuple([_sds(pack.shape, F32)] + hosted.out_shapes),
        in_specs=[vmem] + hosted.in_specs, out_specs=tuple([vmem] + hosted.out_specs),
        input_output_aliases=hosted.aliases,
        scratch_shapes=[pltpu.VMEM((3, r, 128), F32), pltpu.VMEM((3, r, 128), F32), pltpu.SemaphoreType.DMA((3,)),
                        pltpu.SemaphoreType.DMA((3,))] + hosted.scratch,
        compiler_params=pltpu.CompilerParams(vmem_limit_bytes=VMEM_LIMIT_V7X),
    )(pack, *hosted.operands)
    return res[0], list(res[1:])


def _adamw_math(w, g, m, v):
    m = ADAM_B1 * m + (1.0 - ADAM_B1) * g
    v = ADAM_B2 * v + (1.0 - ADAM_B2) * (g * g)
    m_hat = m / (1.0 - ADAM_B1 ** ADAM_STEP)
    v_hat = v / (1.0 - ADAM_B2 ** ADAM_STEP)
    delta = -ADAM_LR * (m_hat / (jnp.sqrt(v_hat) + ADAM_EPS) + ADAM_WD * w)
    return delta, m, v


def _adamw_big(name, land, w, m, v, comm=None):
    nl, n_slots, r, c = land.shape
    tr = max(8, min(r, (256 * 640) // c // 8 * 8))
    while r % tr:
        tr -= 8
    grid = (nl, r // tr)
    hosted = _Hosted(comm, 4, 4)
    n_ci, n_co = len(hosted.operands), len(hosted.out_shapes)

    def body(*refs):
        land_ref, w_ref, m_ref, v_ref = refs[:4]
        c_ins = refs[4:4 + n_ci]
        g_out, d_out, m_out, v_out = refs[4 + n_ci:8 + n_ci]
        c_outs = refs[8 + n_ci:8 + n_ci + n_co]
        sems = refs[8 + n_ci + n_co:]

        def compute():
            g = land_ref[0].astype(F32)
            for s in range(1, n_slots):
                g = g + land_ref[s].astype(F32)
            delta, mn, vn = _adamw_math(w_ref[...], g, m_ref[...], v_ref[...])
            g_out[...] = g
            d_out[...] = delta
            m_out[...] = mn
            v_out[...] = vn

        hosted.wrap(grid, compute, c_ins, c_outs, sems)

    blk = pl.BlockSpec((None, tr, c), lambda l, i: (l, i, 0))
    res = pl.pallas_call(
        body, name=name, grid=grid, out_shape=tuple([_sds((nl, r, c), F32)] * 4 + hosted.out_shapes),
        in_specs=[pl.BlockSpec((None, n_slots, tr, c), lambda l, i: (l, 0, i, 0)), blk, blk, blk] + hosted.in_specs,
        out_specs=tuple([blk] * 4 + hosted.out_specs), input_output_aliases=hosted.aliases,
        scratch_shapes=hosted.scratch, compiler_params=_cparams(2))(land, w, m, v, *hosted.operands)
    return list(res[:4]), list(res[4:])


def _adamw_small(g, w, m, v):
    def body(g_ref, w_ref, m_ref, v_ref, d_out, m_out, v_out):
        delta, mn, vn = _adamw_math(w_ref[...], g_ref[...], m_ref[...], v_ref[...])
        d_out[...] = delta
        m_out[...] = mn
        v_out[...] = vn

    return pl.pallas_call(body, name="adamw_small", out_shape=tuple([_sds(g.shape, F32)] * 3),
                          compiler_params=pltpu.CompilerParams(vmem_limit_bytes=VMEM_LIMIT_V7X))(g, w, m, v)


def _rows(a):
    return a.reshape(-1, 128)


BIG = ["w_in", "w_out", "w_gate", "w_up", "w_down"]
AG_HOSTS = {
    ("norm1", 0): [("w_in", 0), ("conv_w", 0)],
    ("mm_in", 0): [("w_out", 0), ("w_gate", 0, 0, 2)], ("mixer", 0): [("w_gate", 0, 1, 2)],
    ("mm_out", 0): [("w_up", 0, 0, 2)], ("norm2", 0): [("w_up", 0, 1, 2)],
    ("mm_swiglu", 0): [("w_down", 0), ("w_in", 1), ("w_out", 1)], ("mm_down", 0): [("w_gate", 1)],
    ("mm_in", 1): [("w_up", 1)], ("mm_swiglu", 1): [("w_down", 1)],
}


def kernel(x, norm1_g, w_in, gmlp_ln_g, gmlp_ln_b, w_spatial, b_spatial, conv_w, group_norm_g, w_out, norm2_g, w_gate, w_up, w_down, final_norm_g, loss_target, m_norm1_g, m_w_in, m_gmlp_ln_g, m_gmlp_ln_b, m_w_spatial, m_b_spatial, m_conv_w, m_group_norm_g, m_w_out, m_norm2_g, m_w_gate, m_w_up, m_w_down, m_final_norm_g, v_norm1_g, v_w_in, v_gmlp_ln_g, v_gmlp_ln_b, v_w_spatial, v_b_spatial, v_conv_w, v_group_norm_g, v_w_out, v_norm2_g, v_w_gate, v_w_up, v_w_down, v_final_norm_g):
    nl = N_LAYERS
    t, d = x.shape[1], x.shape[2]
    a = d // 2
    hd = a // HEADS
    xin = x.reshape(t, d)
    target = loss_target.reshape(t, d)
    me = _index(_place())

    tr = lambda w: jnp.transpose(w, (0, 2, 1))
    big = {"w_in": w_in, "w_out": w_out, "w_gate": tr(w_gate), "w_up": tr(w_up), "w_down": w_down}
    big_m = {"w_in": m_w_in, "w_out": m_w_out, "w_gate": tr(m_w_gate), "w_up": tr(m_w_up), "w_down": m_w_down}
    big_v = {"w_in": v_w_in, "w_out": v_w_out, "w_gate": tr(v_w_gate), "w_up": tr(v_w_up), "w_down": v_w_down}
    block = {k: big[k].shape[1:] for k in BIG}
    view = {k: _cols_view(block[k][1]) if k == "w_in" else _rows_view(block[k][0]) for k in BIG}
    full_shape = {k: (block[k][0], N_DEV * block[k][1]) if k == "w_in" else (N_DEV * block[k][0], block[k][1])
                  for k in BIG}

    weights = {}
    shards = {(k, l): big[k][l].astype(BF16) for k in BIG for l in range(nl)}

    def ag_spec(k, l, part=0, n_parts=1):
        if k == "conv_w":
            return (conv_w, _sds((N_DEV, *conv_w.shape), F32), _SLOT_WHOLE, (0,), None)
        halves = (_cols_halves(*block[k], part, n_parts) if k == "w_in" else _rows_halves(block[k][0], part, n_parts))
        return (shards[(k, l)], _sds(full_shape[k], BF16), halves, (0, 1), weights.get((k, l)))

    bb = jnp.broadcast_to(b_spatial[..., None], (nl, HEADS, BLK, hd))

    def hosted(name, l):
        keys = AG_HOSTS.get((name, l), [])
        return keys, ([_ag_piece([ag_spec(*key) for key in keys])] if keys else None)

    def landed(keys, couts):
        for key, arr in zip(keys, couts):
            weights[key[:2]] = arr

    saved = []
    xl = xin
    for l in range(nl):
        keys, comm = hosted("norm1", l)
        h, couts = _rmsnorm_fwd(xl, norm1_g[l:l + 1], comm)
        landed(keys, couts)
        if l == 0:
            conv_full = jnp.transpose(weights[("conv_w", 0)], (1, 2, 0, 3)).reshape(nl, 3, a)
        keys, comm = hosted("mm_in", l)
        z, couts = _mm_in(h, weights[("w_in", l)], comm)
        landed(keys, couts)
        keys, comm = hosted("mixer", l)
        y, couts = _mixer_fwd(z, gmlp_ln_g[l:l + 1], gmlp_ln_b[l:l + 1], w_spatial[l], bb[l], conv_full[l],
                              group_norm_g[l:l + 1], comm)
        landed(keys, couts)
        keys, comm = hosted("mm_out", l)
        x1, couts = _mm_out(y, weights[("w_out", l)], xl, comm)
        landed(keys, couts)
        keys, comm = hosted("norm2", l)
        h2, couts = _rmsnorm_fwd(x1, norm2_g[l:l + 1], comm)
        landed(keys, couts)
        keys, comm = hosted("mm_swiglu", l)
        (act, dact_dgate, dact_dup), couts = _mm_swiglu(h2, weights[("w_gate", l)], weights[("w_up", l)], comm)
        landed(keys, couts)
        keys, comm = hosted("mm_down", l)
        x2, couts = _mm_down(act, weights[("w_down", l)], x1, comm)
        landed(keys, couts)
        saved.append(dict(x=xl, h=h, z=z, y=y, x1=x1, h2=h2, dact_dgate=dact_dgate, dact_dup=dact_dup, act=act))
        xl = x2

    dx, dxb, d_final_g, loss_part = _loss_head(xl, final_norm_g.reshape(1, d), target)
    small = [None] * nl
    core = lax.axis_index("c").astype(jnp.int32).reshape(1)
    half_of = {"w_in_a": ("w_in", 0), "w_in_b": ("w_in", 1)}
    for k in half_of:
        block[k] = (block["w_in"][0] // 2, block["w_in"][1])
        view[k] = view["w_in"]
    stage_shape = {k: _sds((N_CHIPS, *block[k]), BF16) for k in block}
    land_shape = {k: _sds((nl, N_CHIPS, *block[k]), BF16) for k in BIG}
    grads = [dict() for _ in range(nl)]
    stages = [dict() for _ in range(nl)]
    sums = [dict() for _ in range(nl)]
    lands = {k: None for k in BIG}

    def core_job(l, keys):
        def sink(outs):
            stages[l].update(zip(keys, outs))
        return _rs_core_piece([(grads[l][k], stage_shape[k], view[k]) for k in keys]), sink

    def chip_job(l, items):
        keys = [half_of.get(item[0], (item[0], 0))[0] for item in items]

        def rows(k, p0, p1, n_parts):
            per = block[k][0] // n_parts
            landing = half_of.get(k, (k, 0))[1] * block[k][0]
            return (p0 * per, landing + p0 * per, (p1 - p0) * per)

        def sink(outs):
            lands.update(zip(keys, outs))
        return _rs_chip_piece([(sums[l][k], land_shape[key], rows(k, p0, p1, n_parts), lands[key])
                               for key, (k, p0, p1, n_parts) in zip(keys, items)], l), sink

    def add_up(l, keys):
        for k in keys:
            sums[l][k] = _chip_sums(f"chip_sums_{k}", grads[l][k], stages[l][k], k.startswith("w_in"), core)

    def host(*jobs):
        def deliver(couts):
            i = 0
            for piece, sink in jobs:
                n_out = len(piece.out_shapes)
                sink(couts[i:i + n_out])
                i += n_out
        return [piece for piece, _ in jobs], deliver

    whole = lambda k: (k, 0, 1, 1)
    rep = ["norm1_g", "gmlp_ln_g", "gmlp_ln_b", "w_spatial", "b_spatial", "group_norm_g", "norm2_g"]
    rep_w = dict(norm1_g=norm1_g, gmlp_ln_g=gmlp_ln_g, gmlp_ln_b=gmlp_ln_b, w_spatial=w_spatial, b_spatial=b_spatial,
                 group_norm_g=group_norm_g, norm2_g=norm2_g)
    rep_m = dict(norm1_g=m_norm1_g, gmlp_ln_g=m_gmlp_ln_g, gmlp_ln_b=m_gmlp_ln_b, w_spatial=m_w_spatial,
                 b_spatial=m_b_spatial, group_norm_g=m_group_norm_g, norm2_g=m_norm2_g)
    rep_v = dict(norm1_g=v_norm1_g, gmlp_ln_g=v_gmlp_ln_g, gmlp_ln_b=v_gmlp_ln_b, w_spatial=v_w_spatial,
                 b_spatial=v_b_spatial, group_norm_g=v_group_norm_g, norm2_g=v_norm2_g)

    def small_grad_parts():
        parts = [_rows(jnp.stack([small[l][k].reshape(rep_w[k].shape[1:]) for l in range(nl)])) for k in rep]
        parts.append(_rows(d_final_g))
        parts.append(_rows(jnp.stack([small[l]["conv_w"] for l in range(nl)])))
        parts.append(jnp.broadcast_to(loss_part, (8, 128)))
        return parts

    for l in reversed(range(nl)):
        s = saved[l]
        wi, wo, wgt, wut, wd = [weights[(k, l)] for k in BIG]
        later = l + 1 < nl
        comm, deliver = host(chip_job(l + 1, [("w_in", 0, 1, 2)])) if later else host()
        (grads[l]["w_down"],), couts = _mm_dw("mm_dw_down", [s["act"]], dxb, 2816, 1024, comm)
        deliver(couts)
        comm, deliver = (host(core_job(l, ["w_down"]), chip_job(l + 1, [("w_in", 1, 2, 2)])) if later
                         else host(core_job(l, ["w_down"])))
        (dgate, dup), couts = _mm_dact(dxb, wd, s["dact_dgate"], s["dact_dup"], comm)
        deliver(couts)
        add_up(l, ["w_down"])
        comm, deliver = host(chip_job(l, [("w_down", 0, 3, 4)]))
        (grads[l]["w_gate"],), couts = _mm_dw("mm_dw_gate", [dgate], s["h2"], 2816, 1024, comm)
        deliver(couts)
        comm, deliver = host(chip_job(l, [("w_down", 3, 4, 4)]), core_job(l, ["w_gate"]))
        (grads[l]["w_up"],), couts = _mm_dw("mm_dw_up", [dup], s["h2"], 2816, 1024, comm)
        deliver(couts)
        add_up(l, ["w_gate"])
        comm, deliver = host(chip_job(l, [whole("w_gate")]), core_job(l, ["w_up"]))
        dh2, couts = _mm_dh2(dgate, dup, wgt, wut, comm)
        deliver(couts)
        add_up(l, ["w_up"])
        dx1, dx1b, d_n2 = _rmsnorm_bwd(s["x1"], norm2_g[l:l + 1], dh2, dx)
        comm, deliver = host(chip_job(l, [("w_up", 0, 1, 4)]))
        dy, couts = _mm_dy(dx1b, wo, comm)
        deliver(couts)
        comm, deliver = host(chip_job(l, [("w_up", 1, 2, 4)]))
        (grads[l]["w_out"],), couts = _mm_dw("mm_dw_out", [s["y"]], dx1b, 1024, 1024, comm)
        deliver(couts)
        comm, deliver = host(chip_job(l, [("w_up", 2, 4, 4)]), core_job(l, ["w_out"]))
        (dz, d_lng, d_lnb, d_ws, d_bb, d_cw, d_gg), couts = _mixer_bwd(
            s["z"], dy, gmlp_ln_g[l:l + 1], gmlp_ln_b[l:l + 1], w_spatial[l], bb[l], conv_full[l], group_norm_g[l:l + 1],
            comm)
        deliver(couts)
        add_up(l, ["w_out"])
        small[l] = dict(norm1_g=jnp.zeros((1, d), F32), gmlp_ln_g=d_lng, gmlp_ln_b=d_lnb, w_spatial=d_ws,
                        b_spatial=d_bb[:, :, 0], group_norm_g=d_gg, norm2_g=d_n2, conv_w=d_cw[0:3])
        if l > 0:
            comm, deliver = host(chip_job(l, [whole("w_out")]))
            (grads[l]["w_in"],), couts = _mm_dw("mm_dw_in", [s["h"]], dz, 2048, 1024, comm)
            deliver(couts)
            comm, deliver = host(core_job(l, ["w_in"]))
            dh, couts = _mm_dh(dz, wi, comm)
            deliver(couts)
            add_up(l, ["w_in"])
        else:
            comm, deliver = host(chip_job(l, [whole("w_out")]))
            (grads[l]["w_in_a"],), couts = _mm_dw("mm_dw_in_a", [s["h"]], dz, 1024, 1024, comm, m_part=(0, 2))
            deliver(couts)
            comm, deliver = host(core_job(l, ["w_in_a"]))
            (grads[l]["w_in_b"],), couts = _mm_dw("mm_dw_in_b", [s["h"]], dz, 1024, 1024, comm, m_part=(1, 2))
            deliver(couts)
            add_up(l, ["w_in_a"])
            parts = small_grad_parts()
            reduced = []
            comm, deliver = host(chip_job(l, [whole("w_in_a")]), core_job(l, ["w_in_b"]),
                                 (_all_reduce_piece(jnp.concatenate(parts, axis=0)), reduced.extend))
            dh, couts = _mm_dh(dz, wi, comm)
            deliver(couts)
            add_up(l, ["w_in_b"])
        dx, dxb, small[l]["norm1_g"] = _rmsnorm_bwd(s["x"], norm1_g[l:l + 1], dh, dx1)
    grad_x = dx.reshape(x.shape)

    sizes = [p.shape[0] for p in parts]
    comm, deliver = host(chip_job(0, [whole("w_in_b")]))
    last, couts = _all_reduce_small(_rows(small[0]["norm1_g"]), comm)
    deliver(couts)
    total = lax.dynamic_update_slice(reduced[0], last, (0, 0))
    offs = [0]
    for n in sizes:
        offs.append(offs[-1] + n)
    pieces = [total[offs[i]:offs[i + 1]] for i in range(len(parts))]
    loss = pieces[-1][0, 0]
    conv_g_full = pieces[-2].reshape(nl, 3, N_DEV, a // N_DEV)
    conv_g = lax.dynamic_index_in_dim(conv_g_full, me, axis=2, keepdims=False)
    n_rep = offs[len(rep) + 1]
    pad = jnp.zeros((2, 128), F32)

    def small_pack(named, final, conv):
        return jnp.concatenate([_rows(named[k]) for k in rep] + [_rows(final), _rows(conv), pad], axis=0)

    g_small = jnp.concatenate([total[:n_rep], _rows(conv_g), pad], axis=0)
    d_small, m_small, v_small = _adamw_small(
        g_small, small_pack(rep_w, final_norm_g, conv_w), small_pack(rep_m, m_final_norm_g, m_conv_w),
        small_pack(rep_v, v_final_norm_g, v_conv_w))

    def unpack(packed):
        out = {k: packed[offs[i]:offs[i + 1]].reshape(rep_w[k].shape) for i, k in enumerate(rep)}
        out["final_norm_g"] = packed[offs[len(rep)]:n_rep].reshape(final_norm_g.shape)
        out["conv_w"] = packed[n_rep:n_rep + 6].reshape(conv_w.shape)
        return out

    res = {"grad": unpack(g_small), "delta": unpack(d_small), "m": unpack(m_small), "v": unpack(v_small)}

    for k in BIG:
        outs, _ = _adamw_big(f"adamw_{k}", lands[k], big[k], big_m[k], big_v[k])
        if k in ("w_gate", "w_up"):
            outs = [tr(o) for o in outs]
        res["grad"][k], res["delta"][k], res["m"][k], res["v"][k] = outs

    order = ["norm1_g", "w_in", "gmlp_ln_g", "gmlp_ln_b", "w_spatial", "b_spatial", "conv_w", "group_norm_g", "w_out",
             "norm2_g", "w_gate", "w_up", "w_down", "final_norm_g"]
    return (loss, grad_x, *[res["grad"][k] for k in order], *[res["delta"][k] for k in order],
            *[res["m"][k] for k in order], *[res["v"][k] for k in order])
```

```python
import functools
import math
import operator

import jax
import jax.numpy as jnp
from jax import lax
from jax.experimental import pallas as pl
from jax.experimental.pallas import tpu as pltpu

F32 = jnp.float32
BF16 = jnp.bfloat16
MESH = pl.DeviceIdType.MESH

N_DEV = 8
N_LAYERS = 2
HEADS = 8
BLK = 128
CHUNK = 64
HALO = 16
RMS_EPS = 1e-6
LN_EPS = 1e-5
ADAM_LR, ADAM_B1, ADAM_B2, ADAM_EPS, ADAM_WD, ADAM_STEP = 0.001, 0.9, 0.999, 1e-8, 0.01, 10
GELU_C = math.sqrt(2.0 / math.pi)
GELU_A = 0.044715

VMEM_LIMIT_V7X = 56 * 1024 * 1024
_TM = 1024
_TN = 1024
_TT = 1024
_TM_MIX = 256
_TM_NORM = 512


def _cparams(n_axes):
    return pltpu.CompilerParams(dimension_semantics=("arbitrary",) * n_axes, vmem_limit_bytes=VMEM_LIMIT_V7X)


def _sds(shape, dtype):
    return jax.ShapeDtypeStruct(tuple(shape), dtype)


def _place():
    return lax.axis_index("x"), lax.axis_index("y"), lax.axis_index("c")


def _index(place):
    return 4 * place[0] + 2 * place[1] + place[2]


class _Piece:
    def __init__(self, operands, out_shapes, aliases, n_sems, start, finish, mid1=None, mid2=None, vmem=(),
                 hooks=(0.6, 0.87)):
        self.operands, self.out_shapes, self.aliases, self.n_sems = list(operands), list(out_shapes), dict(aliases), n_sems
        self.vmem = list(vmem)
        self.hooks = hooks
        nothing = lambda ctx: None
        self.start, self.mid1, self.mid2, self.finish = start, mid1 or nothing, mid2 or nothing, finish


class _Ctx:
    def __init__(self, ins, outs, sems, offs):
        self.ins, self.outs, self.sems = ins, outs, sems
        self.o_in, self.o_out, self.o_send, self.o_recv, self.o_loc, self.o_vmem = offs

    def vmem(self, i):
        return self.sems[3 + self.o_vmem + i]

    def inp(self, i):
        return self.ins[self.o_in + i]

    def out(self, i):
        return self.outs[self.o_out + i]

    def send(self, k):
        return self.sems[0].at[self.o_send + k]

    def recv(self, k):
        return self.sems[1].at[self.o_recv + k]

    def local(self, k):
        return self.sems[2].at[self.o_loc + k]


class _Hosted:
    def __init__(self, pieces, n_in_before, n_out_before):
        self.pieces = [p for p in (pieces or []) if p is not None]
        self.operands, self.out_shapes, self.aliases, self.offs = [], [], {}, []
        counts, vmem = [0, 0, 0], []
        for p in self.pieces:
            self.offs.append((len(self.operands), len(self.out_shapes), *counts, len(vmem)))
            for i, j in p.aliases.items():
                self.aliases[n_in_before + len(self.operands) + i] = n_out_before + len(self.out_shapes) + j
            self.operands += p.operands
            self.out_shapes += p.out_shapes
            counts = [c + n for c, n in zip(counts, p.n_sems)]
            vmem += p.vmem
        hbm = pl.BlockSpec(memory_space=pl.ANY)
        self.in_specs = [hbm] * len(self.operands)
        self.out_specs = [hbm] * len(self.out_shapes)
        self.scratch = ([pltpu.SemaphoreType.DMA((max(c, 1),)) for c in counts] + vmem) if self.pieces else []

    def run(self, stage, ins, outs, sems):
        for p, offs in zip(self.pieces, self.offs):
            getattr(p, stage)(_Ctx(ins, outs, sems, offs))

    def wrap(self, grid, compute, ins, outs, sems):
        if not self.pieces:
            compute()
            return
        n_steps = math.prod(grid)
        lin = 0
        for ax, g in enumerate(grid):
            lin = lin * g + pl.program_id(ax)
        pl.when(lin == 0)(lambda: self.run("start", ins, outs, sems))
        compute()
        for stage, which in (("mid1", 0), ("mid2", 1)):
            for p, offs in zip(self.pieces, self.offs):
                at = min(n_steps - 1, int(p.hooks[which] * n_steps))
                pl.when(lin == at)(functools.partial(getattr(p, stage), _Ctx(ins, outs, sems, offs)))
        pl.when(lin == n_steps - 1)(lambda: self.run("finish", ins, outs, sems))


def _cols_view(width):
    return lambda ref, p: ref.at[:, pl.ds(pl.multiple_of(p * width, 128), width)]


def _rows_view(height):
    return lambda ref, p: ref.at[pl.ds(pl.multiple_of(p * height, 16), height), :]


def _cols_halves(rows, width, part, n_parts):
    hr = rows // n_parts // 2
    at = lambda h: pl.ds(part * 2 * hr + h * hr, hr)
    return (lambda ref, p, h: ref.at[at(h), pl.ds(pl.multiple_of(p * width, 128), width)],
            lambda ref, h: ref.at[at(h), :], 2)


def _rows_halves(height, part, n_parts):
    hh = height // n_parts // 2
    return (lambda ref, p, h: ref.at[pl.ds(pl.multiple_of(p * height + part * 2 * hh + h * hh, 16), hh), :],
            lambda ref, h: ref.at[pl.ds(part * 2 * hh + h * hh, hh), :], 2)


_SLOT_WHOLE = (lambda ref, p, h: ref.at[p], lambda ref, h: ref, 1)


def _ag_piece(specs):
    units = [(a, h) for a, s in enumerate(specs) for h in s[3]]

    def plan(ctx):
        x, y, c = _place()
        me, sib, xn, yn, dg = (x, y, c), (x, y, 1 - c), (1 - x, y, c), (x, 1 - y, c), (1 - x, 1 - y, c)

        def copy(u, k, block, to, from_shard=False):
            a, h = units[u]
            dst_of, src_of, _ = specs[a][2]
            dst = dst_of(ctx.out(a), _index(block), h)
            return pltpu.make_async_remote_copy(
                src_ref=src_of(ctx.inp(a), h) if from_shard else dst, dst_ref=dst, send_sem=ctx.send(7 * u + k),
                recv_sem=ctx.recv(7 * u + k), device_id=to, device_id_type=MESH)

        def local(u):
            a, h = units[u]
            dst_of, src_of, _ = specs[a][2]
            return pltpu.make_async_copy(src_of(ctx.inp(a), h), dst_of(ctx.out(a), _index(me), h), ctx.local(u))

        def relay(u):
            return copy(u, 3, xn, yn) if units[u][1] % 2 == 0 else copy(u, 3, yn, xn)

        return me, sib, xn, yn, dg, c, copy, local, relay

    def start(ctx):
        me, sib, xn, yn, dg, c, copy, local, relay = plan(ctx)
        for u in range(len(units)):
            local(u).start()
            for k, to in enumerate((sib, xn, yn)):
                copy(u, k, me, to, from_shard=True).start()

    def mid1(ctx):
        me, sib, xn, yn, dg, c, copy, local, relay = plan(ctx)
        for u in range(len(units)):
            copy(u, 1, xn, me).wait_recv()
            copy(u, 2, yn, me).wait_recv()
            relay(u).start()
            copy(u, 4, xn, sib).start()
            copy(u, 5, yn, sib).start()

    def mid2(ctx):
        me, sib, xn, yn, dg, c, copy, local, relay = plan(ctx)
        for u in range(len(units)):
            copy(u, 3, dg, me).wait_recv()
            copy(u, 6, dg, sib).start()

    def finish(ctx):
        me, sib, xn, yn, dg, c, copy, local, relay = plan(ctx)
        other = lambda place: (place[0], place[1], 1 - c)
        for u in range(len(units)):
            for k, block in ((0, sib), (4, other(xn)), (5, other(yn)), (6, other(dg))):
                copy(u, k, block, me).wait_recv()
        for u in range(len(units)):
            for k, to in enumerate((sib, xn, yn)):
                copy(u, k, me, to, from_shard=True).wait_send()
            relay(u).wait_send()
            for k, block in ((4, xn), (5, yn), (6, dg)):
                copy(u, k, block, sib).wait_send()
            local(u).wait()

    n_u = len(units)
    operands, aliases = [s[0] for s in specs], {}
    for a, spec in enumerate(specs):
        if spec[4] is not None:
            aliases[len(operands)] = a
            operands.append(spec[4])
    return _Piece(operands, [s[1] for s in specs], aliases, (7 * n_u, 7 * n_u, n_u), start, finish, mid1, mid2)


N_CHIPS = 4


def _rs_core_piece(specs):
    n = len(specs)

    def copies(ctx):
        x, y, c = _place()
        out = []
        for a in range(n):
            for q in range(N_CHIPS):
                out.append(pltpu.make_async_remote_copy(
                    src_ref=specs[a][2](ctx.inp(a), 2 * q + (1 - c)), dst_ref=ctx.out(a).at[q],
                    send_sem=ctx.send(N_CHIPS * a + q), recv_sem=ctx.recv(N_CHIPS * a + q), device_id=(x, y, 1 - c),
                    device_id_type=MESH))
        return out

    def start(ctx):
        for cp in copies(ctx):
            cp.start()

    def finish(ctx):
        for cp in copies(ctx):
            cp.wait_recv()
            cp.wait_send()

    return _Piece([s[0] for s in specs], [s[1] for s in specs], {}, (N_CHIPS * n, N_CHIPS * n, 0), start, finish)


def _rs_chip_piece(specs, layer):
    n = len(specs)
    hops = [(1, 0), (0, 1), (1, 1)]

    def copies(ctx):
        x, y, c = _place()
        mine = 2 * x + y
        out = []
        for a in range(n):
            first, landing, size = specs[a][2]
            rows, to = pl.ds(first, size), pl.ds(landing, size)
            sums, land = ctx.inp(a), ctx.out(a)
            out.append((pltpu.make_async_copy(sums.at[mine, rows], land.at[layer, mine, to], ctx.local(a)), None))
            for j, (dx, dy) in enumerate(hops):
                px, py = x ^ dx, y ^ dy
                peer = 2 * px + py
                send = pltpu.make_async_remote_copy(
                    src_ref=sums.at[peer, rows], dst_ref=land.at[layer, mine, to], send_sem=ctx.send(3 * a + j),
                    recv_sem=ctx.recv(3 * a + j), device_id=(px, py, c), device_id_type=MESH)
                recv = pltpu.make_async_remote_copy(
                    src_ref=sums.at[peer, rows], dst_ref=land.at[layer, peer, to], send_sem=ctx.send(3 * a + j),
                    recv_sem=ctx.recv(3 * a + j), device_id=(px, py, c), device_id_type=MESH)
                out.append((send, recv))
        return out

    def start(ctx):
        for send, _ in copies(ctx):
            send.start()

    def finish(ctx):
        for send, recv in copies(ctx):
            if recv is None:
                send.wait()
            else:
                recv.wait_recv()
                send.wait_send()

    operands, aliases = [s[0] for s in specs], {}
    for a, spec in enumerate(specs):
        if spec[3] is not None:
            aliases[len(operands)] = a
            operands.append(spec[3])
    return _Piece(operands, [s[1] for s in specs], aliases, (3 * n, 3 * n, n), start, finish)


def _all_reduce_piece(pack):
    r = pack.shape[0]
    half = r // 2

    def plan(ctx):
        x, y, c = _place()
        acc, got = ctx.vmem(0), ctx.vmem(1)
        mine = pl.ds(pl.multiple_of(c * half, 8), half)
        sib = (x, y, 1 - c)
        copies = [
            pltpu.make_async_remote_copy(src_ref=acc.at[0], dst_ref=got.at[0], send_sem=ctx.send(0), recv_sem=ctx.recv(0),
                                         device_id=sib, device_id_type=MESH),
            pltpu.make_async_remote_copy(src_ref=acc.at[1, mine], dst_ref=got.at[1, mine], send_sem=ctx.send(1),
                                         recv_sem=ctx.recv(1), device_id=(1 - x, y, c), device_id_type=MESH),
            pltpu.make_async_remote_copy(src_ref=acc.at[2, mine], dst_ref=got.at[2, mine], send_sem=ctx.send(2),
                                         recv_sem=ctx.recv(2), device_id=(x, 1 - y, c), device_id_type=MESH),
            pltpu.make_async_remote_copy(src_ref=acc.at[3, mine], dst_ref=acc.at[3, mine], send_sem=ctx.send(3),
                                         recv_sem=ctx.recv(3), device_id=sib, device_id_type=MESH),
        ]
        other = pl.ds(pl.multiple_of((1 - c) * half, 8), half)
        arrival = pltpu.make_async_remote_copy(src_ref=acc.at[3, other], dst_ref=acc.at[3, other], send_sem=ctx.send(3),
                                               recv_sem=ctx.recv(3), device_id=sib, device_id_type=MESH)
        return acc, got, mine, copies, arrival

    def start(ctx):
        acc, got, mine, copies, arrival = plan(ctx)
        load = pltpu.make_async_copy(ctx.inp(0), acc.at[0], ctx.local(0))
        load.start()
        load.wait()
        copies[0].start()

    def mid1(ctx):
        acc, got, mine, copies, arrival = plan(ctx)
        copies[0].wait()
        acc[1] = acc[0] + got[0]
        copies[1].start()

    def mid2(ctx):
        acc, got, mine, copies, arrival = plan(ctx)
        copies[1].wait()
        acc[2, mine] = acc[1, mine] + got[1, mine]
        copies[2].start()

    def finish(ctx):
        acc, got, mine, copies, arrival = plan(ctx)
        copies[2].wait()
        acc[3, mine] = acc[2, mine] + got[2, mine]
        copies[3].start()
        copies[3].wait_send()
        arrival.wait_recv()
        store = pltpu.make_async_copy(acc.at[3], ctx.out(0), ctx.local(0))
        store.start()
        store.wait()

    return _Piece([pack], [_sds(pack.shape, F32)], {}, (4, 4, 1), start, finish, mid1, mid2,
                  vmem=[pltpu.VMEM((4, r, 128), F32), pltpu.VMEM((3, r, 128), F32)], hooks=(0.25, 0.6))


def _chip_sums(name, grad, stage, by_cols, core):
    _, r, c = stage.shape
    tr = r
    while tr * c > 1024 * 1024 or r % tr or tr % 16:
        tr -= 16
    n_t = r // tr

    def body(core_ref, g_ref, s_ref, o_ref):
        o_ref[...] = (g_ref[...].astype(F32) + s_ref[...].astype(F32)).astype(BF16)

    if by_cols:
        gspec = pl.BlockSpec((tr, c), lambda q, i, core_ref: (i, 2 * q + core_ref[0]))
    else:
        gspec = pl.BlockSpec((tr, c), lambda q, i, core_ref: ((2 * q + core_ref[0]) * n_t + i, 0))
    sspec = pl.BlockSpec((None, tr, c), lambda q, i, core_ref: (q, i, 0))
    return pl.pallas_call(
        body, name=name, out_shape=_sds(stage.shape, BF16),
        grid_spec=pltpu.PrefetchScalarGridSpec(num_scalar_prefetch=1, grid=(N_CHIPS, n_t), in_specs=[gspec, sspec],
                                               out_specs=sspec),
        compiler_params=_cparams(2))(core, grad, stage)


def _call_hosting(body, name, grid, out_shapes, in_specs, out_specs, operands, scratch, comm):
    n_in, n_out, n_scr = len(operands), len(out_shapes), len(scratch)
    hosted = _Hosted(comm, n_in, n_out)
    n_ci, n_co = len(hosted.operands), len(hosted.out_shapes)

    def hosting_body(*refs):
        ins, rest = refs[:n_in], refs[n_in:]
        c_ins, rest = rest[:n_ci], rest[n_ci:]
        outs, rest = rest[:n_out], rest[n_out:]
        c_outs, rest = rest[:n_co], rest[n_co:]
        hosted.wrap(grid, lambda: body(*ins, *outs, *rest[:n_scr]), c_ins, c_outs, rest[n_scr:])

    res = pl.pallas_call(
        hosting_body, name=name, grid=grid, out_shape=tuple(list(out_shapes) + hosted.out_shapes),
        in_specs=list(in_specs) + hosted.in_specs, out_specs=tuple(list(out_specs) + hosted.out_specs),
        input_output_aliases=hosted.aliases, scratch_shapes=list(scratch) + hosted.scratch,
        compiler_params=_cparams(len(grid)))(*operands, *hosted.operands)
    return list(res[:n_out]), list(res[n_out:])


def _matmul(name, grid, nk, kaxis, pairs, dims, extras, outs, epilogue, sum_pairs, acc_shape, comm=None, split=None):
    n_p, n_e, n_o = len(pairs), len(extras), len(outs)
    n_acc = 0 if nk == 1 else (1 if sum_pairs else n_p)
    n_in = 2 * n_p + n_e
    hosted = _Hosted(comm, n_in, n_o)
    n_ci, n_co = len(hosted.operands), len(hosted.out_shapes)

    def body(*refs):
        a_refs = refs[0:2 * n_p:2]
        b_refs = refs[1:2 * n_p:2]
        e_refs = refs[2 * n_p:n_in]
        c_ins = refs[n_in:n_in + n_ci]
        o_refs = refs[n_in + n_ci:n_in + n_ci + n_o]
        c_outs = refs[n_in + n_ci + n_o:n_in + n_ci + n_o + n_co]
        acc_refs = refs[n_in + n_ci + n_o + n_co:n_in + n_ci + n_o + n_co + n_acc]
        sems = refs[n_in + n_ci + n_o + n_co + n_acc:]

        def dots():
            if sum_pairs and n_p > 1 and dims == NN:
                a_all = jnp.concatenate([a[...] for a in a_refs], axis=1)
                b_all = jnp.concatenate([b[...] for b in b_refs], axis=0)
                return [lax.dot_general(a_all, b_all, (dims, ((), ())), preferred_element_type=F32)]
            prods = [lax.dot_general(a[...], b[...], (dims, ((), ())), preferred_element_type=F32)
                     for a, b in zip(a_refs, b_refs)]
            if sum_pairs and n_p > 1:
                prods = [functools.reduce(operator.add, prods)]
            return prods

        def compute():
            if nk == 1 and split is not None:
                n_split, b_axis, n_row = split
                width = b_refs[0].shape[b_axis] // n_split
                height = a_refs[0].shape[0] // n_row
                for s in range(n_split):
                    cols = pl.ds(s * width, width)
                    for r in range(n_row):
                        rows = pl.ds(r * height, height)
                        epilogue([lax.dot_general(a[rows, :], b[cols, :] if b_axis == 0 else b[:, cols], (dims, ((), ())),
                                                  preferred_element_type=F32) for a, b in zip(a_refs, b_refs)],
                                 e_refs, o_refs, rows, cols)
                return
            if nk == 1:
                epilogue(dots(), e_refs, o_refs)
                return
            k = pl.program_id(kaxis)

            @pl.when(k == 0)
            def _():
                for acc, p in zip(acc_refs, dots()):
                    acc[...] = p

            if nk > 2:
                @pl.when((k > 0) & (k < nk - 1))
                def _():
                    for acc, p in zip(acc_refs, dots()):
                        acc[...] += p

            @pl.when(k == nk - 1)
            def _():
                epilogue([acc[...] + p for acc, p in zip(acc_refs, dots())], e_refs, o_refs)

        hosted.wrap(grid, compute, c_ins, c_outs, sems)

    operands, in_specs = [], []
    for a, a_spec, b, b_spec in pairs:
        operands += [a, b]
        in_specs += [a_spec, b_spec]
    for e, e_spec in extras:
        operands.append(e)
        in_specs.append(e_spec)
    res = pl.pallas_call(
        body, name=name, grid=grid,
        out_shape=tuple([o for o, _ in outs] + hosted.out_shapes),
        in_specs=in_specs + hosted.in_specs, out_specs=tuple([s for _, s in outs] + hosted.out_specs),
        input_output_aliases=hosted.aliases,
        scratch_shapes=[pltpu.VMEM(acc_shape, F32) for _ in range(n_acc)] + hosted.scratch,
        compiler_params=_cparams(len(grid)),
    )(*operands, *hosted.operands)
    return list(res[:n_o]), list(res[n_o:])


NN = ((1,), (0,))
NT = ((1,), (1,))
TN = ((0,), (0,))


def _tile(n, want):
    if n <= want:
        return n
    t = want // 128 * 128
    while n % t:
        t -= 128
    return t


def _silu_parts(g):
    s = 0.5 + 0.5 * jnp.tanh(0.5 * g)
    return s, g * s


def _mm_in(h, w_in, comm=None):
    t, d = h.shape
    n = w_in.shape[1]
    tm, tn = _tile(t, _TM), _tile(n, _TN)

    def epi(accs, e, o):
        o[0][...] = accs[0].astype(BF16)

    outs, couts = _matmul(
        "mm_in", (n // tn, t // tm), 1, None,
        [(h, pl.BlockSpec((tm, d), lambda j, i: (i, 0)), w_in, pl.BlockSpec((d, tn), lambda j, i: (0, j)))],
        NN, [], [(_sds((t, n), BF16), pl.BlockSpec((tm, tn), lambda j, i: (i, j)))], epi, True, None, comm)
    return outs[0], couts


def _mm_out(y, w_out, x, comm=None):
    t, m = y.shape
    d = w_out.shape[1]
    tm, tn = _tile(t, _TM), _tile(d, _TN)

    def epi(accs, e, o):
        o[0][...] = e[0][...] + accs[0]

    outs, couts = _matmul(
        "mm_out", (t // tm, d // tn), 1, None,
        [(y, pl.BlockSpec((tm, m), lambda i, j: (i, 0)), w_out, pl.BlockSpec((m, tn), lambda i, j: (0, j)))],
        NN, [(x, pl.BlockSpec((tm, tn), lambda i, j: (i, j)))],
        [(_sds((t, d), F32), pl.BlockSpec((tm, tn), lambda i, j: (i, j)))], epi, True, None, comm)
    return outs[0], couts


def _mm_swiglu(h2, wgt, wut, comm=None):
    t, d = h2.shape
    f = wgt.shape[0]
    tm, tn = _tile(t, 2 * _TM), _tile(f, 512)

    def epi(accs, e, o, rows, cols):
        g, u = accs
        s, sg = _silu_parts(g)
        o[0][rows, cols] = (sg * u).astype(BF16)
        o[1][rows, cols] = (u * (s + sg * (1.0 - s))).astype(BF16)
        o[2][rows, cols] = sg.astype(BF16)

    wspec = pl.BlockSpec((tn, d), lambda i, j: (j, 0))
    hspec = pl.BlockSpec((tm, d), lambda i, j: (i, 0))
    ospec = pl.BlockSpec((tm, tn), lambda i, j: (i, j))
    osh = _sds((t, f), BF16)
    outs, couts = _matmul("mm_swiglu", (t // tm, f // tn), 1, None, [(h2, hspec, wgt, wspec), (h2, hspec, wut, wspec)],
                          NT, [], [(osh, ospec)] * 3, epi, False, None, comm, split=(tn // 256, 0, 2))
    return outs, couts


def _mm_down(act, wd, x1, comm=None):
    t, f = act.shape
    d = wd.shape[1]
    tm, tn = _tile(t, _TM), _tile(d, _TN)
    nk = 2
    tk = f // nk

    def epi(accs, e, o):
        o[0][...] = e[0][...] + accs[0]

    outs, couts = _matmul(
        "mm_down", (t // tm, d // tn, nk), nk, 2,
        [(act, pl.BlockSpec((tm, tk), lambda i, j, k: (i, k)), wd, pl.BlockSpec((tk, tn), lambda i, j, k: (k, j)))],
        NN, [(x1, pl.BlockSpec((tm, tn), lambda i, j, k: (i, j)))],
        [(_sds((t, d), F32), pl.BlockSpec((tm, tn), lambda i, j, k: (i, j)))], epi, True, (tm, tn), comm)
    return outs[0], couts


def _mm_dact(dxb, wd, dact_dgate, dact_dup, comm=None):
    t, d = dxb.shape
    f = wd.shape[0]
    tm, tn = _tile(t, 2 * _TM), _tile(f, 512)

    def epi(accs, e, o, rows, cols):
        da = accs[0]
        o[0][rows, cols] = (da * e[0][rows, cols].astype(F32)).astype(BF16)
        o[1][rows, cols] = (da * e[1][rows, cols].astype(F32)).astype(BF16)

    bspec = pl.BlockSpec((tm, tn), lambda i, j: (i, j))
    osh = _sds((t, f), BF16)
    outs, couts = _matmul(
        "mm_dact", (t // tm, f // tn), 1, None,
        [(dxb, pl.BlockSpec((tm, d), lambda i, j: (i, 0)), wd, pl.BlockSpec((tn, d), lambda i, j: (j, 0)))],
        NT, [(dact_dgate, bspec), (dact_dup, bspec)], [(osh, bspec)] * 2, epi, True, None, comm, split=(tn // 256, 0, 2))
    return outs, couts


def _mm_dh2(dgate, dup, wgt, wut, comm=None):
    t, f = dgate.shape
    d = wgt.shape[1]
    tm, tn = _tile(t, _TM), _tile(d, _TN)
    nk = 4
    tk = f // nk

    def epi(accs, e, o):
        o[0][...] = accs[0]

    aspec = pl.BlockSpec((tm, tk), lambda i, j, k: (i, k))
    wspec = pl.BlockSpec((tk, tn), lambda i, j, k: (k, j))
    outs, couts = _matmul("mm_dh2", (t // tm, d // tn, nk), nk, 2, [(dgate, aspec, wgt, wspec), (dup, aspec, wut, wspec)],
                          NN, [], [(_sds((t, d), F32), pl.BlockSpec((tm, tn), lambda i, j, k: (i, j)))], epi, True,
                          (tm, tn), comm)
    return outs[0], couts


def _mm_dw(name, a_list, b, tmo, tno, comm=None, m_part=(0, 1)):
    t, m = a_list[0].shape
    m = m // m_part[1]
    n = b.shape[1]
    tt = _tile(t, _TT)
    nk = t // tt
    tmo, tno = _tile(m, tmo), _tile(n, tno)
    first = m_part[0] * (m // tmo)

    def epi(accs, e, o):
        for acc, out in zip(accs, o):
            out[...] = acc.astype(BF16)

    aspec = pl.BlockSpec((tt, tmo), lambda i, j, k: (k, first + i))
    bspec = pl.BlockSpec((tt, tno), lambda i, j, k: (k, j))
    ospec = pl.BlockSpec((tmo, tno), lambda i, j, k: (i, j))
    if nk == 1:
        return _matmul(name, (m // tmo, n // tno, 1), 1, None, [(a, aspec, b, bspec) for a in a_list], TN, [],
                       [(_sds((m, n), BF16), ospec)] * len(a_list), epi, False, None, comm)
    return _matmul(name, (m // tmo, n // tno, nk), nk, 2, [(a, aspec, b, bspec) for a in a_list], TN, [],
                   [(_sds((m, n), BF16), ospec)] * len(a_list), epi, False, (tmo, tno), comm)


def _mm_dy(dxb, w_out, comm=None):
    t, d = dxb.shape
    m = w_out.shape[0]
    tm, tn = _tile(t, _TM), _tile(m, _TN)

    def epi(accs, e, o):
        o[0][...] = accs[0].astype(BF16)

    outs, couts = _matmul(
        "mm_dy", (t // tm, m // tn), 1, None,
        [(dxb, pl.BlockSpec((tm, d), lambda i, j: (i, 0)), w_out, pl.BlockSpec((tn, d), lambda i, j: (j, 0)))], NT, [],
        [(_sds((t, m), BF16), pl.BlockSpec((tm, tn), lambda i, j: (i, j)))], epi, True, None, comm)
    return outs[0], couts


def _mm_dh(dz, w_in, comm=None):
    t, n = dz.shape
    d = w_in.shape[0]
    tm, tn = _tile(t, _TM), _tile(d, _TN)
    nk = 2
    tk = n // nk

    def epi(accs, e, o):
        o[0][...] = accs[0]

    outs, couts = _matmul(
        "mm_dh", (t // tm, d // tn, nk), nk, 2,
        [(dz, pl.BlockSpec((tm, tk), lambda i, j, k: (i, k)), w_in, pl.BlockSpec((tn, tk), lambda i, j, k: (j, k)))], NT,
        [], [(_sds((t, d), F32), pl.BlockSpec((tm, tn), lambda i, j, k: (i, j)))], epi, True, (tm, tn), comm)
    return outs[0], couts


def _rmsnorm_fwd(x, g, comm=None):
    t, d = x.shape
    tm = min(_TM_NORM, t)

    def body(x_ref, g_ref, o_ref):
        xv = x_ref[...]
        rs = lax.rsqrt(jnp.mean(xv * xv, axis=-1, keepdims=True) + RMS_EPS)
        o_ref[...] = (xv * rs * g_ref[...]).astype(BF16)

    outs, couts = _call_hosting(
        body, "rmsnorm_fwd", (t // tm,), [_sds((t, d), BF16)],
        [pl.BlockSpec((tm, d), lambda i: (i, 0)), pl.BlockSpec((1, d), lambda i: (0, 0))],
        [pl.BlockSpec((tm, d), lambda i: (i, 0))], [x, g], [], comm)
    return outs[0], couts


def _rmsnorm_bwd_math(xv, g, dh):
    rs = lax.rsqrt(jnp.mean(xv * xv, axis=-1, keepdims=True) + RMS_EPS)
    xh = xv * rs
    gd = dh * g
    dx = rs * (gd - xh * jnp.mean(gd * xh, axis=-1, keepdims=True))
    return dx, jnp.sum(dh * xh, axis=0, keepdims=True)


def _rmsnorm_bwd(x, g, dh, dres):
    t, d = x.shape
    tm = min(_TM_NORM, t)

    def body(x_ref, g_ref, dh_ref, dres_ref, dx_ref, dxb_ref, dg_ref):
        dx, dg = _rmsnorm_bwd_math(x_ref[...], g_ref[...], dh_ref[...])
        dx = dx + dres_ref[...]
        dx_ref[...] = dx
        dxb_ref[...] = dx.astype(BF16)

        @pl.when(pl.program_id(0) == 0)
        def _():
            dg_ref[...] = dg

        @pl.when(pl.program_id(0) > 0)
        def _():
            dg_ref[...] += dg

    row = pl.BlockSpec((tm, d), lambda i: (i, 0))
    vec = pl.BlockSpec((1, d), lambda i: (0, 0))
    return pl.pallas_call(
        body, name="rmsnorm_bwd", grid=(t // tm,),
        out_shape=(_sds((t, d), F32), _sds((t, d), BF16), _sds((1, d), F32)),
        in_specs=[row, vec, row, row], out_specs=(row, row, vec), compiler_params=_cparams(1))(x, g, dh, dres)


def _loss_head(x, g, target):
    t, d = x.shape
    tm = min(_TM_NORM, t)

    def body(x_ref, g_ref, t_ref, dx_ref, dxb_ref, dg_ref, loss_ref):
        xv, gv = x_ref[...], g_ref[...]
        rs = lax.rsqrt(jnp.mean(xv * xv, axis=-1, keepdims=True) + RMS_EPS)
        diff = xv * rs * gv - t_ref[...]
        part = 0.5 * jnp.sum(jnp.mean(diff * diff, axis=-1, keepdims=True), axis=0, keepdims=True)
        part = jnp.broadcast_to(part, (1, 128))
        dx, dg = _rmsnorm_bwd_math(xv, gv, diff * (1.0 / d))
        dx_ref[...] = dx
        dxb_ref[...] = dx.astype(BF16)

        @pl.when(pl.program_id(0) == 0)
        def _():
            dg_ref[...] = dg
            loss_ref[...] = part

        @pl.when(pl.program_id(0) > 0)
        def _():
            dg_ref[...] += dg
            loss_ref[...] += part

    row = pl.BlockSpec((tm, d), lambda i: (i, 0))
    vec = pl.BlockSpec((1, d), lambda i: (0, 0))
    return pl.pallas_call(
        body, name="loss_head", grid=(t // tm,),
        out_shape=(_sds((t, d), F32), _sds((t, d), BF16), _sds((1, d), F32), _sds((1, 128), F32)),
        in_specs=[row, vec, row], out_specs=(row, row, vec, pl.BlockSpec((1, 128), lambda i: (0, 0))),
        compiler_params=_cparams(1))(x, g, target)


def _gelu(x):
    th = jnp.tanh(GELU_C * (x + GELU_A * x * x * x))
    return 0.5 * x * (1.0 + th), th


def _gelu_grad(x, th):
    return 0.5 * (1.0 + th) + 0.5 * x * (1.0 - th * th) * GELU_C * (1.0 + 3.0 * GELU_A * x * x)


def _masked_ws(ws_ref, h):
    i = lax.broadcasted_iota(jnp.int32, (BLK, BLK), 0) // CHUNK
    j = lax.broadcasted_iota(jnp.int32, (BLK, BLK), 1) // CHUNK
    return jnp.where(j <= i, ws_ref[h], 0.0)


def _shift_down(q, n, first_rows):
    rolled = pltpu.roll(q, n, 0)
    row = lax.broadcasted_iota(jnp.int32, q.shape, 0)
    for r, val in enumerate(first_rows):
        rolled = jnp.where(row == r, val, rolled)
    return rolled


def _shift_up(q, n, last_rows):
    tm = q.shape[0]
    rolled = pltpu.roll(q, tm - n, 0)
    row = lax.broadcasted_iota(jnp.int32, q.shape, 0)
    for r, val in enumerate(last_rows):
        rolled = jnp.where(row == tm - n + r, val, rolled)
    return rolled


def _mixer_specs(t, a, tm):
    hb = tm // HALO
    last = t // HALO - 1
    tile = pl.BlockSpec((tm, 5 * a), lambda i: (i, 0))
    prev = [pl.BlockSpec((HALO, a), functools.partial(lambda i, col: (jnp.maximum(i * hb - 1, 0), col), col=col))
            for col in (3, 4)]
    nxt = [pl.BlockSpec((HALO, a), functools.partial(lambda i, col: (jnp.minimum((i + 1) * hb, last), col), col=col))
           for col in (2, 3, 4)]
    return tile, prev, nxt


def _group_a_fwd(zu, zv, lng, lnb, ws_ref, bb_ref, mixed_ref, vln_ref):
    u, thu = _gelu(zu)
    v, thv = _gelu(zv)
    mu = jnp.mean(v, axis=-1, keepdims=True)
    vc = v - mu
    rs = lax.rsqrt(jnp.mean(vc * vc, axis=-1, keepdims=True) + LN_EPS)
    vhat = vc * rs
    vln_ref[...] = vhat * lng + lnb
    tm, a = zu.shape
    hd = a // HEADS
    for h in range(HEADS):
        w = _masked_ws(ws_ref, h).astype(BF16)
        for b in range(tm // BLK):
            rows, cols = pl.ds(b * BLK, BLK), pl.ds(h * hd, hd)
            mixed_ref[rows, cols] = jnp.dot(w, vln_ref[rows, cols].astype(BF16), preferred_element_type=F32) + bb_ref[h]
    return u, thu, thv, rs, vhat


def _mixer_fwd(z, ln_g, ln_b, w_spatial, bb, conv_w, gg, comm=None):
    t = z.shape[0]
    a = z.shape[1] // 5
    tm = min(_TM_MIX, t)
    tile, prev, _ = _mixer_specs(t, a, tm)

    def body(z_ref, pc_ref, ph_ref, lng_ref, lnb_ref, ws_ref, bb_ref, cw_ref, gg_ref, y_ref, mixed_ref, vln_ref):
        i = pl.program_id(0)
        zu = z_ref[:, 0:a].astype(F32)
        zv = z_ref[:, a:2 * a].astype(F32)
        u, _, _, _, _ = _group_a_fwd(zu, zv, lng_ref[...], lnb_ref[...], ws_ref, bb_ref, mixed_ref, vln_ref)
        ya = u * mixed_ref[...]
        ra = lax.rsqrt(jnp.mean(ya * ya, axis=-1, keepdims=True) + RMS_EPS)
        y_ref[:, 0:a] = (ya * ra * gg_ref[:, 0:a]).astype(BF16)

        zb = z_ref[:, 2 * a:3 * a].astype(F32)
        q = z_ref[:, 3 * a:4 * a].astype(F32) * z_ref[:, 4 * a:5 * a].astype(F32)
        qp = jnp.where(i > 0, pc_ref[...].astype(F32) * ph_ref[...].astype(F32), 0.0)
        qm1 = _shift_down(q, 1, [qp[HALO - 1:HALO]])
        qm2 = _shift_down(q, 2, [qp[HALO - 2:HALO - 1], qp[HALO - 1:HALO]])
        cv = cw_ref[0:1, :] * qm2 + cw_ref[1:2, :] * qm1 + cw_ref[2:3, :] * q
        yb = zb * cv
        rb = lax.rsqrt(jnp.mean(yb * yb, axis=-1, keepdims=True) + RMS_EPS)
        y_ref[:, a:2 * a] = (yb * rb * gg_ref[:, a:2 * a]).astype(BF16)

    full = lambda shape: pl.BlockSpec(shape, lambda i: (0,) * len(shape))
    outs, couts = _call_hosting(
        body, "mixer_fwd", (t // tm,), [_sds((t, 2 * a), BF16)],
        [tile, *prev, full((1, a)), full((1, a)), full(w_spatial.shape), full(bb.shape), full(conv_w.shape),
         full((1, 2 * a))],
        [pl.BlockSpec((tm, 2 * a), lambda i: (i, 0))], [z, z, z, ln_g, ln_b, w_spatial, bb, conv_w, gg],
        [pltpu.VMEM((tm, a), F32), pltpu.VMEM((tm, a), F32)], comm)
    return outs[0], couts


def _mixer_bwd(z, dy, ln_g, ln_b, w_spatial, bb, conv_w, gg, comm=None):
    t = z.shape[0]
    a = z.shape[1] // 5
    hd = a // HEADS
    tm = min(_TM_MIX, t)
    n_tiles = t // tm
    tile, prev, nxt = _mixer_specs(t, a, tm)
    hb = tm // HALO
    dy_tile = pl.BlockSpec((tm, 2 * a), lambda i: (i, 0))
    dy_next = pl.BlockSpec((HALO, a), lambda i: (jnp.minimum((i + 1) * hb, t // HALO - 1), 1))

    def body(z_ref, pc_ref, ph_ref, nb_ref, nc_ref, nh_ref, dy_ref, ndy_ref, lng_ref, lnb_ref, ws_ref, bb_ref, cw_ref,
             gg_ref, dz_ref, dlng_ref, dlnb_ref, dws_ref, dbb_ref, dcw_ref, dgg_ref, mixed_ref, vln_ref, dmix_ref,
             dvln_ref):
        i = pl.program_id(0)

        @pl.when(i == 0)
        def _():
            for ref in (dlng_ref, dlnb_ref, dws_ref, dbb_ref, dcw_ref, dgg_ref):
                ref[...] = jnp.zeros(ref.shape, F32)

        lng = lng_ref[...]
        zu = z_ref[:, 0:a].astype(F32)
        zv = z_ref[:, a:2 * a].astype(F32)
        u, thu, thv, rs, vhat = _group_a_fwd(zu, zv, lng, lnb_ref[...], ws_ref, bb_ref, mixed_ref, vln_ref)
        mixed = mixed_ref[...]
        ya = u * mixed
        ra = lax.rsqrt(jnp.mean(ya * ya, axis=-1, keepdims=True) + RMS_EPS)
        da = dy_ref[:, 0:a].astype(F32)
        yah = ya * ra
        dgg_ref[:, 0:a] += jnp.sum(da * yah, axis=0, keepdims=True)
        ga = da * gg_ref[:, 0:a]
        dya = ra * (ga - yah * jnp.mean(ga * yah, axis=-1, keepdims=True))
        dz_ref[:, 0:a] = (dya * mixed * _gelu_grad(zu, thu)).astype(BF16)
        dmix_ref[...] = dya * u
        for h in range(HEADS):
            w = _masked_ws(ws_ref, h).astype(BF16)
            dw = jnp.zeros((BLK, BLK), F32)
            db = jnp.zeros((BLK, hd), F32)
            for b in range(tm // BLK):
                rows, cols = pl.ds(b * BLK, BLK), pl.ds(h * hd, hd)
                dm = dmix_ref[rows, cols]
                dmb = dm.astype(BF16)
                db = db + dm
                dw = dw + lax.dot_general(dmb, vln_ref[rows, cols].astype(BF16), (NT, ((), ())),
                                          preferred_element_type=F32)
                dvln_ref[rows, cols] = lax.dot_general(w, dmb, (TN, ((), ())), preferred_element_type=F32)
            dws_ref[h] += dw
            dbb_ref[h] += db
        dvln = dvln_ref[...]
        dlng_ref[...] += jnp.sum(dvln * vhat, axis=0, keepdims=True)
        dlnb_ref[...] += jnp.sum(dvln, axis=0, keepdims=True)
        dvh = dvln * lng
        dv = rs * (dvh - jnp.mean(dvh, axis=-1, keepdims=True) - vhat * jnp.mean(dvh * vhat, axis=-1, keepdims=True))
        dz_ref[:, a:2 * a] = (dv * _gelu_grad(zv, thv)).astype(BF16)

        w0, w1, w2 = cw_ref[0:1, :], cw_ref[1:2, :], cw_ref[2:3, :]
        ggb = gg_ref[:, a:2 * a]
        zb = z_ref[:, 2 * a:3 * a].astype(F32)
        zc = z_ref[:, 3 * a:4 * a].astype(F32)
        zh = z_ref[:, 4 * a:5 * a].astype(F32)
        q = zc * zh
        qp = jnp.where(i > 0, pc_ref[...].astype(F32) * ph_ref[...].astype(F32), 0.0)
        qm1 = _shift_down(q, 1, [qp[HALO - 1:HALO]])
        qm2 = _shift_down(q, 2, [qp[HALO - 2:HALO - 1], qp[HALO - 1:HALO]])
        cv = w0 * qm2 + w1 * qm1 + w2 * q

        def conv_out_grad(zb_, cv_, dout_):
            yb = zb_ * cv_
            rb = lax.rsqrt(jnp.mean(yb * yb, axis=-1, keepdims=True) + RMS_EPS)
            ybh = yb * rb
            gb = dout_ * ggb
            dyb = rb * (gb - ybh * jnp.mean(gb * ybh, axis=-1, keepdims=True))
            return dyb * zb_, dyb * cv_, ybh

        db_out = dy_ref[:, a:2 * a].astype(F32)
        g, dzb, ybh = conv_out_grad(zb, cv, db_out)
        dgg_ref[:, a:2 * a] += jnp.sum(db_out * ybh, axis=0, keepdims=True)
        dz_ref[:, 2 * a:3 * a] = dzb.astype(BF16)
        qn = nc_ref[...].astype(F32) * nh_ref[...].astype(F32)
        zbn = nb_ref[...].astype(F32)
        cvn = w0 * _shift_down(qn, 2, [q[tm - 2:tm - 1], q[tm - 1:tm]]) + w1 * _shift_down(qn, 1, [q[tm - 1:tm]]) + w2 * qn
        gn, _, _ = conv_out_grad(zbn, cvn, ndy_ref[...].astype(F32))
        gn = jnp.where(i < n_tiles - 1, gn, 0.0)
        dq = w2 * g + w1 * _shift_up(g, 1, [gn[0:1]]) + w0 * _shift_up(g, 2, [gn[0:1], gn[1:2]])
        dz_ref[:, 3 * a:4 * a] = (dq * zh).astype(BF16)
        dz_ref[:, 4 * a:5 * a] = (dq * zc).astype(BF16)
        dcw_ref[0:1, :] += jnp.sum(g * qm2, axis=0, keepdims=True)
        dcw_ref[1:2, :] += jnp.sum(g * qm1, axis=0, keepdims=True)
        dcw_ref[2:3, :] += jnp.sum(g * q, axis=0, keepdims=True)

        @pl.when(i == n_tiles - 1)
        def _():
            for h in range(HEADS):
                dbb_ref[h] = jnp.broadcast_to(jnp.sum(dbb_ref[h], axis=1, keepdims=True), (BLK, hd))
                dws_ref[h] = _masked_ws(dws_ref, h)

    full = lambda shape: pl.BlockSpec(tuple(shape), lambda i: (0,) * len(shape))
    out_shapes = (_sds((t, 5 * a), BF16), _sds((1, a), F32), _sds((1, a), F32), _sds(w_spatial.shape, F32),
                  _sds(bb.shape, F32), _sds((8, a), F32), _sds((1, 2 * a), F32))
    return _call_hosting(
        body, "mixer_bwd", (n_tiles,), out_shapes,
        [tile, *prev, *nxt, dy_tile, dy_next, full((1, a)), full((1, a)), full(w_spatial.shape), full(bb.shape),
         full(conv_w.shape), full((1, 2 * a))],
        [tile, *[full(s.shape) for s in out_shapes[1:]]], [z, z, z, z, z, z, dy, dy, ln_g, ln_b, w_spatial, bb, conv_w, gg],
        [pltpu.VMEM((tm, a), F32)] * 4, comm)


def _all_reduce_small(pack, comm=None):
    r = pack.shape[0]
    hosted = _Hosted(comm, 1, 1)
    n_ci, n_co = len(hosted.operands), len(hosted.out_shapes)

    def body(*refs):
        in_ref, c_ins, out_ref, c_outs = refs[0], refs[1:1 + n_ci], refs[1 + n_ci], refs[2 + n_ci:2 + n_ci + n_co]
        acc_ref, recv_ref, send_sems, recv_sems = refs[2 + n_ci + n_co:6 + n_ci + n_co]
        sems = refs[6 + n_ci + n_co:]
        hosted.run("start", c_ins, c_outs, sems)
        x, y, c = _place()
        partners = [(x, y, 1 - c), (1 - x, y, c), (x, 1 - y, c)]
        acc_ref[0] = in_ref[...]
        for s, partner in enumerate(partners):
            cp = pltpu.make_async_remote_copy(
                src_ref=acc_ref.at[s], dst_ref=recv_ref.at[s], send_sem=send_sems.at[s], recv_sem=recv_sems.at[s],
                device_id=partner, device_id_type=MESH)
            cp.start()
            cp.wait()
            if s < 2:
                acc_ref[s + 1] = acc_ref[s] + recv_ref[s]
            else:
                out_ref[...] = acc_ref[s] + recv_ref[s]
        for stage in ("mid1", "mid2", "finish"):
            hosted.run(stage, c_ins, c_outs, sems)

    vmem = pl.BlockSpec(memory_space=pltpu.VMEM)
    res = pl.pallas_call(
        body, name="all_reduce_small", out_shape=tuple([_sds(pack.shape, F32)] + hosted.out_shapes),
        in_specs=[vmem] + hosted.in_specs, out_specs=tuple([vmem] + hosted.out_specs),
        input_output_aliases=hosted.aliases,
        scratch_shapes=[pltpu.VMEM((3, r, 128), F32), pltpu.VMEM((3, r, 128), F32), pltpu.SemaphoreType.DMA((3,)),
                        pltpu.SemaphoreType.DMA((3,))] + hosted.scratch,
        compiler_params=pltpu.CompilerParams(vmem_limit_bytes=VMEM_LIMIT_V7X),
    )(pack, *hosted.operands)
    return res[0], list(res[1:])


def _adamw_math(w, g, m, v):
    m = ADAM_B1 * m + (1.0 - ADAM_B1) * g
    v = ADAM_B2 * v + (1.0 - ADAM_B2) * (g * g)
    m_hat = m / (1.0 - ADAM_B1 ** ADAM_STEP)
    v_hat = v / (1.0 - ADAM_B2 ** ADAM_STEP)
    delta = -ADAM_LR * (m_hat / (jnp.sqrt(v_hat) + ADAM_EPS) + ADAM_WD * w)
    return delta, m, v


def _adamw_big(name, land, w, m, v, comm=None):
    nl, n_slots, r, c = land.shape
    tr = max(8, min(r, (256 * 640) // c // 8 * 8))
    while r % tr:
        tr -= 8
    grid = (nl, r // tr)
    hosted = _Hosted(comm, 4, 4)
    n_ci, n_co = len(hosted.operands), len(hosted.out_shapes)

    def body(*refs):
        land_ref, w_ref, m_ref, v_ref = refs[:4]
        c_ins = refs[4:4 + n_ci]
        g_out, d_out, m_out, v_out = refs[4 + n_ci:8 + n_ci]
        c_outs = refs[8 + n_ci:8 + n_ci + n_co]
        sems = refs[8 + n_ci + n_co:]

        def compute():
            g = land_ref[0].astype(F32)
            for s in range(1, n_slots):
                g = g + land_ref[s].astype(F32)
            delta, mn, vn = _adamw_math(w_ref[...], g, m_ref[...], v_ref[...])
            g_out[...] = g
            d_out[...] = delta
            m_out[...] = mn
            v_out[...] = vn

        hosted.wrap(grid, compute, c_ins, c_outs, sems)

    blk = pl.BlockSpec((None, tr, c), lambda l, i: (l, i, 0))
    res = pl.pallas_call(
        body, name=name, grid=grid, out_shape=tuple([_sds((nl, r, c), F32)] * 4 + hosted.out_shapes),
        in_specs=[pl.BlockSpec((None, n_slots, tr, c), lambda l, i: (l, 0, i, 0)), blk, blk, blk] + hosted.in_specs,
        out_specs=tuple([blk] * 4 + hosted.out_specs), input_output_aliases=hosted.aliases,
        scratch_shapes=hosted.scratch, compiler_params=_cparams(2))(land, w, m, v, *hosted.operands)
    return list(res[:4]), list(res[4:])


def _adamw_small(g, w, m, v):
    def body(g_ref, w_ref, m_ref, v_ref, d_out, m_out, v_out):
        delta, mn, vn = _adamw_math(w_ref[...], g_ref[...], m_ref[...], v_ref[...])
        d_out[...] = delta
        m_out[...] = mn
        v_out[...] = vn

    return pl.pallas_call(body, name="adamw_small", out_shape=tuple([_sds(g.shape, F32)] * 3),
                          compiler_params=pltpu.CompilerParams(vmem_limit_bytes=VMEM_LIMIT_V7X))(g, w, m, v)


def _rows(a):
    return a.reshape(-1, 128)


BIG = ["w_in", "w_out", "w_gate", "w_up", "w_down"]
AG_HOSTS = {
    ("norm1", 0): [("w_in", 0), ("conv_w", 0)],
    ("mm_in", 0): [("w_out", 0), ("w_gate", 0, 0, 2)], ("mixer", 0): [("w_gate", 0, 1, 2)],
    ("mm_out", 0): [("w_up", 0, 0, 2)], ("norm2", 0): [("w_up", 0, 1, 2)],
    ("mm_swiglu", 0): [("w_down", 0), ("w_in", 1)], ("mm_down", 0): [("w_out", 1), ("w_gate", 1, 0, 2)],
    ("mm_in", 1): [("w_gate", 1, 1, 2)], ("mixer", 1): [("w_up", 1, 0, 2)], ("mm_out", 1): [("w_up", 1, 1, 2)],
    ("mm_swiglu", 1): [("w_down", 1)],
}


def kernel(x, norm1_g, w_in, gmlp_ln_g, gmlp_ln_b, w_spatial, b_spatial, conv_w, group_norm_g, w_out, norm2_g, w_gate, w_up, w_down, final_norm_g, loss_target, m_norm1_g, m_w_in, m_gmlp_ln_g, m_gmlp_ln_b, m_w_spatial, m_b_spatial, m_conv_w, m_group_norm_g, m_w_out, m_norm2_g, m_w_gate, m_w_up, m_w_down, m_final_norm_g, v_norm1_g, v_w_in, v_gmlp_ln_g, v_gmlp_ln_b, v_w_spatial, v_b_spatial, v_conv_w, v_group_norm_g, v_w_out, v_norm2_g, v_w_gate, v_w_up, v_w_down, v_final_norm_g):
    nl = N_LAYERS
    t, d = x.shape[1], x.shape[2]
    a = d // 2
    hd = a // HEADS
    xin = x.reshape(t, d)
    target = loss_target.reshape(t, d)
    me = _index(_place())

    tr = lambda w: jnp.transpose(w, (0, 2, 1))
    big = {"w_in": w_in, "w_out": w_out, "w_gate": tr(w_gate), "w_up": tr(w_up), "w_down": w_down}
    big_m = {"w_in": m_w_in, "w_out": m_w_out, "w_gate": tr(m_w_gate), "w_up": tr(m_w_up), "w_down": m_w_down}
    big_v = {"w_in": v_w_in, "w_out": v_w_out, "w_gate": tr(v_w_gate), "w_up": tr(v_w_up), "w_down": v_w_down}
    block = {k: big[k].shape[1:] for k in BIG}
    view = {k: _cols_view(block[k][1]) if k == "w_in" else _rows_view(block[k][0]) for k in BIG}
    full_shape = {k: (block[k][0], N_DEV * block[k][1]) if k == "w_in" else (N_DEV * block[k][0], block[k][1])
                  for k in BIG}

    weights = {}
    shards = {(k, l): big[k][l].astype(BF16) for k in BIG for l in range(nl)}

    def ag_spec(k, l, part=0, n_parts=1):
        if k == "conv_w":
            return (conv_w, _sds((N_DEV, *conv_w.shape), F32), _SLOT_WHOLE, (0,), None)
        halves = (_cols_halves(*block[k], part, n_parts) if k == "w_in" else _rows_halves(block[k][0], part, n_parts))
        return (shards[(k, l)], _sds(full_shape[k], BF16), halves, (0, 1), weights.get((k, l)))

    bb = jnp.broadcast_to(b_spatial[..., None], (nl, HEADS, BLK, hd))

    def hosted(name, l):
        keys = AG_HOSTS.get((name, l), [])
        return keys, ([_ag_piece([ag_spec(*key) for key in keys])] if keys else None)

    def landed(keys, couts):
        for key, arr in zip(keys, couts):
            weights[key[:2]] = arr

    saved = []
    xl = xin
    for l in range(nl):
        keys, comm = hosted("norm1", l)
        h, couts = _rmsnorm_fwd(xl, norm1_g[l:l + 1], comm)
        landed(keys, couts)
        if l == 0:
            conv_full = jnp.transpose(weights[("conv_w", 0)], (1, 2, 0, 3)).reshape(nl, 3, a)
        keys, comm = hosted("mm_in", l)
        z, couts = _mm_in(h, weights[("w_in", l)], comm)
        landed(keys, couts)
        keys, comm = hosted("mixer", l)
        y, couts = _mixer_fwd(z, gmlp_ln_g[l:l + 1], gmlp_ln_b[l:l + 1], w_spatial[l], bb[l], conv_full[l],
                              group_norm_g[l:l + 1], comm)
        landed(keys, couts)
        keys, comm = hosted("mm_out", l)
        x1, couts = _mm_out(y, weights[("w_out", l)], xl, comm)
        landed(keys, couts)
        keys, comm = hosted("norm2", l)
        h2, couts = _rmsnorm_fwd(x1, norm2_g[l:l + 1], comm)
        landed(keys, couts)
        keys, comm = hosted("mm_swiglu", l)
        (act, dact_dgate, dact_dup), couts = _mm_swiglu(h2, weights[("w_gate", l)], weights[("w_up", l)], comm)
        landed(keys, couts)
        keys, comm = hosted("mm_down", l)
        x2, couts = _mm_down(act, weights[("w_down", l)], x1, comm)
        landed(keys, couts)
        saved.append(dict(x=xl, h=h, z=z, y=y, x1=x1, h2=h2, dact_dgate=dact_dgate, dact_dup=dact_dup, act=act))
        xl = x2

    dx, dxb, d_final_g, loss_part = _loss_head(xl, final_norm_g.reshape(1, d), target)
    small = [None] * nl
    core = lax.axis_index("c").astype(jnp.int32).reshape(1)
    half_of = {"w_in_a": ("w_in", 0), "w_in_b": ("w_in", 1)}
    for k in half_of:
        block[k] = (block["w_in"][0] // 2, block["w_in"][1])
        view[k] = view["w_in"]
    stage_shape = {k: _sds((N_CHIPS, *block[k]), BF16) for k in block}
    land_shape = {k: _sds((nl, N_CHIPS, *block[k]), BF16) for k in BIG}
    grads = [dict() for _ in range(nl)]
    stages = [dict() for _ in range(nl)]
    sums = [dict() for _ in range(nl)]
    lands = {k: None for k in BIG}

    def core_job(l, keys):
        def sink(outs):
            stages[l].update(zip(keys, outs))
        return _rs_core_piece([(grads[l][k], stage_shape[k], view[k]) for k in keys]), sink

    def chip_job(l, items):
        keys = [half_of.get(item[0], (item[0], 0))[0] for item in items]

        def rows(k, p0, p1, n_parts):
            per = block[k][0] // n_parts
            landing = half_of.get(k, (k, 0))[1] * block[k][0]
            return (p0 * per, landing + p0 * per, (p1 - p0) * per)

        def sink(outs):
            lands.update(zip(keys, outs))
        return _rs_chip_piece([(sums[l][k], land_shape[key], rows(k, p0, p1, n_parts), lands[key])
                               for key, (k, p0, p1, n_parts) in zip(keys, items)], l), sink

    def add_up(l, keys):
        for k in keys:
            sums[l][k] = _chip_sums(f"chip_sums_{k}", grads[l][k], stages[l][k], k.startswith("w_in"), core)

    def host(*jobs):
        def deliver(couts):
            i = 0
            for piece, sink in jobs:
                n_out = len(piece.out_shapes)
                sink(couts[i:i + n_out])
                i += n_out
        return [piece for piece, _ in jobs], deliver

    whole = lambda k: (k, 0, 1, 1)
    rep = ["norm1_g", "gmlp_ln_g", "gmlp_ln_b", "w_spatial", "b_spatial", "group_norm_g", "norm2_g"]
    rep_w = dict(norm1_g=norm1_g, gmlp_ln_g=gmlp_ln_g, gmlp_ln_b=gmlp_ln_b, w_spatial=w_spatial, b_spatial=b_spatial,
                 group_norm_g=group_norm_g, norm2_g=norm2_g)
    rep_m = dict(norm1_g=m_norm1_g, gmlp_ln_g=m_gmlp_ln_g, gmlp_ln_b=m_gmlp_ln_b, w_spatial=m_w_spatial,
                 b_spatial=m_b_spatial, group_norm_g=m_group_norm_g, norm2_g=m_norm2_g)
    rep_v = dict(norm1_g=v_norm1_g, gmlp_ln_g=v_gmlp_ln_g, gmlp_ln_b=v_gmlp_ln_b, w_spatial=v_w_spatial,
                 b_spatial=v_b_spatial, group_norm_g=v_group_norm_g, norm2_g=v_norm2_g)

    def small_grad_parts():
        parts = [_rows(jnp.stack([small[l][k].reshape(rep_w[k].shape[1:]) for l in range(nl)])) for k in rep]
        parts.append(_rows(d_final_g))
        parts.append(_rows(jnp.stack([small[l]["conv_w"] for l in range(nl)])))
        parts.append(jnp.broadcast_to(loss_part, (8, 128)))
        rows = sum(p.shape[0] for p in parts)
        parts.append(jnp.zeros((-rows % 16, 128), F32))
        return parts

    for l in reversed(range(nl)):
        s = saved[l]
        wi, wo, wgt, wut, wd = [weights[(k, l)] for k in BIG]
        later = l + 1 < nl
        comm, deliver = host(chip_job(l + 1, [("w_in", 0, 1, 2)])) if later else host()
        (grads[l]["w_down"],), couts = _mm_dw("mm_dw_down", [s["act"]], dxb, 2816, 1024, comm)
        deliver(couts)
        comm, deliver = (host(core_job(l, ["w_down"]), chip_job(l + 1, [("w_in", 1, 2, 2)])) if later
                         else host(core_job(l, ["w_down"])))
        (dgate, dup), couts = _mm_dact(dxb, wd, s["dact_dgate"], s["dact_dup"], comm)
        deliver(couts)
        add_up(l, ["w_down"])
        comm, deliver = host(chip_job(l, [("w_down", 0, 3, 4)]))
        (grads[l]["w_gate"],), couts = _mm_dw("mm_dw_gate", [dgate], s["h2"], 2816, 1024, comm)
        deliver(couts)
        comm, deliver = host(chip_job(l, [("w_down", 3, 4, 4)]), core_job(l, ["w_gate"]))
        (grads[l]["w_up"],), couts = _mm_dw("mm_dw_up", [dup], s["h2"], 2816, 1024, comm)
        deliver(couts)
        add_up(l, ["w_gate"])
        comm, deliver = host(chip_job(l, [whole("w_gate")]), core_job(l, ["w_up"]))
        dh2, couts = _mm_dh2(dgate, dup, wgt, wut, comm)
        deliver(couts)
        add_up(l, ["w_up"])
        dx1, dx1b, d_n2 = _rmsnorm_bwd(s["x1"], norm2_g[l:l + 1], dh2, dx)
        comm, deliver = host(chip_job(l, [("w_up", 0, 1, 4)]))
        dy, couts = _mm_dy(dx1b, wo, comm)
        deliver(couts)
        comm, deliver = host(chip_job(l, [("w_up", 1, 2, 4)]))
        (grads[l]["w_out"],), couts = _mm_dw("mm_dw_out", [s["y"]], dx1b, 1024, 1024, comm)
        deliver(couts)
        comm, deliver = host(chip_job(l, [("w_up", 2, 4, 4)]), core_job(l, ["w_out"]))
        (dz, d_lng, d_lnb, d_ws, d_bb, d_cw, d_gg), couts = _mixer_bwd(
            s["z"], dy, gmlp_ln_g[l:l + 1], gmlp_ln_b[l:l + 1], w_spatial[l], bb[l], conv_full[l], group_norm_g[l:l + 1],
            comm)
        deliver(couts)
        add_up(l, ["w_out"])
        small[l] = dict(norm1_g=jnp.zeros((1, d), F32), gmlp_ln_g=d_lng, gmlp_ln_b=d_lnb, w_spatial=d_ws,
                        b_spatial=d_bb[:, :, 0], group_norm_g=d_gg, norm2_g=d_n2, conv_w=d_cw[0:3])
        if l > 0:
            comm, deliver = host(chip_job(l, [whole("w_out")]))
            (grads[l]["w_in"],), couts = _mm_dw("mm_dw_in", [s["h"]], dz, 2048, 1024, comm)
            deliver(couts)
            comm, deliver = host(core_job(l, ["w_in"]))
            dh, couts = _mm_dh(dz, wi, comm)
            deliver(couts)
            add_up(l, ["w_in"])
        else:
            comm, deliver = host(chip_job(l, [whole("w_out")]))
            (grads[l]["w_in_a"],), couts = _mm_dw("mm_dw_in_a", [s["h"]], dz, 1024, 1024, comm, m_part=(0, 2))
            deliver(couts)
            comm, deliver = host(core_job(l, ["w_in_a"]))
            (grads[l]["w_in_b"],), couts = _mm_dw("mm_dw_in_b", [s["h"]], dz, 1024, 1024, comm, m_part=(1, 2))
            deliver(couts)
            add_up(l, ["w_in_a"])
            parts = small_grad_parts()
            reduced = []
            comm, deliver = host(chip_job(l, [whole("w_in_a")]), core_job(l, ["w_in_b"]),
                                 (_all_reduce_piece(jnp.concatenate(parts, axis=0)), reduced.extend))
            dh, couts = _mm_dh(dz, wi, comm)
            deliver(couts)
            add_up(l, ["w_in_b"])
        dx, dxb, small[l]["norm1_g"] = _rmsnorm_bwd(s["x"], norm1_g[l:l + 1], dh, dx1)
    grad_x = dx.reshape(x.shape)

    sizes = [p.shape[0] for p in parts]
    comm, deliver = host(chip_job(0, [whole("w_in_b")]))
    last, couts = _all_reduce_small(_rows(small[0]["norm1_g"]), comm)
    deliver(couts)
    total = lax.dynamic_update_slice(reduced[0], last, (0, 0))
    offs = [0]
    for n in sizes:
        offs.append(offs[-1] + n)
    pieces = [total[offs[i]:offs[i + 1]] for i in range(len(parts))]
    loss = pieces[len(rep) + 2][0, 0]
    conv_g_full = pieces[len(rep) + 1].reshape(nl, 3, N_DEV, a // N_DEV)
    conv_g = lax.dynamic_index_in_dim(conv_g_full, me, axis=2, keepdims=False)
    n_rep = offs[len(rep) + 1]
    pad = jnp.zeros((2, 128), F32)

    def small_pack(named, final, conv):
        return jnp.concatenate([_rows(named[k]) for k in rep] + [_rows(final), _rows(conv), pad], axis=0)

    g_small = jnp.concatenate([total[:n_rep], _rows(conv_g), pad], axis=0)
    d_small, m_small, v_small = _adamw_small(
        g_small, small_pack(rep_w, final_norm_g, conv_w), small_pack(rep_m, m_final_norm_g, m_conv_w),
        small_pack(rep_v, v_final_norm_g, v_conv_w))

    def unpack(packed):
        out = {k: packed[offs[i]:offs[i + 1]].reshape(rep_w[k].shape) for i, k in enumerate(rep)}
        out["final_norm_g"] = packed[offs[len(rep)]:n_rep].reshape(final_norm_g.shape)
        out["conv_w"] = packed[n_rep:n_rep + 6].reshape(conv_w.shape)
        return out

    res = {"grad": unpack(g_small), "delta": unpack(d_small), "m": unpack(m_small), "v": unpack(v_small)}

    for k in BIG:
        outs, _ = _adamw_big(f"adamw_{k}", lands[k], big[k], big_m[k], big_v[k])
        if k in ("w_gate", "w_up"):
            outs = [tr(o) for o in outs]
        res["grad"][k], res["delta"][k], res["m"][k], res["v"][k] = outs

    order = ["norm1_g", "w_in", "gmlp_ln_g", "gmlp_ln_b", "w_spatial", "b_spatial", "conv_w", "group_norm_g", "w_out",
             "norm2_g", "w_gate", "w_up", "w_down", "final_norm_g"]
    return (loss, grad_x, *[res["grad"][k] for k in order], *[res["delta"][k] for k in order],
            *[res["m"][k] for k in order], *[res["v"][k] for k in order])
```

```python
import functools
import math
import operator

import jax
import jax.numpy as jnp
from jax import lax
from jax.experimental import pallas as pl
from jax.experimental.pallas import tpu as pltpu

F32 = jnp.float32
BF16 = jnp.bfloat16
MESH = pl.DeviceIdType.MESH

N_DEV = 8
N_LAYERS = 2
HEADS = 8
BLK = 128
CHUNK = 64
HALO = 16
RMS_EPS = 1e-6
LN_EPS = 1e-5
ADAM_LR, ADAM_B1, ADAM_B2, ADAM_EPS, ADAM_WD, ADAM_STEP = 0.001, 0.9, 0.999, 1e-8, 0.01, 10
GELU_C = math.sqrt(2.0 / math.pi)
GELU_A = 0.044715

VMEM_LIMIT_V7X = 56 * 1024 * 1024
_TM = 1024
_TN = 1024
_TT = 1024
_TM_MIX = 256
_TM_NORM = 512


def _cparams(n_axes):
    return pltpu.CompilerParams(dimension_semantics=("arbitrary",) * n_axes, vmem_limit_bytes=VMEM_LIMIT_V7X)


def _sds(shape, dtype):
    return jax.ShapeDtypeStruct(tuple(shape), dtype)


def _place():
    return lax.axis_index("x"), lax.axis_index("y"), lax.axis_index("c")


def _index(place):
    return 4 * place[0] + 2 * place[1] + place[2]


class _Piece:
    def __init__(self, operands, out_shapes, aliases, n_sems, start, finish, mid1=None, mid2=None, vmem=(),
                 hooks=(0.6, 0.87)):
        self.operands, self.out_shapes, self.aliases, self.n_sems = list(operands), list(out_shapes), dict(aliases), n_sems
        self.vmem = list(vmem)
        self.hooks = hooks
        nothing = lambda ctx: None
        self.start, self.mid1, self.mid2, self.finish = start, mid1 or nothing, mid2 or nothing, finish


class _Ctx:
    def __init__(self, ins, outs, sems, offs):
        self.ins, self.outs, self.sems = ins, outs, sems
        self.o_in, self.o_out, self.o_send, self.o_recv, self.o_loc, self.o_vmem = offs

    def vmem(self, i):
        return self.sems[3 + self.o_vmem + i]

    def inp(self, i):
        return self.ins[self.o_in + i]

    def out(self, i):
        return self.outs[self.o_out + i]

    def send(self, k):
        return self.sems[0].at[self.o_send + k]

    def recv(self, k):
        return self.sems[1].at[self.o_recv + k]

    def local(self, k):
        return self.sems[2].at[self.o_loc + k]


class _Hosted:
    def __init__(self, pieces, n_in_before, n_out_before):
        self.pieces = [p for p in (pieces or []) if p is not None]
        self.operands, self.out_shapes, self.aliases, self.offs = [], [], {}, []
        counts, vmem = [0, 0, 0], []
        for p in self.pieces:
            self.offs.append((len(self.operands), len(self.out_shapes), *counts, len(vmem)))
            for i, j in p.aliases.items():
                self.aliases[n_in_before + len(self.operands) + i] = n_out_before + len(self.out_shapes) + j
            self.operands += p.operands
            self.out_shapes += p.out_shapes
            counts = [c + n for c, n in zip(counts, p.n_sems)]
            vmem += p.vmem
        hbm = pl.BlockSpec(memory_space=pl.ANY)
        self.in_specs = [hbm] * len(self.operands)
        self.out_specs = [hbm] * len(self.out_shapes)
        self.scratch = ([pltpu.SemaphoreType.DMA((max(c, 1),)) for c in counts] + vmem) if self.pieces else []

    def run(self, stage, ins, outs, sems):
        for p, offs in zip(self.pieces, self.offs):
            getattr(p, stage)(_Ctx(ins, outs, sems, offs))

    def wrap(self, grid, compute, ins, outs, sems):
        if not self.pieces:
            compute()
            return
        n_steps = math.prod(grid)
        lin = 0
        for ax, g in enumerate(grid):
            lin = lin * g + pl.program_id(ax)
        pl.when(lin == 0)(lambda: self.run("start", ins, outs, sems))
        compute()
        for stage, which in (("mid1", 0), ("mid2", 1)):
            for p, offs in zip(self.pieces, self.offs):
                at = min(n_steps - 1, int(p.hooks[which] * n_steps))
                pl.when(lin == at)(functools.partial(getattr(p, stage), _Ctx(ins, outs, sems, offs)))
        pl.when(lin == n_steps - 1)(lambda: self.run("finish", ins, outs, sems))


def _cols_view(width):
    return lambda ref, p: ref.at[:, pl.ds(pl.multiple_of(p * width, 128), width)]


def _rows_view(height):
    return lambda ref, p: ref.at[pl.ds(pl.multiple_of(p * height, 16), height), :]


def _cols_halves(rows, width, part, n_parts):
    hr = rows // n_parts // 2
    at = lambda h: pl.ds(part * 2 * hr + h * hr, hr)
    return (lambda ref, p, h: ref.at[at(h), pl.ds(pl.multiple_of(p * width, 128), width)],
            lambda ref, h: ref.at[at(h), :], 2)


def _rows_halves(height, part, n_parts):
    hh = height // n_parts // 2
    return (lambda ref, p, h: ref.at[pl.ds(pl.multiple_of(p * height + part * 2 * hh + h * hh, 16), hh), :],
            lambda ref, h: ref.at[pl.ds(part * 2 * hh + h * hh, hh), :], 2)


_SLOT_WHOLE = (lambda ref, p, h: ref.at[p], lambda ref, h: ref, 1)


def _ag_piece(specs):
    units = [(a, h) for a, s in enumerate(specs) for h in s[3]]

    def plan(ctx):
        x, y, c = _place()
        me, sib, xn, yn, dg = (x, y, c), (x, y, 1 - c), (1 - x, y, c), (x, 1 - y, c), (1 - x, 1 - y, c)

        def copy(u, k, block, to, from_shard=False):
            a, h = units[u]
            dst_of, src_of, _ = specs[a][2]
            dst = dst_of(ctx.out(a), _index(block), h)
            return pltpu.make_async_remote_copy(
                src_ref=src_of(ctx.inp(a), h) if from_shard else dst, dst_ref=dst, send_sem=ctx.send(7 * u + k),
                recv_sem=ctx.recv(7 * u + k), device_id=to, device_id_type=MESH)

        def local(u):
            a, h = units[u]
            dst_of, src_of, _ = specs[a][2]
            return pltpu.make_async_copy(src_of(ctx.inp(a), h), dst_of(ctx.out(a), _index(me), h), ctx.local(u))

        def relay(u):
            return copy(u, 3, xn, yn) if units[u][1] % 2 == 0 else copy(u, 3, yn, xn)

        return me, sib, xn, yn, dg, c, copy, local, relay

    def start(ctx):
        me, sib, xn, yn, dg, c, copy, local, relay = plan(ctx)
        for u in range(len(units)):
            local(u).start()
            for k, to in enumerate((sib, xn, yn)):
                copy(u, k, me, to, from_shard=True).start()

    def mid1(ctx):
        me, sib, xn, yn, dg, c, copy, local, relay = plan(ctx)
        for u in range(len(units)):
            copy(u, 1, xn, me).wait_recv()
            copy(u, 2, yn, me).wait_recv()
            relay(u).start()
            copy(u, 4, xn, sib).start()
            copy(u, 5, yn, sib).start()

    def mid2(ctx):
        me, sib, xn, yn, dg, c, copy, local, relay = plan(ctx)
        for u in range(len(units)):
            copy(u, 3, dg, me).wait_recv()
            copy(u, 6, dg, sib).start()

    def finish(ctx):
        me, sib, xn, yn, dg, c, copy, local, relay = plan(ctx)
        other = lambda place: (place[0], place[1], 1 - c)
        for u in range(len(units)):
            for k, block in ((0, sib), (4, other(xn)), (5, other(yn)), (6, other(dg))):
                copy(u, k, block, me).wait_recv()
        for u in range(len(units)):
            for k, to in enumerate((sib, xn, yn)):
                copy(u, k, me, to, from_shard=True).wait_send()
            relay(u).wait_send()
            for k, block in ((4, xn), (5, yn), (6, dg)):
                copy(u, k, block, sib).wait_send()
            local(u).wait()

    n_u = len(units)
    operands, aliases = [s[0] for s in specs], {}
    for a, spec in enumerate(specs):
        if spec[4] is not None:
            aliases[len(operands)] = a
            operands.append(spec[4])
    return _Piece(operands, [s[1] for s in specs], aliases, (7 * n_u, 7 * n_u, n_u), start, finish, mid1, mid2)


N_CHIPS = 4


def _rs_core_piece(specs):
    n = len(specs)

    def copies(ctx):
        x, y, c = _place()
        out = []
        for a in range(n):
            for q in range(N_CHIPS):
                out.append(pltpu.make_async_remote_copy(
                    src_ref=specs[a][2](ctx.inp(a), 2 * q + (1 - c)), dst_ref=ctx.out(a).at[q],
                    send_sem=ctx.send(N_CHIPS * a + q), recv_sem=ctx.recv(N_CHIPS * a + q), device_id=(x, y, 1 - c),
                    device_id_type=MESH))
        return out

    def start(ctx):
        for cp in copies(ctx):
            cp.start()

    def finish(ctx):
        for cp in copies(ctx):
            cp.wait_recv()
            cp.wait_send()

    return _Piece([s[0] for s in specs], [s[1] for s in specs], {}, (N_CHIPS * n, N_CHIPS * n, 0), start, finish)


def _rs_chip_piece(specs, layer):
    n = len(specs)
    hops = [(1, 0), (0, 1), (1, 1)]

    def copies(ctx):
        x, y, c = _place()
        mine = 2 * x + y
        out = []
        for a in range(n):
            first, landing, size = specs[a][2]
            rows, to = pl.ds(first, size), pl.ds(landing, size)
            sums, land = ctx.inp(a), ctx.out(a)
            out.append((pltpu.make_async_copy(sums.at[mine, rows], land.at[layer, mine, to], ctx.local(a)), None))
            for j, (dx, dy) in enumerate(hops):
                px, py = x ^ dx, y ^ dy
                peer = 2 * px + py
                send = pltpu.make_async_remote_copy(
                    src_ref=sums.at[peer, rows], dst_ref=land.at[layer, mine, to], send_sem=ctx.send(3 * a + j),
                    recv_sem=ctx.recv(3 * a + j), device_id=(px, py, c), device_id_type=MESH)
                recv = pltpu.make_async_remote_copy(
                    src_ref=sums.at[peer, rows], dst_ref=land.at[layer, peer, to], send_sem=ctx.send(3 * a + j),
                    recv_sem=ctx.recv(3 * a + j), device_id=(px, py, c), device_id_type=MESH)
                out.append((send, recv))
        return out

    def start(ctx):
        for send, _ in copies(ctx):
            send.start()

    def finish(ctx):
        for send, recv in copies(ctx):
            if recv is None:
                send.wait()
            else:
                recv.wait_recv()
                send.wait_send()

    operands, aliases = [s[0] for s in specs], {}
    for a, spec in enumerate(specs):
        if spec[3] is not None:
            aliases[len(operands)] = a
            operands.append(spec[3])
    return _Piece(operands, [s[1] for s in specs], aliases, (3 * n, 3 * n, n), start, finish)


def _all_reduce_piece(pack):
    r = pack.shape[0]
    half = r // 2

    def plan(ctx):
        x, y, c = _place()
        acc, got = ctx.vmem(0), ctx.vmem(1)
        mine = pl.ds(pl.multiple_of(c * half, 8), half)
        sib = (x, y, 1 - c)
        copies = [
            pltpu.make_async_remote_copy(src_ref=acc.at[0], dst_ref=got.at[0], send_sem=ctx.send(0), recv_sem=ctx.recv(0),
                                         device_id=sib, device_id_type=MESH),
            pltpu.make_async_remote_copy(src_ref=acc.at[1, mine], dst_ref=got.at[1, mine], send_sem=ctx.send(1),
                                         recv_sem=ctx.recv(1), device_id=(1 - x, y, c), device_id_type=MESH),
            pltpu.make_async_remote_copy(src_ref=acc.at[2, mine], dst_ref=got.at[2, mine], send_sem=ctx.send(2),
                                         recv_sem=ctx.recv(2), device_id=(x, 1 - y, c), device_id_type=MESH),
            pltpu.make_async_remote_copy(src_ref=acc.at[3, mine], dst_ref=acc.at[3, mine], send_sem=ctx.send(3),
                                         recv_sem=ctx.recv(3), device_id=sib, device_id_type=MESH),
        ]
        other = pl.ds(pl.multiple_of((1 - c) * half, 8), half)
        arrival = pltpu.make_async_remote_copy(src_ref=acc.at[3, other], dst_ref=acc.at[3, other], send_sem=ctx.send(3),
                                               recv_sem=ctx.recv(3), device_id=sib, device_id_type=MESH)
        return acc, got, mine, copies, arrival

    def start(ctx):
        acc, got, mine, copies, arrival = plan(ctx)
        load = pltpu.make_async_copy(ctx.inp(0), acc.at[0], ctx.local(0))
        load.start()
        load.wait()
        copies[0].start()

    def mid1(ctx):
        acc, got, mine, copies, arrival = plan(ctx)
        copies[0].wait()
        acc[1] = acc[0] + got[0]
        copies[1].start()

    def mid2(ctx):
        acc, got, mine, copies, arrival = plan(ctx)
        copies[1].wait()
        acc[2, mine] = acc[1, mine] + got[1, mine]
        copies[2].start()

    def finish(ctx):
        acc, got, mine, copies, arrival = plan(ctx)
        copies[2].wait()
        acc[3, mine] = acc[2, mine] + got[2, mine]
        copies[3].start()
        copies[3].wait_send()
        arrival.wait_recv()
        store = pltpu.make_async_copy(acc.at[3], ctx.out(0), ctx.local(0))
        store.start()
        store.wait()

    return _Piece([pack], [_sds(pack.shape, F32)], {}, (4, 4, 1), start, finish, mid1, mid2,
                  vmem=[pltpu.VMEM((4, r, 128), F32), pltpu.VMEM((3, r, 128), F32)], hooks=(0.25, 0.6))


def _chip_sums(name, grad, stage, by_cols, core):
    _, r, c = stage.shape
    tr = r
    while tr * c > 1024 * 1024 or r % tr or tr % 16:
        tr -= 16
    n_t = r // tr

    def body(core_ref, g_ref, s_ref, o_ref):
        o_ref[...] = (g_ref[...].astype(F32) + s_ref[...].astype(F32)).astype(BF16)

    if by_cols:
        gspec = pl.BlockSpec((tr, c), lambda q, i, core_ref: (i, 2 * q + core_ref[0]))
    else:
        gspec = pl.BlockSpec((tr, c), lambda q, i, core_ref: ((2 * q + core_ref[0]) * n_t + i, 0))
    sspec = pl.BlockSpec((None, tr, c), lambda q, i, core_ref: (q, i, 0))
    return pl.pallas_call(
        body, name=name, out_shape=_sds(stage.shape, BF16),
        grid_spec=pltpu.PrefetchScalarGridSpec(num_scalar_prefetch=1, grid=(N_CHIPS, n_t), in_specs=[gspec, sspec],
                                               out_specs=sspec),
        compiler_params=_cparams(2))(core, grad, stage)


def _call_hosting(body, name, grid, out_shapes, in_specs, out_specs, operands, scratch, comm):
    n_in, n_out, n_scr = len(operands), len(out_shapes), len(scratch)
    hosted = _Hosted(comm, n_in, n_out)
    n_ci, n_co = len(hosted.operands), len(hosted.out_shapes)

    def hosting_body(*refs):
        ins, rest = refs[:n_in], refs[n_in:]
        c_ins, rest = rest[:n_ci], rest[n_ci:]
        outs, rest = rest[:n_out], rest[n_out:]
        c_outs, rest = rest[:n_co], rest[n_co:]
        hosted.wrap(grid, lambda: body(*ins, *outs, *rest[:n_scr]), c_ins, c_outs, rest[n_scr:])

    res = pl.pallas_call(
        hosting_body, name=name, grid=grid, out_shape=tuple(list(out_shapes) + hosted.out_shapes),
        in_specs=list(in_specs) + hosted.in_specs, out_specs=tuple(list(out_specs) + hosted.out_specs),
        input_output_aliases=hosted.aliases, scratch_shapes=list(scratch) + hosted.scratch,
        compiler_params=_cparams(len(grid)))(*operands, *hosted.operands)
    return list(res[:n_out]), list(res[n_out:])


def _matmul(name, grid, nk, kaxis, pairs, dims, extras, outs, epilogue, sum_pairs, acc_shape, comm=None, split=None):
    n_p, n_e, n_o = len(pairs), len(extras), len(outs)
    n_acc = 0 if nk == 1 else (1 if sum_pairs else n_p)
    n_in = 2 * n_p + n_e
    hosted = _Hosted(comm, n_in, n_o)
    n_ci, n_co = len(hosted.operands), len(hosted.out_shapes)

    def body(*refs):
        a_refs = refs[0:2 * n_p:2]
        b_refs = refs[1:2 * n_p:2]
        e_refs = refs[2 * n_p:n_in]
        c_ins = refs[n_in:n_in + n_ci]
        o_refs = refs[n_in + n_ci:n_in + n_ci + n_o]
        c_outs = refs[n_in + n_ci + n_o:n_in + n_ci + n_o + n_co]
        acc_refs = refs[n_in + n_ci + n_o + n_co:n_in + n_ci + n_o + n_co + n_acc]
        sems = refs[n_in + n_ci + n_o + n_co + n_acc:]

        def dots():
            if sum_pairs and n_p > 1 and dims == NN:
                a_all = jnp.concatenate([a[...] for a in a_refs], axis=1)
                b_all = jnp.concatenate([b[...] for b in b_refs], axis=0)
                return [lax.dot_general(a_all, b_all, (dims, ((), ())), preferred_element_type=F32)]
            prods = [lax.dot_general(a[...], b[...], (dims, ((), ())), preferred_element_type=F32)
                     for a, b in zip(a_refs, b_refs)]
            if sum_pairs and n_p > 1:
                prods = [functools.reduce(operator.add, prods)]
            return prods

        def compute():
            if nk == 1 and split is not None:
                n_split, b_axis, n_row = split
                width = b_refs[0].shape[b_axis] // n_split
                height = a_refs[0].shape[0] // n_row
                for s in range(n_split):
                    cols = pl.ds(s * width, width)
                    for r in range(n_row):
                        rows = pl.ds(r * height, height)
                        epilogue([lax.dot_general(a[rows, :], b[cols, :] if b_axis == 0 else b[:, cols], (dims, ((), ())),
                                                  preferred_element_type=F32) for a, b in zip(a_refs, b_refs)],
                                 e_refs, o_refs, rows, cols)
                return
            if nk == 1:
                epilogue(dots(), e_refs, o_refs)
                return
            k = pl.program_id(kaxis)

            @pl.when(k == 0)
            def _():
                for acc, p in zip(acc_refs, dots()):
                    acc[...] = p

            if nk > 2:
                @pl.when((k > 0) & (k < nk - 1))
                def _():
                    for acc, p in zip(acc_refs, dots()):
                        acc[...] += p

            @pl.when(k == nk - 1)
            def _():
                epilogue([acc[...] + p for acc, p in zip(acc_refs, dots())], e_refs, o_refs)

        hosted.wrap(grid, compute, c_ins, c_outs, sems)

    operands, in_specs = [], []
    for a, a_spec, b, b_spec in pairs:
        operands += [a, b]
        in_specs += [a_spec, b_spec]
    for e, e_spec in extras:
        operands.append(e)
        in_specs.append(e_spec)
    res = pl.pallas_call(
        body, name=name, grid=grid,
        out_shape=tuple([o for o, _ in outs] + hosted.out_shapes),
        in_specs=in_specs + hosted.in_specs, out_specs=tuple([s for _, s in outs] + hosted.out_specs),
        input_output_aliases=hosted.aliases,
        scratch_shapes=[pltpu.VMEM(acc_shape, F32) for _ in range(n_acc)] + hosted.scratch,
        compiler_params=_cparams(len(grid)),
    )(*operands, *hosted.operands)
    return list(res[:n_o]), list(res[n_o:])


NN = ((1,), (0,))
NT = ((1,), (1,))
TN = ((0,), (0,))


def _tile(n, want):
    if n <= want:
        return n
    t = want // 128 * 128
    while n % t:
        t -= 128
    return t


def _silu_parts(g):
    s = 0.5 + 0.5 * jnp.tanh(0.5 * g)
    return s, g * s


def _mm_in(h, w_in, comm=None):
    t, d = h.shape
    n = w_in.shape[1]
    tm, tn = _tile(t, _TM), _tile(n, _TN)

    def epi(accs, e, o):
        o[0][...] = accs[0].astype(BF16)

    outs, couts = _matmul(
        "mm_in", (n // tn, t // tm), 1, None,
        [(h, pl.BlockSpec((tm, d), lambda j, i: (i, 0)), w_in, pl.BlockSpec((d, tn), lambda j, i: (0, j)))],
        NN, [], [(_sds((t, n), BF16), pl.BlockSpec((tm, tn), lambda j, i: (i, j)))], epi, True, None, comm)
    return outs[0], couts


def _mm_out(y, w_out, x, comm=None):
    t, m = y.shape
    d = w_out.shape[1]
    tm, tn = _tile(t, _TM), _tile(d, _TN)

    def epi(accs, e, o):
        o[0][...] = e[0][...] + accs[0]

    outs, couts = _matmul(
        "mm_out", (t // tm, d // tn), 1, None,
        [(y, pl.BlockSpec((tm, m), lambda i, j: (i, 0)), w_out, pl.BlockSpec((m, tn), lambda i, j: (0, j)))],
        NN, [(x, pl.BlockSpec((tm, tn), lambda i, j: (i, j)))],
        [(_sds((t, d), F32), pl.BlockSpec((tm, tn), lambda i, j: (i, j)))], epi, True, None, comm)
    return outs[0], couts


def _mm_swiglu(h2, wgt, wut, comm=None):
    t, d = h2.shape
    f = wgt.shape[0]
    tm, tn = _tile(t, 2 * _TM), _tile(f, 512)

    def epi(accs, e, o, rows, cols):
        g, u = accs
        s, sg = _silu_parts(g)
        o[0][rows, cols] = (sg * u).astype(BF16)
        o[1][rows, cols] = (u * (s + sg * (1.0 - s))).astype(BF16)
        o[2][rows, cols] = sg.astype(BF16)

    wspec = pl.BlockSpec((tn, d), lambda i, j: (j, 0))
    hspec = pl.BlockSpec((tm, d), lambda i, j: (i, 0))
    ospec = pl.BlockSpec((tm, tn), lambda i, j: (i, j))
    osh = _sds((t, f), BF16)
    outs, couts = _matmul("mm_swiglu", (t // tm, f // tn), 1, None, [(h2, hspec, wgt, wspec), (h2, hspec, wut, wspec)],
                          NT, [], [(osh, ospec)] * 3, epi, False, None, comm, split=(tn // 256, 0, 2))
    return outs, couts


def _mm_down(act, wd, x1, comm=None):
    t, f = act.shape
    d = wd.shape[1]
    tm, tn = _tile(t, _TM), _tile(d, _TN)
    nk = 2
    tk = f // nk

    def epi(accs, e, o):
        o[0][...] = e[0][...] + accs[0]

    outs, couts = _matmul(
        "mm_down", (t // tm, d // tn, nk), nk, 2,
        [(act, pl.BlockSpec((tm, tk), lambda i, j, k: (i, k)), wd, pl.BlockSpec((tk, tn), lambda i, j, k: (k, j)))],
        NN, [(x1, pl.BlockSpec((tm, tn), lambda i, j, k: (i, j)))],
        [(_sds((t, d), F32), pl.BlockSpec((tm, tn), lambda i, j, k: (i, j)))], epi, True, (tm, tn), comm)
    return outs[0], couts


def _mm_dact(dxb, wd, dact_dgate, dact_dup, comm=None):
    t, d = dxb.shape
    f = wd.shape[0]
    tm, tn = _tile(t, 2 * _TM), _tile(f, 512)

    def epi(accs, e, o, rows, cols):
        da = accs[0]
        o[0][rows, cols] = (da * e[0][rows, cols].astype(F32)).astype(BF16)
        o[1][rows, cols] = (da * e[1][rows, cols].astype(F32)).astype(BF16)

    bspec = pl.BlockSpec((tm, tn), lambda i, j: (i, j))
    osh = _sds((t, f), BF16)
    outs, couts = _matmul(
        "mm_dact", (t // tm, f // tn), 1, None,
        [(dxb, pl.BlockSpec((tm, d), lambda i, j: (i, 0)), wd, pl.BlockSpec((tn, d), lambda i, j: (j, 0)))],
        NT, [(dact_dgate, bspec), (dact_dup, bspec)], [(osh, bspec)] * 2, epi, True, None, comm, split=(tn // 256, 0, 2))
    return outs, couts


def _mm_dh2(dgate, dup, wgt, wut, comm=None):
    t, f = dgate.shape
    d = wgt.shape[1]
    tm, tn = _tile(t, _TM), _tile(d, _TN)
    nk = 4
    tk = f // nk

    def epi(accs, e, o):
        o[0][...] = accs[0]

    aspec = pl.BlockSpec((tm, tk), lambda i, j, k: (i, k))
    wspec = pl.BlockSpec((tk, tn), lambda i, j, k: (k, j))
    outs, couts = _matmul("mm_dh2", (t // tm, d // tn, nk), nk, 2, [(dgate, aspec, wgt, wspec), (dup, aspec, wut, wspec)],
                          NN, [], [(_sds((t, d), F32), pl.BlockSpec((tm, tn), lambda i, j, k: (i, j)))], epi, True,
                          (tm, tn), comm)
    return outs[0], couts


def _mm_dw(name, a_list, b, tmo, tno, comm=None, m_rows=None):
    t, m = a_list[0].shape
    start, m = (0, m) if m_rows is None else m_rows
    n = b.shape[1]
    tt = _tile(t, _TT)
    nk = t // tt
    tmo, tno = _tile(m, tmo), _tile(n, tno)
    first = start // tmo

    def epi(accs, e, o):
        for acc, out in zip(accs, o):
            out[...] = acc.astype(BF16)

    aspec = pl.BlockSpec((tt, tmo), lambda i, j, k: (k, first + i))
    bspec = pl.BlockSpec((tt, tno), lambda i, j, k: (k, j))
    ospec = pl.BlockSpec((tmo, tno), lambda i, j, k: (i, j))
    if nk == 1:
        return _matmul(name, (m // tmo, n // tno, 1), 1, None, [(a, aspec, b, bspec) for a in a_list], TN, [],
                       [(_sds((m, n), BF16), ospec)] * len(a_list), epi, False, None, comm)
    return _matmul(name, (m // tmo, n // tno, nk), nk, 2, [(a, aspec, b, bspec) for a in a_list], TN, [],
                   [(_sds((m, n), BF16), ospec)] * len(a_list), epi, False, (tmo, tno), comm)


def _mm_dy(dxb, w_out, comm=None):
    t, d = dxb.shape
    m = w_out.shape[0]
    tm, tn = _tile(t, _TM), _tile(m, _TN)

    def epi(accs, e, o):
        o[0][...] = accs[0].astype(BF16)

    outs, couts = _matmul(
        "mm_dy", (t // tm, m // tn), 1, None,
        [(dxb, pl.BlockSpec((tm, d), lambda i, j: (i, 0)), w_out, pl.BlockSpec((tn, d), lambda i, j: (j, 0)))], NT, [],
        [(_sds((t, m), BF16), pl.BlockSpec((tm, tn), lambda i, j: (i, j)))], epi, True, None, comm)
    return outs[0], couts


def _mm_dh(dz, w_in, comm=None):
    t, n = dz.shape
    d = w_in.shape[0]
    tm, tn = _tile(t, _TM), _tile(d, _TN)
    nk = 2
    tk = n // nk

    def epi(accs, e, o):
        o[0][...] = accs[0]

    outs, couts = _matmul(
        "mm_dh", (t // tm, d // tn, nk), nk, 2,
        [(dz, pl.BlockSpec((tm, tk), lambda i, j, k: (i, k)), w_in, pl.BlockSpec((tn, tk), lambda i, j, k: (j, k)))], NT,
        [], [(_sds((t, d), F32), pl.BlockSpec((tm, tn), lambda i, j, k: (i, j)))], epi, True, (tm, tn), comm)
    return outs[0], couts


def _rmsnorm_fwd(x, g, comm=None):
    t, d = x.shape
    tm = min(_TM_NORM, t)

    def body(x_ref, g_ref, o_ref):
        xv = x_ref[...]
        rs = lax.rsqrt(jnp.mean(xv * xv, axis=-1, keepdims=True) + RMS_EPS)
        o_ref[...] = (xv * rs * g_ref[...]).astype(BF16)

    outs, couts = _call_hosting(
        body, "rmsnorm_fwd", (t // tm,), [_sds((t, d), BF16)],
        [pl.BlockSpec((tm, d), lambda i: (i, 0)), pl.BlockSpec((1, d), lambda i: (0, 0))],
        [pl.BlockSpec((tm, d), lambda i: (i, 0))], [x, g], [], comm)
    return outs[0], couts


def _rmsnorm_bwd_math(xv, g, dh):
    rs = lax.rsqrt(jnp.mean(xv * xv, axis=-1, keepdims=True) + RMS_EPS)
    xh = xv * rs
    gd = dh * g
    dx = rs * (gd - xh * jnp.mean(gd * xh, axis=-1, keepdims=True))
    return dx, jnp.sum(dh * xh, axis=0, keepdims=True)


def _rmsnorm_bwd(x, g, dh, dres):
    t, d = x.shape
    tm = min(_TM_NORM, t)

    def body(x_ref, g_ref, dh_ref, dres_ref, dx_ref, dxb_ref, dg_ref):
        dx, dg = _rmsnorm_bwd_math(x_ref[...], g_ref[...], dh_ref[...])
        dx = dx + dres_ref[...]
        dx_ref[...] = dx
        dxb_ref[...] = dx.astype(BF16)

        @pl.when(pl.program_id(0) == 0)
        def _():
            dg_ref[...] = dg

        @pl.when(pl.program_id(0) > 0)
        def _():
            dg_ref[...] += dg

    row = pl.BlockSpec((tm, d), lambda i: (i, 0))
    vec = pl.BlockSpec((1, d), lambda i: (0, 0))
    return pl.pallas_call(
        body, name="rmsnorm_bwd", grid=(t // tm,),
        out_shape=(_sds((t, d), F32), _sds((t, d), BF16), _sds((1, d), F32)),
        in_specs=[row, vec, row, row], out_specs=(row, row, vec), compiler_params=_cparams(1))(x, g, dh, dres)


def _loss_head(x, g, target):
    t, d = x.shape
    tm = min(_TM_NORM, t)

    def body(x_ref, g_ref, t_ref, dx_ref, dxb_ref, dg_ref, loss_ref):
        xv, gv = x_ref[...], g_ref[...]
        rs = lax.rsqrt(jnp.mean(xv * xv, axis=-1, keepdims=True) + RMS_EPS)
        diff = xv * rs * gv - t_ref[...]
        part = 0.5 * jnp.sum(jnp.mean(diff * diff, axis=-1, keepdims=True), axis=0, keepdims=True)
        part = jnp.broadcast_to(part, (1, 128))
        dx, dg = _rmsnorm_bwd_math(xv, gv, diff * (1.0 / d))
        dx_ref[...] = dx
        dxb_ref[...] = dx.astype(BF16)

        @pl.when(pl.program_id(0) == 0)
        def _():
            dg_ref[...] = dg
            loss_ref[...] = part

        @pl.when(pl.program_id(0) > 0)
        def _():
            dg_ref[...] += dg
            loss_ref[...] += part

    row = pl.BlockSpec((tm, d), lambda i: (i, 0))
    vec = pl.BlockSpec((1, d), lambda i: (0, 0))
    return pl.pallas_call(
        body, name="loss_head", grid=(t // tm,),
        out_shape=(_sds((t, d), F32), _sds((t, d), BF16), _sds((1, d), F32), _sds((1, 128), F32)),
        in_specs=[row, vec, row], out_specs=(row, row, vec, pl.BlockSpec((1, 128), lambda i: (0, 0))),
        compiler_params=_cparams(1))(x, g, target)


def _gelu(x):
    th = jnp.tanh(GELU_C * (x + GELU_A * x * x * x))
    return 0.5 * x * (1.0 + th), th


def _gelu_grad(x, th):
    return 0.5 * (1.0 + th) + 0.5 * x * (1.0 - th * th) * GELU_C * (1.0 + 3.0 * GELU_A * x * x)


def _masked_ws(ws_ref, h):
    i = lax.broadcasted_iota(jnp.int32, (BLK, BLK), 0) // CHUNK
    j = lax.broadcasted_iota(jnp.int32, (BLK, BLK), 1) // CHUNK
    return jnp.where(j <= i, ws_ref[h], 0.0)


def _shift_down(q, n, first_rows):
    rolled = pltpu.roll(q, n, 0)
    row = lax.broadcasted_iota(jnp.int32, q.shape, 0)
    for r, val in enumerate(first_rows):
        rolled = jnp.where(row == r, val, rolled)
    return rolled


def _shift_up(q, n, last_rows):
    tm = q.shape[0]
    rolled = pltpu.roll(q, tm - n, 0)
    row = lax.broadcasted_iota(jnp.int32, q.shape, 0)
    for r, val in enumerate(last_rows):
        rolled = jnp.where(row == tm - n + r, val, rolled)
    return rolled


def _mixer_specs(t, a, tm):
    hb = tm // HALO
    last = t // HALO - 1
    tile = pl.BlockSpec((tm, 5 * a), lambda i: (i, 0))
    prev = [pl.BlockSpec((HALO, a), functools.partial(lambda i, col: (jnp.maximum(i * hb - 1, 0), col), col=col))
            for col in (3, 4)]
    nxt = [pl.BlockSpec((HALO, a), functools.partial(lambda i, col: (jnp.minimum((i + 1) * hb, last), col), col=col))
           for col in (2, 3, 4)]
    return tile, prev, nxt


def _group_a_fwd(zu, zv, lng, lnb, ws_ref, bb_ref, mixed_ref, vln_ref):
    u, thu = _gelu(zu)
    v, thv = _gelu(zv)
    mu = jnp.mean(v, axis=-1, keepdims=True)
    vc = v - mu
    rs = lax.rsqrt(jnp.mean(vc * vc, axis=-1, keepdims=True) + LN_EPS)
    vhat = vc * rs
    vln_ref[...] = vhat * lng + lnb
    tm, a = zu.shape
    hd = a // HEADS
    for h in range(HEADS):
        w = _masked_ws(ws_ref, h).astype(BF16)
        for b in range(tm // BLK):
            rows, cols = pl.ds(b * BLK, BLK), pl.ds(h * hd, hd)
            mixed_ref[rows, cols] = jnp.dot(w, vln_ref[rows, cols].astype(BF16), preferred_element_type=F32) + bb_ref[h]
    return u, thu, thv, rs, vhat


def _mixer_fwd(z, ln_g, ln_b, w_spatial, bb, conv_w, gg, comm=None):
    t = z.shape[0]
    a = z.shape[1] // 5
    tm = min(_TM_MIX, t)
    tile, prev, _ = _mixer_specs(t, a, tm)

    def body(z_ref, pc_ref, ph_ref, lng_ref, lnb_ref, ws_ref, bb_ref, cw_ref, gg_ref, y_ref, mixed_ref, vln_ref):
        i = pl.program_id(0)
        zu = z_ref[:, 0:a].astype(F32)
        zv = z_ref[:, a:2 * a].astype(F32)
        u, _, _, _, _ = _group_a_fwd(zu, zv, lng_ref[...], lnb_ref[...], ws_ref, bb_ref, mixed_ref, vln_ref)
        ya = u * mixed_ref[...]
        ra = lax.rsqrt(jnp.mean(ya * ya, axis=-1, keepdims=True) + RMS_EPS)
        y_ref[:, 0:a] = (ya * ra * gg_ref[:, 0:a]).astype(BF16)

        zb = z_ref[:, 2 * a:3 * a].astype(F32)
        q = z_ref[:, 3 * a:4 * a].astype(F32) * z_ref[:, 4 * a:5 * a].astype(F32)
        qp = jnp.where(i > 0, pc_ref[...].astype(F32) * ph_ref[...].astype(F32), 0.0)
        qm1 = _shift_down(q, 1, [qp[HALO - 1:HALO]])
        qm2 = _shift_down(q, 2, [qp[HALO - 2:HALO - 1], qp[HALO - 1:HALO]])
        cv = cw_ref[0:1, :] * qm2 + cw_ref[1:2, :] * qm1 + cw_ref[2:3, :] * q
        yb = zb * cv
        rb = lax.rsqrt(jnp.mean(yb * yb, axis=-1, keepdims=True) + RMS_EPS)
        y_ref[:, a:2 * a] = (yb * rb * gg_ref[:, a:2 * a]).astype(BF16)

    full = lambda shape: pl.BlockSpec(shape, lambda i: (0,) * len(shape))
    outs, couts = _call_hosting(
        body, "mixer_fwd", (t // tm,), [_sds((t, 2 * a), BF16)],
        [tile, *prev, full((1, a)), full((1, a)), full(w_spatial.shape), full(bb.shape), full(conv_w.shape),
         full((1, 2 * a))],
        [pl.BlockSpec((tm, 2 * a), lambda i: (i, 0))], [z, z, z, ln_g, ln_b, w_spatial, bb, conv_w, gg],
        [pltpu.VMEM((tm, a), F32), pltpu.VMEM((tm, a), F32)], comm)
    return outs[0], couts


def _mixer_bwd(z, dy, ln_g, ln_b, w_spatial, bb, conv_w, gg, comm=None):
    t = z.shape[0]
    a = z.shape[1] // 5
    hd = a // HEADS
    tm = min(_TM_MIX, t)
    n_tiles = t // tm
    tile, prev, nxt = _mixer_specs(t, a, tm)
    hb = tm // HALO
    dy_tile = pl.BlockSpec((tm, 2 * a), lambda i: (i, 0))
    dy_next = pl.BlockSpec((HALO, a), lambda i: (jnp.minimum((i + 1) * hb, t // HALO - 1), 1))

    def body(z_ref, pc_ref, ph_ref, nb_ref, nc_ref, nh_ref, dy_ref, ndy_ref, lng_ref, lnb_ref, ws_ref, bb_ref, cw_ref,
             gg_ref, dz_ref, dlng_ref, dlnb_ref, dws_ref, dbb_ref, dcw_ref, dgg_ref, mixed_ref, vln_ref, dmix_ref,
             dvln_ref):
        i = pl.program_id(0)

        @pl.when(i == 0)
        def _():
            for ref in (dlng_ref, dlnb_ref, dws_ref, dbb_ref, dcw_ref, dgg_ref):
                ref[...] = jnp.zeros(ref.shape, F32)

        lng = lng_ref[...]
        zu = z_ref[:, 0:a].astype(F32)
        zv = z_ref[:, a:2 * a].astype(F32)
        u, thu, thv, rs, vhat = _group_a_fwd(zu, zv, lng, lnb_ref[...], ws_ref, bb_ref, mixed_ref, vln_ref)
        mixed = mixed_ref[...]
        ya = u * mixed
        ra = lax.rsqrt(jnp.mean(ya * ya, axis=-1, keepdims=True) + RMS_EPS)
        da = dy_ref[:, 0:a].astype(F32)
        yah = ya * ra
        dgg_ref[:, 0:a] += jnp.sum(da * yah, axis=0, keepdims=True)
        ga = da * gg_ref[:, 0:a]
        dya = ra * (ga - yah * jnp.mean(ga * yah, axis=-1, keepdims=True))
        dz_ref[:, 0:a] = (dya * mixed * _gelu_grad(zu, thu)).astype(BF16)
        dmix_ref[...] = dya * u
        for h in range(HEADS):
            w = _masked_ws(ws_ref, h).astype(BF16)
            dw = jnp.zeros((BLK, BLK), F32)
            db = jnp.zeros((BLK, hd), F32)
            for b in range(tm // BLK):
                rows, cols = pl.ds(b * BLK, BLK), pl.ds(h * hd, hd)
                dm = dmix_ref[rows, cols]
                dmb = dm.astype(BF16)
                db = db + dm
                dw = dw + lax.dot_general(dmb, vln_ref[rows, cols].astype(BF16), (NT, ((), ())),
                                          preferred_element_type=F32)
                dvln_ref[rows, cols] = lax.dot_general(w, dmb, (TN, ((), ())), preferred_element_type=F32)
            dws_ref[h] += dw
            dbb_ref[h] += db
        dvln = dvln_ref[...]
        dlng_ref[...] += jnp.sum(dvln * vhat, axis=0, keepdims=True)
        dlnb_ref[...] += jnp.sum(dvln, axis=0, keepdims=True)
        dvh = dvln * lng
        dv = rs * (dvh - jnp.mean(dvh, axis=-1, keepdims=True) - vhat * jnp.mean(dvh * vhat, axis=-1, keepdims=True))
        dz_ref[:, a:2 * a] = (dv * _gelu_grad(zv, thv)).astype(BF16)

        w0, w1, w2 = cw_ref[0:1, :], cw_ref[1:2, :], cw_ref[2:3, :]
        ggb = gg_ref[:, a:2 * a]
        zb = z_ref[:, 2 * a:3 * a].astype(F32)
        zc = z_ref[:, 3 * a:4 * a].astype(F32)
        zh = z_ref[:, 4 * a:5 * a].astype(F32)
        q = zc * zh
        qp = jnp.where(i > 0, pc_ref[...].astype(F32) * ph_ref[...].astype(F32), 0.0)
        qm1 = _shift_down(q, 1, [qp[HALO - 1:HALO]])
        qm2 = _shift_down(q, 2, [qp[HALO - 2:HALO - 1], qp[HALO - 1:HALO]])
        cv = w0 * qm2 + w1 * qm1 + w2 * q

        def conv_out_grad(zb_, cv_, dout_):
            yb = zb_ * cv_
            rb = lax.rsqrt(jnp.mean(yb * yb, axis=-1, keepdims=True) + RMS_EPS)
            ybh = yb * rb
            gb = dout_ * ggb
            dyb = rb * (gb - ybh * jnp.mean(gb * ybh, axis=-1, keepdims=True))
            return dyb * zb_, dyb * cv_, ybh

        db_out = dy_ref[:, a:2 * a].astype(F32)
        g, dzb, ybh = conv_out_grad(zb, cv, db_out)
        dgg_ref[:, a:2 * a] += jnp.sum(db_out * ybh, axis=0, keepdims=True)
        dz_ref[:, 2 * a:3 * a] = dzb.astype(BF16)
        qn = nc_ref[...].astype(F32) * nh_ref[...].astype(F32)
        zbn = nb_ref[...].astype(F32)
        cvn = w0 * _shift_down(qn, 2, [q[tm - 2:tm - 1], q[tm - 1:tm]]) + w1 * _shift_down(qn, 1, [q[tm - 1:tm]]) + w2 * qn
        gn, _, _ = conv_out_grad(zbn, cvn, ndy_ref[...].astype(F32))
        gn = jnp.where(i < n_tiles - 1, gn, 0.0)
        dq = w2 * g + w1 * _shift_up(g, 1, [gn[0:1]]) + w0 * _shift_up(g, 2, [gn[0:1], gn[1:2]])
        dz_ref[:, 3 * a:4 * a] = (dq * zh).astype(BF16)
        dz_ref[:, 4 * a:5 * a] = (dq * zc).astype(BF16)
        dcw_ref[0:1, :] += jnp.sum(g * qm2, axis=0, keepdims=True)
        dcw_ref[1:2, :] += jnp.sum(g * qm1, axis=0, keepdims=True)
        dcw_ref[2:3, :] += jnp.sum(g * q, axis=0, keepdims=True)

        @pl.when(i == n_tiles - 1)
        def _():
            for h in range(HEADS):
                dbb_ref[h] = jnp.broadcast_to(jnp.sum(dbb_ref[h], axis=1, keepdims=True), (BLK, hd))
                dws_ref[h] = _masked_ws(dws_ref, h)

    full = lambda shape: pl.BlockSpec(tuple(shape), lambda i: (0,) * len(shape))
    out_shapes = (_sds((t, 5 * a), BF16), _sds((1, a), F32), _sds((1, a), F32), _sds(w_spatial.shape, F32),
                  _sds(bb.shape, F32), _sds((8, a), F32), _sds((1, 2 * a), F32))
    return _call_hosting(
        body, "mixer_bwd", (n_tiles,), out_shapes,
        [tile, *prev, *nxt, dy_tile, dy_next, full((1, a)), full((1, a)), full(w_spatial.shape), full(bb.shape),
         full(conv_w.shape), full((1, 2 * a))],
        [tile, *[full(s.shape) for s in out_shapes[1:]]], [z, z, z, z, z, z, dy, dy, ln_g, ln_b, w_spatial, bb, conv_w, gg],
        [pltpu.VMEM((tm, a), F32)] * 4, comm)


def _all_reduce_small(pack, comm=None):
    r = pack.shape[0]
    hosted = _Hosted(comm, 1, 1)
    n_ci, n_co = len(hosted.operands), len(hosted.out_shapes)

    def body(*refs):
        in_ref, c_ins, out_ref, c_outs = refs[0], refs[1:1 + n_ci], refs[1 + n_ci], refs[2 + n_ci:2 + n_ci + n_co]
        acc_ref, recv_ref, send_sems, recv_sems = refs[2 + n_ci + n_co:6 + n_ci + n_co]
        sems = refs[6 + n_ci + n_co:]
        hosted.run("start", c_ins, c_outs, sems)
        x, y, c = _place()
        partners = [(x, y, 1 - c), (1 - x, y, c), (x, 1 - y, c)]
        acc_ref[0] = in_ref[...]
        for s, partner in enumerate(partners):
            cp = pltpu.make_async_remote_copy(
                src_ref=acc_ref.at[s], dst_ref=recv_ref.at[s], send_sem=send_sems.at[s], recv_sem=recv_sems.at[s],
                device_id=partner, device_id_type=MESH)
            cp.start()
            cp.wait()
            if s < 2:
                acc_ref[s + 1] = acc_ref[s] + recv_ref[s]
            else:
                out_ref[...] = acc_ref[s] + recv_ref[s]
        for stage in ("mid1", "mid2", "finish"):
            hosted.run(stage, c_ins, c_outs, sems)

    vmem = pl.BlockSpec(memory_space=pltpu.VMEM)
    res = pl.pallas_call(
        body, name="all_reduce_small", out_shape=tuple([_sds(pack.shape, F32)] + hosted.out_shapes),
        in_specs=[vmem] + hosted.in_specs, out_specs=tuple([vmem] + hosted.out_specs),
        input_output_aliases=hosted.aliases,
        scratch_shapes=[pltpu.VMEM((3, r, 128), F32), pltpu.VMEM((3, r, 128), F32), pltpu.SemaphoreType.DMA((3,)),
                        pltpu.SemaphoreType.DMA((3,))] + hosted.scratch,
        compiler_params=pltpu.CompilerParams(vmem_limit_bytes=VMEM_LIMIT_V7X),
    )(pack, *hosted.operands)
    return res[0], list(res[1:])


def _adamw_math(w, g, m, v):
    m = ADAM_B1 * m + (1.0 - ADAM_B1) * g
    v = ADAM_B2 * v + (1.0 - ADAM_B2) * (g * g)
    m_hat = m / (1.0 - ADAM_B1 ** ADAM_STEP)
    v_hat = v / (1.0 - ADAM_B2 ** ADAM_STEP)
    delta = -ADAM_LR * (m_hat / (jnp.sqrt(v_hat) + ADAM_EPS) + ADAM_WD * w)
    return delta, m, v


def _adamw_big(name, land, w, m, v, comm=None):
    nl, n_slots, r, c = land.shape
    tr = max(8, min(r, (256 * 640) // c // 8 * 8))
    while r % tr:
        tr -= 8
    grid = (nl, r // tr)
    hosted = _Hosted(comm, 4, 4)
    n_ci, n_co = len(hosted.operands), len(hosted.out_shapes)

    def body(*refs):
        land_ref, w_ref, m_ref, v_ref = refs[:4]
        c_ins = refs[4:4 + n_ci]
        g_out, d_out, m_out, v_out = refs[4 + n_ci:8 + n_ci]
        c_outs = refs[8 + n_ci:8 + n_ci + n_co]
        sems = refs[8 + n_ci + n_co:]

        def compute():
            g = land_ref[0].astype(F32)
            for s in range(1, n_slots):
                g = g + land_ref[s].astype(F32)
            delta, mn, vn = _adamw_math(w_ref[...], g, m_ref[...], v_ref[...])
            g_out[...] = g
            d_out[...] = delta
            m_out[...] = mn
            v_out[...] = vn

        hosted.wrap(grid, compute, c_ins, c_outs, sems)

    blk = pl.BlockSpec((None, tr, c), lambda l, i: (l, i, 0))
    res = pl.pallas_call(
        body, name=name, grid=grid, out_shape=tuple([_sds((nl, r, c), F32)] * 4 + hosted.out_shapes),
        in_specs=[pl.BlockSpec((None, n_slots, tr, c), lambda l, i: (l, 0, i, 0)), blk, blk, blk] + hosted.in_specs,
        out_specs=tuple([blk] * 4 + hosted.out_specs), input_output_aliases=hosted.aliases,
        scratch_shapes=hosted.scratch, compiler_params=_cparams(2))(land, w, m, v, *hosted.operands)
    return list(res[:4]), list(res[4:])


def _adamw_small(g, w, m, v):
    def body(g_ref, w_ref, m_ref, v_ref, d_out, m_out, v_out):
        delta, mn, vn = _adamw_math(w_ref[...], g_ref[...], m_ref[...], v_ref[...])
        d_out[...] = delta
        m_out[...] = mn
        v_out[...] = vn

    return pl.pallas_call(body, name="adamw_small", out_shape=tuple([_sds(g.shape, F32)] * 3),
                          compiler_params=pltpu.CompilerParams(vmem_limit_bytes=VMEM_LIMIT_V7X))(g, w, m, v)


def _rows(a):
    return a.reshape(-1, 128)


BIG = ["w_in", "w_out", "w_gate", "w_up", "w_down"]
AG_HOSTS = {
    ("norm1", 0): [("w_in", 0), ("conv_w", 0)],
    ("mm_in", 0): [("w_out", 0), ("w_gate", 0, 0, 2)], ("mixer", 0): [("w_gate", 0, 1, 2)],
    ("mm_out", 0): [("w_up", 0, 0, 2)], ("norm2", 0): [("w_up", 0, 1, 2)],
    ("mm_swiglu", 0): [("w_down", 0), ("w_in", 1)], ("mm_down", 0): [("w_out", 1), ("w_gate", 1, 0, 2)],
    ("mm_in", 1): [("w_gate", 1, 1, 2)], ("mixer", 1): [("w_up", 1, 0, 2)], ("mm_out", 1): [("w_up", 1, 1, 2)],
    ("mm_swiglu", 1): [("w_down", 1)],
}


def kernel(x, norm1_g, w_in, gmlp_ln_g, gmlp_ln_b, w_spatial, b_spatial, conv_w, group_norm_g, w_out, norm2_g, w_gate, w_up, w_down, final_norm_g, loss_target, m_norm1_g, m_w_in, m_gmlp_ln_g, m_gmlp_ln_b, m_w_spatial, m_b_spatial, m_conv_w, m_group_norm_g, m_w_out, m_norm2_g, m_w_gate, m_w_up, m_w_down, m_final_norm_g, v_norm1_g, v_w_in, v_gmlp_ln_g, v_gmlp_ln_b, v_w_spatial, v_b_spatial, v_conv_w, v_group_norm_g, v_w_out, v_norm2_g, v_w_gate, v_w_up, v_w_down, v_final_norm_g):
    nl = N_LAYERS
    t, d = x.shape[1], x.shape[2]
    a = d // 2
    hd = a // HEADS
    xin = x.reshape(t, d)
    target = loss_target.reshape(t, d)
    me = _index(_place())

    tr = lambda w: jnp.transpose(w, (0, 2, 1))
    big = {"w_in": w_in, "w_out": w_out, "w_gate": tr(w_gate), "w_up": tr(w_up), "w_down": w_down}
    big_m = {"w_in": m_w_in, "w_out": m_w_out, "w_gate": tr(m_w_gate), "w_up": tr(m_w_up), "w_down": m_w_down}
    big_v = {"w_in": v_w_in, "w_out": v_w_out, "w_gate": tr(v_w_gate), "w_up": tr(v_w_up), "w_down": v_w_down}
    block = {k: big[k].shape[1:] for k in BIG}
    view = {k: _cols_view(block[k][1]) if k == "w_in" else _rows_view(block[k][0]) for k in BIG}
    full_shape = {k: (block[k][0], N_DEV * block[k][1]) if k == "w_in" else (N_DEV * block[k][0], block[k][1])
                  for k in BIG}

    weights = {}
    shards = {(k, l): big[k][l].astype(BF16) for k in BIG for l in range(nl)}

    def ag_spec(k, l, part=0, n_parts=1):
        if k == "conv_w":
            return (conv_w, _sds((N_DEV, *conv_w.shape), F32), _SLOT_WHOLE, (0,), None)
        halves = (_cols_halves(*block[k], part, n_parts) if k == "w_in" else _rows_halves(block[k][0], part, n_parts))
        return (shards[(k, l)], _sds(full_shape[k], BF16), halves, (0, 1), weights.get((k, l)))

    bb = jnp.broadcast_to(b_spatial[..., None], (nl, HEADS, BLK, hd))

    def hosted(name, l):
        keys = AG_HOSTS.get((name, l), [])
        return keys, ([_ag_piece([ag_spec(*key) for key in keys])] if keys else None)

    def landed(keys, couts):
        for key, arr in zip(keys, couts):
            weights[key[:2]] = arr

    saved = []
    xl = xin
    for l in range(nl):
        keys, comm = hosted("norm1", l)
        h, couts = _rmsnorm_fwd(xl, norm1_g[l:l + 1], comm)
        landed(keys, couts)
        if l == 0:
            conv_full = jnp.transpose(weights[("conv_w", 0)], (1, 2, 0, 3)).reshape(nl, 3, a)
        keys, comm = hosted("mm_in", l)
        z, couts = _mm_in(h, weights[("w_in", l)], comm)
        landed(keys, couts)
        keys, comm = hosted("mixer", l)
        y, couts = _mixer_fwd(z, gmlp_ln_g[l:l + 1], gmlp_ln_b[l:l + 1], w_spatial[l], bb[l], conv_full[l],
                              group_norm_g[l:l + 1], comm)
        landed(keys, couts)
        keys, comm = hosted("mm_out", l)
        x1, couts = _mm_out(y, weights[("w_out", l)], xl, comm)
        landed(keys, couts)
        keys, comm = hosted("norm2", l)
        h2, couts = _rmsnorm_fwd(x1, norm2_g[l:l + 1], comm)
        landed(keys, couts)
        keys, comm = hosted("mm_swiglu", l)
        (act, dact_dgate, dact_dup), couts = _mm_swiglu(h2, weights[("w_gate", l)], weights[("w_up", l)], comm)
        landed(keys, couts)
        keys, comm = hosted("mm_down", l)
        x2, couts = _mm_down(act, weights[("w_down", l)], x1, comm)
        landed(keys, couts)
        saved.append(dict(x=xl, h=h, z=z, y=y, x1=x1, h2=h2, dact_dgate=dact_dgate, dact_dup=dact_dup, act=act))
        xl = x2

    dx, dxb, d_final_g, loss_part = _loss_head(xl, final_norm_g.reshape(1, d), target)
    small = [None] * nl
    core = lax.axis_index("c").astype(jnp.int32).reshape(1)
    in_rows = block["w_in"][0]
    part_of = {"w_in_a": ("w_in", 0), "w_in_b": ("w_in", 3 * in_rows // 4)}
    block["w_in_a"], block["w_in_b"] = (3 * in_rows // 4, block["w_in"][1]), (in_rows // 4, block["w_in"][1])
    for k in part_of:
        view[k] = view["w_in"]
    stage_shape = {k: _sds((N_CHIPS, *block[k]), BF16) for k in block}
    land_shape = {k: _sds((nl, N_CHIPS, *block[k]), BF16) for k in BIG}
    grads = [dict() for _ in range(nl)]
    stages = [dict() for _ in range(nl)]
    sums = [dict() for _ in range(nl)]
    lands = {k: None for k in BIG}

    def core_job(l, keys):
        def sink(outs):
            stages[l].update(zip(keys, outs))
        return _rs_core_piece([(grads[l][k], stage_shape[k], view[k]) for k in keys]), sink

    def chip_job(l, items):
        keys = [part_of.get(item[0], (item[0], 0))[0] for item in items]

        def rows(k, p0, p1, n_parts):
            per = block[k][0] // n_parts
            landing = part_of.get(k, (k, 0))[1]
            return (p0 * per, landing + p0 * per, (p1 - p0) * per)

        def sink(outs):
            lands.update(zip(keys, outs))
        return _rs_chip_piece([(sums[l][k], land_shape[key], rows(k, p0, p1, n_parts), lands[key])
                               for key, (k, p0, p1, n_parts) in zip(keys, items)], l), sink

    def add_up(l, keys):
        for k in keys:
            sums[l][k] = _chip_sums(f"chip_sums_{k}", grads[l][k], stages[l][k], k.startswith("w_in"), core)

    def host(*jobs):
        def deliver(couts):
            i = 0
            for piece, sink in jobs:
                n_out = len(piece.out_shapes)
                sink(couts[i:i + n_out])
                i += n_out
        return [piece for piece, _ in jobs], deliver

    whole = lambda k: (k, 0, 1, 1)
    rep = ["norm1_g", "gmlp_ln_g", "gmlp_ln_b", "w_spatial", "b_spatial", "group_norm_g", "norm2_g"]
    rep_w = dict(norm1_g=norm1_g, gmlp_ln_g=gmlp_ln_g, gmlp_ln_b=gmlp_ln_b, w_spatial=w_spatial, b_spatial=b_spatial,
                 group_norm_g=group_norm_g, norm2_g=norm2_g)
    rep_m = dict(norm1_g=m_norm1_g, gmlp_ln_g=m_gmlp_ln_g, gmlp_ln_b=m_gmlp_ln_b, w_spatial=m_w_spatial,
                 b_spatial=m_b_spatial, group_norm_g=m_group_norm_g, norm2_g=m_norm2_g)
    rep_v = dict(norm1_g=v_norm1_g, gmlp_ln_g=v_gmlp_ln_g, gmlp_ln_b=v_gmlp_ln_b, w_spatial=v_w_spatial,
                 b_spatial=v_b_spatial, group_norm_g=v_group_norm_g, norm2_g=v_norm2_g)

    def small_grad_parts():
        parts = [_rows(jnp.stack([small[l][k].reshape(rep_w[k].shape[1:]) for l in range(nl)])) for k in rep]
        parts.append(_rows(d_final_g))
        parts.append(_rows(jnp.stack([small[l]["conv_w"] for l in range(nl)])))
        parts.append(jnp.broadcast_to(loss_part, (8, 128)))
        rows = sum(p.shape[0] for p in parts)
        parts.append(jnp.zeros((-rows % 16, 128), F32))
        return parts

    for l in reversed(range(nl)):
        s = saved[l]
        wi, wo, wgt, wut, wd = [weights[(k, l)] for k in BIG]
        later = l + 1 < nl
        comm, deliver = host(chip_job(l + 1, [("w_in", 0, 1, 2)])) if later else host()
        (grads[l]["w_down"],), couts = _mm_dw("mm_dw_down", [s["act"]], dxb, 2816, 1024, comm)
        deliver(couts)
        comm, deliver = (host(core_job(l, ["w_down"]), chip_job(l + 1, [("w_in", 1, 2, 2)])) if later
                         else host(core_job(l, ["w_down"])))
        (dgate, dup), couts = _mm_dact(dxb, wd, s["dact_dgate"], s["dact_dup"], comm)
        deliver(couts)
        add_up(l, ["w_down"])
        comm, deliver = host(chip_job(l, [("w_down", 0, 3, 4)]))
        (grads[l]["w_gate"],), couts = _mm_dw("mm_dw_gate", [dgate], s["h2"], 2816, 1024, comm)
        deliver(couts)
        comm, deliver = host(chip_job(l, [("w_down", 3, 4, 4)]), core_job(l, ["w_gate"]))
        (grads[l]["w_up"],), couts = _mm_dw("mm_dw_up", [dup], s["h2"], 2816, 1024, comm)
        deliver(couts)
        add_up(l, ["w_gate"])
        comm, deliver = host(chip_job(l, [whole("w_gate")]), core_job(l, ["w_up"]))
        dh2, couts = _mm_dh2(dgate, dup, wgt, wut, comm)
        deliver(couts)
        add_up(l, ["w_up"])
        dx1, dx1b, d_n2 = _rmsnorm_bwd(s["x1"], norm2_g[l:l + 1], dh2, dx)
        comm, deliver = host(chip_job(l, [("w_up", 0, 1, 4)]))
        dy, couts = _mm_dy(dx1b, wo, comm)
        deliver(couts)
        comm, deliver = host(chip_job(l, [("w_up", 1, 2, 4)]))
        (grads[l]["w_out"],), couts = _mm_dw("mm_dw_out", [s["y"]], dx1b, 1024, 1024, comm)
        deliver(couts)
        comm, deliver = host(chip_job(l, [("w_up", 2, 4, 4)]), core_job(l, ["w_out"]))
        (dz, d_lng, d_lnb, d_ws, d_bb, d_cw, d_gg), couts = _mixer_bwd(
            s["z"], dy, gmlp_ln_g[l:l + 1], gmlp_ln_b[l:l + 1], w_spatial[l], bb[l], conv_full[l], group_norm_g[l:l + 1],
            comm)
        deliver(couts)
        add_up(l, ["w_out"])
        small[l] = dict(norm1_g=jnp.zeros((1, d), F32), gmlp_ln_g=d_lng, gmlp_ln_b=d_lnb, w_spatial=d_ws,
                        b_spatial=d_bb[:, :, 0], group_norm_g=d_gg, norm2_g=d_n2, conv_w=d_cw[0:3])
        if l > 0:
            comm, deliver = host(chip_job(l, [whole("w_out")]))
            (grads[l]["w_in"],), couts = _mm_dw("mm_dw_in", [s["h"]], dz, 2048, 1024, comm)
            deliver(couts)
            comm, deliver = host(core_job(l, ["w_in"]))
            dh, couts = _mm_dh(dz, wi, comm)
            deliver(couts)
            add_up(l, ["w_in"])
        else:
            comm, deliver = host(chip_job(l, [whole("w_out")]))
            (grads[l]["w_in_a"],), couts = _mm_dw("mm_dw_in_a", [s["h"]], dz, block["w_in_a"][0], 1024, comm,
                                                  m_rows=(0, block["w_in_a"][0]))
            deliver(couts)
            comm, deliver = host(core_job(l, ["w_in_a"]))
            (grads[l]["w_in_b"],), couts = _mm_dw("mm_dw_in_b", [s["h"]], dz, block["w_in_b"][0], 2560, comm,
                                                  m_rows=(block["w_in_a"][0], block["w_in_b"][0]))
            deliver(couts)
            add_up(l, ["w_in_a"])
            parts = small_grad_parts()
            reduced = []
            comm, deliver = host(chip_job(l, [whole("w_in_a")]), core_job(l, ["w_in_b"]),
                                 (_all_reduce_piece(jnp.concatenate(parts, axis=0)), reduced.extend))
            dh, couts = _mm_dh(dz, wi, comm)
            deliver(couts)
            add_up(l, ["w_in_b"])
        dx, dxb, small[l]["norm1_g"] = _rmsnorm_bwd(s["x"], norm1_g[l:l + 1], dh, dx1)
    grad_x = dx.reshape(x.shape)

    sizes = [p.shape[0] for p in parts]
    comm, deliver = host(chip_job(0, [whole("w_in_b")]))
    last, couts = _all_reduce_small(_rows(small[0]["norm1_g"]), comm)
    deliver(couts)
    total = lax.dynamic_update_slice(reduced[0], last, (0, 0))
    offs = [0]
    for n in sizes:
        offs.append(offs[-1] + n)
    pieces = [total[offs[i]:offs[i + 1]] for i in range(len(parts))]
    loss = pieces[len(rep) + 2][0, 0]
    conv_g_full = pieces[len(rep) + 1].reshape(nl, 3, N_DEV, a // N_DEV)
    conv_g = lax.dynamic_index_in_dim(conv_g_full, me, axis=2, keepdims=False)
    n_rep = offs[len(rep) + 1]
    pad = jnp.zeros((2, 128), F32)

    def small_pack(named, final, conv):
        return jnp.concatenate([_rows(named[k]) for k in rep] + [_rows(final), _rows(conv), pad], axis=0)

    g_small = jnp.concatenate([total[:n_rep], _rows(conv_g), pad], axis=0)
    d_small, m_small, v_small = _adamw_small(
        g_small, small_pack(rep_w, final_norm_g, conv_w), small_pack(rep_m, m_final_norm_g, m_conv_w),
        small_pack(rep_v, v_final_norm_g, v_conv_w))

    def unpack(packed):
        out = {k: packed[offs[i]:offs[i + 1]].reshape(rep_w[k].shape) for i, k in enumerate(rep)}
        out["final_norm_g"] = packed[offs[len(rep)]:n_rep].reshape(final_norm_g.shape)
        out["conv_w"] = packed[n_rep:n_rep + 6].reshape(conv_w.shape)
        return out

    res = {"grad": unpack(g_small), "delta": unpack(d_small), "m": unpack(m_small), "v": unpack(v_small)}

    for k in BIG:
        outs, _ = _adamw_big(f"adamw_{k}", lands[k], big[k], big_m[k], big_v[k])
        if k in ("w_gate", "w_up"):
            outs = [tr(o) for o in outs]
        res["grad"][k], res["delta"][k], res["m"][k], res["v"][k] = outs

    order = ["norm1_g", "w_in", "gmlp_ln_g", "gmlp_ln_b", "w_spatial", "b_spatial", "conv_w", "group_norm_g", "w_out",
             "norm2_g", "w_gate", "w_up", "w_down", "final_norm_g"]
    return (loss, grad_x, *[res["grad"][k] for k in order], *[res["delta"][k] for k in order],
            *[res["m"][k] for k in order], *[res["v"][k] for k in order])
```

```python
import functools
import math
import operator

import jax
import jax.numpy as jnp
from jax import lax
from jax.experimental import pallas as pl
from jax.experimental.pallas import tpu as pltpu

F32 = jnp.float32
BF16 = jnp.bfloat16
MESH = pl.DeviceIdType.MESH

N_DEV = 8
N_LAYERS = 2
HEADS = 8
BLK = 128
CHUNK = 64
HALO = 16
RMS_EPS = 1e-6
LN_EPS = 1e-5
ADAM_LR, ADAM_B1, ADAM_B2, ADAM_EPS, ADAM_WD, ADAM_STEP = 0.001, 0.9, 0.999, 1e-8, 0.01, 10
GELU_C = math.sqrt(2.0 / math.pi)
GELU_A = 0.044715

VMEM_LIMIT_V7X = 56 * 1024 * 1024
_TM = 1024
_TN = 1024
_TT = 1024
_TM_MIX = 256
_TM_NORM = 512


def _cparams(n_axes):
    return pltpu.CompilerParams(dimension_semantics=("arbitrary",) * n_axes, vmem_limit_bytes=VMEM_LIMIT_V7X)


def _sds(shape, dtype):
    return jax.ShapeDtypeStruct(tuple(shape), dtype)


def _place():
    return lax.axis_index("x"), lax.axis_index("y"), lax.axis_index("c")


def _index(place):
    return 4 * place[0] + 2 * place[1] + place[2]


class _Piece:
    def __init__(self, operands, out_shapes, aliases, n_sems, start, finish, mid1=None, mid2=None, vmem=(),
                 hooks=(0.6, 0.87)):
        self.operands, self.out_shapes, self.aliases, self.n_sems = list(operands), list(out_shapes), dict(aliases), n_sems
        self.vmem = list(vmem)
        self.hooks = hooks
        nothing = lambda ctx: None
        self.start, self.mid1, self.mid2, self.finish = start, mid1 or nothing, mid2 or nothing, finish


class _Ctx:
    def __init__(self, ins, outs, sems, offs):
        self.ins, self.outs, self.sems = ins, outs, sems
        self.o_in, self.o_out, self.o_send, self.o_recv, self.o_loc, self.o_vmem = offs

    def vmem(self, i):
        return self.sems[3 + self.o_vmem + i]

    def inp(self, i):
        return self.ins[self.o_in + i]

    def out(self, i):
        return self.outs[self.o_out + i]

    def send(self, k):
        return self.sems[0].at[self.o_send + k]

    def recv(self, k):
        return self.sems[1].at[self.o_recv + k]

    def local(self, k):
        return self.sems[2].at[self.o_loc + k]


class _Hosted:
    def __init__(self, pieces, n_in_before, n_out_before):
        self.pieces = [p for p in (pieces or []) if p is not None]
        self.operands, self.out_shapes, self.aliases, self.offs = [], [], {}, []
        counts, vmem = [0, 0, 0], []
        for p in self.pieces:
            self.offs.append((len(self.operands), len(self.out_shapes), *counts, len(vmem)))
            for i, j in p.aliases.items():
                self.aliases[n_in_before + len(self.operands) + i] = n_out_before + len(self.out_shapes) + j
            self.operands += p.operands
            self.out_shapes += p.out_shapes
            counts = [c + n for c, n in zip(counts, p.n_sems)]
            vmem += p.vmem
        hbm = pl.BlockSpec(memory_space=pl.ANY)
        self.in_specs = [hbm] * len(self.operands)
        self.out_specs = [hbm] * len(self.out_shapes)
        self.scratch = ([pltpu.SemaphoreType.DMA((max(c, 1),)) for c in counts] + vmem) if self.pieces else []

    def run(self, stage, ins, outs, sems):
        for p, offs in zip(self.pieces, self.offs):
            getattr(p, stage)(_Ctx(ins, outs, sems, offs))

    def wrap(self, grid, compute, ins, outs, sems):
        if not self.pieces:
            compute()
            return
        n_steps = math.prod(grid)
        lin = 0
        for ax, g in enumerate(grid):
            lin = lin * g + pl.program_id(ax)
        pl.when(lin == 0)(lambda: self.run("start", ins, outs, sems))
        compute()
        for stage, which in (("mid1", 0), ("mid2", 1)):
            for p, offs in zip(self.pieces, self.offs):
                at = min(n_steps - 1, int(p.hooks[which] * n_steps))
                pl.when(lin == at)(functools.partial(getattr(p, stage), _Ctx(ins, outs, sems, offs)))
        pl.when(lin == n_steps - 1)(lambda: self.run("finish", ins, outs, sems))


def _cols_view(width):
    return lambda ref, p: ref.at[:, pl.ds(pl.multiple_of(p * width, 128), width)]


def _rows_view(height):
    return lambda ref, p: ref.at[pl.ds(pl.multiple_of(p * height, 16), height), :]


def _cols_halves(rows, width, part, n_parts):
    hr = rows // n_parts // 2
    at = lambda h: pl.ds(part * 2 * hr + h * hr, hr)
    return (lambda ref, p, h: ref.at[at(h), pl.ds(pl.multiple_of(p * width, 128), width)],
            lambda ref, h: ref.at[at(h), :], 2)


def _rows_halves(height, part, n_parts):
    hh = height // n_parts // 2
    return (lambda ref, p, h: ref.at[pl.ds(pl.multiple_of(p * height + part * 2 * hh + h * hh, 16), hh), :],
            lambda ref, h: ref.at[pl.ds(part * 2 * hh + h * hh, hh), :], 2)


_SLOT_WHOLE = (lambda ref, p, h: ref.at[p], lambda ref, h: ref, 1)


def _ag_piece(specs):
    units = [(a, h) for a, s in enumerate(specs) for h in s[3]]

    def plan(ctx):
        x, y, c = _place()
        me, sib, xn, yn, dg = (x, y, c), (x, y, 1 - c), (1 - x, y, c), (x, 1 - y, c), (1 - x, 1 - y, c)

        def copy(u, k, block, to, from_shard=False):
            a, h = units[u]
            dst_of, src_of, _ = specs[a][2]
            dst = dst_of(ctx.out(a), _index(block), h)
            return pltpu.make_async_remote_copy(
                src_ref=src_of(ctx.inp(a), h) if from_shard else dst, dst_ref=dst, send_sem=ctx.send(7 * u + k),
                recv_sem=ctx.recv(7 * u + k), device_id=to, device_id_type=MESH)

        def local(u):
            a, h = units[u]
            dst_of, src_of, _ = specs[a][2]
            return pltpu.make_async_copy(src_of(ctx.inp(a), h), dst_of(ctx.out(a), _index(me), h), ctx.local(u))

        def relay(u):
            return copy(u, 3, xn, yn) if units[u][1] % 2 == 0 else copy(u, 3, yn, xn)

        return me, sib, xn, yn, dg, c, copy, local, relay

    def start(ctx):
        me, sib, xn, yn, dg, c, copy, local, relay = plan(ctx)
        for u in range(len(units)):
            local(u).start()
            for k, to in enumerate((sib, xn, yn)):
                copy(u, k, me, to, from_shard=True).start()

    def mid1(ctx):
        me, sib, xn, yn, dg, c, copy, local, relay = plan(ctx)
        for u in range(len(units)):
            copy(u, 1, xn, me).wait_recv()
            copy(u, 2, yn, me).wait_recv()
            relay(u).start()
            copy(u, 4, xn, sib).start()
            copy(u, 5, yn, sib).start()

    def mid2(ctx):
        me, sib, xn, yn, dg, c, copy, local, relay = plan(ctx)
        for u in range(len(units)):
            copy(u, 3, dg, me).wait_recv()
            copy(u, 6, dg, sib).start()

    def finish(ctx):
        me, sib, xn, yn, dg, c, copy, local, relay = plan(ctx)
        other = lambda place: (place[0], place[1], 1 - c)
        for u in range(len(units)):
            for k, block in ((0, sib), (4, other(xn)), (5, other(yn)), (6, other(dg))):
                copy(u, k, block, me).wait_recv()
        for u in range(len(units)):
            for k, to in enumerate((sib, xn, yn)):
                copy(u, k, me, to, from_shard=True).wait_send()
            relay(u).wait_send()
            for k, block in ((4, xn), (5, yn), (6, dg)):
                copy(u, k, block, sib).wait_send()
            local(u).wait()

    n_u = len(units)
    operands, aliases = [s[0] for s in specs], {}
    for a, spec in enumerate(specs):
        if spec[4] is not None:
            aliases[len(operands)] = a
            operands.append(spec[4])
    return _Piece(operands, [s[1] for s in specs], aliases, (7 * n_u, 7 * n_u, n_u), start, finish, mid1, mid2)


N_CHIPS = 4


def _rs_core_piece(specs):
    n = len(specs)

    def copies(ctx):
        x, y, c = _place()
        out = []
        for a in range(n):
            for q in range(N_CHIPS):
                out.append(pltpu.make_async_remote_copy(
                    src_ref=specs[a][2](ctx.inp(a), 2 * q + (1 - c)), dst_ref=ctx.out(a).at[q],
                    send_sem=ctx.send(N_CHIPS * a + q), recv_sem=ctx.recv(N_CHIPS * a + q), device_id=(x, y, 1 - c),
                    device_id_type=MESH))
        return out

    def start(ctx):
        for cp in copies(ctx):
            cp.start()

    def finish(ctx):
        for cp in copies(ctx):
            cp.wait_recv()
            cp.wait_send()

    return _Piece([s[0] for s in specs], [s[1] for s in specs], {}, (N_CHIPS * n, N_CHIPS * n, 0), start, finish)


def _rs_chip_piece(specs, layer):
    n = len(specs)
    hops = [(1, 0), (0, 1), (1, 1)]

    def copies(ctx):
        x, y, c = _place()
        mine = 2 * x + y
        out = []
        for a in range(n):
            first, landing, size = specs[a][2]
            rows, to = pl.ds(first, size), pl.ds(landing, size)
            sums, land = ctx.inp(a), ctx.out(a)
            out.append((pltpu.make_async_copy(sums.at[mine, rows], land.at[layer, mine, to], ctx.local(a)), None))
            for j, (dx, dy) in enumerate(hops):
                px, py = x ^ dx, y ^ dy
                peer = 2 * px + py
                send = pltpu.make_async_remote_copy(
                    src_ref=sums.at[peer, rows], dst_ref=land.at[layer, mine, to], send_sem=ctx.send(3 * a + j),
                    recv_sem=ctx.recv(3 * a + j), device_id=(px, py, c), device_id_type=MESH)
                recv = pltpu.make_async_remote_copy(
                    src_ref=sums.at[peer, rows], dst_ref=land.at[layer, peer, to], send_sem=ctx.send(3 * a + j),
                    recv_sem=ctx.recv(3 * a + j), device_id=(px, py, c), device_id_type=MESH)
                out.append((send, recv))
        return out

    def start(ctx):
        for send, _ in copies(ctx):
            send.start()

    def finish(ctx):
        for send, recv in copies(ctx):
            if recv is None:
                send.wait()
            else:
                recv.wait_recv()
                send.wait_send()

    operands, aliases = [s[0] for s in specs], {}
    for a, spec in enumerate(specs):
        if spec[3] is not None:
            aliases[len(operands)] = a
            operands.append(spec[3])
    return _Piece(operands, [s[1] for s in specs], aliases, (3 * n, 3 * n, n), start, finish)


def _all_reduce_piece(pack):
    r = pack.shape[0]
    half = r // 2

    def plan(ctx):
        x, y, c = _place()
        acc, got = ctx.vmem(0), ctx.vmem(1)
        mine = pl.ds(pl.multiple_of(c * half, 8), half)
        sib = (x, y, 1 - c)
        copies = [
            pltpu.make_async_remote_copy(src_ref=acc.at[0], dst_ref=got.at[0], send_sem=ctx.send(0), recv_sem=ctx.recv(0),
                                         device_id=sib, device_id_type=MESH),
            pltpu.make_async_remote_copy(src_ref=acc.at[1, mine], dst_ref=got.at[1, mine], send_sem=ctx.send(1),
                                         recv_sem=ctx.recv(1), device_id=(1 - x, y, c), device_id_type=MESH),
            pltpu.make_async_remote_copy(src_ref=acc.at[2, mine], dst_ref=got.at[2, mine], send_sem=ctx.send(2),
                                         recv_sem=ctx.recv(2), device_id=(x, 1 - y, c), device_id_type=MESH),
            pltpu.make_async_remote_copy(src_ref=acc.at[3, mine], dst_ref=acc.at[3, mine], send_sem=ctx.send(3),
                                         recv_sem=ctx.recv(3), device_id=sib, device_id_type=MESH),
        ]
        other = pl.ds(pl.multiple_of((1 - c) * half, 8), half)
        arrival = pltpu.make_async_remote_copy(src_ref=acc.at[3, other], dst_ref=acc.at[3, other], send_sem=ctx.send(3),
                                               recv_sem=ctx.recv(3), device_id=sib, device_id_type=MESH)
        return acc, got, mine, copies, arrival

    def start(ctx):
        acc, got, mine, copies, arrival = plan(ctx)
        load = pltpu.make_async_copy(ctx.inp(0), acc.at[0], ctx.local(0))
        load.start()
        load.wait()
        copies[0].start()

    def mid1(ctx):
        acc, got, mine, copies, arrival = plan(ctx)
        copies[0].wait()
        acc[1] = acc[0] + got[0]
        copies[1].start()

    def mid2(ctx):
        acc, got, mine, copies, arrival = plan(ctx)
        copies[1].wait()
        acc[2, mine] = acc[1, mine] + got[1, mine]
        copies[2].start()

    def finish(ctx):
        acc, got, mine, copies, arrival = plan(ctx)
        copies[2].wait()
        acc[3, mine] = acc[2, mine] + got[2, mine]
        copies[3].start()
        copies[3].wait_send()
        arrival.wait_recv()
        store = pltpu.make_async_copy(acc.at[3], ctx.out(0), ctx.local(0))
        store.start()
        store.wait()

    return _Piece([pack], [_sds(pack.shape, F32)], {}, (4, 4, 1), start, finish, mid1, mid2,
                  vmem=[pltpu.VMEM((4, r, 128), F32), pltpu.VMEM((3, r, 128), F32)], hooks=(0.25, 0.6))


def _chip_sums(name, grad, stage, by_cols, core):
    _, r, c = stage.shape
    tr = r
    while tr * c > 1024 * 1024 or r % tr or tr % 16:
        tr -= 16
    n_t = r // tr

    def body(core_ref, g_ref, s_ref, o_ref):
        o_ref[...] = (g_ref[...].astype(F32) + s_ref[...].astype(F32)).astype(BF16)

    if by_cols:
        gspec = pl.BlockSpec((tr, c), lambda q, i, core_ref: (i, 2 * q + core_ref[0]))
    else:
        gspec = pl.BlockSpec((tr, c), lambda q, i, core_ref: ((2 * q + core_ref[0]) * n_t + i, 0))
    sspec = pl.BlockSpec((None, tr, c), lambda q, i, core_ref: (q, i, 0))
    return pl.pallas_call(
        body, name=name, out_shape=_sds(stage.shape, BF16),
        grid_spec=pltpu.PrefetchScalarGridSpec(num_scalar_prefetch=1, grid=(N_CHIPS, n_t), in_specs=[gspec, sspec],
                                               out_specs=sspec),
        compiler_params=_cparams(2))(core, grad, stage)


def _call_hosting(body, name, grid, out_shapes, in_specs, out_specs, operands, scratch, comm):
    n_in, n_out, n_scr = len(operands), len(out_shapes), len(scratch)
    hosted = _Hosted(comm, n_in, n_out)
    n_ci, n_co = len(hosted.operands), len(hosted.out_shapes)

    def hosting_body(*refs):
        ins, rest = refs[:n_in], refs[n_in:]
        c_ins, rest = rest[:n_ci], rest[n_ci:]
        outs, rest = rest[:n_out], rest[n_out:]
        c_outs, rest = rest[:n_co], rest[n_co:]
        hosted.wrap(grid, lambda: body(*ins, *outs, *rest[:n_scr]), c_ins, c_outs, rest[n_scr:])

    res = pl.pallas_call(
        hosting_body, name=name, grid=grid, out_shape=tuple(list(out_shapes) + hosted.out_shapes),
        in_specs=list(in_specs) + hosted.in_specs, out_specs=tuple(list(out_specs) + hosted.out_specs),
        input_output_aliases=hosted.aliases, scratch_shapes=list(scratch) + hosted.scratch,
        compiler_params=_cparams(len(grid)))(*operands, *hosted.operands)
    return list(res[:n_out]), list(res[n_out:])


def _matmul(name, grid, nk, kaxis, pairs, dims, extras, outs, epilogue, sum_pairs, acc_shape, comm=None, split=None):
    n_p, n_e, n_o = len(pairs), len(extras), len(outs)
    n_acc = 0 if nk == 1 else (1 if sum_pairs else n_p)
    n_in = 2 * n_p + n_e
    hosted = _Hosted(comm, n_in, n_o)
    n_ci, n_co = len(hosted.operands), len(hosted.out_shapes)

    def body(*refs):
        a_refs = refs[0:2 * n_p:2]
        b_refs = refs[1:2 * n_p:2]
        e_refs = refs[2 * n_p:n_in]
        c_ins = refs[n_in:n_in + n_ci]
        o_refs = refs[n_in + n_ci:n_in + n_ci + n_o]
        c_outs = refs[n_in + n_ci + n_o:n_in + n_ci + n_o + n_co]
        acc_refs = refs[n_in + n_ci + n_o + n_co:n_in + n_ci + n_o + n_co + n_acc]
        sems = refs[n_in + n_ci + n_o + n_co + n_acc:]

        def dots():
            if sum_pairs and n_p > 1 and dims == NN:
                a_all = jnp.concatenate([a[...] for a in a_refs], axis=1)
                b_all = jnp.concatenate([b[...] for b in b_refs], axis=0)
                return [lax.dot_general(a_all, b_all, (dims, ((), ())), preferred_element_type=F32)]
            prods = [lax.dot_general(a[...], b[...], (dims, ((), ())), preferred_element_type=F32)
                     for a, b in zip(a_refs, b_refs)]
            if sum_pairs and n_p > 1:
                prods = [functools.reduce(operator.add, prods)]
            return prods

        def compute():
            if nk == 1 and split is not None:
                n_split, b_axis, n_row = split
                width = b_refs[0].shape[b_axis] // n_split
                height = a_refs[0].shape[0] // n_row
                for s in range(n_split):
                    cols = pl.ds(s * width, width)
                    for r in range(n_row):
                        rows = pl.ds(r * height, height)
                        epilogue([lax.dot_general(a[rows, :], b[cols, :] if b_axis == 0 else b[:, cols], (dims, ((), ())),
                                                  preferred_element_type=F32) for a, b in zip(a_refs, b_refs)],
                                 e_refs, o_refs, rows, cols)
                return
            if nk == 1:
                epilogue(dots(), e_refs, o_refs)
                return
            k = pl.program_id(kaxis)

            @pl.when(k == 0)
            def _():
                for acc, p in zip(acc_refs, dots()):
                    acc[...] = p

            if nk > 2:
                @pl.when((k > 0) & (k < nk - 1))
                def _():
                    for acc, p in zip(acc_refs, dots()):
                        acc[...] += p

            @pl.when(k == nk - 1)
            def _():
                epilogue([acc[...] + p for acc, p in zip(acc_refs, dots())], e_refs, o_refs)

        hosted.wrap(grid, compute, c_ins, c_outs, sems)

    operands, in_specs = [], []
    for a, a_spec, b, b_spec in pairs:
        operands += [a, b]
        in_specs += [a_spec, b_spec]
    for e, e_spec in extras:
        operands.append(e)
        in_specs.append(e_spec)
    res = pl.pallas_call(
        body, name=name, grid=grid,
        out_shape=tuple([o for o, _ in outs] + hosted.out_shapes),
        in_specs=in_specs + hosted.in_specs, out_specs=tuple([s for _, s in outs] + hosted.out_specs),
        input_output_aliases=hosted.aliases,
        scratch_shapes=[pltpu.VMEM(acc_shape, F32) for _ in range(n_acc)] + hosted.scratch,
        compiler_params=_cparams(len(grid)),
    )(*operands, *hosted.operands)
    return list(res[:n_o]), list(res[n_o:])


NN = ((1,), (0,))
NT = ((1,), (1,))
TN = ((0,), (0,))


def _tile(n, want):
    if n <= want:
        return n
    t = want // 128 * 128
    while n % t:
        t -= 128
    return t


def _silu_parts(g):
    s = 0.5 + 0.5 * jnp.tanh(0.5 * g)
    return s, g * s


def _mm_in(h, w_in, comm=None):
    t, d = h.shape
    n = w_in.shape[1]
    tm, tn = _tile(t, _TM), _tile(n, _TN)

    def epi(accs, e, o):
        o[0][...] = accs[0].astype(BF16)

    outs, couts = _matmul(
        "mm_in", (n // tn, t // tm), 1, None,
        [(h, pl.BlockSpec((tm, d), lambda j, i: (i, 0)), w_in, pl.BlockSpec((d, tn), lambda j, i: (0, j)))],
        NN, [], [(_sds((t, n), BF16), pl.BlockSpec((tm, tn), lambda j, i: (i, j)))], epi, True, None, comm)
    return outs[0], couts


def _mm_out(y, w_out, x, comm=None):
    t, m = y.shape
    d = w_out.shape[1]
    tm, tn = _tile(t, _TM), _tile(d, _TN)

    def epi(accs, e, o):
        o[0][...] = e[0][...] + accs[0]

    outs, couts = _matmul(
        "mm_out", (t // tm, d // tn), 1, None,
        [(y, pl.BlockSpec((tm, m), lambda i, j: (i, 0)), w_out, pl.BlockSpec((m, tn), lambda i, j: (0, j)))],
        NN, [(x, pl.BlockSpec((tm, tn), lambda i, j: (i, j)))],
        [(_sds((t, d), F32), pl.BlockSpec((tm, tn), lambda i, j: (i, j)))], epi, True, None, comm)
    return outs[0], couts


def _mm_swiglu(h2, wgt, wut, comm=None):
    t, d = h2.shape
    f = wgt.shape[0]
    tm, tn = _tile(t, 2 * _TM), _tile(f, 512)

    def epi(accs, e, o, rows, cols):
        g, u = accs
        s, sg = _silu_parts(g)
        o[0][rows, cols] = (sg * u).astype(BF16)
        o[1][rows, cols] = (u * (s + sg * (1.0 - s))).astype(BF16)
        o[2][rows, cols] = sg.astype(BF16)

    wspec = pl.BlockSpec((tn, d), lambda i, j: (j, 0))
    hspec = pl.BlockSpec((tm, d), lambda i, j: (i, 0))
    ospec = pl.BlockSpec((tm, tn), lambda i, j: (i, j))
    osh = _sds((t, f), BF16)
    outs, couts = _matmul("mm_swiglu", (t // tm, f // tn), 1, None, [(h2, hspec, wgt, wspec), (h2, hspec, wut, wspec)],
                          NT, [], [(osh, ospec)] * 3, epi, False, None, comm, split=(tn // 256, 0, 2))
    return outs, couts


def _mm_down(act, wd, x1, comm=None):
    t, f = act.shape
    d = wd.shape[1]
    tm, tn = _tile(t, _TM), _tile(d, _TN)
    nk = 2
    tk = f // nk

    def epi(accs, e, o):
        o[0][...] = e[0][...] + accs[0]

    outs, couts = _matmul(
        "mm_down", (t // tm, d // tn, nk), nk, 2,
        [(act, pl.BlockSpec((tm, tk), lambda i, j, k: (i, k)), wd, pl.BlockSpec((tk, tn), lambda i, j, k: (k, j)))],
        NN, [(x1, pl.BlockSpec((tm, tn), lambda i, j, k: (i, j)))],
        [(_sds((t, d), F32), pl.BlockSpec((tm, tn), lambda i, j, k: (i, j)))], epi, True, (tm, tn), comm)
    return outs[0], couts


def _mm_dact(dxb, wd, dact_dgate, dact_dup, comm=None):
    t, d = dxb.shape
    f = wd.shape[0]
    tm, tn = _tile(t, 2 * _TM), _tile(f, 512)

    def epi(accs, e, o, rows, cols):
        da = accs[0]
        o[0][rows, cols] = (da * e[0][rows, cols].astype(F32)).astype(BF16)
        o[1][rows, cols] = (da * e[1][rows, cols].astype(F32)).astype(BF16)

    bspec = pl.BlockSpec((tm, tn), lambda i, j: (i, j))
    osh = _sds((t, f), BF16)
    outs, couts = _matmul(
        "mm_dact", (t // tm, f // tn), 1, None,
        [(dxb, pl.BlockSpec((tm, d), lambda i, j: (i, 0)), wd, pl.BlockSpec((tn, d), lambda i, j: (j, 0)))],
        NT, [(dact_dgate, bspec), (dact_dup, bspec)], [(osh, bspec)] * 2, epi, True, None, comm, split=(tn // 256, 0, 2))
    return outs, couts


def _mm_dh2(dgate, dup, wgt, wut, comm=None):
    t, f = dgate.shape
    d = wgt.shape[1]
    tm, tn = _tile(t, _TM), _tile(d, _TN)
    nk = 4
    tk = f // nk

    def epi(accs, e, o):
        o[0][...] = accs[0]

    aspec = pl.BlockSpec((tm, tk), lambda i, j, k: (i, k))
    wspec = pl.BlockSpec((tk, tn), lambda i, j, k: (k, j))
    outs, couts = _matmul("mm_dh2", (t // tm, d // tn, nk), nk, 2, [(dgate, aspec, wgt, wspec), (dup, aspec, wut, wspec)],
                          NN, [], [(_sds((t, d), F32), pl.BlockSpec((tm, tn), lambda i, j, k: (i, j)))], epi, True,
                          (tm, tn), comm)
    return outs[0], couts


def _mm_dw(name, a_list, b, tmo, tno, comm=None, m_rows=None):
    t, m = a_list[0].shape
    start, m = (0, m) if m_rows is None else m_rows
    n = b.shape[1]
    tt = _tile(t, _TT)
    nk = t // tt
    tmo, tno = _tile(m, tmo), _tile(n, tno)
    first = start // tmo

    def epi(accs, e, o):
        for acc, out in zip(accs, o):
            out[...] = acc.astype(BF16)

    aspec = pl.BlockSpec((tt, tmo), lambda i, j, k: (k, first + i))
    bspec = pl.BlockSpec((tt, tno), lambda i, j, k: (k, j))
    ospec = pl.BlockSpec((tmo, tno), lambda i, j, k: (i, j))
    if nk == 1:
        return _matmul(name, (m // tmo, n // tno, 1), 1, None, [(a, aspec, b, bspec) for a in a_list], TN, [],
                       [(_sds((m, n), BF16), ospec)] * len(a_list), epi, False, None, comm)
    return _matmul(name, (m // tmo, n // tno, nk), nk, 2, [(a, aspec, b, bspec) for a in a_list], TN, [],
                   [(_sds((m, n), BF16), ospec)] * len(a_list), epi, False, (tmo, tno), comm)


def _mm_dy(dxb, w_out, comm=None):
    t, d = dxb.shape
    m = w_out.shape[0]
    tm, tn = _tile(t, _TM), _tile(m, _TN)

    def epi(accs, e, o):
        o[0][...] = accs[0].astype(BF16)

    outs, couts = _matmul(
        "mm_dy", (t // tm, m // tn), 1, None,
        [(dxb, pl.BlockSpec((tm, d), lambda i, j: (i, 0)), w_out, pl.BlockSpec((tn, d), lambda i, j: (j, 0)))], NT, [],
        [(_sds((t, m), BF16), pl.BlockSpec((tm, tn), lambda i, j: (i, j)))], epi, True, None, comm)
    return outs[0], couts


def _mm_dh(dz, w_in, comm=None):
    t, n = dz.shape
    d = w_in.shape[0]
    tm, tn = _tile(t, _TM), _tile(d, _TN)
    nk = 2
    tk = n // nk

    def epi(accs, e, o):
        o[0][...] = accs[0]

    outs, couts = _matmul(
        "mm_dh", (t // tm, d // tn, nk), nk, 2,
        [(dz, pl.BlockSpec((tm, tk), lambda i, j, k: (i, k)), w_in, pl.BlockSpec((tn, tk), lambda i, j, k: (j, k)))], NT,
        [], [(_sds((t, d), F32), pl.BlockSpec((tm, tn), lambda i, j, k: (i, j)))], epi, True, (tm, tn), comm)
    return outs[0], couts


def _rmsnorm_fwd(x, g, comm=None):
    t, d = x.shape
    tm = min(_TM_NORM, t)

    def body(x_ref, g_ref, o_ref):
        xv = x_ref[...]
        rs = lax.rsqrt(jnp.mean(xv * xv, axis=-1, keepdims=True) + RMS_EPS)
        o_ref[...] = (xv * rs * g_ref[...]).astype(BF16)

    outs, couts = _call_hosting(
        body, "rmsnorm_fwd", (t // tm,), [_sds((t, d), BF16)],
        [pl.BlockSpec((tm, d), lambda i: (i, 0)), pl.BlockSpec((1, d), lambda i: (0, 0))],
        [pl.BlockSpec((tm, d), lambda i: (i, 0))], [x, g], [], comm)
    return outs[0], couts


def _rmsnorm_bwd_math(xv, g, dh):
    rs = lax.rsqrt(jnp.mean(xv * xv, axis=-1, keepdims=True) + RMS_EPS)
    xh = xv * rs
    gd = dh * g
    dx = rs * (gd - xh * jnp.mean(gd * xh, axis=-1, keepdims=True))
    return dx, jnp.sum(dh * xh, axis=0, keepdims=True)


def _rmsnorm_bwd(x, g, dh, dres):
    t, d = x.shape
    tm = min(_TM_NORM, t)

    def body(x_ref, g_ref, dh_ref, dres_ref, dx_ref, dxb_ref, dg_ref):
        dx, dg = _rmsnorm_bwd_math(x_ref[...], g_ref[...], dh_ref[...])
        dx = dx + dres_ref[...]
        dx_ref[...] = dx
        dxb_ref[...] = dx.astype(BF16)

        @pl.when(pl.program_id(0) == 0)
        def _():
            dg_ref[...] = dg

        @pl.when(pl.program_id(0) > 0)
        def _():
            dg_ref[...] += dg

    row = pl.BlockSpec((tm, d), lambda i: (i, 0))
    vec = pl.BlockSpec((1, d), lambda i: (0, 0))
    return pl.pallas_call(
        body, name="rmsnorm_bwd", grid=(t // tm,),
        out_shape=(_sds((t, d), F32), _sds((t, d), BF16), _sds((1, d), F32)),
        in_specs=[row, vec, row, row], out_specs=(row, row, vec), compiler_params=_cparams(1))(x, g, dh, dres)


def _loss_head(x, g, target):
    t, d = x.shape
    tm = min(_TM_NORM, t)

    def body(x_ref, g_ref, t_ref, dx_ref, dxb_ref, dg_ref, loss_ref):
        xv, gv = x_ref[...], g_ref[...]
        rs = lax.rsqrt(jnp.mean(xv * xv, axis=-1, keepdims=True) + RMS_EPS)
        diff = xv * rs * gv - t_ref[...]
        part = 0.5 * jnp.sum(jnp.mean(diff * diff, axis=-1, keepdims=True), axis=0, keepdims=True)
        part = jnp.broadcast_to(part, (1, 128))
        dx, dg = _rmsnorm_bwd_math(xv, gv, diff * (1.0 / d))
        dx_ref[...] = dx
        dxb_ref[...] = dx.astype(BF16)

        @pl.when(pl.program_id(0) == 0)
        def _():
            dg_ref[...] = dg
            loss_ref[...] = part

        @pl.when(pl.program_id(0) > 0)
        def _():
            dg_ref[...] += dg
            loss_ref[...] += part

    row = pl.BlockSpec((tm, d), lambda i: (i, 0))
    vec = pl.BlockSpec((1, d), lambda i: (0, 0))
    return pl.pallas_call(
        body, name="loss_head", grid=(t // tm,),
        out_shape=(_sds((t, d), F32), _sds((t, d), BF16), _sds((1, d), F32), _sds((1, 128), F32)),
        in_specs=[row, vec, row], out_specs=(row, row, vec, pl.BlockSpec((1, 128), lambda i: (0, 0))),
        compiler_params=_cparams(1))(x, g, target)


def _gelu(x):
    th = jnp.tanh(GELU_C * (x + GELU_A * x * x * x))
    return 0.5 * x * (1.0 + th), th


def _gelu_grad(x, th):
    return 0.5 * (1.0 + th) + 0.5 * x * (1.0 - th * th) * GELU_C * (1.0 + 3.0 * GELU_A * x * x)


def _masked_ws(ws_ref, h):
    i = lax.broadcasted_iota(jnp.int32, (BLK, BLK), 0) // CHUNK
    j = lax.broadcasted_iota(jnp.int32, (BLK, BLK), 1) // CHUNK
    return jnp.where(j <= i, ws_ref[h], 0.0)


def _shift_down(q, n, first_rows):
    rolled = pltpu.roll(q, n, 0)
    row = lax.broadcasted_iota(jnp.int32, q.shape, 0)
    for r, val in enumerate(first_rows):
        rolled = jnp.where(row == r, val, rolled)
    return rolled


def _shift_up(q, n, last_rows):
    tm = q.shape[0]
    rolled = pltpu.roll(q, tm - n, 0)
    row = lax.broadcasted_iota(jnp.int32, q.shape, 0)
    for r, val in enumerate(last_rows):
        rolled = jnp.where(row == tm - n + r, val, rolled)
    return rolled


def _mixer_specs(t, a, tm):
    hb = tm // HALO
    last = t // HALO - 1
    tile = pl.BlockSpec((tm, 5 * a), lambda i: (i, 0))
    prev = [pl.BlockSpec((HALO, a), functools.partial(lambda i, col: (jnp.maximum(i * hb - 1, 0), col), col=col))
            for col in (3, 4)]
    nxt = [pl.BlockSpec((HALO, a), functools.partial(lambda i, col: (jnp.minimum((i + 1) * hb, last), col), col=col))
           for col in (2, 3, 4)]
    return tile, prev, nxt


def _group_a_fwd(zu, zv, lng, lnb, ws_ref, bb_ref, mixed_ref, vln_ref):
    u, thu = _gelu(zu)
    v, thv = _gelu(zv)
    mu = jnp.mean(v, axis=-1, keepdims=True)
    vc = v - mu
    rs = lax.rsqrt(jnp.mean(vc * vc, axis=-1, keepdims=True) + LN_EPS)
    vhat = vc * rs
    vln_ref[...] = vhat * lng + lnb
    tm, a = zu.shape
    hd = a // HEADS
    for h in range(HEADS):
        w = _masked_ws(ws_ref, h).astype(BF16)
        for b in range(tm // BLK):
            rows, cols = pl.ds(b * BLK, BLK), pl.ds(h * hd, hd)
            mixed_ref[rows, cols] = jnp.dot(w, vln_ref[rows, cols].astype(BF16), preferred_element_type=F32) + bb_ref[h]
    return u, thu, thv, rs, vhat


def _mixer_fwd(z, ln_g, ln_b, w_spatial, bb, conv_w, gg, comm=None):
    t = z.shape[0]
    a = z.shape[1] // 5
    tm = min(_TM_MIX, t)
    tile, prev, _ = _mixer_specs(t, a, tm)

    def body(z_ref, pc_ref, ph_ref, lng_ref, lnb_ref, ws_ref, bb_ref, cw_ref, gg_ref, y_ref, mixed_ref, vln_ref):
        i = pl.program_id(0)
        zu = z_ref[:, 0:a].astype(F32)
        zv = z_ref[:, a:2 * a].astype(F32)
        u, _, _, _, _ = _group_a_fwd(zu, zv, lng_ref[...], lnb_ref[...], ws_ref, bb_ref, mixed_ref, vln_ref)
        ya = u * mixed_ref[...]
        ra = lax.rsqrt(jnp.mean(ya * ya, axis=-1, keepdims=True) + RMS_EPS)
        y_ref[:, 0:a] = (ya * ra * gg_ref[:, 0:a]).astype(BF16)

        zb = z_ref[:, 2 * a:3 * a].astype(F32)
        q = z_ref[:, 3 * a:4 * a].astype(F32) * z_ref[:, 4 * a:5 * a].astype(F32)
        qp = jnp.where(i > 0, pc_ref[...].astype(F32) * ph_ref[...].astype(F32), 0.0)
        qm1 = _shift_down(q, 1, [qp[HALO - 1:HALO]])
        qm2 = _shift_down(q, 2, [qp[HALO - 2:HALO - 1], qp[HALO - 1:HALO]])
        cv = cw_ref[0:1, :] * qm2 + cw_ref[1:2, :] * qm1 + cw_ref[2:3, :] * q
        yb = zb * cv
        rb = lax.rsqrt(jnp.mean(yb * yb, axis=-1, keepdims=True) + RMS_EPS)
        y_ref[:, a:2 * a] = (yb * rb * gg_ref[:, a:2 * a]).astype(BF16)

    full = lambda shape: pl.BlockSpec(shape, lambda i: (0,) * len(shape))
    outs, couts = _call_hosting(
        body, "mixer_fwd", (t // tm,), [_sds((t, 2 * a), BF16)],
        [tile, *prev, full((1, a)), full((1, a)), full(w_spatial.shape), full(bb.shape), full(conv_w.shape),
         full((1, 2 * a))],
        [pl.BlockSpec((tm, 2 * a), lambda i: (i, 0))], [z, z, z, ln_g, ln_b, w_spatial, bb, conv_w, gg],
        [pltpu.VMEM((tm, a), F32), pltpu.VMEM((tm, a), F32)], comm)
    return outs[0], couts


def _mixer_bwd(z, dy, ln_g, ln_b, w_spatial, bb, conv_w, gg, comm=None):
    t = z.shape[0]
    a = z.shape[1] // 5
    hd = a // HEADS
    tm = min(_TM_MIX, t)
    n_tiles = t // tm
    tile, prev, nxt = _mixer_specs(t, a, tm)
    hb = tm // HALO
    dy_tile = pl.BlockSpec((tm, 2 * a), lambda i: (i, 0))
    dy_next = pl.BlockSpec((HALO, a), lambda i: (jnp.minimum((i + 1) * hb, t // HALO - 1), 1))

    def body(z_ref, pc_ref, ph_ref, nb_ref, nc_ref, nh_ref, dy_ref, ndy_ref, lng_ref, lnb_ref, ws_ref, bb_ref, cw_ref,
             gg_ref, dz_ref, dlng_ref, dlnb_ref, dws_ref, dbb_ref, dcw_ref, dgg_ref, mixed_ref, vln_ref, dmix_ref,
             dvln_ref):
        i = pl.program_id(0)

        @pl.when(i == 0)
        def _():
            for ref in (dlng_ref, dlnb_ref, dws_ref, dbb_ref, dcw_ref, dgg_ref):
                ref[...] = jnp.zeros(ref.shape, F32)

        lng = lng_ref[...]
        zu = z_ref[:, 0:a].astype(F32)
        zv = z_ref[:, a:2 * a].astype(F32)
        u, thu, thv, rs, vhat = _group_a_fwd(zu, zv, lng, lnb_ref[...], ws_ref, bb_ref, mixed_ref, vln_ref)
        mixed = mixed_ref[...]
        ya = u * mixed
        ra = lax.rsqrt(jnp.mean(ya * ya, axis=-1, keepdims=True) + RMS_EPS)
        da = dy_ref[:, 0:a].astype(F32)
        yah = ya * ra
        dgg_ref[:, 0:a] += jnp.sum(da * yah, axis=0, keepdims=True)
        ga = da * gg_ref[:, 0:a]
        dya = ra * (ga - yah * jnp.mean(ga * yah, axis=-1, keepdims=True))
        dz_ref[:, 0:a] = (dya * mixed * _gelu_grad(zu, thu)).astype(BF16)
        dmix_ref[...] = dya * u
        for h in range(HEADS):
            w = _masked_ws(ws_ref, h).astype(BF16)
            dw = jnp.zeros((BLK, BLK), F32)
            db = jnp.zeros((BLK, hd), F32)
            for b in range(tm // BLK):
                rows, cols = pl.ds(b * BLK, BLK), pl.ds(h * hd, hd)
                dm = dmix_ref[rows, cols]
                dmb = dm.astype(BF16)
                db = db + dm
                dw = dw + lax.dot_general(dmb, vln_ref[rows, cols].astype(BF16), (NT, ((), ())),
                                          preferred_element_type=F32)
                dvln_ref[rows, cols] = lax.dot_general(w, dmb, (TN, ((), ())), preferred_element_type=F32)
            dws_ref[h] += dw
            dbb_ref[h] += db
        dvln = dvln_ref[...]
        dlng_ref[...] += jnp.sum(dvln * vhat, axis=0, keepdims=True)
        dlnb_ref[...] += jnp.sum(dvln, axis=0, keepdims=True)
        dvh = dvln * lng
        dv = rs * (dvh - jnp.mean(dvh, axis=-1, keepdims=True) - vhat * jnp.mean(dvh * vhat, axis=-1, keepdims=True))
        dz_ref[:, a:2 * a] = (dv * _gelu_grad(zv, thv)).astype(BF16)

        w0, w1, w2 = cw_ref[0:1, :], cw_ref[1:2, :], cw_ref[2:3, :]
        ggb = gg_ref[:, a:2 * a]
        zb = z_ref[:, 2 * a:3 * a].astype(F32)
        zc = z_ref[:, 3 * a:4 * a].astype(F32)
        zh = z_ref[:, 4 * a:5 * a].astype(F32)
        q = zc * zh
        qp = jnp.where(i > 0, pc_ref[...].astype(F32) * ph_ref[...].astype(F32), 0.0)
        qm1 = _shift_down(q, 1, [qp[HALO - 1:HALO]])
        qm2 = _shift_down(q, 2, [qp[HALO - 2:HALO - 1], qp[HALO - 1:HALO]])
        cv = w0 * qm2 + w1 * qm1 + w2 * q

        def conv_out_grad(zb_, cv_, dout_):
            yb = zb_ * cv_
            rb = lax.rsqrt(jnp.mean(yb * yb, axis=-1, keepdims=True) + RMS_EPS)
            ybh = yb * rb
            gb = dout_ * ggb
            dyb = rb * (gb - ybh * jnp.mean(gb * ybh, axis=-1, keepdims=True))
            return dyb * zb_, dyb * cv_, ybh

        db_out = dy_ref[:, a:2 * a].astype(F32)
        g, dzb, ybh = conv_out_grad(zb, cv, db_out)
        dgg_ref[:, a:2 * a] += jnp.sum(db_out * ybh, axis=0, keepdims=True)
        dz_ref[:, 2 * a:3 * a] = dzb.astype(BF16)
        qn = nc_ref[...].astype(F32) * nh_ref[...].astype(F32)
        zbn = nb_ref[...].astype(F32)
        cvn = w0 * _shift_down(qn, 2, [q[tm - 2:tm - 1], q[tm - 1:tm]]) + w1 * _shift_down(qn, 1, [q[tm - 1:tm]]) + w2 * qn
        gn, _, _ = conv_out_grad(zbn, cvn, ndy_ref[...].astype(F32))
        gn = jnp.where(i < n_tiles - 1, gn, 0.0)
        dq = w2 * g + w1 * _shift_up(g, 1, [gn[0:1]]) + w0 * _shift_up(g, 2, [gn[0:1], gn[1:2]])
        dz_ref[:, 3 * a:4 * a] = (dq * zh).astype(BF16)
        dz_ref[:, 4 * a:5 * a] = (dq * zc).astype(BF16)
        dcw_ref[0:1, :] += jnp.sum(g * qm2, axis=0, keepdims=True)
        dcw_ref[1:2, :] += jnp.sum(g * qm1, axis=0, keepdims=True)
        dcw_ref[2:3, :] += jnp.sum(g * q, axis=0, keepdims=True)

        @pl.when(i == n_tiles - 1)
        def _():
            for h in range(HEADS):
                dbb_ref[h] = jnp.broadcast_to(jnp.sum(dbb_ref[h], axis=1, keepdims=True), (BLK, hd))
                dws_ref[h] = _masked_ws(dws_ref, h)

    full = lambda shape: pl.BlockSpec(tuple(shape), lambda i: (0,) * len(shape))
    out_shapes = (_sds((t, 5 * a), BF16), _sds((1, a), F32), _sds((1, a), F32), _sds(w_spatial.shape, F32),
                  _sds(bb.shape, F32), _sds((8, a), F32), _sds((1, 2 * a), F32))
    return _call_hosting(
        body, "mixer_bwd", (n_tiles,), out_shapes,
        [tile, *prev, *nxt, dy_tile, dy_next, full((1, a)), full((1, a)), full(w_spatial.shape), full(bb.shape),
         full(conv_w.shape), full((1, 2 * a))],
        [tile, *[full(s.shape) for s in out_shapes[1:]]], [z, z, z, z, z, z, dy, dy, ln_g, ln_b, w_spatial, bb, conv_w, gg],
        [pltpu.VMEM((tm, a), F32)] * 4, comm)


def _all_reduce_small(pack, comm=None):
    r = pack.shape[0]
    hosted = _Hosted(comm, 1, 1)
    n_ci, n_co = len(hosted.operands), len(hosted.out_shapes)

    def body(*refs):
        in_ref, c_ins, out_ref, c_outs = refs[0], refs[1:1 + n_ci], refs[1 + n_ci], refs[2 + n_ci:2 + n_ci + n_co]
        acc_ref, recv_ref, send_sems, recv_sems = refs[2 + n_ci + n_co:6 + n_ci + n_co]
        sems = refs[6 + n_ci + n_co:]
        hosted.run("start", c_ins, c_outs, sems)
        x, y, c = _place()
        partners = [(x, y, 1 - c), (1 - x, y, c), (x, 1 - y, c)]
        acc_ref[0] = in_ref[...]
        for s, partner in enumerate(partners):
            cp = pltpu.make_async_remote_copy(
                src_ref=acc_ref.at[s], dst_ref=recv_ref.at[s], send_sem=send_sems.at[s], recv_sem=recv_sems.at[s],
                device_id=partner, device_id_type=MESH)
            cp.start()
            cp.wait()
            if s < 2:
                acc_ref[s + 1] = acc_ref[s] + recv_ref[s]
            else:
                out_ref[...] = acc_ref[s] + recv_ref[s]
        for stage in ("mid1", "mid2", "finish"):
            hosted.run(stage, c_ins, c_outs, sems)

    vmem = pl.BlockSpec(memory_space=pltpu.VMEM)
    res = pl.pallas_call(
        body, name="all_reduce_small", out_shape=tuple([_sds(pack.shape, F32)] + hosted.out_shapes),
        in_specs=[vmem] + hosted.in_specs, out_specs=tuple([vmem] + hosted.out_specs),
        input_output_aliases=hosted.aliases,
        scratch_shapes=[pltpu.VMEM((3, r, 128), F32), pltpu.VMEM((3, r, 128), F32), pltpu.SemaphoreType.DMA((3,)),
                        pltpu.SemaphoreType.DMA((3,))] + hosted.scratch,
        compiler_params=pltpu.CompilerParams(vmem_limit_bytes=VMEM_LIMIT_V7X),
    )(pack, *hosted.operands)
    return res[0], list(res[1:])


def _adamw_math(w, g, m, v):
    m = ADAM_B1 * m + (1.0 - ADAM_B1) * g
    v = ADAM_B2 * v + (1.0 - ADAM_B2) * (g * g)
    m_hat = m / (1.0 - ADAM_B1 ** ADAM_STEP)
    v_hat = v / (1.0 - ADAM_B2 ** ADAM_STEP)
    delta = -ADAM_LR * (m_hat / (jnp.sqrt(v_hat) + ADAM_EPS) + ADAM_WD * w)
    return delta, m, v


def _adamw_big(name, land, w, m, v, comm=None):
    nl, n_slots, r, c = land.shape
    tr = max(8, min(r, (256 * 640) // c // 8 * 8))
    while r % tr:
        tr -= 8
    grid = (nl, r // tr)
    hosted = _Hosted(comm, 4, 4)
    n_ci, n_co = len(hosted.operands), len(hosted.out_shapes)

    def body(*refs):
        land_ref, w_ref, m_ref, v_ref = refs[:4]
        c_ins = refs[4:4 + n_ci]
        g_out, d_out, m_out, v_out = refs[4 + n_ci:8 + n_ci]
        c_outs = refs[8 + n_ci:8 + n_ci + n_co]
        sems = refs[8 + n_ci + n_co:]

        def compute():
            g = land_ref[0].astype(F32)
            for s in range(1, n_slots):
                g = g + land_ref[s].astype(F32)
            delta, mn, vn = _adamw_math(w_ref[...], g, m_ref[...], v_ref[...])
            g_out[...] = g
            d_out[...] = delta
            m_out[...] = mn
            v_out[...] = vn

        hosted.wrap(grid, compute, c_ins, c_outs, sems)

    blk = pl.BlockSpec((None, tr, c), lambda l, i: (l, i, 0))
    res = pl.pallas_call(
        body, name=name, grid=grid, out_shape=tuple([_sds((nl, r, c), F32)] * 4 + hosted.out_shapes),
        in_specs=[pl.BlockSpec((None, n_slots, tr, c), lambda l, i: (l, 0, i, 0)), blk, blk, blk] + hosted.in_specs,
        out_specs=tuple([blk] * 4 + hosted.out_specs), input_output_aliases=hosted.aliases,
        scratch_shapes=hosted.scratch, compiler_params=_cparams(2))(land, w, m, v, *hosted.operands)
    return list(res[:4]), list(res[4:])


def _adamw_small(gs, ws, ms, vs):
    n = len(gs)

    def body(*refs):
        g_refs, w_refs, m_refs, v_refs = refs[:n], refs[n:2 * n], refs[2 * n:3 * n], refs[3 * n:4 * n]
        d_outs, m_outs, v_outs = refs[4 * n:5 * n], refs[5 * n:6 * n], refs[6 * n:7 * n]
        for i in range(n):
            delta, mn, vn = _adamw_math(w_refs[i][...], g_refs[i][...], m_refs[i][...], v_refs[i][...])
            d_outs[i][...] = delta
            m_outs[i][...] = mn
            v_outs[i][...] = vn

    shapes = [_sds(g.shape, F32) for g in gs]
    res = pl.pallas_call(body, name="adamw_small", out_shape=tuple(shapes * 3),
                         compiler_params=pltpu.CompilerParams(vmem_limit_bytes=VMEM_LIMIT_V7X))(*gs, *ws, *ms, *vs)
    return list(res[:n]), list(res[n:2 * n]), list(res[2 * n:])


def _rows(a):
    return a.reshape(-1, 128)


BIG = ["w_in", "w_out", "w_gate", "w_up", "w_down"]
AG_HOSTS = {
    ("norm1", 0): [("w_in", 0), ("conv_w", 0)],
    ("mm_in", 0): [("w_out", 0), ("w_gate", 0, 0, 2)], ("mixer", 0): [("w_gate", 0, 1, 2)],
    ("mm_out", 0): [("w_up", 0, 0, 2)], ("norm2", 0): [("w_up", 0, 1, 2)],
    ("mm_swiglu", 0): [("w_down", 0), ("w_in", 1)], ("mm_down", 0): [("w_out", 1), ("w_gate", 1, 0, 2)],
    ("mm_in", 1): [("w_gate", 1, 1, 2)], ("mixer", 1): [("w_up", 1, 0, 2)], ("mm_out", 1): [("w_up", 1, 1, 2)],
    ("mm_swiglu", 1): [("w_down", 1)],
}


def kernel(x, norm1_g, w_in, gmlp_ln_g, gmlp_ln_b, w_spatial, b_spatial, conv_w, group_norm_g, w_out, norm2_g, w_gate, w_up, w_down, final_norm_g, loss_target, m_norm1_g, m_w_in, m_gmlp_ln_g, m_gmlp_ln_b, m_w_spatial, m_b_spatial, m_conv_w, m_group_norm_g, m_w_out, m_norm2_g, m_w_gate, m_w_up, m_w_down, m_final_norm_g, v_norm1_g, v_w_in, v_gmlp_ln_g, v_gmlp_ln_b, v_w_spatial, v_b_spatial, v_conv_w, v_group_norm_g, v_w_out, v_norm2_g, v_w_gate, v_w_up, v_w_down, v_final_norm_g):
    nl = N_LAYERS
    t, d = x.shape[1], x.shape[2]
    a = d // 2
    hd = a // HEADS
    xin = x.reshape(t, d)
    target = loss_target.reshape(t, d)
    me = _index(_place())

    tr = lambda w: jnp.transpose(w, (0, 2, 1))
    big = {"w_in": w_in, "w_out": w_out, "w_gate": tr(w_gate), "w_up": tr(w_up), "w_down": w_down}
    big_m = {"w_in": m_w_in, "w_out": m_w_out, "w_gate": tr(m_w_gate), "w_up": tr(m_w_up), "w_down": m_w_down}
    big_v = {"w_in": v_w_in, "w_out": v_w_out, "w_gate": tr(v_w_gate), "w_up": tr(v_w_up), "w_down": v_w_down}
    block = {k: big[k].shape[1:] for k in BIG}
    view = {k: _cols_view(block[k][1]) if k == "w_in" else _rows_view(block[k][0]) for k in BIG}
    full_shape = {k: (block[k][0], N_DEV * block[k][1]) if k == "w_in" else (N_DEV * block[k][0], block[k][1])
                  for k in BIG}

    weights = {}
    shards = {(k, l): big[k][l].astype(BF16) for k in BIG for l in range(nl)}

    def ag_spec(k, l, part=0, n_parts=1):
        if k == "conv_w":
            return (conv_w, _sds((N_DEV, *conv_w.shape), F32), _SLOT_WHOLE, (0,), None)
        halves = (_cols_halves(*block[k], part, n_parts) if k == "w_in" else _rows_halves(block[k][0], part, n_parts))
        return (shards[(k, l)], _sds(full_shape[k], BF16), halves, (0, 1), weights.get((k, l)))

    bb = jnp.broadcast_to(b_spatial[..., None], (nl, HEADS, BLK, hd))

    def hosted(name, l):
        keys = AG_HOSTS.get((name, l), [])
        return keys, ([_ag_piece([ag_spec(*key) for key in keys])] if keys else None)

    def landed(keys, couts):
        for key, arr in zip(keys, couts):
            weights[key[:2]] = arr

    saved = []
    xl = xin
    for l in range(nl):
        keys, comm = hosted("norm1", l)
        h, couts = _rmsnorm_fwd(xl, norm1_g[l:l + 1], comm)
        landed(keys, couts)
        if l == 0:
            conv_full = jnp.transpose(weights[("conv_w", 0)], (1, 2, 0, 3)).reshape(nl, 3, a)
        keys, comm = hosted("mm_in", l)
        z, couts = _mm_in(h, weights[("w_in", l)], comm)
        landed(keys, couts)
        keys, comm = hosted("mixer", l)
        y, couts = _mixer_fwd(z, gmlp_ln_g[l:l + 1], gmlp_ln_b[l:l + 1], w_spatial[l], bb[l], conv_full[l],
                              group_norm_g[l:l + 1], comm)
        landed(keys, couts)
        keys, comm = hosted("mm_out", l)
        x1, couts = _mm_out(y, weights[("w_out", l)], xl, comm)
        landed(keys, couts)
        keys, comm = hosted("norm2", l)
        h2, couts = _rmsnorm_fwd(x1, norm2_g[l:l + 1], comm)
        landed(keys, couts)
        keys, comm = hosted("mm_swiglu", l)
        (act, dact_dgate, dact_dup), couts = _mm_swiglu(h2, weights[("w_gate", l)], weights[("w_up", l)], comm)
        landed(keys, couts)
        keys, comm = hosted("mm_down", l)
        x2, couts = _mm_down(act, weights[("w_down", l)], x1, comm)
        landed(keys, couts)
        saved.append(dict(x=xl, h=h, z=z, y=y, x1=x1, h2=h2, dact_dgate=dact_dgate, dact_dup=dact_dup, act=act))
        xl = x2

    dx, dxb, d_final_g, loss_part = _loss_head(xl, final_norm_g.reshape(1, d), target)
    small = [None] * nl
    core = lax.axis_index("c").astype(jnp.int32).reshape(1)
    in_rows = block["w_in"][0]
    part_of = {"w_in_a": ("w_in", 0), "w_in_b": ("w_in", 3 * in_rows // 4)}
    block["w_in_a"], block["w_in_b"] = (3 * in_rows // 4, block["w_in"][1]), (in_rows // 4, block["w_in"][1])
    for k in part_of:
        view[k] = view["w_in"]
    stage_shape = {k: _sds((N_CHIPS, *block[k]), BF16) for k in block}
    land_shape = {k: _sds((nl, N_CHIPS, *block[k]), BF16) for k in BIG}
    grads = [dict() for _ in range(nl)]
    stages = [dict() for _ in range(nl)]
    sums = [dict() for _ in range(nl)]
    lands = {k: None for k in BIG}

    def core_job(l, keys):
        def sink(outs):
            stages[l].update(zip(keys, outs))
        return _rs_core_piece([(grads[l][k], stage_shape[k], view[k]) for k in keys]), sink

    def chip_job(l, items):
        keys = [part_of.get(item[0], (item[0], 0))[0] for item in items]

        def rows(k, p0, p1, n_parts):
            per = block[k][0] // n_parts
            landing = part_of.get(k, (k, 0))[1]
            return (p0 * per, landing + p0 * per, (p1 - p0) * per)

        def sink(outs):
            lands.update(zip(keys, outs))
        return _rs_chip_piece([(sums[l][k], land_shape[key], rows(k, p0, p1, n_parts), lands[key])
                               for key, (k, p0, p1, n_parts) in zip(keys, items)], l), sink

    def add_up(l, keys):
        for k in keys:
            sums[l][k] = _chip_sums(f"chip_sums_{k}", grads[l][k], stages[l][k], k.startswith("w_in"), core)

    def host(*jobs):
        def deliver(couts):
            i = 0
            for piece, sink in jobs:
                n_out = len(piece.out_shapes)
                sink(couts[i:i + n_out])
                i += n_out
        return [piece for piece, _ in jobs], deliver

    whole = lambda k: (k, 0, 1, 1)
    rep = ["norm1_g", "gmlp_ln_g", "gmlp_ln_b", "w_spatial", "b_spatial", "group_norm_g", "norm2_g"]
    rep_w = dict(norm1_g=norm1_g, gmlp_ln_g=gmlp_ln_g, gmlp_ln_b=gmlp_ln_b, w_spatial=w_spatial, b_spatial=b_spatial,
                 group_norm_g=group_norm_g, norm2_g=norm2_g)
    rep_m = dict(norm1_g=m_norm1_g, gmlp_ln_g=m_gmlp_ln_g, gmlp_ln_b=m_gmlp_ln_b, w_spatial=m_w_spatial,
                 b_spatial=m_b_spatial, group_norm_g=m_group_norm_g, norm2_g=m_norm2_g)
    rep_v = dict(norm1_g=v_norm1_g, gmlp_ln_g=v_gmlp_ln_g, gmlp_ln_b=v_gmlp_ln_b, w_spatial=v_w_spatial,
                 b_spatial=v_b_spatial, group_norm_g=v_group_norm_g, norm2_g=v_norm2_g)

    def small_grad_parts():
        parts = [_rows(jnp.stack([small[l][k].reshape(rep_w[k].shape[1:]) for l in range(nl)])) for k in rep]
        parts.append(_rows(d_final_g))
        parts.append(_rows(jnp.stack([small[l]["conv_w"] for l in range(nl)])))
        parts.append(jnp.broadcast_to(loss_part, (8, 128)))
        rows = sum(p.shape[0] for p in parts)
        parts.append(jnp.zeros((-rows % 16, 128), F32))
        return parts

    for l in reversed(range(nl)):
        s = saved[l]
        wi, wo, wgt, wut, wd = [weights[(k, l)] for k in BIG]
        later = l + 1 < nl
        comm, deliver = host(chip_job(l + 1, [("w_in", 0, 1, 2)])) if later else host()
        (grads[l]["w_down"],), couts = _mm_dw("mm_dw_down", [s["act"]], dxb, 2816, 1024, comm)
        deliver(couts)
        comm, deliver = (host(core_job(l, ["w_down"]), chip_job(l + 1, [("w_in", 1, 2, 2)])) if later
                         else host(core_job(l, ["w_down"])))
        (dgate, dup), couts = _mm_dact(dxb, wd, s["dact_dgate"], s["dact_dup"], comm)
        deliver(couts)
        add_up(l, ["w_down"])
        comm, deliver = host(chip_job(l, [("w_down", 0, 3, 4)]))
        (grads[l]["w_gate"],), couts = _mm_dw("mm_dw_gate", [dgate], s["h2"], 2816, 1024, comm)
        deliver(couts)
        comm, deliver = host(chip_job(l, [("w_down", 3, 4, 4)]), core_job(l, ["w_gate"]))
        (grads[l]["w_up"],), couts = _mm_dw("mm_dw_up", [dup], s["h2"], 2816, 1024, comm)
        deliver(couts)
        add_up(l, ["w_gate"])
        comm, deliver = host(chip_job(l, [whole("w_gate")]), core_job(l, ["w_up"]))
        dh2, couts = _mm_dh2(dgate, dup, wgt, wut, comm)
        deliver(couts)
        add_up(l, ["w_up"])
        dx1, dx1b, d_n2 = _rmsnorm_bwd(s["x1"], norm2_g[l:l + 1], dh2, dx)
        comm, deliver = host(chip_job(l, [("w_up", 0, 1, 4)]))
        dy, couts = _mm_dy(dx1b, wo, comm)
        deliver(couts)
        comm, deliver = host(chip_job(l, [("w_up", 1, 2, 4)]))
        (grads[l]["w_out"],), couts = _mm_dw("mm_dw_out", [s["y"]], dx1b, 1024, 1024, comm)
        deliver(couts)
        comm, deliver = host(chip_job(l, [("w_up", 2, 4, 4)]), core_job(l, ["w_out"]))
        (dz, d_lng, d_lnb, d_ws, d_bb, d_cw, d_gg), couts = _mixer_bwd(
            s["z"], dy, gmlp_ln_g[l:l + 1], gmlp_ln_b[l:l + 1], w_spatial[l], bb[l], conv_full[l], group_norm_g[l:l + 1],
            comm)
        deliver(couts)
        add_up(l, ["w_out"])
        small[l] = dict(norm1_g=jnp.zeros((1, d), F32), gmlp_ln_g=d_lng, gmlp_ln_b=d_lnb, w_spatial=d_ws,
                        b_spatial=d_bb[:, :, 0], group_norm_g=d_gg, norm2_g=d_n2, conv_w=d_cw[0:3])
        if l > 0:
            comm, deliver = host(chip_job(l, [whole("w_out")]))
            (grads[l]["w_in"],), couts = _mm_dw("mm_dw_in", [s["h"]], dz, 2048, 1024, comm)
            deliver(couts)
            comm, deliver = host(core_job(l, ["w_in"]))
            dh, couts = _mm_dh(dz, wi, comm)
            deliver(couts)
            add_up(l, ["w_in"])
        else:
            comm, deliver = host(chip_job(l, [whole("w_out")]))
            (grads[l]["w_in_a"],), couts = _mm_dw("mm_dw_in_a", [s["h"]], dz, block["w_in_a"][0], 1024, comm,
                                                  m_rows=(0, block["w_in_a"][0]))
            deliver(couts)
            comm, deliver = host(core_job(l, ["w_in_a"]))
            (grads[l]["w_in_b"],), couts = _mm_dw("mm_dw_in_b", [s["h"]], dz, block["w_in_b"][0], 2560, comm,
                                                  m_rows=(block["w_in_a"][0], block["w_in_b"][0]))
            deliver(couts)
            add_up(l, ["w_in_a"])
            parts = small_grad_parts()
            reduced = []
            comm, deliver = host(chip_job(l, [whole("w_in_a")]), core_job(l, ["w_in_b"]),
                                 (_all_reduce_piece(jnp.concatenate(parts, axis=0)), reduced.extend))
            dh, couts = _mm_dh(dz, wi, comm)
            deliver(couts)
            add_up(l, ["w_in_b"])
        dx, dxb, small[l]["norm1_g"] = _rmsnorm_bwd(s["x"], norm1_g[l:l + 1], dh, dx1)
    grad_x = dx.reshape(x.shape)

    sizes = [p.shape[0] for p in parts]
    comm, deliver = host(chip_job(0, [whole("w_in_b")]))
    last, couts = _all_reduce_small(_rows(small[0]["norm1_g"]), comm)
    deliver(couts)
    total = lax.dynamic_update_slice(reduced[0], last, (0, 0))
    offs = [0]
    for n in sizes:
        offs.append(offs[-1] + n)
    pieces = [total[offs[i]:offs[i + 1]] for i in range(len(parts))]
    loss = pieces[len(rep) + 2][0, 0]
    conv_g_full = pieces[len(rep) + 1].reshape(nl, 3, N_DEV, a // N_DEV)
    conv_g = lax.dynamic_index_in_dim(conv_g_full, me, axis=2, keepdims=False)
    names = rep + ["final_norm_g", "conv_w"]
    flat = lambda w: w.reshape(-1, w.shape[-1])
    small_w = [flat(rep_w[k]) for k in rep] + [flat(final_norm_g), flat(conv_w)]
    small_m = [flat(rep_m[k]) for k in rep] + [flat(m_final_norm_g), flat(m_conv_w)]
    small_v = [flat(rep_v[k]) for k in rep] + [flat(v_final_norm_g), flat(v_conv_w)]
    small_g = [pieces[i].reshape(small_w[i].shape) for i in range(len(rep) + 1)] + [flat(conv_g)]
    small_d, small_m, small_v = _adamw_small(small_g, small_w, small_m, small_v)
    shape_of = dict(rep_w, final_norm_g=final_norm_g, conv_w=conv_w)
    named = lambda arrays: {k: arr.reshape(shape_of[k].shape) for k, arr in zip(names, arrays)}
    res = {"grad": named(small_g), "delta": named(small_d), "m": named(small_m), "v": named(small_v)}

    for k in BIG:
        outs, _ = _adamw_big(f"adamw_{k}", lands[k], big[k], big_m[k], big_v[k])
        if k in ("w_gate", "w_up"):
            outs = [tr(o) for o in outs]
        res["grad"][k], res["delta"][k], res["m"][k], res["v"][k] = outs

    order = ["norm1_g", "w_in", "gmlp_ln_g", "gmlp_ln_b", "w_spatial", "b_spatial", "conv_w", "group_norm_g", "w_out",
             "norm2_g", "w_gate", "w_up", "w_down", "final_norm_g"]
    return (loss, grad_x, *[res["grad"][k] for k in order], *[res["delta"][k] for k in order],
            *[res["m"][k] for k in order], *[res["v"][k] for k in order])
```

```python
import functools
import math
import operator

import jax
import jax.numpy as jnp
from jax import lax
from jax.experimental import pallas as pl
from jax.experimental.pallas import tpu as pltpu

F32 = jnp.float32
BF16 = jnp.bfloat16
MESH = pl.DeviceIdType.MESH

N_DEV = 8
N_LAYERS = 2
HEADS = 8
BLK = 128
CHUNK = 64
HALO = 16
RMS_EPS = 1e-6
LN_EPS = 1e-5
ADAM_LR, ADAM_B1, ADAM_B2, ADAM_EPS, ADAM_WD, ADAM_STEP = 0.001, 0.9, 0.999, 1e-8, 0.01, 10
GELU_C = math.sqrt(2.0 / math.pi)
GELU_A = 0.044715

VMEM_LIMIT_V7X = 56 * 1024 * 1024
_TM = 1024
_TN = 1024
_TT = 1024
_TM_MIX = 256
_TM_NORM = 512


def _cparams(n_axes):
    return pltpu.CompilerParams(dimension_semantics=("arbitrary",) * n_axes, vmem_limit_bytes=VMEM_LIMIT_V7X)


def _sds(shape, dtype):
    return jax.ShapeDtypeStruct(tuple(shape), dtype)


def _place():
    return lax.axis_index("x"), lax.axis_index("y"), lax.axis_index("c")


def _index(place):
    return 4 * place[0] + 2 * place[1] + place[2]


class _Piece:
    def __init__(self, operands, out_shapes, aliases, n_sems, start, finish, mid1=None, mid2=None, vmem=(),
                 hooks=(0.6, 0.87)):
        self.operands, self.out_shapes, self.aliases, self.n_sems = list(operands), list(out_shapes), dict(aliases), n_sems
        self.vmem = list(vmem)
        self.hooks = hooks
        nothing = lambda ctx: None
        self.start, self.mid1, self.mid2, self.finish = start, mid1 or nothing, mid2 or nothing, finish


class _Ctx:
    def __init__(self, ins, outs, sems, offs):
        self.ins, self.outs, self.sems = ins, outs, sems
        self.o_in, self.o_out, self.o_send, self.o_recv, self.o_loc, self.o_vmem = offs

    def vmem(self, i):
        return self.sems[3 + self.o_vmem + i]

    def inp(self, i):
        return self.ins[self.o_in + i]

    def out(self, i):
        return self.outs[self.o_out + i]

    def send(self, k):
        return self.sems[0].at[self.o_send + k]

    def recv(self, k):
        return self.sems[1].at[self.o_recv + k]

    def local(self, k):
        return self.sems[2].at[self.o_loc + k]


class _Hosted:
    def __init__(self, pieces, n_in_before, n_out_before):
        self.pieces = [p for p in (pieces or []) if p is not None]
        self.operands, self.out_shapes, self.aliases, self.offs = [], [], {}, []
        counts, vmem = [0, 0, 0], []
        for p in self.pieces:
            self.offs.append((len(self.operands), len(self.out_shapes), *counts, len(vmem)))
            for i, j in p.aliases.items():
                self.aliases[n_in_before + len(self.operands) + i] = n_out_before + len(self.out_shapes) + j
            self.operands += p.operands
            self.out_shapes += p.out_shapes
            counts = [c + n for c, n in zip(counts, p.n_sems)]
            vmem += p.vmem
        hbm = pl.BlockSpec(memory_space=pl.ANY)
        self.in_specs = [hbm] * len(self.operands)
        self.out_specs = [hbm] * len(self.out_shapes)
        self.scratch = ([pltpu.SemaphoreType.DMA((max(c, 1),)) for c in counts] + vmem) if self.pieces else []

    def run(self, stage, ins, outs, sems):
        for p, offs in zip(self.pieces, self.offs):
            getattr(p, stage)(_Ctx(ins, outs, sems, offs))

    def wrap(self, grid, compute, ins, outs, sems):
        if not self.pieces:
            compute()
            return
        n_steps = math.prod(grid)
        lin = 0
        for ax, g in enumerate(grid):
            lin = lin * g + pl.program_id(ax)
        pl.when(lin == 0)(lambda: self.run("start", ins, outs, sems))
        compute()
        for stage, which in (("mid1", 0), ("mid2", 1)):
            for p, offs in zip(self.pieces, self.offs):
                at = min(n_steps - 1, int(p.hooks[which] * n_steps))
                pl.when(lin == at)(functools.partial(getattr(p, stage), _Ctx(ins, outs, sems, offs)))
        pl.when(lin == n_steps - 1)(lambda: self.run("finish", ins, outs, sems))


def _cols_view(width):
    return lambda ref, p: ref.at[:, pl.ds(pl.multiple_of(p * width, 128), width)]


def _rows_view(height):
    return lambda ref, p: ref.at[pl.ds(pl.multiple_of(p * height, 16), height), :]


def _cols_halves(rows, width, part, n_parts):
    hr = rows // n_parts // 2
    at = lambda h: pl.ds(part * 2 * hr + h * hr, hr)
    return (lambda ref, p, h: ref.at[at(h), pl.ds(pl.multiple_of(p * width, 128), width)],
            lambda ref, h: ref.at[at(h), :], 2)


def _rows_halves(height, part, n_parts):
    hh = height // n_parts // 2
    return (lambda ref, p, h: ref.at[pl.ds(pl.multiple_of(p * height + part * 2 * hh + h * hh, 16), hh), :],
            lambda ref, h: ref.at[pl.ds(part * 2 * hh + h * hh, hh), :], 2)


_SLOT_WHOLE = (lambda ref, p, h: ref.at[p], lambda ref, h: ref, 1)


def _ag_piece(specs):
    units = [(a, h) for a, s in enumerate(specs) for h in s[3]]

    def plan(ctx):
        x, y, c = _place()
        me, sib, xn, yn, dg = (x, y, c), (x, y, 1 - c), (1 - x, y, c), (x, 1 - y, c), (1 - x, 1 - y, c)

        def copy(u, k, block, to, from_shard=False):
            a, h = units[u]
            dst_of, src_of, _ = specs[a][2]
            dst = dst_of(ctx.out(a), _index(block), h)
            return pltpu.make_async_remote_copy(
                src_ref=src_of(ctx.inp(a), h) if from_shard else dst, dst_ref=dst, send_sem=ctx.send(7 * u + k),
                recv_sem=ctx.recv(7 * u + k), device_id=to, device_id_type=MESH)

        def local(u):
            a, h = units[u]
            dst_of, src_of, _ = specs[a][2]
            return pltpu.make_async_copy(src_of(ctx.inp(a), h), dst_of(ctx.out(a), _index(me), h), ctx.local(u))

        def relay(u):
            return copy(u, 3, xn, yn) if units[u][1] % 2 == 0 else copy(u, 3, yn, xn)

        return me, sib, xn, yn, dg, c, copy, local, relay

    def start(ctx):
        me, sib, xn, yn, dg, c, copy, local, relay = plan(ctx)
        for u in range(len(units)):
            local(u).start()
            for k, to in enumerate((sib, xn, yn)):
                copy(u, k, me, to, from_shard=True).start()

    def mid1(ctx):
        me, sib, xn, yn, dg, c, copy, local, relay = plan(ctx)
        for u in range(len(units)):
            copy(u, 1, xn, me).wait_recv()
            copy(u, 2, yn, me).wait_recv()
            relay(u).start()
            copy(u, 4, xn, sib).start()
            copy(u, 5, yn, sib).start()

    def mid2(ctx):
        me, sib, xn, yn, dg, c, copy, local, relay = plan(ctx)
        for u in range(len(units)):
            copy(u, 3, dg, me).wait_recv()
            copy(u, 6, dg, sib).start()

    def finish(ctx):
        me, sib, xn, yn, dg, c, copy, local, relay = plan(ctx)
        other = lambda place: (place[0], place[1], 1 - c)
        for u in range(len(units)):
            for k, block in ((0, sib), (4, other(xn)), (5, other(yn)), (6, other(dg))):
                copy(u, k, block, me).wait_recv()
        for u in range(len(units)):
            for k, to in enumerate((sib, xn, yn)):
                copy(u, k, me, to, from_shard=True).wait_send()
            relay(u).wait_send()
            for k, block in ((4, xn), (5, yn), (6, dg)):
                copy(u, k, block, sib).wait_send()
            local(u).wait()

    n_u = len(units)
    operands, aliases = [s[0] for s in specs], {}
    for a, spec in enumerate(specs):
        if spec[4] is not None:
            aliases[len(operands)] = a
            operands.append(spec[4])
    return _Piece(operands, [s[1] for s in specs], aliases, (7 * n_u, 7 * n_u, n_u), start, finish, mid1, mid2)


N_CHIPS = 4


def _rs_core_piece(specs):
    n = len(specs)

    def copies(ctx):
        x, y, c = _place()
        out = []
        for a in range(n):
            for q in range(N_CHIPS):
                out.append(pltpu.make_async_remote_copy(
                    src_ref=specs[a][2](ctx.inp(a), 2 * q + (1 - c)), dst_ref=ctx.out(a).at[q],
                    send_sem=ctx.send(N_CHIPS * a + q), recv_sem=ctx.recv(N_CHIPS * a + q), device_id=(x, y, 1 - c),
                    device_id_type=MESH))
        return out

    def start(ctx):
        for cp in copies(ctx):
            cp.start()

    def finish(ctx):
        for cp in copies(ctx):
            cp.wait_recv()
            cp.wait_send()

    return _Piece([s[0] for s in specs], [s[1] for s in specs], {}, (N_CHIPS * n, N_CHIPS * n, 0), start, finish)


def _rs_chip_piece(specs, layer):
    n = len(specs)
    hops = [(1, 0), (0, 1), (1, 1)]

    def copies(ctx):
        x, y, c = _place()
        mine = 2 * x + y
        out = []
        for a in range(n):
            first, landing, size = specs[a][2]
            rows, to = pl.ds(first, size), pl.ds(landing, size)
            sums, land = ctx.inp(a), ctx.out(a)
            out.append((pltpu.make_async_copy(sums.at[mine, rows], land.at[layer, mine, to], ctx.local(a)), None))
            for j, (dx, dy) in enumerate(hops):
                px, py = x ^ dx, y ^ dy
                peer = 2 * px + py
                send = pltpu.make_async_remote_copy(
                    src_ref=sums.at[peer, rows], dst_ref=land.at[layer, mine, to], send_sem=ctx.send(3 * a + j),
                    recv_sem=ctx.recv(3 * a + j), device_id=(px, py, c), device_id_type=MESH)
                recv = pltpu.make_async_remote_copy(
                    src_ref=sums.at[peer, rows], dst_ref=land.at[layer, peer, to], send_sem=ctx.send(3 * a + j),
                    recv_sem=ctx.recv(3 * a + j), device_id=(px, py, c), device_id_type=MESH)
                out.append((send, recv))
        return out

    def start(ctx):
        for send, _ in copies(ctx):
            send.start()

    def finish(ctx):
        for send, recv in copies(ctx):
            if recv is None:
                send.wait()
            else:
                recv.wait_recv()
                send.wait_send()

    operands, aliases = [s[0] for s in specs], {}
    for a, spec in enumerate(specs):
        if spec[3] is not None:
            aliases[len(operands)] = a
            operands.append(spec[3])
    return _Piece(operands, [s[1] for s in specs], aliases, (3 * n, 3 * n, n), start, finish)


def _all_reduce_piece(pack):
    r = pack.shape[0]
    half = r // 2

    def plan(ctx):
        x, y, c = _place()
        acc, got = ctx.vmem(0), ctx.vmem(1)
        mine = pl.ds(pl.multiple_of(c * half, 8), half)
        sib = (x, y, 1 - c)
        copies = [
            pltpu.make_async_remote_copy(src_ref=acc.at[0], dst_ref=got.at[0], send_sem=ctx.send(0), recv_sem=ctx.recv(0),
                                         device_id=sib, device_id_type=MESH),
            pltpu.make_async_remote_copy(src_ref=acc.at[1, mine], dst_ref=got.at[1, mine], send_sem=ctx.send(1),
                                         recv_sem=ctx.recv(1), device_id=(1 - x, y, c), device_id_type=MESH),
            pltpu.make_async_remote_copy(src_ref=acc.at[2, mine], dst_ref=got.at[2, mine], send_sem=ctx.send(2),
                                         recv_sem=ctx.recv(2), device_id=(x, 1 - y, c), device_id_type=MESH),
            pltpu.make_async_remote_copy(src_ref=acc.at[3, mine], dst_ref=acc.at[3, mine], send_sem=ctx.send(3),
                                         recv_sem=ctx.recv(3), device_id=sib, device_id_type=MESH),
        ]
        other = pl.ds(pl.multiple_of((1 - c) * half, 8), half)
        arrival = pltpu.make_async_remote_copy(src_ref=acc.at[3, other], dst_ref=acc.at[3, other], send_sem=ctx.send(3),
                                               recv_sem=ctx.recv(3), device_id=sib, device_id_type=MESH)
        return acc, got, mine, copies, arrival

    def start(ctx):
        acc, got, mine, copies, arrival = plan(ctx)
        load = pltpu.make_async_copy(ctx.inp(0), acc.at[0], ctx.local(0))
        load.start()
        load.wait()
        copies[0].start()

    def mid1(ctx):
        acc, got, mine, copies, arrival = plan(ctx)
        copies[0].wait()
        acc[1] = acc[0] + got[0]
        copies[1].start()

    def mid2(ctx):
        acc, got, mine, copies, arrival = plan(ctx)
        copies[1].wait()
        acc[2, mine] = acc[1, mine] + got[1, mine]
        copies[2].start()

    def finish(ctx):
        acc, got, mine, copies, arrival = plan(ctx)
        copies[2].wait()
        acc[3, mine] = acc[2, mine] + got[2, mine]
        copies[3].start()
        copies[3].wait_send()
        arrival.wait_recv()
        store = pltpu.make_async_copy(acc.at[3], ctx.out(0), ctx.local(0))
        store.start()
        store.wait()

    return _Piece([pack], [_sds(pack.shape, F32)], {}, (4, 4, 1), start, finish, mid1, mid2,
                  vmem=[pltpu.VMEM((4, r, 128), F32), pltpu.VMEM((3, r, 128), F32)], hooks=(0.25, 0.6))


def _chip_sums(name, grad, stage, by_cols, core):
    _, r, c = stage.shape
    tr = r
    while tr * c > 1024 * 1024 or r % tr or tr % 16:
        tr -= 16
    n_t = r // tr

    def body(core_ref, g_ref, s_ref, o_ref):
        o_ref[...] = (g_ref[...].astype(F32) + s_ref[...].astype(F32)).astype(BF16)

    if by_cols:
        gspec = pl.BlockSpec((tr, c), lambda q, i, core_ref: (i, 2 * q + core_ref[0]))
    else:
        gspec = pl.BlockSpec((tr, c), lambda q, i, core_ref: ((2 * q + core_ref[0]) * n_t + i, 0))
    sspec = pl.BlockSpec((None, tr, c), lambda q, i, core_ref: (q, i, 0))
    return pl.pallas_call(
        body, name=name, out_shape=_sds(stage.shape, BF16),
        grid_spec=pltpu.PrefetchScalarGridSpec(num_scalar_prefetch=1, grid=(N_CHIPS, n_t), in_specs=[gspec, sspec],
                                               out_specs=sspec),
        compiler_params=_cparams(2))(core, grad, stage)


def _call_hosting(body, name, grid, out_shapes, in_specs, out_specs, operands, scratch, comm):
    n_in, n_out, n_scr = len(operands), len(out_shapes), len(scratch)
    hosted = _Hosted(comm, n_in, n_out)
    n_ci, n_co = len(hosted.operands), len(hosted.out_shapes)

    def hosting_body(*refs):
        ins, rest = refs[:n_in], refs[n_in:]
        c_ins, rest = rest[:n_ci], rest[n_ci:]
        outs, rest = rest[:n_out], rest[n_out:]
        c_outs, rest = rest[:n_co], rest[n_co:]
        hosted.wrap(grid, lambda: body(*ins, *outs, *rest[:n_scr]), c_ins, c_outs, rest[n_scr:])

    res = pl.pallas_call(
        hosting_body, name=name, grid=grid, out_shape=tuple(list(out_shapes) + hosted.out_shapes),
        in_specs=list(in_specs) + hosted.in_specs, out_specs=tuple(list(out_specs) + hosted.out_specs),
        input_output_aliases=hosted.aliases, scratch_shapes=list(scratch) + hosted.scratch,
        compiler_params=_cparams(len(grid)))(*operands, *hosted.operands)
    return list(res[:n_out]), list(res[n_out:])


def _matmul(name, grid, nk, kaxis, pairs, dims, extras, outs, epilogue, sum_pairs, acc_shape, comm=None, split=None):
    n_p, n_e, n_o = len(pairs), len(extras), len(outs)
    n_acc = 0 if nk == 1 else (1 if sum_pairs else n_p)
    n_in = 2 * n_p + n_e
    hosted = _Hosted(comm, n_in, n_o)
    n_ci, n_co = len(hosted.operands), len(hosted.out_shapes)

    def body(*refs):
        a_refs = refs[0:2 * n_p:2]
        b_refs = refs[1:2 * n_p:2]
        e_refs = refs[2 * n_p:n_in]
        c_ins = refs[n_in:n_in + n_ci]
        o_refs = refs[n_in + n_ci:n_in + n_ci + n_o]
        c_outs = refs[n_in + n_ci + n_o:n_in + n_ci + n_o + n_co]
        acc_refs = refs[n_in + n_ci + n_o + n_co:n_in + n_ci + n_o + n_co + n_acc]
        sems = refs[n_in + n_ci + n_o + n_co + n_acc:]

        def dots():
            if sum_pairs and n_p > 1 and dims == NN:
                a_all = jnp.concatenate([a[...] for a in a_refs], axis=1)
                b_all = jnp.concatenate([b[...] for b in b_refs], axis=0)
                return [lax.dot_general(a_all, b_all, (dims, ((), ())), preferred_element_type=F32)]
            prods = [lax.dot_general(a[...], b[...], (dims, ((), ())), preferred_element_type=F32)
                     for a, b in zip(a_refs, b_refs)]
            if sum_pairs and n_p > 1:
                prods = [functools.reduce(operator.add, prods)]
            return prods

        def compute():
            if nk == 1 and split is not None:
                n_split, b_axis, n_row = split
                width = b_refs[0].shape[b_axis] // n_split
                height = a_refs[0].shape[0] // n_row
                for s in range(n_split):
                    cols = pl.ds(s * width, width)
                    for r in range(n_row):
                        rows = pl.ds(r * height, height)
                        epilogue([lax.dot_general(a[rows, :], b[cols, :] if b_axis == 0 else b[:, cols], (dims, ((), ())),
                                                  preferred_element_type=F32) for a, b in zip(a_refs, b_refs)],
                                 e_refs, o_refs, rows, cols)
                return
            if nk == 1:
                epilogue(dots(), e_refs, o_refs)
                return
            k = pl.program_id(kaxis)

            @pl.when(k == 0)
            def _():
                for acc, p in zip(acc_refs, dots()):
                    acc[...] = p

            if nk > 2:
                @pl.when((k > 0) & (k < nk - 1))
                def _():
                    for acc, p in zip(acc_refs, dots()):
                        acc[...] += p

            @pl.when(k == nk - 1)
            def _():
                epilogue([acc[...] + p for acc, p in zip(acc_refs, dots())], e_refs, o_refs)

        hosted.wrap(grid, compute, c_ins, c_outs, sems)

    operands, in_specs = [], []
    for a, a_spec, b, b_spec in pairs:
        operands += [a, b]
        in_specs += [a_spec, b_spec]
    for e, e_spec in extras:
        operands.append(e)
        in_specs.append(e_spec)
    res = pl.pallas_call(
        body, name=name, grid=grid,
        out_shape=tuple([o for o, _ in outs] + hosted.out_shapes),
        in_specs=in_specs + hosted.in_specs, out_specs=tuple([s for _, s in outs] + hosted.out_specs),
        input_output_aliases=hosted.aliases,
        scratch_shapes=[pltpu.VMEM(acc_shape, F32) for _ in range(n_acc)] + hosted.scratch,
        compiler_params=_cparams(len(grid)),
    )(*operands, *hosted.operands)
    return list(res[:n_o]), list(res[n_o:])


NN = ((1,), (0,))
NT = ((1,), (1,))
TN = ((0,), (0,))


def _tile(n, want):
    if n <= want:
        return n
    t = want // 128 * 128
    while n % t:
        t -= 128
    return t


def _silu_parts(g):
    s = 0.5 + 0.5 * jnp.tanh(0.5 * g)
    return s, g * s


def _mm_in(h, w_in, comm=None):
    t, d = h.shape
    n = w_in.shape[1]
    tm, tn = _tile(t, _TM), _tile(n, _TN)

    def epi(accs, e, o):
        o[0][...] = accs[0].astype(BF16)

    outs, couts = _matmul(
        "mm_in", (n // tn, t // tm), 1, None,
        [(h, pl.BlockSpec((tm, d), lambda j, i: (i, 0)), w_in, pl.BlockSpec((d, tn), lambda j, i: (0, j)))],
        NN, [], [(_sds((t, n), BF16), pl.BlockSpec((tm, tn), lambda j, i: (i, j)))], epi, True, None, comm)
    return outs[0], couts


def _mm_out(y, w_out, x, comm=None):
    t, m = y.shape
    d = w_out.shape[1]
    tm, tn = _tile(t, _TM), _tile(d, _TN)

    def epi(accs, e, o):
        o[0][...] = e[0][...] + accs[0]

    outs, couts = _matmul(
        "mm_out", (t // tm, d // tn), 1, None,
        [(y, pl.BlockSpec((tm, m), lambda i, j: (i, 0)), w_out, pl.BlockSpec((m, tn), lambda i, j: (0, j)))],
        NN, [(x, pl.BlockSpec((tm, tn), lambda i, j: (i, j)))],
        [(_sds((t, d), F32), pl.BlockSpec((tm, tn), lambda i, j: (i, j)))], epi, True, None, comm)
    return outs[0], couts


def _mm_swiglu(h2, wgt, wut, comm=None):
    t, d = h2.shape
    f = wgt.shape[0]
    tm, tn = _tile(t, 2 * _TM), _tile(f, 512)

    def epi(accs, e, o, rows, cols):
        g, u = accs
        s, sg = _silu_parts(g)
        o[0][rows, cols] = (sg * u).astype(BF16)
        o[1][rows, cols] = (u * (s + sg * (1.0 - s))).astype(BF16)
        o[2][rows, cols] = sg.astype(BF16)

    wspec = pl.BlockSpec((tn, d), lambda i, j: (j, 0))
    hspec = pl.BlockSpec((tm, d), lambda i, j: (i, 0))
    ospec = pl.BlockSpec((tm, tn), lambda i, j: (i, j))
    osh = _sds((t, f), BF16)
    outs, couts = _matmul("mm_swiglu", (t // tm, f // tn), 1, None, [(h2, hspec, wgt, wspec), (h2, hspec, wut, wspec)],
                          NT, [], [(osh, ospec)] * 3, epi, False, None, comm, split=(tn // 256, 0, 2))
    return outs, couts


def _mm_down(act, wd, x1, comm=None):
    t, f = act.shape
    d = wd.shape[1]
    tm, tn = _tile(t, _TM), _tile(d, _TN)
    nk = 2
    tk = f // nk

    def epi(accs, e, o):
        o[0][...] = e[0][...] + accs[0]

    outs, couts = _matmul(
        "mm_down", (t // tm, d // tn, nk), nk, 2,
        [(act, pl.BlockSpec((tm, tk), lambda i, j, k: (i, k)), wd, pl.BlockSpec((tk, tn), lambda i, j, k: (k, j)))],
        NN, [(x1, pl.BlockSpec((tm, tn), lambda i, j, k: (i, j)))],
        [(_sds((t, d), F32), pl.BlockSpec((tm, tn), lambda i, j, k: (i, j)))], epi, True, (tm, tn), comm)
    return outs[0], couts


def _mm_dact(dxb, wd, dact_dgate, dact_dup, comm=None):
    t, d = dxb.shape
    f = wd.shape[0]
    tm, tn = _tile(t, 2 * _TM), _tile(f, 512)

    def epi(accs, e, o, rows, cols):
        da = accs[0]
        o[0][rows, cols] = (da * e[0][rows, cols].astype(F32)).astype(BF16)
        o[1][rows, cols] = (da * e[1][rows, cols].astype(F32)).astype(BF16)

    bspec = pl.BlockSpec((tm, tn), lambda i, j: (i, j))
    osh = _sds((t, f), BF16)
    outs, couts = _matmul(
        "mm_dact", (t // tm, f // tn), 1, None,
        [(dxb, pl.BlockSpec((tm, d), lambda i, j: (i, 0)), wd, pl.BlockSpec((tn, d), lambda i, j: (j, 0)))],
        NT, [(dact_dgate, bspec), (dact_dup, bspec)], [(osh, bspec)] * 2, epi, True, None, comm, split=(tn // 256, 0, 2))
    return outs, couts


def _mm_dh2(dgate, dup, wgt, wut, comm=None):
    t, f = dgate.shape
    d = wgt.shape[1]
    tm, tn = _tile(t, _TM), _tile(d, _TN)
    nk = 4
    tk = f // nk

    def epi(accs, e, o):
        o[0][...] = accs[0]

    aspec = pl.BlockSpec((tm, tk), lambda i, j, k: (i, k))
    wspec = pl.BlockSpec((tk, tn), lambda i, j, k: (k, j))
    outs, couts = _matmul("mm_dh2", (t // tm, d // tn, nk), nk, 2, [(dgate, aspec, wgt, wspec), (dup, aspec, wut, wspec)],
                          NN, [], [(_sds((t, d), F32), pl.BlockSpec((tm, tn), lambda i, j, k: (i, j)))], epi, True,
                          (tm, tn), comm)
    return outs[0], couts


def _mm_dw(name, a_list, b, tmo, tno, comm=None, m_rows=None):
    t, m = a_list[0].shape
    start, m = (0, m) if m_rows is None else m_rows
    n = b.shape[1]
    tt = _tile(t, _TT)
    nk = t // tt
    tmo, tno = _tile(m, tmo), _tile(n, tno)
    first = start // tmo

    def epi(accs, e, o):
        for acc, out in zip(accs, o):
            out[...] = acc.astype(BF16)

    aspec = pl.BlockSpec((tt, tmo), lambda i, j, k: (k, first + i))
    bspec = pl.BlockSpec((tt, tno), lambda i, j, k: (k, j))
    ospec = pl.BlockSpec((tmo, tno), lambda i, j, k: (i, j))
    if nk == 1:
        return _matmul(name, (m // tmo, n // tno, 1), 1, None, [(a, aspec, b, bspec) for a in a_list], TN, [],
                       [(_sds((m, n), BF16), ospec)] * len(a_list), epi, False, None, comm)
    return _matmul(name, (m // tmo, n // tno, nk), nk, 2, [(a, aspec, b, bspec) for a in a_list], TN, [],
                   [(_sds((m, n), BF16), ospec)] * len(a_list), epi, False, (tmo, tno), comm)


def _mm_dy(dxb, w_out, comm=None):
    t, d = dxb.shape
    m = w_out.shape[0]
    tm, tn = _tile(t, _TM), _tile(m, _TN)

    def epi(accs, e, o):
        o[0][...] = accs[0].astype(BF16)

    outs, couts = _matmul(
        "mm_dy", (t // tm, m // tn), 1, None,
        [(dxb, pl.BlockSpec((tm, d), lambda i, j: (i, 0)), w_out, pl.BlockSpec((tn, d), lambda i, j: (j, 0)))], NT, [],
        [(_sds((t, m), BF16), pl.BlockSpec((tm, tn), lambda i, j: (i, j)))], epi, True, None, comm)
    return outs[0], couts


def _mm_dh(dz, w_in, comm=None):
    t, n = dz.shape
    d = w_in.shape[0]
    tm, tn = _tile(t, _TM), _tile(d, _TN)
    nk = 2
    tk = n // nk

    def epi(accs, e, o):
        o[0][...] = accs[0]

    outs, couts = _matmul(
        "mm_dh", (t // tm, d // tn, nk), nk, 2,
        [(dz, pl.BlockSpec((tm, tk), lambda i, j, k: (i, k)), w_in, pl.BlockSpec((tn, tk), lambda i, j, k: (j, k)))], NT,
        [], [(_sds((t, d), F32), pl.BlockSpec((tm, tn), lambda i, j, k: (i, j)))], epi, True, (tm, tn), comm)
    return outs[0], couts


def _rmsnorm_fwd(x, g, comm=None):
    t, d = x.shape
    tm = min(_TM_NORM, t)

    def body(x_ref, g_ref, o_ref):
        xv = x_ref[...]
        rs = lax.rsqrt(jnp.mean(xv * xv, axis=-1, keepdims=True) + RMS_EPS)
        o_ref[...] = (xv * rs * g_ref[...]).astype(BF16)

    outs, couts = _call_hosting(
        body, "rmsnorm_fwd", (t // tm,), [_sds((t, d), BF16)],
        [pl.BlockSpec((tm, d), lambda i: (i, 0)), pl.BlockSpec((1, d), lambda i: (0, 0))],
        [pl.BlockSpec((tm, d), lambda i: (i, 0))], [x, g], [], comm)
    return outs[0], couts


def _rmsnorm_bwd_math(xv, g, dh):
    rs = lax.rsqrt(jnp.mean(xv * xv, axis=-1, keepdims=True) + RMS_EPS)
    xh = xv * rs
    gd = dh * g
    dx = rs * (gd - xh * jnp.mean(gd * xh, axis=-1, keepdims=True))
    return dx, jnp.sum(dh * xh, axis=0, keepdims=True)


def _rmsnorm_bwd(x, g, dh, dres):
    t, d = x.shape
    tm = min(_TM_NORM, t)

    def body(x_ref, g_ref, dh_ref, dres_ref, dx_ref, dxb_ref, dg_ref):
        dx, dg = _rmsnorm_bwd_math(x_ref[...], g_ref[...], dh_ref[...])
        dx = dx + dres_ref[...]
        dx_ref[...] = dx
        dxb_ref[...] = dx.astype(BF16)

        @pl.when(pl.program_id(0) == 0)
        def _():
            dg_ref[...] = dg

        @pl.when(pl.program_id(0) > 0)
        def _():
            dg_ref[...] += dg

    row = pl.BlockSpec((tm, d), lambda i: (i, 0))
    vec = pl.BlockSpec((1, d), lambda i: (0, 0))
    return pl.pallas_call(
        body, name="rmsnorm_bwd", grid=(t // tm,),
        out_shape=(_sds((t, d), F32), _sds((t, d), BF16), _sds((1, d), F32)),
        in_specs=[row, vec, row, row], out_specs=(row, row, vec), compiler_params=_cparams(1))(x, g, dh, dres)


def _loss_head(x, g, target):
    t, d = x.shape
    tm = min(_TM_NORM, t)

    def body(x_ref, g_ref, t_ref, dx_ref, dxb_ref, dg_ref, loss_ref):
        xv, gv = x_ref[...], g_ref[...]
        rs = lax.rsqrt(jnp.mean(xv * xv, axis=-1, keepdims=True) + RMS_EPS)
        diff = xv * rs * gv - t_ref[...]
        part = 0.5 * jnp.sum(jnp.mean(diff * diff, axis=-1, keepdims=True), axis=0, keepdims=True)
        part = jnp.broadcast_to(part, (1, 128))
        dx, dg = _rmsnorm_bwd_math(xv, gv, diff * (1.0 / d))
        dx_ref[...] = dx
        dxb_ref[...] = dx.astype(BF16)

        @pl.when(pl.program_id(0) == 0)
        def _():
            dg_ref[...] = dg
            loss_ref[...] = part

        @pl.when(pl.program_id(0) > 0)
        def _():
            dg_ref[...] += dg
            loss_ref[...] += part

    row = pl.BlockSpec((tm, d), lambda i: (i, 0))
    vec = pl.BlockSpec((1, d), lambda i: (0, 0))
    return pl.pallas_call(
        body, name="loss_head", grid=(t // tm,),
        out_shape=(_sds((t, d), F32), _sds((t, d), BF16), _sds((1, d), F32), _sds((1, 128), F32)),
        in_specs=[row, vec, row], out_specs=(row, row, vec, pl.BlockSpec((1, 128), lambda i: (0, 0))),
        compiler_params=_cparams(1))(x, g, target)


def _gelu(x):
    th = jnp.tanh(GELU_C * (x + GELU_A * x * x * x))
    return 0.5 * x * (1.0 + th), th


def _gelu_grad(x, th):
    return 0.5 * (1.0 + th) + 0.5 * x * (1.0 - th * th) * GELU_C * (1.0 + 3.0 * GELU_A * x * x)


def _masked_ws(ws_ref, h):
    i = lax.broadcasted_iota(jnp.int32, (BLK, BLK), 0) // CHUNK
    j = lax.broadcasted_iota(jnp.int32, (BLK, BLK), 1) // CHUNK
    return jnp.where(j <= i, ws_ref[h], 0.0)


def _shift_down(q, n, first_rows):
    rolled = pltpu.roll(q, n, 0)
    row = lax.broadcasted_iota(jnp.int32, q.shape, 0)
    for r, val in enumerate(first_rows):
        rolled = jnp.where(row == r, val, rolled)
    return rolled


def _shift_up(q, n, last_rows):
    tm = q.shape[0]
    rolled = pltpu.roll(q, tm - n, 0)
    row = lax.broadcasted_iota(jnp.int32, q.shape, 0)
    for r, val in enumerate(last_rows):
        rolled = jnp.where(row == tm - n + r, val, rolled)
    return rolled


def _mixer_specs(t, a, tm):
    hb = tm // HALO
    last = t // HALO - 1
    tile = pl.BlockSpec((tm, 5 * a), lambda i: (i, 0))
    prev = [pl.BlockSpec((HALO, a), functools.partial(lambda i, col: (jnp.maximum(i * hb - 1, 0), col), col=col))
            for col in (3, 4)]
    nxt = [pl.BlockSpec((HALO, a), functools.partial(lambda i, col: (jnp.minimum((i + 1) * hb, last), col), col=col))
           for col in (2, 3, 4)]
    return tile, prev, nxt


def _group_a_fwd(zu, zv, lng, lnb, ws_ref, bb_ref, mixed_ref, vln_ref):
    u, thu = _gelu(zu)
    v, thv = _gelu(zv)
    mu = jnp.mean(v, axis=-1, keepdims=True)
    vc = v - mu
    rs = lax.rsqrt(jnp.mean(vc * vc, axis=-1, keepdims=True) + LN_EPS)
    vhat = vc * rs
    vln_ref[...] = vhat * lng + lnb
    tm, a = zu.shape
    hd = a // HEADS
    for h in range(HEADS):
        w = _masked_ws(ws_ref, h).astype(BF16)
        for b in range(tm // BLK):
            rows, cols = pl.ds(b * BLK, BLK), pl.ds(h * hd, hd)
            mixed_ref[rows, cols] = jnp.dot(w, vln_ref[rows, cols].astype(BF16), preferred_element_type=F32) + bb_ref[h]
    return u, thu, thv, rs, vhat


def _mixer_fwd(z, ln_g, ln_b, w_spatial, bb, conv_w, gg, comm=None):
    t = z.shape[0]
    a = z.shape[1] // 5
    tm = min(_TM_MIX, t)
    tile, prev, _ = _mixer_specs(t, a, tm)

    def body(z_ref, pc_ref, ph_ref, lng_ref, lnb_ref, ws_ref, bb_ref, cw_ref, gg_ref, y_ref, mixed_ref, vln_ref):
        i = pl.program_id(0)
        zu = z_ref[:, 0:a].astype(F32)
        zv = z_ref[:, a:2 * a].astype(F32)
        u, _, _, _, _ = _group_a_fwd(zu, zv, lng_ref[...], lnb_ref[...], ws_ref, bb_ref, mixed_ref, vln_ref)
        ya = u * mixed_ref[...]
        ra = lax.rsqrt(jnp.mean(ya * ya, axis=-1, keepdims=True) + RMS_EPS)
        y_ref[:, 0:a] = (ya * ra * gg_ref[:, 0:a]).astype(BF16)

        zb = z_ref[:, 2 * a:3 * a].astype(F32)
        q = z_ref[:, 3 * a:4 * a].astype(F32) * z_ref[:, 4 * a:5 * a].astype(F32)
        qp = jnp.where(i > 0, pc_ref[...].astype(F32) * ph_ref[...].astype(F32), 0.0)
        qm1 = _shift_down(q, 1, [qp[HALO - 1:HALO]])
        qm2 = _shift_down(q, 2, [qp[HALO - 2:HALO - 1], qp[HALO - 1:HALO]])
        cv = cw_ref[0:1, :] * qm2 + cw_ref[1:2, :] * qm1 + cw_ref[2:3, :] * q
        yb = zb * cv
        rb = lax.rsqrt(jnp.mean(yb * yb, axis=-1, keepdims=True) + RMS_EPS)
        y_ref[:, a:2 * a] = (yb * rb * gg_ref[:, a:2 * a]).astype(BF16)

    full = lambda shape: pl.BlockSpec(shape, lambda i: (0,) * len(shape))
    outs, couts = _call_hosting(
        body, "mixer_fwd", (t // tm,), [_sds((t, 2 * a), BF16)],
        [tile, *prev, full((1, a)), full((1, a)), full(w_spatial.shape), full(bb.shape), full(conv_w.shape),
         full((1, 2 * a))],
        [pl.BlockSpec((tm, 2 * a), lambda i: (i, 0))], [z, z, z, ln_g, ln_b, w_spatial, bb, conv_w, gg],
        [pltpu.VMEM((tm, a), F32), pltpu.VMEM((tm, a), F32)], comm)
    return outs[0], couts


def _mixer_bwd(z, dy, ln_g, ln_b, w_spatial, bb, conv_w, gg, comm=None):
    t = z.shape[0]
    a = z.shape[1] // 5
    hd = a // HEADS
    tm = min(_TM_MIX, t)
    n_tiles = t // tm
    tile, prev, nxt = _mixer_specs(t, a, tm)
    hb = tm // HALO
    dy_tile = pl.BlockSpec((tm, 2 * a), lambda i: (i, 0))
    dy_next = pl.BlockSpec((HALO, a), lambda i: (jnp.minimum((i + 1) * hb, t // HALO - 1), 1))

    def body(z_ref, pc_ref, ph_ref, nb_ref, nc_ref, nh_ref, dy_ref, ndy_ref, lng_ref, lnb_ref, ws_ref, bb_ref, cw_ref,
             gg_ref, dz_ref, dlng_ref, dlnb_ref, dws_ref, dbb_ref, dcw_ref, dgg_ref, mixed_ref, vln_ref, dmix_ref,
             dvln_ref):
        i = pl.program_id(0)

        @pl.when(i == 0)
        def _():
            for ref in (dlng_ref, dlnb_ref, dws_ref, dbb_ref, dcw_ref, dgg_ref):
                ref[...] = jnp.zeros(ref.shape, F32)

        lng = lng_ref[...]
        zu = z_ref[:, 0:a].astype(F32)
        zv = z_ref[:, a:2 * a].astype(F32)
        u, thu, thv, rs, vhat = _group_a_fwd(zu, zv, lng, lnb_ref[...], ws_ref, bb_ref, mixed_ref, vln_ref)
        mixed = mixed_ref[...]
        ya = u * mixed
        ra = lax.rsqrt(jnp.mean(ya * ya, axis=-1, keepdims=True) + RMS_EPS)
        da = dy_ref[:, 0:a].astype(F32)
        yah = ya * ra
        dgg_ref[:, 0:a] += jnp.sum(da * yah, axis=0, keepdims=True)
        ga = da * gg_ref[:, 0:a]
        dya = ra * (ga - yah * jnp.mean(ga * yah, axis=-1, keepdims=True))
        dz_ref[:, 0:a] = (dya * mixed * _gelu_grad(zu, thu)).astype(BF16)
        dmix_ref[...] = dya * u
        for h in range(HEADS):
            w = _masked_ws(ws_ref, h).astype(BF16)
            dw = jnp.zeros((BLK, BLK), F32)
            db = jnp.zeros((BLK, hd), F32)
            for b in range(tm // BLK):
                rows, cols = pl.ds(b * BLK, BLK), pl.ds(h * hd, hd)
                dm = dmix_ref[rows, cols]
                dmb = dm.astype(BF16)
                db = db + dm
                dw = dw + lax.dot_general(dmb, vln_ref[rows, cols].astype(BF16), (NT, ((), ())),
                                          preferred_element_type=F32)
                dvln_ref[rows, cols] = lax.dot_general(w, dmb, (TN, ((), ())), preferred_element_type=F32)
            dws_ref[h] += dw
            dbb_ref[h] += db
        dvln = dvln_ref[...]
        dlng_ref[...] += jnp.sum(dvln * vhat, axis=0, keepdims=True)
        dlnb_ref[...] += jnp.sum(dvln, axis=0, keepdims=True)
        dvh = dvln * lng
        dv = rs * (dvh - jnp.mean(dvh, axis=-1, keepdims=True) - vhat * jnp.mean(dvh * vhat, axis=-1, keepdims=True))
        dz_ref[:, a:2 * a] = (dv * _gelu_grad(zv, thv)).astype(BF16)

        w0, w1, w2 = cw_ref[0:1, :], cw_ref[1:2, :], cw_ref[2:3, :]
        ggb = gg_ref[:, a:2 * a]
        zb = z_ref[:, 2 * a:3 * a].astype(F32)
        zc = z_ref[:, 3 * a:4 * a].astype(F32)
        zh = z_ref[:, 4 * a:5 * a].astype(F32)
        q = zc * zh
        qp = jnp.where(i > 0, pc_ref[...].astype(F32) * ph_ref[...].astype(F32), 0.0)
        qm1 = _shift_down(q, 1, [qp[HALO - 1:HALO]])
        qm2 = _shift_down(q, 2, [qp[HALO - 2:HALO - 1], qp[HALO - 1:HALO]])
        cv = w0 * qm2 + w1 * qm1 + w2 * q

        def conv_out_grad(zb_, cv_, dout_):
            yb = zb_ * cv_
            rb = lax.rsqrt(jnp.mean(yb * yb, axis=-1, keepdims=True) + RMS_EPS)
            ybh = yb * rb
            gb = dout_ * ggb
            dyb = rb * (gb - ybh * jnp.mean(gb * ybh, axis=-1, keepdims=True))
            return dyb * zb_, dyb * cv_, ybh

        db_out = dy_ref[:, a:2 * a].astype(F32)
        g, dzb, ybh = conv_out_grad(zb, cv, db_out)
        dgg_ref[:, a:2 * a] += jnp.sum(db_out * ybh, axis=0, keepdims=True)
        dz_ref[:, 2 * a:3 * a] = dzb.astype(BF16)
        qn = nc_ref[...].astype(F32) * nh_ref[...].astype(F32)
        zbn = nb_ref[...].astype(F32)
        cvn = w0 * _shift_down(qn, 2, [q[tm - 2:tm - 1], q[tm - 1:tm]]) + w1 * _shift_down(qn, 1, [q[tm - 1:tm]]) + w2 * qn
        gn, _, _ = conv_out_grad(zbn, cvn, ndy_ref[...].astype(F32))
        gn = jnp.where(i < n_tiles - 1, gn, 0.0)
        dq = w2 * g + w1 * _shift_up(g, 1, [gn[0:1]]) + w0 * _shift_up(g, 2, [gn[0:1], gn[1:2]])
        dz_ref[:, 3 * a:4 * a] = (dq * zh).astype(BF16)
        dz_ref[:, 4 * a:5 * a] = (dq * zc).astype(BF16)
        dcw_ref[0:1, :] += jnp.sum(g * qm2, axis=0, keepdims=True)
        dcw_ref[1:2, :] += jnp.sum(g * qm1, axis=0, keepdims=True)
        dcw_ref[2:3, :] += jnp.sum(g * q, axis=0, keepdims=True)

        @pl.when(i == n_tiles - 1)
        def _():
            for h in range(HEADS):
                dbb_ref[h] = jnp.broadcast_to(jnp.sum(dbb_ref[h], axis=1, keepdims=True), (BLK, hd))
                dws_ref[h] = _masked_ws(dws_ref, h)

    full = lambda shape: pl.BlockSpec(tuple(shape), lambda i: (0,) * len(shape))
    out_shapes = (_sds((t, 5 * a), BF16), _sds((1, a), F32), _sds((1, a), F32), _sds(w_spatial.shape, F32),
                  _sds(bb.shape, F32), _sds((8, a), F32), _sds((1, 2 * a), F32))
    return _call_hosting(
        body, "mixer_bwd", (n_tiles,), out_shapes,
        [tile, *prev, *nxt, dy_tile, dy_next, full((1, a)), full((1, a)), full(w_spatial.shape), full(bb.shape),
         full(conv_w.shape), full((1, 2 * a))],
        [tile, *[full(s.shape) for s in out_shapes[1:]]], [z, z, z, z, z, z, dy, dy, ln_g, ln_b, w_spatial, bb, conv_w, gg],
        [pltpu.VMEM((tm, a), F32)] * 4, comm)


def _all_reduce_small(pack, comm=None):
    r = pack.shape[0]
    hosted = _Hosted(comm, 1, 1)
    n_ci, n_co = len(hosted.operands), len(hosted.out_shapes)

    def body(*refs):
        in_ref, c_ins, out_ref, c_outs = refs[0], refs[1:1 + n_ci], refs[1 + n_ci], refs[2 + n_ci:2 + n_ci + n_co]
        acc_ref, recv_ref, send_sems, recv_sems = refs[2 + n_ci + n_co:6 + n_ci + n_co]
        sems = refs[6 + n_ci + n_co:]
        hosted.run("start", c_ins, c_outs, sems)
        x, y, c = _place()
        partners = [(x, y, 1 - c), (1 - x, y, c), (x, 1 - y, c)]
        acc_ref[0] = in_ref[...]
        for s, partner in enumerate(partners):
            cp = pltpu.make_async_remote_copy(
                src_ref=acc_ref.at[s], dst_ref=recv_ref.at[s], send_sem=send_sems.at[s], recv_sem=recv_sems.at[s],
                device_id=partner, device_id_type=MESH)
            cp.start()
            cp.wait()
            if s < 2:
                acc_ref[s + 1] = acc_ref[s] + recv_ref[s]
            else:
                out_ref[...] = acc_ref[s] + recv_ref[s]
        for stage in ("mid1", "mid2", "finish"):
            hosted.run(stage, c_ins, c_outs, sems)

    vmem = pl.BlockSpec(memory_space=pltpu.VMEM)
    res = pl.pallas_call(
        body, name="all_reduce_small", out_shape=tuple([_sds(pack.shape, F32)] + hosted.out_shapes),
        in_specs=[vmem] + hosted.in_specs, out_specs=tuple([vmem] + hosted.out_specs),
        input_output_aliases=hosted.aliases,
        scratch_shapes=[pltpu.VMEM((3, r, 128), F32), pltpu.VMEM((3, r, 128), F32), pltpu.SemaphoreType.DMA((3,)),
                        pltpu.SemaphoreType.DMA((3,))] + hosted.scratch,
        compiler_params=pltpu.CompilerParams(vmem_limit_bytes=VMEM_LIMIT_V7X),
    )(pack, *hosted.operands)
    return res[0], list(res[1:])


def _adamw_math(w, g, m, v):
    m = ADAM_B1 * m + (1.0 - ADAM_B1) * g
    v = ADAM_B2 * v + (1.0 - ADAM_B2) * (g * g)
    m_hat = m / (1.0 - ADAM_B1 ** ADAM_STEP)
    v_hat = v / (1.0 - ADAM_B2 ** ADAM_STEP)
    delta = -ADAM_LR * (m_hat / (jnp.sqrt(v_hat) + ADAM_EPS) + ADAM_WD * w)
    return delta, m, v


def _adamw_big(name, land, w, m, v, comm=None):
    nl, n_slots, r, c = land.shape
    tr = max(8, min(r, (256 * 640) // c // 8 * 8))
    while r % tr:
        tr -= 8
    grid = (nl, r // tr)
    hosted = _Hosted(comm, 4, 4)
    n_ci, n_co = len(hosted.operands), len(hosted.out_shapes)

    def body(*refs):
        land_ref, w_ref, m_ref, v_ref = refs[:4]
        c_ins = refs[4:4 + n_ci]
        g_out, d_out, m_out, v_out = refs[4 + n_ci:8 + n_ci]
        c_outs = refs[8 + n_ci:8 + n_ci + n_co]
        sems = refs[8 + n_ci + n_co:]

        def compute():
            g = land_ref[0].astype(F32)
            for s in range(1, n_slots):
                g = g + land_ref[s].astype(F32)
            delta, mn, vn = _adamw_math(w_ref[...], g, m_ref[...], v_ref[...])
            g_out[...] = g
            d_out[...] = delta
            m_out[...] = mn
            v_out[...] = vn

        hosted.wrap(grid, compute, c_ins, c_outs, sems)

    blk = pl.BlockSpec((None, tr, c), lambda l, i: (l, i, 0))
    res = pl.pallas_call(
        body, name=name, grid=grid, out_shape=tuple([_sds((nl, r, c), F32)] * 4 + hosted.out_shapes),
        in_specs=[pl.BlockSpec((None, n_slots, tr, c), lambda l, i: (l, 0, i, 0)), blk, blk, blk] + hosted.in_specs,
        out_specs=tuple([blk] * 4 + hosted.out_specs), input_output_aliases=hosted.aliases,
        scratch_shapes=hosted.scratch, compiler_params=_cparams(2))(land, w, m, v, *hosted.operands)
    return list(res[:4]), list(res[4:])


def _adamw_small(gs, ws, ms, vs):
    n = len(gs)

    def body(*refs):
        g_refs, w_refs, m_refs, v_refs = refs[:n], refs[n:2 * n], refs[2 * n:3 * n], refs[3 * n:4 * n]
        d_outs, m_outs, v_outs = refs[4 * n:5 * n], refs[5 * n:6 * n], refs[6 * n:7 * n]
        for i in range(n):
            delta, mn, vn = _adamw_math(w_refs[i][...], g_refs[i][...], m_refs[i][...], v_refs[i][...])
            d_outs[i][...] = delta
            m_outs[i][...] = mn
            v_outs[i][...] = vn

    shapes = [_sds(g.shape, F32) for g in gs]
    res = pl.pallas_call(body, name="adamw_small", out_shape=tuple(shapes * 3),
                         compiler_params=pltpu.CompilerParams(vmem_limit_bytes=VMEM_LIMIT_V7X))(*gs, *ws, *ms, *vs)
    return list(res[:n]), list(res[n:2 * n]), list(res[2 * n:])


def _rows(a):
    return a.reshape(-1, 128)


BIG = ["w_in", "w_out", "w_gate", "w_up", "w_down"]
AG_HOSTS = {
    ("norm1", 0): [("w_in", 0), ("conv_w", 0)],
    ("mm_in", 0): [("w_out", 0), ("w_gate", 0, 0, 2)], ("mixer", 0): [("w_gate", 0, 1, 2)],
    ("mm_out", 0): [("w_up", 0, 0, 2)], ("norm2", 0): [("w_up", 0, 1, 2)],
    ("mm_swiglu", 0): [("w_down", 0), ("w_in", 1)], ("mm_down", 0): [("w_out", 1), ("w_gate", 1, 0, 2)],
    ("mm_in", 1): [("w_gate", 1, 1, 2)], ("mixer", 1): [("w_up", 1, 0, 2)], ("mm_out", 1): [("w_up", 1, 1, 2)],
    ("mm_swiglu", 1): [("w_down", 1)],
}


def kernel(x, norm1_g, w_in, gmlp_ln_g, gmlp_ln_b, w_spatial, b_spatial, conv_w, group_norm_g, w_out, norm2_g, w_gate, w_up, w_down, final_norm_g, loss_target, m_norm1_g, m_w_in, m_gmlp_ln_g, m_gmlp_ln_b, m_w_spatial, m_b_spatial, m_conv_w, m_group_norm_g, m_w_out, m_norm2_g, m_w_gate, m_w_up, m_w_down, m_final_norm_g, v_norm1_g, v_w_in, v_gmlp_ln_g, v_gmlp_ln_b, v_w_spatial, v_b_spatial, v_conv_w, v_group_norm_g, v_w_out, v_norm2_g, v_w_gate, v_w_up, v_w_down, v_final_norm_g):
    nl = N_LAYERS
    t, d = x.shape[1], x.shape[2]
    a = d // 2
    hd = a // HEADS
    xin = x.reshape(t, d)
    target = loss_target.reshape(t, d)
    me = _index(_place())

    tr = lambda w: jnp.transpose(w, (0, 2, 1))
    big = {"w_in": w_in, "w_out": w_out, "w_gate": tr(w_gate), "w_up": tr(w_up), "w_down": w_down}
    big_m = {"w_in": m_w_in, "w_out": m_w_out, "w_gate": tr(m_w_gate), "w_up": tr(m_w_up), "w_down": m_w_down}
    big_v = {"w_in": v_w_in, "w_out": v_w_out, "w_gate": tr(v_w_gate), "w_up": tr(v_w_up), "w_down": v_w_down}
    block = {k: big[k].shape[1:] for k in BIG}
    view = {k: _cols_view(block[k][1]) if k == "w_in" else _rows_view(block[k][0]) for k in BIG}
    full_shape = {k: (block[k][0], N_DEV * block[k][1]) if k == "w_in" else (N_DEV * block[k][0], block[k][1])
                  for k in BIG}

    weights = {}
    shards = {(k, l): big[k][l].astype(BF16) for k in BIG for l in range(nl)}

    def ag_spec(k, l, part=0, n_parts=1):
        if k == "conv_w":
            return (conv_w, _sds((N_DEV, *conv_w.shape), F32), _SLOT_WHOLE, (0,), None)
        halves = (_cols_halves(*block[k], part, n_parts) if k == "w_in" else _rows_halves(block[k][0], part, n_parts))
        return (shards[(k, l)], _sds(full_shape[k], BF16), halves, (0, 1), weights.get((k, l)))

    bb = jnp.broadcast_to(b_spatial[..., None], (nl, HEADS, BLK, hd))

    def hosted(name, l):
        keys = AG_HOSTS.get((name, l), [])
        return keys, ([_ag_piece([ag_spec(*key) for key in keys])] if keys else None)

    def landed(keys, couts):
        for key, arr in zip(keys, couts):
            weights[key[:2]] = arr

    saved = []
    xl = xin
    for l in range(nl):
        keys, comm = hosted("norm1", l)
        h, couts = _rmsnorm_fwd(xl, norm1_g[l:l + 1], comm)
        landed(keys, couts)
        if l == 0:
            conv_full = jnp.transpose(weights[("conv_w", 0)], (1, 2, 0, 3)).reshape(nl, 3, a)
        keys, comm = hosted("mm_in", l)
        z, couts = _mm_in(h, weights[("w_in", l)], comm)
        landed(keys, couts)
        keys, comm = hosted("mixer", l)
        y, couts = _mixer_fwd(z, gmlp_ln_g[l:l + 1], gmlp_ln_b[l:l + 1], w_spatial[l], bb[l], conv_full[l],
                              group_norm_g[l:l + 1], comm)
        landed(keys, couts)
        keys, comm = hosted("mm_out", l)
        x1, couts = _mm_out(y, weights[("w_out", l)], xl, comm)
        landed(keys, couts)
        keys, comm = hosted("norm2", l)
        h2, couts = _rmsnorm_fwd(x1, norm2_g[l:l + 1], comm)
        landed(keys, couts)
        keys, comm = hosted("mm_swiglu", l)
        (act, dact_dgate, dact_dup), couts = _mm_swiglu(h2, weights[("w_gate", l)], weights[("w_up", l)], comm)
        landed(keys, couts)
        keys, comm = hosted("mm_down", l)
        x2, couts = _mm_down(act, weights[("w_down", l)], x1, comm)
        landed(keys, couts)
        saved.append(dict(x=xl, h=h, z=z, y=y, x1=x1, h2=h2, dact_dgate=dact_dgate, dact_dup=dact_dup, act=act))
        xl = x2

    dx, dxb, d_final_g, loss_part = _loss_head(xl, final_norm_g.reshape(1, d), target)
    small = [None] * nl
    core = lax.axis_index("c").astype(jnp.int32).reshape(1)
    in_rows = block["w_in"][0]
    part_of = {"w_in_a": ("w_in", 0), "w_in_b": ("w_in", 3 * in_rows // 4)}
    block["w_in_a"], block["w_in_b"] = (3 * in_rows // 4, block["w_in"][1]), (in_rows // 4, block["w_in"][1])
    for k in part_of:
        view[k] = view["w_in"]
    stage_shape = {k: _sds((N_CHIPS, *block[k]), BF16) for k in block}
    land_shape = {k: _sds((nl, N_CHIPS, *block[k]), BF16) for k in BIG}
    grads = [dict() for _ in range(nl)]
    stages = [dict() for _ in range(nl)]
    sums = [dict() for _ in range(nl)]
    lands = {k: None for k in BIG}

    def core_job(l, keys):
        def sink(outs):
            stages[l].update(zip(keys, outs))
        return _rs_core_piece([(grads[l][k], stage_shape[k], view[k]) for k in keys]), sink

    def chip_job(l, items):
        keys = [part_of.get(item[0], (item[0], 0))[0] for item in items]

        def rows(k, p0, p1, n_parts):
            per = block[k][0] // n_parts
            landing = part_of.get(k, (k, 0))[1]
            return (p0 * per, landing + p0 * per, (p1 - p0) * per)

        def sink(outs):
            lands.update(zip(keys, outs))
        return _rs_chip_piece([(sums[l][k], land_shape[key], rows(k, p0, p1, n_parts), lands[key])
                               for key, (k, p0, p1, n_parts) in zip(keys, items)], l), sink

    def add_up(l, keys):
        for k in keys:
            sums[l][k] = _chip_sums(f"chip_sums_{k}", grads[l][k], stages[l][k], k.startswith("w_in"), core)

    def host(*jobs):
        def deliver(couts):
            i = 0
            for piece, sink in jobs:
                n_out = len(piece.out_shapes)
                sink(couts[i:i + n_out])
                i += n_out
        return [piece for piece, _ in jobs], deliver

    whole = lambda k: (k, 0, 1, 1)
    rep = ["norm1_g", "gmlp_ln_g", "gmlp_ln_b", "w_spatial", "b_spatial", "group_norm_g", "norm2_g"]
    rep_w = dict(norm1_g=norm1_g, gmlp_ln_g=gmlp_ln_g, gmlp_ln_b=gmlp_ln_b, w_spatial=w_spatial, b_spatial=b_spatial,
                 group_norm_g=group_norm_g, norm2_g=norm2_g)
    rep_m = dict(norm1_g=m_norm1_g, gmlp_ln_g=m_gmlp_ln_g, gmlp_ln_b=m_gmlp_ln_b, w_spatial=m_w_spatial,
                 b_spatial=m_b_spatial, group_norm_g=m_group_norm_g, norm2_g=m_norm2_g)
    rep_v = dict(norm1_g=v_norm1_g, gmlp_ln_g=v_gmlp_ln_g, gmlp_ln_b=v_gmlp_ln_b, w_spatial=v_w_spatial,
                 b_spatial=v_b_spatial, group_norm_g=v_group_norm_g, norm2_g=v_norm2_g)

    def small_grad_parts():
        parts = [_rows(jnp.stack([small[l][k].reshape(rep_w[k].shape[1:]) for l in range(nl)])) for k in rep]
        parts.append(_rows(d_final_g))
        parts.append(_rows(jnp.stack([small[l]["conv_w"] for l in range(nl)])))
        parts.append(jnp.broadcast_to(loss_part, (8, 128)))
        rows = sum(p.shape[0] for p in parts)
        parts.append(jnp.zeros((-rows % 16, 128), F32))
        return parts

    for l in reversed(range(nl)):
        s = saved[l]
        wi, wo, wgt, wut, wd = [weights[(k, l)] for k in BIG]
        later = l + 1 < nl
        comm, deliver = host(chip_job(l + 1, [("w_in", 0, 1, 2)])) if later else host()
        (grads[l]["w_down"],), couts = _mm_dw("mm_dw_down", [s["act"]], dxb, 2816, 1024, comm)
        deliver(couts)
        comm, deliver = (host(core_job(l, ["w_down"]), chip_job(l + 1, [("w_in", 1, 2, 2)])) if later
                         else host(core_job(l, ["w_down"])))
        (dgate, dup), couts = _mm_dact(dxb, wd, s["dact_dgate"], s["dact_dup"], comm)
        deliver(couts)
        add_up(l, ["w_down"])
        comm, deliver = host(chip_job(l, [("w_down", 0, 3, 4)]))
        (grads[l]["w_gate"],), couts = _mm_dw("mm_dw_gate", [dgate], s["h2"], 2816, 1024, comm)
        deliver(couts)
        comm, deliver = host(chip_job(l, [("w_down", 3, 4, 4)]), core_job(l, ["w_gate"]))
        (grads[l]["w_up"],), couts = _mm_dw("mm_dw_up", [dup], s["h2"], 2816, 1024, comm)
        deliver(couts)
        add_up(l, ["w_gate"])
        comm, deliver = host(chip_job(l, [whole("w_gate")]), core_job(l, ["w_up"]))
        dh2, couts = _mm_dh2(dgate, dup, wgt, wut, comm)
        deliver(couts)
        add_up(l, ["w_up"])
        dx1, dx1b, d_n2 = _rmsnorm_bwd(s["x1"], norm2_g[l:l + 1], dh2, dx)
        comm, deliver = host(chip_job(l, [("w_up", 0, 1, 4)]))
        dy, couts = _mm_dy(dx1b, wo, comm)
        deliver(couts)
        comm, deliver = host(chip_job(l, [("w_up", 1, 2, 4)]))
        (grads[l]["w_out"],), couts = _mm_dw("mm_dw_out", [s["y"]], dx1b, 1024, 1024, comm)
        deliver(couts)
        comm, deliver = host(chip_job(l, [("w_up", 2, 4, 4)]), core_job(l, ["w_out"]))
        (dz, d_lng, d_lnb, d_ws, d_bb, d_cw, d_gg), couts = _mixer_bwd(
            s["z"], dy, gmlp_ln_g[l:l + 1], gmlp_ln_b[l:l + 1], w_spatial[l], bb[l], conv_full[l], group_norm_g[l:l + 1],
            comm)
        deliver(couts)
        add_up(l, ["w_out"])
        small[l] = dict(norm1_g=jnp.zeros((1, d), F32), gmlp_ln_g=d_lng, gmlp_ln_b=d_lnb, w_spatial=d_ws,
                        b_spatial=d_bb[:, :, 0], group_norm_g=d_gg, norm2_g=d_n2, conv_w=d_cw[0:3])
        if l > 0:
            comm, deliver = host(chip_job(l, [whole("w_out")]))
            (grads[l]["w_in"],), couts = _mm_dw("mm_dw_in", [s["h"]], dz, 2048, 1024, comm)
            deliver(couts)
            comm, deliver = host(core_job(l, ["w_in"]))
            dh, couts = _mm_dh(dz, wi, comm)
            deliver(couts)
            add_up(l, ["w_in"])
        else:
            parts = small_grad_parts()
            reduced = []
            comm, deliver = host(chip_job(l, [whole("w_out")]),
                                 (_all_reduce_piece(jnp.concatenate(parts, axis=0)), reduced.extend))
            (grads[l]["w_in_a"],), couts = _mm_dw("mm_dw_in_a", [s["h"]], dz, block["w_in_a"][0], 1024, comm,
                                                  m_rows=(0, block["w_in_a"][0]))
            deliver(couts)
            comm, deliver = host(core_job(l, ["w_in_a"]))
            (grads[l]["w_in_b"],), couts = _mm_dw("mm_dw_in_b", [s["h"]], dz, block["w_in_b"][0], 2560, comm,
                                                  m_rows=(block["w_in_a"][0], block["w_in_b"][0]))
            deliver(couts)
            add_up(l, ["w_in_a"])
            comm, deliver = host(chip_job(l, [whole("w_in_a")]), core_job(l, ["w_in_b"]))
            dh, couts = _mm_dh(dz, wi, comm)
            deliver(couts)
            add_up(l, ["w_in_b"])
        dx, dxb, small[l]["norm1_g"] = _rmsnorm_bwd(s["x"], norm1_g[l:l + 1], dh, dx1)
    grad_x = dx.reshape(x.shape)

    sizes = [p.shape[0] for p in parts]
    comm, deliver = host(chip_job(0, [whole("w_in_b")]))
    last, couts = _all_reduce_small(_rows(small[0]["norm1_g"]), comm)
    deliver(couts)
    total = lax.dynamic_update_slice(reduced[0], last, (0, 0))
    offs = [0]
    for n in sizes:
        offs.append(offs[-1] + n)
    pieces = [total[offs[i]:offs[i + 1]] for i in range(len(parts))]
    loss = pieces[len(rep) + 2][0, 0]
    conv_g_full = pieces[len(rep) + 1].reshape(nl, 3, N_DEV, a // N_DEV)
    conv_g = lax.dynamic_index_in_dim(conv_g_full, me, axis=2, keepdims=False)
    names = rep + ["final_norm_g", "conv_w"]
    flat = lambda w: w.reshape(-1, w.shape[-1])
    small_w = [flat(rep_w[k]) for k in rep] + [flat(final_norm_g), flat(conv_w)]
    small_m = [flat(rep_m[k]) for k in rep] + [flat(m_final_norm_g), flat(m_conv_w)]
    small_v = [flat(rep_v[k]) for k in rep] + [flat(v_final_norm_g), flat(v_conv_w)]
    small_g = [pieces[i].reshape(small_w[i].shape) for i in range(len(rep) + 1)] + [flat(conv_g)]
    small_d, small_m, small_v = _adamw_small(small_g, small_w, small_m, small_v)
    shape_of = dict(rep_w, final_norm_g=final_norm_g, conv_w=conv_w)
    named = lambda arrays: {k: arr.reshape(shape_of[k].shape) for k, arr in zip(names, arrays)}
    res = {"grad": named(small_g), "delta": named(small_d), "m": named(small_m), "v": named(small_v)}

    for k in BIG:
        outs, _ = _adamw_big(f"adamw_{k}", lands[k], big[k], big_m[k], big_v[k])
        if k in ("w_gate", "w_up"):
            outs = [tr(o) for o in outs]
        res["grad"][k], res["delta"][k], res["m"][k], res["v"][k] = outs

    order = ["norm1_g", "w_in", "gmlp_ln_g", "gmlp_ln_b", "w_spatial", "b_spatial", "conv_w", "group_norm_g", "w_out",
             "norm2_g", "w_gate", "w_up", "w_down", "final_norm_g"]
    return (loss, grad_x, *[res["grad"][k] for k in order], *[res["delta"][k] for k in order],
            *[res["m"][k] for k in order], *[res["v"][k] for k in order])
```

```python
import functools
import math
import operator

import jax
import jax.numpy as jnp
from jax import lax
from jax.experimental import pallas as pl
from jax.experimental.pallas import tpu as pltpu

F32 = jnp.float32
BF16 = jnp.bfloat16
MESH = pl.DeviceIdType.MESH

N_DEV = 8
N_LAYERS = 2
HEADS = 8
BLK = 128
CHUNK = 64
HALO = 16
RMS_EPS = 1e-6
LN_EPS = 1e-5
ADAM_LR, ADAM_B1, ADAM_B2, ADAM_EPS, ADAM_WD, ADAM_STEP = 0.001, 0.9, 0.999, 1e-8, 0.01, 10
GELU_C = math.sqrt(2.0 / math.pi)
GELU_A = 0.044715

VMEM_LIMIT_V7X = 56 * 1024 * 1024
_TM = 1024
_TN = 1024
_TT = 1024
_TM_MIX = 256
_TM_NORM = 512


def _cparams(n_axes):
    return pltpu.CompilerParams(dimension_semantics=("arbitrary",) * n_axes, vmem_limit_bytes=VMEM_LIMIT_V7X)


def _sds(shape, dtype):
    return jax.ShapeDtypeStruct(tuple(shape), dtype)


def _place():
    return lax.axis_index("x"), lax.axis_index("y"), lax.axis_index("c")


def _index(place):
    return 4 * place[0] + 2 * place[1] + place[2]


class _Piece:
    def __init__(self, operands, out_shapes, aliases, n_sems, start, finish, mid1=None, mid2=None, vmem=(),
                 hooks=(0.6, 0.87)):
        self.operands, self.out_shapes, self.aliases, self.n_sems = list(operands), list(out_shapes), dict(aliases), n_sems
        self.vmem = list(vmem)
        self.hooks = hooks
        nothing = lambda ctx: None
        self.start, self.mid1, self.mid2, self.finish = start, mid1 or nothing, mid2 or nothing, finish


class _Ctx:
    def __init__(self, ins, outs, sems, offs):
        self.ins, self.outs, self.sems = ins, outs, sems
        self.o_in, self.o_out, self.o_send, self.o_recv, self.o_loc, self.o_vmem = offs

    def vmem(self, i):
        return self.sems[3 + self.o_vmem + i]

    def inp(self, i):
        return self.ins[self.o_in + i]

    def out(self, i):
        return self.outs[self.o_out + i]

    def send(self, k):
        return self.sems[0].at[self.o_send + k]

    def recv(self, k):
        return self.sems[1].at[self.o_recv + k]

    def local(self, k):
        return self.sems[2].at[self.o_loc + k]


class _Hosted:
    def __init__(self, pieces, n_in_before, n_out_before):
        self.pieces = [p for p in (pieces or []) if p is not None]
        self.operands, self.out_shapes, self.aliases, self.offs = [], [], {}, []
        counts, vmem = [0, 0, 0], []
        for p in self.pieces:
            self.offs.append((len(self.operands), len(self.out_shapes), *counts, len(vmem)))
            for i, j in p.aliases.items():
                self.aliases[n_in_before + len(self.operands) + i] = n_out_before + len(self.out_shapes) + j
            self.operands += p.operands
            self.out_shapes += p.out_shapes
            counts = [c + n for c, n in zip(counts, p.n_sems)]
            vmem += p.vmem
        hbm = pl.BlockSpec(memory_space=pl.ANY)
        self.in_specs = [hbm] * len(self.operands)
        self.out_specs = [hbm] * len(self.out_shapes)
        self.scratch = ([pltpu.SemaphoreType.DMA((max(c, 1),)) for c in counts] + vmem) if self.pieces else []

    def run(self, stage, ins, outs, sems):
        for p, offs in zip(self.pieces, self.offs):
            getattr(p, stage)(_Ctx(ins, outs, sems, offs))

    def wrap(self, grid, compute, ins, outs, sems):
        if not self.pieces:
            compute()
            return
        n_steps = math.prod(grid)
        lin = 0
        for ax, g in enumerate(grid):
            lin = lin * g + pl.program_id(ax)
        pl.when(lin == 0)(lambda: self.run("start", ins, outs, sems))
        compute()
        for stage, which in (("mid1", 0), ("mid2", 1)):
            for p, offs in zip(self.pieces, self.offs):
                at = min(n_steps - 1, int(p.hooks[which] * n_steps))
                pl.when(lin == at)(functools.partial(getattr(p, stage), _Ctx(ins, outs, sems, offs)))
        pl.when(lin == n_steps - 1)(lambda: self.run("finish", ins, outs, sems))


def _cols_view(width):
    return lambda ref, p: ref.at[:, pl.ds(pl.multiple_of(p * width, 128), width)]


def _rows_view(height):
    return lambda ref, p: ref.at[pl.ds(pl.multiple_of(p * height, 16), height), :]


def _cols_halves(rows, width, part, n_parts):
    hr = rows // n_parts // 2
    at = lambda h: pl.ds(part * 2 * hr + h * hr, hr)
    return (lambda ref, p, h: ref.at[at(h), pl.ds(pl.multiple_of(p * width, 128), width)],
            lambda ref, h: ref.at[at(h), :], 2)


def _rows_halves(height, part, n_parts):
    hh = height // n_parts // 2
    return (lambda ref, p, h: ref.at[pl.ds(pl.multiple_of(p * height + part * 2 * hh + h * hh, 16), hh), :],
            lambda ref, h: ref.at[pl.ds(part * 2 * hh + h * hh, hh), :], 2)


_SLOT_WHOLE = (lambda ref, p, h: ref.at[p], lambda ref, h: ref, 1)


def _ag_piece(specs):
    units = [(a, h) for a, s in enumerate(specs) for h in s[3]]

    def plan(ctx):
        x, y, c = _place()
        me, sib, xn, yn, dg = (x, y, c), (x, y, 1 - c), (1 - x, y, c), (x, 1 - y, c), (1 - x, 1 - y, c)

        def copy(u, k, block, to, from_shard=False):
            a, h = units[u]
            dst_of, src_of, _ = specs[a][2]
            dst = dst_of(ctx.out(a), _index(block), h)
            return pltpu.make_async_remote_copy(
                src_ref=src_of(ctx.inp(a), h) if from_shard else dst, dst_ref=dst, send_sem=ctx.send(7 * u + k),
                recv_sem=ctx.recv(7 * u + k), device_id=to, device_id_type=MESH)

        def local(u):
            a, h = units[u]
            dst_of, src_of, _ = specs[a][2]
            return pltpu.make_async_copy(src_of(ctx.inp(a), h), dst_of(ctx.out(a), _index(me), h), ctx.local(u))

        def relay(u):
            return copy(u, 3, xn, yn) if units[u][1] % 2 == 0 else copy(u, 3, yn, xn)

        return me, sib, xn, yn, dg, c, copy, local, relay

    def start(ctx):
        me, sib, xn, yn, dg, c, copy, local, relay = plan(ctx)
        for u in range(len(units)):
            local(u).start()
            for k, to in enumerate((sib, xn, yn)):
                copy(u, k, me, to, from_shard=True).start()

    def mid1(ctx):
        me, sib, xn, yn, dg, c, copy, local, relay = plan(ctx)
        for u in range(len(units)):
            copy(u, 1, xn, me).wait_recv()
            copy(u, 2, yn, me).wait_recv()
            relay(u).start()
            copy(u, 4, xn, sib).start()
            copy(u, 5, yn, sib).start()

    def mid2(ctx):
        me, sib, xn, yn, dg, c, copy, local, relay = plan(ctx)
        for u in range(len(units)):
            copy(u, 3, dg, me).wait_recv()
            copy(u, 6, dg, sib).start()

    def finish(ctx):
        me, sib, xn, yn, dg, c, copy, local, relay = plan(ctx)
        other = lambda place: (place[0], place[1], 1 - c)
        for u in range(len(units)):
            for k, block in ((0, sib), (4, other(xn)), (5, other(yn)), (6, other(dg))):
                copy(u, k, block, me).wait_recv()
        for u in range(len(units)):
            for k, to in enumerate((sib, xn, yn)):
                copy(u, k, me, to, from_shard=True).wait_send()
            relay(u).wait_send()
            for k, block in ((4, xn), (5, yn), (6, dg)):
                copy(u, k, block, sib).wait_send()
            local(u).wait()

    n_u = len(units)
    operands, aliases = [s[0] for s in specs], {}
    for a, spec in enumerate(specs):
        if spec[4] is not None:
            aliases[len(operands)] = a
            operands.append(spec[4])
    return _Piece(operands, [s[1] for s in specs], aliases, (7 * n_u, 7 * n_u, n_u), start, finish, mid1, mid2)


N_CHIPS = 4


def _rs_core_piece(specs):
    n = len(specs)

    def copies(ctx):
        x, y, c = _place()
        out = []
        for a in range(n):
            for q in range(N_CHIPS):
                out.append(pltpu.make_async_remote_copy(
                    src_ref=specs[a][2](ctx.inp(a), 2 * q + (1 - c)), dst_ref=ctx.out(a).at[q],
                    send_sem=ctx.send(N_CHIPS * a + q), recv_sem=ctx.recv(N_CHIPS * a + q), device_id=(x, y, 1 - c),
                    device_id_type=MESH))
        return out

    def start(ctx):
        for cp in copies(ctx):
            cp.start()

    def finish(ctx):
        for cp in copies(ctx):
            cp.wait_recv()
            cp.wait_send()

    return _Piece([s[0] for s in specs], [s[1] for s in specs], {}, (N_CHIPS * n, N_CHIPS * n, 0), start, finish)


def _rs_chip_piece(specs, layer):
    n = len(specs)
    hops = [(1, 0), (0, 1), (1, 1)]

    def copies(ctx):
        x, y, c = _place()
        mine = 2 * x + y
        out = []
        for a in range(n):
            first, landing, size = specs[a][2]
            rows, to = pl.ds(first, size), pl.ds(landing, size)
            sums, land = ctx.inp(a), ctx.out(a)
            out.append((pltpu.make_async_copy(sums.at[mine, rows], land.at[layer, mine, to], ctx.local(a)), None))
            for j, (dx, dy) in enumerate(hops):
                px, py = x ^ dx, y ^ dy
                peer = 2 * px + py
                send = pltpu.make_async_remote_copy(
                    src_ref=sums.at[peer, rows], dst_ref=land.at[layer, mine, to], send_sem=ctx.send(3 * a + j),
                    recv_sem=ctx.recv(3 * a + j), device_id=(px, py, c), device_id_type=MESH)
                recv = pltpu.make_async_remote_copy(
                    src_ref=sums.at[peer, rows], dst_ref=land.at[layer, peer, to], send_sem=ctx.send(3 * a + j),
                    recv_sem=ctx.recv(3 * a + j), device_id=(px, py, c), device_id_type=MESH)
                out.append((send, recv))
        return out

    def start(ctx):
        for send, _ in copies(ctx):
            send.start()

    def finish(ctx):
        for send, recv in copies(ctx):
            if recv is None:
                send.wait()
            else:
                recv.wait_recv()
                send.wait_send()

    operands, aliases = [s[0] for s in specs], {}
    for a, spec in enumerate(specs):
        if spec[3] is not None:
            aliases[len(operands)] = a
            operands.append(spec[3])
    return _Piece(operands, [s[1] for s in specs], aliases, (3 * n, 3 * n, n), start, finish)


def _all_reduce_piece(pack):
    r = pack.shape[0]
    half = r // 2

    def plan(ctx):
        x, y, c = _place()
        acc, got = ctx.vmem(0), ctx.vmem(1)
        mine = pl.ds(pl.multiple_of(c * half, 8), half)
        sib = (x, y, 1 - c)
        copies = [
            pltpu.make_async_remote_copy(src_ref=acc.at[0], dst_ref=got.at[0], send_sem=ctx.send(0), recv_sem=ctx.recv(0),
                                         device_id=sib, device_id_type=MESH),
            pltpu.make_async_remote_copy(src_ref=acc.at[1, mine], dst_ref=got.at[1, mine], send_sem=ctx.send(1),
                                         recv_sem=ctx.recv(1), device_id=(1 - x, y, c), device_id_type=MESH),
            pltpu.make_async_remote_copy(src_ref=acc.at[2, mine], dst_ref=got.at[2, mine], send_sem=ctx.send(2),
                                         recv_sem=ctx.recv(2), device_id=(x, 1 - y, c), device_id_type=MESH),
            pltpu.make_async_remote_copy(src_ref=acc.at[3, mine], dst_ref=acc.at[3, mine], send_sem=ctx.send(3),
                                         recv_sem=ctx.recv(3), device_id=sib, device_id_type=MESH),
        ]
        other = pl.ds(pl.multiple_of((1 - c) * half, 8), half)
        arrival = pltpu.make_async_remote_copy(src_ref=acc.at[3, other], dst_ref=acc.at[3, other], send_sem=ctx.send(3),
                                               recv_sem=ctx.recv(3), device_id=sib, device_id_type=MESH)
        return acc, got, mine, copies, arrival

    def start(ctx):
        acc, got, mine, copies, arrival = plan(ctx)
        load = pltpu.make_async_copy(ctx.inp(0), acc.at[0], ctx.local(0))
        load.start()
        load.wait()
        copies[0].start()

    def mid1(ctx):
        acc, got, mine, copies, arrival = plan(ctx)
        copies[0].wait()
        acc[1] = acc[0] + got[0]
        copies[1].start()

    def mid2(ctx):
        acc, got, mine, copies, arrival = plan(ctx)
        copies[1].wait()
        acc[2, mine] = acc[1, mine] + got[1, mine]
        copies[2].start()

    def finish(ctx):
        acc, got, mine, copies, arrival = plan(ctx)
        copies[2].wait()
        acc[3, mine] = acc[2, mine] + got[2, mine]
        copies[3].start()
        copies[3].wait_send()
        arrival.wait_recv()
        store = pltpu.make_async_copy(acc.at[3], ctx.out(0), ctx.local(0))
        store.start()
        store.wait()

    return _Piece([pack], [_sds(pack.shape, F32)], {}, (4, 4, 1), start, finish, mid1, mid2,
                  vmem=[pltpu.VMEM((4, r, 128), F32), pltpu.VMEM((3, r, 128), F32)], hooks=(0.25, 0.6))


def _chip_sums(name, grad, stage, by_cols, core):
    _, r, c = stage.shape
    tr = r
    while tr * c > 1024 * 1024 or r % tr or tr % 16:
        tr -= 16
    n_t = r // tr

    def body(core_ref, g_ref, s_ref, o_ref):
        o_ref[...] = (g_ref[...].astype(F32) + s_ref[...].astype(F32)).astype(BF16)

    if by_cols:
        gspec = pl.BlockSpec((tr, c), lambda q, i, core_ref: (i, 2 * q + core_ref[0]))
    else:
        gspec = pl.BlockSpec((tr, c), lambda q, i, core_ref: ((2 * q + core_ref[0]) * n_t + i, 0))
    sspec = pl.BlockSpec((None, tr, c), lambda q, i, core_ref: (q, i, 0))
    return pl.pallas_call(
        body, name=name, out_shape=_sds(stage.shape, BF16),
        grid_spec=pltpu.PrefetchScalarGridSpec(num_scalar_prefetch=1, grid=(N_CHIPS, n_t), in_specs=[gspec, sspec],
                                               out_specs=sspec),
        compiler_params=_cparams(2))(core, grad, stage)


def _call_hosting(body, name, grid, out_shapes, in_specs, out_specs, operands, scratch, comm):
    n_in, n_out, n_scr = len(operands), len(out_shapes), len(scratch)
    hosted = _Hosted(comm, n_in, n_out)
    n_ci, n_co = len(hosted.operands), len(hosted.out_shapes)

    def hosting_body(*refs):
        ins, rest = refs[:n_in], refs[n_in:]
        c_ins, rest = rest[:n_ci], rest[n_ci:]
        outs, rest = rest[:n_out], rest[n_out:]
        c_outs, rest = rest[:n_co], rest[n_co:]
        hosted.wrap(grid, lambda: body(*ins, *outs, *rest[:n_scr]), c_ins, c_outs, rest[n_scr:])

    res = pl.pallas_call(
        hosting_body, name=name, grid=grid, out_shape=tuple(list(out_shapes) + hosted.out_shapes),
        in_specs=list(in_specs) + hosted.in_specs, out_specs=tuple(list(out_specs) + hosted.out_specs),
        input_output_aliases=hosted.aliases, scratch_shapes=list(scratch) + hosted.scratch,
        compiler_params=_cparams(len(grid)))(*operands, *hosted.operands)
    return list(res[:n_out]), list(res[n_out:])


def _matmul(name, grid, nk, kaxis, pairs, dims, extras, outs, epilogue, sum_pairs, acc_shape, comm=None, split=None):
    n_p, n_e, n_o = len(pairs), len(extras), len(outs)
    n_acc = 0 if nk == 1 else (1 if sum_pairs else n_p)
    n_in = 2 * n_p + n_e
    hosted = _Hosted(comm, n_in, n_o)
    n_ci, n_co = len(hosted.operands), len(hosted.out_shapes)

    def body(*refs):
        a_refs = refs[0:2 * n_p:2]
        b_refs = refs[1:2 * n_p:2]
        e_refs = refs[2 * n_p:n_in]
        c_ins = refs[n_in:n_in + n_ci]
        o_refs = refs[n_in + n_ci:n_in + n_ci + n_o]
        c_outs = refs[n_in + n_ci + n_o:n_in + n_ci + n_o + n_co]
        acc_refs = refs[n_in + n_ci + n_o + n_co:n_in + n_ci + n_o + n_co + n_acc]
        sems = refs[n_in + n_ci + n_o + n_co + n_acc:]

        def dots():
            if sum_pairs and n_p > 1 and dims == NN:
                a_all = jnp.concatenate([a[...] for a in a_refs], axis=1)
                b_all = jnp.concatenate([b[...] for b in b_refs], axis=0)
                return [lax.dot_general(a_all, b_all, (dims, ((), ())), preferred_element_type=F32)]
            prods = [lax.dot_general(a[...], b[...], (dims, ((), ())), preferred_element_type=F32)
                     for a, b in zip(a_refs, b_refs)]
            if sum_pairs and n_p > 1:
                prods = [functools.reduce(operator.add, prods)]
            return prods

        def compute():
            if nk == 1 and split is not None:
                n_split, b_axis, n_row = split
                width = b_refs[0].shape[b_axis] // n_split
                height = a_refs[0].shape[0] // n_row
                for s in range(n_split):
                    cols = pl.ds(s * width, width)
                    for r in range(n_row):
                        rows = pl.ds(r * height, height)
                        epilogue([lax.dot_general(a[rows, :], b[cols, :] if b_axis == 0 else b[:, cols], (dims, ((), ())),
                                                  preferred_element_type=F32) for a, b in zip(a_refs, b_refs)],
                                 e_refs, o_refs, rows, cols)
                return
            if nk == 1:
                epilogue(dots(), e_refs, o_refs)
                return
            k = pl.program_id(kaxis)

            @pl.when(k == 0)
            def _():
                for acc, p in zip(acc_refs, dots()):
                    acc[...] = p

            if nk > 2:
                @pl.when((k > 0) & (k < nk - 1))
                def _():
                    for acc, p in zip(acc_refs, dots()):
                        acc[...] += p

            @pl.when(k == nk - 1)
            def _():
                epilogue([acc[...] + p for acc, p in zip(acc_refs, dots())], e_refs, o_refs)

        hosted.wrap(grid, compute, c_ins, c_outs, sems)

    operands, in_specs = [], []
    for a, a_spec, b, b_spec in pairs:
        operands += [a, b]
        in_specs += [a_spec, b_spec]
    for e, e_spec in extras:
        operands.append(e)
        in_specs.append(e_spec)
    res = pl.pallas_call(
        body, name=name, grid=grid,
        out_shape=tuple([o for o, _ in outs] + hosted.out_shapes),
        in_specs=in_specs + hosted.in_specs, out_specs=tuple([s for _, s in outs] + hosted.out_specs),
        input_output_aliases=hosted.aliases,
        scratch_shapes=[pltpu.VMEM(acc_shape, F32) for _ in range(n_acc)] + hosted.scratch,
        compiler_params=_cparams(len(grid)),
    )(*operands, *hosted.operands)
    return list(res[:n_o]), list(res[n_o:])


NN = ((1,), (0,))
NT = ((1,), (1,))
TN = ((0,), (0,))


def _tile(n, want):
    if n <= want:
        return n
    t = want // 128 * 128
    while n % t:
        t -= 128
    return t


def _silu_parts(g):
    s = 0.5 + 0.5 * jnp.tanh(0.5 * g)
    return s, g * s


def _mm_in(h, w_in, comm=None):
    t, d = h.shape
    n = w_in.shape[1]
    tm, tn = _tile(t, _TM), _tile(n, _TN)

    def epi(accs, e, o):
        o[0][...] = accs[0].astype(BF16)

    outs, couts = _matmul(
        "mm_in", (n // tn, t // tm), 1, None,
        [(h, pl.BlockSpec((tm, d), lambda j, i: (i, 0)), w_in, pl.BlockSpec((d, tn), lambda j, i: (0, j)))],
        NN, [], [(_sds((t, n), BF16), pl.BlockSpec((tm, tn), lambda j, i: (i, j)))], epi, True, None, comm)
    return outs[0], couts


def _mm_out(y, w_out, x, comm=None):
    t, m = y.shape
    d = w_out.shape[1]
    tm, tn = _tile(t, _TM), _tile(d, _TN)

    def epi(accs, e, o):
        o[0][...] = e[0][...] + accs[0]

    outs, couts = _matmul(
        "mm_out", (t // tm, d // tn), 1, None,
        [(y, pl.BlockSpec((tm, m), lambda i, j: (i, 0)), w_out, pl.BlockSpec((m, tn), lambda i, j: (0, j)))],
        NN, [(x, pl.BlockSpec((tm, tn), lambda i, j: (i, j)))],
        [(_sds((t, d), F32), pl.BlockSpec((tm, tn), lambda i, j: (i, j)))], epi, True, None, comm)
    return outs[0], couts


def _mm_swiglu(h2, wgt, wut, comm=None):
    t, d = h2.shape
    f = wgt.shape[0]
    tm, tn = _tile(t, 2 * _TM), _tile(f, 512)

    def epi(accs, e, o, rows, cols):
        g, u = accs
        s, sg = _silu_parts(g)
        o[0][rows, cols] = (sg * u).astype(BF16)
        o[1][rows, cols] = (u * (s + sg * (1.0 - s))).astype(BF16)
        o[2][rows, cols] = sg.astype(BF16)

    wspec = pl.BlockSpec((tn, d), lambda i, j: (j, 0))
    hspec = pl.BlockSpec((tm, d), lambda i, j: (i, 0))
    ospec = pl.BlockSpec((tm, tn), lambda i, j: (i, j))
    osh = _sds((t, f), BF16)
    outs, couts = _matmul("mm_swiglu", (t // tm, f // tn), 1, None, [(h2, hspec, wgt, wspec), (h2, hspec, wut, wspec)],
                          NT, [], [(osh, ospec)] * 3, epi, False, None, comm, split=(tn // 256, 0, 2))
    return outs, couts


def _mm_down(act, wd, x1, comm=None):
    t, f = act.shape
    d = wd.shape[1]
    tm, tn = _tile(t, _TM), _tile(d, _TN)
    nk = 2
    tk = f // nk

    def epi(accs, e, o):
        o[0][...] = e[0][...] + accs[0]

    outs, couts = _matmul(
        "mm_down", (t // tm, d // tn, nk), nk, 2,
        [(act, pl.BlockSpec((tm, tk), lambda i, j, k: (i, k)), wd, pl.BlockSpec((tk, tn), lambda i, j, k: (k, j)))],
        NN, [(x1, pl.BlockSpec((tm, tn), lambda i, j, k: (i, j)))],
        [(_sds((t, d), F32), pl.BlockSpec((tm, tn), lambda i, j, k: (i, j)))], epi, True, (tm, tn), comm)
    return outs[0], couts


def _mm_dact(dxb, wd, dact_dgate, dact_dup, comm=None):
    t, d = dxb.shape
    f = wd.shape[0]
    tm, tn = _tile(t, 2 * _TM), _tile(f, 512)

    def epi(accs, e, o, rows, cols):
        da = accs[0]
        o[0][rows, cols] = (da * e[0][rows, cols].astype(F32)).astype(BF16)
        o[1][rows, cols] = (da * e[1][rows, cols].astype(F32)).astype(BF16)

    bspec = pl.BlockSpec((tm, tn), lambda i, j: (i, j))
    osh = _sds((t, f), BF16)
    outs, couts = _matmul(
        "mm_dact", (t // tm, f // tn), 1, None,
        [(dxb, pl.BlockSpec((tm, d), lambda i, j: (i, 0)), wd, pl.BlockSpec((tn, d), lambda i, j: (j, 0)))],
        NT, [(dact_dgate, bspec), (dact_dup, bspec)], [(osh, bspec)] * 2, epi, True, None, comm, split=(tn // 256, 0, 2))
    return outs, couts


def _mm_dh2(dgate, dup, wgt, wut, comm=None):
    t, f = dgate.shape
    d = wgt.shape[1]
    tm, tn = _tile(t, _TM), _tile(d, _TN)
    nk = 4
    tk = f // nk

    def epi(accs, e, o):
        o[0][...] = accs[0]

    aspec = pl.BlockSpec((tm, tk), lambda i, j, k: (i, k))
    wspec = pl.BlockSpec((tk, tn), lambda i, j, k: (k, j))
    outs, couts = _matmul("mm_dh2", (t // tm, d // tn, nk), nk, 2, [(dgate, aspec, wgt, wspec), (dup, aspec, wut, wspec)],
                          NN, [], [(_sds((t, d), F32), pl.BlockSpec((tm, tn), lambda i, j, k: (i, j)))], epi, True,
                          (tm, tn), comm)
    return outs[0], couts


def _mm_dw(name, a_list, b, tmo, tno, comm=None, m_rows=None):
    t, m = a_list[0].shape
    start, m = (0, m) if m_rows is None else m_rows
    n = b.shape[1]
    tt = _tile(t, _TT)
    nk = t // tt
    tmo, tno = _tile(m, tmo), _tile(n, tno)
    first = start // tmo

    def epi(accs, e, o):
        for acc, out in zip(accs, o):
            out[...] = acc.astype(BF16)

    aspec = pl.BlockSpec((tt, tmo), lambda i, j, k: (k, first + i))
    bspec = pl.BlockSpec((tt, tno), lambda i, j, k: (k, j))
    ospec = pl.BlockSpec((tmo, tno), lambda i, j, k: (i, j))
    if nk == 1:
        return _matmul(name, (m // tmo, n // tno, 1), 1, None, [(a, aspec, b, bspec) for a in a_list], TN, [],
                       [(_sds((m, n), BF16), ospec)] * len(a_list), epi, False, None, comm)
    return _matmul(name, (m // tmo, n // tno, nk), nk, 2, [(a, aspec, b, bspec) for a in a_list], TN, [],
                   [(_sds((m, n), BF16), ospec)] * len(a_list), epi, False, (tmo, tno), comm)


def _mm_dy(dxb, w_out, comm=None):
    t, d = dxb.shape
    m = w_out.shape[0]
    tm, tn = _tile(t, _TM), _tile(m, _TN)

    def epi(accs, e, o):
        o[0][...] = accs[0].astype(BF16)

    outs, couts = _matmul(
        "mm_dy", (t // tm, m // tn), 1, None,
        [(dxb, pl.BlockSpec((tm, d), lambda i, j: (i, 0)), w_out, pl.BlockSpec((tn, d), lambda i, j: (j, 0)))], NT, [],
        [(_sds((t, m), BF16), pl.BlockSpec((tm, tn), lambda i, j: (i, j)))], epi, True, None, comm)
    return outs[0], couts


def _mm_dh(dz, w_in, comm=None):
    t, n = dz.shape
    d = w_in.shape[0]
    tm, tn = _tile(t, _TM), _tile(d, _TN)
    nk = 2
    tk = n // nk

    def epi(accs, e, o):
        o[0][...] = accs[0]

    outs, couts = _matmul(
        "mm_dh", (t // tm, d // tn, nk), nk, 2,
        [(dz, pl.BlockSpec((tm, tk), lambda i, j, k: (i, k)), w_in, pl.BlockSpec((tn, tk), lambda i, j, k: (j, k)))], NT,
        [], [(_sds((t, d), F32), pl.BlockSpec((tm, tn), lambda i, j, k: (i, j)))], epi, True, (tm, tn), comm)
    return outs[0], couts


def _rmsnorm_fwd(x, g, comm=None):
    t, d = x.shape
    tm = min(_TM_NORM, t)

    def body(x_ref, g_ref, o_ref):
        xv = x_ref[...]
        rs = lax.rsqrt(jnp.mean(xv * xv, axis=-1, keepdims=True) + RMS_EPS)
        o_ref[...] = (xv * rs * g_ref[...]).astype(BF16)

    outs, couts = _call_hosting(
        body, "rmsnorm_fwd", (t // tm,), [_sds((t, d), BF16)],
        [pl.BlockSpec((tm, d), lambda i: (i, 0)), pl.BlockSpec((1, d), lambda i: (0, 0))],
        [pl.BlockSpec((tm, d), lambda i: (i, 0))], [x, g], [], comm)
    return outs[0], couts


def _rmsnorm_bwd_math(xv, g, dh):
    rs = lax.rsqrt(jnp.mean(xv * xv, axis=-1, keepdims=True) + RMS_EPS)
    xh = xv * rs
    gd = dh * g
    dx = rs * (gd - xh * jnp.mean(gd * xh, axis=-1, keepdims=True))
    return dx, jnp.sum(dh * xh, axis=0, keepdims=True)


def _rmsnorm_bwd(x, g, dh, dres, with_bf16=True):
    t, d = x.shape
    tm = min(_TM_NORM, t)

    def body(x_ref, g_ref, dh_ref, dres_ref, dx_ref, *rest):
        dg_ref = rest[-1]
        dx, dg = _rmsnorm_bwd_math(x_ref[...], g_ref[...], dh_ref[...])
        dx = dx + dres_ref[...]
        dx_ref[...] = dx
        if with_bf16:
            rest[0][...] = dx.astype(BF16)

        @pl.when(pl.program_id(0) == 0)
        def _():
            dg_ref[...] = dg

        @pl.when(pl.program_id(0) > 0)
        def _():
            dg_ref[...] += dg

    row = pl.BlockSpec((tm, d), lambda i: (i, 0))
    vec = pl.BlockSpec((1, d), lambda i: (0, 0))
    halves = [(_sds((t, d), BF16), row)] if with_bf16 else []
    outs = [(_sds((t, d), F32), row), *halves, (_sds((1, d), F32), vec)]
    res = pl.pallas_call(
        body, name="rmsnorm_bwd", grid=(t // tm,), out_shape=tuple(o for o, _ in outs),
        in_specs=[row, vec, row, row], out_specs=tuple(s for _, s in outs), compiler_params=_cparams(1))(x, g, dh, dres)
    return (res[0], res[1], res[2]) if with_bf16 else (res[0], None, res[1])


def _loss_head(x, g, target):
    t, d = x.shape
    tm = min(_TM_NORM, t)

    def body(x_ref, g_ref, t_ref, dx_ref, dxb_ref, dg_ref, loss_ref):
        xv, gv = x_ref[...], g_ref[...]
        rs = lax.rsqrt(jnp.mean(xv * xv, axis=-1, keepdims=True) + RMS_EPS)
        diff = xv * rs * gv - t_ref[...]
        part = 0.5 * jnp.sum(jnp.mean(diff * diff, axis=-1, keepdims=True), axis=0, keepdims=True)
        part = jnp.broadcast_to(part, (1, 128))
        dx, dg = _rmsnorm_bwd_math(xv, gv, diff * (1.0 / d))
        dx_ref[...] = dx
        dxb_ref[...] = dx.astype(BF16)

        @pl.when(pl.program_id(0) == 0)
        def _():
            dg_ref[...] = dg
            loss_ref[...] = part

        @pl.when(pl.program_id(0) > 0)
        def _():
            dg_ref[...] += dg
            loss_ref[...] += part

    row = pl.BlockSpec((tm, d), lambda i: (i, 0))
    vec = pl.BlockSpec((1, d), lambda i: (0, 0))
    return pl.pallas_call(
        body, name="loss_head", grid=(t // tm,),
        out_shape=(_sds((t, d), F32), _sds((t, d), BF16), _sds((1, d), F32), _sds((1, 128), F32)),
        in_specs=[row, vec, row], out_specs=(row, row, vec, pl.BlockSpec((1, 128), lambda i: (0, 0))),
        compiler_params=_cparams(1))(x, g, target)


def _gelu(x):
    th = jnp.tanh(GELU_C * (x + GELU_A * x * x * x))
    return 0.5 * x * (1.0 + th), th


def _gelu_grad(x, th):
    return 0.5 * (1.0 + th) + 0.5 * x * (1.0 - th * th) * GELU_C * (1.0 + 3.0 * GELU_A * x * x)


def _masked_ws(ws_ref, h):
    i = lax.broadcasted_iota(jnp.int32, (BLK, BLK), 0) // CHUNK
    j = lax.broadcasted_iota(jnp.int32, (BLK, BLK), 1) // CHUNK
    return jnp.where(j <= i, ws_ref[h], 0.0)


def _shift_down(q, n, first_rows):
    rolled = pltpu.roll(q, n, 0)
    row = lax.broadcasted_iota(jnp.int32, q.shape, 0)
    for r, val in enumerate(first_rows):
        rolled = jnp.where(row == r, val, rolled)
    return rolled


def _shift_up(q, n, last_rows):
    tm = q.shape[0]
    rolled = pltpu.roll(q, tm - n, 0)
    row = lax.broadcasted_iota(jnp.int32, q.shape, 0)
    for r, val in enumerate(last_rows):
        rolled = jnp.where(row == tm - n + r, val, rolled)
    return rolled


def _mixer_specs(t, a, tm):
    hb = tm // HALO
    last = t // HALO - 1
    tile = pl.BlockSpec((tm, 5 * a), lambda i: (i, 0))
    prev = [pl.BlockSpec((HALO, a), functools.partial(lambda i, col: (jnp.maximum(i * hb - 1, 0), col), col=col))
            for col in (3, 4)]
    nxt = [pl.BlockSpec((HALO, a), functools.partial(lambda i, col: (jnp.minimum((i + 1) * hb, last), col), col=col))
           for col in (2, 3, 4)]
    return tile, prev, nxt


def _group_a_fwd(zu, zv, lng, lnb, ws_ref, bb_ref, mixed_ref, vln_ref):
    u, thu = _gelu(zu)
    v, thv = _gelu(zv)
    mu = jnp.mean(v, axis=-1, keepdims=True)
    vc = v - mu
    rs = lax.rsqrt(jnp.mean(vc * vc, axis=-1, keepdims=True) + LN_EPS)
    vhat = vc * rs
    vln_ref[...] = vhat * lng + lnb
    tm, a = zu.shape
    hd = a // HEADS
    for h in range(HEADS):
        w = _masked_ws(ws_ref, h).astype(BF16)
        for b in range(tm // BLK):
            rows, cols = pl.ds(b * BLK, BLK), pl.ds(h * hd, hd)
            mixed_ref[rows, cols] = jnp.dot(w, vln_ref[rows, cols].astype(BF16), preferred_element_type=F32) + bb_ref[h]
    return u, thu, thv, rs, vhat


def _mixer_fwd(z, ln_g, ln_b, w_spatial, bb, conv_w, gg, comm=None):
    t = z.shape[0]
    a = z.shape[1] // 5
    tm = min(_TM_MIX, t)
    tile, prev, _ = _mixer_specs(t, a, tm)

    def body(z_ref, pc_ref, ph_ref, lng_ref, lnb_ref, ws_ref, bb_ref, cw_ref, gg_ref, y_ref, mixed_ref, vln_ref):
        i = pl.program_id(0)
        zu = z_ref[:, 0:a].astype(F32)
        zv = z_ref[:, a:2 * a].astype(F32)
        u, _, _, _, _ = _group_a_fwd(zu, zv, lng_ref[...], lnb_ref[...], ws_ref, bb_ref, mixed_ref, vln_ref)
        ya = u * mixed_ref[...]
        ra = lax.rsqrt(jnp.mean(ya * ya, axis=-1, keepdims=True) + RMS_EPS)
        y_ref[:, 0:a] = (ya * ra * gg_ref[:, 0:a]).astype(BF16)

        zb = z_ref[:, 2 * a:3 * a].astype(F32)
        q = z_ref[:, 3 * a:4 * a].astype(F32) * z_ref[:, 4 * a:5 * a].astype(F32)
        qp = jnp.where(i > 0, pc_ref[...].astype(F32) * ph_ref[...].astype(F32), 0.0)
        qm1 = _shift_down(q, 1, [qp[HALO - 1:HALO]])
        qm2 = _shift_down(q, 2, [qp[HALO - 2:HALO - 1], qp[HALO - 1:HALO]])
        cv = cw_ref[0:1, :] * qm2 + cw_ref[1:2, :] * qm1 + cw_ref[2:3, :] * q
        yb = zb * cv
        rb = lax.rsqrt(jnp.mean(yb * yb, axis=-1, keepdims=True) + RMS_EPS)
        y_ref[:, a:2 * a] = (yb * rb * gg_ref[:, a:2 * a]).astype(BF16)

    full = lambda shape: pl.BlockSpec(shape, lambda i: (0,) * len(shape))
    outs, couts = _call_hosting(
        body, "mixer_fwd", (t // tm,), [_sds((t, 2 * a), BF16)],
        [tile, *prev, full((1, a)), full((1, a)), full(w_spatial.shape), full(bb.shape), full(conv_w.shape),
         full((1, 2 * a))],
        [pl.BlockSpec((tm, 2 * a), lambda i: (i, 0))], [z, z, z, ln_g, ln_b, w_spatial, bb, conv_w, gg],
        [pltpu.VMEM((tm, a), F32), pltpu.VMEM((tm, a), F32)], comm)
    return outs[0], couts


def _mixer_bwd(z, dy, ln_g, ln_b, w_spatial, bb, conv_w, gg, comm=None):
    t = z.shape[0]
    a = z.shape[1] // 5
    hd = a // HEADS
    tm = min(_TM_MIX, t)
    n_tiles = t // tm
    tile, prev, nxt = _mixer_specs(t, a, tm)
    hb = tm // HALO
    dy_tile = pl.BlockSpec((tm, 2 * a), lambda i: (i, 0))
    dy_next = pl.BlockSpec((HALO, a), lambda i: (jnp.minimum((i + 1) * hb, t // HALO - 1), 1))

    def body(z_ref, pc_ref, ph_ref, nb_ref, nc_ref, nh_ref, dy_ref, ndy_ref, lng_ref, lnb_ref, ws_ref, bb_ref, cw_ref,
             gg_ref, dz_ref, dlng_ref, dlnb_ref, dws_ref, dbb_ref, dcw_ref, dgg_ref, mixed_ref, vln_ref, dmix_ref,
             dvln_ref):
        i = pl.program_id(0)

        @pl.when(i == 0)
        def _():
            for ref in (dlng_ref, dlnb_ref, dws_ref, dbb_ref, dcw_ref, dgg_ref):
                ref[...] = jnp.zeros(ref.shape, F32)

        lng = lng_ref[...]
        zu = z_ref[:, 0:a].astype(F32)
        zv = z_ref[:, a:2 * a].astype(F32)
        u, thu, thv, rs, vhat = _group_a_fwd(zu, zv, lng, lnb_ref[...], ws_ref, bb_ref, mixed_ref, vln_ref)
        mixed = mixed_ref[...]
        ya = u * mixed
        ra = lax.rsqrt(jnp.mean(ya * ya, axis=-1, keepdims=True) + RMS_EPS)
        da = dy_ref[:, 0:a].astype(F32)
        yah = ya * ra
        dgg_ref[:, 0:a] += jnp.sum(da * yah, axis=0, keepdims=True)
        ga = da * gg_ref[:, 0:a]
        dya = ra * (ga - yah * jnp.mean(ga * yah, axis=-1, keepdims=True))
        dz_ref[:, 0:a] = (dya * mixed * _gelu_grad(zu, thu)).astype(BF16)
        dmix_ref[...] = dya * u
        for h in range(HEADS):
            w = _masked_ws(ws_ref, h).astype(BF16)
            dw = jnp.zeros((BLK, BLK), F32)
            db = jnp.zeros((BLK, hd), F32)
            for b in range(tm // BLK):
                rows, cols = pl.ds(b * BLK, BLK), pl.ds(h * hd, hd)
                dm = dmix_ref[rows, cols]
                dmb = dm.astype(BF16)
                db = db + dm
                dw = dw + lax.dot_general(dmb, vln_ref[rows, cols].astype(BF16), (NT, ((), ())),
                                          preferred_element_type=F32)
                dvln_ref[rows, cols] = lax.dot_general(w, dmb, (TN, ((), ())), preferred_element_type=F32)
            dws_ref[h] += dw
            dbb_ref[h] += db
        dvln = dvln_ref[...]
        dlng_ref[...] += jnp.sum(dvln * vhat, axis=0, keepdims=True)
        dlnb_ref[...] += jnp.sum(dvln, axis=0, keepdims=True)
        dvh = dvln * lng
        dv = rs * (dvh - jnp.mean(dvh, axis=-1, keepdims=True) - vhat * jnp.mean(dvh * vhat, axis=-1, keepdims=True))
        dz_ref[:, a:2 * a] = (dv * _gelu_grad(zv, thv)).astype(BF16)

        w0, w1, w2 = cw_ref[0:1, :], cw_ref[1:2, :], cw_ref[2:3, :]
        ggb = gg_ref[:, a:2 * a]
        zb = z_ref[:, 2 * a:3 * a].astype(F32)
        zc = z_ref[:, 3 * a:4 * a].astype(F32)
        zh = z_ref[:, 4 * a:5 * a].astype(F32)
        q = zc * zh
        qp = jnp.where(i > 0, pc_ref[...].astype(F32) * ph_ref[...].astype(F32), 0.0)
        qm1 = _shift_down(q, 1, [qp[HALO - 1:HALO]])
        qm2 = _shift_down(q, 2, [qp[HALO - 2:HALO - 1], qp[HALO - 1:HALO]])
        cv = w0 * qm2 + w1 * qm1 + w2 * q

        def conv_out_grad(zb_, cv_, dout_):
            yb = zb_ * cv_
            rb = lax.rsqrt(jnp.mean(yb * yb, axis=-1, keepdims=True) + RMS_EPS)
            ybh = yb * rb
            gb = dout_ * ggb
            dyb = rb * (gb - ybh * jnp.mean(gb * ybh, axis=-1, keepdims=True))
            return dyb * zb_, dyb * cv_, ybh

        db_out = dy_ref[:, a:2 * a].astype(F32)
        g, dzb, ybh = conv_out_grad(zb, cv, db_out)
        dgg_ref[:, a:2 * a] += jnp.sum(db_out * ybh, axis=0, keepdims=True)
        dz_ref[:, 2 * a:3 * a] = dzb.astype(BF16)
        qn = nc_ref[...].astype(F32) * nh_ref[...].astype(F32)
        zbn = nb_ref[...].astype(F32)
        cvn = w0 * _shift_down(qn, 2, [q[tm - 2:tm - 1], q[tm - 1:tm]]) + w1 * _shift_down(qn, 1, [q[tm - 1:tm]]) + w2 * qn
        gn, _, _ = conv_out_grad(zbn, cvn, ndy_ref[...].astype(F32))
        gn = jnp.where(i < n_tiles - 1, gn, 0.0)
        dq = w2 * g + w1 * _shift_up(g, 1, [gn[0:1]]) + w0 * _shift_up(g, 2, [gn[0:1], gn[1:2]])
        dz_ref[:, 3 * a:4 * a] = (dq * zh).astype(BF16)
        dz_ref[:, 4 * a:5 * a] = (dq * zc).astype(BF16)
        dcw_ref[0:1, :] += jnp.sum(g * qm2, axis=0, keepdims=True)
        dcw_ref[1:2, :] += jnp.sum(g * qm1, axis=0, keepdims=True)
        dcw_ref[2:3, :] += jnp.sum(g * q, axis=0, keepdims=True)

        @pl.when(i == n_tiles - 1)
        def _():
            for h in range(HEADS):
                dbb_ref[h] = jnp.broadcast_to(jnp.sum(dbb_ref[h], axis=1, keepdims=True), (BLK, hd))
                dws_ref[h] = _masked_ws(dws_ref, h)

    full = lambda shape: pl.BlockSpec(tuple(shape), lambda i: (0,) * len(shape))
    out_shapes = (_sds((t, 5 * a), BF16), _sds((1, a), F32), _sds((1, a), F32), _sds(w_spatial.shape, F32),
                  _sds(bb.shape, F32), _sds((8, a), F32), _sds((1, 2 * a), F32))
    return _call_hosting(
        body, "mixer_bwd", (n_tiles,), out_shapes,
        [tile, *prev, *nxt, dy_tile, dy_next, full((1, a)), full((1, a)), full(w_spatial.shape), full(bb.shape),
         full(conv_w.shape), full((1, 2 * a))],
        [tile, *[full(s.shape) for s in out_shapes[1:]]], [z, z, z, z, z, z, dy, dy, ln_g, ln_b, w_spatial, bb, conv_w, gg],
        [pltpu.VMEM((tm, a), F32)] * 4, comm)


def _all_reduce_small(pack, comm=None):
    r = pack.shape[0]
    hosted = _Hosted(comm, 1, 1)
    n_ci, n_co = len(hosted.operands), len(hosted.out_shapes)

    def body(*refs):
        in_ref, c_ins, out_ref, c_outs = refs[0], refs[1:1 + n_ci], refs[1 + n_ci], refs[2 + n_ci:2 + n_ci + n_co]
        acc_ref, recv_ref, send_sems, recv_sems = refs[2 + n_ci + n_co:6 + n_ci + n_co]
        sems = refs[6 + n_ci + n_co:]
        hosted.run("start", c_ins, c_outs, sems)
        x, y, c = _place()
        partners = [(x, y, 1 - c), (1 - x, y, c), (x, 1 - y, c)]
        acc_ref[0] = in_ref[...]
        for s, partner in enumerate(partners):
            cp = pltpu.make_async_remote_copy(
                src_ref=acc_ref.at[s], dst_ref=recv_ref.at[s], send_sem=send_sems.at[s], recv_sem=recv_sems.at[s],
                device_id=partner, device_id_type=MESH)
            cp.start()
            cp.wait()
            if s < 2:
                acc_ref[s + 1] = acc_ref[s] + recv_ref[s]
            else:
                out_ref[...] = acc_ref[s] + recv_ref[s]
        for stage in ("mid1", "mid2", "finish"):
            hosted.run(stage, c_ins, c_outs, sems)

    vmem = pl.BlockSpec(memory_space=pltpu.VMEM)
    res = pl.pallas_call(
        body, name="all_reduce_small", out_shape=tuple([_sds(pack.shape, F32)] + hosted.out_shapes),
        in_specs=[vmem] + hosted.in_specs, out_specs=tuple([vmem] + hosted.out_specs),
        input_output_aliases=hosted.aliases,
        scratch_shapes=[pltpu.VMEM((3, r, 128), F32), pltpu.VMEM((3, r, 128), F32), pltpu.SemaphoreType.DMA((3,)),
                        pltpu.SemaphoreType.DMA((3,))] + hosted.scratch,
        compiler_params=pltpu.CompilerParams(vmem_limit_bytes=VMEM_LIMIT_V7X),
    )(pack, *hosted.operands)
    return res[0], list(res[1:])


def _adamw_math(w, g, m, v):
    m = ADAM_B1 * m + (1.0 - ADAM_B1) * g
    v = ADAM_B2 * v + (1.0 - ADAM_B2) * (g * g)
    m_hat = m / (1.0 - ADAM_B1 ** ADAM_STEP)
    v_hat = v / (1.0 - ADAM_B2 ** ADAM_STEP)
    delta = -ADAM_LR * (m_hat / (jnp.sqrt(v_hat) + ADAM_EPS) + ADAM_WD * w)
    return delta, m, v


def _adamw_big(name, land, w, m, v, comm=None):
    nl, n_slots, r, c = land.shape
    tr = max(8, min(r, (256 * 640) // c // 8 * 8))
    while r % tr:
        tr -= 8
    grid = (nl, r // tr)
    hosted = _Hosted(comm, 4, 4)
    n_ci, n_co = len(hosted.operands), len(hosted.out_shapes)

    def body(*refs):
        land_ref, w_ref, m_ref, v_ref = refs[:4]
        c_ins = refs[4:4 + n_ci]
        g_out, d_out, m_out, v_out = refs[4 + n_ci:8 + n_ci]
        c_outs = refs[8 + n_ci:8 + n_ci + n_co]
        sems = refs[8 + n_ci + n_co:]

        def compute():
            g = land_ref[0].astype(F32)
            for s in range(1, n_slots):
                g = g + land_ref[s].astype(F32)
            delta, mn, vn = _adamw_math(w_ref[...], g, m_ref[...], v_ref[...])
            g_out[...] = g
            d_out[...] = delta
            m_out[...] = mn
            v_out[...] = vn

        hosted.wrap(grid, compute, c_ins, c_outs, sems)

    blk = pl.BlockSpec((None, tr, c), lambda l, i: (l, i, 0))
    res = pl.pallas_call(
        body, name=name, grid=grid, out_shape=tuple([_sds((nl, r, c), F32)] * 4 + hosted.out_shapes),
        in_specs=[pl.BlockSpec((None, n_slots, tr, c), lambda l, i: (l, 0, i, 0)), blk, blk, blk] + hosted.in_specs,
        out_specs=tuple([blk] * 4 + hosted.out_specs), input_output_aliases=hosted.aliases,
        scratch_shapes=hosted.scratch, compiler_params=_cparams(2))(land, w, m, v, *hosted.operands)
    return list(res[:4]), list(res[4:])


def _adamw_small(gs, ws, ms, vs):
    n = len(gs)

    def body(*refs):
        g_refs, w_refs, m_refs, v_refs = refs[:n], refs[n:2 * n], refs[2 * n:3 * n], refs[3 * n:4 * n]
        d_outs, m_outs, v_outs = refs[4 * n:5 * n], refs[5 * n:6 * n], refs[6 * n:7 * n]
        for i in range(n):
            delta, mn, vn = _adamw_math(w_refs[i][...], g_refs[i][...], m_refs[i][...], v_refs[i][...])
            d_outs[i][...] = delta
            m_outs[i][...] = mn
            v_outs[i][...] = vn

    shapes = [_sds(g.shape, F32) for g in gs]
    res = pl.pallas_call(body, name="adamw_small", out_shape=tuple(shapes * 3),
                         compiler_params=pltpu.CompilerParams(vmem_limit_bytes=VMEM_LIMIT_V7X))(*gs, *ws, *ms, *vs)
    return list(res[:n]), list(res[n:2 * n]), list(res[2 * n:])


def _rows(a):
    return a.reshape(-1, 128)


BIG = ["w_in", "w_out", "w_gate", "w_up", "w_down"]
AG_HOSTS = {
    ("norm1", 0): [("w_in", 0), ("conv_w", 0)],
    ("mm_in", 0): [("w_out", 0), ("w_gate", 0, 0, 2)], ("mixer", 0): [("w_gate", 0, 1, 2)],
    ("mm_out", 0): [("w_up", 0, 0, 2)], ("norm2", 0): [("w_up", 0, 1, 2)],
    ("mm_swiglu", 0): [("w_down", 0), ("w_in", 1)], ("mm_down", 0): [("w_out", 1), ("w_gate", 1, 0, 2)],
    ("mm_in", 1): [("w_gate", 1, 1, 2)], ("mixer", 1): [("w_up", 1, 0, 2)], ("mm_out", 1): [("w_up", 1, 1, 2)],
    ("mm_swiglu", 1): [("w_down", 1)],
}


def kernel(x, norm1_g, w_in, gmlp_ln_g, gmlp_ln_b, w_spatial, b_spatial, conv_w, group_norm_g, w_out, norm2_g, w_gate, w_up, w_down, final_norm_g, loss_target, m_norm1_g, m_w_in, m_gmlp_ln_g, m_gmlp_ln_b, m_w_spatial, m_b_spatial, m_conv_w, m_group_norm_g, m_w_out, m_norm2_g, m_w_gate, m_w_up, m_w_down, m_final_norm_g, v_norm1_g, v_w_in, v_gmlp_ln_g, v_gmlp_ln_b, v_w_spatial, v_b_spatial, v_conv_w, v_group_norm_g, v_w_out, v_norm2_g, v_w_gate, v_w_up, v_w_down, v_final_norm_g):
    nl = N_LAYERS
    t, d = x.shape[1], x.shape[2]
    a = d // 2
    hd = a // HEADS
    xin = x.reshape(t, d)
    target = loss_target.reshape(t, d)
    me = _index(_place())

    tr = lambda w: jnp.transpose(w, (0, 2, 1))
    big = {"w_in": w_in, "w_out": w_out, "w_gate": tr(w_gate), "w_up": tr(w_up), "w_down": w_down}
    big_m = {"w_in": m_w_in, "w_out": m_w_out, "w_gate": tr(m_w_gate), "w_up": tr(m_w_up), "w_down": m_w_down}
    big_v = {"w_in": v_w_in, "w_out": v_w_out, "w_gate": tr(v_w_gate), "w_up": tr(v_w_up), "w_down": v_w_down}
    block = {k: big[k].shape[1:] for k in BIG}
    view = {k: _cols_view(block[k][1]) if k == "w_in" else _rows_view(block[k][0]) for k in BIG}
    full_shape = {k: (block[k][0], N_DEV * block[k][1]) if k == "w_in" else (N_DEV * block[k][0], block[k][1])
                  for k in BIG}

    weights = {}
    shards = {(k, l): big[k][l].astype(BF16) for k in BIG for l in range(nl)}

    def ag_spec(k, l, part=0, n_parts=1):
        if k == "conv_w":
            return (conv_w, _sds((N_DEV, *conv_w.shape), F32), _SLOT_WHOLE, (0,), None)
        halves = (_cols_halves(*block[k], part, n_parts) if k == "w_in" else _rows_halves(block[k][0], part, n_parts))
        return (shards[(k, l)], _sds(full_shape[k], BF16), halves, (0, 1), weights.get((k, l)))

    bb = jnp.broadcast_to(b_spatial[..., None], (nl, HEADS, BLK, hd))

    def hosted(name, l):
        keys = AG_HOSTS.get((name, l), [])
        return keys, ([_ag_piece([ag_spec(*key) for key in keys])] if keys else None)

    def landed(keys, couts):
        for key, arr in zip(keys, couts):
            weights[key[:2]] = arr

    saved = []
    xl = xin
    for l in range(nl):
        keys, comm = hosted("norm1", l)
        h, couts = _rmsnorm_fwd(xl, norm1_g[l:l + 1], comm)
        landed(keys, couts)
        if l == 0:
            conv_full = jnp.transpose(weights[("conv_w", 0)], (1, 2, 0, 3)).reshape(nl, 3, a)
        keys, comm = hosted("mm_in", l)
        z, couts = _mm_in(h, weights[("w_in", l)], comm)
        landed(keys, couts)
        keys, comm = hosted("mixer", l)
        y, couts = _mixer_fwd(z, gmlp_ln_g[l:l + 1], gmlp_ln_b[l:l + 1], w_spatial[l], bb[l], conv_full[l],
                              group_norm_g[l:l + 1], comm)
        landed(keys, couts)
        keys, comm = hosted("mm_out", l)
        x1, couts = _mm_out(y, weights[("w_out", l)], xl, comm)
        landed(keys, couts)
        keys, comm = hosted("norm2", l)
        h2, couts = _rmsnorm_fwd(x1, norm2_g[l:l + 1], comm)
        landed(keys, couts)
        keys, comm = hosted("mm_swiglu", l)
        (act, dact_dgate, dact_dup), couts = _mm_swiglu(h2, weights[("w_gate", l)], weights[("w_up", l)], comm)
        landed(keys, couts)
        keys, comm = hosted("mm_down", l)
        x2, couts = _mm_down(act, weights[("w_down", l)], x1, comm)
        landed(keys, couts)
        saved.append(dict(x=xl, h=h, z=z, y=y, x1=x1, h2=h2, dact_dgate=dact_dgate, dact_dup=dact_dup, act=act))
        xl = x2

    dx, dxb, d_final_g, loss_part = _loss_head(xl, final_norm_g.reshape(1, d), target)
    small = [None] * nl
    core = lax.axis_index("c").astype(jnp.int32).reshape(1)
    in_rows = block["w_in"][0]
    part_of = {"w_in_a": ("w_in", 0), "w_in_b": ("w_in", 3 * in_rows // 4)}
    block["w_in_a"], block["w_in_b"] = (3 * in_rows // 4, block["w_in"][1]), (in_rows // 4, block["w_in"][1])
    for k in part_of:
        view[k] = view["w_in"]
    stage_shape = {k: _sds((N_CHIPS, *block[k]), BF16) for k in block}
    land_shape = {k: _sds((nl, N_CHIPS, *block[k]), BF16) for k in BIG}
    grads = [dict() for _ in range(nl)]
    stages = [dict() for _ in range(nl)]
    sums = [dict() for _ in range(nl)]
    lands = {k: None for k in BIG}

    def core_job(l, keys):
        def sink(outs):
            stages[l].update(zip(keys, outs))
        return _rs_core_piece([(grads[l][k], stage_shape[k], view[k]) for k in keys]), sink

    def chip_job(l, items):
        keys = [part_of.get(item[0], (item[0], 0))[0] for item in items]

        def rows(k, p0, p1, n_parts):
            per = block[k][0] // n_parts
            landing = part_of.get(k, (k, 0))[1]
            return (p0 * per, landing + p0 * per, (p1 - p0) * per)

        def sink(outs):
            lands.update(zip(keys, outs))
        return _rs_chip_piece([(sums[l][k], land_shape[key], rows(k, p0, p1, n_parts), lands[key])
                               for key, (k, p0, p1, n_parts) in zip(keys, items)], l), sink

    def add_up(l, keys):
        for k in keys:
            sums[l][k] = _chip_sums(f"chip_sums_{k}", grads[l][k], stages[l][k], k.startswith("w_in"), core)

    def host(*jobs):
        def deliver(couts):
            i = 0
            for piece, sink in jobs:
                n_out = len(piece.out_shapes)
                sink(couts[i:i + n_out])
                i += n_out
        return [piece for piece, _ in jobs], deliver

    whole = lambda k: (k, 0, 1, 1)
    rep = ["norm1_g", "gmlp_ln_g", "gmlp_ln_b", "w_spatial", "b_spatial", "group_norm_g", "norm2_g"]
    rep_w = dict(norm1_g=norm1_g, gmlp_ln_g=gmlp_ln_g, gmlp_ln_b=gmlp_ln_b, w_spatial=w_spatial, b_spatial=b_spatial,
                 group_norm_g=group_norm_g, norm2_g=norm2_g)
    rep_m = dict(norm1_g=m_norm1_g, gmlp_ln_g=m_gmlp_ln_g, gmlp_ln_b=m_gmlp_ln_b, w_spatial=m_w_spatial,
                 b_spatial=m_b_spatial, group_norm_g=m_group_norm_g, norm2_g=m_norm2_g)
    rep_v = dict(norm1_g=v_norm1_g, gmlp_ln_g=v_gmlp_ln_g, gmlp_ln_b=v_gmlp_ln_b, w_spatial=v_w_spatial,
                 b_spatial=v_b_spatial, group_norm_g=v_group_norm_g, norm2_g=v_norm2_g)

    def small_grad_parts():
        parts = [_rows(jnp.stack([small[l][k].reshape(rep_w[k].shape[1:]) for l in range(nl)])) for k in rep]
        parts.append(_rows(d_final_g))
        parts.append(_rows(jnp.stack([small[l]["conv_w"] for l in range(nl)])))
        parts.append(jnp.broadcast_to(loss_part, (8, 128)))
        rows = sum(p.shape[0] for p in parts)
        parts.append(jnp.zeros((-rows % 16, 128), F32))
        return parts

    for l in reversed(range(nl)):
        s = saved[l]
        wi, wo, wgt, wut, wd = [weights[(k, l)] for k in BIG]
        later = l + 1 < nl
        comm, deliver = host(chip_job(l + 1, [("w_in", 0, 1, 2)])) if later else host()
        (grads[l]["w_down"],), couts = _mm_dw("mm_dw_down", [s["act"]], dxb, 2816, 1024, comm)
        deliver(couts)
        comm, deliver = (host(core_job(l, ["w_down"]), chip_job(l + 1, [("w_in", 1, 2, 2)])) if later
                         else host(core_job(l, ["w_down"])))
        (dgate, dup), couts = _mm_dact(dxb, wd, s["dact_dgate"], s["dact_dup"], comm)
        deliver(couts)
        add_up(l, ["w_down"])
        comm, deliver = host(chip_job(l, [("w_down", 0, 3, 4)]))
        (grads[l]["w_gate"],), couts = _mm_dw("mm_dw_gate", [dgate], s["h2"], 2816, 1024, comm)
        deliver(couts)
        comm, deliver = host(chip_job(l, [("w_down", 3, 4, 4)]), core_job(l, ["w_gate"]))
        (grads[l]["w_up"],), couts = _mm_dw("mm_dw_up", [dup], s["h2"], 2816, 1024, comm)
        deliver(couts)
        add_up(l, ["w_gate"])
        comm, deliver = host(chip_job(l, [whole("w_gate")]), core_job(l, ["w_up"]))
        dh2, couts = _mm_dh2(dgate, dup, wgt, wut, comm)
        deliver(couts)
        add_up(l, ["w_up"])
        dx1, dx1b, d_n2 = _rmsnorm_bwd(s["x1"], norm2_g[l:l + 1], dh2, dx)
        comm, deliver = host(chip_job(l, [("w_up", 0, 1, 4)]))
        dy, couts = _mm_dy(dx1b, wo, comm)
        deliver(couts)
        comm, deliver = host(chip_job(l, [("w_up", 1, 2, 4)]))
        (grads[l]["w_out"],), couts = _mm_dw("mm_dw_out", [s["y"]], dx1b, 1024, 1024, comm)
        deliver(couts)
        comm, deliver = host(chip_job(l, [("w_up", 2, 4, 4)]), core_job(l, ["w_out"]))
        (dz, d_lng, d_lnb, d_ws, d_bb, d_cw, d_gg), couts = _mixer_bwd(
            s["z"], dy, gmlp_ln_g[l:l + 1], gmlp_ln_b[l:l + 1], w_spatial[l], bb[l], conv_full[l], group_norm_g[l:l + 1],
            comm)
        deliver(couts)
        add_up(l, ["w_out"])
        small[l] = dict(norm1_g=jnp.zeros((1, d), F32), gmlp_ln_g=d_lng, gmlp_ln_b=d_lnb, w_spatial=d_ws,
                        b_spatial=d_bb[:, :, 0], group_norm_g=d_gg, norm2_g=d_n2, conv_w=d_cw[0:3])
        if l > 0:
            comm, deliver = host(chip_job(l, [whole("w_out")]))
            (grads[l]["w_in"],), couts = _mm_dw("mm_dw_in", [s["h"]], dz, 2048, 1024, comm)
            deliver(couts)
            comm, deliver = host(core_job(l, ["w_in"]))
            dh, couts = _mm_dh(dz, wi, comm)
            deliver(couts)
            add_up(l, ["w_in"])
        else:
            parts = small_grad_parts()
            reduced = []
            comm, deliver = host(chip_job(l, [whole("w_out")]),
                                 (_all_reduce_piece(jnp.concatenate(parts, axis=0)), reduced.extend))
            (grads[l]["w_in_a"],), couts = _mm_dw("mm_dw_in_a", [s["h"]], dz, block["w_in_a"][0], 1024, comm,
                                                  m_rows=(0, block["w_in_a"][0]))
            deliver(couts)
            comm, deliver = host(core_job(l, ["w_in_a"]))
            (grads[l]["w_in_b"],), couts = _mm_dw("mm_dw_in_b", [s["h"]], dz, block["w_in_b"][0], 2560, comm,
                                                  m_rows=(block["w_in_a"][0], block["w_in_b"][0]))
            deliver(couts)
            add_up(l, ["w_in_a"])
            comm, deliver = host(chip_job(l, [whole("w_in_a")]), core_job(l, ["w_in_b"]))
            dh, couts = _mm_dh(dz, wi, comm)
            deliver(couts)
            add_up(l, ["w_in_b"])
        dx, dxb, small[l]["norm1_g"] = _rmsnorm_bwd(s["x"], norm1_g[l:l + 1], dh, dx1, with_bf16=l > 0)
    grad_x = dx.reshape(x.shape)

    sizes = [p.shape[0] for p in parts]
    comm, deliver = host(chip_job(0, [whole("w_in_b")]))
    last, couts = _all_reduce_small(_rows(small[0]["norm1_g"]), comm)
    deliver(couts)
    total = lax.dynamic_update_slice(reduced[0], last, (0, 0))
    offs = [0]
    for n in sizes:
        offs.append(offs[-1] + n)
    pieces = [total[offs[i]:offs[i + 1]] for i in range(len(parts))]
    loss = pieces[len(rep) + 2][0, 0]
    conv_g_full = pieces[len(rep) + 1].reshape(nl, 3, N_DEV, a // N_DEV)
    conv_g = lax.dynamic_index_in_dim(conv_g_full, me, axis=2, keepdims=False)
    names = rep + ["final_norm_g", "conv_w"]
    flat = lambda w: w.reshape(-1, w.shape[-1])
    small_w = [flat(rep_w[k]) for k in rep] + [flat(final_norm_g), flat(conv_w)]
    small_m = [flat(rep_m[k]) for k in rep] + [flat(m_final_norm_g), flat(m_conv_w)]
    small_v = [flat(rep_v[k]) for k in rep] + [flat(v_final_norm_g), flat(v_conv_w)]
    small_g = [pieces[i].reshape(small_w[i].shape) for i in range(len(rep) + 1)] + [flat(conv_g)]
    small_d, small_m, small_v = _adamw_small(small_g, small_w, small_m, small_v)
    shape_of = dict(rep_w, final_norm_g=final_norm_g, conv_w=conv_w)
    named = lambda arrays: {k: arr.reshape(shape_of[k].shape) for k, arr in zip(names, arrays)}
    res = {"grad": named(small_g), "delta": named(small_d), "m": named(small_m), "v": named(small_v)}

    for k in BIG:
        outs, _ = _adamw_big(f"adamw_{k}", lands[k], big[k], big_m[k], big_v[k])
        if k in ("w_gate", "w_up"):
            outs = [tr(o) for o in outs]
        res["grad"][k], res["delta"][k], res["m"][k], res["v"][k] = outs

    order = ["norm1_g", "w_in", "gmlp_ln_g", "gmlp_ln_b", "w_spatial", "b_spatial", "conv_w", "group_norm_g", "w_out",
             "norm2_g", "w_gate", "w_up", "w_down", "final_norm_g"]
    return (loss, grad_x, *[res["grad"][k] for k in order], *[res["delta"][k] for k in order],
            *[res["m"][k] for k in order], *[res["v"][k] for k in order])
```

```python
import functools
import math
import operator

import jax
import jax.numpy as jnp
from jax import lax
from jax.experimental import pallas as pl
from jax.experimental.pallas import tpu as pltpu

F32 = jnp.float32
BF16 = jnp.bfloat16
MESH = pl.DeviceIdType.MESH

N_DEV = 8
N_LAYERS = 2
HEADS = 8
BLK = 128
CHUNK = 64
HALO = 16
RMS_EPS = 1e-6
LN_EPS = 1e-5
ADAM_LR, ADAM_B1, ADAM_B2, ADAM_EPS, ADAM_WD, ADAM_STEP = 0.001, 0.9, 0.999, 1e-8, 0.01, 10
GELU_C = math.sqrt(2.0 / math.pi)
GELU_A = 0.044715

VMEM_LIMIT_V7X = 56 * 1024 * 1024
_TM = 1024
_TN = 1024
_TT = 1024
_TM_MIX = 256
_TM_NORM = 512


def _cparams(n_axes):
    return pltpu.CompilerParams(dimension_semantics=("arbitrary",) * n_axes, vmem_limit_bytes=VMEM_LIMIT_V7X)


def _sds(shape, dtype):
    return jax.ShapeDtypeStruct(tuple(shape), dtype)


def _place():
    return lax.axis_index("x"), lax.axis_index("y"), lax.axis_index("c")


def _index(place):
    return 4 * place[0] + 2 * place[1] + place[2]


class _Piece:
    def __init__(self, operands, out_shapes, aliases, n_sems, start, finish, mid1=None, mid2=None, vmem=(),
                 hooks=(0.6, 0.87)):
        self.operands, self.out_shapes, self.aliases, self.n_sems = list(operands), list(out_shapes), dict(aliases), n_sems
        self.vmem = list(vmem)
        self.hooks = hooks
        nothing = lambda ctx: None
        self.start, self.mid1, self.mid2, self.finish = start, mid1 or nothing, mid2 or nothing, finish


class _Ctx:
    def __init__(self, ins, outs, sems, offs):
        self.ins, self.outs, self.sems = ins, outs, sems
        self.o_in, self.o_out, self.o_send, self.o_recv, self.o_loc, self.o_vmem = offs

    def vmem(self, i):
        return self.sems[3 + self.o_vmem + i]

    def inp(self, i):
        return self.ins[self.o_in + i]

    def out(self, i):
        return self.outs[self.o_out + i]

    def send(self, k):
        return self.sems[0].at[self.o_send + k]

    def recv(self, k):
        return self.sems[1].at[self.o_recv + k]

    def local(self, k):
        return self.sems[2].at[self.o_loc + k]


class _Hosted:
    def __init__(self, pieces, n_in_before, n_out_before):
        self.pieces = [p for p in (pieces or []) if p is not None]
        self.operands, self.out_shapes, self.aliases, self.offs = [], [], {}, []
        counts, vmem = [0, 0, 0], []
        for p in self.pieces:
            self.offs.append((len(self.operands), len(self.out_shapes), *counts, len(vmem)))
            for i, j in p.aliases.items():
                self.aliases[n_in_before + len(self.operands) + i] = n_out_before + len(self.out_shapes) + j
            self.operands += p.operands
            self.out_shapes += p.out_shapes
            counts = [c + n for c, n in zip(counts, p.n_sems)]
            vmem += p.vmem
        hbm = pl.BlockSpec(memory_space=pl.ANY)
        self.in_specs = [hbm] * len(self.operands)
        self.out_specs = [hbm] * len(self.out_shapes)
        self.scratch = ([pltpu.SemaphoreType.DMA((max(c, 1),)) for c in counts] + vmem) if self.pieces else []

    def run(self, stage, ins, outs, sems):
        for p, offs in zip(self.pieces, self.offs):
            getattr(p, stage)(_Ctx(ins, outs, sems, offs))

    def wrap(self, grid, compute, ins, outs, sems):
        if not self.pieces:
            compute()
            return
        n_steps = math.prod(grid)
        lin = 0
        for ax, g in enumerate(grid):
            lin = lin * g + pl.program_id(ax)
        pl.when(lin == 0)(lambda: self.run("start", ins, outs, sems))
        compute()
        for stage, which in (("mid1", 0), ("mid2", 1)):
            for p, offs in zip(self.pieces, self.offs):
                at = min(n_steps - 1, int(p.hooks[which] * n_steps))
                pl.when(lin == at)(functools.partial(getattr(p, stage), _Ctx(ins, outs, sems, offs)))
        pl.when(lin == n_steps - 1)(lambda: self.run("finish", ins, outs, sems))


def _cols_view(width):
    return lambda ref, p: ref.at[:, pl.ds(pl.multiple_of(p * width, 128), width)]


def _rows_view(height):
    return lambda ref, p: ref.at[pl.ds(pl.multiple_of(p * height, 16), height), :]


def _cols_halves(rows, width, part, n_parts):
    hr = rows // n_parts // 2
    at = lambda h: pl.ds(part * 2 * hr + h * hr, hr)
    return (lambda ref, p, h: ref.at[at(h), pl.ds(pl.multiple_of(p * width, 128), width)],
            lambda ref, h: ref.at[at(h), :], 2)


def _rows_halves(height, part, n_parts):
    hh = height // n_parts // 2
    return (lambda ref, p, h: ref.at[pl.ds(pl.multiple_of(p * height + part * 2 * hh + h * hh, 16), hh), :],
            lambda ref, h: ref.at[pl.ds(part * 2 * hh + h * hh, hh), :], 2)


_SLOT_WHOLE = (lambda ref, p, h: ref.at[p], lambda ref, h: ref, 1)


def _ag_piece(specs):
    units = [(a, h) for a, s in enumerate(specs) for h in s[3]]

    def plan(ctx):
        x, y, c = _place()
        me, sib, xn, yn, dg = (x, y, c), (x, y, 1 - c), (1 - x, y, c), (x, 1 - y, c), (1 - x, 1 - y, c)

        def copy(u, k, block, to, from_shard=False):
            a, h = units[u]
            dst_of, src_of, _ = specs[a][2]
            dst = dst_of(ctx.out(a), _index(block), h)
            return pltpu.make_async_remote_copy(
                src_ref=src_of(ctx.inp(a), h) if from_shard else dst, dst_ref=dst, send_sem=ctx.send(7 * u + k),
                recv_sem=ctx.recv(7 * u + k), device_id=to, device_id_type=MESH)

        def local(u):
            a, h = units[u]
            dst_of, src_of, _ = specs[a][2]
            return pltpu.make_async_copy(src_of(ctx.inp(a), h), dst_of(ctx.out(a), _index(me), h), ctx.local(u))

        def relay(u):
            return copy(u, 3, xn, yn) if units[u][1] % 2 == 0 else copy(u, 3, yn, xn)

        return me, sib, xn, yn, dg, c, copy, local, relay

    def start(ctx):
        me, sib, xn, yn, dg, c, copy, local, relay = plan(ctx)
        for u in range(len(units)):
            local(u).start()
            for k, to in enumerate((sib, xn, yn)):
                copy(u, k, me, to, from_shard=True).start()

    def mid1(ctx):
        me, sib, xn, yn, dg, c, copy, local, relay = plan(ctx)
        for u in range(len(units)):
            copy(u, 1, xn, me).wait_recv()
            copy(u, 2, yn, me).wait_recv()
            relay(u).start()
            copy(u, 4, xn, sib).start()
            copy(u, 5, yn, sib).start()

    def mid2(ctx):
        me, sib, xn, yn, dg, c, copy, local, relay = plan(ctx)
        for u in range(len(units)):
            copy(u, 3, dg, me).wait_recv()
            copy(u, 6, dg, sib).start()

    def finish(ctx):
        me, sib, xn, yn, dg, c, copy, local, relay = plan(ctx)
        other = lambda place: (place[0], place[1], 1 - c)
        for u in range(len(units)):
            for k, block in ((0, sib), (4, other(xn)), (5, other(yn)), (6, other(dg))):
                copy(u, k, block, me).wait_recv()
        for u in range(len(units)):
            for k, to in enumerate((sib, xn, yn)):
                copy(u, k, me, to, from_shard=True).wait_send()
            relay(u).wait_send()
            for k, block in ((4, xn), (5, yn), (6, dg)):
                copy(u, k, block, sib).wait_send()
            local(u).wait()

    n_u = len(units)
    operands, aliases = [s[0] for s in specs], {}
    for a, spec in enumerate(specs):
        if spec[4] is not None:
            aliases[len(operands)] = a
            operands.append(spec[4])
    return _Piece(operands, [s[1] for s in specs], aliases, (7 * n_u, 7 * n_u, n_u), start, finish, mid1, mid2)


N_CHIPS = 4


def _rs_core_piece(specs):
    n = len(specs)

    def copies(ctx):
        x, y, c = _place()
        out = []
        for a in range(n):
            for q in range(N_CHIPS):
                out.append(pltpu.make_async_remote_copy(
                    src_ref=specs[a][2](ctx.inp(a), 2 * q + (1 - c)), dst_ref=ctx.out(a).at[q],
                    send_sem=ctx.send(N_CHIPS * a + q), recv_sem=ctx.recv(N_CHIPS * a + q), device_id=(x, y, 1 - c),
                    device_id_type=MESH))
        return out

    def start(ctx):
        for cp in copies(ctx):
            cp.start()

    def finish(ctx):
        for cp in copies(ctx):
            cp.wait_recv()
            cp.wait_send()

    return _Piece([s[0] for s in specs], [s[1] for s in specs], {}, (N_CHIPS * n, N_CHIPS * n, 0), start, finish)


def _rs_chip_piece(specs, layer):
    n = len(specs)
    hops = [(1, 0), (0, 1), (1, 1)]

    def copies(ctx):
        x, y, c = _place()
        mine = 2 * x + y
        out = []
        for a in range(n):
            first, landing, size = specs[a][2]
            rows, to = pl.ds(first, size), pl.ds(landing, size)
            sums, land = ctx.inp(a), ctx.out(a)
            out.append((pltpu.make_async_copy(sums.at[mine, rows], land.at[layer, mine, to], ctx.local(a)), None))
            for j, (dx, dy) in enumerate(hops):
                px, py = x ^ dx, y ^ dy
                peer = 2 * px + py
                send = pltpu.make_async_remote_copy(
                    src_ref=sums.at[peer, rows], dst_ref=land.at[layer, mine, to], send_sem=ctx.send(3 * a + j),
                    recv_sem=ctx.recv(3 * a + j), device_id=(px, py, c), device_id_type=MESH)
                recv = pltpu.make_async_remote_copy(
                    src_ref=sums.at[peer, rows], dst_ref=land.at[layer, peer, to], send_sem=ctx.send(3 * a + j),
                    recv_sem=ctx.recv(3 * a + j), device_id=(px, py, c), device_id_type=MESH)
                out.append((send, recv))
        return out

    def start(ctx):
        for send, _ in copies(ctx):
            send.start()

    def finish(ctx):
        for send, recv in copies(ctx):
            if recv is None:
                send.wait()
            else:
                recv.wait_recv()
                send.wait_send()

    operands, aliases = [s[0] for s in specs], {}
    for a, spec in enumerate(specs):
        if spec[3] is not None:
            aliases[len(operands)] = a
            operands.append(spec[3])
    return _Piece(operands, [s[1] for s in specs], aliases, (3 * n, 3 * n, n), start, finish)


def _all_reduce_piece(pack):
    r = pack.shape[0]
    half = r // 2

    def plan(ctx):
        x, y, c = _place()
        acc, got = ctx.vmem(0), ctx.vmem(1)
        mine = pl.ds(pl.multiple_of(c * half, 8), half)
        sib = (x, y, 1 - c)
        copies = [
            pltpu.make_async_remote_copy(src_ref=acc.at[0], dst_ref=got.at[0], send_sem=ctx.send(0), recv_sem=ctx.recv(0),
                                         device_id=sib, device_id_type=MESH),
            pltpu.make_async_remote_copy(src_ref=acc.at[1, mine], dst_ref=got.at[1, mine], send_sem=ctx.send(1),
                                         recv_sem=ctx.recv(1), device_id=(1 - x, y, c), device_id_type=MESH),
            pltpu.make_async_remote_copy(src_ref=acc.at[2, mine], dst_ref=got.at[2, mine], send_sem=ctx.send(2),
                                         recv_sem=ctx.recv(2), device_id=(x, 1 - y, c), device_id_type=MESH),
            pltpu.make_async_remote_copy(src_ref=acc.at[3, mine], dst_ref=acc.at[3, mine], send_sem=ctx.send(3),
                                         recv_sem=ctx.recv(3), device_id=sib, device_id_type=MESH),
        ]
        other = pl.ds(pl.multiple_of((1 - c) * half, 8), half)
        arrival = pltpu.make_async_remote_copy(src_ref=acc.at[3, other], dst_ref=acc.at[3, other], send_sem=ctx.send(3),
                                               recv_sem=ctx.recv(3), device_id=sib, device_id_type=MESH)
        return acc, got, mine, copies, arrival

    def start(ctx):
        acc, got, mine, copies, arrival = plan(ctx)
        load = pltpu.make_async_copy(ctx.inp(0), acc.at[0], ctx.local(0))
        load.start()
        load.wait()
        copies[0].start()

    def mid1(ctx):
        acc, got, mine, copies, arrival = plan(ctx)
        copies[0].wait()
        acc[1] = acc[0] + got[0]
        copies[1].start()

    def mid2(ctx):
        acc, got, mine, copies, arrival = plan(ctx)
        copies[1].wait()
        acc[2, mine] = acc[1, mine] + got[1, mine]
        copies[2].start()

    def finish(ctx):
        acc, got, mine, copies, arrival = plan(ctx)
        copies[2].wait()
        acc[3, mine] = acc[2, mine] + got[2, mine]
        copies[3].start()
        copies[3].wait_send()
        arrival.wait_recv()
        store = pltpu.make_async_copy(acc.at[3], ctx.out(0), ctx.local(0))
        store.start()
        store.wait()

    return _Piece([pack], [_sds(pack.shape, F32)], {}, (4, 4, 1), start, finish, mid1, mid2,
                  vmem=[pltpu.VMEM((4, r, 128), F32), pltpu.VMEM((3, r, 128), F32)], hooks=(0.25, 0.6))


def _chip_sums(name, grad, stage, by_cols, core):
    _, r, c = stage.shape
    tr = r
    while tr * c > 1024 * 1024 or r % tr or tr % 16:
        tr -= 16
    n_t = r // tr

    def body(core_ref, g_ref, s_ref, o_ref):
        o_ref[...] = (g_ref[...].astype(F32) + s_ref[...].astype(F32)).astype(BF16)

    if by_cols:
        gspec = pl.BlockSpec((tr, c), lambda q, i, core_ref: (i, 2 * q + core_ref[0]))
    else:
        gspec = pl.BlockSpec((tr, c), lambda q, i, core_ref: ((2 * q + core_ref[0]) * n_t + i, 0))
    sspec = pl.BlockSpec((None, tr, c), lambda q, i, core_ref: (q, i, 0))
    return pl.pallas_call(
        body, name=name, out_shape=_sds(stage.shape, BF16),
        grid_spec=pltpu.PrefetchScalarGridSpec(num_scalar_prefetch=1, grid=(N_CHIPS, n_t), in_specs=[gspec, sspec],
                                               out_specs=sspec),
        compiler_params=_cparams(2))(core, grad, stage)


def _call_hosting(body, name, grid, out_shapes, in_specs, out_specs, operands, scratch, comm):
    n_in, n_out, n_scr = len(operands), len(out_shapes), len(scratch)
    hosted = _Hosted(comm, n_in, n_out)
    n_ci, n_co = len(hosted.operands), len(hosted.out_shapes)

    def hosting_body(*refs):
        ins, rest = refs[:n_in], refs[n_in:]
        c_ins, rest = rest[:n_ci], rest[n_ci:]
        outs, rest = rest[:n_out], rest[n_out:]
        c_outs, rest = rest[:n_co], rest[n_co:]
        hosted.wrap(grid, lambda: body(*ins, *outs, *rest[:n_scr]), c_ins, c_outs, rest[n_scr:])

    res = pl.pallas_call(
        hosting_body, name=name, grid=grid, out_shape=tuple(list(out_shapes) + hosted.out_shapes),
        in_specs=list(in_specs) + hosted.in_specs, out_specs=tuple(list(out_specs) + hosted.out_specs),
        input_output_aliases=hosted.aliases, scratch_shapes=list(scratch) + hosted.scratch,
        compiler_params=_cparams(len(grid)))(*operands, *hosted.operands)
    return list(res[:n_out]), list(res[n_out:])


def _matmul(name, grid, nk, kaxis, pairs, dims, extras, outs, epilogue, sum_pairs, acc_shape, comm=None, split=None):
    n_p, n_e, n_o = len(pairs), len(extras), len(outs)
    n_acc = 0 if nk == 1 else (1 if sum_pairs else n_p)
    n_in = 2 * n_p + n_e
    hosted = _Hosted(comm, n_in, n_o)
    n_ci, n_co = len(hosted.operands), len(hosted.out_shapes)

    def body(*refs):
        a_refs = refs[0:2 * n_p:2]
        b_refs = refs[1:2 * n_p:2]
        e_refs = refs[2 * n_p:n_in]
        c_ins = refs[n_in:n_in + n_ci]
        o_refs = refs[n_in + n_ci:n_in + n_ci + n_o]
        c_outs = refs[n_in + n_ci + n_o:n_in + n_ci + n_o + n_co]
        acc_refs = refs[n_in + n_ci + n_o + n_co:n_in + n_ci + n_o + n_co + n_acc]
        sems = refs[n_in + n_ci + n_o + n_co + n_acc:]

        def dots():
            if sum_pairs and n_p > 1 and dims == NN:
                a_all = jnp.concatenate([a[...] for a in a_refs], axis=1)
                b_all = jnp.concatenate([b[...] for b in b_refs], axis=0)
                return [lax.dot_general(a_all, b_all, (dims, ((), ())), preferred_element_type=F32)]
            prods = [lax.dot_general(a[...], b[...], (dims, ((), ())), preferred_element_type=F32)
                     for a, b in zip(a_refs, b_refs)]
            if sum_pairs and n_p > 1:
                prods = [functools.reduce(operator.add, prods)]
            return prods

        def compute():
            if nk == 1 and split is not None:
                n_split, b_axis, n_row = split
                width = b_refs[0].shape[b_axis] // n_split
                height = a_refs[0].shape[0] // n_row
                for s in range(n_split):
                    cols = pl.ds(s * width, width)
                    for r in range(n_row):
                        rows = pl.ds(r * height, height)
                        epilogue([lax.dot_general(a[rows, :], b[cols, :] if b_axis == 0 else b[:, cols], (dims, ((), ())),
                                                  preferred_element_type=F32) for a, b in zip(a_refs, b_refs)],
                                 e_refs, o_refs, rows, cols)
                return
            if nk == 1:
                epilogue(dots(), e_refs, o_refs)
                return
            k = pl.program_id(kaxis)

            @pl.when(k == 0)
            def _():
                for acc, p in zip(acc_refs, dots()):
                    acc[...] = p

            if nk > 2:
                @pl.when((k > 0) & (k < nk - 1))
                def _():
                    for acc, p in zip(acc_refs, dots()):
                        acc[...] += p

            @pl.when(k == nk - 1)
            def _():
                epilogue([acc[...] + p for acc, p in zip(acc_refs, dots())], e_refs, o_refs)

        hosted.wrap(grid, compute, c_ins, c_outs, sems)

    operands, in_specs = [], []
    for a, a_spec, b, b_spec in pairs:
        operands += [a, b]
        in_specs += [a_spec, b_spec]
    for e, e_spec in extras:
        operands.append(e)
        in_specs.append(e_spec)
    res = pl.pallas_call(
        body, name=name, grid=grid,
        out_shape=tuple([o for o, _ in outs] + hosted.out_shapes),
        in_specs=in_specs + hosted.in_specs, out_specs=tuple([s for _, s in outs] + hosted.out_specs),
        input_output_aliases=hosted.aliases,
        scratch_shapes=[pltpu.VMEM(acc_shape, F32) for _ in range(n_acc)] + hosted.scratch,
        compiler_params=_cparams(len(grid)),
    )(*operands, *hosted.operands)
    return list(res[:n_o]), list(res[n_o:])


NN = ((1,), (0,))
NT = ((1,), (1,))
TN = ((0,), (0,))


def _tile(n, want):
    if n <= want:
        return n
    t = want // 128 * 128
    while n % t:
        t -= 128
    return t


def _silu_parts(g):
    s = 0.5 + 0.5 * jnp.tanh(0.5 * g)
    return s, g * s


def _mm_in(h, w_in, comm=None):
    t, d = h.shape
    n = w_in.shape[1]
    tm, tn = _tile(t, _TM), _tile(n, _TN)

    def epi(accs, e, o):
        o[0][...] = accs[0].astype(BF16)

    outs, couts = _matmul(
        "mm_in", (n // tn, t // tm), 1, None,
        [(h, pl.BlockSpec((tm, d), lambda j, i: (i, 0)), w_in, pl.BlockSpec((d, tn), lambda j, i: (0, j)))],
        NN, [], [(_sds((t, n), BF16), pl.BlockSpec((tm, tn), lambda j, i: (i, j)))], epi, True, None, comm)
    return outs[0], couts


def _mm_out(y, w_out, x, comm=None):
    t, m = y.shape
    d = w_out.shape[1]
    tm, tn = _tile(t, _TM), _tile(d, _TN)

    def epi(accs, e, o):
        o[0][...] = e[0][...] + accs[0]

    outs, couts = _matmul(
        "mm_out", (t // tm, d // tn), 1, None,
        [(y, pl.BlockSpec((tm, m), lambda i, j: (i, 0)), w_out, pl.BlockSpec((m, tn), lambda i, j: (0, j)))],
        NN, [(x, pl.BlockSpec((tm, tn), lambda i, j: (i, j)))],
        [(_sds((t, d), F32), pl.BlockSpec((tm, tn), lambda i, j: (i, j)))], epi, True, None, comm)
    return outs[0], couts


def _mm_swiglu(h2, wgt, wut, comm=None):
    t, d = h2.shape
    f = wgt.shape[0]
    tm, tn = _tile(t, 2 * _TM), _tile(f, 512)

    def epi(accs, e, o, rows, cols):
        g, u = accs
        s, sg = _silu_parts(g)
        o[0][rows, cols] = (sg * u).astype(BF16)
        o[1][rows, cols] = (u * (s + sg * (1.0 - s))).astype(BF16)
        o[2][rows, cols] = sg.astype(BF16)

    wspec = pl.BlockSpec((tn, d), lambda i, j: (j, 0))
    hspec = pl.BlockSpec((tm, d), lambda i, j: (i, 0))
    ospec = pl.BlockSpec((tm, tn), lambda i, j: (i, j))
    osh = _sds((t, f), BF16)
    outs, couts = _matmul("mm_swiglu", (t // tm, f // tn), 1, None, [(h2, hspec, wgt, wspec), (h2, hspec, wut, wspec)],
                          NT, [], [(osh, ospec)] * 3, epi, False, None, comm, split=(tn // 256, 0, 2))
    return outs, couts


def _mm_down(act, wd, x1, comm=None):
    t, f = act.shape
    d = wd.shape[1]
    tm, tn = _tile(t, _TM), _tile(d, _TN)
    nk = 2
    tk = f // nk

    def epi(accs, e, o):
        o[0][...] = e[0][...] + accs[0]

    outs, couts = _matmul(
        "mm_down", (t // tm, d // tn, nk), nk, 2,
        [(act, pl.BlockSpec((tm, tk), lambda i, j, k: (i, k)), wd, pl.BlockSpec((tk, tn), lambda i, j, k: (k, j)))],
        NN, [(x1, pl.BlockSpec((tm, tn), lambda i, j, k: (i, j)))],
        [(_sds((t, d), F32), pl.BlockSpec((tm, tn), lambda i, j, k: (i, j)))], epi, True, (tm, tn), comm)
    return outs[0], couts


def _mm_dact(dxb, wd, dact_dgate, dact_dup, comm=None):
    t, d = dxb.shape
    f = wd.shape[0]
    tm, tn = _tile(t, 2 * _TM), _tile(f, 512)

    def epi(accs, e, o, rows, cols):
        da = accs[0]
        o[0][rows, cols] = (da * e[0][rows, cols].astype(F32)).astype(BF16)
        o[1][rows, cols] = (da * e[1][rows, cols].astype(F32)).astype(BF16)

    bspec = pl.BlockSpec((tm, tn), lambda i, j: (i, j))
    osh = _sds((t, f), BF16)
    outs, couts = _matmul(
        "mm_dact", (t // tm, f // tn), 1, None,
        [(dxb, pl.BlockSpec((tm, d), lambda i, j: (i, 0)), wd, pl.BlockSpec((tn, d), lambda i, j: (j, 0)))],
        NT, [(dact_dgate, bspec), (dact_dup, bspec)], [(osh, bspec)] * 2, epi, True, None, comm, split=(tn // 256, 0, 2))
    return outs, couts


def _mm_dh2(dgate, dup, wgt, wut, comm=None):
    t, f = dgate.shape
    d = wgt.shape[1]
    tm, tn = _tile(t, _TM), _tile(d, _TN)
    nk = 4
    tk = f // nk

    def epi(accs, e, o):
        o[0][...] = accs[0].astype(BF16)

    aspec = pl.BlockSpec((tm, tk), lambda i, j, k: (i, k))
    wspec = pl.BlockSpec((tk, tn), lambda i, j, k: (k, j))
    outs, couts = _matmul("mm_dh2", (t // tm, d // tn, nk), nk, 2, [(dgate, aspec, wgt, wspec), (dup, aspec, wut, wspec)],
                          NN, [], [(_sds((t, d), BF16), pl.BlockSpec((tm, tn), lambda i, j, k: (i, j)))], epi, True,
                          (tm, tn), comm)
    return outs[0], couts


def _mm_dw(name, a_list, b, tmo, tno, comm=None, m_rows=None):
    t, m = a_list[0].shape
    start, m = (0, m) if m_rows is None else m_rows
    n = b.shape[1]
    tt = _tile(t, _TT)
    nk = t // tt
    tmo, tno = _tile(m, tmo), _tile(n, tno)
    first = start // tmo

    def epi(accs, e, o):
        for acc, out in zip(accs, o):
            out[...] = acc.astype(BF16)

    aspec = pl.BlockSpec((tt, tmo), lambda i, j, k: (k, first + i))
    bspec = pl.BlockSpec((tt, tno), lambda i, j, k: (k, j))
    ospec = pl.BlockSpec((tmo, tno), lambda i, j, k: (i, j))
    if nk == 1:
        return _matmul(name, (m // tmo, n // tno, 1), 1, None, [(a, aspec, b, bspec) for a in a_list], TN, [],
                       [(_sds((m, n), BF16), ospec)] * len(a_list), epi, False, None, comm)
    return _matmul(name, (m // tmo, n // tno, nk), nk, 2, [(a, aspec, b, bspec) for a in a_list], TN, [],
                   [(_sds((m, n), BF16), ospec)] * len(a_list), epi, False, (tmo, tno), comm)


def _mm_dy(dxb, w_out, comm=None):
    t, d = dxb.shape
    m = w_out.shape[0]
    tm, tn = _tile(t, _TM), _tile(m, _TN)

    def epi(accs, e, o):
        o[0][...] = accs[0].astype(BF16)

    outs, couts = _matmul(
        "mm_dy", (t // tm, m // tn), 1, None,
        [(dxb, pl.BlockSpec((tm, d), lambda i, j: (i, 0)), w_out, pl.BlockSpec((tn, d), lambda i, j: (j, 0)))], NT, [],
        [(_sds((t, m), BF16), pl.BlockSpec((tm, tn), lambda i, j: (i, j)))], epi, True, None, comm)
    return outs[0], couts


def _mm_dh(dz, w_in, comm=None):
    t, n = dz.shape
    d = w_in.shape[0]
    tm, tn = _tile(t, _TM), _tile(d, _TN)
    nk = 2
    tk = n // nk

    def epi(accs, e, o):
        o[0][...] = accs[0].astype(BF16)

    outs, couts = _matmul(
        "mm_dh", (t // tm, d // tn, nk), nk, 2,
        [(dz, pl.BlockSpec((tm, tk), lambda i, j, k: (i, k)), w_in, pl.BlockSpec((tn, tk), lambda i, j, k: (j, k)))], NT,
        [], [(_sds((t, d), BF16), pl.BlockSpec((tm, tn), lambda i, j, k: (i, j)))], epi, True, (tm, tn), comm)
    return outs[0], couts


def _rmsnorm_fwd(x, g, comm=None):
    t, d = x.shape
    tm = min(_TM_NORM, t)

    def body(x_ref, g_ref, o_ref):
        xv = x_ref[...]
        rs = lax.rsqrt(jnp.mean(xv * xv, axis=-1, keepdims=True) + RMS_EPS)
        o_ref[...] = (xv * rs * g_ref[...]).astype(BF16)

    outs, couts = _call_hosting(
        body, "rmsnorm_fwd", (t // tm,), [_sds((t, d), BF16)],
        [pl.BlockSpec((tm, d), lambda i: (i, 0)), pl.BlockSpec((1, d), lambda i: (0, 0))],
        [pl.BlockSpec((tm, d), lambda i: (i, 0))], [x, g], [], comm)
    return outs[0], couts


def _rmsnorm_bwd_math(xv, g, dh):
    rs = lax.rsqrt(jnp.mean(xv * xv, axis=-1, keepdims=True) + RMS_EPS)
    xh = xv * rs
    gd = dh * g
    dx = rs * (gd - xh * jnp.mean(gd * xh, axis=-1, keepdims=True))
    return dx, jnp.sum(dh * xh, axis=0, keepdims=True)


def _rmsnorm_bwd(x, g, dh, dres, with_bf16=True):
    t, d = x.shape
    tm = min(_TM_NORM, t)

    def body(x_ref, g_ref, dh_ref, dres_ref, dx_ref, *rest):
        dg_ref = rest[-1]
        dx, dg = _rmsnorm_bwd_math(x_ref[...], g_ref[...], dh_ref[...].astype(F32))
        dx = dx + dres_ref[...]
        dx_ref[...] = dx
        if with_bf16:
            rest[0][...] = dx.astype(BF16)

        @pl.when(pl.program_id(0) == 0)
        def _():
            dg_ref[...] = dg

        @pl.when(pl.program_id(0) > 0)
        def _():
            dg_ref[...] += dg

    row = pl.BlockSpec((tm, d), lambda i: (i, 0))
    vec = pl.BlockSpec((1, d), lambda i: (0, 0))
    halves = [(_sds((t, d), BF16), row)] if with_bf16 else []
    outs = [(_sds((t, d), F32), row), *halves, (_sds((1, d), F32), vec)]
    res = pl.pallas_call(
        body, name="rmsnorm_bwd", grid=(t // tm,), out_shape=tuple(o for o, _ in outs),
        in_specs=[row, vec, row, row], out_specs=tuple(s for _, s in outs), compiler_params=_cparams(1))(x, g, dh, dres)
    return (res[0], res[1], res[2]) if with_bf16 else (res[0], None, res[1])


def _loss_head(x, g, target):
    t, d = x.shape
    tm = min(_TM_NORM, t)

    def body(x_ref, g_ref, t_ref, dx_ref, dxb_ref, dg_ref, loss_ref):
        xv, gv = x_ref[...], g_ref[...]
        rs = lax.rsqrt(jnp.mean(xv * xv, axis=-1, keepdims=True) + RMS_EPS)
        diff = xv * rs * gv - t_ref[...]
        part = 0.5 * jnp.sum(jnp.mean(diff * diff, axis=-1, keepdims=True), axis=0, keepdims=True)
        part = jnp.broadcast_to(part, (1, 128))
        dx, dg = _rmsnorm_bwd_math(xv, gv, diff * (1.0 / d))
        dx_ref[...] = dx
        dxb_ref[...] = dx.astype(BF16)

        @pl.when(pl.program_id(0) == 0)
        def _():
            dg_ref[...] = dg
            loss_ref[...] = part

        @pl.when(pl.program_id(0) > 0)
        def _():
            dg_ref[...] += dg
            loss_ref[...] += part

    row = pl.BlockSpec((tm, d), lambda i: (i, 0))
    vec = pl.BlockSpec((1, d), lambda i: (0, 0))
    return pl.pallas_call(
        body, name="loss_head", grid=(t // tm,),
        out_shape=(_sds((t, d), F32), _sds((t, d), BF16), _sds((1, d), F32), _sds((1, 128), F32)),
        in_specs=[row, vec, row], out_specs=(row, row, vec, pl.BlockSpec((1, 128), lambda i: (0, 0))),
        compiler_params=_cparams(1))(x, g, target)


def _gelu(x):
    th = jnp.tanh(GELU_C * (x + GELU_A * x * x * x))
    return 0.5 * x * (1.0 + th), th


def _gelu_grad(x, th):
    return 0.5 * (1.0 + th) + 0.5 * x * (1.0 - th * th) * GELU_C * (1.0 + 3.0 * GELU_A * x * x)


def _masked_ws(ws_ref, h):
    i = lax.broadcasted_iota(jnp.int32, (BLK, BLK), 0) // CHUNK
    j = lax.broadcasted_iota(jnp.int32, (BLK, BLK), 1) // CHUNK
    return jnp.where(j <= i, ws_ref[h], 0.0)


def _shift_down(q, n, first_rows):
    rolled = pltpu.roll(q, n, 0)
    row = lax.broadcasted_iota(jnp.int32, q.shape, 0)
    for r, val in enumerate(first_rows):
        rolled = jnp.where(row == r, val, rolled)
    return rolled


def _shift_up(q, n, last_rows):
    tm = q.shape[0]
    rolled = pltpu.roll(q, tm - n, 0)
    row = lax.broadcasted_iota(jnp.int32, q.shape, 0)
    for r, val in enumerate(last_rows):
        rolled = jnp.where(row == tm - n + r, val, rolled)
    return rolled


def _mixer_specs(t, a, tm):
    hb = tm // HALO
    last = t // HALO - 1
    tile = pl.BlockSpec((tm, 5 * a), lambda i: (i, 0))
    prev = [pl.BlockSpec((HALO, a), functools.partial(lambda i, col: (jnp.maximum(i * hb - 1, 0), col), col=col))
            for col in (3, 4)]
    nxt = [pl.BlockSpec((HALO, a), functools.partial(lambda i, col: (jnp.minimum((i + 1) * hb, last), col), col=col))
           for col in (2, 3, 4)]
    return tile, prev, nxt


def _group_a_fwd(zu, zv, lng, lnb, ws_ref, bb_ref, mixed_ref, vln_ref):
    u, thu = _gelu(zu)
    v, thv = _gelu(zv)
    mu = jnp.mean(v, axis=-1, keepdims=True)
    vc = v - mu
    rs = lax.rsqrt(jnp.mean(vc * vc, axis=-1, keepdims=True) + LN_EPS)
    vhat = vc * rs
    vln_ref[...] = vhat * lng + lnb
    tm, a = zu.shape
    hd = a // HEADS
    for h in range(HEADS):
        w = _masked_ws(ws_ref, h).astype(BF16)
        for b in range(tm // BLK):
            rows, cols = pl.ds(b * BLK, BLK), pl.ds(h * hd, hd)
            mixed_ref[rows, cols] = jnp.dot(w, vln_ref[rows, cols].astype(BF16), preferred_element_type=F32) + bb_ref[h]
    return u, thu, thv, rs, vhat


def _mixer_fwd(z, ln_g, ln_b, w_spatial, bb, conv_w, gg, comm=None):
    t = z.shape[0]
    a = z.shape[1] // 5
    tm = min(_TM_MIX, t)
    tile, prev, _ = _mixer_specs(t, a, tm)

    def body(z_ref, pc_ref, ph_ref, lng_ref, lnb_ref, ws_ref, bb_ref, cw_ref, gg_ref, y_ref, mixed_ref, vln_ref):
        i = pl.program_id(0)
        zu = z_ref[:, 0:a].astype(F32)
        zv = z_ref[:, a:2 * a].astype(F32)
        u, _, _, _, _ = _group_a_fwd(zu, zv, lng_ref[...], lnb_ref[...], ws_ref, bb_ref, mixed_ref, vln_ref)
        ya = u * mixed_ref[...]
        ra = lax.rsqrt(jnp.mean(ya * ya, axis=-1, keepdims=True) + RMS_EPS)
        y_ref[:, 0:a] = (ya * ra * gg_ref[:, 0:a]).astype(BF16)

        zb = z_ref[:, 2 * a:3 * a].astype(F32)
        q = z_ref[:, 3 * a:4 * a].astype(F32) * z_ref[:, 4 * a:5 * a].astype(F32)
        qp = jnp.where(i > 0, pc_ref[...].astype(F32) * ph_ref[...].astype(F32), 0.0)
        qm1 = _shift_down(q, 1, [qp[HALO - 1:HALO]])
        qm2 = _shift_down(q, 2, [qp[HALO - 2:HALO - 1], qp[HALO - 1:HALO]])
        cv = cw_ref[0:1, :] * qm2 + cw_ref[1:2, :] * qm1 + cw_ref[2:3, :] * q
        yb = zb * cv
        rb = lax.rsqrt(jnp.mean(yb * yb, axis=-1, keepdims=True) + RMS_EPS)
        y_ref[:, a:2 * a] = (yb * rb * gg_ref[:, a:2 * a]).astype(BF16)

    full = lambda shape: pl.BlockSpec(shape, lambda i: (0,) * len(shape))
    outs, couts = _call_hosting(
        body, "mixer_fwd", (t // tm,), [_sds((t, 2 * a), BF16)],
        [tile, *prev, full((1, a)), full((1, a)), full(w_spatial.shape), full(bb.shape), full(conv_w.shape),
         full((1, 2 * a))],
        [pl.BlockSpec((tm, 2 * a), lambda i: (i, 0))], [z, z, z, ln_g, ln_b, w_spatial, bb, conv_w, gg],
        [pltpu.VMEM((tm, a), F32), pltpu.VMEM((tm, a), F32)], comm)
    return outs[0], couts


def _mixer_bwd(z, dy, ln_g, ln_b, w_spatial, bb, conv_w, gg, comm=None):
    t = z.shape[0]
    a = z.shape[1] // 5
    hd = a // HEADS
    tm = min(_TM_MIX, t)
    n_tiles = t // tm
    tile, prev, nxt = _mixer_specs(t, a, tm)
    hb = tm // HALO
    dy_tile = pl.BlockSpec((tm, 2 * a), lambda i: (i, 0))
    dy_next = pl.BlockSpec((HALO, a), lambda i: (jnp.minimum((i + 1) * hb, t // HALO - 1), 1))

    def body(z_ref, pc_ref, ph_ref, nb_ref, nc_ref, nh_ref, dy_ref, ndy_ref, lng_ref, lnb_ref, ws_ref, bb_ref, cw_ref,
             gg_ref, dz_ref, dlng_ref, dlnb_ref, dws_ref, dbb_ref, dcw_ref, dgg_ref, mixed_ref, vln_ref, dmix_ref,
             dvln_ref):
        i = pl.program_id(0)

        @pl.when(i == 0)
        def _():
            for ref in (dlng_ref, dlnb_ref, dws_ref, dbb_ref, dcw_ref, dgg_ref):
                ref[...] = jnp.zeros(ref.shape, F32)

        lng = lng_ref[...]
        zu = z_ref[:, 0:a].astype(F32)
        zv = z_ref[:, a:2 * a].astype(F32)
        u, thu, thv, rs, vhat = _group_a_fwd(zu, zv, lng, lnb_ref[...], ws_ref, bb_ref, mixed_ref, vln_ref)
        mixed = mixed_ref[...]
        ya = u * mixed
        ra = lax.rsqrt(jnp.mean(ya * ya, axis=-1, keepdims=True) + RMS_EPS)
        da = dy_ref[:, 0:a].astype(F32)
        yah = ya * ra
        dgg_ref[:, 0:a] += jnp.sum(da * yah, axis=0, keepdims=True)
        ga = da * gg_ref[:, 0:a]
        dya = ra * (ga - yah * jnp.mean(ga * yah, axis=-1, keepdims=True))
        dz_ref[:, 0:a] = (dya * mixed * _gelu_grad(zu, thu)).astype(BF16)
        dmix_ref[...] = dya * u
        for h in range(HEADS):
            w = _masked_ws(ws_ref, h).astype(BF16)
            dw = jnp.zeros((BLK, BLK), F32)
            db = jnp.zeros((BLK, hd), F32)
            for b in range(tm // BLK):
                rows, cols = pl.ds(b * BLK, BLK), pl.ds(h * hd, hd)
                dm = dmix_ref[rows, cols]
                dmb = dm.astype(BF16)
                db = db + dm
                dw = dw + lax.dot_general(dmb, vln_ref[rows, cols].astype(BF16), (NT, ((), ())),
                                          preferred_element_type=F32)
                dvln_ref[rows, cols] = lax.dot_general(w, dmb, (TN, ((), ())), preferred_element_type=F32)
            dws_ref[h] += dw
            dbb_ref[h] += db
        dvln = dvln_ref[...]
        dlng_ref[...] += jnp.sum(dvln * vhat, axis=0, keepdims=True)
        dlnb_ref[...] += jnp.sum(dvln, axis=0, keepdims=True)
        dvh = dvln * lng
        dv = rs * (dvh - jnp.mean(dvh, axis=-1, keepdims=True) - vhat * jnp.mean(dvh * vhat, axis=-1, keepdims=True))
        dz_ref[:, a:2 * a] = (dv * _gelu_grad(zv, thv)).astype(BF16)

        w0, w1, w2 = cw_ref[0:1, :], cw_ref[1:2, :], cw_ref[2:3, :]
        ggb = gg_ref[:, a:2 * a]
        zb = z_ref[:, 2 * a:3 * a].astype(F32)
        zc = z_ref[:, 3 * a:4 * a].astype(F32)
        zh = z_ref[:, 4 * a:5 * a].astype(F32)
        q = zc * zh
        qp = jnp.where(i > 0, pc_ref[...].astype(F32) * ph_ref[...].astype(F32), 0.0)
        qm1 = _shift_down(q, 1, [qp[HALO - 1:HALO]])
        qm2 = _shift_down(q, 2, [qp[HALO - 2:HALO - 1], qp[HALO - 1:HALO]])
        cv = w0 * qm2 + w1 * qm1 + w2 * q

        def conv_out_grad(zb_, cv_, dout_):
            yb = zb_ * cv_
            rb = lax.rsqrt(jnp.mean(yb * yb, axis=-1, keepdims=True) + RMS_EPS)
            ybh = yb * rb
            gb = dout_ * ggb
            dyb = rb * (gb - ybh * jnp.mean(gb * ybh, axis=-1, keepdims=True))
            return dyb * zb_, dyb * cv_, ybh

        db_out = dy_ref[:, a:2 * a].astype(F32)
        g, dzb, ybh = conv_out_grad(zb, cv, db_out)
        dgg_ref[:, a:2 * a] += jnp.sum(db_out * ybh, axis=0, keepdims=True)
        dz_ref[:, 2 * a:3 * a] = dzb.astype(BF16)
        qn = nc_ref[...].astype(F32) * nh_ref[...].astype(F32)
        zbn = nb_ref[...].astype(F32)
        cvn = w0 * _shift_down(qn, 2, [q[tm - 2:tm - 1], q[tm - 1:tm]]) + w1 * _shift_down(qn, 1, [q[tm - 1:tm]]) + w2 * qn
        gn, _, _ = conv_out_grad(zbn, cvn, ndy_ref[...].astype(F32))
        gn = jnp.where(i < n_tiles - 1, gn, 0.0)
        dq = w2 * g + w1 * _shift_up(g, 1, [gn[0:1]]) + w0 * _shift_up(g, 2, [gn[0:1], gn[1:2]])
        dz_ref[:, 3 * a:4 * a] = (dq * zh).astype(BF16)
        dz_ref[:, 4 * a:5 * a] = (dq * zc).astype(BF16)
        dcw_ref[0:1, :] += jnp.sum(g * qm2, axis=0, keepdims=True)
        dcw_ref[1:2, :] += jnp.sum(g * qm1, axis=0, keepdims=True)
        dcw_ref[2:3, :] += jnp.sum(g * q, axis=0, keepdims=True)

        @pl.when(i == n_tiles - 1)
        def _():
            for h in range(HEADS):
                dbb_ref[h] = jnp.broadcast_to(jnp.sum(dbb_ref[h], axis=1, keepdims=True), (BLK, hd))
                dws_ref[h] = _masked_ws(dws_ref, h)

    full = lambda shape: pl.BlockSpec(tuple(shape), lambda i: (0,) * len(shape))
    out_shapes = (_sds((t, 5 * a), BF16), _sds((1, a), F32), _sds((1, a), F32), _sds(w_spatial.shape, F32),
                  _sds(bb.shape, F32), _sds((8, a), F32), _sds((1, 2 * a), F32))
    return _call_hosting(
        body, "mixer_bwd", (n_tiles,), out_shapes,
        [tile, *prev, *nxt, dy_tile, dy_next, full((1, a)), full((1, a)), full(w_spatial.shape), full(bb.shape),
         full(conv_w.shape), full((1, 2 * a))],
        [tile, *[full(s.shape) for s in out_shapes[1:]]], [z, z, z, z, z, z, dy, dy, ln_g, ln_b, w_spatial, bb, conv_w, gg],
        [pltpu.VMEM((tm, a), F32)] * 4, comm)


def _all_reduce_small(pack, comm=None):
    r = pack.shape[0]
    hosted = _Hosted(comm, 1, 1)
    n_ci, n_co = len(hosted.operands), len(hosted.out_shapes)

    def body(*refs):
        in_ref, c_ins, out_ref, c_outs = refs[0], refs[1:1 + n_ci], refs[1 + n_ci], refs[2 + n_ci:2 + n_ci + n_co]
        acc_ref, recv_ref, send_sems, recv_sems = refs[2 + n_ci + n_co:6 + n_ci + n_co]
        sems = refs[6 + n_ci + n_co:]
        hosted.run("start", c_ins, c_outs, sems)
        x, y, c = _place()
        partners = [(x, y, 1 - c), (1 - x, y, c), (x, 1 - y, c)]
        acc_ref[0] = in_ref[...]
        for s, partner in enumerate(partners):
            cp = pltpu.make_async_remote_copy(
                src_ref=acc_ref.at[s], dst_ref=recv_ref.at[s], send_sem=send_sems.at[s], recv_sem=recv_sems.at[s],
                device_id=partner, device_id_type=MESH)
            cp.start()
            cp.wait()
            if s < 2:
                acc_ref[s + 1] = acc_ref[s] + recv_ref[s]
            else:
                out_ref[...] = acc_ref[s] + recv_ref[s]
        for stage in ("mid1", "mid2", "finish"):
            hosted.run(stage, c_ins, c_outs, sems)

    vmem = pl.BlockSpec(memory_space=pltpu.VMEM)
    res = pl.pallas_call(
        body, name="all_reduce_small", out_shape=tuple([_sds(pack.shape, F32)] + hosted.out_shapes),
        in_specs=[vmem] + hosted.in_specs, out_specs=tuple([vmem] + hosted.out_specs),
        input_output_aliases=hosted.aliases,
        scratch_shapes=[pltpu.VMEM((3, r, 128), F32), pltpu.VMEM((3, r, 128), F32), pltpu.SemaphoreType.DMA((3,)),
                        pltpu.SemaphoreType.DMA((3,))] + hosted.scratch,
        compiler_params=pltpu.CompilerParams(vmem_limit_bytes=VMEM_LIMIT_V7X),
    )(pack, *hosted.operands)
    return res[0], list(res[1:])


def _adamw_math(w, g, m, v):
    m = ADAM_B1 * m + (1.0 - ADAM_B1) * g
    v = ADAM_B2 * v + (1.0 - ADAM_B2) * (g * g)
    m_hat = m / (1.0 - ADAM_B1 ** ADAM_STEP)
    v_hat = v / (1.0 - ADAM_B2 ** ADAM_STEP)
    delta = -ADAM_LR * (m_hat / (jnp.sqrt(v_hat) + ADAM_EPS) + ADAM_WD * w)
    return delta, m, v


def _adamw_big(name, land, w, m, v, comm=None):
    nl, n_slots, r, c = land.shape
    tr = max(8, min(r, (256 * 640) // c // 8 * 8))
    while r % tr:
        tr -= 8
    grid = (nl, r // tr)
    hosted = _Hosted(comm, 4, 4)
    n_ci, n_co = len(hosted.operands), len(hosted.out_shapes)

    def body(*refs):
        land_ref, w_ref, m_ref, v_ref = refs[:4]
        c_ins = refs[4:4 + n_ci]
        g_out, d_out, m_out, v_out = refs[4 + n_ci:8 + n_ci]
        c_outs = refs[8 + n_ci:8 + n_ci + n_co]
        sems = refs[8 + n_ci + n_co:]

        def compute():
            g = land_ref[0].astype(F32)
            for s in range(1, n_slots):
                g = g + land_ref[s].astype(F32)
            delta, mn, vn = _adamw_math(w_ref[...], g, m_ref[...], v_ref[...])
            g_out[...] = g
            d_out[...] = delta
            m_out[...] = mn
            v_out[...] = vn

        hosted.wrap(grid, compute, c_ins, c_outs, sems)

    blk = pl.BlockSpec((None, tr, c), lambda l, i: (l, i, 0))
    res = pl.pallas_call(
        body, name=name, grid=grid, out_shape=tuple([_sds((nl, r, c), F32)] * 4 + hosted.out_shapes),
        in_specs=[pl.BlockSpec((None, n_slots, tr, c), lambda l, i: (l, 0, i, 0)), blk, blk, blk] + hosted.in_specs,
        out_specs=tuple([blk] * 4 + hosted.out_specs), input_output_aliases=hosted.aliases,
        scratch_shapes=hosted.scratch, compiler_params=_cparams(2))(land, w, m, v, *hosted.operands)
    return list(res[:4]), list(res[4:])


def _adamw_small(gs, ws, ms, vs):
    n = len(gs)

    def body(*refs):
        g_refs, w_refs, m_refs, v_refs = refs[:n], refs[n:2 * n], refs[2 * n:3 * n], refs[3 * n:4 * n]
        d_outs, m_outs, v_outs = refs[4 * n:5 * n], refs[5 * n:6 * n], refs[6 * n:7 * n]
        for i in range(n):
            delta, mn, vn = _adamw_math(w_refs[i][...], g_refs[i][...], m_refs[i][...], v_refs[i][...])
            d_outs[i][...] = delta
            m_outs[i][...] = mn
            v_outs[i][...] = vn

    shapes = [_sds(g.shape, F32) for g in gs]
    res = pl.pallas_call(body, name="adamw_small", out_shape=tuple(shapes * 3),
                         compiler_params=pltpu.CompilerParams(vmem_limit_bytes=VMEM_LIMIT_V7X))(*gs, *ws, *ms, *vs)
    return list(res[:n]), list(res[n:2 * n]), list(res[2 * n:])


def _rows(a):
    return a.reshape(-1, 128)


BIG = ["w_in", "w_out", "w_gate", "w_up", "w_down"]
AG_HOSTS = {
    ("norm1", 0): [("w_in", 0), ("conv_w", 0)],
    ("mm_in", 0): [("w_out", 0), ("w_gate", 0, 0, 2)], ("mixer", 0): [("w_gate", 0, 1, 2)],
    ("mm_out", 0): [("w_up", 0, 0, 2)], ("norm2", 0): [("w_up", 0, 1, 2)],
    ("mm_swiglu", 0): [("w_down", 0), ("w_in", 1)], ("mm_down", 0): [("w_out", 1), ("w_gate", 1, 0, 2)],
    ("mm_in", 1): [("w_gate", 1, 1, 2)], ("mixer", 1): [("w_up", 1, 0, 2)], ("mm_out", 1): [("w_up", 1, 1, 2)],
    ("mm_swiglu", 1): [("w_down", 1)],
}


def kernel(x, norm1_g, w_in, gmlp_ln_g, gmlp_ln_b, w_spatial, b_spatial, conv_w, group_norm_g, w_out, norm2_g, w_gate, w_up, w_down, final_norm_g, loss_target, m_norm1_g, m_w_in, m_gmlp_ln_g, m_gmlp_ln_b, m_w_spatial, m_b_spatial, m_conv_w, m_group_norm_g, m_w_out, m_norm2_g, m_w_gate, m_w_up, m_w_down, m_final_norm_g, v_norm1_g, v_w_in, v_gmlp_ln_g, v_gmlp_ln_b, v_w_spatial, v_b_spatial, v_conv_w, v_group_norm_g, v_w_out, v_norm2_g, v_w_gate, v_w_up, v_w_down, v_final_norm_g):
    nl = N_LAYERS
    t, d = x.shape[1], x.shape[2]
    a = d // 2
    hd = a // HEADS
    xin = x.reshape(t, d)
    target = loss_target.reshape(t, d)
    me = _index(_place())

    tr = lambda w: jnp.transpose(w, (0, 2, 1))
    big = {"w_in": w_in, "w_out": w_out, "w_gate": tr(w_gate), "w_up": tr(w_up), "w_down": w_down}
    big_m = {"w_in": m_w_in, "w_out": m_w_out, "w_gate": tr(m_w_gate), "w_up": tr(m_w_up), "w_down": m_w_down}
    big_v = {"w_in": v_w_in, "w_out": v_w_out, "w_gate": tr(v_w_gate), "w_up": tr(v_w_up), "w_down": v_w_down}
    block = {k: big[k].shape[1:] for k in BIG}
    view = {k: _cols_view(block[k][1]) if k == "w_in" else _rows_view(block[k][0]) for k in BIG}
    full_shape = {k: (block[k][0], N_DEV * block[k][1]) if k == "w_in" else (N_DEV * block[k][0], block[k][1])
                  for k in BIG}

    weights = {}
    shards = {(k, l): big[k][l].astype(BF16) for k in BIG for l in range(nl)}

    def ag_spec(k, l, part=0, n_parts=1):
        if k == "conv_w":
            return (conv_w, _sds((N_DEV, *conv_w.shape), F32), _SLOT_WHOLE, (0,), None)
        halves = (_cols_halves(*block[k], part, n_parts) if k == "w_in" else _rows_halves(block[k][0], part, n_parts))
        return (shards[(k, l)], _sds(full_shape[k], BF16), halves, (0, 1), weights.get((k, l)))

    bb = jnp.broadcast_to(b_spatial[..., None], (nl, HEADS, BLK, hd))

    def hosted(name, l):
        keys = AG_HOSTS.get((name, l), [])
        return keys, ([_ag_piece([ag_spec(*key) for key in keys])] if keys else None)

    def landed(keys, couts):
        for key, arr in zip(keys, couts):
            weights[key[:2]] = arr

    saved = []
    xl = xin
    for l in range(nl):
        keys, comm = hosted("norm1", l)
        h, couts = _rmsnorm_fwd(xl, norm1_g[l:l + 1], comm)
        landed(keys, couts)
        if l == 0:
            conv_full = jnp.transpose(weights[("conv_w", 0)], (1, 2, 0, 3)).reshape(nl, 3, a)
        keys, comm = hosted("mm_in", l)
        z, couts = _mm_in(h, weights[("w_in", l)], comm)
        landed(keys, couts)
        keys, comm = hosted("mixer", l)
        y, couts = _mixer_fwd(z, gmlp_ln_g[l:l + 1], gmlp_ln_b[l:l + 1], w_spatial[l], bb[l], conv_full[l],
                              group_norm_g[l:l + 1], comm)
        landed(keys, couts)
        keys, comm = hosted("mm_out", l)
        x1, couts = _mm_out(y, weights[("w_out", l)], xl, comm)
        landed(keys, couts)
        keys, comm = hosted("norm2", l)
        h2, couts = _rmsnorm_fwd(x1, norm2_g[l:l + 1], comm)
        landed(keys, couts)
        keys, comm = hosted("mm_swiglu", l)
        (act, dact_dgate, dact_dup), couts = _mm_swiglu(h2, weights[("w_gate", l)], weights[("w_up", l)], comm)
        landed(keys, couts)
        keys, comm = hosted("mm_down", l)
        x2, couts = _mm_down(act, weights[("w_down", l)], x1, comm)
        landed(keys, couts)
        saved.append(dict(x=xl, h=h, z=z, y=y, x1=x1, h2=h2, dact_dgate=dact_dgate, dact_dup=dact_dup, act=act))
        xl = x2

    dx, dxb, d_final_g, loss_part = _loss_head(xl, final_norm_g.reshape(1, d), target)
    small = [None] * nl
    core = lax.axis_index("c").astype(jnp.int32).reshape(1)
    in_rows = block["w_in"][0]
    part_of = {"w_in_a": ("w_in", 0), "w_in_b": ("w_in", 3 * in_rows // 4)}
    block["w_in_a"], block["w_in_b"] = (3 * in_rows // 4, block["w_in"][1]), (in_rows // 4, block["w_in"][1])
    for k in part_of:
        view[k] = view["w_in"]
    stage_shape = {k: _sds((N_CHIPS, *block[k]), BF16) for k in block}
    land_shape = {k: _sds((nl, N_CHIPS, *block[k]), BF16) for k in BIG}
    grads = [dict() for _ in range(nl)]
    stages = [dict() for _ in range(nl)]
    sums = [dict() for _ in range(nl)]
    lands = {k: None for k in BIG}

    def core_job(l, keys):
        def sink(outs):
            stages[l].update(zip(keys, outs))
        return _rs_core_piece([(grads[l][k], stage_shape[k], view[k]) for k in keys]), sink

    def chip_job(l, items):
        keys = [part_of.get(item[0], (item[0], 0))[0] for item in items]

        def rows(k, p0, p1, n_parts):
            per = block[k][0] // n_parts
            landing = part_of.get(k, (k, 0))[1]
            return (p0 * per, landing + p0 * per, (p1 - p0) * per)

        def sink(outs):
            lands.update(zip(keys, outs))
        return _rs_chip_piece([(sums[l][k], land_shape[key], rows(k, p0, p1, n_parts), lands[key])
                               for key, (k, p0, p1, n_parts) in zip(keys, items)], l), sink

    def add_up(l, keys):
        for k in keys:
            sums[l][k] = _chip_sums(f"chip_sums_{k}", grads[l][k], stages[l][k], k.startswith("w_in"), core)

    def host(*jobs):
        def deliver(couts):
            i = 0
            for piece, sink in jobs:
                n_out = len(piece.out_shapes)
                sink(couts[i:i + n_out])
                i += n_out
        return [piece for piece, _ in jobs], deliver

    whole = lambda k: (k, 0, 1, 1)
    rep = ["norm1_g", "gmlp_ln_g", "gmlp_ln_b", "w_spatial", "b_spatial", "group_norm_g", "norm2_g"]
    rep_w = dict(norm1_g=norm1_g, gmlp_ln_g=gmlp_ln_g, gmlp_ln_b=gmlp_ln_b, w_spatial=w_spatial, b_spatial=b_spatial,
                 group_norm_g=group_norm_g, norm2_g=norm2_g)
    rep_m = dict(norm1_g=m_norm1_g, gmlp_ln_g=m_gmlp_ln_g, gmlp_ln_b=m_gmlp_ln_b, w_spatial=m_w_spatial,
                 b_spatial=m_b_spatial, group_norm_g=m_group_norm_g, norm2_g=m_norm2_g)
    rep_v = dict(norm1_g=v_norm1_g, gmlp_ln_g=v_gmlp_ln_g, gmlp_ln_b=v_gmlp_ln_b, w_spatial=v_w_spatial,
                 b_spatial=v_b_spatial, group_norm_g=v_group_norm_g, norm2_g=v_norm2_g)

    def small_grad_parts():
        parts = [_rows(jnp.stack([small[l][k].reshape(rep_w[k].shape[1:]) for l in range(nl)])) for k in rep]
        parts.append(_rows(d_final_g))
        parts.append(_rows(jnp.stack([small[l]["conv_w"] for l in range(nl)])))
        parts.append(jnp.broadcast_to(loss_part, (8, 128)))
        rows = sum(p.shape[0] for p in parts)
        parts.append(jnp.zeros((-rows % 16, 128), F32))
        return parts

    for l in reversed(range(nl)):
        s = saved[l]
        wi, wo, wgt, wut, wd = [weights[(k, l)] for k in BIG]
        later = l + 1 < nl
        comm, deliver = host(chip_job(l + 1, [("w_in", 0, 1, 2)])) if later else host()
        (grads[l]["w_down"],), couts = _mm_dw("mm_dw_down", [s["act"]], dxb, 2816, 1024, comm)
        deliver(couts)
        comm, deliver = (host(core_job(l, ["w_down"]), chip_job(l + 1, [("w_in", 1, 2, 2)])) if later
                         else host(core_job(l, ["w_down"])))
        (dgate, dup), couts = _mm_dact(dxb, wd, s["dact_dgate"], s["dact_dup"], comm)
        deliver(couts)
        add_up(l, ["w_down"])
        comm, deliver = host(chip_job(l, [("w_down", 0, 3, 4)]))
        (grads[l]["w_gate"],), couts = _mm_dw("mm_dw_gate", [dgate], s["h2"], 2816, 1024, comm)
        deliver(couts)
        comm, deliver = host(chip_job(l, [("w_down", 3, 4, 4)]), core_job(l, ["w_gate"]))
        (grads[l]["w_up"],), couts = _mm_dw("mm_dw_up", [dup], s["h2"], 2816, 1024, comm)
        deliver(couts)
        add_up(l, ["w_gate"])
        comm, deliver = host(chip_job(l, [whole("w_gate")]), core_job(l, ["w_up"]))
        dh2, couts = _mm_dh2(dgate, dup, wgt, wut, comm)
        deliver(couts)
        add_up(l, ["w_up"])
        dx1, dx1b, d_n2 = _rmsnorm_bwd(s["x1"], norm2_g[l:l + 1], dh2, dx)
        comm, deliver = host(chip_job(l, [("w_up", 0, 1, 4)]))
        dy, couts = _mm_dy(dx1b, wo, comm)
        deliver(couts)
        comm, deliver = host(chip_job(l, [("w_up", 1, 2, 4)]))
        (grads[l]["w_out"],), couts = _mm_dw("mm_dw_out", [s["y"]], dx1b, 1024, 1024, comm)
        deliver(couts)
        comm, deliver = host(chip_job(l, [("w_up", 2, 4, 4)]), core_job(l, ["w_out"]))
        (dz, d_lng, d_lnb, d_ws, d_bb, d_cw, d_gg), couts = _mixer_bwd(
            s["z"], dy, gmlp_ln_g[l:l + 1], gmlp_ln_b[l:l + 1], w_spatial[l], bb[l], conv_full[l], group_norm_g[l:l + 1],
            comm)
        deliver(couts)
        add_up(l, ["w_out"])
        small[l] = dict(norm1_g=jnp.zeros((1, d), F32), gmlp_ln_g=d_lng, gmlp_ln_b=d_lnb, w_spatial=d_ws,
                        b_spatial=d_bb[:, :, 0], group_norm_g=d_gg, norm2_g=d_n2, conv_w=d_cw[0:3])
        if l > 0:
            comm, deliver = host(chip_job(l, [whole("w_out")]))
            (grads[l]["w_in"],), couts = _mm_dw("mm_dw_in", [s["h"]], dz, 2048, 1024, comm)
            deliver(couts)
            comm, deliver = host(core_job(l, ["w_in"]))
            dh, couts = _mm_dh(dz, wi, comm)
            deliver(couts)
            add_up(l, ["w_in"])
        else:
            parts = small_grad_parts()
            reduced = []
            comm, deliver = host(chip_job(l, [whole("w_out")]),
                                 (_all_reduce_piece(jnp.concatenate(parts, axis=0)), reduced.extend))
            (grads[l]["w_in_a"],), couts = _mm_dw("mm_dw_in_a", [s["h"]], dz, block["w_in_a"][0], 1024, comm,
                                                  m_rows=(0, block["w_in_a"][0]))
            deliver(couts)
            comm, deliver = host(core_job(l, ["w_in_a"]))
            (grads[l]["w_in_b"],), couts = _mm_dw("mm_dw_in_b", [s["h"]], dz, block["w_in_b"][0], 2560, comm,
                                                  m_rows=(block["w_in_a"][0], block["w_in_b"][0]))
            deliver(couts)
            add_up(l, ["w_in_a"])
            comm, deliver = host(chip_job(l, [whole("w_in_a")]), core_job(l, ["w_in_b"]))
            dh, couts = _mm_dh(dz, wi, comm)
            deliver(couts)
            add_up(l, ["w_in_b"])
        dx, dxb, small[l]["norm1_g"] = _rmsnorm_bwd(s["x"], norm1_g[l:l + 1], dh, dx1, with_bf16=l > 0)
    grad_x = dx.reshape(x.shape)

    sizes = [p.shape[0] for p in parts]
    comm, deliver = host(chip_job(0, [whole("w_in_b")]))
    last, couts = _all_reduce_small(_rows(small[0]["norm1_g"]), comm)
    deliver(couts)
    total = lax.dynamic_update_slice(reduced[0], last, (0, 0))
    offs = [0]
    for n in sizes:
        offs.append(offs[-1] + n)
    pieces = [total[offs[i]:offs[i + 1]] for i in range(len(parts))]
    loss = pieces[len(rep) + 2][0, 0]
    conv_g_full = pieces[len(rep) + 1].reshape(nl, 3, N_DEV, a // N_DEV)
    conv_g = lax.dynamic_index_in_dim(conv_g_full, me, axis=2, keepdims=False)
    names = rep + ["final_norm_g", "conv_w"]
    flat = lambda w: w.reshape(-1, w.shape[-1])
    small_w = [flat(rep_w[k]) for k in rep] + [flat(final_norm_g), flat(conv_w)]
    small_m = [flat(rep_m[k]) for k in rep] + [flat(m_final_norm_g), flat(m_conv_w)]
    small_v = [flat(rep_v[k]) for k in rep] + [flat(v_final_norm_g), flat(v_conv_w)]
    small_g = [pieces[i].reshape(small_w[i].shape) for i in range(len(rep) + 1)] + [flat(conv_g)]
    small_d, small_m, small_v = _adamw_small(small_g, small_w, small_m, small_v)
    shape_of = dict(rep_w, final_norm_g=final_norm_g, conv_w=conv_w)
    named = lambda arrays: {k: arr.reshape(shape_of[k].shape) for k, arr in zip(names, arrays)}
    res = {"grad": named(small_g), "delta": named(small_d), "m": named(small_m), "v": named(small_v)}

    for k in BIG:
        outs, _ = _adamw_big(f"adamw_{k}", lands[k], big[k], big_m[k], big_v[k])
        if k in ("w_gate", "w_up"):
            outs = [tr(o) for o in outs]
        res["grad"][k], res["delta"][k], res["m"][k], res["v"][k] = outs

    order = ["norm1_g", "w_in", "gmlp_ln_g", "gmlp_ln_b", "w_spatial", "b_spatial", "conv_w", "group_norm_g", "w_out",
             "norm2_g", "w_gate", "w_up", "w_down", "final_norm_g"]
    return (loss, grad_x, *[res["grad"][k] for k in order], *[res["delta"][k] for k in order],
            *[res["m"][k] for k in order], *[res["v"][k] for k in order])
```

```python
import functools
import math
import operator

import jax
import jax.numpy as jnp
from jax import lax
from jax.experimental import pallas as pl
from jax.experimental.pallas import tpu as pltpu

F32 = jnp.float32
BF16 = jnp.bfloat16
MESH = pl.DeviceIdType.MESH

N_DEV = 8
N_LAYERS = 2
HEADS = 8
BLK = 128
CHUNK = 64
HALO = 16
RMS_EPS = 1e-6
LN_EPS = 1e-5
ADAM_LR, ADAM_B1, ADAM_B2, ADAM_EPS, ADAM_WD, ADAM_STEP = 0.001, 0.9, 0.999, 1e-8, 0.01, 10
GELU_C = math.sqrt(2.0 / math.pi)
GELU_A = 0.044715

VMEM_LIMIT_V7X = 56 * 1024 * 1024
_TM = 1024
_TN = 1024
_TT = 1024
_TM_MIX = 256
_TM_NORM = 512


def _cparams(n_axes):
    return pltpu.CompilerParams(dimension_semantics=("arbitrary",) * n_axes, vmem_limit_bytes=VMEM_LIMIT_V7X)


def _sds(shape, dtype):
    return jax.ShapeDtypeStruct(tuple(shape), dtype)


def _place():
    return lax.axis_index("x"), lax.axis_index("y"), lax.axis_index("c")


def _index(place):
    return 4 * place[0] + 2 * place[1] + place[2]


class _Piece:
    def __init__(self, operands, out_shapes, aliases, n_sems, start, finish, mid1=None, mid2=None, vmem=(),
                 hooks=(0.6, 0.87)):
        self.operands, self.out_shapes, self.aliases, self.n_sems = list(operands), list(out_shapes), dict(aliases), n_sems
        self.vmem = list(vmem)
        self.hooks = hooks
        nothing = lambda ctx: None
        self.start, self.mid1, self.mid2, self.finish = start, mid1 or nothing, mid2 or nothing, finish


class _Ctx:
    def __init__(self, ins, outs, sems, offs):
        self.ins, self.outs, self.sems = ins, outs, sems
        self.o_in, self.o_out, self.o_send, self.o_recv, self.o_loc, self.o_vmem = offs

    def vmem(self, i):
        return self.sems[3 + self.o_vmem + i]

    def inp(self, i):
        return self.ins[self.o_in + i]

    def out(self, i):
        return self.outs[self.o_out + i]

    def send(self, k):
        return self.sems[0].at[self.o_send + k]

    def recv(self, k):
        return self.sems[1].at[self.o_recv + k]

    def local(self, k):
        return self.sems[2].at[self.o_loc + k]


class _Hosted:
    def __init__(self, pieces, n_in_before, n_out_before):
        self.pieces = [p for p in (pieces or []) if p is not None]
        self.operands, self.out_shapes, self.aliases, self.offs = [], [], {}, []
        counts, vmem = [0, 0, 0], []
        for p in self.pieces:
            self.offs.append((len(self.operands), len(self.out_shapes), *counts, len(vmem)))
            for i, j in p.aliases.items():
                self.aliases[n_in_before + len(self.operands) + i] = n_out_before + len(self.out_shapes) + j
            self.operands += p.operands
            self.out_shapes += p.out_shapes
            counts = [c + n for c, n in zip(counts, p.n_sems)]
            vmem += p.vmem
        hbm = pl.BlockSpec(memory_space=pl.ANY)
        self.in_specs = [hbm] * len(self.operands)
        self.out_specs = [hbm] * len(self.out_shapes)
        self.scratch = ([pltpu.SemaphoreType.DMA((max(c, 1),)) for c in counts] + vmem) if self.pieces else []

    def run(self, stage, ins, outs, sems):
        for p, offs in zip(self.pieces, self.offs):
            getattr(p, stage)(_Ctx(ins, outs, sems, offs))

    def wrap(self, grid, compute, ins, outs, sems):
        if not self.pieces:
            compute()
            return
        n_steps = math.prod(grid)
        lin = 0
        for ax, g in enumerate(grid):
            lin = lin * g + pl.program_id(ax)
        pl.when(lin == 0)(lambda: self.run("start", ins, outs, sems))
        compute()
        for stage, which in (("mid1", 0), ("mid2", 1)):
            for p, offs in zip(self.pieces, self.offs):
                at = min(n_steps - 1, int(p.hooks[which] * n_steps))
                pl.when(lin == at)(functools.partial(getattr(p, stage), _Ctx(ins, outs, sems, offs)))
        pl.when(lin == n_steps - 1)(lambda: self.run("finish", ins, outs, sems))


def _cols_view(width):
    return lambda ref, p: ref.at[:, pl.ds(pl.multiple_of(p * width, 128), width)]


def _rows_view(height):
    return lambda ref, p: ref.at[pl.ds(pl.multiple_of(p * height, 16), height), :]


def _cols_halves(rows, width, part, n_parts):
    hr = rows // n_parts // 2
    at = lambda h: pl.ds(part * 2 * hr + h * hr, hr)
    return (lambda ref, p, h: ref.at[at(h), pl.ds(pl.multiple_of(p * width, 128), width)],
            lambda ref, h: ref.at[at(h), :], 2)


def _rows_halves(height, part, n_parts):
    hh = height // n_parts // 2
    return (lambda ref, p, h: ref.at[pl.ds(pl.multiple_of(p * height + part * 2 * hh + h * hh, 16), hh), :],
            lambda ref, h: ref.at[pl.ds(part * 2 * hh + h * hh, hh), :], 2)


_SLOT_WHOLE = (lambda ref, p, h: ref.at[p], lambda ref, h: ref, 1)


def _ag_piece(specs):
    units = [(a, h) for a, s in enumerate(specs) for h in s[3]]

    def plan(ctx):
        x, y, c = _place()
        me, sib, xn, yn, dg = (x, y, c), (x, y, 1 - c), (1 - x, y, c), (x, 1 - y, c), (1 - x, 1 - y, c)

        def copy(u, k, block, to, from_shard=False):
            a, h = units[u]
            dst_of, src_of, _ = specs[a][2]
            dst = dst_of(ctx.out(a), _index(block), h)
            return pltpu.make_async_remote_copy(
                src_ref=src_of(ctx.inp(a), h) if from_shard else dst, dst_ref=dst, send_sem=ctx.send(7 * u + k),
                recv_sem=ctx.recv(7 * u + k), device_id=to, device_id_type=MESH)

        def local(u):
            a, h = units[u]
            dst_of, src_of, _ = specs[a][2]
            return pltpu.make_async_copy(src_of(ctx.inp(a), h), dst_of(ctx.out(a), _index(me), h), ctx.local(u))

        def relay(u):
            return copy(u, 3, xn, yn) if units[u][1] % 2 == 0 else copy(u, 3, yn, xn)

        return me, sib, xn, yn, dg, c, copy, local, relay

    def start(ctx):
        me, sib, xn, yn, dg, c, copy, local, relay = plan(ctx)
        for u in range(len(units)):
            local(u).start()
            for k, to in enumerate((sib, xn, yn)):
                copy(u, k, me, to, from_shard=True).start()

    def mid1(ctx):
        me, sib, xn, yn, dg, c, copy, local, relay = plan(ctx)
        for u in range(len(units)):
            copy(u, 1, xn, me).wait_recv()
            copy(u, 2, yn, me).wait_recv()
            relay(u).start()
            copy(u, 4, xn, sib).start()
            copy(u, 5, yn, sib).start()

    def mid2(ctx):
        me, sib, xn, yn, dg, c, copy, local, relay = plan(ctx)
        for u in range(len(units)):
            copy(u, 3, dg, me).wait_recv()
            copy(u, 6, dg, sib).start()

    def finish(ctx):
        me, sib, xn, yn, dg, c, copy, local, relay = plan(ctx)
        other = lambda place: (place[0], place[1], 1 - c)
        for u in range(len(units)):
            for k, block in ((0, sib), (4, other(xn)), (5, other(yn)), (6, other(dg))):
                copy(u, k, block, me).wait_recv()
        for u in range(len(units)):
            for k, to in enumerate((sib, xn, yn)):
                copy(u, k, me, to, from_shard=True).wait_send()
            relay(u).wait_send()
            for k, block in ((4, xn), (5, yn), (6, dg)):
                copy(u, k, block, sib).wait_send()
            local(u).wait()

    n_u = len(units)
    operands, aliases = [s[0] for s in specs], {}
    for a, spec in enumerate(specs):
        if spec[4] is not None:
            aliases[len(operands)] = a
            operands.append(spec[4])
    return _Piece(operands, [s[1] for s in specs], aliases, (7 * n_u, 7 * n_u, n_u), start, finish, mid1, mid2)


N_CHIPS = 4


def _rs_core_piece(specs):
    n = len(specs)

    def copies(ctx):
        x, y, c = _place()
        out = []
        for a in range(n):
            for q in range(N_CHIPS):
                out.append(pltpu.make_async_remote_copy(
                    src_ref=specs[a][2](ctx.inp(a), 2 * q + (1 - c)), dst_ref=ctx.out(a).at[q],
                    send_sem=ctx.send(N_CHIPS * a + q), recv_sem=ctx.recv(N_CHIPS * a + q), device_id=(x, y, 1 - c),
                    device_id_type=MESH))
        return out

    def start(ctx):
        for cp in copies(ctx):
            cp.start()

    def finish(ctx):
        for cp in copies(ctx):
            cp.wait_recv()
            cp.wait_send()

    return _Piece([s[0] for s in specs], [s[1] for s in specs], {}, (N_CHIPS * n, N_CHIPS * n, 0), start, finish)


def _rs_chip_piece(specs, layer):
    n = len(specs)
    hops = [(1, 0), (0, 1), (1, 1)]

    def copies(ctx):
        x, y, c = _place()
        mine = 2 * x + y
        out = []
        for a in range(n):
            first, landing, size = specs[a][2]
            rows, to = pl.ds(first, size), pl.ds(landing, size)
            sums, land = ctx.inp(a), ctx.out(a)
            out.append((pltpu.make_async_copy(sums.at[mine, rows], land.at[layer, mine, to], ctx.local(a)), None))
            for j, (dx, dy) in enumerate(hops):
                px, py = x ^ dx, y ^ dy
                peer = 2 * px + py
                send = pltpu.make_async_remote_copy(
                    src_ref=sums.at[peer, rows], dst_ref=land.at[layer, mine, to], send_sem=ctx.send(3 * a + j),
                    recv_sem=ctx.recv(3 * a + j), device_id=(px, py, c), device_id_type=MESH)
                recv = pltpu.make_async_remote_copy(
                    src_ref=sums.at[peer, rows], dst_ref=land.at[layer, peer, to], send_sem=ctx.send(3 * a + j),
                    recv_sem=ctx.recv(3 * a + j), device_id=(px, py, c), device_id_type=MESH)
                out.append((send, recv))
        return out

    def start(ctx):
        for send, _ in copies(ctx):
            send.start()

    def finish(ctx):
        for send, recv in copies(ctx):
            if recv is None:
                send.wait()
            else:
                recv.wait_recv()
                send.wait_send()

    operands, aliases = [s[0] for s in specs], {}
    for a, spec in enumerate(specs):
        if spec[3] is not None:
            aliases[len(operands)] = a
            operands.append(spec[3])
    return _Piece(operands, [s[1] for s in specs], aliases, (3 * n, 3 * n, n), start, finish)


def _all_reduce_piece(pack):
    r = pack.shape[0]
    half = r // 2

    def plan(ctx):
        x, y, c = _place()
        acc, got = ctx.vmem(0), ctx.vmem(1)
        mine = pl.ds(pl.multiple_of(c * half, 8), half)
        sib = (x, y, 1 - c)
        copies = [
            pltpu.make_async_remote_copy(src_ref=acc.at[0], dst_ref=got.at[0], send_sem=ctx.send(0), recv_sem=ctx.recv(0),
                                         device_id=sib, device_id_type=MESH),
            pltpu.make_async_remote_copy(src_ref=acc.at[1, mine], dst_ref=got.at[1, mine], send_sem=ctx.send(1),
                                         recv_sem=ctx.recv(1), device_id=(1 - x, y, c), device_id_type=MESH),
            pltpu.make_async_remote_copy(src_ref=acc.at[2, mine], dst_ref=got.at[2, mine], send_sem=ctx.send(2),
                                         recv_sem=ctx.recv(2), device_id=(x, 1 - y, c), device_id_type=MESH),
            pltpu.make_async_remote_copy(src_ref=acc.at[3, mine], dst_ref=acc.at[3, mine], send_sem=ctx.send(3),
                                         recv_sem=ctx.recv(3), device_id=sib, device_id_type=MESH),
        ]
        other = pl.ds(pl.multiple_of((1 - c) * half, 8), half)
        arrival = pltpu.make_async_remote_copy(src_ref=acc.at[3, other], dst_ref=acc.at[3, other], send_sem=ctx.send(3),
                                               recv_sem=ctx.recv(3), device_id=sib, device_id_type=MESH)
        return acc, got, mine, copies, arrival

    def start(ctx):
        acc, got, mine, copies, arrival = plan(ctx)
        load = pltpu.make_async_copy(ctx.inp(0), acc.at[0], ctx.local(0))
        load.start()
        load.wait()
        copies[0].start()

    def mid1(ctx):
        acc, got, mine, copies, arrival = plan(ctx)
        copies[0].wait()
        acc[1] = acc[0] + got[0]
        copies[1].start()

    def mid2(ctx):
        acc, got, mine, copies, arrival = plan(ctx)
        copies[1].wait()
        acc[2, mine] = acc[1, mine] + got[1, mine]
        copies[2].start()

    def finish(ctx):
        acc, got, mine, copies, arrival = plan(ctx)
        copies[2].wait()
        acc[3, mine] = acc[2, mine] + got[2, mine]
        copies[3].start()
        copies[3].wait_send()
        arrival.wait_recv()
        store = pltpu.make_async_copy(acc.at[3], ctx.out(0), ctx.local(0))
        store.start()
        store.wait()

    return _Piece([pack], [_sds(pack.shape, F32)], {}, (4, 4, 1), start, finish, mid1, mid2,
                  vmem=[pltpu.VMEM((4, r, 128), F32), pltpu.VMEM((3, r, 128), F32)], hooks=(0.25, 0.6))


def _chip_sums(name, grad, stage, by_cols, core):
    _, r, c = stage.shape
    tr = r
    while tr * c > 1024 * 1024 or r % tr or tr % 16:
        tr -= 16
    n_t = r // tr

    def body(core_ref, g_ref, s_ref, o_ref):
        o_ref[...] = (g_ref[...].astype(F32) + s_ref[...].astype(F32)).astype(BF16)

    if by_cols:
        gspec = pl.BlockSpec((tr, c), lambda q, i, core_ref: (i, 2 * q + core_ref[0]))
    else:
        gspec = pl.BlockSpec((tr, c), lambda q, i, core_ref: ((2 * q + core_ref[0]) * n_t + i, 0))
    sspec = pl.BlockSpec((None, tr, c), lambda q, i, core_ref: (q, i, 0))
    return pl.pallas_call(
        body, name=name, out_shape=_sds(stage.shape, BF16),
        grid_spec=pltpu.PrefetchScalarGridSpec(num_scalar_prefetch=1, grid=(N_CHIPS, n_t), in_specs=[gspec, sspec],
                                               out_specs=sspec),
        compiler_params=_cparams(2))(core, grad, stage)


def _call_hosting(body, name, grid, out_shapes, in_specs, out_specs, operands, scratch, comm):
    n_in, n_out, n_scr = len(operands), len(out_shapes), len(scratch)
    hosted = _Hosted(comm, n_in, n_out)
    n_ci, n_co = len(hosted.operands), len(hosted.out_shapes)

    def hosting_body(*refs):
        ins, rest = refs[:n_in], refs[n_in:]
        c_ins, rest = rest[:n_ci], rest[n_ci:]
        outs, rest = rest[:n_out], rest[n_out:]
        c_outs, rest = rest[:n_co], rest[n_co:]
        hosted.wrap(grid, lambda: body(*ins, *outs, *rest[:n_scr]), c_ins, c_outs, rest[n_scr:])

    res = pl.pallas_call(
        hosting_body, name=name, grid=grid, out_shape=tuple(list(out_shapes) + hosted.out_shapes),
        in_specs=list(in_specs) + hosted.in_specs, out_specs=tuple(list(out_specs) + hosted.out_specs),
        input_output_aliases=hosted.aliases, scratch_shapes=list(scratch) + hosted.scratch,
        compiler_params=_cparams(len(grid)))(*operands, *hosted.operands)
    return list(res[:n_out]), list(res[n_out:])


def _matmul(name, grid, nk, kaxis, pairs, dims, extras, outs, epilogue, sum_pairs, acc_shape, comm=None, split=None):
    n_p, n_e, n_o = len(pairs), len(extras), len(outs)
    n_acc = 0 if nk == 1 else (1 if sum_pairs else n_p)
    n_in = 2 * n_p + n_e
    hosted = _Hosted(comm, n_in, n_o)
    n_ci, n_co = len(hosted.operands), len(hosted.out_shapes)

    def body(*refs):
        a_refs = refs[0:2 * n_p:2]
        b_refs = refs[1:2 * n_p:2]
        e_refs = refs[2 * n_p:n_in]
        c_ins = refs[n_in:n_in + n_ci]
        o_refs = refs[n_in + n_ci:n_in + n_ci + n_o]
        c_outs = refs[n_in + n_ci + n_o:n_in + n_ci + n_o + n_co]
        acc_refs = refs[n_in + n_ci + n_o + n_co:n_in + n_ci + n_o + n_co + n_acc]
        sems = refs[n_in + n_ci + n_o + n_co + n_acc:]

        def dots():
            if sum_pairs and n_p > 1 and dims == NN:
                a_all = jnp.concatenate([a[...] for a in a_refs], axis=1)
                b_all = jnp.concatenate([b[...] for b in b_refs], axis=0)
                return [lax.dot_general(a_all, b_all, (dims, ((), ())), preferred_element_type=F32)]
            prods = [lax.dot_general(a[...], b[...], (dims, ((), ())), preferred_element_type=F32)
                     for a, b in zip(a_refs, b_refs)]
            if sum_pairs and n_p > 1:
                prods = [functools.reduce(operator.add, prods)]
            return prods

        def compute():
            if nk == 1 and split is not None:
                n_split, b_axis, n_row = split
                width = b_refs[0].shape[b_axis] // n_split
                height = a_refs[0].shape[0] // n_row
                for s in range(n_split):
                    cols = pl.ds(s * width, width)
                    for r in range(n_row):
                        rows = pl.ds(r * height, height)
                        epilogue([lax.dot_general(a[rows, :], b[cols, :] if b_axis == 0 else b[:, cols], (dims, ((), ())),
                                                  preferred_element_type=F32) for a, b in zip(a_refs, b_refs)],
                                 e_refs, o_refs, rows, cols)
                return
            if nk == 1:
                epilogue(dots(), e_refs, o_refs)
                return
            k = pl.program_id(kaxis)

            @pl.when(k == 0)
            def _():
                for acc, p in zip(acc_refs, dots()):
                    acc[...] = p

            if nk > 2:
                @pl.when((k > 0) & (k < nk - 1))
                def _():
                    for acc, p in zip(acc_refs, dots()):
                        acc[...] += p

            @pl.when(k == nk - 1)
            def _():
                epilogue([acc[...] + p for acc, p in zip(acc_refs, dots())], e_refs, o_refs)

        hosted.wrap(grid, compute, c_ins, c_outs, sems)

    operands, in_specs = [], []
    for a, a_spec, b, b_spec in pairs:
        operands += [a, b]
        in_specs += [a_spec, b_spec]
    for e, e_spec in extras:
        operands.append(e)
        in_specs.append(e_spec)
    res = pl.pallas_call(
        body, name=name, grid=grid,
        out_shape=tuple([o for o, _ in outs] + hosted.out_shapes),
        in_specs=in_specs + hosted.in_specs, out_specs=tuple([s for _, s in outs] + hosted.out_specs),
        input_output_aliases=hosted.aliases,
        scratch_shapes=[pltpu.VMEM(acc_shape, F32) for _ in range(n_acc)] + hosted.scratch,
        compiler_params=_cparams(len(grid)),
    )(*operands, *hosted.operands)
    return list(res[:n_o]), list(res[n_o:])


NN = ((1,), (0,))
NT = ((1,), (1,))
TN = ((0,), (0,))


def _tile(n, want):
    if n <= want:
        return n
    t = want // 128 * 128
    while n % t:
        t -= 128
    return t


def _silu_parts(g):
    s = 0.5 + 0.5 * jnp.tanh(0.5 * g)
    return s, g * s


def _mm_in(h, w_in, comm=None):
    t, d = h.shape
    n = w_in.shape[1]
    tm, tn = _tile(t, _TM), _tile(n, _TN)

    def epi(accs, e, o):
        o[0][...] = accs[0].astype(BF16)

    outs, couts = _matmul(
        "mm_in", (n // tn, t // tm), 1, None,
        [(h, pl.BlockSpec((tm, d), lambda j, i: (i, 0)), w_in, pl.BlockSpec((d, tn), lambda j, i: (0, j)))],
        NN, [], [(_sds((t, n), BF16), pl.BlockSpec((tm, tn), lambda j, i: (i, j)))], epi, True, None, comm)
    return outs[0], couts


def _mm_out(y, w_out, x, comm=None):
    t, m = y.shape
    d = w_out.shape[1]
    tm, tn = _tile(t, _TM), _tile(d, _TN)

    def epi(accs, e, o):
        o[0][...] = e[0][...] + accs[0]

    outs, couts = _matmul(
        "mm_out", (t // tm, d // tn), 1, None,
        [(y, pl.BlockSpec((tm, m), lambda i, j: (i, 0)), w_out, pl.BlockSpec((m, tn), lambda i, j: (0, j)))],
        NN, [(x, pl.BlockSpec((tm, tn), lambda i, j: (i, j)))],
        [(_sds((t, d), F32), pl.BlockSpec((tm, tn), lambda i, j: (i, j)))], epi, True, None, comm)
    return outs[0], couts


def _mm_swiglu(h2, wgt, wut, comm=None):
    t, d = h2.shape
    f = wgt.shape[0]
    tm, tn = _tile(t, 2 * _TM), _tile(f, 512)

    def epi(accs, e, o, rows, cols):
        g, u = accs
        s, sg = _silu_parts(g)
        o[0][rows, cols] = (sg * u).astype(BF16)
        o[1][rows, cols] = (u * (s + sg * (1.0 - s))).astype(BF16)
        o[2][rows, cols] = sg.astype(BF16)

    wspec = pl.BlockSpec((tn, d), lambda i, j: (j, 0))
    hspec = pl.BlockSpec((tm, d), lambda i, j: (i, 0))
    ospec = pl.BlockSpec((tm, tn), lambda i, j: (i, j))
    osh = _sds((t, f), BF16)
    outs, couts = _matmul("mm_swiglu", (t // tm, f // tn), 1, None, [(h2, hspec, wgt, wspec), (h2, hspec, wut, wspec)],
                          NT, [], [(osh, ospec)] * 3, epi, False, None, comm, split=(tn // 256, 0, 2))
    return outs, couts


def _mm_down(act, wd, x1, comm=None):
    t, f = act.shape
    d = wd.shape[1]
    tm, tn = _tile(t, _TM), _tile(d, _TN)
    nk = 2
    tk = f // nk

    def epi(accs, e, o):
        o[0][...] = e[0][...] + accs[0]

    outs, couts = _matmul(
        "mm_down", (t // tm, d // tn, nk), nk, 2,
        [(act, pl.BlockSpec((tm, tk), lambda i, j, k: (i, k)), wd, pl.BlockSpec((tk, tn), lambda i, j, k: (k, j)))],
        NN, [(x1, pl.BlockSpec((tm, tn), lambda i, j, k: (i, j)))],
        [(_sds((t, d), F32), pl.BlockSpec((tm, tn), lambda i, j, k: (i, j)))], epi, True, (tm, tn), comm)
    return outs[0], couts


def _mm_dact(dxb, wd, dact_dgate, dact_dup, comm=None):
    t, d = dxb.shape
    f = wd.shape[0]
    tm, tn = _tile(t, 2 * _TM), _tile(f, 512)

    def epi(accs, e, o, rows, cols):
        da = accs[0]
        o[0][rows, cols] = (da * e[0][rows, cols].astype(F32)).astype(BF16)
        o[1][rows, cols] = (da * e[1][rows, cols].astype(F32)).astype(BF16)

    bspec = pl.BlockSpec((tm, tn), lambda i, j: (i, j))
    osh = _sds((t, f), BF16)
    outs, couts = _matmul(
        "mm_dact", (t // tm, f // tn), 1, None,
        [(dxb, pl.BlockSpec((tm, d), lambda i, j: (i, 0)), wd, pl.BlockSpec((tn, d), lambda i, j: (j, 0)))],
        NT, [(dact_dgate, bspec), (dact_dup, bspec)], [(osh, bspec)] * 2, epi, True, None, comm, split=(tn // 256, 0, 2))
    return outs, couts


def _mm_dh2(dgate, dup, wgt, wut, comm=None):
    t, f = dgate.shape
    d = wgt.shape[1]
    tm, tn = _tile(t, _TM), _tile(d, _TN)
    nk = 4
    tk = f // nk

    def epi(accs, e, o):
        o[0][...] = accs[0].astype(BF16)

    aspec = pl.BlockSpec((tm, tk), lambda i, j, k: (i, k))
    wspec = pl.BlockSpec((tk, tn), lambda i, j, k: (k, j))
    outs, couts = _matmul("mm_dh2", (t // tm, d // tn, nk), nk, 2, [(dgate, aspec, wgt, wspec), (dup, aspec, wut, wspec)],
                          NN, [], [(_sds((t, d), BF16), pl.BlockSpec((tm, tn), lambda i, j, k: (i, j)))], epi, True,
                          (tm, tn), comm)
    return outs[0], couts


def _mm_dw(name, a_list, b, tmo, tno, comm=None, m_rows=None):
    t, m = a_list[0].shape
    start, m = (0, m) if m_rows is None else m_rows
    n = b.shape[1]
    tt = _tile(t, _TT)
    nk = t // tt
    tmo, tno = _tile(m, tmo), _tile(n, tno)
    first = start // tmo

    def epi(accs, e, o):
        for acc, out in zip(accs, o):
            out[...] = acc.astype(BF16)

    aspec = pl.BlockSpec((tt, tmo), lambda i, j, k: (k, first + i))
    bspec = pl.BlockSpec((tt, tno), lambda i, j, k: (k, j))
    ospec = pl.BlockSpec((tmo, tno), lambda i, j, k: (i, j))
    if nk == 1:
        return _matmul(name, (m // tmo, n // tno, 1), 1, None, [(a, aspec, b, bspec) for a in a_list], TN, [],
                       [(_sds((m, n), BF16), ospec)] * len(a_list), epi, False, None, comm)
    return _matmul(name, (m // tmo, n // tno, nk), nk, 2, [(a, aspec, b, bspec) for a in a_list], TN, [],
                   [(_sds((m, n), BF16), ospec)] * len(a_list), epi, False, (tmo, tno), comm)


def _mm_dy(dxb, w_out, comm=None):
    t, d = dxb.shape
    m = w_out.shape[0]
    tm, tn = _tile(t, _TM), _tile(m, _TN)

    def epi(accs, e, o):
        o[0][...] = accs[0].astype(BF16)

    outs, couts = _matmul(
        "mm_dy", (t // tm, m // tn), 1, None,
        [(dxb, pl.BlockSpec((tm, d), lambda i, j: (i, 0)), w_out, pl.BlockSpec((tn, d), lambda i, j: (j, 0)))], NT, [],
        [(_sds((t, m), BF16), pl.BlockSpec((tm, tn), lambda i, j: (i, j)))], epi, True, None, comm)
    return outs[0], couts


def _mm_dh(dz, w_in, comm=None):
    t, n = dz.shape
    d = w_in.shape[0]
    tm, tn = _tile(t, _TM), _tile(d, _TN)
    nk = 2
    tk = n // nk

    def epi(accs, e, o):
        o[0][...] = accs[0].astype(BF16)

    outs, couts = _matmul(
        "mm_dh", (t // tm, d // tn, nk), nk, 2,
        [(dz, pl.BlockSpec((tm, tk), lambda i, j, k: (i, k)), w_in, pl.BlockSpec((tn, tk), lambda i, j, k: (j, k)))], NT,
        [], [(_sds((t, d), BF16), pl.BlockSpec((tm, tn), lambda i, j, k: (i, j)))], epi, True, (tm, tn), comm)
    return outs[0], couts


def _rmsnorm_fwd(x, g, comm=None):
    t, d = x.shape
    tm = min(_TM_NORM, t)

    def body(x_ref, g_ref, o_ref):
        xv = x_ref[...]
        rs = lax.rsqrt(jnp.mean(xv * xv, axis=-1, keepdims=True) + RMS_EPS)
        o_ref[...] = (xv * rs * g_ref[...]).astype(BF16)

    outs, couts = _call_hosting(
        body, "rmsnorm_fwd", (t // tm,), [_sds((t, d), BF16)],
        [pl.BlockSpec((tm, d), lambda i: (i, 0)), pl.BlockSpec((1, d), lambda i: (0, 0))],
        [pl.BlockSpec((tm, d), lambda i: (i, 0))], [x, g], [], comm)
    return outs[0], couts


def _rmsnorm_bwd_math(xv, g, dh):
    rs = lax.rsqrt(jnp.mean(xv * xv, axis=-1, keepdims=True) + RMS_EPS)
    xh = xv * rs
    gd = dh * g
    dx = rs * (gd - xh * jnp.mean(gd * xh, axis=-1, keepdims=True))
    return dx, jnp.sum(dh * xh, axis=0, keepdims=True)


def _rmsnorm_bwd(x, g, dh, dres, with_bf16=True):
    t, d = x.shape
    tm = min(_TM_NORM, t)

    def body(x_ref, g_ref, dh_ref, dres_ref, dx_ref, *rest):
        dg_ref = rest[-1]
        dx, dg = _rmsnorm_bwd_math(x_ref[...], g_ref[...], dh_ref[...].astype(F32))
        dx = dx + dres_ref[...]
        dx_ref[...] = dx
        if with_bf16:
            rest[0][...] = dx.astype(BF16)

        @pl.when(pl.program_id(0) == 0)
        def _():
            dg_ref[...] = dg

        @pl.when(pl.program_id(0) > 0)
        def _():
            dg_ref[...] += dg

    row = pl.BlockSpec((tm, d), lambda i: (i, 0))
    vec = pl.BlockSpec((1, d), lambda i: (0, 0))
    halves = [(_sds((t, d), BF16), row)] if with_bf16 else []
    outs = [(_sds((t, d), F32), row), *halves, (_sds((1, d), F32), vec)]
    res = pl.pallas_call(
        body, name="rmsnorm_bwd", grid=(t // tm,), out_shape=tuple(o for o, _ in outs),
        in_specs=[row, vec, row, row], out_specs=tuple(s for _, s in outs), compiler_params=_cparams(1))(x, g, dh, dres)
    return (res[0], res[1], res[2]) if with_bf16 else (res[0], None, res[1])


def _loss_head(x, g, target):
    t, d = x.shape
    tm = min(_TM_NORM, t)

    def body(x_ref, g_ref, t_ref, dx_ref, dxb_ref, dg_ref, loss_ref):
        xv, gv = x_ref[...], g_ref[...]
        rs = lax.rsqrt(jnp.mean(xv * xv, axis=-1, keepdims=True) + RMS_EPS)
        diff = xv * rs * gv - t_ref[...]
        part = 0.5 * jnp.sum(jnp.mean(diff * diff, axis=-1, keepdims=True), axis=0, keepdims=True)
        part = jnp.broadcast_to(part, (1, 128))
        dx, dg = _rmsnorm_bwd_math(xv, gv, diff * (1.0 / d))
        dx_ref[...] = dx
        dxb_ref[...] = dx.astype(BF16)

        @pl.when(pl.program_id(0) == 0)
        def _():
            dg_ref[...] = dg
            loss_ref[...] = part

        @pl.when(pl.program_id(0) > 0)
        def _():
            dg_ref[...] += dg
            loss_ref[...] += part

    row = pl.BlockSpec((tm, d), lambda i: (i, 0))
    vec = pl.BlockSpec((1, d), lambda i: (0, 0))
    return pl.pallas_call(
        body, name="loss_head", grid=(t // tm,),
        out_shape=(_sds((t, d), F32), _sds((t, d), BF16), _sds((1, d), F32), _sds((1, 128), F32)),
        in_specs=[row, vec, row], out_specs=(row, row, vec, pl.BlockSpec((1, 128), lambda i: (0, 0))),
        compiler_params=_cparams(1))(x, g, target)


def _gelu(x):
    th = jnp.tanh(GELU_C * (x + GELU_A * x * x * x))
    return 0.5 * x * (1.0 + th), th


def _gelu_grad(x, th):
    return 0.5 * (1.0 + th) + 0.5 * x * (1.0 - th * th) * GELU_C * (1.0 + 3.0 * GELU_A * x * x)


def _masked_ws(ws_ref, h):
    i = lax.broadcasted_iota(jnp.int32, (BLK, BLK), 0) // CHUNK
    j = lax.broadcasted_iota(jnp.int32, (BLK, BLK), 1) // CHUNK
    return jnp.where(j <= i, ws_ref[h], 0.0)


def _shift_down(q, n, first_rows):
    rolled = pltpu.roll(q, n, 0)
    row = lax.broadcasted_iota(jnp.int32, q.shape, 0)
    for r, val in enumerate(first_rows):
        rolled = jnp.where(row == r, val, rolled)
    return rolled


def _shift_up(q, n, last_rows):
    tm = q.shape[0]
    rolled = pltpu.roll(q, tm - n, 0)
    row = lax.broadcasted_iota(jnp.int32, q.shape, 0)
    for r, val in enumerate(last_rows):
        rolled = jnp.where(row == tm - n + r, val, rolled)
    return rolled


def _mixer_specs(t, a, tm):
    hb = tm // HALO
    last = t // HALO - 1
    tile = pl.BlockSpec((tm, 5 * a), lambda i: (i, 0))
    prev = [pl.BlockSpec((HALO, a), functools.partial(lambda i, col: (jnp.maximum(i * hb - 1, 0), col), col=col))
            for col in (3, 4)]
    nxt = [pl.BlockSpec((HALO, a), functools.partial(lambda i, col: (jnp.minimum((i + 1) * hb, last), col), col=col))
           for col in (2, 3, 4)]
    return tile, prev, nxt


def _group_a_fwd(zu, zv, lng, lnb, ws_ref, bb_ref, mixed_ref, vln_ref):
    u, thu = _gelu(zu)
    v, thv = _gelu(zv)
    mu = jnp.mean(v, axis=-1, keepdims=True)
    vc = v - mu
    rs = lax.rsqrt(jnp.mean(vc * vc, axis=-1, keepdims=True) + LN_EPS)
    vhat = vc * rs
    vln_ref[...] = vhat * lng + lnb
    tm, a = zu.shape
    hd = a // HEADS
    for h in range(HEADS):
        w = _masked_ws(ws_ref, h).astype(BF16)
        for b in range(tm // BLK):
            rows, cols = pl.ds(b * BLK, BLK), pl.ds(h * hd, hd)
            mixed_ref[rows, cols] = jnp.dot(w, vln_ref[rows, cols].astype(BF16), preferred_element_type=F32) + bb_ref[h]
    return u, thu, thv, rs, vhat


def _mixer_fwd(z, ln_g, ln_b, w_spatial, bb, conv_w, gg, comm=None):
    t = z.shape[0]
    a = z.shape[1] // 5
    tm = min(_TM_MIX, t)
    tile, prev, _ = _mixer_specs(t, a, tm)

    def body(z_ref, pc_ref, ph_ref, lng_ref, lnb_ref, ws_ref, bb_ref, cw_ref, gg_ref, y_ref, mixed_ref, vln_ref):
        i = pl.program_id(0)
        zu = z_ref[:, 0:a].astype(F32)
        zv = z_ref[:, a:2 * a].astype(F32)
        u, _, _, _, _ = _group_a_fwd(zu, zv, lng_ref[...], lnb_ref[...], ws_ref, bb_ref, mixed_ref, vln_ref)
        ya = u * mixed_ref[...]
        ra = lax.rsqrt(jnp.mean(ya * ya, axis=-1, keepdims=True) + RMS_EPS)
        y_ref[:, 0:a] = (ya * ra * gg_ref[:, 0:a]).astype(BF16)

        zb = z_ref[:, 2 * a:3 * a].astype(F32)
        q = z_ref[:, 3 * a:4 * a].astype(F32) * z_ref[:, 4 * a:5 * a].astype(F32)
        qp = jnp.where(i > 0, pc_ref[...].astype(F32) * ph_ref[...].astype(F32), 0.0)
        qm1 = _shift_down(q, 1, [qp[HALO - 1:HALO]])
        qm2 = _shift_down(q, 2, [qp[HALO - 2:HALO - 1], qp[HALO - 1:HALO]])
        cv = cw_ref[0:1, :] * qm2 + cw_ref[1:2, :] * qm1 + cw_ref[2:3, :] * q
        yb = zb * cv
        rb = lax.rsqrt(jnp.mean(yb * yb, axis=-1, keepdims=True) + RMS_EPS)
        y_ref[:, a:2 * a] = (yb * rb * gg_ref[:, a:2 * a]).astype(BF16)

    full = lambda shape: pl.BlockSpec(shape, lambda i: (0,) * len(shape))
    outs, couts = _call_hosting(
        body, "mixer_fwd", (t // tm,), [_sds((t, 2 * a), BF16)],
        [tile, *prev, full((1, a)), full((1, a)), full(w_spatial.shape), full(bb.shape), full(conv_w.shape),
         full((1, 2 * a))],
        [pl.BlockSpec((tm, 2 * a), lambda i: (i, 0))], [z, z, z, ln_g, ln_b, w_spatial, bb, conv_w, gg],
        [pltpu.VMEM((tm, a), F32), pltpu.VMEM((tm, a), F32)], comm)
    return outs[0], couts


def _mixer_bwd(z, dy, ln_g, ln_b, w_spatial, bb, conv_w, gg, comm=None):
    t = z.shape[0]
    a = z.shape[1] // 5
    hd = a // HEADS
    tm = min(_TM_MIX, t)
    n_tiles = t // tm
    tile, prev, nxt = _mixer_specs(t, a, tm)
    hb = tm // HALO
    dy_tile = pl.BlockSpec((tm, 2 * a), lambda i: (i, 0))
    dy_next = pl.BlockSpec((HALO, a), lambda i: (jnp.minimum((i + 1) * hb, t // HALO - 1), 1))

    def body(z_ref, pc_ref, ph_ref, nb_ref, nc_ref, nh_ref, dy_ref, ndy_ref, lng_ref, lnb_ref, ws_ref, bb_ref, cw_ref,
             gg_ref, dz_ref, dlng_ref, dlnb_ref, dws_ref, dbb_ref, dcw_ref, dgg_ref, mixed_ref, vln_ref, dmix_ref,
             dvln_ref):
        i = pl.program_id(0)

        @pl.when(i == 0)
        def _():
            for ref in (dlng_ref, dlnb_ref, dws_ref, dbb_ref, dcw_ref, dgg_ref):
                ref[...] = jnp.zeros(ref.shape, F32)

        lng = lng_ref[...]
        zu = z_ref[:, 0:a].astype(F32)
        zv = z_ref[:, a:2 * a].astype(F32)
        u, thu, thv, rs, vhat = _group_a_fwd(zu, zv, lng, lnb_ref[...], ws_ref, bb_ref, mixed_ref, vln_ref)
        mixed = mixed_ref[...]
        ya = u * mixed
        ra = lax.rsqrt(jnp.mean(ya * ya, axis=-1, keepdims=True) + RMS_EPS)
        da = dy_ref[:, 0:a].astype(F32)
        yah = ya * ra
        dgg_ref[:, 0:a] += jnp.sum(da * yah, axis=0, keepdims=True)
        ga = da * gg_ref[:, 0:a]
        dya = ra * (ga - yah * jnp.mean(ga * yah, axis=-1, keepdims=True))
        dz_ref[:, 0:a] = (dya * mixed * _gelu_grad(zu, thu)).astype(BF16)
        dmix_ref[...] = dya * u
        for h in range(HEADS):
            w = _masked_ws(ws_ref, h).astype(BF16)
            dw = jnp.zeros((BLK, BLK), F32)
            db = jnp.zeros((BLK, hd), F32)
            for b in range(tm // BLK):
                rows, cols = pl.ds(b * BLK, BLK), pl.ds(h * hd, hd)
                dm = dmix_ref[rows, cols]
                dmb = dm.astype(BF16)
                db = db + dm
                dw = dw + lax.dot_general(dmb, vln_ref[rows, cols].astype(BF16), (NT, ((), ())),
                                          preferred_element_type=F32)
                dvln_ref[rows, cols] = lax.dot_general(w, dmb, (TN, ((), ())), preferred_element_type=F32)
            dws_ref[h] += dw
            dbb_ref[h] += db
        dvln = dvln_ref[...]
        dlng_ref[...] += jnp.sum(dvln * vhat, axis=0, keepdims=True)
        dlnb_ref[...] += jnp.sum(dvln, axis=0, keepdims=True)
        dvh = dvln * lng
        dv = rs * (dvh - jnp.mean(dvh, axis=-1, keepdims=True) - vhat * jnp.mean(dvh * vhat, axis=-1, keepdims=True))
        dz_ref[:, a:2 * a] = (dv * _gelu_grad(zv, thv)).astype(BF16)

        w0, w1, w2 = cw_ref[0:1, :], cw_ref[1:2, :], cw_ref[2:3, :]
        ggb = gg_ref[:, a:2 * a]
        zb = z_ref[:, 2 * a:3 * a].astype(F32)
        zc = z_ref[:, 3 * a:4 * a].astype(F32)
        zh = z_ref[:, 4 * a:5 * a].astype(F32)
        q = zc * zh
        qp = jnp.where(i > 0, pc_ref[...].astype(F32) * ph_ref[...].astype(F32), 0.0)
        qm1 = _shift_down(q, 1, [qp[HALO - 1:HALO]])
        qm2 = _shift_down(q, 2, [qp[HALO - 2:HALO - 1], qp[HALO - 1:HALO]])
        cv = w0 * qm2 + w1 * qm1 + w2 * q

        def conv_out_grad(zb_, cv_, dout_):
            yb = zb_ * cv_
            rb = lax.rsqrt(jnp.mean(yb * yb, axis=-1, keepdims=True) + RMS_EPS)
            ybh = yb * rb
            gb = dout_ * ggb
            dyb = rb * (gb - ybh * jnp.mean(gb * ybh, axis=-1, keepdims=True))
            return dyb * zb_, dyb * cv_, ybh

        db_out = dy_ref[:, a:2 * a].astype(F32)
        g, dzb, ybh = conv_out_grad(zb, cv, db_out)
        dgg_ref[:, a:2 * a] += jnp.sum(db_out * ybh, axis=0, keepdims=True)
        dz_ref[:, 2 * a:3 * a] = dzb.astype(BF16)
        qn = nc_ref[...].astype(F32) * nh_ref[...].astype(F32)
        zbn = nb_ref[...].astype(F32)
        cvn = w0 * _shift_down(qn, 2, [q[tm - 2:tm - 1], q[tm - 1:tm]]) + w1 * _shift_down(qn, 1, [q[tm - 1:tm]]) + w2 * qn
        gn, _, _ = conv_out_grad(zbn, cvn, ndy_ref[...].astype(F32))
        gn = jnp.where(i < n_tiles - 1, gn, 0.0)
        dq = w2 * g + w1 * _shift_up(g, 1, [gn[0:1]]) + w0 * _shift_up(g, 2, [gn[0:1], gn[1:2]])
        dz_ref[:, 3 * a:4 * a] = (dq * zh).astype(BF16)
        dz_ref[:, 4 * a:5 * a] = (dq * zc).astype(BF16)
        dcw_ref[0:1, :] += jnp.sum(g * qm2, axis=0, keepdims=True)
        dcw_ref[1:2, :] += jnp.sum(g * qm1, axis=0, keepdims=True)
        dcw_ref[2:3, :] += jnp.sum(g * q, axis=0, keepdims=True)

        @pl.when(i == n_tiles - 1)
        def _():
            for h in range(HEADS):
                dbb_ref[h] = jnp.broadcast_to(jnp.sum(dbb_ref[h], axis=1, keepdims=True), (BLK, hd))
                dws_ref[h] = _masked_ws(dws_ref, h)

    full = lambda shape: pl.BlockSpec(tuple(shape), lambda i: (0,) * len(shape))
    out_shapes = (_sds((t, 5 * a), BF16), _sds((1, a), F32), _sds((1, a), F32), _sds(w_spatial.shape, F32),
                  _sds(bb.shape, F32), _sds((8, a), F32), _sds((1, 2 * a), F32))
    return _call_hosting(
        body, "mixer_bwd", (n_tiles,), out_shapes,
        [tile, *prev, *nxt, dy_tile, dy_next, full((1, a)), full((1, a)), full(w_spatial.shape), full(bb.shape),
         full(conv_w.shape), full((1, 2 * a))],
        [tile, *[full(s.shape) for s in out_shapes[1:]]], [z, z, z, z, z, z, dy, dy, ln_g, ln_b, w_spatial, bb, conv_w, gg],
        [pltpu.VMEM((tm, a), F32)] * 4, comm)


def _all_reduce_small(pack, comm=None):
    r = pack.shape[0]
    hosted = _Hosted(comm, 1, 1)
    n_ci, n_co = len(hosted.operands), len(hosted.out_shapes)

    def body(*refs):
        in_ref, c_ins, out_ref, c_outs = refs[0], refs[1:1 + n_ci], refs[1 + n_ci], refs[2 + n_ci:2 + n_ci + n_co]
        acc_ref, recv_ref, send_sems, recv_sems = refs[2 + n_ci + n_co:6 + n_ci + n_co]
        sems = refs[6 + n_ci + n_co:]
        hosted.run("start", c_ins, c_outs, sems)
        x, y, c = _place()
        partners = [(x, y, 1 - c), (1 - x, y, c), (x, 1 - y, c)]
        acc_ref[0] = in_ref[...]
        for s, partner in enumerate(partners):
            cp = pltpu.make_async_remote_copy(
                src_ref=acc_ref.at[s], dst_ref=recv_ref.at[s], send_sem=send_sems.at[s], recv_sem=recv_sems.at[s],
                device_id=partner, device_id_type=MESH)
            cp.start()
            cp.wait()
            if s < 2:
                acc_ref[s + 1] = acc_ref[s] + recv_ref[s]
            else:
                out_ref[...] = acc_ref[s] + recv_ref[s]
        for stage in ("mid1", "mid2", "finish"):
            hosted.run(stage, c_ins, c_outs, sems)

    vmem = pl.BlockSpec(memory_space=pltpu.VMEM)
    res = pl.pallas_call(
        body, name="all_reduce_small", out_shape=tuple([_sds(pack.shape, F32)] + hosted.out_shapes),
        in_specs=[vmem] + hosted.in_specs, out_specs=tuple([vmem] + hosted.out_specs),
        input_output_aliases=hosted.aliases,
        scratch_shapes=[pltpu.VMEM((3, r, 128), F32), pltpu.VMEM((3, r, 128), F32), pltpu.SemaphoreType.DMA((3,)),
                        pltpu.SemaphoreType.DMA((3,))] + hosted.scratch,
        compiler_params=pltpu.CompilerParams(vmem_limit_bytes=VMEM_LIMIT_V7X),
    )(pack, *hosted.operands)
    return res[0], list(res[1:])


def _adamw_math(w, g, m, v):
    m = ADAM_B1 * m + (1.0 - ADAM_B1) * g
    v = ADAM_B2 * v + (1.0 - ADAM_B2) * (g * g)
    m_hat = m / (1.0 - ADAM_B1 ** ADAM_STEP)
    v_hat = v / (1.0 - ADAM_B2 ** ADAM_STEP)
    delta = -ADAM_LR * (m_hat / (jnp.sqrt(v_hat) + ADAM_EPS) + ADAM_WD * w)
    return delta, m, v


def _adamw_big(name, land, w, m, v, comm=None):
    nl, n_slots, r, c = land.shape
    tr = max(8, min(r, (256 * 640) // c // 8 * 8))
    while r % tr:
        tr -= 8
    grid = (nl, r // tr)
    hosted = _Hosted(comm, 4, 4)
    n_ci, n_co = len(hosted.operands), len(hosted.out_shapes)

    def body(*refs):
        land_ref, w_ref, m_ref, v_ref = refs[:4]
        c_ins = refs[4:4 + n_ci]
        g_out, d_out, m_out, v_out = refs[4 + n_ci:8 + n_ci]
        c_outs = refs[8 + n_ci:8 + n_ci + n_co]
        sems = refs[8 + n_ci + n_co:]

        def compute():
            g = land_ref[0].astype(F32)
            for s in range(1, n_slots):
                g = g + land_ref[s].astype(F32)
            delta, mn, vn = _adamw_math(w_ref[...], g, m_ref[...], v_ref[...])
            g_out[...] = g
            d_out[...] = delta
            m_out[...] = mn
            v_out[...] = vn

        hosted.wrap(grid, compute, c_ins, c_outs, sems)

    blk = pl.BlockSpec((None, tr, c), lambda l, i: (l, i, 0))
    res = pl.pallas_call(
        body, name=name, grid=grid, out_shape=tuple([_sds((nl, r, c), F32)] * 4 + hosted.out_shapes),
        in_specs=[pl.BlockSpec((None, n_slots, tr, c), lambda l, i: (l, 0, i, 0)), blk, blk, blk] + hosted.in_specs,
        out_specs=tuple([blk] * 4 + hosted.out_specs), input_output_aliases=hosted.aliases,
        scratch_shapes=hosted.scratch, compiler_params=_cparams(2))(land, w, m, v, *hosted.operands)
    return list(res[:4]), list(res[4:])


def _adamw_small(gs, ws, ms, vs):
    n = len(gs)

    def body(*refs):
        g_refs, w_refs, m_refs, v_refs = refs[:n], refs[n:2 * n], refs[2 * n:3 * n], refs[3 * n:4 * n]
        d_outs, m_outs, v_outs = refs[4 * n:5 * n], refs[5 * n:6 * n], refs[6 * n:7 * n]
        for i in range(n):
            delta, mn, vn = _adamw_math(w_refs[i][...], g_refs[i][...], m_refs[i][...], v_refs[i][...])
            d_outs[i][...] = delta
            m_outs[i][...] = mn
            v_outs[i][...] = vn

    shapes = [_sds(g.shape, F32) for g in gs]
    res = pl.pallas_call(body, name="adamw_small", out_shape=tuple(shapes * 3),
                         compiler_params=pltpu.CompilerParams(vmem_limit_bytes=VMEM_LIMIT_V7X))(*gs, *ws, *ms, *vs)
    return list(res[:n]), list(res[n:2 * n]), list(res[2 * n:])


def _rows(a):
    return a.reshape(-1, 128)


BIG = ["w_in", "w_out", "w_gate", "w_up", "w_down"]
AG_HOSTS = {
    ("norm1", 0): [("w_in", 0), ("conv_w", 0), ("w_out", 0)],
    ("mm_in", 0): [("w_gate", 0)], ("mixer", 0): [("w_up", 0, 0, 2)], ("mm_out", 0): [("w_up", 0, 1, 2)],
    ("mm_swiglu", 0): [("w_down", 0), ("w_in", 1)], ("mm_down", 0): [("w_out", 1), ("w_gate", 1, 0, 2)],
    ("mm_in", 1): [("w_gate", 1, 1, 2)], ("mixer", 1): [("w_up", 1, 0, 2)], ("mm_out", 1): [("w_up", 1, 1, 2)],
    ("mm_swiglu", 1): [("w_down", 1)],
}


def kernel(x, norm1_g, w_in, gmlp_ln_g, gmlp_ln_b, w_spatial, b_spatial, conv_w, group_norm_g, w_out, norm2_g, w_gate, w_up, w_down, final_norm_g, loss_target, m_norm1_g, m_w_in, m_gmlp_ln_g, m_gmlp_ln_b, m_w_spatial, m_b_spatial, m_conv_w, m_group_norm_g, m_w_out, m_norm2_g, m_w_gate, m_w_up, m_w_down, m_final_norm_g, v_norm1_g, v_w_in, v_gmlp_ln_g, v_gmlp_ln_b, v_w_spatial, v_b_spatial, v_conv_w, v_group_norm_g, v_w_out, v_norm2_g, v_w_gate, v_w_up, v_w_down, v_final_norm_g):
    nl = N_LAYERS
    t, d = x.shape[1], x.shape[2]
    a = d // 2
    hd = a // HEADS
    xin = x.reshape(t, d)
    target = loss_target.reshape(t, d)
    me = _index(_place())

    tr = lambda w: jnp.transpose(w, (0, 2, 1))
    big = {"w_in": w_in, "w_out": w_out, "w_gate": tr(w_gate), "w_up": tr(w_up), "w_down": w_down}
    big_m = {"w_in": m_w_in, "w_out": m_w_out, "w_gate": tr(m_w_gate), "w_up": tr(m_w_up), "w_down": m_w_down}
    big_v = {"w_in": v_w_in, "w_out": v_w_out, "w_gate": tr(v_w_gate), "w_up": tr(v_w_up), "w_down": v_w_down}
    block = {k: big[k].shape[1:] for k in BIG}
    view = {k: _cols_view(block[k][1]) if k == "w_in" else _rows_view(block[k][0]) for k in BIG}
    full_shape = {k: (block[k][0], N_DEV * block[k][1]) if k == "w_in" else (N_DEV * block[k][0], block[k][1])
                  for k in BIG}

    weights = {}
    shards = {(k, l): big[k][l].astype(BF16) for k in BIG for l in range(nl)}

    def ag_spec(k, l, part=0, n_parts=1):
        if k == "conv_w":
            return (conv_w, _sds((N_DEV, *conv_w.shape), F32), _SLOT_WHOLE, (0,), None)
        halves = (_cols_halves(*block[k], part, n_parts) if k == "w_in" else _rows_halves(block[k][0], part, n_parts))
        return (shards[(k, l)], _sds(full_shape[k], BF16), halves, (0, 1), weights.get((k, l)))

    bb = jnp.broadcast_to(b_spatial[..., None], (nl, HEADS, BLK, hd))

    def hosted(name, l):
        keys = AG_HOSTS.get((name, l), [])
        return keys, ([_ag_piece([ag_spec(*key) for key in keys])] if keys else None)

    def landed(keys, couts):
        for key, arr in zip(keys, couts):
            weights[key[:2]] = arr

    saved = []
    xl = xin
    for l in range(nl):
        keys, comm = hosted("norm1", l)
        h, couts = _rmsnorm_fwd(xl, norm1_g[l:l + 1], comm)
        landed(keys, couts)
        if l == 0:
            conv_full = jnp.transpose(weights[("conv_w", 0)], (1, 2, 0, 3)).reshape(nl, 3, a)
        keys, comm = hosted("mm_in", l)
        z, couts = _mm_in(h, weights[("w_in", l)], comm)
        landed(keys, couts)
        keys, comm = hosted("mixer", l)
        y, couts = _mixer_fwd(z, gmlp_ln_g[l:l + 1], gmlp_ln_b[l:l + 1], w_spatial[l], bb[l], conv_full[l],
                              group_norm_g[l:l + 1], comm)
        landed(keys, couts)
        keys, comm = hosted("mm_out", l)
        x1, couts = _mm_out(y, weights[("w_out", l)], xl, comm)
        landed(keys, couts)
        keys, comm = hosted("norm2", l)
        h2, couts = _rmsnorm_fwd(x1, norm2_g[l:l + 1], comm)
        landed(keys, couts)
        keys, comm = hosted("mm_swiglu", l)
        (act, dact_dgate, dact_dup), couts = _mm_swiglu(h2, weights[("w_gate", l)], weights[("w_up", l)], comm)
        landed(keys, couts)
        keys, comm = hosted("mm_down", l)
        x2, couts = _mm_down(act, weights[("w_down", l)], x1, comm)
        landed(keys, couts)
        saved.append(dict(x=xl, h=h, z=z, y=y, x1=x1, h2=h2, dact_dgate=dact_dgate, dact_dup=dact_dup, act=act))
        xl = x2

    dx, dxb, d_final_g, loss_part = _loss_head(xl, final_norm_g.reshape(1, d), target)
    small = [None] * nl
    core = lax.axis_index("c").astype(jnp.int32).reshape(1)
    in_rows = block["w_in"][0]
    part_of = {"w_in_a": ("w_in", 0), "w_in_b": ("w_in", 3 * in_rows // 4)}
    block["w_in_a"], block["w_in_b"] = (3 * in_rows // 4, block["w_in"][1]), (in_rows // 4, block["w_in"][1])
    for k in part_of:
        view[k] = view["w_in"]
    stage_shape = {k: _sds((N_CHIPS, *block[k]), BF16) for k in block}
    land_shape = {k: _sds((nl, N_CHIPS, *block[k]), BF16) for k in BIG}
    grads = [dict() for _ in range(nl)]
    stages = [dict() for _ in range(nl)]
    sums = [dict() for _ in range(nl)]
    lands = {k: None for k in BIG}

    def core_job(l, keys):
        def sink(outs):
            stages[l].update(zip(keys, outs))
        return _rs_core_piece([(grads[l][k], stage_shape[k], view[k]) for k in keys]), sink

    def chip_job(l, items):
        keys = [part_of.get(item[0], (item[0], 0))[0] for item in items]

        def rows(k, p0, p1, n_parts):
            per = block[k][0] // n_parts
            landing = part_of.get(k, (k, 0))[1]
            return (p0 * per, landing + p0 * per, (p1 - p0) * per)

        def sink(outs):
            lands.update(zip(keys, outs))
        return _rs_chip_piece([(sums[l][k], land_shape[key], rows(k, p0, p1, n_parts), lands[key])
                               for key, (k, p0, p1, n_parts) in zip(keys, items)], l), sink

    def add_up(l, keys):
        for k in keys:
            sums[l][k] = _chip_sums(f"chip_sums_{k}", grads[l][k], stages[l][k], k.startswith("w_in"), core)

    def host(*jobs):
        def deliver(couts):
            i = 0
            for piece, sink in jobs:
                n_out = len(piece.out_shapes)
                sink(couts[i:i + n_out])
                i += n_out
        return [piece for piece, _ in jobs], deliver

    whole = lambda k: (k, 0, 1, 1)
    rep = ["norm1_g", "gmlp_ln_g", "gmlp_ln_b", "w_spatial", "b_spatial", "group_norm_g", "norm2_g"]
    rep_w = dict(norm1_g=norm1_g, gmlp_ln_g=gmlp_ln_g, gmlp_ln_b=gmlp_ln_b, w_spatial=w_spatial, b_spatial=b_spatial,
                 group_norm_g=group_norm_g, norm2_g=norm2_g)
    rep_m = dict(norm1_g=m_norm1_g, gmlp_ln_g=m_gmlp_ln_g, gmlp_ln_b=m_gmlp_ln_b, w_spatial=m_w_spatial,
                 b_spatial=m_b_spatial, group_norm_g=m_group_norm_g, norm2_g=m_norm2_g)
    rep_v = dict(norm1_g=v_norm1_g, gmlp_ln_g=v_gmlp_ln_g, gmlp_ln_b=v_gmlp_ln_b, w_spatial=v_w_spatial,
                 b_spatial=v_b_spatial, group_norm_g=v_group_norm_g, norm2_g=v_norm2_g)

    def small_grad_parts():
        parts = [_rows(jnp.stack([small[l][k].reshape(rep_w[k].shape[1:]) for l in range(nl)])) for k in rep]
        parts.append(_rows(d_final_g))
        parts.append(_rows(jnp.stack([small[l]["conv_w"] for l in range(nl)])))
        parts.append(jnp.broadcast_to(loss_part, (8, 128)))
        rows = sum(p.shape[0] for p in parts)
        parts.append(jnp.zeros((-rows % 16, 128), F32))
        return parts

    for l in reversed(range(nl)):
        s = saved[l]
        wi, wo, wgt, wut, wd = [weights[(k, l)] for k in BIG]
        later = l + 1 < nl
        comm, deliver = host(chip_job(l + 1, [("w_in", 0, 1, 2)])) if later else host()
        (grads[l]["w_down"],), couts = _mm_dw("mm_dw_down", [s["act"]], dxb, 2816, 1024, comm)
        deliver(couts)
        comm, deliver = (host(core_job(l, ["w_down"]), chip_job(l + 1, [("w_in", 1, 2, 2)])) if later
                         else host(core_job(l, ["w_down"])))
        (dgate, dup), couts = _mm_dact(dxb, wd, s["dact_dgate"], s["dact_dup"], comm)
        deliver(couts)
        add_up(l, ["w_down"])
        comm, deliver = host(chip_job(l, [("w_down", 0, 3, 4)]))
        (grads[l]["w_gate"],), couts = _mm_dw("mm_dw_gate", [dgate], s["h2"], 2816, 1024, comm)
        deliver(couts)
        comm, deliver = host(chip_job(l, [("w_down", 3, 4, 4)]), core_job(l, ["w_gate"]))
        (grads[l]["w_up"],), couts = _mm_dw("mm_dw_up", [dup], s["h2"], 2816, 1024, comm)
        deliver(couts)
        add_up(l, ["w_gate"])
        comm, deliver = host(chip_job(l, [whole("w_gate")]), core_job(l, ["w_up"]))
        dh2, couts = _mm_dh2(dgate, dup, wgt, wut, comm)
        deliver(couts)
        add_up(l, ["w_up"])
        dx1, dx1b, d_n2 = _rmsnorm_bwd(s["x1"], norm2_g[l:l + 1], dh2, dx)
        comm, deliver = host(chip_job(l, [("w_up", 0, 1, 4)]))
        dy, couts = _mm_dy(dx1b, wo, comm)
        deliver(couts)
        comm, deliver = host(chip_job(l, [("w_up", 1, 2, 4)]))
        (grads[l]["w_out"],), couts = _mm_dw("mm_dw_out", [s["y"]], dx1b, 1024, 1024, comm)
        deliver(couts)
        comm, deliver = host(chip_job(l, [("w_up", 2, 4, 4)]), core_job(l, ["w_out"]))
        (dz, d_lng, d_lnb, d_ws, d_bb, d_cw, d_gg), couts = _mixer_bwd(
            s["z"], dy, gmlp_ln_g[l:l + 1], gmlp_ln_b[l:l + 1], w_spatial[l], bb[l], conv_full[l], group_norm_g[l:l + 1],
            comm)
        deliver(couts)
        add_up(l, ["w_out"])
        small[l] = dict(norm1_g=jnp.zeros((1, d), F32), gmlp_ln_g=d_lng, gmlp_ln_b=d_lnb, w_spatial=d_ws,
                        b_spatial=d_bb[:, :, 0], group_norm_g=d_gg, norm2_g=d_n2, conv_w=d_cw[0:3])
        if l > 0:
            comm, deliver = host(chip_job(l, [whole("w_out")]))
            (grads[l]["w_in"],), couts = _mm_dw("mm_dw_in", [s["h"]], dz, 2048, 1024, comm)
            deliver(couts)
            comm, deliver = host(core_job(l, ["w_in"]))
            dh, couts = _mm_dh(dz, wi, comm)
            deliver(couts)
            add_up(l, ["w_in"])
        else:
            parts = small_grad_parts()
            reduced = []
            comm, deliver = host(chip_job(l, [whole("w_out")]),
                                 (_all_reduce_piece(jnp.concatenate(parts, axis=0)), reduced.extend))
            (grads[l]["w_in_a"],), couts = _mm_dw("mm_dw_in_a", [s["h"]], dz, block["w_in_a"][0], 1024, comm,
                                                  m_rows=(0, block["w_in_a"][0]))
            deliver(couts)
            comm, deliver = host(core_job(l, ["w_in_a"]))
            (grads[l]["w_in_b"],), couts = _mm_dw("mm_dw_in_b", [s["h"]], dz, block["w_in_b"][0], 2560, comm,
                                                  m_rows=(block["w_in_a"][0], block["w_in_b"][0]))
            deliver(couts)
            add_up(l, ["w_in_a"])
            comm, deliver = host(chip_job(l, [whole("w_in_a")]), core_job(l, ["w_in_b"]))
            dh, couts = _mm_dh(dz, wi, comm)
            deliver(couts)
            add_up(l, ["w_in_b"])
        dx, dxb, small[l]["norm1_g"] = _rmsnorm_bwd(s["x"], norm1_g[l:l + 1], dh, dx1, with_bf16=l > 0)
    grad_x = dx.reshape(x.shape)

    sizes = [p.shape[0] for p in parts]
    comm, deliver = host(chip_job(0, [whole("w_in_b")]))
    last, couts = _all_reduce_small(_rows(small[0]["norm1_g"]), comm)
    deliver(couts)
    total = lax.dynamic_update_slice(reduced[0], last, (0, 0))
    offs = [0]
    for n in sizes:
        offs.append(offs[-1] + n)
    pieces = [total[offs[i]:offs[i + 1]] for i in range(len(parts))]
    loss = pieces[len(rep) + 2][0, 0]
    conv_g_full = pieces[len(rep) + 1].reshape(nl, 3, N_DEV, a // N_DEV)
    conv_g = lax.dynamic_index_in_dim(conv_g_full, me, axis=2, keepdims=False)
    names = rep + ["final_norm_g", "conv_w"]
    flat = lambda w: w.reshape(-1, w.shape[-1])
    small_w = [flat(rep_w[k]) for k in rep] + [flat(final_norm_g), flat(conv_w)]
    small_m = [flat(rep_m[k]) for k in rep] + [flat(m_final_norm_g), flat(m_conv_w)]
    small_v = [flat(rep_v[k]) for k in rep] + [flat(v_final_norm_g), flat(v_conv_w)]
    small_g = [pieces[i].reshape(small_w[i].shape) for i in range(len(rep) + 1)] + [flat(conv_g)]
    small_d, small_m, small_v = _adamw_small(small_g, small_w, small_m, small_v)
    shape_of = dict(rep_w, final_norm_g=final_norm_g, conv_w=conv_w)
    named = lambda arrays: {k: arr.reshape(shape_of[k].shape) for k, arr in zip(names, arrays)}
    res = {"grad": named(small_g), "delta": named(small_d), "m": named(small_m), "v": named(small_v)}

    for k in BIG:
        outs, _ = _adamw_big(f"adamw_{k}", lands[k], big[k], big_m[k], big_v[k])
        if k in ("w_gate", "w_up"):
            outs = [tr(o) for o in outs]
        res["grad"][k], res["delta"][k], res["m"][k], res["v"][k] = outs

    order = ["norm1_g", "w_in", "gmlp_ln_g", "gmlp_ln_b", "w_spatial", "b_spatial", "conv_w", "group_norm_g", "w_out",
             "norm2_g", "w_gate", "w_up", "w_down", "final_norm_g"]
    return (loss, grad_x, *[res["grad"][k] for k in order], *[res["delta"][k] for k in order],
            *[res["m"][k] for k in order], *[res["v"][k] for k in order])
```

```python
import functools
import math
import operator

import jax
import jax.numpy as jnp
from jax import lax
from jax.experimental import pallas as pl
from jax.experimental.pallas import tpu as pltpu

F32 = jnp.float32
BF16 = jnp.bfloat16
MESH = pl.DeviceIdType.MESH

N_DEV = 8
N_LAYERS = 2
HEADS = 8
BLK = 128
CHUNK = 64
HALO = 16
RMS_EPS = 1e-6
LN_EPS = 1e-5
ADAM_LR, ADAM_B1, ADAM_B2, ADAM_EPS, ADAM_WD, ADAM_STEP = 0.001, 0.9, 0.999, 1e-8, 0.01, 10
GELU_C = math.sqrt(2.0 / math.pi)
GELU_A = 0.044715

VMEM_LIMIT_V7X = 56 * 1024 * 1024
_TM = 1024
_TN = 1024
_TT = 1024
_TM_MIX = 256
_TM_NORM = 512


def _cparams(n_axes):
    return pltpu.CompilerParams(dimension_semantics=("arbitrary",) * n_axes, vmem_limit_bytes=VMEM_LIMIT_V7X)


def _sds(shape, dtype):
    return jax.ShapeDtypeStruct(tuple(shape), dtype)


def _place():
    return lax.axis_index("x"), lax.axis_index("y"), lax.axis_index("c")


def _index(place):
    return 4 * place[0] + 2 * place[1] + place[2]


class _Piece:
    def __init__(self, operands, out_shapes, aliases, n_sems, start, finish, mid1=None, mid2=None, vmem=(),
                 hooks=(0.6, 0.87)):
        self.operands, self.out_shapes, self.aliases, self.n_sems = list(operands), list(out_shapes), dict(aliases), n_sems
        self.vmem = list(vmem)
        self.hooks = hooks
        nothing = lambda ctx: None
        self.start, self.mid1, self.mid2, self.finish = start, mid1 or nothing, mid2 or nothing, finish


class _Ctx:
    def __init__(self, ins, outs, sems, offs):
        self.ins, self.outs, self.sems = ins, outs, sems
        self.o_in, self.o_out, self.o_send, self.o_recv, self.o_loc, self.o_vmem = offs

    def vmem(self, i):
        return self.sems[3 + self.o_vmem + i]

    def inp(self, i):
        return self.ins[self.o_in + i]

    def out(self, i):
        return self.outs[self.o_out + i]

    def send(self, k):
        return self.sems[0].at[self.o_send + k]

    def recv(self, k):
        return self.sems[1].at[self.o_recv + k]

    def local(self, k):
        return self.sems[2].at[self.o_loc + k]


class _Hosted:
    def __init__(self, pieces, n_in_before, n_out_before):
        self.pieces = [p for p in (pieces or []) if p is not None]
        self.operands, self.out_shapes, self.aliases, self.offs = [], [], {}, []
        counts, vmem = [0, 0, 0], []
        for p in self.pieces:
            self.offs.append((len(self.operands), len(self.out_shapes), *counts, len(vmem)))
            for i, j in p.aliases.items():
                self.aliases[n_in_before + len(self.operands) + i] = n_out_before + len(self.out_shapes) + j
            self.operands += p.operands
            self.out_shapes += p.out_shapes
            counts = [c + n for c, n in zip(counts, p.n_sems)]
            vmem += p.vmem
        hbm = pl.BlockSpec(memory_space=pl.ANY)
        self.in_specs = [hbm] * len(self.operands)
        self.out_specs = [hbm] * len(self.out_shapes)
        self.scratch = ([pltpu.SemaphoreType.DMA((max(c, 1),)) for c in counts] + vmem) if self.pieces else []

    def run(self, stage, ins, outs, sems):
        for p, offs in zip(self.pieces, self.offs):
            getattr(p, stage)(_Ctx(ins, outs, sems, offs))

    def wrap(self, grid, compute, ins, outs, sems):
        if not self.pieces:
            compute()
            return
        n_steps = math.prod(grid)
        lin = 0
        for ax, g in enumerate(grid):
            lin = lin * g + pl.program_id(ax)
        pl.when(lin == 0)(lambda: self.run("start", ins, outs, sems))
        compute()
        for stage, which in (("mid1", 0), ("mid2", 1)):
            for p, offs in zip(self.pieces, self.offs):
                at = min(n_steps - 1, int(p.hooks[which] * n_steps))
                pl.when(lin == at)(functools.partial(getattr(p, stage), _Ctx(ins, outs, sems, offs)))
        pl.when(lin == n_steps - 1)(lambda: self.run("finish", ins, outs, sems))


def _cols_view(width):
    return lambda ref, p: ref.at[:, pl.ds(pl.multiple_of(p * width, 128), width)]


def _rows_view(height):
    return lambda ref, p: ref.at[pl.ds(pl.multiple_of(p * height, 16), height), :]


def _cols_halves(rows, width, part, n_parts):
    hr = rows // n_parts // 2
    at = lambda h: pl.ds(part * 2 * hr + h * hr, hr)
    return (lambda ref, p, h: ref.at[at(h), pl.ds(pl.multiple_of(p * width, 128), width)],
            lambda ref, h: ref.at[at(h), :], 2)


def _rows_halves(height, part, n_parts):
    hh = height // n_parts // 2
    return (lambda ref, p, h: ref.at[pl.ds(pl.multiple_of(p * height + part * 2 * hh + h * hh, 16), hh), :],
            lambda ref, h: ref.at[pl.ds(part * 2 * hh + h * hh, hh), :], 2)


_SLOT_WHOLE = (lambda ref, p, h: ref.at[p], lambda ref, h: ref, 1)


def _ag_piece(specs):
    units = [(a, h) for a, s in enumerate(specs) for h in s[3]]

    def plan(ctx):
        x, y, c = _place()
        me, sib, xn, yn, dg = (x, y, c), (x, y, 1 - c), (1 - x, y, c), (x, 1 - y, c), (1 - x, 1 - y, c)

        def copy(u, k, block, to, from_shard=False):
            a, h = units[u]
            dst_of, src_of, _ = specs[a][2]
            dst = dst_of(ctx.out(a), _index(block), h)
            return pltpu.make_async_remote_copy(
                src_ref=src_of(ctx.inp(a), h) if from_shard else dst, dst_ref=dst, send_sem=ctx.send(7 * u + k),
                recv_sem=ctx.recv(7 * u + k), device_id=to, device_id_type=MESH)

        def local(u):
            a, h = units[u]
            dst_of, src_of, _ = specs[a][2]
            return pltpu.make_async_copy(src_of(ctx.inp(a), h), dst_of(ctx.out(a), _index(me), h), ctx.local(u))

        def relay(u):
            return copy(u, 3, xn, yn) if units[u][1] % 2 == 0 else copy(u, 3, yn, xn)

        return me, sib, xn, yn, dg, c, copy, local, relay

    def start(ctx):
        me, sib, xn, yn, dg, c, copy, local, relay = plan(ctx)
        for u in range(len(units)):
            local(u).start()
            for k, to in enumerate((sib, xn, yn)):
                copy(u, k, me, to, from_shard=True).start()

    def mid1(ctx):
        me, sib, xn, yn, dg, c, copy, local, relay = plan(ctx)
        for u in range(len(units)):
            copy(u, 1, xn, me).wait_recv()
            copy(u, 2, yn, me).wait_recv()
            relay(u).start()
            copy(u, 4, xn, sib).start()
            copy(u, 5, yn, sib).start()

    def mid2(ctx):
        me, sib, xn, yn, dg, c, copy, local, relay = plan(ctx)
        for u in range(len(units)):
            copy(u, 3, dg, me).wait_recv()
            copy(u, 6, dg, sib).start()

    def finish(ctx):
        me, sib, xn, yn, dg, c, copy, local, relay = plan(ctx)
        other = lambda place: (place[0], place[1], 1 - c)
        for u in range(len(units)):
            for k, block in ((0, sib), (4, other(xn)), (5, other(yn)), (6, other(dg))):
                copy(u, k, block, me).wait_recv()
        for u in range(len(units)):
            for k, to in enumerate((sib, xn, yn)):
                copy(u, k, me, to, from_shard=True).wait_send()
            relay(u).wait_send()
            for k, block in ((4, xn), (5, yn), (6, dg)):
                copy(u, k, block, sib).wait_send()
            local(u).wait()

    n_u = len(units)
    operands, aliases = [s[0] for s in specs], {}
    for a, spec in enumerate(specs):
        if spec[4] is not None:
            aliases[len(operands)] = a
            operands.append(spec[4])
    return _Piece(operands, [s[1] for s in specs], aliases, (7 * n_u, 7 * n_u, n_u), start, finish, mid1, mid2)


N_CHIPS = 4


def _rs_core_piece(specs):
    n = len(specs)

    def copies(ctx):
        x, y, c = _place()
        out = []
        for a in range(n):
            for q in range(N_CHIPS):
                out.append(pltpu.make_async_remote_copy(
                    src_ref=specs[a][2](ctx.inp(a), 2 * q + (1 - c)), dst_ref=ctx.out(a).at[q],
                    send_sem=ctx.send(N_CHIPS * a + q), recv_sem=ctx.recv(N_CHIPS * a + q), device_id=(x, y, 1 - c),
                    device_id_type=MESH))
        return out

    def start(ctx):
        for cp in copies(ctx):
            cp.start()

    def finish(ctx):
        for cp in copies(ctx):
            cp.wait_recv()
            cp.wait_send()

    return _Piece([s[0] for s in specs], [s[1] for s in specs], {}, (N_CHIPS * n, N_CHIPS * n, 0), start, finish)


def _rs_chip_piece(specs, layer):
    n = len(specs)
    hops = [(1, 0), (0, 1), (1, 1)]

    def copies(ctx):
        x, y, c = _place()
        mine = 2 * x + y
        out = []
        for a in range(n):
            first, landing, size = specs[a][2]
            rows, to = pl.ds(first, size), pl.ds(landing, size)
            sums, land = ctx.inp(a), ctx.out(a)
            out.append((pltpu.make_async_copy(sums.at[mine, rows], land.at[layer, mine, to], ctx.local(a)), None))
            for j, (dx, dy) in enumerate(hops):
                px, py = x ^ dx, y ^ dy
                peer = 2 * px + py
                send = pltpu.make_async_remote_copy(
                    src_ref=sums.at[peer, rows], dst_ref=land.at[layer, mine, to], send_sem=ctx.send(3 * a + j),
                    recv_sem=ctx.recv(3 * a + j), device_id=(px, py, c), device_id_type=MESH)
                recv = pltpu.make_async_remote_copy(
                    src_ref=sums.at[peer, rows], dst_ref=land.at[layer, peer, to], send_sem=ctx.send(3 * a + j),
                    recv_sem=ctx.recv(3 * a + j), device_id=(px, py, c), device_id_type=MESH)
                out.append((send, recv))
        return out

    def start(ctx):
        for send, _ in copies(ctx):
            send.start()

    def finish(ctx):
        for send, recv in copies(ctx):
            if recv is None:
                send.wait()
            else:
                recv.wait_recv()
                send.wait_send()

    operands, aliases = [s[0] for s in specs], {}
    for a, spec in enumerate(specs):
        if spec[3] is not None:
            aliases[len(operands)] = a
            operands.append(spec[3])
    return _Piece(operands, [s[1] for s in specs], aliases, (3 * n, 3 * n, n), start, finish)


def _all_reduce_piece(pack):
    r = pack.shape[0]
    half = r // 2

    def plan(ctx):
        x, y, c = _place()
        acc, got = ctx.vmem(0), ctx.vmem(1)
        mine = pl.ds(pl.multiple_of(c * half, 8), half)
        sib = (x, y, 1 - c)
        copies = [
            pltpu.make_async_remote_copy(src_ref=acc.at[0], dst_ref=got.at[0], send_sem=ctx.send(0), recv_sem=ctx.recv(0),
                                         device_id=sib, device_id_type=MESH),
            pltpu.make_async_remote_copy(src_ref=acc.at[1, mine], dst_ref=got.at[1, mine], send_sem=ctx.send(1),
                                         recv_sem=ctx.recv(1), device_id=(1 - x, y, c), device_id_type=MESH),
            pltpu.make_async_remote_copy(src_ref=acc.at[2, mine], dst_ref=got.at[2, mine], send_sem=ctx.send(2),
                                         recv_sem=ctx.recv(2), device_id=(x, 1 - y, c), device_id_type=MESH),
            pltpu.make_async_remote_copy(src_ref=acc.at[3, mine], dst_ref=acc.at[3, mine], send_sem=ctx.send(3),
                                         recv_sem=ctx.recv(3), device_id=sib, device_id_type=MESH),
        ]
        other = pl.ds(pl.multiple_of((1 - c) * half, 8), half)
        arrival = pltpu.make_async_remote_copy(src_ref=acc.at[3, other], dst_ref=acc.at[3, other], send_sem=ctx.send(3),
                                               recv_sem=ctx.recv(3), device_id=sib, device_id_type=MESH)
        return acc, got, mine, copies, arrival

    def start(ctx):
        acc, got, mine, copies, arrival = plan(ctx)
        load = pltpu.make_async_copy(ctx.inp(0), acc.at[0], ctx.local(0))
        load.start()
        load.wait()
        copies[0].start()

    def mid1(ctx):
        acc, got, mine, copies, arrival = plan(ctx)
        copies[0].wait()
        acc[1] = acc[0] + got[0]
        copies[1].start()

    def mid2(ctx):
        acc, got, mine, copies, arrival = plan(ctx)
        copies[1].wait()
        acc[2, mine] = acc[1, mine] + got[1, mine]
        copies[2].start()

    def finish(ctx):
        acc, got, mine, copies, arrival = plan(ctx)
        copies[2].wait()
        acc[3, mine] = acc[2, mine] + got[2, mine]
        copies[3].start()
        copies[3].wait_send()
        arrival.wait_recv()
        store = pltpu.make_async_copy(acc.at[3], ctx.out(0), ctx.local(0))
        store.start()
        store.wait()

    return _Piece([pack], [_sds(pack.shape, F32)], {}, (4, 4, 1), start, finish, mid1, mid2,
                  vmem=[pltpu.VMEM((4, r, 128), F32), pltpu.VMEM((3, r, 128), F32)], hooks=(0.25, 0.6))


def _chip_sums(name, grad, stage, by_cols, core):
    _, r, c = stage.shape
    tr = r
    while tr * c > 1024 * 1024 or r % tr or tr % 16:
        tr -= 16
    n_t = r // tr

    def body(core_ref, g_ref, s_ref, o_ref):
        o_ref[...] = (g_ref[...].astype(F32) + s_ref[...].astype(F32)).astype(BF16)

    if by_cols:
        gspec = pl.BlockSpec((tr, c), lambda q, i, core_ref: (i, 2 * q + core_ref[0]))
    else:
        gspec = pl.BlockSpec((tr, c), lambda q, i, core_ref: ((2 * q + core_ref[0]) * n_t + i, 0))
    sspec = pl.BlockSpec((None, tr, c), lambda q, i, core_ref: (q, i, 0))
    return pl.pallas_call(
        body, name=name, out_shape=_sds(stage.shape, BF16),
        grid_spec=pltpu.PrefetchScalarGridSpec(num_scalar_prefetch=1, grid=(N_CHIPS, n_t), in_specs=[gspec, sspec],
                                               out_specs=sspec),
        compiler_params=_cparams(2))(core, grad, stage)


def _call_hosting(body, name, grid, out_shapes, in_specs, out_specs, operands, scratch, comm):
    n_in, n_out, n_scr = len(operands), len(out_shapes), len(scratch)
    hosted = _Hosted(comm, n_in, n_out)
    n_ci, n_co = len(hosted.operands), len(hosted.out_shapes)

    def hosting_body(*refs):
        ins, rest = refs[:n_in], refs[n_in:]
        c_ins, rest = rest[:n_ci], rest[n_ci:]
        outs, rest = rest[:n_out], rest[n_out:]
        c_outs, rest = rest[:n_co], rest[n_co:]
        hosted.wrap(grid, lambda: body(*ins, *outs, *rest[:n_scr]), c_ins, c_outs, rest[n_scr:])

    res = pl.pallas_call(
        hosting_body, name=name, grid=grid, out_shape=tuple(list(out_shapes) + hosted.out_shapes),
        in_specs=list(in_specs) + hosted.in_specs, out_specs=tuple(list(out_specs) + hosted.out_specs),
        input_output_aliases=hosted.aliases, scratch_shapes=list(scratch) + hosted.scratch,
        compiler_params=_cparams(len(grid)))(*operands, *hosted.operands)
    return list(res[:n_out]), list(res[n_out:])


def _matmul(name, grid, nk, kaxis, pairs, dims, extras, outs, epilogue, sum_pairs, acc_shape, comm=None, split=None):
    n_p, n_e, n_o = len(pairs), len(extras), len(outs)
    n_acc = 0 if nk == 1 else (1 if sum_pairs else n_p)
    n_in = 2 * n_p + n_e
    hosted = _Hosted(comm, n_in, n_o)
    n_ci, n_co = len(hosted.operands), len(hosted.out_shapes)

    def body(*refs):
        a_refs = refs[0:2 * n_p:2]
        b_refs = refs[1:2 * n_p:2]
        e_refs = refs[2 * n_p:n_in]
        c_ins = refs[n_in:n_in + n_ci]
        o_refs = refs[n_in + n_ci:n_in + n_ci + n_o]
        c_outs = refs[n_in + n_ci + n_o:n_in + n_ci + n_o + n_co]
        acc_refs = refs[n_in + n_ci + n_o + n_co:n_in + n_ci + n_o + n_co + n_acc]
        sems = refs[n_in + n_ci + n_o + n_co + n_acc:]

        def dots():
            if sum_pairs and n_p > 1 and dims == NN:
                a_all = jnp.concatenate([a[...] for a in a_refs], axis=1)
                b_all = jnp.concatenate([b[...] for b in b_refs], axis=0)
                return [lax.dot_general(a_all, b_all, (dims, ((), ())), preferred_element_type=F32)]
            prods = [lax.dot_general(a[...], b[...], (dims, ((), ())), preferred_element_type=F32)
                     for a, b in zip(a_refs, b_refs)]
            if sum_pairs and n_p > 1:
                prods = [functools.reduce(operator.add, prods)]
            return prods

        def compute():
            if nk == 1 and split is not None:
                n_split, b_axis, n_row = split
                width = b_refs[0].shape[b_axis] // n_split
                height = a_refs[0].shape[0] // n_row
                for s in range(n_split):
                    cols = pl.ds(s * width, width)
                    for r in range(n_row):
                        rows = pl.ds(r * height, height)
                        epilogue([lax.dot_general(a[rows, :], b[cols, :] if b_axis == 0 else b[:, cols], (dims, ((), ())),
                                                  preferred_element_type=F32) for a, b in zip(a_refs, b_refs)],
                                 e_refs, o_refs, rows, cols)
                return
            if nk == 1:
                epilogue(dots(), e_refs, o_refs)
                return
            k = pl.program_id(kaxis)

            @pl.when(k == 0)
            def _():
                for acc, p in zip(acc_refs, dots()):
                    acc[...] = p

            if nk > 2:
                @pl.when((k > 0) & (k < nk - 1))
                def _():
                    for acc, p in zip(acc_refs, dots()):
                        acc[...] += p

            @pl.when(k == nk - 1)
            def _():
                epilogue([acc[...] + p for acc, p in zip(acc_refs, dots())], e_refs, o_refs)

        hosted.wrap(grid, compute, c_ins, c_outs, sems)

    operands, in_specs = [], []
    for a, a_spec, b, b_spec in pairs:
        operands += [a, b]
        in_specs += [a_spec, b_spec]
    for e, e_spec in extras:
        operands.append(e)
        in_specs.append(e_spec)
    res = pl.pallas_call(
        body, name=name, grid=grid,
        out_shape=tuple([o for o, _ in outs] + hosted.out_shapes),
        in_specs=in_specs + hosted.in_specs, out_specs=tuple([s for _, s in outs] + hosted.out_specs),
        input_output_aliases=hosted.aliases,
        scratch_shapes=[pltpu.VMEM(acc_shape, F32) for _ in range(n_acc)] + hosted.scratch,
        compiler_params=_cparams(len(grid)),
    )(*operands, *hosted.operands)
    return list(res[:n_o]), list(res[n_o:])


NN = ((1,), (0,))
NT = ((1,), (1,))
TN = ((0,), (0,))


def _tile(n, want):
    if n <= want:
        return n
    t = want // 128 * 128
    while n % t:
        t -= 128
    return t


def _silu_parts(g):
    s = 0.5 + 0.5 * jnp.tanh(0.5 * g)
    return s, g * s


def _mm_in(h, w_in, comm=None):
    t, d = h.shape
    n = w_in.shape[1]
    tm, tn = _tile(t, _TM), _tile(n, _TN)

    def epi(accs, e, o):
        o[0][...] = accs[0].astype(BF16)

    outs, couts = _matmul(
        "mm_in", (n // tn, t // tm), 1, None,
        [(h, pl.BlockSpec((tm, d), lambda j, i: (i, 0)), w_in, pl.BlockSpec((d, tn), lambda j, i: (0, j)))],
        NN, [], [(_sds((t, n), BF16), pl.BlockSpec((tm, tn), lambda j, i: (i, j)))], epi, True, None, comm)
    return outs[0], couts


def _mm_out(y, w_out, x, comm=None):
    t, m = y.shape
    d = w_out.shape[1]
    tm, tn = _tile(t, _TM), _tile(d, _TN)

    def epi(accs, e, o):
        o[0][...] = e[0][...] + accs[0]

    outs, couts = _matmul(
        "mm_out", (t // tm, d // tn), 1, None,
        [(y, pl.BlockSpec((tm, m), lambda i, j: (i, 0)), w_out, pl.BlockSpec((m, tn), lambda i, j: (0, j)))],
        NN, [(x, pl.BlockSpec((tm, tn), lambda i, j: (i, j)))],
        [(_sds((t, d), F32), pl.BlockSpec((tm, tn), lambda i, j: (i, j)))], epi, True, None, comm)
    return outs[0], couts


def _mm_swiglu(h2, wgt, wut, comm=None):
    t, d = h2.shape
    f = wgt.shape[0]
    tm, tn = _tile(t, 2 * _TM), _tile(f, 512)

    def epi(accs, e, o, rows, cols):
        g, u = accs
        s, sg = _silu_parts(g)
        o[0][rows, cols] = (sg * u).astype(BF16)
        o[1][rows, cols] = (u * (s + sg * (1.0 - s))).astype(BF16)
        o[2][rows, cols] = sg.astype(BF16)

    wspec = pl.BlockSpec((tn, d), lambda i, j: (j, 0))
    hspec = pl.BlockSpec((tm, d), lambda i, j: (i, 0))
    ospec = pl.BlockSpec((tm, tn), lambda i, j: (i, j))
    osh = _sds((t, f), BF16)
    outs, couts = _matmul("mm_swiglu", (t // tm, f // tn), 1, None, [(h2, hspec, wgt, wspec), (h2, hspec, wut, wspec)],
                          NT, [], [(osh, ospec)] * 3, epi, False, None, comm, split=(tn // 256, 0, 2))
    return outs, couts


def _mm_down(act, wd, x1, comm=None):
    t, f = act.shape
    d = wd.shape[1]
    tm, tn = _tile(t, _TM), _tile(d, _TN)
    nk = 2
    tk = f // nk

    def epi(accs, e, o):
        o[0][...] = e[0][...] + accs[0]

    outs, couts = _matmul(
        "mm_down", (t // tm, d // tn, nk), nk, 2,
        [(act, pl.BlockSpec((tm, tk), lambda i, j, k: (i, k)), wd, pl.BlockSpec((tk, tn), lambda i, j, k: (k, j)))],
        NN, [(x1, pl.BlockSpec((tm, tn), lambda i, j, k: (i, j)))],
        [(_sds((t, d), F32), pl.BlockSpec((tm, tn), lambda i, j, k: (i, j)))], epi, True, (tm, tn), comm)
    return outs[0], couts


def _mm_dact(dxb, wd, dact_dgate, dact_dup, comm=None):
    t, d = dxb.shape
    f = wd.shape[0]
    tm, tn = _tile(t, 2 * _TM), _tile(f, 512)

    def epi(accs, e, o, rows, cols):
        da = accs[0]
        o[0][rows, cols] = (da * e[0][rows, cols].astype(F32)).astype(BF16)
        o[1][rows, cols] = (da * e[1][rows, cols].astype(F32)).astype(BF16)

    bspec = pl.BlockSpec((tm, tn), lambda i, j: (i, j))
    osh = _sds((t, f), BF16)
    outs, couts = _matmul(
        "mm_dact", (t // tm, f // tn), 1, None,
        [(dxb, pl.BlockSpec((tm, d), lambda i, j: (i, 0)), wd, pl.BlockSpec((tn, d), lambda i, j: (j, 0)))],
        NT, [(dact_dgate, bspec), (dact_dup, bspec)], [(osh, bspec)] * 2, epi, True, None, comm, split=(tn // 256, 0, 2))
    return outs, couts


def _mm_dh2(dgate, dup, wgt, wut, comm=None):
    t, f = dgate.shape
    d = wgt.shape[1]
    tm, tn = _tile(t, _TM), _tile(d, _TN)
    nk = 4
    tk = f // nk

    def epi(accs, e, o):
        o[0][...] = accs[0].astype(BF16)

    aspec = pl.BlockSpec((tm, tk), lambda i, j, k: (i, k))
    wspec = pl.BlockSpec((tk, tn), lambda i, j, k: (k, j))
    outs, couts = _matmul("mm_dh2", (t // tm, d // tn, nk), nk, 2, [(dgate, aspec, wgt, wspec), (dup, aspec, wut, wspec)],
                          NN, [], [(_sds((t, d), BF16), pl.BlockSpec((tm, tn), lambda i, j, k: (i, j)))], epi, True,
                          (tm, tn), comm)
    return outs[0], couts


def _mm_dw(name, a_list, b, tmo, tno, comm=None, m_rows=None):
    t, m = a_list[0].shape
    start, m = (0, m) if m_rows is None else m_rows
    n = b.shape[1]
    tt = _tile(t, _TT)
    nk = t // tt
    tmo, tno = _tile(m, tmo), _tile(n, tno)
    first = start // tmo

    def epi(accs, e, o):
        for acc, out in zip(accs, o):
            out[...] = acc.astype(BF16)

    aspec = pl.BlockSpec((tt, tmo), lambda i, j, k: (k, first + i))
    bspec = pl.BlockSpec((tt, tno), lambda i, j, k: (k, j))
    ospec = pl.BlockSpec((tmo, tno), lambda i, j, k: (i, j))
    if nk == 1:
        return _matmul(name, (m // tmo, n // tno, 1), 1, None, [(a, aspec, b, bspec) for a in a_list], TN, [],
                       [(_sds((m, n), BF16), ospec)] * len(a_list), epi, False, None, comm)
    return _matmul(name, (m // tmo, n // tno, nk), nk, 2, [(a, aspec, b, bspec) for a in a_list], TN, [],
                   [(_sds((m, n), BF16), ospec)] * len(a_list), epi, False, (tmo, tno), comm)


def _mm_dy(dxb, w_out, comm=None):
    t, d = dxb.shape
    m = w_out.shape[0]
    tm, tn = _tile(t, _TM), _tile(m, _TN)

    def epi(accs, e, o):
        o[0][...] = accs[0].astype(BF16)

    outs, couts = _matmul(
        "mm_dy", (t // tm, m // tn), 1, None,
        [(dxb, pl.BlockSpec((tm, d), lambda i, j: (i, 0)), w_out, pl.BlockSpec((tn, d), lambda i, j: (j, 0)))], NT, [],
        [(_sds((t, m), BF16), pl.BlockSpec((tm, tn), lambda i, j: (i, j)))], epi, True, None, comm)
    return outs[0], couts


def _mm_dh(dz, w_in, comm=None):
    t, n = dz.shape
    d = w_in.shape[0]
    tm, tn = _tile(t, _TM), _tile(d, _TN)
    nk = 2
    tk = n // nk

    def epi(accs, e, o):
        o[0][...] = accs[0].astype(BF16)

    outs, couts = _matmul(
        "mm_dh", (t // tm, d // tn, nk), nk, 2,
        [(dz, pl.BlockSpec((tm, tk), lambda i, j, k: (i, k)), w_in, pl.BlockSpec((tn, tk), lambda i, j, k: (j, k)))], NT,
        [], [(_sds((t, d), BF16), pl.BlockSpec((tm, tn), lambda i, j, k: (i, j)))], epi, True, (tm, tn), comm)
    return outs[0], couts


def _rmsnorm_fwd(x, g, comm=None):
    t, d = x.shape
    tm = min(_TM_NORM, t)

    def body(x_ref, g_ref, o_ref):
        xv = x_ref[...]
        rs = lax.rsqrt(jnp.mean(xv * xv, axis=-1, keepdims=True) + RMS_EPS)
        o_ref[...] = (xv * rs * g_ref[...]).astype(BF16)

    outs, couts = _call_hosting(
        body, "rmsnorm_fwd", (t // tm,), [_sds((t, d), BF16)],
        [pl.BlockSpec((tm, d), lambda i: (i, 0)), pl.BlockSpec((1, d), lambda i: (0, 0))],
        [pl.BlockSpec((tm, d), lambda i: (i, 0))], [x, g], [], comm)
    return outs[0], couts


def _rmsnorm_bwd_math(xv, g, dh):
    rs = lax.rsqrt(jnp.mean(xv * xv, axis=-1, keepdims=True) + RMS_EPS)
    xh = xv * rs
    gd = dh * g
    dx = rs * (gd - xh * jnp.mean(gd * xh, axis=-1, keepdims=True))
    return dx, jnp.sum(dh * xh, axis=0, keepdims=True)


def _rmsnorm_bwd(x, g, dh, dres, with_bf16=True):
    t, d = x.shape
    tm = min(_TM_NORM, t)

    def body(x_ref, g_ref, dh_ref, dres_ref, dx_ref, *rest):
        dg_ref = rest[-1]
        dx, dg = _rmsnorm_bwd_math(x_ref[...], g_ref[...], dh_ref[...].astype(F32))
        dx = dx + dres_ref[...]
        dx_ref[...] = dx
        if with_bf16:
            rest[0][...] = dx.astype(BF16)

        @pl.when(pl.program_id(0) == 0)
        def _():
            dg_ref[...] = dg

        @pl.when(pl.program_id(0) > 0)
        def _():
            dg_ref[...] += dg

    row = pl.BlockSpec((tm, d), lambda i: (i, 0))
    vec = pl.BlockSpec((1, d), lambda i: (0, 0))
    halves = [(_sds((t, d), BF16), row)] if with_bf16 else []
    outs = [(_sds((t, d), F32), row), *halves, (_sds((1, d), F32), vec)]
    res = pl.pallas_call(
        body, name="rmsnorm_bwd", grid=(t // tm,), out_shape=tuple(o for o, _ in outs),
        in_specs=[row, vec, row, row], out_specs=tuple(s for _, s in outs), compiler_params=_cparams(1))(x, g, dh, dres)
    return (res[0], res[1], res[2]) if with_bf16 else (res[0], None, res[1])


def _loss_head(x, g, target):
    t, d = x.shape
    tm = min(_TM_NORM, t)

    def body(x_ref, g_ref, t_ref, dx_ref, dxb_ref, dg_ref, loss_ref):
        xv, gv = x_ref[...], g_ref[...]
        rs = lax.rsqrt(jnp.mean(xv * xv, axis=-1, keepdims=True) + RMS_EPS)
        diff = xv * rs * gv - t_ref[...]
        part = 0.5 * jnp.sum(jnp.mean(diff * diff, axis=-1, keepdims=True), axis=0, keepdims=True)
        part = jnp.broadcast_to(part, (1, 128))
        dx, dg = _rmsnorm_bwd_math(xv, gv, diff * (1.0 / d))
        dx_ref[...] = dx
        dxb_ref[...] = dx.astype(BF16)

        @pl.when(pl.program_id(0) == 0)
        def _():
            dg_ref[...] = dg
            loss_ref[...] = part

        @pl.when(pl.program_id(0) > 0)
        def _():
            dg_ref[...] += dg
            loss_ref[...] += part

    row = pl.BlockSpec((tm, d), lambda i: (i, 0))
    vec = pl.BlockSpec((1, d), lambda i: (0, 0))
    return pl.pallas_call(
        body, name="loss_head", grid=(t // tm,),
        out_shape=(_sds((t, d), F32), _sds((t, d), BF16), _sds((1, d), F32), _sds((1, 128), F32)),
        in_specs=[row, vec, row], out_specs=(row, row, vec, pl.BlockSpec((1, 128), lambda i: (0, 0))),
        compiler_params=_cparams(1))(x, g, target)


def _gelu(x):
    th = jnp.tanh(GELU_C * (x + GELU_A * x * x * x))
    return 0.5 * x * (1.0 + th), th


def _gelu_grad(x, th):
    return 0.5 * (1.0 + th) + 0.5 * x * (1.0 - th * th) * GELU_C * (1.0 + 3.0 * GELU_A * x * x)


def _masked_ws(ws_ref, h):
    i = lax.broadcasted_iota(jnp.int32, (BLK, BLK), 0) // CHUNK
    j = lax.broadcasted_iota(jnp.int32, (BLK, BLK), 1) // CHUNK
    return jnp.where(j <= i, ws_ref[h], 0.0)


def _shift_down(q, n, first_rows):
    rolled = pltpu.roll(q, n, 0)
    row = lax.broadcasted_iota(jnp.int32, q.shape, 0)
    for r, val in enumerate(first_rows):
        rolled = jnp.where(row == r, val, rolled)
    return rolled


def _shift_up(q, n, last_rows):
    tm = q.shape[0]
    rolled = pltpu.roll(q, tm - n, 0)
    row = lax.broadcasted_iota(jnp.int32, q.shape, 0)
    for r, val in enumerate(last_rows):
        rolled = jnp.where(row == tm - n + r, val, rolled)
    return rolled


def _mixer_specs(t, a, tm):
    hb = tm // HALO
    last = t // HALO - 1
    tile = pl.BlockSpec((tm, 5 * a), lambda i: (i, 0))
    prev = [pl.BlockSpec((HALO, a), functools.partial(lambda i, col: (jnp.maximum(i * hb - 1, 0), col), col=col))
            for col in (3, 4)]
    nxt = [pl.BlockSpec((HALO, a), functools.partial(lambda i, col: (jnp.minimum((i + 1) * hb, last), col), col=col))
           for col in (2, 3, 4)]
    return tile, prev, nxt


def _group_a_fwd(zu, zv, lng, lnb, ws_ref, bb_ref, mixed_ref, vln_ref):
    u, thu = _gelu(zu)
    v, thv = _gelu(zv)
    mu = jnp.mean(v, axis=-1, keepdims=True)
    vc = v - mu
    rs = lax.rsqrt(jnp.mean(vc * vc, axis=-1, keepdims=True) + LN_EPS)
    vhat = vc * rs
    vln_ref[...] = vhat * lng + lnb
    tm, a = zu.shape
    hd = a // HEADS
    for h in range(HEADS):
        w = _masked_ws(ws_ref, h).astype(BF16)
        for b in range(tm // BLK):
            rows, cols = pl.ds(b * BLK, BLK), pl.ds(h * hd, hd)
            mixed_ref[rows, cols] = jnp.dot(w, vln_ref[rows, cols].astype(BF16), preferred_element_type=F32) + bb_ref[h]
    return u, thu, thv, rs, vhat


def _mixer_fwd(z, ln_g, ln_b, w_spatial, bb, conv_w, gg, comm=None):
    t = z.shape[0]
    a = z.shape[1] // 5
    tm = min(_TM_MIX, t)
    tile, prev, _ = _mixer_specs(t, a, tm)

    def body(z_ref, pc_ref, ph_ref, lng_ref, lnb_ref, ws_ref, bb_ref, cw_ref, gg_ref, y_ref, mixed_ref, vln_ref):
        i = pl.program_id(0)
        zu = z_ref[:, 0:a].astype(F32)
        zv = z_ref[:, a:2 * a].astype(F32)
        u, _, _, _, _ = _group_a_fwd(zu, zv, lng_ref[...], lnb_ref[...], ws_ref, bb_ref, mixed_ref, vln_ref)
        ya = u * mixed_ref[...]
        ra = lax.rsqrt(jnp.mean(ya * ya, axis=-1, keepdims=True) + RMS_EPS)
        y_ref[:, 0:a] = (ya * ra * gg_ref[:, 0:a]).astype(BF16)

        zb = z_ref[:, 2 * a:3 * a].astype(F32)
        q = z_ref[:, 3 * a:4 * a].astype(F32) * z_ref[:, 4 * a:5 * a].astype(F32)
        qp = jnp.where(i > 0, pc_ref[...].astype(F32) * ph_ref[...].astype(F32), 0.0)
        qm1 = _shift_down(q, 1, [qp[HALO - 1:HALO]])
        qm2 = _shift_down(q, 2, [qp[HALO - 2:HALO - 1], qp[HALO - 1:HALO]])
        cv = cw_ref[0:1, :] * qm2 + cw_ref[1:2, :] * qm1 + cw_ref[2:3, :] * q
        yb = zb * cv
        rb = lax.rsqrt(jnp.mean(yb * yb, axis=-1, keepdims=True) + RMS_EPS)
        y_ref[:, a:2 * a] = (yb * rb * gg_ref[:, a:2 * a]).astype(BF16)

    full = lambda shape: pl.BlockSpec(shape, lambda i: (0,) * len(shape))
    outs, couts = _call_hosting(
        body, "mixer_fwd", (t // tm,), [_sds((t, 2 * a), BF16)],
        [tile, *prev, full((1, a)), full((1, a)), full(w_spatial.shape), full(bb.shape), full(conv_w.shape),
         full((1, 2 * a))],
        [pl.BlockSpec((tm, 2 * a), lambda i: (i, 0))], [z, z, z, ln_g, ln_b, w_spatial, bb, conv_w, gg],
        [pltpu.VMEM((tm, a), F32), pltpu.VMEM((tm, a), F32)], comm)
    return outs[0], couts


def _mixer_bwd(z, dy, ln_g, ln_b, w_spatial, bb, conv_w, gg, comm=None):
    t = z.shape[0]
    a = z.shape[1] // 5
    hd = a // HEADS
    tm = min(_TM_MIX, t)
    n_tiles = t // tm
    tile, prev, nxt = _mixer_specs(t, a, tm)
    hb = tm // HALO
    dy_tile = pl.BlockSpec((tm, 2 * a), lambda i: (i, 0))
    dy_next = pl.BlockSpec((HALO, a), lambda i: (jnp.minimum((i + 1) * hb, t // HALO - 1), 1))

    def body(z_ref, pc_ref, ph_ref, nb_ref, nc_ref, nh_ref, dy_ref, ndy_ref, lng_ref, lnb_ref, ws_ref, bb_ref, cw_ref,
             gg_ref, dz_ref, dlng_ref, dlnb_ref, dws_ref, dbb_ref, dcw_ref, dgg_ref, mixed_ref, vln_ref, dmix_ref,
             dvln_ref):
        i = pl.program_id(0)

        @pl.when(i == 0)
        def _():
            for ref in (dlng_ref, dlnb_ref, dws_ref, dbb_ref, dcw_ref, dgg_ref):
                ref[...] = jnp.zeros(ref.shape, F32)

        lng = lng_ref[...]
        zu = z_ref[:, 0:a].astype(F32)
        zv = z_ref[:, a:2 * a].astype(F32)
        u, thu, thv, rs, vhat = _group_a_fwd(zu, zv, lng, lnb_ref[...], ws_ref, bb_ref, mixed_ref, vln_ref)
        mixed = mixed_ref[...]
        ya = u * mixed
        ra = lax.rsqrt(jnp.mean(ya * ya, axis=-1, keepdims=True) + RMS_EPS)
        da = dy_ref[:, 0:a].astype(F32)
        yah = ya * ra
        dgg_ref[:, 0:a] += jnp.sum(da * yah, axis=0, keepdims=True)
        ga = da * gg_ref[:, 0:a]
        dya = ra * (ga - yah * jnp.mean(ga * yah, axis=-1, keepdims=True))
        dz_ref[:, 0:a] = (dya * mixed * _gelu_grad(zu, thu)).astype(BF16)
        dmix_ref[...] = dya * u
        for h in range(HEADS):
            w = _masked_ws(ws_ref, h).astype(BF16)
            dw = jnp.zeros((BLK, BLK), F32)
            db = jnp.zeros((BLK, hd), F32)
            for b in range(tm // BLK):
                rows, cols = pl.ds(b * BLK, BLK), pl.ds(h * hd, hd)
                dm = dmix_ref[rows, cols]
                dmb = dm.astype(BF16)
                db = db + dm
                dw = dw + lax.dot_general(dmb, vln_ref[rows, cols].astype(BF16), (NT, ((), ())),
                                          preferred_element_type=F32)
                dvln_ref[rows, cols] = lax.dot_general(w, dmb, (TN, ((), ())), preferred_element_type=F32)
            dws_ref[h] += dw
            dbb_ref[h] += db
        dvln = dvln_ref[...]
        dlng_ref[...] += jnp.sum(dvln * vhat, axis=0, keepdims=True)
        dlnb_ref[...] += jnp.sum(dvln, axis=0, keepdims=True)
        dvh = dvln * lng
        dv = rs * (dvh - jnp.mean(dvh, axis=-1, keepdims=True) - vhat * jnp.mean(dvh * vhat, axis=-1, keepdims=True))
        dz_ref[:, a:2 * a] = (dv * _gelu_grad(zv, thv)).astype(BF16)

        w0, w1, w2 = cw_ref[0:1, :], cw_ref[1:2, :], cw_ref[2:3, :]
        ggb = gg_ref[:, a:2 * a]
        zb = z_ref[:, 2 * a:3 * a].astype(F32)
        zc = z_ref[:, 3 * a:4 * a].astype(F32)
        zh = z_ref[:, 4 * a:5 * a].astype(F32)
        q = zc * zh
        qp = jnp.where(i > 0, pc_ref[...].astype(F32) * ph_ref[...].astype(F32), 0.0)
        qm1 = _shift_down(q, 1, [qp[HALO - 1:HALO]])
        qm2 = _shift_down(q, 2, [qp[HALO - 2:HALO - 1], qp[HALO - 1:HALO]])
        cv = w0 * qm2 + w1 * qm1 + w2 * q

        def conv_out_grad(zb_, cv_, dout_):
            yb = zb_ * cv_
            rb = lax.rsqrt(jnp.mean(yb * yb, axis=-1, keepdims=True) + RMS_EPS)
            ybh = yb * rb
            gb = dout_ * ggb
            dyb = rb * (gb - ybh * jnp.mean(gb * ybh, axis=-1, keepdims=True))
            return dyb * zb_, dyb * cv_, ybh

        db_out = dy_ref[:, a:2 * a].astype(F32)
        g, dzb, ybh = conv_out_grad(zb, cv, db_out)
        dgg_ref[:, a:2 * a] += jnp.sum(db_out * ybh, axis=0, keepdims=True)
        dz_ref[:, 2 * a:3 * a] = dzb.astype(BF16)
        qn = nc_ref[...].astype(F32) * nh_ref[...].astype(F32)
        zbn = nb_ref[...].astype(F32)
        cvn = w0 * _shift_down(qn, 2, [q[tm - 2:tm - 1], q[tm - 1:tm]]) + w1 * _shift_down(qn, 1, [q[tm - 1:tm]]) + w2 * qn
        gn, _, _ = conv_out_grad(zbn, cvn, ndy_ref[...].astype(F32))
        gn = jnp.where(i < n_tiles - 1, gn, 0.0)
        dq = w2 * g + w1 * _shift_up(g, 1, [gn[0:1]]) + w0 * _shift_up(g, 2, [gn[0:1], gn[1:2]])
        dz_ref[:, 3 * a:4 * a] = (dq * zh).astype(BF16)
        dz_ref[:, 4 * a:5 * a] = (dq * zc).astype(BF16)
        dcw_ref[0:1, :] += jnp.sum(g * qm2, axis=0, keepdims=True)
        dcw_ref[1:2, :] += jnp.sum(g * qm1, axis=0, keepdims=True)
        dcw_ref[2:3, :] += jnp.sum(g * q, axis=0, keepdims=True)

        @pl.when(i == n_tiles - 1)
        def _():
            for h in range(HEADS):
                dbb_ref[h] = jnp.broadcast_to(jnp.sum(dbb_ref[h], axis=1, keepdims=True), (BLK, hd))
                dws_ref[h] = _masked_ws(dws_ref, h)

    full = lambda shape: pl.BlockSpec(tuple(shape), lambda i: (0,) * len(shape))
    out_shapes = (_sds((t, 5 * a), BF16), _sds((1, a), F32), _sds((1, a), F32), _sds(w_spatial.shape, F32),
                  _sds(bb.shape, F32), _sds((8, a), F32), _sds((1, 2 * a), F32))
    return _call_hosting(
        body, "mixer_bwd", (n_tiles,), out_shapes,
        [tile, *prev, *nxt, dy_tile, dy_next, full((1, a)), full((1, a)), full(w_spatial.shape), full(bb.shape),
         full(conv_w.shape), full((1, 2 * a))],
        [tile, *[full(s.shape) for s in out_shapes[1:]]], [z, z, z, z, z, z, dy, dy, ln_g, ln_b, w_spatial, bb, conv_w, gg],
        [pltpu.VMEM((tm, a), F32)] * 4, comm)


def _all_reduce_small(pack, comm=None):
    r = pack.shape[0]
    hosted = _Hosted(comm, 1, 1)
    n_ci, n_co = len(hosted.operands), len(hosted.out_shapes)

    def body(*refs):
        in_ref, c_ins, out_ref, c_outs = refs[0], refs[1:1 + n_ci], refs[1 + n_ci], refs[2 + n_ci:2 + n_ci + n_co]
        acc_ref, recv_ref, send_sems, recv_sems = refs[2 + n_ci + n_co:6 + n_ci + n_co]
        sems = refs[6 + n_ci + n_co:]
        hosted.run("start", c_ins, c_outs, sems)
        x, y, c = _place()
        partners = [(x, y, 1 - c), (1 - x, y, c), (x, 1 - y, c)]
        acc_ref[0] = in_ref[...]
        for s, partner in enumerate(partners):
            cp = pltpu.make_async_remote_copy(
                src_ref=acc_ref.at[s], dst_ref=recv_ref.at[s], send_sem=send_sems.at[s], recv_sem=recv_sems.at[s],
                device_id=partner, device_id_type=MESH)
            cp.start()
            cp.wait()
            if s < 2:
                acc_ref[s + 1] = acc_ref[s] + recv_ref[s]
            else:
                out_ref[...] = acc_ref[s] + recv_ref[s]
        for stage in ("mid1", "mid2", "finish"):
            hosted.run(stage, c_ins, c_outs, sems)

    vmem = pl.BlockSpec(memory_space=pltpu.VMEM)
    res = pl.pallas_call(
        body, name="all_reduce_small", out_shape=tuple([_sds(pack.shape, F32)] + hosted.out_shapes),
        in_specs=[vmem] + hosted.in_specs, out_specs=tuple([vmem] + hosted.out_specs),
        input_output_aliases=hosted.aliases,
        scratch_shapes=[pltpu.VMEM((3, r, 128), F32), pltpu.VMEM((3, r, 128), F32), pltpu.SemaphoreType.DMA((3,)),
                        pltpu.SemaphoreType.DMA((3,))] + hosted.scratch,
        compiler_params=pltpu.CompilerParams(vmem_limit_bytes=VMEM_LIMIT_V7X),
    )(pack, *hosted.operands)
    return res[0], list(res[1:])


def _adamw_math(w, g, m, v):
    m = ADAM_B1 * m + (1.0 - ADAM_B1) * g
    v = ADAM_B2 * v + (1.0 - ADAM_B2) * (g * g)
    m_hat = m / (1.0 - ADAM_B1 ** ADAM_STEP)
    v_hat = v / (1.0 - ADAM_B2 ** ADAM_STEP)
    delta = -ADAM_LR * (m_hat / (jnp.sqrt(v_hat) + ADAM_EPS) + ADAM_WD * w)
    return delta, m, v


def _adamw_big(name, land, w, m, v, comm=None):
    nl, n_slots, r, c = land.shape
    tr = max(8, min(r, (256 * 640) // c // 8 * 8))
    while r % tr:
        tr -= 8
    grid = (nl, r // tr)
    hosted = _Hosted(comm, 4, 4)
    n_ci, n_co = len(hosted.operands), len(hosted.out_shapes)

    def body(*refs):
        land_ref, w_ref, m_ref, v_ref = refs[:4]
        c_ins = refs[4:4 + n_ci]
        g_out, d_out, m_out, v_out = refs[4 + n_ci:8 + n_ci]
        c_outs = refs[8 + n_ci:8 + n_ci + n_co]
        sems = refs[8 + n_ci + n_co:]

        def compute():
            g = land_ref[0].astype(F32)
            for s in range(1, n_slots):
                g = g + land_ref[s].astype(F32)
            delta, mn, vn = _adamw_math(w_ref[...], g, m_ref[...], v_ref[...])
            g_out[...] = g
            d_out[...] = delta
            m_out[...] = mn
            v_out[...] = vn

        hosted.wrap(grid, compute, c_ins, c_outs, sems)

    blk = pl.BlockSpec((None, tr, c), lambda l, i: (l, i, 0))
    res = pl.pallas_call(
        body, name=name, grid=grid, out_shape=tuple([_sds((nl, r, c), F32)] * 4 + hosted.out_shapes),
        in_specs=[pl.BlockSpec((None, n_slots, tr, c), lambda l, i: (l, 0, i, 0)), blk, blk, blk] + hosted.in_specs,
        out_specs=tuple([blk] * 4 + hosted.out_specs), input_output_aliases=hosted.aliases,
        scratch_shapes=hosted.scratch, compiler_params=_cparams(2))(land, w, m, v, *hosted.operands)
    return list(res[:4]), list(res[4:])


def _adamw_small(gs, ws, ms, vs):
    n = len(gs)

    def body(*refs):
        g_refs, w_refs, m_refs, v_refs = refs[:n], refs[n:2 * n], refs[2 * n:3 * n], refs[3 * n:4 * n]
        d_outs, m_outs, v_outs = refs[4 * n:5 * n], refs[5 * n:6 * n], refs[6 * n:7 * n]
        for i in range(n):
            delta, mn, vn = _adamw_math(w_refs[i][...], g_refs[i][...], m_refs[i][...], v_refs[i][...])
            d_outs[i][...] = delta
            m_outs[i][...] = mn
            v_outs[i][...] = vn

    shapes = [_sds(g.shape, F32) for g in gs]
    res = pl.pallas_call(body, name="adamw_small", out_shape=tuple(shapes * 3),
                         compiler_params=pltpu.CompilerParams(vmem_limit_bytes=VMEM_LIMIT_V7X))(*gs, *ws, *ms, *vs)
    return list(res[:n]), list(res[n:2 * n]), list(res[2 * n:])


def _rows(a):
    return a.reshape(-1, 128)


BIG = ["w_in", "w_out", "w_gate", "w_up", "w_down"]
AG_HOSTS = {
    ("norm1", 0): [("w_in", 0), ("conv_w", 0), ("w_out", 0)],
    ("mm_in", 0): [("w_gate", 0)], ("mixer", 0): [("w_up", 0, 0, 2)], ("mm_out", 0): [("w_up", 0, 1, 2)],
    ("mm_swiglu", 0): [("w_down", 0), ("w_in", 1), ("w_out", 1)], ("mm_down", 0): [("w_gate", 1)],
    ("mm_in", 1): [("w_up", 1)], ("mm_swiglu", 1): [("w_down", 1)],
}


def kernel(x, norm1_g, w_in, gmlp_ln_g, gmlp_ln_b, w_spatial, b_spatial, conv_w, group_norm_g, w_out, norm2_g, w_gate, w_up, w_down, final_norm_g, loss_target, m_norm1_g, m_w_in, m_gmlp_ln_g, m_gmlp_ln_b, m_w_spatial, m_b_spatial, m_conv_w, m_group_norm_g, m_w_out, m_norm2_g, m_w_gate, m_w_up, m_w_down, m_final_norm_g, v_norm1_g, v_w_in, v_gmlp_ln_g, v_gmlp_ln_b, v_w_spatial, v_b_spatial, v_conv_w, v_group_norm_g, v_w_out, v_norm2_g, v_w_gate, v_w_up, v_w_down, v_final_norm_g):
    nl = N_LAYERS
    t, d = x.shape[1], x.shape[2]
    a = d // 2
    hd = a // HEADS
    xin = x.reshape(t, d)
    target = loss_target.reshape(t, d)
    me = _index(_place())

    tr = lambda w: jnp.transpose(w, (0, 2, 1))
    big = {"w_in": w_in, "w_out": w_out, "w_gate": tr(w_gate), "w_up": tr(w_up), "w_down": w_down}
    big_m = {"w_in": m_w_in, "w_out": m_w_out, "w_gate": tr(m_w_gate), "w_up": tr(m_w_up), "w_down": m_w_down}
    big_v = {"w_in": v_w_in, "w_out": v_w_out, "w_gate": tr(v_w_gate), "w_up": tr(v_w_up), "w_down": v_w_down}
    block = {k: big[k].shape[1:] for k in BIG}
    view = {k: _cols_view(block[k][1]) if k == "w_in" else _rows_view(block[k][0]) for k in BIG}
    full_shape = {k: (block[k][0], N_DEV * block[k][1]) if k == "w_in" else (N_DEV * block[k][0], block[k][1])
                  for k in BIG}

    weights = {}
    shards = {(k, l): big[k][l].astype(BF16) for k in BIG for l in range(nl)}

    def ag_spec(k, l, part=0, n_parts=1):
        if k == "conv_w":
            return (conv_w, _sds((N_DEV, *conv_w.shape), F32), _SLOT_WHOLE, (0,), None)
        halves = (_cols_halves(*block[k], part, n_parts) if k == "w_in" else _rows_halves(block[k][0], part, n_parts))
        return (shards[(k, l)], _sds(full_shape[k], BF16), halves, (0, 1), weights.get((k, l)))

    bb = jnp.broadcast_to(b_spatial[..., None], (nl, HEADS, BLK, hd))

    def hosted(name, l):
        keys = AG_HOSTS.get((name, l), [])
        return keys, ([_ag_piece([ag_spec(*key) for key in keys])] if keys else None)

    def landed(keys, couts):
        for key, arr in zip(keys, couts):
            weights[key[:2]] = arr

    saved = []
    xl = xin
    for l in range(nl):
        keys, comm = hosted("norm1", l)
        h, couts = _rmsnorm_fwd(xl, norm1_g[l:l + 1], comm)
        landed(keys, couts)
        if l == 0:
            conv_full = jnp.transpose(weights[("conv_w", 0)], (1, 2, 0, 3)).reshape(nl, 3, a)
        keys, comm = hosted("mm_in", l)
        z, couts = _mm_in(h, weights[("w_in", l)], comm)
        landed(keys, couts)
        keys, comm = hosted("mixer", l)
        y, couts = _mixer_fwd(z, gmlp_ln_g[l:l + 1], gmlp_ln_b[l:l + 1], w_spatial[l], bb[l], conv_full[l],
                              group_norm_g[l:l + 1], comm)
        landed(keys, couts)
        keys, comm = hosted("mm_out", l)
        x1, couts = _mm_out(y, weights[("w_out", l)], xl, comm)
        landed(keys, couts)
        keys, comm = hosted("norm2", l)
        h2, couts = _rmsnorm_fwd(x1, norm2_g[l:l + 1], comm)
        landed(keys, couts)
        keys, comm = hosted("mm_swiglu", l)
        (act, dact_dgate, dact_dup), couts = _mm_swiglu(h2, weights[("w_gate", l)], weights[("w_up", l)], comm)
        landed(keys, couts)
        keys, comm = hosted("mm_down", l)
        x2, couts = _mm_down(act, weights[("w_down", l)], x1, comm)
        landed(keys, couts)
        saved.append(dict(x=xl, h=h, z=z, y=y, x1=x1, h2=h2, dact_dgate=dact_dgate, dact_dup=dact_dup, act=act))
        xl = x2

    dx, dxb, d_final_g, loss_part = _loss_head(xl, final_norm_g.reshape(1, d), target)
    small = [None] * nl
    core = lax.axis_index("c").astype(jnp.int32).reshape(1)
    in_rows = block["w_in"][0]
    part_of = {"w_in_a": ("w_in", 0), "w_in_b": ("w_in", 3 * in_rows // 4)}
    block["w_in_a"], block["w_in_b"] = (3 * in_rows // 4, block["w_in"][1]), (in_rows // 4, block["w_in"][1])
    for k in part_of:
        view[k] = view["w_in"]
    stage_shape = {k: _sds((N_CHIPS, *block[k]), BF16) for k in block}
    land_shape = {k: _sds((nl, N_CHIPS, *block[k]), BF16) for k in BIG}
    grads = [dict() for _ in range(nl)]
    stages = [dict() for _ in range(nl)]
    sums = [dict() for _ in range(nl)]
    lands = {k: None for k in BIG}

    def core_job(l, keys):
        def sink(outs):
            stages[l].update(zip(keys, outs))
        return _rs_core_piece([(grads[l][k], stage_shape[k], view[k]) for k in keys]), sink

    def chip_job(l, items):
        keys = [part_of.get(item[0], (item[0], 0))[0] for item in items]

        def rows(k, p0, p1, n_parts):
            per = block[k][0] // n_parts
            landing = part_of.get(k, (k, 0))[1]
            return (p0 * per, landing + p0 * per, (p1 - p0) * per)

        def sink(outs):
            lands.update(zip(keys, outs))
        return _rs_chip_piece([(sums[l][k], land_shape[key], rows(k, p0, p1, n_parts), lands[key])
                               for key, (k, p0, p1, n_parts) in zip(keys, items)], l), sink

    def add_up(l, keys):
        for k in keys:
            sums[l][k] = _chip_sums(f"chip_sums_{k}", grads[l][k], stages[l][k], k.startswith("w_in"), core)

    def host(*jobs):
        def deliver(couts):
            i = 0
            for piece, sink in jobs:
                n_out = len(piece.out_shapes)
                sink(couts[i:i + n_out])
                i += n_out
        return [piece for piece, _ in jobs], deliver

    whole = lambda k: (k, 0, 1, 1)
    rep = ["norm1_g", "gmlp_ln_g", "gmlp_ln_b", "w_spatial", "b_spatial", "group_norm_g", "norm2_g"]
    rep_w = dict(norm1_g=norm1_g, gmlp_ln_g=gmlp_ln_g, gmlp_ln_b=gmlp_ln_b, w_spatial=w_spatial, b_spatial=b_spatial,
                 group_norm_g=group_norm_g, norm2_g=norm2_g)
    rep_m = dict(norm1_g=m_norm1_g, gmlp_ln_g=m_gmlp_ln_g, gmlp_ln_b=m_gmlp_ln_b, w_spatial=m_w_spatial,
                 b_spatial=m_b_spatial, group_norm_g=m_group_norm_g, norm2_g=m_norm2_g)
    rep_v = dict(norm1_g=v_norm1_g, gmlp_ln_g=v_gmlp_ln_g, gmlp_ln_b=v_gmlp_ln_b, w_spatial=v_w_spatial,
                 b_spatial=v_b_spatial, group_norm_g=v_group_norm_g, norm2_g=v_norm2_g)

    def small_grad_parts():
        parts = [_rows(jnp.stack([small[l][k].reshape(rep_w[k].shape[1:]) for l in range(nl)])) for k in rep]
        parts.append(_rows(d_final_g))
        parts.append(_rows(jnp.stack([small[l]["conv_w"] for l in range(nl)])))
        parts.append(jnp.broadcast_to(loss_part, (8, 128)))
        rows = sum(p.shape[0] for p in parts)
        parts.append(jnp.zeros((-rows % 16, 128), F32))
        return parts

    for l in reversed(range(nl)):
        s = saved[l]
        wi, wo, wgt, wut, wd = [weights[(k, l)] for k in BIG]
        later = l + 1 < nl
        comm, deliver = host(chip_job(l + 1, [("w_in", 0, 1, 2)])) if later else host()
        (grads[l]["w_down"],), couts = _mm_dw("mm_dw_down", [s["act"]], dxb, 2816, 1024, comm)
        deliver(couts)
        comm, deliver = (host(core_job(l, ["w_down"]), chip_job(l + 1, [("w_in", 1, 2, 2)])) if later
                         else host(core_job(l, ["w_down"])))
        (dgate, dup), couts = _mm_dact(dxb, wd, s["dact_dgate"], s["dact_dup"], comm)
        deliver(couts)
        add_up(l, ["w_down"])
        comm, deliver = host(chip_job(l, [("w_down", 0, 3, 4)]))
        (grads[l]["w_gate"],), couts = _mm_dw("mm_dw_gate", [dgate], s["h2"], 2816, 1024, comm)
        deliver(couts)
        comm, deliver = host(chip_job(l, [("w_down", 3, 4, 4)]), core_job(l, ["w_gate"]))
        (grads[l]["w_up"],), couts = _mm_dw("mm_dw_up", [dup], s["h2"], 2816, 1024, comm)
        deliver(couts)
        add_up(l, ["w_gate"])
        comm, deliver = host(chip_job(l, [whole("w_gate")]), core_job(l, ["w_up"]))
        dh2, couts = _mm_dh2(dgate, dup, wgt, wut, comm)
        deliver(couts)
        add_up(l, ["w_up"])
        dx1, dx1b, d_n2 = _rmsnorm_bwd(s["x1"], norm2_g[l:l + 1], dh2, dx)
        comm, deliver = host(chip_job(l, [("w_up", 0, 1, 4)]))
        dy, couts = _mm_dy(dx1b, wo, comm)
        deliver(couts)
        comm, deliver = host(chip_job(l, [("w_up", 1, 2, 4)]))
        (grads[l]["w_out"],), couts = _mm_dw("mm_dw_out", [s["y"]], dx1b, 1024, 1024, comm)
        deliver(couts)
        comm, deliver = host(chip_job(l, [("w_up", 2, 4, 4)]), core_job(l, ["w_out"]))
        (dz, d_lng, d_lnb, d_ws, d_bb, d_cw, d_gg), couts = _mixer_bwd(
            s["z"], dy, gmlp_ln_g[l:l + 1], gmlp_ln_b[l:l + 1], w_spatial[l], bb[l], conv_full[l], group_norm_g[l:l + 1],
            comm)
        deliver(couts)
        add_up(l, ["w_out"])
        small[l] = dict(norm1_g=jnp.zeros((1, d), F32), gmlp_ln_g=d_lng, gmlp_ln_b=d_lnb, w_spatial=d_ws,
                        b_spatial=d_bb[:, :, 0], group_norm_g=d_gg, norm2_g=d_n2, conv_w=d_cw[0:3])
        if l > 0:
            comm, deliver = host(chip_job(l, [whole("w_out")]))
            (grads[l]["w_in"],), couts = _mm_dw("mm_dw_in", [s["h"]], dz, 2048, 1024, comm)
            deliver(couts)
            comm, deliver = host(core_job(l, ["w_in"]))
            dh, couts = _mm_dh(dz, wi, comm)
            deliver(couts)
            add_up(l, ["w_in"])
        else:
            parts = small_grad_parts()
            reduced = []
            comm, deliver = host(chip_job(l, [whole("w_out")]),
                                 (_all_reduce_piece(jnp.concatenate(parts, axis=0)), reduced.extend))
            (grads[l]["w_in_a"],), couts = _mm_dw("mm_dw_in_a", [s["h"]], dz, block["w_in_a"][0], 1024, comm,
                                                  m_rows=(0, block["w_in_a"][0]))
            deliver(couts)
            comm, deliver = host(core_job(l, ["w_in_a"]))
            (grads[l]["w_in_b"],), couts = _mm_dw("mm_dw_in_b", [s["h"]], dz, block["w_in_b"][0], 2560, comm,
                                                  m_rows=(block["w_in_a"][0], block["w_in_b"][0]))
            deliver(couts)
            add_up(l, ["w_in_a"])
            comm, deliver = host(chip_job(l, [whole("w_in_a")]), core_job(l, ["w_in_b"]))
            dh, couts = _mm_dh(dz, wi, comm)
            deliver(couts)
            add_up(l, ["w_in_b"])
        dx, dxb, small[l]["norm1_g"] = _rmsnorm_bwd(s["x"], norm1_g[l:l + 1], dh, dx1, with_bf16=l > 0)
    grad_x = dx.reshape(x.shape)

    sizes = [p.shape[0] for p in parts]
    comm, deliver = host(chip_job(0, [whole("w_in_b")]))
    last, couts = _all_reduce_small(_rows(small[0]["norm1_g"]), comm)
    deliver(couts)
    total = lax.dynamic_update_slice(reduced[0], last, (0, 0))
    offs = [0]
    for n in sizes:
        offs.append(offs[-1] + n)
    pieces = [total[offs[i]:offs[i + 1]] for i in range(len(parts))]
    loss = pieces[len(rep) + 2][0, 0]
    conv_g_full = pieces[len(rep) + 1].reshape(nl, 3, N_DEV, a // N_DEV)
    conv_g = lax.dynamic_index_in_dim(conv_g_full, me, axis=2, keepdims=False)
    names = rep + ["final_norm_g", "conv_w"]
    flat = lambda w: w.reshape(-1, w.shape[-1])
    small_w = [flat(rep_w[k]) for k in rep] + [flat(final_norm_g), flat(conv_w)]
    small_m = [flat(rep_m[k]) for k in rep] + [flat(m_final_norm_g), flat(m_conv_w)]
    small_v = [flat(rep_v[k]) for k in rep] + [flat(v_final_norm_g), flat(v_conv_w)]
    small_g = [pieces[i].reshape(small_w[i].shape) for i in range(len(rep) + 1)] + [flat(conv_g)]
    small_d, small_m, small_v = _adamw_small(small_g, small_w, small_m, small_v)
    shape_of = dict(rep_w, final_norm_g=final_norm_g, conv_w=conv_w)
    named = lambda arrays: {k: arr.reshape(shape_of[k].shape) for k, arr in zip(names, arrays)}
    res = {"grad": named(small_g), "delta": named(small_d), "m": named(small_m), "v": named(small_v)}

    for k in BIG:
        outs, _ = _adamw_big(f"adamw_{k}", lands[k], big[k], big_m[k], big_v[k])
        if k in ("w_gate", "w_up"):
            outs = [tr(o) for o in outs]
        res["grad"][k], res["delta"][k], res["m"][k], res["v"][k] = outs

    order = ["norm1_g", "w_in", "gmlp_ln_g", "gmlp_ln_b", "w_spatial", "b_spatial", "conv_w", "group_norm_g", "w_out",
             "norm2_g", "w_gate", "w_up", "w_down", "final_norm_g"]
    return (loss, grad_x, *[res["grad"][k] for k in order], *[res["delta"][k] for k in order],
            *[res["m"][k] for k in order], *[res["v"][k] for k in order])
```

```python
import functools
import math
import operator

import jax
import jax.numpy as jnp
from jax import lax
from jax.experimental import pallas as pl
from jax.experimental.pallas import tpu as pltpu

F32 = jnp.float32
BF16 = jnp.bfloat16
MESH = pl.DeviceIdType.MESH

N_DEV = 8
N_LAYERS = 2
HEADS = 8
BLK = 128
CHUNK = 64
HALO = 16
RMS_EPS = 1e-6
LN_EPS = 1e-5
ADAM_LR, ADAM_B1, ADAM_B2, ADAM_EPS, ADAM_WD, ADAM_STEP = 0.001, 0.9, 0.999, 1e-8, 0.01, 10
GELU_C = math.sqrt(2.0 / math.pi)
GELU_A = 0.044715

VMEM_LIMIT_V7X = 56 * 1024 * 1024
_TM = 1024
_TN = 1024
_TT = 1024
_TM_MIX = 256
_TM_NORM = 512


def _cparams(n_axes):
    return pltpu.CompilerParams(dimension_semantics=("arbitrary",) * n_axes, vmem_limit_bytes=VMEM_LIMIT_V7X)


def _sds(shape, dtype):
    return jax.ShapeDtypeStruct(tuple(shape), dtype)


def _place():
    return lax.axis_index("x"), lax.axis_index("y"), lax.axis_index("c")


def _index(place):
    return 4 * place[0] + 2 * place[1] + place[2]


class _Piece:
    def __init__(self, operands, out_shapes, aliases, n_sems, start, finish, mid1=None, mid2=None, vmem=(),
                 hooks=(0.6, 0.87)):
        self.operands, self.out_shapes, self.aliases, self.n_sems = list(operands), list(out_shapes), dict(aliases), n_sems
        self.vmem = list(vmem)
        self.hooks = hooks
        nothing = lambda ctx: None
        self.start, self.mid1, self.mid2, self.finish = start, mid1 or nothing, mid2 or nothing, finish


class _Ctx:
    def __init__(self, ins, outs, sems, offs):
        self.ins, self.outs, self.sems = ins, outs, sems
        self.o_in, self.o_out, self.o_send, self.o_recv, self.o_loc, self.o_vmem = offs

    def vmem(self, i):
        return self.sems[3 + self.o_vmem + i]

    def inp(self, i):
        return self.ins[self.o_in + i]

    def out(self, i):
        return self.outs[self.o_out + i]

    def send(self, k):
        return self.sems[0].at[self.o_send + k]

    def recv(self, k):
        return self.sems[1].at[self.o_recv + k]

    def local(self, k):
        return self.sems[2].at[self.o_loc + k]


class _Hosted:
    def __init__(self, pieces, n_in_before, n_out_before):
        self.pieces = [p for p in (pieces or []) if p is not None]
        self.operands, self.out_shapes, self.aliases, self.offs = [], [], {}, []
        counts, vmem = [0, 0, 0], []
        for p in self.pieces:
            self.offs.append((len(self.operands), len(self.out_shapes), *counts, len(vmem)))
            for i, j in p.aliases.items():
                self.aliases[n_in_before + len(self.operands) + i] = n_out_before + len(self.out_shapes) + j
            self.operands += p.operands
            self.out_shapes += p.out_shapes
            counts = [c + n for c, n in zip(counts, p.n_sems)]
            vmem += p.vmem
        hbm = pl.BlockSpec(memory_space=pl.ANY)
        self.in_specs = [hbm] * len(self.operands)
        self.out_specs = [hbm] * len(self.out_shapes)
        self.scratch = ([pltpu.SemaphoreType.DMA((max(c, 1),)) for c in counts] + vmem) if self.pieces else []

    def run(self, stage, ins, outs, sems):
        for p, offs in zip(self.pieces, self.offs):
            getattr(p, stage)(_Ctx(ins, outs, sems, offs))

    def wrap(self, grid, compute, ins, outs, sems):
        if not self.pieces:
            compute()
            return
        n_steps = math.prod(grid)
        lin = 0
        for ax, g in enumerate(grid):
            lin = lin * g + pl.program_id(ax)
        pl.when(lin == 0)(lambda: self.run("start", ins, outs, sems))
        compute()
        for stage, which in (("mid1", 0), ("mid2", 1)):
            for p, offs in zip(self.pieces, self.offs):
                at = min(n_steps - 1, int(p.hooks[which] * n_steps))
                pl.when(lin == at)(functools.partial(getattr(p, stage), _Ctx(ins, outs, sems, offs)))
        pl.when(lin == n_steps - 1)(lambda: self.run("finish", ins, outs, sems))


def _cols_view(width):
    return lambda ref, p: ref.at[:, pl.ds(pl.multiple_of(p * width, 128), width)]


def _rows_view(height):
    return lambda ref, p: ref.at[pl.ds(pl.multiple_of(p * height, 16), height), :]


def _cols_halves(rows, width, part, n_parts):
    hr = rows // n_parts // 2
    at = lambda h: pl.ds(part * 2 * hr + h * hr, hr)
    return (lambda ref, p, h: ref.at[at(h), pl.ds(pl.multiple_of(p * width, 128), width)],
            lambda ref, h: ref.at[at(h), :], 2)


def _rows_halves(height, part, n_parts):
    hh = height // n_parts // 2
    return (lambda ref, p, h: ref.at[pl.ds(pl.multiple_of(p * height + part * 2 * hh + h * hh, 16), hh), :],
            lambda ref, h: ref.at[pl.ds(part * 2 * hh + h * hh, hh), :], 2)


_SLOT_WHOLE = (lambda ref, p, h: ref.at[p], lambda ref, h: ref, 1)


def _ag_piece(specs):
    units = [(a, h) for a, s in enumerate(specs) for h in s[3]]

    def plan(ctx):
        x, y, c = _place()
        me, sib, xn, yn, dg = (x, y, c), (x, y, 1 - c), (1 - x, y, c), (x, 1 - y, c), (1 - x, 1 - y, c)

        def copy(u, k, block, to, from_shard=False):
            a, h = units[u]
            dst_of, src_of, _ = specs[a][2]
            dst = dst_of(ctx.out(a), _index(block), h)
            return pltpu.make_async_remote_copy(
                src_ref=src_of(ctx.inp(a), h) if from_shard else dst, dst_ref=dst, send_sem=ctx.send(7 * u + k),
                recv_sem=ctx.recv(7 * u + k), device_id=to, device_id_type=MESH)

        def local(u):
            a, h = units[u]
            dst_of, src_of, _ = specs[a][2]
            return pltpu.make_async_copy(src_of(ctx.inp(a), h), dst_of(ctx.out(a), _index(me), h), ctx.local(u))

        def relay(u):
            return copy(u, 3, xn, yn) if units[u][1] % 2 == 0 else copy(u, 3, yn, xn)

        return me, sib, xn, yn, dg, c, copy, local, relay

    def start(ctx):
        me, sib, xn, yn, dg, c, copy, local, relay = plan(ctx)
        for u in range(len(units)):
            local(u).start()
            for k, to in enumerate((sib, xn, yn)):
                copy(u, k, me, to, from_shard=True).start()

    def mid1(ctx):
        me, sib, xn, yn, dg, c, copy, local, relay = plan(ctx)
        for u in range(len(units)):
            copy(u, 1, xn, me).wait_recv()
            copy(u, 2, yn, me).wait_recv()
            relay(u).start()
            copy(u, 4, xn, sib).start()
            copy(u, 5, yn, sib).start()

    def mid2(ctx):
        me, sib, xn, yn, dg, c, copy, local, relay = plan(ctx)
        for u in range(len(units)):
            copy(u, 3, dg, me).wait_recv()
            copy(u, 6, dg, sib).start()

    def finish(ctx):
        me, sib, xn, yn, dg, c, copy, local, relay = plan(ctx)
        other = lambda place: (place[0], place[1], 1 - c)
        for u in range(len(units)):
            for k, block in ((0, sib), (4, other(xn)), (5, other(yn)), (6, other(dg))):
                copy(u, k, block, me).wait_recv()
        for u in range(len(units)):
            for k, to in enumerate((sib, xn, yn)):
                copy(u, k, me, to, from_shard=True).wait_send()
            relay(u).wait_send()
            for k, block in ((4, xn), (5, yn), (6, dg)):
                copy(u, k, block, sib).wait_send()
            local(u).wait()

    n_u = len(units)
    operands, aliases = [s[0] for s in specs], {}
    for a, spec in enumerate(specs):
        if spec[4] is not None:
            aliases[len(operands)] = a
            operands.append(spec[4])
    return _Piece(operands, [s[1] for s in specs], aliases, (7 * n_u, 7 * n_u, n_u), start, finish, mid1, mid2)


N_CHIPS = 4


def _rs_core_piece(specs):
    n = len(specs)

    def copies(ctx):
        x, y, c = _place()
        out = []
        for a in range(n):
            for q in range(N_CHIPS):
                out.append(pltpu.make_async_remote_copy(
                    src_ref=specs[a][2](ctx.inp(a), 2 * q + (1 - c)), dst_ref=ctx.out(a).at[q],
                    send_sem=ctx.send(N_CHIPS * a + q), recv_sem=ctx.recv(N_CHIPS * a + q), device_id=(x, y, 1 - c),
                    device_id_type=MESH))
        return out

    def start(ctx):
        for cp in copies(ctx):
            cp.start()

    def finish(ctx):
        for cp in copies(ctx):
            cp.wait_recv()
            cp.wait_send()

    return _Piece([s[0] for s in specs], [s[1] for s in specs], {}, (N_CHIPS * n, N_CHIPS * n, 0), start, finish)


def _rs_chip_piece(specs, layer):
    n = len(specs)
    hops = [(1, 0), (0, 1), (1, 1)]

    def copies(ctx):
        x, y, c = _place()
        mine = 2 * x + y
        out = []
        for a in range(n):
            first, landing, size = specs[a][2]
            rows, to = pl.ds(first, size), pl.ds(landing, size)
            sums, land = ctx.inp(a), ctx.out(a)
            out.append((pltpu.make_async_copy(sums.at[mine, rows], land.at[layer, mine, to], ctx.local(a)), None))
            for j, (dx, dy) in enumerate(hops):
                px, py = x ^ dx, y ^ dy
                peer = 2 * px + py
                send = pltpu.make_async_remote_copy(
                    src_ref=sums.at[peer, rows], dst_ref=land.at[layer, mine, to], send_sem=ctx.send(3 * a + j),
                    recv_sem=ctx.recv(3 * a + j), device_id=(px, py, c), device_id_type=MESH)
                recv = pltpu.make_async_remote_copy(
                    src_ref=sums.at[peer, rows], dst_ref=land.at[layer, peer, to], send_sem=ctx.send(3 * a + j),
                    recv_sem=ctx.recv(3 * a + j), device_id=(px, py, c), device_id_type=MESH)
                out.append((send, recv))
        return out

    def start(ctx):
        for send, _ in copies(ctx):
            send.start()

    def finish(ctx):
        for send, recv in copies(ctx):
            if recv is None:
                send.wait()
            else:
                recv.wait_recv()
                send.wait_send()

    operands, aliases = [s[0] for s in specs], {}
    for a, spec in enumerate(specs):
        if spec[3] is not None:
            aliases[len(operands)] = a
            operands.append(spec[3])
    return _Piece(operands, [s[1] for s in specs], aliases, (3 * n, 3 * n, n), start, finish)


def _all_reduce_piece(pack):
    r = pack.shape[0]
    half = r // 2

    def plan(ctx):
        x, y, c = _place()
        acc, got = ctx.vmem(0), ctx.vmem(1)
        mine = pl.ds(pl.multiple_of(c * half, 8), half)
        sib = (x, y, 1 - c)
        copies = [
            pltpu.make_async_remote_copy(src_ref=acc.at[0], dst_ref=got.at[0], send_sem=ctx.send(0), recv_sem=ctx.recv(0),
                                         device_id=sib, device_id_type=MESH),
            pltpu.make_async_remote_copy(src_ref=acc.at[1, mine], dst_ref=got.at[1, mine], send_sem=ctx.send(1),
                                         recv_sem=ctx.recv(1), device_id=(1 - x, y, c), device_id_type=MESH),
            pltpu.make_async_remote_copy(src_ref=acc.at[2, mine], dst_ref=got.at[2, mine], send_sem=ctx.send(2),
                                         recv_sem=ctx.recv(2), device_id=(x, 1 - y, c), device_id_type=MESH),
            pltpu.make_async_remote_copy(src_ref=acc.at[3, mine], dst_ref=acc.at[3, mine], send_sem=ctx.send(3),
                                         recv_sem=ctx.recv(3), device_id=sib, device_id_type=MESH),
        ]
        other = pl.ds(pl.multiple_of((1 - c) * half, 8), half)
        arrival = pltpu.make_async_remote_copy(src_ref=acc.at[3, other], dst_ref=acc.at[3, other], send_sem=ctx.send(3),
                                               recv_sem=ctx.recv(3), device_id=sib, device_id_type=MESH)
        return acc, got, mine, copies, arrival

    def start(ctx):
        acc, got, mine, copies, arrival = plan(ctx)
        load = pltpu.make_async_copy(ctx.inp(0), acc.at[0], ctx.local(0))
        load.start()
        load.wait()
        copies[0].start()

    def mid1(ctx):
        acc, got, mine, copies, arrival = plan(ctx)
        copies[0].wait()
        acc[1] = acc[0] + got[0]
        copies[1].start()

    def mid2(ctx):
        acc, got, mine, copies, arrival = plan(ctx)
        copies[1].wait()
        acc[2, mine] = acc[1, mine] + got[1, mine]
        copies[2].start()

    def finish(ctx):
        acc, got, mine, copies, arrival = plan(ctx)
        copies[2].wait()
        acc[3, mine] = acc[2, mine] + got[2, mine]
        copies[3].start()
        copies[3].wait_send()
        arrival.wait_recv()
        store = pltpu.make_async_copy(acc.at[3], ctx.out(0), ctx.local(0))
        store.start()
        store.wait()

    return _Piece([pack], [_sds(pack.shape, F32)], {}, (4, 4, 1), start, finish, mid1, mid2,
                  vmem=[pltpu.VMEM((4, r, 128), F32), pltpu.VMEM((3, r, 128), F32)], hooks=(0.25, 0.6))


def _chip_sums(name, grad, stage, by_cols, core):
    _, r, c = stage.shape
    tr = r
    while tr * c > 1024 * 1024 or r % tr or tr % 16:
        tr -= 16
    n_t = r // tr

    def body(core_ref, g_ref, s_ref, o_ref):
        o_ref[...] = (g_ref[...].astype(F32) + s_ref[...].astype(F32)).astype(BF16)

    if by_cols:
        gspec = pl.BlockSpec((tr, c), lambda q, i, core_ref: (i, 2 * q + core_ref[0]))
    else:
        gspec = pl.BlockSpec((tr, c), lambda q, i, core_ref: ((2 * q + core_ref[0]) * n_t + i, 0))
    sspec = pl.BlockSpec((None, tr, c), lambda q, i, core_ref: (q, i, 0))
    return pl.pallas_call(
        body, name=name, out_shape=_sds(stage.shape, BF16),
        grid_spec=pltpu.PrefetchScalarGridSpec(num_scalar_prefetch=1, grid=(N_CHIPS, n_t), in_specs=[gspec, sspec],
                                               out_specs=sspec),
        compiler_params=_cparams(2))(core, grad, stage)


def _call_hosting(body, name, grid, out_shapes, in_specs, out_specs, operands, scratch, comm):
    n_in, n_out, n_scr = len(operands), len(out_shapes), len(scratch)
    hosted = _Hosted(comm, n_in, n_out)
    n_ci, n_co = len(hosted.operands), len(hosted.out_shapes)

    def hosting_body(*refs):
        ins, rest = refs[:n_in], refs[n_in:]
        c_ins, rest = rest[:n_ci], rest[n_ci:]
        outs, rest = rest[:n_out], rest[n_out:]
        c_outs, rest = rest[:n_co], rest[n_co:]
        hosted.wrap(grid, lambda: body(*ins, *outs, *rest[:n_scr]), c_ins, c_outs, rest[n_scr:])

    res = pl.pallas_call(
        hosting_body, name=name, grid=grid, out_shape=tuple(list(out_shapes) + hosted.out_shapes),
        in_specs=list(in_specs) + hosted.in_specs, out_specs=tuple(list(out_specs) + hosted.out_specs),
        input_output_aliases=hosted.aliases, scratch_shapes=list(scratch) + hosted.scratch,
        compiler_params=_cparams(len(grid)))(*operands, *hosted.operands)
    return list(res[:n_out]), list(res[n_out:])


def _matmul(name, grid, nk, kaxis, pairs, dims, extras, outs, epilogue, sum_pairs, acc_shape, comm=None, split=None):
    n_p, n_e, n_o = len(pairs), len(extras), len(outs)
    n_acc = 0 if nk == 1 else (1 if sum_pairs else n_p)
    n_in = 2 * n_p + n_e
    hosted = _Hosted(comm, n_in, n_o)
    n_ci, n_co = len(hosted.operands), len(hosted.out_shapes)

    def body(*refs):
        a_refs = refs[0:2 * n_p:2]
        b_refs = refs[1:2 * n_p:2]
        e_refs = refs[2 * n_p:n_in]
        c_ins = refs[n_in:n_in + n_ci]
        o_refs = refs[n_in + n_ci:n_in + n_ci + n_o]
        c_outs = refs[n_in + n_ci + n_o:n_in + n_ci + n_o + n_co]
        acc_refs = refs[n_in + n_ci + n_o + n_co:n_in + n_ci + n_o + n_co + n_acc]
        sems = refs[n_in + n_ci + n_o + n_co + n_acc:]

        def dots():
            if sum_pairs and n_p > 1 and dims == NN:
                a_all = jnp.concatenate([a[...] for a in a_refs], axis=1)
                b_all = jnp.concatenate([b[...] for b in b_refs], axis=0)
                return [lax.dot_general(a_all, b_all, (dims, ((), ())), preferred_element_type=F32)]
            prods = [lax.dot_general(a[...], b[...], (dims, ((), ())), preferred_element_type=F32)
                     for a, b in zip(a_refs, b_refs)]
            if sum_pairs and n_p > 1:
                prods = [functools.reduce(operator.add, prods)]
            return prods

        def compute():
            if nk == 1 and split is not None:
                n_split, b_axis, n_row = split
                width = b_refs[0].shape[b_axis] // n_split
                height = a_refs[0].shape[0] // n_row
                for s in range(n_split):
                    cols = pl.ds(s * width, width)
                    for r in range(n_row):
                        rows = pl.ds(r * height, height)
                        epilogue([lax.dot_general(a[rows, :], b[cols, :] if b_axis == 0 else b[:, cols], (dims, ((), ())),
                                                  preferred_element_type=F32) for a, b in zip(a_refs, b_refs)],
                                 e_refs, o_refs, rows, cols)
                return
            if nk == 1:
                epilogue(dots(), e_refs, o_refs)
                return
            k = pl.program_id(kaxis)

            @pl.when(k == 0)
            def _():
                for acc, p in zip(acc_refs, dots()):
                    acc[...] = p

            if nk > 2:
                @pl.when((k > 0) & (k < nk - 1))
                def _():
                    for acc, p in zip(acc_refs, dots()):
                        acc[...] += p

            @pl.when(k == nk - 1)
            def _():
                epilogue([acc[...] + p for acc, p in zip(acc_refs, dots())], e_refs, o_refs)

        hosted.wrap(grid, compute, c_ins, c_outs, sems)

    operands, in_specs = [], []
    for a, a_spec, b, b_spec in pairs:
        operands += [a, b]
        in_specs += [a_spec, b_spec]
    for e, e_spec in extras:
        operands.append(e)
        in_specs.append(e_spec)
    res = pl.pallas_call(
        body, name=name, grid=grid,
        out_shape=tuple([o for o, _ in outs] + hosted.out_shapes),
        in_specs=in_specs + hosted.in_specs, out_specs=tuple([s for _, s in outs] + hosted.out_specs),
        input_output_aliases=hosted.aliases,
        scratch_shapes=[pltpu.VMEM(acc_shape, F32) for _ in range(n_acc)] + hosted.scratch,
        compiler_params=_cparams(len(grid)),
    )(*operands, *hosted.operands)
    return list(res[:n_o]), list(res[n_o:])


NN = ((1,), (0,))
NT = ((1,), (1,))
TN = ((0,), (0,))


def _tile(n, want):
    if n <= want:
        return n
    t = want // 128 * 128
    while n % t:
        t -= 128
    return t


def _silu_parts(g):
    s = 0.5 + 0.5 * jnp.tanh(0.5 * g)
    return s, g * s


def _mm_in(h, w_in, comm=None):
    t, d = h.shape
    n = w_in.shape[1]
    tm, tn = _tile(t, _TM), _tile(n, _TN)

    def epi(accs, e, o):
        o[0][...] = accs[0].astype(BF16)

    outs, couts = _matmul(
        "mm_in", (n // tn, t // tm), 1, None,
        [(h, pl.BlockSpec((tm, d), lambda j, i: (i, 0)), w_in, pl.BlockSpec((d, tn), lambda j, i: (0, j)))],
        NN, [], [(_sds((t, n), BF16), pl.BlockSpec((tm, tn), lambda j, i: (i, j)))], epi, True, None, comm)
    return outs[0], couts


def _mm_out(y, w_out, x, comm=None):
    t, m = y.shape
    d = w_out.shape[1]
    tm, tn = _tile(t, _TM), _tile(d, _TN)

    def epi(accs, e, o):
        o[0][...] = e[0][...] + accs[0]

    outs, couts = _matmul(
        "mm_out", (t // tm, d // tn), 1, None,
        [(y, pl.BlockSpec((tm, m), lambda i, j: (i, 0)), w_out, pl.BlockSpec((m, tn), lambda i, j: (0, j)))],
        NN, [(x, pl.BlockSpec((tm, tn), lambda i, j: (i, j)))],
        [(_sds((t, d), F32), pl.BlockSpec((tm, tn), lambda i, j: (i, j)))], epi, True, None, comm)
    return outs[0], couts


def _mm_swiglu(h2, wgt, wut, comm=None):
    t, d = h2.shape
    f = wgt.shape[0]
    tm, tn = _tile(t, 2 * _TM), _tile(f, 512)

    def epi(accs, e, o, rows, cols):
        g, u = accs
        s, sg = _silu_parts(g)
        o[0][rows, cols] = (sg * u).astype(BF16)
        o[1][rows, cols] = (u * (s + sg * (1.0 - s))).astype(BF16)
        o[2][rows, cols] = sg.astype(BF16)

    wspec = pl.BlockSpec((tn, d), lambda i, j: (j, 0))
    hspec = pl.BlockSpec((tm, d), lambda i, j: (i, 0))
    ospec = pl.BlockSpec((tm, tn), lambda i, j: (i, j))
    osh = _sds((t, f), BF16)
    outs, couts = _matmul("mm_swiglu", (t // tm, f // tn), 1, None, [(h2, hspec, wgt, wspec), (h2, hspec, wut, wspec)],
                          NT, [], [(osh, ospec)] * 3, epi, False, None, comm, split=(tn // 256, 0, 2))
    return outs, couts


def _mm_down(act, wd, x1, comm=None):
    t, f = act.shape
    d = wd.shape[1]
    tm, tn = _tile(t, _TM), _tile(d, _TN)
    nk = 2
    tk = f // nk

    def epi(accs, e, o):
        o[0][...] = e[0][...] + accs[0]

    outs, couts = _matmul(
        "mm_down", (t // tm, d // tn, nk), nk, 2,
        [(act, pl.BlockSpec((tm, tk), lambda i, j, k: (i, k)), wd, pl.BlockSpec((tk, tn), lambda i, j, k: (k, j)))],
        NN, [(x1, pl.BlockSpec((tm, tn), lambda i, j, k: (i, j)))],
        [(_sds((t, d), F32), pl.BlockSpec((tm, tn), lambda i, j, k: (i, j)))], epi, True, (tm, tn), comm)
    return outs[0], couts


def _mm_dact(dxb, wd, dact_dgate, dact_dup, comm=None):
    t, d = dxb.shape
    f = wd.shape[0]
    tm, tn = _tile(t, 2 * _TM), _tile(f, 512)

    def epi(accs, e, o, rows, cols):
        da = accs[0]
        o[0][rows, cols] = (da * e[0][rows, cols].astype(F32)).astype(BF16)
        o[1][rows, cols] = (da * e[1][rows, cols].astype(F32)).astype(BF16)

    bspec = pl.BlockSpec((tm, tn), lambda i, j: (i, j))
    osh = _sds((t, f), BF16)
    outs, couts = _matmul(
        "mm_dact", (t // tm, f // tn), 1, None,
        [(dxb, pl.BlockSpec((tm, d), lambda i, j: (i, 0)), wd, pl.BlockSpec((tn, d), lambda i, j: (j, 0)))],
        NT, [(dact_dgate, bspec), (dact_dup, bspec)], [(osh, bspec)] * 2, epi, True, None, comm, split=(tn // 256, 0, 2))
    return outs, couts


def _mm_dh2(dgate, dup, wgt, wut, comm=None):
    t, f = dgate.shape
    d = wgt.shape[1]
    tm, tn = _tile(t, _TM), _tile(d, _TN)
    nk = 4
    tk = f // nk

    def epi(accs, e, o):
        o[0][...] = accs[0].astype(BF16)

    aspec = pl.BlockSpec((tm, tk), lambda i, j, k: (i, k))
    wspec = pl.BlockSpec((tk, tn), lambda i, j, k: (k, j))
    outs, couts = _matmul("mm_dh2", (t // tm, d // tn, nk), nk, 2, [(dgate, aspec, wgt, wspec), (dup, aspec, wut, wspec)],
                          NN, [], [(_sds((t, d), BF16), pl.BlockSpec((tm, tn), lambda i, j, k: (i, j)))], epi, True,
                          (tm, tn), comm)
    return outs[0], couts


def _mm_dw(name, a_list, b, tmo, tno, comm=None, m_rows=None):
    t, m = a_list[0].shape
    start, m = (0, m) if m_rows is None else m_rows
    n = b.shape[1]
    tt = _tile(t, _TT)
    nk = t // tt
    tmo, tno = _tile(m, tmo), _tile(n, tno)
    first = start // tmo

    def epi(accs, e, o):
        for acc, out in zip(accs, o):
            out[...] = acc.astype(BF16)

    aspec = pl.BlockSpec((tt, tmo), lambda i, j, k: (k, first + i))
    bspec = pl.BlockSpec((tt, tno), lambda i, j, k: (k, j))
    ospec = pl.BlockSpec((tmo, tno), lambda i, j, k: (i, j))
    if nk == 1:
        return _matmul(name, (m // tmo, n // tno, 1), 1, None, [(a, aspec, b, bspec) for a in a_list], TN, [],
                       [(_sds((m, n), BF16), ospec)] * len(a_list), epi, False, None, comm)
    return _matmul(name, (m // tmo, n // tno, nk), nk, 2, [(a, aspec, b, bspec) for a in a_list], TN, [],
                   [(_sds((m, n), BF16), ospec)] * len(a_list), epi, False, (tmo, tno), comm)


def _mm_dy(dxb, w_out, comm=None):
    t, d = dxb.shape
    m = w_out.shape[0]
    tm, tn = _tile(t, _TM), _tile(m, _TN)

    def epi(accs, e, o):
        o[0][...] = accs[0].astype(BF16)

    outs, couts = _matmul(
        "mm_dy", (t // tm, m // tn), 1, None,
        [(dxb, pl.BlockSpec((tm, d), lambda i, j: (i, 0)), w_out, pl.BlockSpec((tn, d), lambda i, j: (j, 0)))], NT, [],
        [(_sds((t, m), BF16), pl.BlockSpec((tm, tn), lambda i, j: (i, j)))], epi, True, None, comm)
    return outs[0], couts


def _mm_dh(dz, w_in, comm=None):
    t, n = dz.shape
    d = w_in.shape[0]
    tm, tn = _tile(t, _TM), _tile(d, _TN)
    nk = 2
    tk = n // nk

    def epi(accs, e, o):
        o[0][...] = accs[0].astype(BF16)

    outs, couts = _matmul(
        "mm_dh", (t // tm, d // tn, nk), nk, 2,
        [(dz, pl.BlockSpec((tm, tk), lambda i, j, k: (i, k)), w_in, pl.BlockSpec((tn, tk), lambda i, j, k: (j, k)))], NT,
        [], [(_sds((t, d), BF16), pl.BlockSpec((tm, tn), lambda i, j, k: (i, j)))], epi, True, (tm, tn), comm)
    return outs[0], couts


def _rmsnorm_fwd(x, g, comm=None):
    t, d = x.shape
    tm = min(_TM_NORM, t)

    def body(x_ref, g_ref, o_ref):
        xv = x_ref[...]
        rs = lax.rsqrt(jnp.mean(xv * xv, axis=-1, keepdims=True) + RMS_EPS)
        o_ref[...] = (xv * rs * g_ref[...]).astype(BF16)

    outs, couts = _call_hosting(
        body, "rmsnorm_fwd", (t // tm,), [_sds((t, d), BF16)],
        [pl.BlockSpec((tm, d), lambda i: (i, 0)), pl.BlockSpec((1, d), lambda i: (0, 0))],
        [pl.BlockSpec((tm, d), lambda i: (i, 0))], [x, g], [], comm)
    return outs[0], couts


def _rmsnorm_bwd_math(xv, g, dh):
    rs = lax.rsqrt(jnp.mean(xv * xv, axis=-1, keepdims=True) + RMS_EPS)
    xh = xv * rs
    gd = dh * g
    dx = rs * (gd - xh * jnp.mean(gd * xh, axis=-1, keepdims=True))
    return dx, jnp.sum(dh * xh, axis=0, keepdims=True)


def _rmsnorm_bwd(x, g, dh, dres, with_bf16=True):
    t, d = x.shape
    tm = min(_TM_NORM, t)

    def body(x_ref, g_ref, dh_ref, dres_ref, dx_ref, *rest):
        dg_ref = rest[-1]
        dx, dg = _rmsnorm_bwd_math(x_ref[...], g_ref[...], dh_ref[...].astype(F32))
        dx = dx + dres_ref[...]
        dx_ref[...] = dx
        if with_bf16:
            rest[0][...] = dx.astype(BF16)

        @pl.when(pl.program_id(0) == 0)
        def _():
            dg_ref[...] = dg

        @pl.when(pl.program_id(0) > 0)
        def _():
            dg_ref[...] += dg

    row = pl.BlockSpec((tm, d), lambda i: (i, 0))
    vec = pl.BlockSpec((1, d), lambda i: (0, 0))
    halves = [(_sds((t, d), BF16), row)] if with_bf16 else []
    outs = [(_sds((t, d), F32), row), *halves, (_sds((1, d), F32), vec)]
    res = pl.pallas_call(
        body, name="rmsnorm_bwd", grid=(t // tm,), out_shape=tuple(o for o, _ in outs),
        in_specs=[row, vec, row, row], out_specs=tuple(s for _, s in outs), compiler_params=_cparams(1))(x, g, dh, dres)
    return (res[0], res[1], res[2]) if with_bf16 else (res[0], None, res[1])


def _loss_head(x, g, target):
    t, d = x.shape
    tm = min(_TM_NORM, t)

    def body(x_ref, g_ref, t_ref, dx_ref, dxb_ref, dg_ref, loss_ref):
        xv, gv = x_ref[...], g_ref[...]
        rs = lax.rsqrt(jnp.mean(xv * xv, axis=-1, keepdims=True) + RMS_EPS)
        diff = xv * rs * gv - t_ref[...]
        part = 0.5 * jnp.sum(jnp.mean(diff * diff, axis=-1, keepdims=True), axis=0, keepdims=True)
        part = jnp.broadcast_to(part, (1, 128))
        dx, dg = _rmsnorm_bwd_math(xv, gv, diff * (1.0 / d))
        dx_ref[...] = dx
        dxb_ref[...] = dx.astype(BF16)

        @pl.when(pl.program_id(0) == 0)
        def _():
            dg_ref[...] = dg
            loss_ref[...] = part

        @pl.when(pl.program_id(0) > 0)
        def _():
            dg_ref[...] += dg
            loss_ref[...] += part

    row = pl.BlockSpec((tm, d), lambda i: (i, 0))
    vec = pl.BlockSpec((1, d), lambda i: (0, 0))
    return pl.pallas_call(
        body, name="loss_head", grid=(t // tm,),
        out_shape=(_sds((t, d), F32), _sds((t, d), BF16), _sds((1, d), F32), _sds((1, 128), F32)),
        in_specs=[row, vec, row], out_specs=(row, row, vec, pl.BlockSpec((1, 128), lambda i: (0, 0))),
        compiler_params=_cparams(1))(x, g, target)


def _gelu(x):
    th = jnp.tanh(GELU_C * (x + GELU_A * x * x * x))
    return 0.5 * x * (1.0 + th), th


def _gelu_grad(x, th):
    return 0.5 * (1.0 + th) + 0.5 * x * (1.0 - th * th) * GELU_C * (1.0 + 3.0 * GELU_A * x * x)


def _masked_ws(ws_ref, h):
    i = lax.broadcasted_iota(jnp.int32, (BLK, BLK), 0) // CHUNK
    j = lax.broadcasted_iota(jnp.int32, (BLK, BLK), 1) // CHUNK
    return jnp.where(j <= i, ws_ref[h], 0.0)


def _shift_down(q, n, first_rows):
    rolled = pltpu.roll(q, n, 0)
    row = lax.broadcasted_iota(jnp.int32, q.shape, 0)
    for r, val in enumerate(first_rows):
        rolled = jnp.where(row == r, val, rolled)
    return rolled


def _shift_up(q, n, last_rows):
    tm = q.shape[0]
    rolled = pltpu.roll(q, tm - n, 0)
    row = lax.broadcasted_iota(jnp.int32, q.shape, 0)
    for r, val in enumerate(last_rows):
        rolled = jnp.where(row == tm - n + r, val, rolled)
    return rolled


def _mixer_specs(t, a, tm):
    hb = tm // HALO
    last = t // HALO - 1
    tile = pl.BlockSpec((tm, 5 * a), lambda i: (i, 0))
    prev = [pl.BlockSpec((HALO, a), functools.partial(lambda i, col: (jnp.maximum(i * hb - 1, 0), col), col=col))
            for col in (3, 4)]
    nxt = [pl.BlockSpec((HALO, a), functools.partial(lambda i, col: (jnp.minimum((i + 1) * hb, last), col), col=col))
           for col in (2, 3, 4)]
    return tile, prev, nxt


def _group_a_fwd(zu, zv, lng, lnb, ws_ref, bb_ref, mixed_ref, vln_ref):
    u, thu = _gelu(zu)
    v, thv = _gelu(zv)
    mu = jnp.mean(v, axis=-1, keepdims=True)
    vc = v - mu
    rs = lax.rsqrt(jnp.mean(vc * vc, axis=-1, keepdims=True) + LN_EPS)
    vhat = vc * rs
    vln_ref[...] = vhat * lng + lnb
    tm, a = zu.shape
    hd = a // HEADS
    for h in range(HEADS):
        w = _masked_ws(ws_ref, h).astype(BF16)
        for b in range(tm // BLK):
            rows, cols = pl.ds(b * BLK, BLK), pl.ds(h * hd, hd)
            mixed_ref[rows, cols] = jnp.dot(w, vln_ref[rows, cols].astype(BF16), preferred_element_type=F32) + bb_ref[h]
    return u, thu, thv, rs, vhat


def _mixer_fwd(z, ln_g, ln_b, w_spatial, bb, conv_w, gg, comm=None):
    t = z.shape[0]
    a = z.shape[1] // 5
    tm = min(_TM_MIX, t)
    tile, prev, _ = _mixer_specs(t, a, tm)

    def body(z_ref, pc_ref, ph_ref, lng_ref, lnb_ref, ws_ref, bb_ref, cw_ref, gg_ref, y_ref, mixed_ref, vln_ref):
        i = pl.program_id(0)
        zu = z_ref[:, 0:a].astype(F32)
        zv = z_ref[:, a:2 * a].astype(F32)
        u, _, _, _, _ = _group_a_fwd(zu, zv, lng_ref[...], lnb_ref[...], ws_ref, bb_ref, mixed_ref, vln_ref)
        ya = u * mixed_ref[...]
        ra = lax.rsqrt(jnp.mean(ya * ya, axis=-1, keepdims=True) + RMS_EPS)
        y_ref[:, 0:a] = (ya * ra * gg_ref[:, 0:a]).astype(BF16)

        zb = z_ref[:, 2 * a:3 * a].astype(F32)
        q = z_ref[:, 3 * a:4 * a].astype(F32) * z_ref[:, 4 * a:5 * a].astype(F32)
        qp = jnp.where(i > 0, pc_ref[...].astype(F32) * ph_ref[...].astype(F32), 0.0)
        qm1 = _shift_down(q, 1, [qp[HALO - 1:HALO]])
        qm2 = _shift_down(q, 2, [qp[HALO - 2:HALO - 1], qp[HALO - 1:HALO]])
        cv = cw_ref[0:1, :] * qm2 + cw_ref[1:2, :] * qm1 + cw_ref[2:3, :] * q
        yb = zb * cv
        rb = lax.rsqrt(jnp.mean(yb * yb, axis=-1, keepdims=True) + RMS_EPS)
        y_ref[:, a:2 * a] = (yb * rb * gg_ref[:, a:2 * a]).astype(BF16)

    full = lambda shape: pl.BlockSpec(shape, lambda i: (0,) * len(shape))
    outs, couts = _call_hosting(
        body, "mixer_fwd", (t // tm,), [_sds((t, 2 * a), BF16)],
        [tile, *prev, full((1, a)), full((1, a)), full(w_spatial.shape), full(bb.shape), full(conv_w.shape),
         full((1, 2 * a))],
        [pl.BlockSpec((tm, 2 * a), lambda i: (i, 0))], [z, z, z, ln_g, ln_b, w_spatial, bb, conv_w, gg],
        [pltpu.VMEM((tm, a), F32), pltpu.VMEM((tm, a), F32)], comm)
    return outs[0], couts


def _mixer_bwd(z, dy, ln_g, ln_b, w_spatial, bb, conv_w, gg, comm=None):
    t = z.shape[0]
    a = z.shape[1] // 5
    hd = a // HEADS
    tm = min(_TM_MIX, t)
    n_tiles = t // tm
    tile, prev, nxt = _mixer_specs(t, a, tm)
    hb = tm // HALO
    dy_tile = pl.BlockSpec((tm, 2 * a), lambda i: (i, 0))
    dy_next = pl.BlockSpec((HALO, a), lambda i: (jnp.minimum((i + 1) * hb, t // HALO - 1), 1))

    def body(z_ref, pc_ref, ph_ref, nb_ref, nc_ref, nh_ref, dy_ref, ndy_ref, lng_ref, lnb_ref, ws_ref, bb_ref, cw_ref,
             gg_ref, dz_ref, dlng_ref, dlnb_ref, dws_ref, dbb_ref, dcw_ref, dgg_ref, mixed_ref, vln_ref, dmix_ref,
             dvln_ref):
        i = pl.program_id(0)

        @pl.when(i == 0)
        def _():
            for ref in (dlng_ref, dlnb_ref, dws_ref, dbb_ref, dcw_ref, dgg_ref):
                ref[...] = jnp.zeros(ref.shape, F32)

        lng = lng_ref[...]
        zu = z_ref[:, 0:a].astype(F32)
        zv = z_ref[:, a:2 * a].astype(F32)
        u, thu, thv, rs, vhat = _group_a_fwd(zu, zv, lng, lnb_ref[...], ws_ref, bb_ref, mixed_ref, vln_ref)
        mixed = mixed_ref[...]
        ya = u * mixed
        ra = lax.rsqrt(jnp.mean(ya * ya, axis=-1, keepdims=True) + RMS_EPS)
        da = dy_ref[:, 0:a].astype(F32)
        yah = ya * ra
        dgg_ref[:, 0:a] += jnp.sum(da * yah, axis=0, keepdims=True)
        ga = da * gg_ref[:, 0:a]
        dya = ra * (ga - yah * jnp.mean(ga * yah, axis=-1, keepdims=True))
        dz_ref[:, 0:a] = (dya * mixed * _gelu_grad(zu, thu)).astype(BF16)
        dmix_ref[...] = dya * u
        for h in range(HEADS):
            w = _masked_ws(ws_ref, h).astype(BF16)
            dw = jnp.zeros((BLK, BLK), F32)
            db = jnp.zeros((BLK, hd), F32)
            for b in range(tm // BLK):
                rows, cols = pl.ds(b * BLK, BLK), pl.ds(h * hd, hd)
                dm = dmix_ref[rows, cols]
                dmb = dm.astype(BF16)
                db = db + dm
                dw = dw + lax.dot_general(dmb, vln_ref[rows, cols].astype(BF16), (NT, ((), ())),
                                          preferred_element_type=F32)
                dvln_ref[rows, cols] = lax.dot_general(w, dmb, (TN, ((), ())), preferred_element_type=F32)
            dws_ref[h] += dw
            dbb_ref[h] += db
        dvln = dvln_ref[...]
        dlng_ref[...] += jnp.sum(dvln * vhat, axis=0, keepdims=True)
        dlnb_ref[...] += jnp.sum(dvln, axis=0, keepdims=True)
        dvh = dvln * lng
        dv = rs * (dvh - jnp.mean(dvh, axis=-1, keepdims=True) - vhat * jnp.mean(dvh * vhat, axis=-1, keepdims=True))
        dz_ref[:, a:2 * a] = (dv * _gelu_grad(zv, thv)).astype(BF16)

        w0, w1, w2 = cw_ref[0:1, :], cw_ref[1:2, :], cw_ref[2:3, :]
        ggb = gg_ref[:, a:2 * a]
        zb = z_ref[:, 2 * a:3 * a].astype(F32)
        zc = z_ref[:, 3 * a:4 * a].astype(F32)
        zh = z_ref[:, 4 * a:5 * a].astype(F32)
        q = zc * zh
        qp = jnp.where(i > 0, pc_ref[...].astype(F32) * ph_ref[...].astype(F32), 0.0)
        qm1 = _shift_down(q, 1, [qp[HALO - 1:HALO]])
        qm2 = _shift_down(q, 2, [qp[HALO - 2:HALO - 1], qp[HALO - 1:HALO]])
        cv = w0 * qm2 + w1 * qm1 + w2 * q

        def conv_out_grad(zb_, cv_, dout_):
            yb = zb_ * cv_
            rb = lax.rsqrt(jnp.mean(yb * yb, axis=-1, keepdims=True) + RMS_EPS)
            ybh = yb * rb
            gb = dout_ * ggb
            dyb = rb * (gb - ybh * jnp.mean(gb * ybh, axis=-1, keepdims=True))
            return dyb * zb_, dyb * cv_, ybh

        db_out = dy_ref[:, a:2 * a].astype(F32)
        g, dzb, ybh = conv_out_grad(zb, cv, db_out)
        dgg_ref[:, a:2 * a] += jnp.sum(db_out * ybh, axis=0, keepdims=True)
        dz_ref[:, 2 * a:3 * a] = dzb.astype(BF16)
        qn = nc_ref[...].astype(F32) * nh_ref[...].astype(F32)
        zbn = nb_ref[...].astype(F32)
        cvn = w0 * _shift_down(qn, 2, [q[tm - 2:tm - 1], q[tm - 1:tm]]) + w1 * _shift_down(qn, 1, [q[tm - 1:tm]]) + w2 * qn
        gn, _, _ = conv_out_grad(zbn, cvn, ndy_ref[...].astype(F32))
        gn = jnp.where(i < n_tiles - 1, gn, 0.0)
        dq = w2 * g + w1 * _shift_up(g, 1, [gn[0:1]]) + w0 * _shift_up(g, 2, [gn[0:1], gn[1:2]])
        dz_ref[:, 3 * a:4 * a] = (dq * zh).astype(BF16)
        dz_ref[:, 4 * a:5 * a] = (dq * zc).astype(BF16)
        dcw_ref[0:1, :] += jnp.sum(g * qm2, axis=0, keepdims=True)
        dcw_ref[1:2, :] += jnp.sum(g * qm1, axis=0, keepdims=True)
        dcw_ref[2:3, :] += jnp.sum(g * q, axis=0, keepdims=True)

        @pl.when(i == n_tiles - 1)
        def _():
            for h in range(HEADS):
                dbb_ref[h] = jnp.broadcast_to(jnp.sum(dbb_ref[h], axis=1, keepdims=True), (BLK, hd))
                dws_ref[h] = _masked_ws(dws_ref, h)

    full = lambda shape: pl.BlockSpec(tuple(shape), lambda i: (0,) * len(shape))
    out_shapes = (_sds((t, 5 * a), BF16), _sds((1, a), F32), _sds((1, a), F32), _sds(w_spatial.shape, F32),
                  _sds(bb.shape, F32), _sds((8, a), F32), _sds((1, 2 * a), F32))
    return _call_hosting(
        body, "mixer_bwd", (n_tiles,), out_shapes,
        [tile, *prev, *nxt, dy_tile, dy_next, full((1, a)), full((1, a)), full(w_spatial.shape), full(bb.shape),
         full(conv_w.shape), full((1, 2 * a))],
        [tile, *[full(s.shape) for s in out_shapes[1:]]], [z, z, z, z, z, z, dy, dy, ln_g, ln_b, w_spatial, bb, conv_w, gg],
        [pltpu.VMEM((tm, a), F32)] * 4, comm)


def _all_reduce_small(pack, comm=None):
    r = pack.shape[0]
    hosted = _Hosted(comm, 1, 1)
    n_ci, n_co = len(hosted.operands), len(hosted.out_shapes)

    def body(*refs):
        in_ref, c_ins, out_ref, c_outs = refs[0], refs[1:1 + n_ci], refs[1 + n_ci], refs[2 + n_ci:2 + n_ci + n_co]
        acc_ref, recv_ref, send_sems, recv_sems = refs[2 + n_ci + n_co:6 + n_ci + n_co]
        sems = refs[6 + n_ci + n_co:]
        hosted.run("start", c_ins, c_outs, sems)
        x, y, c = _place()
        partners = [(x, y, 1 - c), (1 - x, y, c), (x, 1 - y, c)]
        acc_ref[0] = in_ref[...]
        for s, partner in enumerate(partners):
            cp = pltpu.make_async_remote_copy(
                src_ref=acc_ref.at[s], dst_ref=recv_ref.at[s], send_sem=send_sems.at[s], recv_sem=recv_sems.at[s],
                device_id=partner, device_id_type=MESH)
            cp.start()
            cp.wait()
            if s < 2:
                acc_ref[s + 1] = acc_ref[s] + recv_ref[s]
            else:
                out_ref[...] = acc_ref[s] + recv_ref[s]
        for stage in ("mid1", "mid2", "finish"):
            hosted.run(stage, c_ins, c_outs, sems)

    vmem = pl.BlockSpec(memory_space=pltpu.VMEM)
    res = pl.pallas_call(
        body, name="all_reduce_small", out_shape=tuple([_sds(pack.shape, F32)] + hosted.out_shapes),
        in_specs=[vmem] + hosted.in_specs, out_specs=tuple([vmem] + hosted.out_specs),
        input_output_aliases=hosted.aliases,
        scratch_shapes=[pltpu.VMEM((3, r, 128), F32), pltpu.VMEM((3, r, 128), F32), pltpu.SemaphoreType.DMA((3,)),
                        pltpu.SemaphoreType.DMA((3,))] + hosted.scratch,
        compiler_params=pltpu.CompilerParams(vmem_limit_bytes=VMEM_LIMIT_V7X),
    )(pack, *hosted.operands)
    return res[0], list(res[1:])


def _adamw_math(w, g, m, v):
    m = ADAM_B1 * m + (1.0 - ADAM_B1) * g
    v = ADAM_B2 * v + (1.0 - ADAM_B2) * (g * g)
    m_hat = m / (1.0 - ADAM_B1 ** ADAM_STEP)
    v_hat = v / (1.0 - ADAM_B2 ** ADAM_STEP)
    delta = -ADAM_LR * (m_hat / (jnp.sqrt(v_hat) + ADAM_EPS) + ADAM_WD * w)
    return delta, m, v


def _adamw_big(name, land, w, m, v, comm=None):
    nl, n_slots, r, c = land.shape
    tr = max(8, min(r, (256 * 640) // c // 8 * 8))
    while r % tr:
        tr -= 8
    grid = (nl, r // tr)
    hosted = _Hosted(comm, 4, 4)
    n_ci, n_co = len(hosted.operands), len(hosted.out_shapes)

    def body(*refs):
        land_ref, w_ref, m_ref, v_ref = refs[:4]
        c_ins = refs[4:4 + n_ci]
        g_out, d_out, m_out, v_out = refs[4 + n_ci:8 + n_ci]
        c_outs = refs[8 + n_ci:8 + n_ci + n_co]
        sems = refs[8 + n_ci + n_co:]

        def compute():
            g = land_ref[0].astype(F32)
            for s in range(1, n_slots):
                g = g + land_ref[s].astype(F32)
            delta, mn, vn = _adamw_math(w_ref[...], g, m_ref[...], v_ref[...])
            g_out[...] = g
            d_out[...] = delta
            m_out[...] = mn
            v_out[...] = vn

        hosted.wrap(grid, compute, c_ins, c_outs, sems)

    blk = pl.BlockSpec((None, tr, c), lambda l, i: (l, i, 0))
    res = pl.pallas_call(
        body, name=name, grid=grid, out_shape=tuple([_sds((nl, r, c), F32)] * 4 + hosted.out_shapes),
        in_specs=[pl.BlockSpec((None, n_slots, tr, c), lambda l, i: (l, 0, i, 0)), blk, blk, blk] + hosted.in_specs,
        out_specs=tuple([blk] * 4 + hosted.out_specs), input_output_aliases=hosted.aliases,
        scratch_shapes=hosted.scratch, compiler_params=_cparams(2))(land, w, m, v, *hosted.operands)
    return list(res[:4]), list(res[4:])


def _adamw_small(gs, ws, ms, vs):
    n = len(gs)

    def body(*refs):
        g_refs, w_refs, m_refs, v_refs = refs[:n], refs[n:2 * n], refs[2 * n:3 * n], refs[3 * n:4 * n]
        d_outs, m_outs, v_outs = refs[4 * n:5 * n], refs[5 * n:6 * n], refs[6 * n:7 * n]
        for i in range(n):
            delta, mn, vn = _adamw_math(w_refs[i][...], g_refs[i][...], m_refs[i][...], v_refs[i][...])
            d_outs[i][...] = delta
            m_outs[i][...] = mn
            v_outs[i][...] = vn

    shapes = [_sds(g.shape, F32) for g in gs]
    res = pl.pallas_call(body, name="adamw_small", out_shape=tuple(shapes * 3),
                         compiler_params=pltpu.CompilerParams(vmem_limit_bytes=VMEM_LIMIT_V7X))(*gs, *ws, *ms, *vs)
    return list(res[:n]), list(res[n:2 * n]), list(res[2 * n:])


def _rows(a):
    return a.reshape(-1, 128)


BIG = ["w_in", "w_out", "w_gate", "w_up", "w_down"]
AG_HOSTS = {
    ("norm1", 0): [("w_in", 0), ("conv_w", 0), ("w_out", 0)],
    ("mm_in", 0): [("w_gate", 0)], ("mixer", 0): [("w_up", 0, 0, 2)], ("mm_out", 0): [("w_up", 0, 1, 2)],
    ("mm_swiglu", 0): [("w_down", 0), ("w_in", 1), ("w_out", 1)], ("mm_down", 0): [("w_gate", 1)],
    ("mm_in", 1): [("w_up", 1)], ("mm_swiglu", 1): [("w_down", 1)],
}


def kernel(x, norm1_g, w_in, gmlp_ln_g, gmlp_ln_b, w_spatial, b_spatial, conv_w, group_norm_g, w_out, norm2_g, w_gate, w_up, w_down, final_norm_g, loss_target, m_norm1_g, m_w_in, m_gmlp_ln_g, m_gmlp_ln_b, m_w_spatial, m_b_spatial, m_conv_w, m_group_norm_g, m_w_out, m_norm2_g, m_w_gate, m_w_up, m_w_down, m_final_norm_g, v_norm1_g, v_w_in, v_gmlp_ln_g, v_gmlp_ln_b, v_w_spatial, v_b_spatial, v_conv_w, v_group_norm_g, v_w_out, v_norm2_g, v_w_gate, v_w_up, v_w_down, v_final_norm_g):
    nl = N_LAYERS
    t, d = x.shape[1], x.shape[2]
    a = d // 2
    hd = a // HEADS
    xin = x.reshape(t, d)
    target = loss_target.reshape(t, d)
    me = _index(_place())

    tr = lambda w: jnp.transpose(w, (0, 2, 1))
    big = {"w_in": w_in, "w_out": w_out, "w_gate": tr(w_gate), "w_up": tr(w_up), "w_down": w_down}
    big_m = {"w_in": m_w_in, "w_out": m_w_out, "w_gate": tr(m_w_gate), "w_up": tr(m_w_up), "w_down": m_w_down}
    big_v = {"w_in": v_w_in, "w_out": v_w_out, "w_gate": tr(v_w_gate), "w_up": tr(v_w_up), "w_down": v_w_down}
    block = {k: big[k].shape[1:] for k in BIG}
    view = {k: _cols_view(block[k][1]) if k == "w_in" else _rows_view(block[k][0]) for k in BIG}
    full_shape = {k: (block[k][0], N_DEV * block[k][1]) if k == "w_in" else (N_DEV * block[k][0], block[k][1])
                  for k in BIG}

    weights = {}
    shards = {(k, l): big[k][l].astype(BF16) for k in BIG for l in range(nl)}

    def ag_spec(k, l, part=0, n_parts=1):
        if k == "conv_w":
            return (conv_w, _sds((N_DEV, *conv_w.shape), F32), _SLOT_WHOLE, (0,), None)
        halves = (_cols_halves(*block[k], part, n_parts) if k == "w_in" else _rows_halves(block[k][0], part, n_parts))
        return (shards[(k, l)], _sds(full_shape[k], BF16), halves, (0, 1), weights.get((k, l)))

    bb = jnp.broadcast_to(b_spatial[..., None], (nl, HEADS, BLK, hd))

    def hosted(name, l):
        keys = AG_HOSTS.get((name, l), [])
        return keys, ([_ag_piece([ag_spec(*key) for key in keys])] if keys else None)

    def landed(keys, couts):
        for key, arr in zip(keys, couts):
            weights[key[:2]] = arr

    saved = []
    xl = xin
    for l in range(nl):
        keys, comm = hosted("norm1", l)
        h, couts = _rmsnorm_fwd(xl, norm1_g[l:l + 1], comm)
        landed(keys, couts)
        if l == 0:
            conv_full = jnp.transpose(weights[("conv_w", 0)], (1, 2, 0, 3)).reshape(nl, 3, a)
        keys, comm = hosted("mm_in", l)
        z, couts = _mm_in(h, weights[("w_in", l)], comm)
        landed(keys, couts)
        keys, comm = hosted("mixer", l)
        y, couts = _mixer_fwd(z, gmlp_ln_g[l:l + 1], gmlp_ln_b[l:l + 1], w_spatial[l], bb[l], conv_full[l],
                              group_norm_g[l:l + 1], comm)
        landed(keys, couts)
        keys, comm = hosted("mm_out", l)
        x1, couts = _mm_out(y, weights[("w_out", l)], xl, comm)
        landed(keys, couts)
        keys, comm = hosted("norm2", l)
        h2, couts = _rmsnorm_fwd(x1, norm2_g[l:l + 1], comm)
        landed(keys, couts)
        keys, comm = hosted("mm_swiglu", l)
        (act, dact_dgate, dact_dup), couts = _mm_swiglu(h2, weights[("w_gate", l)], weights[("w_up", l)], comm)
        landed(keys, couts)
        keys, comm = hosted("mm_down", l)
        x2, couts = _mm_down(act, weights[("w_down", l)], x1, comm)
        landed(keys, couts)
        saved.append(dict(x=xl, h=h, z=z, y=y, x1=x1, h2=h2, dact_dgate=dact_dgate, dact_dup=dact_dup, act=act))
        xl = x2

    dx, dxb, d_final_g, loss_part = _loss_head(xl, final_norm_g.reshape(1, d), target)
    small = [None] * nl
    core = lax.axis_index("c").astype(jnp.int32).reshape(1)
    in_rows = block["w_in"][0]
    part_of = {"w_in_a": ("w_in", 0), "w_in_b": ("w_in", 7 * in_rows // 8)}
    block["w_in_a"], block["w_in_b"] = (7 * in_rows // 8, block["w_in"][1]), (in_rows // 8, block["w_in"][1])
    for k in part_of:
        view[k] = view["w_in"]
    stage_shape = {k: _sds((N_CHIPS, *block[k]), BF16) for k in block}
    land_shape = {k: _sds((nl, N_CHIPS, *block[k]), BF16) for k in BIG}
    grads = [dict() for _ in range(nl)]
    stages = [dict() for _ in range(nl)]
    sums = [dict() for _ in range(nl)]
    lands = {k: None for k in BIG}

    def core_job(l, keys):
        def sink(outs):
            stages[l].update(zip(keys, outs))
        return _rs_core_piece([(grads[l][k], stage_shape[k], view[k]) for k in keys]), sink

    def chip_job(l, items):
        keys = [part_of.get(item[0], (item[0], 0))[0] for item in items]

        def rows(k, p0, p1, n_parts):
            per = block[k][0] // n_parts
            landing = part_of.get(k, (k, 0))[1]
            return (p0 * per, landing + p0 * per, (p1 - p0) * per)

        def sink(outs):
            lands.update(zip(keys, outs))
        return _rs_chip_piece([(sums[l][k], land_shape[key], rows(k, p0, p1, n_parts), lands[key])
                               for key, (k, p0, p1, n_parts) in zip(keys, items)], l), sink

    def add_up(l, keys):
        for k in keys:
            sums[l][k] = _chip_sums(f"chip_sums_{k}", grads[l][k], stages[l][k], k.startswith("w_in"), core)

    def host(*jobs):
        def deliver(couts):
            i = 0
            for piece, sink in jobs:
                n_out = len(piece.out_shapes)
                sink(couts[i:i + n_out])
                i += n_out
        return [piece for piece, _ in jobs], deliver

    whole = lambda k: (k, 0, 1, 1)
    rep = ["norm1_g", "gmlp_ln_g", "gmlp_ln_b", "w_spatial", "b_spatial", "group_norm_g", "norm2_g"]
    rep_w = dict(norm1_g=norm1_g, gmlp_ln_g=gmlp_ln_g, gmlp_ln_b=gmlp_ln_b, w_spatial=w_spatial, b_spatial=b_spatial,
                 group_norm_g=group_norm_g, norm2_g=norm2_g)
    rep_m = dict(norm1_g=m_norm1_g, gmlp_ln_g=m_gmlp_ln_g, gmlp_ln_b=m_gmlp_ln_b, w_spatial=m_w_spatial,
                 b_spatial=m_b_spatial, group_norm_g=m_group_norm_g, norm2_g=m_norm2_g)
    rep_v = dict(norm1_g=v_norm1_g, gmlp_ln_g=v_gmlp_ln_g, gmlp_ln_b=v_gmlp_ln_b, w_spatial=v_w_spatial,
                 b_spatial=v_b_spatial, group_norm_g=v_group_norm_g, norm2_g=v_norm2_g)

    def small_grad_parts():
        parts = [_rows(jnp.stack([small[l][k].reshape(rep_w[k].shape[1:]) for l in range(nl)])) for k in rep]
        parts.append(_rows(d_final_g))
        parts.append(_rows(jnp.stack([small[l]["conv_w"] for l in range(nl)])))
        parts.append(jnp.broadcast_to(loss_part, (8, 128)))
        rows = sum(p.shape[0] for p in parts)
        parts.append(jnp.zeros((-rows % 16, 128), F32))
        return parts

    for l in reversed(range(nl)):
        s = saved[l]
        wi, wo, wgt, wut, wd = [weights[(k, l)] for k in BIG]
        later = l + 1 < nl
        comm, deliver = host(chip_job(l + 1, [("w_in", 0, 1, 2)])) if later else host()
        (grads[l]["w_down"],), couts = _mm_dw("mm_dw_down", [s["act"]], dxb, 2816, 1024, comm)
        deliver(couts)
        comm, deliver = (host(core_job(l, ["w_down"]), chip_job(l + 1, [("w_in", 1, 2, 2)])) if later
                         else host(core_job(l, ["w_down"])))
        (dgate, dup), couts = _mm_dact(dxb, wd, s["dact_dgate"], s["dact_dup"], comm)
        deliver(couts)
        add_up(l, ["w_down"])
        comm, deliver = host(chip_job(l, [("w_down", 0, 3, 4)]))
        (grads[l]["w_gate"],), couts = _mm_dw("mm_dw_gate", [dgate], s["h2"], 2816, 1024, comm)
        deliver(couts)
        comm, deliver = host(chip_job(l, [("w_down", 3, 4, 4)]), core_job(l, ["w_gate"]))
        (grads[l]["w_up"],), couts = _mm_dw("mm_dw_up", [dup], s["h2"], 2816, 1024, comm)
        deliver(couts)
        add_up(l, ["w_gate"])
        comm, deliver = host(chip_job(l, [whole("w_gate")]), core_job(l, ["w_up"]))
        dh2, couts = _mm_dh2(dgate, dup, wgt, wut, comm)
        deliver(couts)
        add_up(l, ["w_up"])
        dx1, dx1b, d_n2 = _rmsnorm_bwd(s["x1"], norm2_g[l:l + 1], dh2, dx)
        comm, deliver = host(chip_job(l, [("w_up", 0, 1, 4)]))
        dy, couts = _mm_dy(dx1b, wo, comm)
        deliver(couts)
        comm, deliver = host(chip_job(l, [("w_up", 1, 2, 4)]))
        (grads[l]["w_out"],), couts = _mm_dw("mm_dw_out", [s["y"]], dx1b, 1024, 1024, comm)
        deliver(couts)
        comm, deliver = host(chip_job(l, [("w_up", 2, 4, 4)]), core_job(l, ["w_out"]))
        (dz, d_lng, d_lnb, d_ws, d_bb, d_cw, d_gg), couts = _mixer_bwd(
            s["z"], dy, gmlp_ln_g[l:l + 1], gmlp_ln_b[l:l + 1], w_spatial[l], bb[l], conv_full[l], group_norm_g[l:l + 1],
            comm)
        deliver(couts)
        add_up(l, ["w_out"])
        small[l] = dict(norm1_g=jnp.zeros((1, d), F32), gmlp_ln_g=d_lng, gmlp_ln_b=d_lnb, w_spatial=d_ws,
                        b_spatial=d_bb[:, :, 0], group_norm_g=d_gg, norm2_g=d_n2, conv_w=d_cw[0:3])
        if l > 0:
            comm, deliver = host(chip_job(l, [whole("w_out")]))
            (grads[l]["w_in"],), couts = _mm_dw("mm_dw_in", [s["h"]], dz, 2048, 1024, comm)
            deliver(couts)
            comm, deliver = host(core_job(l, ["w_in"]))
            dh, couts = _mm_dh(dz, wi, comm)
            deliver(couts)
            add_up(l, ["w_in"])
        else:
            parts = small_grad_parts()
            reduced = []
            comm, deliver = host(chip_job(l, [whole("w_out")]),
                                 (_all_reduce_piece(jnp.concatenate(parts, axis=0)), reduced.extend))
            (grads[l]["w_in_a"],), couts = _mm_dw("mm_dw_in_a", [s["h"]], dz, block["w_in_a"][0], 1024, comm,
                                                  m_rows=(0, block["w_in_a"][0]))
            deliver(couts)
            comm, deliver = host(core_job(l, ["w_in_a"]))
            (grads[l]["w_in_b"],), couts = _mm_dw("mm_dw_in_b", [s["h"]], dz, block["w_in_b"][0], 2560, comm,
                                                  m_rows=(block["w_in_a"][0], block["w_in_b"][0]))
            deliver(couts)
            add_up(l, ["w_in_a"])
            comm, deliver = host(chip_job(l, [whole("w_in_a")]), core_job(l, ["w_in_b"]))
            dh, couts = _mm_dh(dz, wi, comm)
            deliver(couts)
            add_up(l, ["w_in_b"])
        dx, dxb, small[l]["norm1_g"] = _rmsnorm_bwd(s["x"], norm1_g[l:l + 1], dh, dx1, with_bf16=l > 0)
    grad_x = dx.reshape(x.shape)

    sizes = [p.shape[0] for p in parts]
    comm, deliver = host(chip_job(0, [whole("w_in_b")]))
    last, couts = _all_reduce_small(_rows(small[0]["norm1_g"]), comm)
    deliver(couts)
    total = lax.dynamic_update_slice(reduced[0], last, (0, 0))
    offs = [0]
    for n in sizes:
        offs.append(offs[-1] + n)
    pieces = [total[offs[i]:offs[i + 1]] for i in range(len(parts))]
    loss = pieces[len(rep) + 2][0, 0]
    conv_g_full = pieces[len(rep) + 1].reshape(nl, 3, N_DEV, a // N_DEV)
    conv_g = lax.dynamic_index_in_dim(conv_g_full, me, axis=2, keepdims=False)
    names = rep + ["final_norm_g", "conv_w"]
    flat = lambda w: w.reshape(-1, w.shape[-1])
    small_w = [flat(rep_w[k]) for k in rep] + [flat(final_norm_g), flat(conv_w)]
    small_m = [flat(rep_m[k]) for k in rep] + [flat(m_final_norm_g), flat(m_conv_w)]
    small_v = [flat(rep_v[k]) for k in rep] + [flat(v_final_norm_g), flat(v_conv_w)]
    small_g = [pieces[i].reshape(small_w[i].shape) for i in range(len(rep) + 1)] + [flat(conv_g)]
    small_d, small_m, small_v = _adamw_small(small_g, small_w, small_m, small_v)
    shape_of = dict(rep_w, final_norm_g=final_norm_g, conv_w=conv_w)
    named = lambda arrays: {k: arr.reshape(shape_of[k].shape) for k, arr in zip(names, arrays)}
    res = {"grad": named(small_g), "delta": named(small_d), "m": named(small_m), "v": named(small_v)}

    for k in BIG:
        outs, _ = _adamw_big(f"adamw_{k}", lands[k], big[k], big_m[k], big_v[k])
        if k in ("w_gate", "w_up"):
            outs = [tr(o) for o in outs]
        res["grad"][k], res["delta"][k], res["m"][k], res["v"][k] = outs

    order = ["norm1_g", "w_in", "gmlp_ln_g", "gmlp_ln_b", "w_spatial", "b_spatial", "conv_w", "group_norm_g", "w_out",
             "norm2_g", "w_gate", "w_up", "w_down", "final_norm_g"]
    return (loss, grad_x, *[res["grad"][k] for k in order], *[res["delta"][k] for k in order],
            *[res["m"][k] for k in order], *[res["v"][k] for k in order])
```

```python
import functools
import math
import operator

import jax
import jax.numpy as jnp
from jax import lax
from jax.experimental import pallas as pl
from jax.experimental.pallas import tpu as pltpu

F32 = jnp.float32
BF16 = jnp.bfloat16
MESH = pl.DeviceIdType.MESH

N_DEV = 8
N_LAYERS = 2
HEADS = 8
BLK = 128
CHUNK = 64
HALO = 16
RMS_EPS = 1e-6
LN_EPS = 1e-5
ADAM_LR, ADAM_B1, ADAM_B2, ADAM_EPS, ADAM_WD, ADAM_STEP = 0.001, 0.9, 0.999, 1e-8, 0.01, 10
GELU_C = math.sqrt(2.0 / math.pi)
GELU_A = 0.044715

VMEM_LIMIT_V7X = 56 * 1024 * 1024
_TM = 1024
_TN = 1024
_TT = 1024
_TM_MIX = 256
_TM_NORM = 512


def _cparams(n_axes):
    return pltpu.CompilerParams(dimension_semantics=("arbitrary",) * n_axes, vmem_limit_bytes=VMEM_LIMIT_V7X)


def _sds(shape, dtype):
    return jax.ShapeDtypeStruct(tuple(shape), dtype)


def _place():
    return lax.axis_index("x"), lax.axis_index("y"), lax.axis_index("c")


def _index(place):
    return 4 * place[0] + 2 * place[1] + place[2]


class _Piece:
    def __init__(self, operands, out_shapes, aliases, n_sems, start, finish, mid1=None, mid2=None, vmem=(),
                 hooks=(0.6, 0.87)):
        self.operands, self.out_shapes, self.aliases, self.n_sems = list(operands), list(out_shapes), dict(aliases), n_sems
        self.vmem = list(vmem)
        self.hooks = hooks
        nothing = lambda ctx: None
        self.start, self.mid1, self.mid2, self.finish = start, mid1 or nothing, mid2 or nothing, finish


class _Ctx:
    def __init__(self, ins, outs, sems, offs):
        self.ins, self.outs, self.sems = ins, outs, sems
        self.o_in, self.o_out, self.o_send, self.o_recv, self.o_loc, self.o_vmem = offs

    def vmem(self, i):
        return self.sems[3 + self.o_vmem + i]

    def inp(self, i):
        return self.ins[self.o_in + i]

    def out(self, i):
        return self.outs[self.o_out + i]

    def send(self, k):
        return self.sems[0].at[self.o_send + k]

    def recv(self, k):
        return self.sems[1].at[self.o_recv + k]

    def local(self, k):
        return self.sems[2].at[self.o_loc + k]


class _Hosted:
    def __init__(self, pieces, n_in_before, n_out_before):
        self.pieces = [p for p in (pieces or []) if p is not None]
        self.operands, self.out_shapes, self.aliases, self.offs = [], [], {}, []
        counts, vmem = [0, 0, 0], []
        for p in self.pieces:
            self.offs.append((len(self.operands), len(self.out_shapes), *counts, len(vmem)))
            for i, j in p.aliases.items():
                self.aliases[n_in_before + len(self.operands) + i] = n_out_before + len(self.out_shapes) + j
            self.operands += p.operands
            self.out_shapes += p.out_shapes
            counts = [c + n for c, n in zip(counts, p.n_sems)]
            vmem += p.vmem
        hbm = pl.BlockSpec(memory_space=pl.ANY)
        self.in_specs = [hbm] * len(self.operands)
        self.out_specs = [hbm] * len(self.out_shapes)
        self.scratch = ([pltpu.SemaphoreType.DMA((max(c, 1),)) for c in counts] + vmem) if self.pieces else []

    def run(self, stage, ins, outs, sems):
        for p, offs in zip(self.pieces, self.offs):
            getattr(p, stage)(_Ctx(ins, outs, sems, offs))

    def wrap(self, grid, compute, ins, outs, sems):
        if not self.pieces:
            compute()
            return
        n_steps = math.prod(grid)
        lin = 0
        for ax, g in enumerate(grid):
            lin = lin * g + pl.program_id(ax)
        pl.when(lin == 0)(lambda: self.run("start", ins, outs, sems))
        compute()
        for stage, which in (("mid1", 0), ("mid2", 1)):
            for p, offs in zip(self.pieces, self.offs):
                at = min(n_steps - 1, int(p.hooks[which] * n_steps))
                pl.when(lin == at)(functools.partial(getattr(p, stage), _Ctx(ins, outs, sems, offs)))
        pl.when(lin == n_steps - 1)(lambda: self.run("finish", ins, outs, sems))


def _cols_view(width):
    return lambda ref, p: ref.at[:, pl.ds(pl.multiple_of(p * width, 128), width)]


def _rows_view(height):
    return lambda ref, p: ref.at[pl.ds(pl.multiple_of(p * height, 16), height), :]


def _cols_halves(rows, width, part, n_parts):
    hr = rows // n_parts // 2
    at = lambda h: pl.ds(part * 2 * hr + h * hr, hr)
    return (lambda ref, p, h: ref.at[at(h), pl.ds(pl.multiple_of(p * width, 128), width)],
            lambda ref, h: ref.at[at(h), :], 2)


def _rows_halves(height, part, n_parts):
    hh = height // n_parts // 2
    return (lambda ref, p, h: ref.at[pl.ds(pl.multiple_of(p * height + part * 2 * hh + h * hh, 16), hh), :],
            lambda ref, h: ref.at[pl.ds(part * 2 * hh + h * hh, hh), :], 2)


_SLOT_WHOLE = (lambda ref, p, h: ref.at[p], lambda ref, h: ref, 1)


def _ag_piece(specs):
    units = [(a, h) for a, s in enumerate(specs) for h in s[3]]

    def plan(ctx):
        x, y, c = _place()
        me, sib, xn, yn, dg = (x, y, c), (x, y, 1 - c), (1 - x, y, c), (x, 1 - y, c), (1 - x, 1 - y, c)

        def copy(u, k, block, to, from_shard=False):
            a, h = units[u]
            dst_of, src_of, _ = specs[a][2]
            dst = dst_of(ctx.out(a), _index(block), h)
            return pltpu.make_async_remote_copy(
                src_ref=src_of(ctx.inp(a), h) if from_shard else dst, dst_ref=dst, send_sem=ctx.send(7 * u + k),
                recv_sem=ctx.recv(7 * u + k), device_id=to, device_id_type=MESH)

        def local(u):
            a, h = units[u]
            dst_of, src_of, _ = specs[a][2]
            return pltpu.make_async_copy(src_of(ctx.inp(a), h), dst_of(ctx.out(a), _index(me), h), ctx.local(u))

        def relay(u):
            return copy(u, 3, xn, yn) if units[u][1] % 2 == 0 else copy(u, 3, yn, xn)

        return me, sib, xn, yn, dg, c, copy, local, relay

    def start(ctx):
        me, sib, xn, yn, dg, c, copy, local, relay = plan(ctx)
        for u in range(len(units)):
            local(u).start()
            for k, to in enumerate((sib, xn, yn)):
                copy(u, k, me, to, from_shard=True).start()

    def mid1(ctx):
        me, sib, xn, yn, dg, c, copy, local, relay = plan(ctx)
        for u in range(len(units)):
            copy(u, 1, xn, me).wait_recv()
            copy(u, 2, yn, me).wait_recv()
            relay(u).start()
            copy(u, 4, xn, sib).start()
            copy(u, 5, yn, sib).start()

    def mid2(ctx):
        me, sib, xn, yn, dg, c, copy, local, relay = plan(ctx)
        for u in range(len(units)):
            copy(u, 3, dg, me).wait_recv()
            copy(u, 6, dg, sib).start()

    def finish(ctx):
        me, sib, xn, yn, dg, c, copy, local, relay = plan(ctx)
        other = lambda place: (place[0], place[1], 1 - c)
        for u in range(len(units)):
            for k, block in ((0, sib), (4, other(xn)), (5, other(yn)), (6, other(dg))):
                copy(u, k, block, me).wait_recv()
        for u in range(len(units)):
            for k, to in enumerate((sib, xn, yn)):
                copy(u, k, me, to, from_shard=True).wait_send()
            relay(u).wait_send()
            for k, block in ((4, xn), (5, yn), (6, dg)):
                copy(u, k, block, sib).wait_send()
            local(u).wait()

    n_u = len(units)
    operands, aliases = [s[0] for s in specs], {}
    for a, spec in enumerate(specs):
        if spec[4] is not None:
            aliases[len(operands)] = a
            operands.append(spec[4])
    return _Piece(operands, [s[1] for s in specs], aliases, (7 * n_u, 7 * n_u, n_u), start, finish, mid1, mid2)


N_CHIPS = 4


def _rs_core_piece(specs):
    n = len(specs)

    def copies(ctx):
        x, y, c = _place()
        out = []
        for a in range(n):
            for q in range(N_CHIPS):
                out.append(pltpu.make_async_remote_copy(
                    src_ref=specs[a][2](ctx.inp(a), 2 * q + (1 - c)), dst_ref=ctx.out(a).at[q],
                    send_sem=ctx.send(N_CHIPS * a + q), recv_sem=ctx.recv(N_CHIPS * a + q), device_id=(x, y, 1 - c),
                    device_id_type=MESH))
        return out

    def start(ctx):
        for cp in copies(ctx):
            cp.start()

    def finish(ctx):
        for cp in copies(ctx):
            cp.wait_recv()
            cp.wait_send()

    return _Piece([s[0] for s in specs], [s[1] for s in specs], {}, (N_CHIPS * n, N_CHIPS * n, 0), start, finish)


def _rs_chip_piece(specs, layer):
    n = len(specs)
    hops = [(1, 0), (0, 1), (1, 1)]

    def copies(ctx):
        x, y, c = _place()
        mine = 2 * x + y
        out = []
        for a in range(n):
            first, landing, size = specs[a][2]
            rows, to = pl.ds(first, size), pl.ds(landing, size)
            sums, land = ctx.inp(a), ctx.out(a)
            out.append((pltpu.make_async_copy(sums.at[mine, rows], land.at[layer, mine, to], ctx.local(a)), None))
            for j, (dx, dy) in enumerate(hops):
                px, py = x ^ dx, y ^ dy
                peer = 2 * px + py
                send = pltpu.make_async_remote_copy(
                    src_ref=sums.at[peer, rows], dst_ref=land.at[layer, mine, to], send_sem=ctx.send(3 * a + j),
                    recv_sem=ctx.recv(3 * a + j), device_id=(px, py, c), device_id_type=MESH)
                recv = pltpu.make_async_remote_copy(
                    src_ref=sums.at[peer, rows], dst_ref=land.at[layer, peer, to], send_sem=ctx.send(3 * a + j),
                    recv_sem=ctx.recv(3 * a + j), device_id=(px, py, c), device_id_type=MESH)
                out.append((send, recv))
        return out

    def start(ctx):
        for send, _ in copies(ctx):
            send.start()

    def finish(ctx):
        for send, recv in copies(ctx):
            if recv is None:
                send.wait()
            else:
                recv.wait_recv()
                send.wait_send()

    operands, aliases = [s[0] for s in specs], {}
    for a, spec in enumerate(specs):
        if spec[3] is not None:
            aliases[len(operands)] = a
            operands.append(spec[3])
    return _Piece(operands, [s[1] for s in specs], aliases, (3 * n, 3 * n, n), start, finish)


def _all_reduce_piece(pack):
    r = pack.shape[0]
    half = r // 2

    def plan(ctx):
        x, y, c = _place()
        acc, got = ctx.vmem(0), ctx.vmem(1)
        mine = pl.ds(pl.multiple_of(c * half, 8), half)
        sib = (x, y, 1 - c)
        copies = [
            pltpu.make_async_remote_copy(src_ref=acc.at[0], dst_ref=got.at[0], send_sem=ctx.send(0), recv_sem=ctx.recv(0),
                                         device_id=sib, device_id_type=MESH),
            pltpu.make_async_remote_copy(src_ref=acc.at[1, mine], dst_ref=got.at[1, mine], send_sem=ctx.send(1),
                                         recv_sem=ctx.recv(1), device_id=(1 - x, y, c), device_id_type=MESH),
            pltpu.make_async_remote_copy(src_ref=acc.at[2, mine], dst_ref=got.at[2, mine], send_sem=ctx.send(2),
                                         recv_sem=ctx.recv(2), device_id=(x, 1 - y, c), device_id_type=MESH),
            pltpu.make_async_remote_copy(src_ref=acc.at[3, mine], dst_ref=acc.at[3, mine], send_sem=ctx.send(3),
                                         recv_sem=ctx.recv(3), device_id=sib, device_id_type=MESH),
        ]
        other = pl.ds(pl.multiple_of((1 - c) * half, 8), half)
        arrival = pltpu.make_async_remote_copy(src_ref=acc.at[3, other], dst_ref=acc.at[3, other], send_sem=ctx.send(3),
                                               recv_sem=ctx.recv(3), device_id=sib, device_id_type=MESH)
        return acc, got, mine, copies, arrival

    def start(ctx):
        acc, got, mine, copies, arrival = plan(ctx)
        load = pltpu.make_async_copy(ctx.inp(0), acc.at[0], ctx.local(0))
        load.start()
        load.wait()
        copies[0].start()

    def mid1(ctx):
        acc, got, mine, copies, arrival = plan(ctx)
        copies[0].wait()
        acc[1] = acc[0] + got[0]
        copies[1].start()

    def mid2(ctx):
        acc, got, mine, copies, arrival = plan(ctx)
        copies[1].wait()
        acc[2, mine] = acc[1, mine] + got[1, mine]
        copies[2].start()

    def finish(ctx):
        acc, got, mine, copies, arrival = plan(ctx)
        copies[2].wait()
        acc[3, mine] = acc[2, mine] + got[2, mine]
        copies[3].start()
        copies[3].wait_send()
        arrival.wait_recv()
        store = pltpu.make_async_copy(acc.at[3], ctx.out(0), ctx.local(0))
        store.start()
        store.wait()

    return _Piece([pack], [_sds(pack.shape, F32)], {}, (4, 4, 1), start, finish, mid1, mid2,
                  vmem=[pltpu.VMEM((4, r, 128), F32), pltpu.VMEM((3, r, 128), F32)], hooks=(0.25, 0.6))


def _chip_sums(name, grad, stage, by_cols, core):
    _, r, c = stage.shape
    tr = r
    while tr * c > 1024 * 1024 or r % tr or tr % 16:
        tr -= 16
    n_t = r // tr

    def body(core_ref, g_ref, s_ref, o_ref):
        o_ref[...] = (g_ref[...].astype(F32) + s_ref[...].astype(F32)).astype(BF16)

    if by_cols:
        gspec = pl.BlockSpec((tr, c), lambda q, i, core_ref: (i, 2 * q + core_ref[0]))
    else:
        gspec = pl.BlockSpec((tr, c), lambda q, i, core_ref: ((2 * q + core_ref[0]) * n_t + i, 0))
    sspec = pl.BlockSpec((None, tr, c), lambda q, i, core_ref: (q, i, 0))
    return pl.pallas_call(
        body, name=name, out_shape=_sds(stage.shape, BF16),
        grid_spec=pltpu.PrefetchScalarGridSpec(num_scalar_prefetch=1, grid=(N_CHIPS, n_t), in_specs=[gspec, sspec],
                                               out_specs=sspec),
        compiler_params=_cparams(2))(core, grad, stage)


def _call_hosting(body, name, grid, out_shapes, in_specs, out_specs, operands, scratch, comm):
    n_in, n_out, n_scr = len(operands), len(out_shapes), len(scratch)
    hosted = _Hosted(comm, n_in, n_out)
    n_ci, n_co = len(hosted.operands), len(hosted.out_shapes)

    def hosting_body(*refs):
        ins, rest = refs[:n_in], refs[n_in:]
        c_ins, rest = rest[:n_ci], rest[n_ci:]
        outs, rest = rest[:n_out], rest[n_out:]
        c_outs, rest = rest[:n_co], rest[n_co:]
        hosted.wrap(grid, lambda: body(*ins, *outs, *rest[:n_scr]), c_ins, c_outs, rest[n_scr:])

    res = pl.pallas_call(
        hosting_body, name=name, grid=grid, out_shape=tuple(list(out_shapes) + hosted.out_shapes),
        in_specs=list(in_specs) + hosted.in_specs, out_specs=tuple(list(out_specs) + hosted.out_specs),
        input_output_aliases=hosted.aliases, scratch_shapes=list(scratch) + hosted.scratch,
        compiler_params=_cparams(len(grid)))(*operands, *hosted.operands)
    return list(res[:n_out]), list(res[n_out:])


def _matmul(name, grid, nk, kaxis, pairs, dims, extras, outs, epilogue, sum_pairs, acc_shape, comm=None, split=None):
    n_p, n_e, n_o = len(pairs), len(extras), len(outs)
    n_acc = 0 if nk == 1 else (1 if sum_pairs else n_p)
    n_in = 2 * n_p + n_e
    hosted = _Hosted(comm, n_in, n_o)
    n_ci, n_co = len(hosted.operands), len(hosted.out_shapes)

    def body(*refs):
        a_refs = refs[0:2 * n_p:2]
        b_refs = refs[1:2 * n_p:2]
        e_refs = refs[2 * n_p:n_in]
        c_ins = refs[n_in:n_in + n_ci]
        o_refs = refs[n_in + n_ci:n_in + n_ci + n_o]
        c_outs = refs[n_in + n_ci + n_o:n_in + n_ci + n_o + n_co]
        acc_refs = refs[n_in + n_ci + n_o + n_co:n_in + n_ci + n_o + n_co + n_acc]
        sems = refs[n_in + n_ci + n_o + n_co + n_acc:]

        def dots():
            if sum_pairs and n_p > 1 and dims == NN:
                a_all = jnp.concatenate([a[...] for a in a_refs], axis=1)
                b_all = jnp.concatenate([b[...] for b in b_refs], axis=0)
                return [lax.dot_general(a_all, b_all, (dims, ((), ())), preferred_element_type=F32)]
            prods = [lax.dot_general(a[...], b[...], (dims, ((), ())), preferred_element_type=F32)
                     for a, b in zip(a_refs, b_refs)]
            if sum_pairs and n_p > 1:
                prods = [functools.reduce(operator.add, prods)]
            return prods

        def compute():
            if nk == 1 and split is not None:
                n_split, b_axis, n_row = split
                width = b_refs[0].shape[b_axis] // n_split
                height = a_refs[0].shape[0] // n_row
                for s in range(n_split):
                    cols = pl.ds(s * width, width)
                    for r in range(n_row):
                        rows = pl.ds(r * height, height)
                        epilogue([lax.dot_general(a[rows, :], b[cols, :] if b_axis == 0 else b[:, cols], (dims, ((), ())),
                                                  preferred_element_type=F32) for a, b in zip(a_refs, b_refs)],
                                 e_refs, o_refs, rows, cols)
                return
            if nk == 1:
                epilogue(dots(), e_refs, o_refs)
                return
            k = pl.program_id(kaxis)

            @pl.when(k == 0)
            def _():
                for acc, p in zip(acc_refs, dots()):
                    acc[...] = p

            if nk > 2:
                @pl.when((k > 0) & (k < nk - 1))
                def _():
                    for acc, p in zip(acc_refs, dots()):
                        acc[...] += p

            @pl.when(k == nk - 1)
            def _():
                epilogue([acc[...] + p for acc, p in zip(acc_refs, dots())], e_refs, o_refs)

        hosted.wrap(grid, compute, c_ins, c_outs, sems)

    operands, in_specs = [], []
    for a, a_spec, b, b_spec in pairs:
        operands += [a, b]
        in_specs += [a_spec, b_spec]
    for e, e_spec in extras:
        operands.append(e)
        in_specs.append(e_spec)
    res = pl.pallas_call(
        body, name=name, grid=grid,
        out_shape=tuple([o for o, _ in outs] + hosted.out_shapes),
        in_specs=in_specs + hosted.in_specs, out_specs=tuple([s for _, s in outs] + hosted.out_specs),
        input_output_aliases=hosted.aliases,
        scratch_shapes=[pltpu.VMEM(acc_shape, F32) for _ in range(n_acc)] + hosted.scratch,
        compiler_params=_cparams(len(grid)),
    )(*operands, *hosted.operands)
    return list(res[:n_o]), list(res[n_o:])


NN = ((1,), (0,))
NT = ((1,), (1,))
TN = ((0,), (0,))


def _tile(n, want):
    if n <= want:
        return n
    t = want // 128 * 128
    while n % t:
        t -= 128
    return t


def _silu_parts(g):
    s = 0.5 + 0.5 * jnp.tanh(0.5 * g)
    return s, g * s


def _mm_in(h, w_in, comm=None):
    t, d = h.shape
    n = w_in.shape[1]
    tm, tn = _tile(t, _TM), _tile(n, _TN)

    def epi(accs, e, o):
        o[0][...] = accs[0].astype(BF16)

    outs, couts = _matmul(
        "mm_in", (n // tn, t // tm), 1, None,
        [(h, pl.BlockSpec((tm, d), lambda j, i: (i, 0)), w_in, pl.BlockSpec((d, tn), lambda j, i: (0, j)))],
        NN, [], [(_sds((t, n), BF16), pl.BlockSpec((tm, tn), lambda j, i: (i, j)))], epi, True, None, comm)
    return outs[0], couts


def _mm_out(y, w_out, x, comm=None):
    t, m = y.shape
    d = w_out.shape[1]
    tm, tn = _tile(t, _TM), _tile(d, _TN)

    def epi(accs, e, o):
        o[0][...] = e[0][...] + accs[0]

    outs, couts = _matmul(
        "mm_out", (t // tm, d // tn), 1, None,
        [(y, pl.BlockSpec((tm, m), lambda i, j: (i, 0)), w_out, pl.BlockSpec((m, tn), lambda i, j: (0, j)))],
        NN, [(x, pl.BlockSpec((tm, tn), lambda i, j: (i, j)))],
        [(_sds((t, d), F32), pl.BlockSpec((tm, tn), lambda i, j: (i, j)))], epi, True, None, comm)
    return outs[0], couts


def _mm_swiglu(h2, wgt, wut, comm=None):
    t, d = h2.shape
    f = wgt.shape[0]
    tm, tn = _tile(t, 2 * _TM), _tile(f, 512)

    def epi(accs, e, o, rows, cols):
        g, u = accs
        s, sg = _silu_parts(g)
        o[0][rows, cols] = (sg * u).astype(BF16)
        o[1][rows, cols] = (u * (s + sg * (1.0 - s))).astype(BF16)
        o[2][rows, cols] = sg.astype(BF16)

    wspec = pl.BlockSpec((tn, d), lambda i, j: (j, 0))
    hspec = pl.BlockSpec((tm, d), lambda i, j: (i, 0))
    ospec = pl.BlockSpec((tm, tn), lambda i, j: (i, j))
    osh = _sds((t, f), BF16)
    outs, couts = _matmul("mm_swiglu", (t // tm, f // tn), 1, None, [(h2, hspec, wgt, wspec), (h2, hspec, wut, wspec)],
                          NT, [], [(osh, ospec)] * 3, epi, False, None, comm, split=(tn // 256, 0, 2))
    return outs, couts


def _mm_down(act, wd, x1, comm=None):
    t, f = act.shape
    d = wd.shape[1]
    tm, tn = _tile(t, _TM), _tile(d, _TN)
    nk = 2
    tk = f // nk

    def epi(accs, e, o):
        o[0][...] = e[0][...] + accs[0]

    outs, couts = _matmul(
        "mm_down", (t // tm, d // tn, nk), nk, 2,
        [(act, pl.BlockSpec((tm, tk), lambda i, j, k: (i, k)), wd, pl.BlockSpec((tk, tn), lambda i, j, k: (k, j)))],
        NN, [(x1, pl.BlockSpec((tm, tn), lambda i, j, k: (i, j)))],
        [(_sds((t, d), F32), pl.BlockSpec((tm, tn), lambda i, j, k: (i, j)))], epi, True, (tm, tn), comm)
    return outs[0], couts


def _mm_dact(dxb, wd, dact_dgate, dact_dup, comm=None):
    t, d = dxb.shape
    f = wd.shape[0]
    tm, tn = _tile(t, 2 * _TM), _tile(f, 512)

    def epi(accs, e, o, rows, cols):
        da = accs[0]
        o[0][rows, cols] = (da * e[0][rows, cols].astype(F32)).astype(BF16)
        o[1][rows, cols] = (da * e[1][rows, cols].astype(F32)).astype(BF16)

    bspec = pl.BlockSpec((tm, tn), lambda i, j: (i, j))
    osh = _sds((t, f), BF16)
    outs, couts = _matmul(
        "mm_dact", (t // tm, f // tn), 1, None,
        [(dxb, pl.BlockSpec((tm, d), lambda i, j: (i, 0)), wd, pl.BlockSpec((tn, d), lambda i, j: (j, 0)))],
        NT, [(dact_dgate, bspec), (dact_dup, bspec)], [(osh, bspec)] * 2, epi, True, None, comm, split=(tn // 256, 0, 2))
    return outs, couts


def _mm_dh2(dgate, dup, wgt, wut, comm=None):
    t, f = dgate.shape
    d = wgt.shape[1]
    tm, tn = _tile(t, _TM), _tile(d, _TN)
    nk = 4
    tk = f // nk

    def epi(accs, e, o):
        o[0][...] = accs[0].astype(BF16)

    aspec = pl.BlockSpec((tm, tk), lambda i, j, k: (i, k))
    wspec = pl.BlockSpec((tk, tn), lambda i, j, k: (k, j))
    outs, couts = _matmul("mm_dh2", (t // tm, d // tn, nk), nk, 2, [(dgate, aspec, wgt, wspec), (dup, aspec, wut, wspec)],
                          NN, [], [(_sds((t, d), BF16), pl.BlockSpec((tm, tn), lambda i, j, k: (i, j)))], epi, True,
                          (tm, tn), comm)
    return outs[0], couts


def _mm_dw(name, a_list, b, tmo, tno, comm=None, m_rows=None):
    t, m = a_list[0].shape
    start, m = (0, m) if m_rows is None else m_rows
    n = b.shape[1]
    tt = _tile(t, _TT)
    nk = t // tt
    tmo, tno = _tile(m, tmo), _tile(n, tno)
    first = start // tmo

    def epi(accs, e, o):
        for acc, out in zip(accs, o):
            out[...] = acc.astype(BF16)

    aspec = pl.BlockSpec((tt, tmo), lambda i, j, k: (k, first + i))
    bspec = pl.BlockSpec((tt, tno), lambda i, j, k: (k, j))
    ospec = pl.BlockSpec((tmo, tno), lambda i, j, k: (i, j))
    if nk == 1:
        return _matmul(name, (m // tmo, n // tno, 1), 1, None, [(a, aspec, b, bspec) for a in a_list], TN, [],
                       [(_sds((m, n), BF16), ospec)] * len(a_list), epi, False, None, comm)
    return _matmul(name, (m // tmo, n // tno, nk), nk, 2, [(a, aspec, b, bspec) for a in a_list], TN, [],
                   [(_sds((m, n), BF16), ospec)] * len(a_list), epi, False, (tmo, tno), comm)


def _mm_dy(dxb, w_out, comm=None):
    t, d = dxb.shape
    m = w_out.shape[0]
    tm, tn = _tile(t, _TM), _tile(m, _TN)

    def epi(accs, e, o):
        o[0][...] = accs[0].astype(BF16)

    outs, couts = _matmul(
        "mm_dy", (t // tm, m // tn), 1, None,
        [(dxb, pl.BlockSpec((tm, d), lambda i, j: (i, 0)), w_out, pl.BlockSpec((tn, d), lambda i, j: (j, 0)))], NT, [],
        [(_sds((t, m), BF16), pl.BlockSpec((tm, tn), lambda i, j: (i, j)))], epi, True, None, comm)
    return outs[0], couts


def _mm_dh(dz, w_in, comm=None):
    t, n = dz.shape
    d = w_in.shape[0]
    tm, tn = _tile(t, _TM), _tile(d, _TN)
    nk = 2
    tk = n // nk

    def epi(accs, e, o):
        o[0][...] = accs[0].astype(BF16)

    outs, couts = _matmul(
        "mm_dh", (t // tm, d // tn, nk), nk, 2,
        [(dz, pl.BlockSpec((tm, tk), lambda i, j, k: (i, k)), w_in, pl.BlockSpec((tn, tk), lambda i, j, k: (j, k)))], NT,
        [], [(_sds((t, d), BF16), pl.BlockSpec((tm, tn), lambda i, j, k: (i, j)))], epi, True, (tm, tn), comm)
    return outs[0], couts


def _rmsnorm_fwd(x, g, comm=None):
    t, d = x.shape
    tm = min(_TM_NORM, t)

    def body(x_ref, g_ref, o_ref):
        xv = x_ref[...]
        rs = lax.rsqrt(jnp.mean(xv * xv, axis=-1, keepdims=True) + RMS_EPS)
        o_ref[...] = (xv * rs * g_ref[...]).astype(BF16)

    outs, couts = _call_hosting(
        body, "rmsnorm_fwd", (t // tm,), [_sds((t, d), BF16)],
        [pl.BlockSpec((tm, d), lambda i: (i, 0)), pl.BlockSpec((1, d), lambda i: (0, 0))],
        [pl.BlockSpec((tm, d), lambda i: (i, 0))], [x, g], [], comm)
    return outs[0], couts


def _rmsnorm_bwd_math(xv, g, dh):
    rs = lax.rsqrt(jnp.mean(xv * xv, axis=-1, keepdims=True) + RMS_EPS)
    xh = xv * rs
    gd = dh * g
    dx = rs * (gd - xh * jnp.mean(gd * xh, axis=-1, keepdims=True))
    return dx, jnp.sum(dh * xh, axis=0, keepdims=True)


def _rmsnorm_bwd(x, g, dh, dres, with_bf16=True):
    t, d = x.shape
    tm = min(_TM_NORM, t)

    def body(x_ref, g_ref, dh_ref, dres_ref, dx_ref, *rest):
        dg_ref = rest[-1]
        dx, dg = _rmsnorm_bwd_math(x_ref[...], g_ref[...], dh_ref[...].astype(F32))
        dx = dx + dres_ref[...]
        dx_ref[...] = dx
        if with_bf16:
            rest[0][...] = dx.astype(BF16)

        @pl.when(pl.program_id(0) == 0)
        def _():
            dg_ref[...] = dg

        @pl.when(pl.program_id(0) > 0)
        def _():
            dg_ref[...] += dg

    row = pl.BlockSpec((tm, d), lambda i: (i, 0))
    vec = pl.BlockSpec((1, d), lambda i: (0, 0))
    halves = [(_sds((t, d), BF16), row)] if with_bf16 else []
    outs = [(_sds((t, d), F32), row), *halves, (_sds((1, d), F32), vec)]
    res = pl.pallas_call(
        body, name="rmsnorm_bwd", grid=(t // tm,), out_shape=tuple(o for o, _ in outs),
        in_specs=[row, vec, row, row], out_specs=tuple(s for _, s in outs), compiler_params=_cparams(1))(x, g, dh, dres)
    return (res[0], res[1], res[2]) if with_bf16 else (res[0], None, res[1])


def _loss_head(x, g, target):
    t, d = x.shape
    tm = min(_TM_NORM, t)

    def body(x_ref, g_ref, t_ref, dx_ref, dxb_ref, dg_ref, loss_ref):
        xv, gv = x_ref[...], g_ref[...]
        rs = lax.rsqrt(jnp.mean(xv * xv, axis=-1, keepdims=True) + RMS_EPS)
        diff = xv * rs * gv - t_ref[...]
        part = 0.5 * jnp.sum(jnp.mean(diff * diff, axis=-1, keepdims=True), axis=0, keepdims=True)
        part = jnp.broadcast_to(part, (1, 128))
        dx, dg = _rmsnorm_bwd_math(xv, gv, diff * (1.0 / d))
        dx_ref[...] = dx
        dxb_ref[...] = dx.astype(BF16)

        @pl.when(pl.program_id(0) == 0)
        def _():
            dg_ref[...] = dg
            loss_ref[...] = part

        @pl.when(pl.program_id(0) > 0)
        def _():
            dg_ref[...] += dg
            loss_ref[...] += part

    row = pl.BlockSpec((tm, d), lambda i: (i, 0))
    vec = pl.BlockSpec((1, d), lambda i: (0, 0))
    return pl.pallas_call(
        body, name="loss_head", grid=(t // tm,),
        out_shape=(_sds((t, d), F32), _sds((t, d), BF16), _sds((1, d), F32), _sds((1, 128), F32)),
        in_specs=[row, vec, row], out_specs=(row, row, vec, pl.BlockSpec((1, 128), lambda i: (0, 0))),
        compiler_params=_cparams(1))(x, g, target)


def _gelu(x):
    th = jnp.tanh(GELU_C * (x + GELU_A * x * x * x))
    return 0.5 * x * (1.0 + th), th


def _gelu_grad(x, th):
    return 0.5 * (1.0 + th) + 0.5 * x * (1.0 - th * th) * GELU_C * (1.0 + 3.0 * GELU_A * x * x)


def _masked_ws(ws_ref, h):
    i = lax.broadcasted_iota(jnp.int32, (BLK, BLK), 0) // CHUNK
    j = lax.broadcasted_iota(jnp.int32, (BLK, BLK), 1) // CHUNK
    return jnp.where(j <= i, ws_ref[h], 0.0)


def _shift_down(q, n, first_rows):
    rolled = pltpu.roll(q, n, 0)
    row = lax.broadcasted_iota(jnp.int32, q.shape, 0)
    for r, val in enumerate(first_rows):
        rolled = jnp.where(row == r, val, rolled)
    return rolled


def _shift_up(q, n, last_rows):
    tm = q.shape[0]
    rolled = pltpu.roll(q, tm - n, 0)
    row = lax.broadcasted_iota(jnp.int32, q.shape, 0)
    for r, val in enumerate(last_rows):
        rolled = jnp.where(row == tm - n + r, val, rolled)
    return rolled


def _mixer_specs(t, a, tm):
    hb = tm // HALO
    last = t // HALO - 1
    tile = pl.BlockSpec((tm, 5 * a), lambda i: (i, 0))
    prev = [pl.BlockSpec((HALO, a), functools.partial(lambda i, col: (jnp.maximum(i * hb - 1, 0), col), col=col))
            for col in (3, 4)]
    nxt = [pl.BlockSpec((HALO, a), functools.partial(lambda i, col: (jnp.minimum((i + 1) * hb, last), col), col=col))
           for col in (2, 3, 4)]
    return tile, prev, nxt


def _group_a_fwd(zu, zv, lng, lnb, ws_ref, bb_ref, mixed_ref, vln_ref):
    u, thu = _gelu(zu)
    v, thv = _gelu(zv)
    mu = jnp.mean(v, axis=-1, keepdims=True)
    vc = v - mu
    rs = lax.rsqrt(jnp.mean(vc * vc, axis=-1, keepdims=True) + LN_EPS)
    vhat = vc * rs
    vln_ref[...] = vhat * lng + lnb
    tm, a = zu.shape
    hd = a // HEADS
    for h in range(HEADS):
        w = _masked_ws(ws_ref, h).astype(BF16)
        for b in range(tm // BLK):
            rows, cols = pl.ds(b * BLK, BLK), pl.ds(h * hd, hd)
            mixed_ref[rows, cols] = jnp.dot(w, vln_ref[rows, cols].astype(BF16), preferred_element_type=F32) + bb_ref[h]
    return u, thu, thv, rs, vhat


def _mixer_fwd(z, ln_g, ln_b, w_spatial, bb, conv_w, gg, comm=None):
    t = z.shape[0]
    a = z.shape[1] // 5
    tm = min(_TM_MIX, t)
    tile, prev, _ = _mixer_specs(t, a, tm)

    def body(z_ref, pc_ref, ph_ref, lng_ref, lnb_ref, ws_ref, bb_ref, cw_ref, gg_ref, y_ref, mixed_ref, vln_ref):
        i = pl.program_id(0)
        zu = z_ref[:, 0:a].astype(F32)
        zv = z_ref[:, a:2 * a].astype(F32)
        u, _, _, _, _ = _group_a_fwd(zu, zv, lng_ref[...], lnb_ref[...], ws_ref, bb_ref, mixed_ref, vln_ref)
        ya = u * mixed_ref[...]
        ra = lax.rsqrt(jnp.mean(ya * ya, axis=-1, keepdims=True) + RMS_EPS)
        y_ref[:, 0:a] = (ya * ra * gg_ref[:, 0:a]).astype(BF16)

        zb = z_ref[:, 2 * a:3 * a].astype(F32)
        q = z_ref[:, 3 * a:4 * a].astype(F32) * z_ref[:, 4 * a:5 * a].astype(F32)
        qp = jnp.where(i > 0, pc_ref[...].astype(F32) * ph_ref[...].astype(F32), 0.0)
        qm1 = _shift_down(q, 1, [qp[HALO - 1:HALO]])
        qm2 = _shift_down(q, 2, [qp[HALO - 2:HALO - 1], qp[HALO - 1:HALO]])
        cv = cw_ref[0:1, :] * qm2 + cw_ref[1:2, :] * qm1 + cw_ref[2:3, :] * q
        yb = zb * cv
        rb = lax.rsqrt(jnp.mean(yb * yb, axis=-1, keepdims=True) + RMS_EPS)
        y_ref[:, a:2 * a] = (yb * rb * gg_ref[:, a:2 * a]).astype(BF16)

    full = lambda shape: pl.BlockSpec(shape, lambda i: (0,) * len(shape))
    outs, couts = _call_hosting(
        body, "mixer_fwd", (t // tm,), [_sds((t, 2 * a), BF16)],
        [tile, *prev, full((1, a)), full((1, a)), full(w_spatial.shape), full(bb.shape), full(conv_w.shape),
         full((1, 2 * a))],
        [pl.BlockSpec((tm, 2 * a), lambda i: (i, 0))], [z, z, z, ln_g, ln_b, w_spatial, bb, conv_w, gg],
        [pltpu.VMEM((tm, a), F32), pltpu.VMEM((tm, a), F32)], comm)
    return outs[0], couts


def _mixer_bwd(z, dy, ln_g, ln_b, w_spatial, bb, conv_w, gg, comm=None):
    t = z.shape[0]
    a = z.shape[1] // 5
    hd = a // HEADS
    tm = min(_TM_MIX, t)
    n_tiles = t // tm
    tile, prev, nxt = _mixer_specs(t, a, tm)
    hb = tm // HALO
    dy_tile = pl.BlockSpec((tm, 2 * a), lambda i: (i, 0))
    dy_next = pl.BlockSpec((HALO, a), lambda i: (jnp.minimum((i + 1) * hb, t // HALO - 1), 1))

    def body(z_ref, pc_ref, ph_ref, nb_ref, nc_ref, nh_ref, dy_ref, ndy_ref, lng_ref, lnb_ref, ws_ref, bb_ref, cw_ref,
             gg_ref, dz_ref, dlng_ref, dlnb_ref, dws_ref, dbb_ref, dcw_ref, dgg_ref, mixed_ref, vln_ref, dmix_ref,
             dvln_ref):
        i = pl.program_id(0)

        @pl.when(i == 0)
        def _():
            for ref in (dlng_ref, dlnb_ref, dws_ref, dbb_ref, dcw_ref, dgg_ref):
                ref[...] = jnp.zeros(ref.shape, F32)

        lng = lng_ref[...]
        zu = z_ref[:, 0:a].astype(F32)
        zv = z_ref[:, a:2 * a].astype(F32)
        u, thu, thv, rs, vhat = _group_a_fwd(zu, zv, lng, lnb_ref[...], ws_ref, bb_ref, mixed_ref, vln_ref)
        mixed = mixed_ref[...]
        ya = u * mixed
        ra = lax.rsqrt(jnp.mean(ya * ya, axis=-1, keepdims=True) + RMS_EPS)
        da = dy_ref[:, 0:a].astype(F32)
        yah = ya * ra
        dgg_ref[:, 0:a] += jnp.sum(da * yah, axis=0, keepdims=True)
        ga = da * gg_ref[:, 0:a]
        dya = ra * (ga - yah * jnp.mean(ga * yah, axis=-1, keepdims=True))
        dz_ref[:, 0:a] = (dya * mixed * _gelu_grad(zu, thu)).astype(BF16)
        dmix_ref[...] = dya * u
        for h in range(HEADS):
            w = _masked_ws(ws_ref, h).astype(BF16)
            dw = jnp.zeros((BLK, BLK), F32)
            db = jnp.zeros((BLK, hd), F32)
            for b in range(tm // BLK):
                rows, cols = pl.ds(b * BLK, BLK), pl.ds(h * hd, hd)
                dm = dmix_ref[rows, cols]
                dmb = dm.astype(BF16)
                db = db + dm
                dw = dw + lax.dot_general(dmb, vln_ref[rows, cols].astype(BF16), (NT, ((), ())),
                                          preferred_element_type=F32)
                dvln_ref[rows, cols] = lax.dot_general(w, dmb, (TN, ((), ())), preferred_element_type=F32)
            dws_ref[h] += dw
            dbb_ref[h] += db
        dvln = dvln_ref[...]
        dlng_ref[...] += jnp.sum(dvln * vhat, axis=0, keepdims=True)
        dlnb_ref[...] += jnp.sum(dvln, axis=0, keepdims=True)
        dvh = dvln * lng
        dv = rs * (dvh - jnp.mean(dvh, axis=-1, keepdims=True) - vhat * jnp.mean(dvh * vhat, axis=-1, keepdims=True))
        dz_ref[:, a:2 * a] = (dv * _gelu_grad(zv, thv)).astype(BF16)

        w0, w1, w2 = cw_ref[0:1, :], cw_ref[1:2, :], cw_ref[2:3, :]
        ggb = gg_ref[:, a:2 * a]
        zb = z_ref[:, 2 * a:3 * a].astype(F32)
        zc = z_ref[:, 3 * a:4 * a].astype(F32)
        zh = z_ref[:, 4 * a:5 * a].astype(F32)
        q = zc * zh
        qp = jnp.where(i > 0, pc_ref[...].astype(F32) * ph_ref[...].astype(F32), 0.0)
        qm1 = _shift_down(q, 1, [qp[HALO - 1:HALO]])
        qm2 = _shift_down(q, 2, [qp[HALO - 2:HALO - 1], qp[HALO - 1:HALO]])
        cv = w0 * qm2 + w1 * qm1 + w2 * q

        def conv_out_grad(zb_, cv_, dout_):
            yb = zb_ * cv_
            rb = lax.rsqrt(jnp.mean(yb * yb, axis=-1, keepdims=True) + RMS_EPS)
            ybh = yb * rb
            gb = dout_ * ggb
            dyb = rb * (gb - ybh * jnp.mean(gb * ybh, axis=-1, keepdims=True))
            return dyb * zb_, dyb * cv_, ybh

        db_out = dy_ref[:, a:2 * a].astype(F32)
        g, dzb, ybh = conv_out_grad(zb, cv, db_out)
        dgg_ref[:, a:2 * a] += jnp.sum(db_out * ybh, axis=0, keepdims=True)
        dz_ref[:, 2 * a:3 * a] = dzb.astype(BF16)
        qn = nc_ref[...].astype(F32) * nh_ref[...].astype(F32)
        zbn = nb_ref[...].astype(F32)
        cvn = w0 * _shift_down(qn, 2, [q[tm - 2:tm - 1], q[tm - 1:tm]]) + w1 * _shift_down(qn, 1, [q[tm - 1:tm]]) + w2 * qn
        gn, _, _ = conv_out_grad(zbn, cvn, ndy_ref[...].astype(F32))
        gn = jnp.where(i < n_tiles - 1, gn, 0.0)
        dq = w2 * g + w1 * _shift_up(g, 1, [gn[0:1]]) + w0 * _shift_up(g, 2, [gn[0:1], gn[1:2]])
        dz_ref[:, 3 * a:4 * a] = (dq * zh).astype(BF16)
        dz_ref[:, 4 * a:5 * a] = (dq * zc).astype(BF16)
        dcw_ref[0:1, :] += jnp.sum(g * qm2, axis=0, keepdims=True)
        dcw_ref[1:2, :] += jnp.sum(g * qm1, axis=0, keepdims=True)
        dcw_ref[2:3, :] += jnp.sum(g * q, axis=0, keepdims=True)

        @pl.when(i == n_tiles - 1)
        def _():
            for h in range(HEADS):
                dbb_ref[h] = jnp.broadcast_to(jnp.sum(dbb_ref[h], axis=1, keepdims=True), (BLK, hd))
                dws_ref[h] = _masked_ws(dws_ref, h)

    full = lambda shape: pl.BlockSpec(tuple(shape), lambda i: (0,) * len(shape))
    out_shapes = (_sds((t, 5 * a), BF16), _sds((1, a), F32), _sds((1, a), F32), _sds(w_spatial.shape, F32),
                  _sds(bb.shape, F32), _sds((8, a), F32), _sds((1, 2 * a), F32))
    return _call_hosting(
        body, "mixer_bwd", (n_tiles,), out_shapes,
        [tile, *prev, *nxt, dy_tile, dy_next, full((1, a)), full((1, a)), full(w_spatial.shape), full(bb.shape),
         full(conv_w.shape), full((1, 2 * a))],
        [tile, *[full(s.shape) for s in out_shapes[1:]]], [z, z, z, z, z, z, dy, dy, ln_g, ln_b, w_spatial, bb, conv_w, gg],
        [pltpu.VMEM((tm, a), F32)] * 4, comm)


def _all_reduce_small(pack, comm=None):
    r = pack.shape[0]
    hosted = _Hosted(comm, 1, 1)
    n_ci, n_co = len(hosted.operands), len(hosted.out_shapes)

    def body(*refs):
        in_ref, c_ins, out_ref, c_outs = refs[0], refs[1:1 + n_ci], refs[1 + n_ci], refs[2 + n_ci:2 + n_ci + n_co]
        acc_ref, recv_ref, send_sems, recv_sems = refs[2 + n_ci + n_co:6 + n_ci + n_co]
        sems = refs[6 + n_ci + n_co:]
        hosted.run("start", c_ins, c_outs, sems)
        x, y, c = _place()
        partners = [(x, y, 1 - c), (1 - x, y, c), (x, 1 - y, c)]
        acc_ref[0] = in_ref[...]
        for s, partner in enumerate(partners):
            cp = pltpu.make_async_remote_copy(
                src_ref=acc_ref.at[s], dst_ref=recv_ref.at[s], send_sem=send_sems.at[s], recv_sem=recv_sems.at[s],
                device_id=partner, device_id_type=MESH)
            cp.start()
            cp.wait()
            if s < 2:
                acc_ref[s + 1] = acc_ref[s] + recv_ref[s]
            else:
                out_ref[...] = acc_ref[s] + recv_ref[s]
        for stage in ("mid1", "mid2", "finish"):
            hosted.run(stage, c_ins, c_outs, sems)

    vmem = pl.BlockSpec(memory_space=pltpu.VMEM)
    res = pl.pallas_call(
        body, name="all_reduce_small", out_shape=tuple([_sds(pack.shape, F32)] + hosted.out_shapes),
        in_specs=[vmem] + hosted.in_specs, out_specs=tuple([vmem] + hosted.out_specs),
        input_output_aliases=hosted.aliases,
        scratch_shapes=[pltpu.VMEM((3, r, 128), F32), pltpu.VMEM((3, r, 128), F32), pltpu.SemaphoreType.DMA((3,)),
                        pltpu.SemaphoreType.DMA((3,))] + hosted.scratch,
        compiler_params=pltpu.CompilerParams(vmem_limit_bytes=VMEM_LIMIT_V7X),
    )(pack, *hosted.operands)
    return res[0], list(res[1:])


def _adamw_math(w, g, m, v):
    m = ADAM_B1 * m + (1.0 - ADAM_B1) * g
    v = ADAM_B2 * v + (1.0 - ADAM_B2) * (g * g)
    m_hat = m / (1.0 - ADAM_B1 ** ADAM_STEP)
    v_hat = v / (1.0 - ADAM_B2 ** ADAM_STEP)
    delta = -ADAM_LR * (m_hat / (jnp.sqrt(v_hat) + ADAM_EPS) + ADAM_WD * w)
    return delta, m, v


def _adamw_big(name, land, w, m, v, comm=None):
    nl, n_slots, r, c = land.shape
    tr = max(8, min(r, (256 * 640) // c // 8 * 8))
    while r % tr:
        tr -= 8
    grid = (nl, r // tr)
    hosted = _Hosted(comm, 4, 4)
    n_ci, n_co = len(hosted.operands), len(hosted.out_shapes)

    def body(*refs):
        land_ref, w_ref, m_ref, v_ref = refs[:4]
        c_ins = refs[4:4 + n_ci]
        g_out, d_out, m_out, v_out = refs[4 + n_ci:8 + n_ci]
        c_outs = refs[8 + n_ci:8 + n_ci + n_co]
        sems = refs[8 + n_ci + n_co:]

        def compute():
            g = land_ref[0].astype(F32)
            for s in range(1, n_slots):
                g = g + land_ref[s].astype(F32)
            delta, mn, vn = _adamw_math(w_ref[...], g, m_ref[...], v_ref[...])
            g_out[...] = g
            d_out[...] = delta
            m_out[...] = mn
            v_out[...] = vn

        hosted.wrap(grid, compute, c_ins, c_outs, sems)

    blk = pl.BlockSpec((None, tr, c), lambda l, i: (l, i, 0))
    res = pl.pallas_call(
        body, name=name, grid=grid, out_shape=tuple([_sds((nl, r, c), F32)] * 4 + hosted.out_shapes),
        in_specs=[pl.BlockSpec((None, n_slots, tr, c), lambda l, i: (l, 0, i, 0)), blk, blk, blk] + hosted.in_specs,
        out_specs=tuple([blk] * 4 + hosted.out_specs), input_output_aliases=hosted.aliases,
        scratch_shapes=hosted.scratch, compiler_params=_cparams(2))(land, w, m, v, *hosted.operands)
    return list(res[:4]), list(res[4:])


def _adamw_small(gs, ws, ms, vs):
    n = len(gs)

    def body(*refs):
        g_refs, w_refs, m_refs, v_refs = refs[:n], refs[n:2 * n], refs[2 * n:3 * n], refs[3 * n:4 * n]
        d_outs, m_outs, v_outs = refs[4 * n:5 * n], refs[5 * n:6 * n], refs[6 * n:7 * n]
        for i in range(n):
            delta, mn, vn = _adamw_math(w_refs[i][...], g_refs[i][...], m_refs[i][...], v_refs[i][...])
            d_outs[i][...] = delta
            m_outs[i][...] = mn
            v_outs[i][...] = vn

    shapes = [_sds(g.shape, F32) for g in gs]
    res = pl.pallas_call(body, name="adamw_small", out_shape=tuple(shapes * 3),
                         compiler_params=pltpu.CompilerParams(vmem_limit_bytes=VMEM_LIMIT_V7X))(*gs, *ws, *ms, *vs)
    return list(res[:n]), list(res[n:2 * n]), list(res[2 * n:])


def _rows(a):
    return a.reshape(-1, 128)


BIG = ["w_in", "w_out", "w_gate", "w_up", "w_down"]
AG_HOSTS = {
    ("norm1", 0): [("w_in", 0), ("conv_w", 0), ("w_out", 0)],
    ("mm_in", 0): [("w_gate", 0)], ("mixer", 0): [("w_up", 0, 0, 2)], ("mm_out", 0): [("w_up", 0, 1, 2)],
    ("mm_swiglu", 0): [("w_down", 0), ("w_in", 1), ("w_out", 1)], ("mm_down", 0): [("w_gate", 1)],
    ("mm_in", 1): [("w_up", 1)], ("mm_swiglu", 1): [("w_down", 1)],
}


def kernel(x, norm1_g, w_in, gmlp_ln_g, gmlp_ln_b, w_spatial, b_spatial, conv_w, group_norm_g, w_out, norm2_g, w_gate, w_up, w_down, final_norm_g, loss_target, m_norm1_g, m_w_in, m_gmlp_ln_g, m_gmlp_ln_b, m_w_spatial, m_b_spatial, m_conv_w, m_group_norm_g, m_w_out, m_norm2_g, m_w_gate, m_w_up, m_w_down, m_final_norm_g, v_norm1_g, v_w_in, v_gmlp_ln_g, v_gmlp_ln_b, v_w_spatial, v_b_spatial, v_conv_w, v_group_norm_g, v_w_out, v_norm2_g, v_w_gate, v_w_up, v_w_down, v_final_norm_g):
    nl = N_LAYERS
    t, d = x.shape[1], x.shape[2]
    a = d // 2
    hd = a // HEADS
    xin = x.reshape(t, d)
    target = loss_target.reshape(t, d)
    me = _index(_place())

    tr = lambda w: jnp.transpose(w, (0, 2, 1))
    big = {"w_in": w_in, "w_out": w_out, "w_gate": tr(w_gate), "w_up": tr(w_up), "w_down": w_down}
    big_m = {"w_in": m_w_in, "w_out": m_w_out, "w_gate": tr(m_w_gate), "w_up": tr(m_w_up), "w_down": m_w_down}
    big_v = {"w_in": v_w_in, "w_out": v_w_out, "w_gate": tr(v_w_gate), "w_up": tr(v_w_up), "w_down": v_w_down}
    block = {k: big[k].shape[1:] for k in BIG}
    view = {k: _cols_view(block[k][1]) if k == "w_in" else _rows_view(block[k][0]) for k in BIG}
    full_shape = {k: (block[k][0], N_DEV * block[k][1]) if k == "w_in" else (N_DEV * block[k][0], block[k][1])
                  for k in BIG}

    weights = {}
    shards = {(k, l): big[k][l].astype(BF16) for k in BIG for l in range(nl)}

    def ag_spec(k, l, part=0, n_parts=1):
        if k == "conv_w":
            return (conv_w, _sds((N_DEV, *conv_w.shape), F32), _SLOT_WHOLE, (0,), None)
        halves = (_cols_halves(*block[k], part, n_parts) if k == "w_in" else _rows_halves(block[k][0], part, n_parts))
        return (shards[(k, l)], _sds(full_shape[k], BF16), halves, (0, 1), weights.get((k, l)))

    bb = jnp.broadcast_to(b_spatial[..., None], (nl, HEADS, BLK, hd))

    def hosted(name, l):
        keys = AG_HOSTS.get((name, l), [])
        return keys, ([_ag_piece([ag_spec(*key) for key in keys])] if keys else None)

    def landed(keys, couts):
        for key, arr in zip(keys, couts):
            weights[key[:2]] = arr

    saved = []
    xl = xin
    for l in range(nl):
        keys, comm = hosted("norm1", l)
        h, couts = _rmsnorm_fwd(xl, norm1_g[l:l + 1], comm)
        landed(keys, couts)
        if l == 0:
            conv_full = jnp.transpose(weights[("conv_w", 0)], (1, 2, 0, 3)).reshape(nl, 3, a)
        keys, comm = hosted("mm_in", l)
        z, couts = _mm_in(h, weights[("w_in", l)], comm)
        landed(keys, couts)
        keys, comm = hosted("mixer", l)
        y, couts = _mixer_fwd(z, gmlp_ln_g[l:l + 1], gmlp_ln_b[l:l + 1], w_spatial[l], bb[l], conv_full[l],
                              group_norm_g[l:l + 1], comm)
        landed(keys, couts)
        keys, comm = hosted("mm_out", l)
        x1, couts = _mm_out(y, weights[("w_out", l)], xl, comm)
        landed(keys, couts)
        keys, comm = hosted("norm2", l)
        h2, couts = _rmsnorm_fwd(x1, norm2_g[l:l + 1], comm)
        landed(keys, couts)
        keys, comm = hosted("mm_swiglu", l)
        (act, dact_dgate, dact_dup), couts = _mm_swiglu(h2, weights[("w_gate", l)], weights[("w_up", l)], comm)
        landed(keys, couts)
        keys, comm = hosted("mm_down", l)
        x2, couts = _mm_down(act, weights[("w_down", l)], x1, comm)
        landed(keys, couts)
        saved.append(dict(x=xl, h=h, z=z, y=y, x1=x1, h2=h2, dact_dgate=dact_dgate, dact_dup=dact_dup, act=act))
        xl = x2

    dx, dxb, d_final_g, loss_part = _loss_head(xl, final_norm_g.reshape(1, d), target)
    small = [None] * nl
    core = lax.axis_index("c").astype(jnp.int32).reshape(1)
    in_rows = block["w_in"][0]
    part_of = {"w_in_a": ("w_in", 0), "w_in_b": ("w_in", 7 * in_rows // 8)}
    block["w_in_a"], block["w_in_b"] = (7 * in_rows // 8, block["w_in"][1]), (in_rows // 8, block["w_in"][1])
    for k in part_of:
        view[k] = view["w_in"]
    stage_shape = {k: _sds((N_CHIPS, *block[k]), BF16) for k in block}
    land_shape = {k: _sds((nl, N_CHIPS, *block[k]), BF16) for k in BIG}
    grads = [dict() for _ in range(nl)]
    stages = [dict() for _ in range(nl)]
    sums = [dict() for _ in range(nl)]
    lands = {k: None for k in BIG}

    def core_job(l, keys):
        def sink(outs):
            stages[l].update(zip(keys, outs))
        return _rs_core_piece([(grads[l][k], stage_shape[k], view[k]) for k in keys]), sink

    def chip_job(l, items):
        keys = [part_of.get(item[0], (item[0], 0))[0] for item in items]

        def rows(k, p0, p1, n_parts):
            per = block[k][0] // n_parts
            landing = part_of.get(k, (k, 0))[1]
            return (p0 * per, landing + p0 * per, (p1 - p0) * per)

        def sink(outs):
            lands.update(zip(keys, outs))
        return _rs_chip_piece([(sums[l][k], land_shape[key], rows(k, p0, p1, n_parts), lands[key])
                               for key, (k, p0, p1, n_parts) in zip(keys, items)], l), sink

    def add_up(l, keys):
        for k in keys:
            sums[l][k] = _chip_sums(f"chip_sums_{k}", grads[l][k], stages[l][k], k.startswith("w_in"), core)

    def host(*jobs):
        def deliver(couts):
            i = 0
            for piece, sink in jobs:
                n_out = len(piece.out_shapes)
                sink(couts[i:i + n_out])
                i += n_out
        return [piece for piece, _ in jobs], deliver

    whole = lambda k: (k, 0, 1, 1)
    rep = ["norm1_g", "gmlp_ln_g", "gmlp_ln_b", "w_spatial", "b_spatial", "group_norm_g", "norm2_g"]
    rep_w = dict(norm1_g=norm1_g, gmlp_ln_g=gmlp_ln_g, gmlp_ln_b=gmlp_ln_b, w_spatial=w_spatial, b_spatial=b_spatial,
                 group_norm_g=group_norm_g, norm2_g=norm2_g)
    rep_m = dict(norm1_g=m_norm1_g, gmlp_ln_g=m_gmlp_ln_g, gmlp_ln_b=m_gmlp_ln_b, w_spatial=m_w_spatial,
                 b_spatial=m_b_spatial, group_norm_g=m_group_norm_g, norm2_g=m_norm2_g)
    rep_v = dict(norm1_g=v_norm1_g, gmlp_ln_g=v_gmlp_ln_g, gmlp_ln_b=v_gmlp_ln_b, w_spatial=v_w_spatial,
                 b_spatial=v_b_spatial, group_norm_g=v_group_norm_g, norm2_g=v_norm2_g)

    def small_grad_parts():
        parts = [_rows(jnp.stack([small[l][k].reshape(rep_w[k].shape[1:]) for l in range(nl)])) for k in rep]
        parts.append(_rows(d_final_g))
        parts.append(_rows(jnp.stack([small[l]["conv_w"] for l in range(nl)])))
        parts.append(jnp.broadcast_to(loss_part, (8, 128)))
        rows = sum(p.shape[0] for p in parts)
        parts.append(jnp.zeros((-rows % 16, 128), F32))
        return parts

    for l in reversed(range(nl)):
        s = saved[l]
        wi, wo, wgt, wut, wd = [weights[(k, l)] for k in BIG]
        later = l + 1 < nl
        comm, deliver = host(chip_job(l + 1, [("w_in", 0, 1, 2)])) if later else host()
        (grads[l]["w_down"],), couts = _mm_dw("mm_dw_down", [s["act"]], dxb, 2816, 1024, comm)
        deliver(couts)
        comm, deliver = (host(core_job(l, ["w_down"]), chip_job(l + 1, [("w_in", 1, 2, 2)])) if later
                         else host(core_job(l, ["w_down"])))
        (dgate, dup), couts = _mm_dact(dxb, wd, s["dact_dgate"], s["dact_dup"], comm)
        deliver(couts)
        add_up(l, ["w_down"])
        if l > 0:
            comm, deliver = host(chip_job(l, [whole("w_down")]))
            (grads[l]["w_gate"], grads[l]["w_up"]), couts = _mm_dw("mm_dw_gate_up", [dgate, dup], s["h2"], 1408, 1024,
                                                                  comm)
            deliver(couts)
            comm, deliver = host(core_job(l, ["w_gate", "w_up"]))
            dh2, couts = _mm_dh2(dgate, dup, wgt, wut, comm)
            deliver(couts)
            add_up(l, ["w_gate", "w_up"])
            late = "w_gate"
        else:
            comm, deliver = host(chip_job(l, [("w_down", 0, 3, 4)]))
            (grads[l]["w_gate"],), couts = _mm_dw("mm_dw_gate", [dgate], s["h2"], 2816, 1024, comm)
            deliver(couts)
            comm, deliver = host(chip_job(l, [("w_down", 3, 4, 4)]), core_job(l, ["w_gate"]))
            (grads[l]["w_up"],), couts = _mm_dw("mm_dw_up", [dup], s["h2"], 2816, 1024, comm)
            deliver(couts)
            add_up(l, ["w_gate"])
            comm, deliver = host(chip_job(l, [whole("w_gate")]), core_job(l, ["w_up"]))
            dh2, couts = _mm_dh2(dgate, dup, wgt, wut, comm)
            deliver(couts)
            add_up(l, ["w_up"])
            late = "w_up"
        dx1, dx1b, d_n2 = _rmsnorm_bwd(s["x1"], norm2_g[l:l + 1], dh2, dx)
        comm, deliver = host(chip_job(l, [(late, 0, 1, 4)]))
        dy, couts = _mm_dy(dx1b, wo, comm)
        deliver(couts)
        comm, deliver = host(chip_job(l, [(late, 1, 2, 4)]))
        (grads[l]["w_out"],), couts = _mm_dw("mm_dw_out", [s["y"]], dx1b, 1024, 1024, comm)
        deliver(couts)
        comm, deliver = host(chip_job(l, [(late, 2, 4, 4)]), core_job(l, ["w_out"]))
        (dz, d_lng, d_lnb, d_ws, d_bb, d_cw, d_gg), couts = _mixer_bwd(
            s["z"], dy, gmlp_ln_g[l:l + 1], gmlp_ln_b[l:l + 1], w_spatial[l], bb[l], conv_full[l], group_norm_g[l:l + 1],
            comm)
        deliver(couts)
        add_up(l, ["w_out"])
        small[l] = dict(norm1_g=jnp.zeros((1, d), F32), gmlp_ln_g=d_lng, gmlp_ln_b=d_lnb, w_spatial=d_ws,
                        b_spatial=d_bb[:, :, 0], group_norm_g=d_gg, norm2_g=d_n2, conv_w=d_cw[0:3])
        if l > 0:
            comm, deliver = host(chip_job(l, [("w_up", 0, 3, 4)]))
            (grads[l]["w_in"],), couts = _mm_dw("mm_dw_in", [s["h"]], dz, 2048, 1024, comm)
            deliver(couts)
            comm, deliver = host(chip_job(l, [("w_up", 3, 4, 4), whole("w_out")]), core_job(l, ["w_in"]))
            dh, couts = _mm_dh(dz, wi, comm)
            deliver(couts)
            add_up(l, ["w_in"])
        else:
            parts = small_grad_parts()
            reduced = []
            comm, deliver = host(chip_job(l, [whole("w_out")]),
                                 (_all_reduce_piece(jnp.concatenate(parts, axis=0)), reduced.extend))
            (grads[l]["w_in_a"],), couts = _mm_dw("mm_dw_in_a", [s["h"]], dz, block["w_in_a"][0], 1024, comm,
                                                  m_rows=(0, block["w_in_a"][0]))
            deliver(couts)
            comm, deliver = host(core_job(l, ["w_in_a"]))
            (grads[l]["w_in_b"],), couts = _mm_dw("mm_dw_in_b", [s["h"]], dz, block["w_in_b"][0], 2560, comm,
                                                  m_rows=(block["w_in_a"][0], block["w_in_b"][0]))
            deliver(couts)
            add_up(l, ["w_in_a"])
            comm, deliver = host(chip_job(l, [whole("w_in_a")]), core_job(l, ["w_in_b"]))
            dh, couts = _mm_dh(dz, wi, comm)
            deliver(couts)
            add_up(l, ["w_in_b"])
        dx, dxb, small[l]["norm1_g"] = _rmsnorm_bwd(s["x"], norm1_g[l:l + 1], dh, dx1, with_bf16=l > 0)
    grad_x = dx.reshape(x.shape)

    sizes = [p.shape[0] for p in parts]
    comm, deliver = host(chip_job(0, [whole("w_in_b")]))
    last, couts = _all_reduce_small(_rows(small[0]["norm1_g"]), comm)
    deliver(couts)
    total = lax.dynamic_update_slice(reduced[0], last, (0, 0))
    offs = [0]
    for n in sizes:
        offs.append(offs[-1] + n)
    pieces = [total[offs[i]:offs[i + 1]] for i in range(len(parts))]
    loss = pieces[len(rep) + 2][0, 0]
    conv_g_full = pieces[len(rep) + 1].reshape(nl, 3, N_DEV, a // N_DEV)
    conv_g = lax.dynamic_index_in_dim(conv_g_full, me, axis=2, keepdims=False)
    names = rep + ["final_norm_g", "conv_w"]
    flat = lambda w: w.reshape(-1, w.shape[-1])
    small_w = [flat(rep_w[k]) for k in rep] + [flat(final_norm_g), flat(conv_w)]
    small_m = [flat(rep_m[k]) for k in rep] + [flat(m_final_norm_g), flat(m_conv_w)]
    small_v = [flat(rep_v[k]) for k in rep] + [flat(v_final_norm_g), flat(v_conv_w)]
    small_g = [pieces[i].reshape(small_w[i].shape) for i in range(len(rep) + 1)] + [flat(conv_g)]
    small_d, small_m, small_v = _adamw_small(small_g, small_w, small_m, small_v)
    shape_of = dict(rep_w, final_norm_g=final_norm_g, conv_w=conv_w)
    named = lambda arrays: {k: arr.reshape(shape_of[k].shape) for k, arr in zip(names, arrays)}
    res = {"grad": named(small_g), "delta": named(small_d), "m": named(small_m), "v": named(small_v)}

    for k in BIG:
        outs, _ = _adamw_big(f"adamw_{k}", lands[k], big[k], big_m[k], big_v[k])
        if k in ("w_gate", "w_up"):
            outs = [tr(o) for o in outs]
        res["grad"][k], res["delta"][k], res["m"][k], res["v"][k] = outs

    order = ["norm1_g", "w_in", "gmlp_ln_g", "gmlp_ln_b", "w_spatial", "b_spatial", "conv_w", "group_norm_g", "w_out",
             "norm2_g", "w_gate", "w_up", "w_down", "final_norm_g"]
    return (loss, grad_x, *[res["grad"][k] for k in order], *[res["delta"][k] for k in order],
            *[res["m"][k] for k in order], *[res["v"][k] for k in order])
```
